```python
import jax, jax.numpy as jnp
from jax import lax
import numpy as np

D_MODEL = 1024
BATCH = 16
SEQ = 4096
DEPTH = 4

N_MIXERS = 4
D_FF = 2816
RMS_EPS = 1e-6
LN_EPS = 1e-5
CONV_WIDTH = 31
FOX_HEADS = 16
FOX_HEAD_DIM = D_MODEL // FOX_HEADS
FOX_BLOCK = 128
HGRN_EXPAND = 128
HGRN_HEADS = D_MODEL // HGRN_EXPAND
HGRN_DK = HGRN_HEADS * HGRN_EXPAND
HGRN_DV = D_MODEL
HGRN_HEAD_DV = HGRN_DV // HGRN_HEADS
HGRN_CHUNK = 32
POOL_WINDOWS = (2, 4, 8, 16)
POOL_GROUP = D_MODEL // len(POOL_WINDOWS)

kernel_name = "hybrid_conv_fox_hgrn2_pool_macaron"


def rms_norm(x, g):
    x32 = x.astype(jnp.float32)
    y = x32 * lax.rsqrt(jnp.mean(x32 * x32, axis=-1, keepdims=True) + RMS_EPS)
    return (y * g.astype(jnp.float32)).astype(x.dtype)


def layer_norm(x, g, b):
    x32 = x.astype(jnp.float32)
    mu = jnp.mean(x32, axis=-1, keepdims=True)
    xc = x32 - mu
    y = xc * lax.rsqrt(jnp.mean(xc * xc, axis=-1, keepdims=True) + LN_EPS)
    return (y * g.astype(jnp.float32) + b.astype(jnp.float32)).astype(x.dtype)


def swiglu(h, w_gate, w_up, w_down):
    return (jax.nn.silu(h @ w_gate) * (h @ w_up)) @ w_down


def conv_module(h, w_in, b_in, dw, dw_b, ln_g, ln_b, w_out):
    a, b = jnp.split(h @ w_in + b_in, 2, axis=-1)
    u = a * jax.nn.sigmoid(b)
    u = lax.conv_general_dilated(
        u, dw[:, None, :], window_strides=(1,),
        padding=((CONV_WIDTH - 1, 0),),
        dimension_numbers=('NWC', 'WIO', 'NWC'),
        feature_group_count=D_MODEL) + dw_b
    u = jax.nn.silu(layer_norm(u, ln_g, ln_b))
    return u @ w_out


def fox_attention(h, w_in, b_f, w_out):
    B, S, _ = h.shape
    proj = h @ w_in
    def heads(t):
        return t.reshape(B, S, FOX_HEADS, FOX_HEAD_DIM).transpose(0, 2, 1, 3)
    q = heads(proj[..., :D_MODEL])
    k = heads(proj[..., D_MODEL:2 * D_MODEL])
    v = heads(proj[..., 2 * D_MODEL:3 * D_MODEL])
    log_f = jax.nn.log_sigmoid((proj[..., 3 * D_MODEL:] + b_f).astype(jnp.float32))
    c = jnp.cumsum(log_f, axis=1).transpose(0, 2, 1)
    scale = FOX_HEAD_DIM ** -0.5
    outs = []
    for blk in range(S // FOX_BLOCK):
        q0, q1 = blk * FOX_BLOCK, (blk + 1) * FOX_BLOCK
        logits = (jnp.einsum('bhqd,bhkd->bhqk', q[:, :, q0:q1], k[:, :, :q1]).astype(jnp.float32) * scale
                  + c[:, :, q0:q1, None] - c[:, :, None, :q1])
        causal = (q0 + jnp.arange(FOX_BLOCK))[:, None] >= jnp.arange(q1)[None, :]
        p = jax.nn.softmax(jnp.where(causal, logits, -jnp.inf), axis=-1)
        outs.append(jnp.einsum('bhqk,bhkd->bhqd', p.astype(v.dtype), v[:, :, :q1]))
    o = jnp.concatenate(outs, axis=2).transpose(0, 2, 1, 3).reshape(B, S, D_MODEL)
    return o @ w_out


def hgrn2_mixer(h, w_in, lb, norm_g, w_out):
    B, S, _ = h.shape
    proj = h @ w_in
    q = jax.nn.silu(proj[..., :HGRN_DK]).astype(jnp.float32)
    f_raw = proj[..., HGRN_DK:2 * HGRN_DK].astype(jnp.float32)
    i_in = proj[..., 2 * HGRN_DK:2 * HGRN_DK + HGRN_DV].astype(jnp.float32)
    g_out = proj[..., 2 * HGRN_DK + HGRN_DV:]
    lb = lb.astype(jnp.float32)
    log_f = jnp.logaddexp(jnp.log(lb), jnp.log1p(-lb) + jax.nn.log_sigmoid(f_raw))
    k = (1.0 - lb) * jax.nn.sigmoid(-f_raw)
    n_chunks = S // HGRN_CHUNK

    def to_chunks(t, d):
        return t.reshape(B, n_chunks, HGRN_CHUNK, HGRN_HEADS, d).transpose(1, 0, 3, 2, 4)

    qc = to_chunks(q, HGRN_EXPAND)
    kc = to_chunks(k, HGRN_EXPAND)
    gc = to_chunks(log_f, HGRN_EXPAND)
    vc = to_chunks(i_in, HGRN_HEAD_DV)
    causal = jnp.tril(jnp.ones((HGRN_CHUNK, HGRN_CHUNK), dtype=bool))[:, :, None]

    def step(state, inp):
        q_t, k_t, g_t, v_t = inp
        G = jnp.cumsum(g_t, axis=2)
        o_inter = jnp.einsum('bhtk,bhkv->bhtv', q_t * jnp.exp(G), state)
        diff = G[:, :, :, None, :] - G[:, :, None, :, :]
        decay = jnp.exp(jnp.where(causal, diff, -jnp.inf))
        A = jnp.einsum('bhtk,bhtsk,bhsk->bhts', q_t, decay, k_t)
        o = o_inter + jnp.einsum('bhts,bhsv->bhtv', A, v_t)
        G_last = G[:, :, -1:, :]
        state = (jnp.exp(G_last[:, :, 0, :])[..., None] * state
                 + jnp.einsum('bhsk,bhsv->bhkv', k_t * jnp.exp(G_last - G), v_t))
        return state, o

    state0 = jnp.zeros((B, HGRN_HEADS, HGRN_EXPAND, HGRN_HEAD_DV), jnp.float32)
    _, o = lax.scan(step, state0, (qc, kc, gc, vc))
    o = o.transpose(1, 0, 3, 2, 4).reshape(B, S, HGRN_HEADS, HGRN_HEAD_DV)
    o = o * lax.rsqrt(jnp.mean(o * o, axis=-1, keepdims=True) + RMS_EPS)
    o = o * norm_g.astype(jnp.float32).reshape(HGRN_HEADS, HGRN_HEAD_DV)
    o = o.reshape(B, S, HGRN_DV) * jax.nn.silu(g_out.astype(jnp.float32))
    return o.astype(h.dtype) @ w_out


def pool_mixer(h, w, scale):
    B, S, _ = h.shape
    h32 = h.astype(jnp.float32)
    pos = jnp.arange(1, S + 1, dtype=jnp.float32)[None, :, None]
    outs = []
    for gi, win in enumerate(POOL_WINDOWS):
        xg = h32[..., gi * POOL_GROUP:(gi + 1) * POOL_GROUP]
        cs = jnp.cumsum(xg, axis=1)
        lag = jnp.pad(cs, ((0, 0), (win, 0), (0, 0)))[:, :S]
        mean = (cs - lag) / jnp.minimum(pos, float(win))
        outs.append((mean - xg).astype(h.dtype) @ w[gi])
    return jnp.concatenate(outs, axis=-1) * scale


def _fwd_setup_inputs(seed: int = 0) -> dict:
    key = jax.random.key(seed)
    ks = jax.random.split(key, 32)
    n_a, n_b, n_c, n_d = (len(range(m, DEPTH, N_MIXERS)) for m in range(N_MIXERS))
    f32 = jnp.float32

    def w(k, shape, fan_in):
        return jax.random.normal(k, shape, f32) * fan_in ** -0.5

    def gain(k, shape):
        return 1.0 + 0.02 * jax.random.normal(k, shape, f32)

    def bias(k, shape):
        return 0.02 * jax.random.normal(k, shape, f32)

    return {
        "x": jax.random.normal(ks[0], (BATCH, SEQ, D_MODEL), f32),
        "ffn_norm": gain(ks[1], (DEPTH, 2, D_MODEL)),
        "ffn_w_gate": w(ks[2], (DEPTH, 2, D_MODEL, D_FF), D_MODEL),
        "ffn_w_up": w(ks[3], (DEPTH, 2, D_MODEL, D_FF), D_MODEL),
        "ffn_w_down": w(ks[4], (DEPTH, 2, D_FF, D_MODEL), D_FF),
        "mix_norm": gain(ks[5], (DEPTH, D_MODEL)),
        "final_norm": gain(ks[6], (D_MODEL,)),
        "conv_w_in": w(ks[7], (n_a, D_MODEL, 2 * D_MODEL), D_MODEL),
        "conv_b_in": bias(ks[8], (n_a, 2 * D_MODEL)),
        "conv_dw": w(ks[9], (n_a, CONV_WIDTH, D_MODEL), CONV_WIDTH),
        "conv_dw_b": bias(ks[10], (n_a, D_MODEL)),
        "conv_ln_g": gain(ks[11], (n_a, D_MODEL)),
        "conv_ln_b": bias(ks[12], (n_a, D_MODEL)),
        "conv_w_out": w(ks[13], (n_a, D_MODEL, D_MODEL), D_MODEL),
        "fox_w_in": w(ks[14], (n_b, D_MODEL, 3 * D_MODEL + FOX_HEADS), D_MODEL),
        "fox_b_f": 1.0 + 3.0 * jax.random.uniform(ks[15], (n_b, FOX_HEADS), f32),
        "fox_w_out": w(ks[16], (n_b, D_MODEL, D_MODEL), D_MODEL),
        "hgrn_w_in": w(ks[17], (n_c, D_MODEL, 2 * HGRN_DK + 2 * HGRN_DV), D_MODEL),
        "hgrn_lb_logits": 0.1 * jax.random.normal(ks[18], (DEPTH, HGRN_DK), f32),
        "hgrn_norm": gain(ks[19], (n_c, HGRN_DV)),
        "hgrn_w_out": w(ks[20], (n_c, HGRN_DV, D_MODEL), HGRN_DV),
        "pool_w": w(ks[21], (n_d, len(POOL_WINDOWS), POOL_GROUP, POOL_GROUP), POOL_GROUP),
        "pool_scale": 1.0 + 0.1 * jax.random.normal(ks[22], (n_d, D_MODEL), f32),
    }


def _fwd_reference(x, ffn_norm, ffn_w_gate, ffn_w_up, ffn_w_down, mix_norm, final_norm,
              conv_w_in, conv_b_in, conv_dw, conv_dw_b, conv_ln_g, conv_ln_b, conv_w_out,
              fox_w_in, fox_b_f, fox_w_out,
              hgrn_w_in, hgrn_lb_logits, hgrn_norm, hgrn_w_out,
              pool_w, pool_scale):
    p = jax.nn.softmax(hgrn_lb_logits.astype(jnp.float32), axis=0)
    lower_bounds = jnp.cumsum(p, axis=0) - p[0]
    for i in range(DEPTH):
        m, j = i % N_MIXERS, i // N_MIXERS
        x = x + 0.5 * swiglu(rms_norm(x, ffn_norm[i, 0]), ffn_w_gate[i, 0], ffn_w_up[i, 0], ffn_w_down[i, 0])
        h = rms_norm(x, mix_norm[i])
        if m == 0:
            y = conv_module(h, conv_w_in[j], conv_b_in[j], conv_dw[j], conv_dw_b[j],
                            conv_ln_g[j], conv_ln_b[j], conv_w_out[j])
        elif m == 1:
            y = fox_attention(h, fox_w_in[j], fox_b_f[j], fox_w_out[j])
        elif m == 2:
            y = hgrn2_mixer(h, hgrn_w_in[j], lower_bounds[i], hgrn_norm[j], hgrn_w_out[j])
        else:
            y = pool_mixer(h, pool_w[j], pool_scale[j])
        x = x + y
        x = x + 0.5 * swiglu(rms_norm(x, ffn_norm[i, 1]), ffn_w_gate[i, 1], ffn_w_up[i, 1], ffn_w_down[i, 1])
    return rms_norm(x, final_norm)


import jax as _jax
import jax.numpy as _jnp

TWIN_FORMAT = 'train_step'
FWD_PARAMS = ['x', 'ffn_norm', 'ffn_w_gate', 'ffn_w_up', 'ffn_w_down', 'mix_norm', 'final_norm', 'conv_w_in', 'conv_b_in', 'conv_dw', 'conv_dw_b', 'conv_ln_g', 'conv_ln_b', 'conv_w_out', 'fox_w_in', 'fox_b_f', 'fox_w_out', 'hgrn_w_in', 'hgrn_lb_logits', 'hgrn_norm', 'hgrn_w_out', 'pool_w', 'pool_scale']
TWIN_WEIGHTS = ['ffn_norm', 'ffn_w_gate', 'ffn_w_up', 'ffn_w_down', 'mix_norm', 'final_norm', 'conv_w_in', 'conv_b_in', 'conv_dw', 'conv_dw_b', 'conv_ln_g', 'conv_ln_b', 'conv_w_out', 'fox_w_in', 'fox_b_f', 'fox_w_out', 'hgrn_w_in', 'hgrn_lb_logits', 'hgrn_norm', 'hgrn_w_out', 'pool_w', 'pool_scale']
TWIN_DIFF_INPUT = 'x'
TWIN_INPUTS = ['x', 'ffn_norm', 'ffn_w_gate', 'ffn_w_up', 'ffn_w_down', 'mix_norm', 'final_norm', 'conv_w_in', 'conv_b_in', 'conv_dw', 'conv_dw_b', 'conv_ln_g', 'conv_ln_b', 'conv_w_out', 'fox_w_in', 'fox_b_f', 'fox_w_out', 'hgrn_w_in', 'hgrn_lb_logits', 'hgrn_norm', 'hgrn_w_out', 'pool_w', 'pool_scale', 'loss_target', 'm_ffn_norm', 'm_ffn_w_gate', 'm_ffn_w_up', 'm_ffn_w_down', 'm_mix_norm', 'm_final_norm', 'm_conv_w_in', 'm_conv_b_in', 'm_conv_dw', 'm_conv_dw_b', 'm_conv_ln_g', 'm_conv_ln_b', 'm_conv_w_out', 'm_fox_w_in', 'm_fox_b_f', 'm_fox_w_out', 'm_hgrn_w_in', 'm_hgrn_lb_logits', 'm_hgrn_norm', 'm_hgrn_w_out', 'm_pool_w', 'm_pool_scale', 'v_ffn_norm', 'v_ffn_w_gate', 'v_ffn_w_up', 'v_ffn_w_down', 'v_mix_norm', 'v_final_norm', 'v_conv_w_in', 'v_conv_b_in', 'v_conv_dw', 'v_conv_dw_b', 'v_conv_ln_g', 'v_conv_ln_b', 'v_conv_w_out', 'v_fox_w_in', 'v_fox_b_f', 'v_fox_w_out', 'v_hgrn_w_in', 'v_hgrn_lb_logits', 'v_hgrn_norm', 'v_hgrn_w_out', 'v_pool_w', 'v_pool_scale']
TWIN_OUTPUTS = ['loss', 'grad_x', 'grad_ffn_norm', 'grad_ffn_w_gate', 'grad_ffn_w_up', 'grad_ffn_w_down', 'grad_mix_norm', 'grad_final_norm', 'grad_conv_w_in', 'grad_conv_b_in', 'grad_conv_dw', 'grad_conv_dw_b', 'grad_conv_ln_g', 'grad_conv_ln_b', 'grad_conv_w_out', 'grad_fox_w_in', 'grad_fox_b_f', 'grad_fox_w_out', 'grad_hgrn_w_in', 'grad_hgrn_lb_logits', 'grad_hgrn_norm', 'grad_hgrn_w_out', 'grad_pool_w', 'grad_pool_scale', 'delta_ffn_norm', 'delta_ffn_w_gate', 'delta_ffn_w_up', 'delta_ffn_w_down', 'delta_mix_norm', 'delta_final_norm', 'delta_conv_w_in', 'delta_conv_b_in', 'delta_conv_dw', 'delta_conv_dw_b', 'delta_conv_ln_g', 'delta_conv_ln_b', 'delta_conv_w_out', 'delta_fox_w_in', 'delta_fox_b_f', 'delta_fox_w_out', 'delta_hgrn_w_in', 'delta_hgrn_lb_logits', 'delta_hgrn_norm', 'delta_hgrn_w_out', 'delta_pool_w', 'delta_pool_scale', 'new_m_ffn_norm', 'new_m_ffn_w_gate', 'new_m_ffn_w_up', 'new_m_ffn_w_down', 'new_m_mix_norm', 'new_m_final_norm', 'new_m_conv_w_in', 'new_m_conv_b_in', 'new_m_conv_dw', 'new_m_conv_dw_b', 'new_m_conv_ln_g', 'new_m_conv_ln_b', 'new_m_conv_w_out', 'new_m_fox_w_in', 'new_m_fox_b_f', 'new_m_fox_w_out', 'new_m_hgrn_w_in', 'new_m_hgrn_lb_logits', 'new_m_hgrn_norm', 'new_m_hgrn_w_out', 'new_m_pool_w', 'new_m_pool_scale', 'new_v_ffn_norm', 'new_v_ffn_w_gate', 'new_v_ffn_w_up', 'new_v_ffn_w_down', 'new_v_mix_norm', 'new_v_final_norm', 'new_v_conv_w_in', 'new_v_conv_b_in', 'new_v_conv_dw', 'new_v_conv_dw_b', 'new_v_conv_ln_g', 'new_v_conv_ln_b', 'new_v_conv_w_out', 'new_v_fox_w_in', 'new_v_fox_b_f', 'new_v_fox_w_out', 'new_v_hgrn_w_in', 'new_v_hgrn_lb_logits', 'new_v_hgrn_norm', 'new_v_hgrn_w_out', 'new_v_pool_w', 'new_v_pool_scale']
TWIN_LEAF_KINDS = {'loss': 'loss', 'grad_x': 'grad_x', 'grad_ffn_norm': 'grad_w', 'grad_ffn_w_gate': 'grad_w', 'grad_ffn_w_up': 'grad_w', 'grad_ffn_w_down': 'grad_w', 'grad_mix_norm': 'grad_w', 'grad_final_norm': 'grad_w', 'grad_conv_w_in': 'grad_w', 'grad_conv_b_in': 'grad_w', 'grad_conv_dw': 'grad_w', 'grad_conv_dw_b': 'grad_w', 'grad_conv_ln_g': 'grad_w', 'grad_conv_ln_b': 'grad_w', 'grad_conv_w_out': 'grad_w', 'grad_fox_w_in': 'grad_w', 'grad_fox_b_f': 'grad_w', 'grad_fox_w_out': 'grad_w', 'grad_hgrn_w_in': 'grad_w', 'grad_hgrn_lb_logits': 'grad_w', 'grad_hgrn_norm': 'grad_w', 'grad_hgrn_w_out': 'grad_w', 'grad_pool_w': 'grad_w', 'grad_pool_scale': 'grad_w', 'delta_ffn_norm': 'delta_w', 'delta_ffn_w_gate': 'delta_w', 'delta_ffn_w_up': 'delta_w', 'delta_ffn_w_down': 'delta_w', 'delta_mix_norm': 'delta_w', 'delta_final_norm': 'delta_w', 'delta_conv_w_in': 'delta_w', 'delta_conv_b_in': 'delta_w', 'delta_conv_dw': 'delta_w', 'delta_conv_dw_b': 'delta_w', 'delta_conv_ln_g': 'delta_w', 'delta_conv_ln_b': 'delta_w', 'delta_conv_w_out': 'delta_w', 'delta_fox_w_in': 'delta_w', 'delta_fox_b_f': 'delta_w', 'delta_fox_w_out': 'delta_w', 'delta_hgrn_w_in': 'delta_w', 'delta_hgrn_lb_logits': 'delta_w', 'delta_hgrn_norm': 'delta_w', 'delta_hgrn_w_out': 'delta_w', 'delta_pool_w': 'delta_w', 'delta_pool_scale': 'delta_w', 'new_m_ffn_norm': 'new_m', 'new_m_ffn_w_gate': 'new_m', 'new_m_ffn_w_up': 'new_m', 'new_m_ffn_w_down': 'new_m', 'new_m_mix_norm': 'new_m', 'new_m_final_norm': 'new_m', 'new_m_conv_w_in': 'new_m', 'new_m_conv_b_in': 'new_m', 'new_m_conv_dw': 'new_m', 'new_m_conv_dw_b': 'new_m', 'new_m_conv_ln_g': 'new_m', 'new_m_conv_ln_b': 'new_m', 'new_m_conv_w_out': 'new_m', 'new_m_fox_w_in': 'new_m', 'new_m_fox_b_f': 'new_m', 'new_m_fox_w_out': 'new_m', 'new_m_hgrn_w_in': 'new_m', 'new_m_hgrn_lb_logits': 'new_m', 'new_m_hgrn_norm': 'new_m', 'new_m_hgrn_w_out': 'new_m', 'new_m_pool_w': 'new_m', 'new_m_pool_scale': 'new_m', 'new_v_ffn_norm': 'new_v', 'new_v_ffn_w_gate': 'new_v', 'new_v_ffn_w_up': 'new_v', 'new_v_ffn_w_down': 'new_v', 'new_v_mix_norm': 'new_v', 'new_v_final_norm': 'new_v', 'new_v_conv_w_in': 'new_v', 'new_v_conv_b_in': 'new_v', 'new_v_conv_dw': 'new_v', 'new_v_conv_dw_b': 'new_v', 'new_v_conv_ln_g': 'new_v', 'new_v_conv_ln_b': 'new_v', 'new_v_conv_w_out': 'new_v', 'new_v_fox_w_in': 'new_v', 'new_v_fox_b_f': 'new_v', 'new_v_fox_w_out': 'new_v', 'new_v_hgrn_w_in': 'new_v', 'new_v_hgrn_lb_logits': 'new_v', 'new_v_hgrn_norm': 'new_v', 'new_v_hgrn_w_out': 'new_v', 'new_v_pool_w': 'new_v', 'new_v_pool_scale': 'new_v'}


def _forward(args):
    return _fwd_reference(*[args[k] for k in FWD_PARAMS])


def _output_shape():
    out = _jax.eval_shape(lambda: _forward(_fwd_setup_inputs(0)))
    return out.shape, out.dtype

N_MICROBATCH = 1
ADAM_LR = 0.001
ADAM_B1 = 0.9
ADAM_B2 = 0.999
ADAM_EPS = 1e-08
ADAM_WD = 0.01
ADAM_STEP = 10
PER_EXAMPLE_BATCH_AXIS = {'x': 0, 'loss_target': 0}
SHARED_INPUTS = []
_WEIGHT_DTYPES = {'ffn_norm': _jnp.float32, 'ffn_w_gate': _jnp.float32, 'ffn_w_up': _jnp.float32, 'ffn_w_down': _jnp.float32, 'mix_norm': _jnp.float32, 'final_norm': _jnp.float32, 'conv_w_in': _jnp.float32, 'conv_b_in': _jnp.float32, 'conv_dw': _jnp.float32, 'conv_dw_b': _jnp.float32, 'conv_ln_g': _jnp.float32, 'conv_ln_b': _jnp.float32, 'conv_w_out': _jnp.float32, 'fox_w_in': _jnp.float32, 'fox_b_f': _jnp.float32, 'fox_w_out': _jnp.float32, 'hgrn_w_in': _jnp.float32, 'hgrn_lb_logits': _jnp.float32, 'hgrn_norm': _jnp.float32, 'hgrn_w_out': _jnp.float32, 'pool_w': _jnp.float32, 'pool_scale': _jnp.float32}
MOMENT_SCALE = {'ffn_norm': 9.743060e-02, 'ffn_w_gate': 4.085129e-02, 'ffn_w_up': 3.961724e-02, 'ffn_w_down': 6.565627e-02, 'mix_norm': 1.522322e-01, 'final_norm': 6.431377e+01, 'conv_w_in': 1.289173e-01, 'conv_b_in': 1.801741e-01, 'conv_dw': 1.695003e-01, 'conv_dw_b': 4.578476e-01, 'conv_ln_g': 2.229762e-01, 'conv_ln_b': 2.167099e-01, 'conv_w_out': 1.716271e-01, 'fox_w_in': 7.315736e-02, 'fox_b_f': 4.498524e-01, 'fox_w_out': 9.318229e-02, 'hgrn_w_in': 7.651823e-02, 'hgrn_lb_logits': 4.834058e-03, 'hgrn_norm': 1.223750e-01, 'hgrn_w_out': 1.054983e-01, 'pool_w': 1.246652e-01, 'pool_scale': 3.483650e-01}


def _to_microbatches(a, axis):
    t = _jnp.moveaxis(a, axis, 0)
    t = t.reshape((N_MICROBATCH, t.shape[0] // N_MICROBATCH) + t.shape[1:])
    return _jnp.moveaxis(t, 1, axis + 1)


def setup_inputs(seed: int = 0) -> dict:
    inp = _fwd_setup_inputs(seed)
    key = _jax.random.fold_in(_jax.random.key(seed), 7919)
    shape, _ = _output_shape()
    out = dict(inp)
    out["loss_target"] = _jax.random.normal(_jax.random.fold_in(key, 0), shape, _jnp.float32)
    for i, name in enumerate(TWIN_WEIGHTS):
        w = inp[name].astype(_jnp.float32)
        if MOMENT_SCALE is None:
            s = _jnp.sqrt(_jnp.mean(_jnp.square(w)) + 1e-30)
        else:
            s = MOMENT_SCALE[name]
        km, kv = _jax.random.split(_jax.random.fold_in(key, i + 1))
        out[name] = w
        out["m_" + name] = s * _jax.random.normal(km, w.shape, _jnp.float32)
        out["v_" + name] = (s * s) * _jax.random.uniform(kv, w.shape, _jnp.float32, 0.5, 1.5)
    if N_MICROBATCH > 1:
        for name, axis in PER_EXAMPLE_BATCH_AXIS.items():
            out[name] = _to_microbatches(out[name], axis)
    return {'x': out['x'], 'ffn_norm': out['ffn_norm'], 'ffn_w_gate': out['ffn_w_gate'], 'ffn_w_up': out['ffn_w_up'], 'ffn_w_down': out['ffn_w_down'], 'mix_norm': out['mix_norm'], 'final_norm': out['final_norm'], 'conv_w_in': out['conv_w_in'], 'conv_b_in': out['conv_b_in'], 'conv_dw': out['conv_dw'], 'conv_dw_b': out['conv_dw_b'], 'conv_ln_g': out['conv_ln_g'], 'conv_ln_b': out['conv_ln_b'], 'conv_w_out': out['conv_w_out'], 'fox_w_in': out['fox_w_in'], 'fox_b_f': out['fox_b_f'], 'fox_w_out': out['fox_w_out'], 'hgrn_w_in': out['hgrn_w_in'], 'hgrn_lb_logits': out['hgrn_lb_logits'], 'hgrn_norm': out['hgrn_norm'], 'hgrn_w_out': out['hgrn_w_out'], 'pool_w': out['pool_w'], 'pool_scale': out['pool_scale'], 'loss_target': out['loss_target'], 'm_ffn_norm': out['m_ffn_norm'], 'm_ffn_w_gate': out['m_ffn_w_gate'], 'm_ffn_w_up': out['m_ffn_w_up'], 'm_ffn_w_down': out['m_ffn_w_down'], 'm_mix_norm': out['m_mix_norm'], 'm_final_norm': out['m_final_norm'], 'm_conv_w_in': out['m_conv_w_in'], 'm_conv_b_in': out['m_conv_b_in'], 'm_conv_dw': out['m_conv_dw'], 'm_conv_dw_b': out['m_conv_dw_b'], 'm_conv_ln_g': out['m_conv_ln_g'], 'm_conv_ln_b': out['m_conv_ln_b'], 'm_conv_w_out': out['m_conv_w_out'], 'm_fox_w_in': out['m_fox_w_in'], 'm_fox_b_f': out['m_fox_b_f'], 'm_fox_w_out': out['m_fox_w_out'], 'm_hgrn_w_in': out['m_hgrn_w_in'], 'm_hgrn_lb_logits': out['m_hgrn_lb_logits'], 'm_hgrn_norm': out['m_hgrn_norm'], 'm_hgrn_w_out': out['m_hgrn_w_out'], 'm_pool_w': out['m_pool_w'], 'm_pool_scale': out['m_pool_scale'], 'v_ffn_norm': out['v_ffn_norm'], 'v_ffn_w_gate': out['v_ffn_w_gate'], 'v_ffn_w_up': out['v_ffn_w_up'], 'v_ffn_w_down': out['v_ffn_w_down'], 'v_mix_norm': out['v_mix_norm'], 'v_final_norm': out['v_final_norm'], 'v_conv_w_in': out['v_conv_w_in'], 'v_conv_b_in': out['v_conv_b_in'], 'v_conv_dw': out['v_conv_dw'], 'v_conv_dw_b': out['v_conv_dw_b'], 'v_conv_ln_g': out['v_conv_ln_g'], 'v_conv_ln_b': out['v_conv_ln_b'], 'v_conv_w_out': out['v_conv_w_out'], 'v_fox_w_in': out['v_fox_w_in'], 'v_fox_b_f': out['v_fox_b_f'], 'v_fox_w_out': out['v_fox_w_out'], 'v_hgrn_w_in': out['v_hgrn_w_in'], 'v_hgrn_lb_logits': out['v_hgrn_lb_logits'], 'v_hgrn_norm': out['v_hgrn_norm'], 'v_hgrn_w_out': out['v_hgrn_w_out'], 'v_pool_w': out['v_pool_w'], 'v_pool_scale': out['v_pool_scale']}


def _loss(weights, diff, rest, loss_target):
    with _jax.named_scope("forward"):
        args = {**rest, TWIN_DIFF_INPUT: diff, **{k: w.astype(_WEIGHT_DTYPES[k]) for k, w in weights.items()}}
        y = _forward(args)
    with _jax.named_scope("loss_head"):
        err = _jnp.square(y.astype(_jnp.float32) - loss_target)
        return 0.5 * _jnp.sum(_jnp.mean(err, axis=-1)) if err.ndim else 0.5 * err


def _adamw(w, g, m, v):
    m = ADAM_B1 * m + (1.0 - ADAM_B1) * g
    v = ADAM_B2 * v + (1.0 - ADAM_B2) * _jnp.square(g)
    m_hat = m / (1.0 - ADAM_B1 ** ADAM_STEP)
    v_hat = v / (1.0 - ADAM_B2 ** ADAM_STEP)
    delta = -ADAM_LR * (m_hat / (_jnp.sqrt(v_hat) + ADAM_EPS) + ADAM_WD * w)
    return delta, m, v


def reference(x, ffn_norm, ffn_w_gate, ffn_w_up, ffn_w_down, mix_norm, final_norm, conv_w_in, conv_b_in, conv_dw, conv_dw_b, conv_ln_g, conv_ln_b, conv_w_out, fox_w_in, fox_b_f, fox_w_out, hgrn_w_in, hgrn_lb_logits, hgrn_norm, hgrn_w_out, pool_w, pool_scale, loss_target, m_ffn_norm, m_ffn_w_gate, m_ffn_w_up, m_ffn_w_down, m_mix_norm, m_final_norm, m_conv_w_in, m_conv_b_in, m_conv_dw, m_conv_dw_b, m_conv_ln_g, m_conv_ln_b, m_conv_w_out, m_fox_w_in, m_fox_b_f, m_fox_w_out, m_hgrn_w_in, m_hgrn_lb_logits, m_hgrn_norm, m_hgrn_w_out, m_pool_w, m_pool_scale, v_ffn_norm, v_ffn_w_gate, v_ffn_w_up, v_ffn_w_down, v_mix_norm, v_final_norm, v_conv_w_in, v_conv_b_in, v_conv_dw, v_conv_dw_b, v_conv_ln_g, v_conv_ln_b, v_conv_w_out, v_fox_w_in, v_fox_b_f, v_fox_w_out, v_hgrn_w_in, v_hgrn_lb_logits, v_hgrn_norm, v_hgrn_w_out, v_pool_w, v_pool_scale):
    given = dict(x=x, ffn_norm=ffn_norm, ffn_w_gate=ffn_w_gate, ffn_w_up=ffn_w_up, ffn_w_down=ffn_w_down, mix_norm=mix_norm, final_norm=final_norm, conv_w_in=conv_w_in, conv_b_in=conv_b_in, conv_dw=conv_dw, conv_dw_b=conv_dw_b, conv_ln_g=conv_ln_g, conv_ln_b=conv_ln_b, conv_w_out=conv_w_out, fox_w_in=fox_w_in, fox_b_f=fox_b_f, fox_w_out=fox_w_out, hgrn_w_in=hgrn_w_in, hgrn_lb_logits=hgrn_lb_logits, hgrn_norm=hgrn_norm, hgrn_w_out=hgrn_w_out, pool_w=pool_w, pool_scale=pool_scale, loss_target=loss_target, m_ffn_norm=m_ffn_norm, m_ffn_w_gate=m_ffn_w_gate, m_ffn_w_up=m_ffn_w_up, m_ffn_w_down=m_ffn_w_down, m_mix_norm=m_mix_norm, m_final_norm=m_final_norm, m_conv_w_in=m_conv_w_in, m_conv_b_in=m_conv_b_in, m_conv_dw=m_conv_dw, m_conv_dw_b=m_conv_dw_b, m_conv_ln_g=m_conv_ln_g, m_conv_ln_b=m_conv_ln_b, m_conv_w_out=m_conv_w_out, m_fox_w_in=m_fox_w_in, m_fox_b_f=m_fox_b_f, m_fox_w_out=m_fox_w_out, m_hgrn_w_in=m_hgrn_w_in, m_hgrn_lb_logits=m_hgrn_lb_logits, m_hgrn_norm=m_hgrn_norm, m_hgrn_w_out=m_hgrn_w_out, m_pool_w=m_pool_w, m_pool_scale=m_pool_scale, v_ffn_norm=v_ffn_norm, v_ffn_w_gate=v_ffn_w_gate, v_ffn_w_up=v_ffn_w_up, v_ffn_w_down=v_ffn_w_down, v_mix_norm=v_mix_norm, v_final_norm=v_final_norm, v_conv_w_in=v_conv_w_in, v_conv_b_in=v_conv_b_in, v_conv_dw=v_conv_dw, v_conv_dw_b=v_conv_dw_b, v_conv_ln_g=v_conv_ln_g, v_conv_ln_b=v_conv_ln_b, v_conv_w_out=v_conv_w_out, v_fox_w_in=v_fox_w_in, v_fox_b_f=v_fox_b_f, v_fox_w_out=v_fox_w_out, v_hgrn_w_in=v_hgrn_w_in, v_hgrn_lb_logits=v_hgrn_lb_logits, v_hgrn_norm=v_hgrn_norm, v_hgrn_w_out=v_hgrn_w_out, v_pool_w=v_pool_w, v_pool_scale=v_pool_scale)
    weights = {n: given[n] for n in TWIN_WEIGHTS}
    shared = {n: given[n] for n in SHARED_INPUTS}
    per_example = {n: given[n] for n in ['x']}
    grad_fn = _jax.value_and_grad(_loss, argnums=(0, 1))

    def one_microbatch(ex, loss_target):
        ex = dict(ex)
        diff = ex.pop(TWIN_DIFF_INPUT)
        return grad_fn(weights, diff, {**shared, **ex}, loss_target)

    if N_MICROBATCH == 1:
        loss, (grad_w, grad_x) = one_microbatch(per_example, given["loss_target"])
    else:
        def body(carry, xs):
            loss_sum, grad_sum = carry
            l_k, (gw_k, gx_k) = one_microbatch(xs[0], xs[1])
            with _jax.named_scope("update"):
                return (loss_sum + l_k, _jax.tree.map(_jnp.add, grad_sum, gw_k)), gx_k

        init = (_jnp.zeros((), _jnp.float32), _jax.tree.map(_jnp.zeros_like, weights))
        (loss, grad_w), grad_x = _jax.lax.scan(body, init, (per_example, given["loss_target"]))
    with _jax.named_scope("update"):
        delta_w, new_m, new_v = {}, {}, {}
        for n in TWIN_WEIGHTS:
            delta_w[n], new_m[n], new_v[n] = _adamw(weights[n], grad_w[n], given["m_" + n], given["v_" + n])
    return (loss, grad_x, *[grad_w[n] for n in TWIN_WEIGHTS], *[delta_w[n] for n in TWIN_WEIGHTS],
            *[new_m[n] for n in TWIN_WEIGHTS], *[new_v[n] for n in TWIN_WEIGHTS])
```

```python
import functools

import jax
import jax.numpy as jnp
from jax import lax
from jax.experimental import pallas as pl
from jax.experimental.pallas import tpu as pltpu

D = 1024
F = 2816
NSH = 4
FS = F // NSH
DEPTH = 4
RMS_EPS = 1e-6
LN_EPS = 1e-5
CONV_W = 31
HALO = 32
FOX_H = 16
FOX_DH = 64
HG_H = 8
HG_DK = 128
HG_C = 32
POOL_WIN = (2, 4, 8, 16)
POOL_G = 256
MXU = jnp.bfloat16
F32 = jnp.float32
VMEM_LIMIT = 52 * 1024 * 1024

ADAM_LR = 0.001
ADAM_B1 = 0.9
ADAM_B2 = 0.999
ADAM_EPS = 1e-08
ADAM_WD = 0.01
ADAM_STEP = 10


def _call(body, *, name, grid, in_specs, out_specs, out_shape, scratch=()):
    return pl.pallas_call(
        body, name=name, grid=grid, in_specs=in_specs, out_specs=out_specs, out_shape=out_shape,
        scratch_shapes=list(scratch),
        compiler_params=pltpu.CompilerParams(dimension_semantics=("arbitrary",) * len(grid),
                                             vmem_limit_bytes=VMEM_LIMIT))


def _dot(a, b):
    return jnp.dot(a, b, preferred_element_type=F32)


def _dot_nt(a, b):
    return lax.dot_general(a, b, (((1,), (1,)), ((), ())), preferred_element_type=F32)


def _dot_tn(a, b):
    return lax.dot_general(a, b, (((0,), (0,)), ((), ())), preferred_element_type=F32)


def _split(x):
    hi = x.astype(MXU)
    return hi, (x - hi.astype(F32)).astype(MXU)


def _sigmoid(x):
    return 1.0 / (1.0 + jnp.exp(-x))


def _rms(x, g):
    r = lax.rsqrt(jnp.mean(x * x, axis=-1, keepdims=True) + RMS_EPS)
    return x * r * g


def _rms_bwd(dh, x, g):
    r = lax.rsqrt(jnp.mean(x * x, axis=-1, keepdims=True) + RMS_EPS)
    xh = x * r
    dhg = dh * g
    dx = r * (dhg - xh * jnp.mean(dhg * xh, axis=-1, keepdims=True))
    return dx, jnp.sum(dh * xh, axis=0, keepdims=True)


def _sds(shape, dtype):
    return jax.ShapeDtypeStruct(shape, dtype)


def _wspec(w, e):
    if w.ndim == 3:
        return pl.BlockSpec((None,) + w.shape[1:], lambda i, j: (j, 0, 0))
    return pl.BlockSpec((None, None) + w.shape[2:], lambda i, j: (j, e, 0, 0))


def ffn_fwd(x, g, wg, wu, wd, e=0, *, tm=512):
    T = x.shape[0]

    def body(x_ref, g_ref, wg_ref, wu_ref, wd_ref, xo_ref, h_ref, a_ref, b_ref, acc_ref):
        j = pl.program_id(1)

        @pl.when(j == 0)
        def _():
            h_ref[...] = _rms(x_ref[...], g_ref[...]).astype(MXU)
            acc_ref[...] = jnp.zeros_like(acc_ref)

        h = h_ref[...]
        a = _dot(h, wg_ref[...])
        b = _dot(h, wu_ref[...])
        a_ref[...] = a.astype(MXU)
        b_ref[...] = b.astype(MXU)
        z = (a * _sigmoid(a) * b).astype(MXU)
        acc_ref[...] += _dot(z, wd_ref[...])

        @pl.when(j == NSH - 1)
        def _():
            xo_ref[...] = x_ref[...] + 0.5 * acc_ref[...]

    return _call(
        body, name="ffn_fwd", grid=(T // tm, NSH),
        in_specs=[pl.BlockSpec((tm, D), lambda i, j: (i, 0)),
                  pl.BlockSpec((1, D), lambda i, j: (0, 0)),
                  _wspec(wg, e), _wspec(wu, e), _wspec(wd, e)],
        out_specs=[pl.BlockSpec((tm, D), lambda i, j: (i, 0)),
                   pl.BlockSpec((tm, D), lambda i, j: (i, 0)),
                   pl.BlockSpec((None, tm, FS), lambda i, j: (j, i, 0)),
                   pl.BlockSpec((None, tm, FS), lambda i, j: (j, i, 0))],
        out_shape=[_sds((T, D), F32), _sds((T, D), MXU), _sds((NSH, T, FS), MXU), _sds((NSH, T, FS), MXU)],
        scratch=[pltpu.VMEM((tm, D), F32)],
    )(x, g, wg, wu, wd)


def ffn_bwd_dx(x, g, dy, a, b, wg, wu, wd, e=0, *, tm=512):
    T = x.shape[0]

    def body(x_ref, g_ref, dy_ref, a_ref, b_ref, wg_ref, wu_ref, wd_ref,
             dx_ref, da_ref, db_ref, z_ref, dyh_ref, dg_ref, acc_ref):
        i = pl.program_id(0)
        j = pl.program_id(1)

        @pl.when(j == 0)
        def _():
            dyh_ref[...] = (0.5 * dy_ref[...]).astype(MXU)
            acc_ref[...] = jnp.zeros_like(acc_ref)

        @pl.when((i == 0) & (j == 0))
        def _():
            dg_ref[...] = jnp.zeros_like(dg_ref)

        dz = _dot_nt(dyh_ref[...], wd_ref[...])
        av = a_ref[...].astype(F32)
        bv = b_ref[...].astype(F32)
        s = _sigmoid(av)
        sa = av * s
        da = (dz * bv * (s * (1.0 + av * (1.0 - s)))).astype(MXU)
        db = (dz * sa).astype(MXU)
        da_ref[...] = da
        db_ref[...] = db
        z_ref[...] = (sa * bv).astype(MXU)
        acc_ref[...] += _dot_nt(da, wg_ref[...]) + _dot_nt(db, wu_ref[...])

        @pl.when(j == NSH - 1)
        def _():
            dxn, dg = _rms_bwd(acc_ref[...], x_ref[...], g_ref[...])
            dx_ref[...] = dy_ref[...] + dxn
            dg_ref[...] += dg

    return _call(
        body, name="ffn_bwd_dx", grid=(T // tm, NSH),
        in_specs=[pl.BlockSpec((tm, D), lambda i, j: (i, 0)),
                  pl.BlockSpec((1, D), lambda i, j: (0, 0)),
                  pl.BlockSpec((tm, D), lambda i, j: (i, 0)),
                  pl.BlockSpec((None, tm, FS), lambda i, j: (j, i, 0)),
                  pl.BlockSpec((None, tm, FS), lambda i, j: (j, i, 0)),
                  _wspec(wg, e), _wspec(wu, e), _wspec(wd, e)],
        out_specs=[pl.BlockSpec((tm, D), lambda i, j: (i, 0)),
                   pl.BlockSpec((None, tm, FS), lambda i, j: (j, i, 0)),
                   pl.BlockSpec((None, tm, FS), lambda i, j: (j, i, 0)),
                   pl.BlockSpec((None, tm, FS), lambda i, j: (j, i, 0)),
                   pl.BlockSpec((tm, D), lambda i, j: (i, 0)),
                   pl.BlockSpec((1, D), lambda i, j: (0, 0))],
        out_shape=[_sds((T, D), F32), _sds((NSH, T, FS), MXU), _sds((NSH, T, FS), MXU),
                   _sds((NSH, T, FS), MXU), _sds((T, D), MXU), _sds((1, D), F32)],
        scratch=[pltpu.VMEM((tm, D), F32)],
    )(x, g, dy, a, b, wg, wu, wd)


def mm_tn(a, b, *, name, G, M, N, a_step=0, b_step=0, tk=512, stack=None):
    T = a.shape[-2]
    if stack is not None:
        return _mm_tn_stack(a, b, name=name, G=G, M=M, N=N, tk=tk, stack=stack)

    def spec(arr, width, step):
        if arr.ndim == 3:
            return pl.BlockSpec((None, tk, width), lambda g, k: (g, k, 0))
        return pl.BlockSpec((tk, width), lambda g, k: (k, g * step))

    def body(a_ref, b_ref, o_ref):
        @pl.when(pl.program_id(1) == 0)
        def _():
            o_ref[...] = jnp.zeros_like(o_ref)

        o_ref[...] += _dot_tn(a_ref[...], b_ref[...])

    return _call(
        body, name=name, grid=(G, T // tk),
        in_specs=[spec(a, M, a_step), spec(b, N, b_step)],
        out_specs=pl.BlockSpec((None, M, N), lambda g, k: (g, 0, 0)),
        out_shape=_sds((G, M, N), F32),
    )(a, b)


def norm_mm(x, g, wb, bias, *, name, out_dtype, tm=512):
    T = x.shape[0]
    G, _, ns = wb.shape
    has_bias = bias is not None

    def body(*refs):
        if has_bias:
            x_ref, g_ref, w_ref, bias_ref, p_ref, h_ref = refs
        else:
            x_ref, g_ref, w_ref, p_ref, h_ref = refs

        @pl.when(pl.program_id(1) == 0)
        def _():
            h_ref[...] = _rms(x_ref[...], g_ref[...]).astype(MXU)

        p = _dot(h_ref[...], w_ref[...])
        if has_bias:
            p = p + bias_ref[...]
        p_ref[...] = p.astype(out_dtype)

    in_specs = [pl.BlockSpec((tm, D), lambda i, j: (i, 0)),
                pl.BlockSpec((1, D), lambda i, j: (0, 0)),
                pl.BlockSpec((None, D, ns), lambda i, j: (j, 0, 0))]
    args = [x, g, wb]
    if has_bias:
        in_specs.append(pl.BlockSpec((None, 1, ns), lambda i, j: (j, 0, 0)))
        args.append(bias)
    return _call(
        body, name=name, grid=(T // tm, G), in_specs=in_specs,
        out_specs=[pl.BlockSpec((tm, ns), lambda i, j: (i, j)),
                   pl.BlockSpec((tm, D), lambda i, j: (i, 0))],
        out_shape=[_sds((T, G * ns), out_dtype), _sds((T, D), MXU)],
    )(*args)


def mm_res(y, w, x, *, name, tm=512):
    T, K = y.shape

    def body(y_ref, w_ref, x_ref, o_ref):
        o_ref[...] = x_ref[...] + _dot(y_ref[...], w_ref[...])

    return _call(
        body, name=name, grid=(T // tm,),
        in_specs=[pl.BlockSpec((tm, K), lambda i: (i, 0)),
                  pl.BlockSpec((K, D), lambda i: (0, 0)),
                  pl.BlockSpec((tm, D), lambda i: (i, 0))],
        out_specs=pl.BlockSpec((tm, D), lambda i: (i, 0)),
        out_shape=_sds((T, D), F32),
    )(y, w, x)


def mm_nt(a, w, *, name, tm=512):
    T, K = a.shape
    N = w.shape[0]

    def body(a_ref, w_ref, o_ref, ab_ref):
        ab = a_ref[...].astype(MXU)
        ab_ref[...] = ab
        o_ref[...] = _dot_nt(ab, w_ref[...])

    return _call(
        body, name=name, grid=(T // tm,),
        in_specs=[pl.BlockSpec((tm, K), lambda i: (i, 0)),
                  pl.BlockSpec((N, K), lambda i: (0, 0))],
        out_specs=[pl.BlockSpec((tm, N), lambda i: (i, 0)),
                   pl.BlockSpec((tm, K), lambda i: (i, 0))],
        out_shape=[_sds((T, N), F32), _sds((T, K), MXU)],
    )(a, w)


def inproj_bwd(dp, wb, x, g, dres, *, name, tm=512):
    T = x.shape[0]
    G, _, ns = wb.shape

    def body(dp_ref, w_ref, x_ref, g_ref, dres_ref, dx_ref, dg_ref, acc_ref):
        i = pl.program_id(0)
        j = pl.program_id(1)

        @pl.when(j == 0)
        def _():
            acc_ref[...] = jnp.zeros_like(acc_ref)

        @pl.when((i == 0) & (j == 0))
        def _():
            dg_ref[...] = jnp.zeros_like(dg_ref)

        acc_ref[...] += _dot_nt(dp_ref[...], w_ref[...])

        @pl.when(j == G - 1)
        def _():
            dxn, dg = _rms_bwd(acc_ref[...], x_ref[...], g_ref[...])
            dx_ref[...] = dres_ref[...] + dxn
            dg_ref[...] += dg

    return _call(
        body, name=name, grid=(T // tm, G),
        in_specs=[pl.BlockSpec((tm, ns), lambda i, j: (i, j)),
                  pl.BlockSpec((None, D, ns), lambda i, j: (j, 0, 0)),
                  pl.BlockSpec((tm, D), lambda i, j: (i, 0)),
                  pl.BlockSpec((1, D), lambda i, j: (0, 0)),
                  pl.BlockSpec((tm, D), lambda i, j: (i, 0))],
        out_specs=[pl.BlockSpec((tm, D), lambda i, j: (i, 0)),
                   pl.BlockSpec((1, D), lambda i, j: (0, 0))],
        out_shape=[_sds((T, D), F32), _sds((1, D), F32)],
        scratch=[pltpu.VMEM((tm, D), F32)],
    )(dp, wb, x, g, dres)


def loss_head(x, gf, tgt, *, tm=512):
    T = x.shape[0]

    def body(x_ref, g_ref, t_ref, loss_ref, dx_ref, dg_ref):
        @pl.when(pl.program_id(0) == 0)
        def _():
            loss_ref[...] = jnp.zeros_like(loss_ref)
            dg_ref[...] = jnp.zeros_like(dg_ref)

        xv = x_ref[...]
        gv = g_ref[...]
        e = _rms(xv, gv) - t_ref[...]
        loss_ref[...] += (0.5 / D) * jnp.sum(e * e)
        dxn, dg = _rms_bwd(e * (1.0 / D), xv, gv)
        dx_ref[...] = dxn
        dg_ref[...] += dg

    return _call(
        body, name="loss_head", grid=(T // tm,),
        in_specs=[pl.BlockSpec((tm, D), lambda i: (i, 0)),
                  pl.BlockSpec((1, D), lambda i: (0, 0)),
                  pl.BlockSpec((tm, D), lambda i: (i, 0))],
        out_specs=[pl.BlockSpec((8, 128), lambda i: (0, 0)),
                   pl.BlockSpec((tm, D), lambda i: (i, 0)),
                   pl.BlockSpec((1, D), lambda i: (0, 0))],
        out_shape=[_sds((8, 128), F32), _sds((T, D), F32), _sds((1, D), F32)],
    )(x, gf, tgt)


def _glu(p):
    return p[:, :D] * _sigmoid(p[:, D:])


def _ln_stats(u):
    mu = jnp.mean(u, axis=-1, keepdims=True)
    xc = u - mu
    rstd = lax.rsqrt(jnp.mean(xc * xc, axis=-1, keepdims=True) + LN_EPS)
    return xc * rstd, rstd


def conv_fwd_core(p, dw, dwb, lng, lnb, *, S, tt=256):
    T = p.shape[0]
    nb = S // tt
    r = tt // HALO

    def body(pc_ref, pp_ref, dw_ref, dwb_ref, lng_ref, lnb_ref, u2_ref, u4_ref, ubuf):
        first = (pl.program_id(0) % nb) == 0
        ubuf[0:HALO, :] = jnp.where(first, 0.0, _glu(pp_ref[...]))
        ubuf[HALO:, :] = _glu(pc_ref[...])
        for c in range(D // 128):
            cs = slice(c * 128, (c + 1) * 128)
            acc = jnp.zeros((tt, 128), F32)
            for k in range(CONV_W):
                acc = acc + dw_ref[k:k + 1, cs] * ubuf[k + 2:k + 2 + tt, cs]
            u2_ref[:, cs] = acc + dwb_ref[:, cs]
        xh, _ = _ln_stats(u2_ref[...])
        u3 = xh * lng_ref[...] + lnb_ref[...]
        u4_ref[...] = (u3 * _sigmoid(u3)).astype(MXU)

    row = pl.BlockSpec((1, D), lambda i: (0, 0))
    return _call(
        body, name="conv_fwd_core", grid=(T // tt,),
        in_specs=[pl.BlockSpec((tt, 2 * D), lambda i: (i, 0)),
                  pl.BlockSpec((HALO, 2 * D), lambda i: (jnp.maximum(i * r - 1, 0), 0)),
                  pl.BlockSpec((HALO, D), lambda i: (0, 0)), row, row, row],
        out_specs=[pl.BlockSpec((tt, D), lambda i: (i, 0)), pl.BlockSpec((tt, D), lambda i: (i, 0))],
        out_shape=[_sds((T, D), F32), _sds((T, D), MXU)],
        scratch=[pltpu.VMEM((tt + HALO, D), F32)],
    )(p, p, dw, dwb, lng, lnb)


def conv_bwd_rows(dy, wout, u2, lng, lnb, *, tm=512):
    T = dy.shape[0]

    def body(dy_ref, w_ref, u2_ref, lng_ref, lnb_ref, du2_ref, dyb_ref, dlng_ref, dlnb_ref, ddwb_ref):
        @pl.when(pl.program_id(0) == 0)
        def _():
            dlng_ref[...] = jnp.zeros_like(dlng_ref)
            dlnb_ref[...] = jnp.zeros_like(dlnb_ref)
            ddwb_ref[...] = jnp.zeros_like(ddwb_ref)

        dyb = dy_ref[...].astype(MXU)
        dyb_ref[...] = dyb
        du4 = _dot_nt(dyb, w_ref[...])
        xh, rstd = _ln_stats(u2_ref[...])
        lng_v = lng_ref[...]
        u3 = xh * lng_v + lnb_ref[...]
        s = _sigmoid(u3)
        du3 = du4 * (s * (1.0 + u3 * (1.0 - s)))
        dlng_ref[...] += jnp.sum(du3 * xh, axis=0, keepdims=True)
        dlnb_ref[...] += jnp.sum(du3, axis=0, keepdims=True)
        dxh = du3 * lng_v
        du2 = rstd * (dxh - jnp.mean(dxh, axis=-1, keepdims=True)
                      - xh * jnp.mean(dxh * xh, axis=-1, keepdims=True))
        du2_ref[...] = du2
        ddwb_ref[...] += jnp.sum(du2, axis=0, keepdims=True)

    row = pl.BlockSpec((1, D), lambda i: (0, 0))
    blk = pl.BlockSpec((tm, D), lambda i: (i, 0))
    return _call(
        body, name="conv_bwd_rows", grid=(T // tm,),
        in_specs=[blk, pl.BlockSpec((D, D), lambda i: (0, 0)), blk, row, row],
        out_specs=[blk, blk, row, row, row],
        out_shape=[_sds((T, D), F32), _sds((T, D), MXU), _sds((1, D), F32), _sds((1, D), F32), _sds((1, D), F32)],
    )(dy, wout, u2, lng, lnb)


def conv_bwd_core(du2, p, dw, *, S, tt=256):
    T = p.shape[0]
    nb = S // tt
    r = tt // HALO
    last_halo = T // HALO - 1

    def body(dc_ref, dn_ref, pc_ref, pp_ref, dw_ref, dp_ref, dbin_ref, ddw_ref, ubuf, dbuf):
        i = pl.program_id(0)

        @pl.when(i == 0)
        def _():
            dbin_ref[...] = jnp.zeros_like(dbin_ref)
            ddw_ref[...] = jnp.zeros_like(ddw_ref)

        first = (i % nb) == 0
        last = (i % nb) == nb - 1
        ubuf[0:HALO, :] = jnp.where(first, 0.0, _glu(pp_ref[...]))
        ubuf[HALO:, :] = _glu(pc_ref[...])
        dbuf[0:tt, :] = dc_ref[...]
        dbuf[tt:, :] = jnp.where(last, 0.0, dn_ref[...])
        pc = pc_ref[...]
        for c in range(D // 128):
            cs = slice(c * 128, (c + 1) * 128)
            dcur = dbuf[0:tt, cs]
            du = jnp.zeros((tt, 128), F32)
            for k in range(CONV_W):
                ddw_ref[k:k + 1, cs] += jnp.sum(dcur * ubuf[k + 2:k + 2 + tt, cs], axis=0, keepdims=True)
                du = du + dw_ref[k:k + 1, cs] * dbuf[CONV_W - 1 - k:CONV_W - 1 - k + tt, cs]
            a = pc[:, c * 128:(c + 1) * 128]
            sb = _sigmoid(pc[:, D + c * 128:D + (c + 1) * 128])
            da = du * sb
            db = du * a * sb * (1.0 - sb)
            dp_ref[:, cs] = da.astype(MXU)
            dp_ref[:, D + c * 128:D + (c + 1) * 128] = db.astype(MXU)
            dbin_ref[:, cs] += jnp.sum(da, axis=0, keepdims=True)
            dbin_ref[:, D + c * 128:D + (c + 1) * 128] += jnp.sum(db, axis=0, keepdims=True)

    return _call(
        body, name="conv_bwd_core", grid=(T // tt,),
        in_specs=[pl.BlockSpec((tt, D), lambda i: (i, 0)),
                  pl.BlockSpec((HALO, D), lambda i: (jnp.minimum((i + 1) * r, last_halo), 0)),
                  pl.BlockSpec((tt, 2 * D), lambda i: (i, 0)),
                  pl.BlockSpec((HALO, 2 * D), lambda i: (jnp.maximum(i * r - 1, 0), 0)),
                  pl.BlockSpec((HALO, D), lambda i: (0, 0))],
        out_specs=[pl.BlockSpec((tt, 2 * D), lambda i: (i, 0)),
                   pl.BlockSpec((1, 2 * D), lambda i: (0, 0)),
                   pl.BlockSpec((HALO, D), lambda i: (0, 0))],
        out_shape=[_sds((T, 2 * D), MXU), _sds((1, 2 * D), F32), _sds((HALO, D), F32)],
        scratch=[pltpu.VMEM((tt + HALO, D), F32), pltpu.VMEM((tt + HALO, D), F32)],
    )(du2, du2, p, p, dw)


PH = 16


def _pool_cnt(i, nb, tt, win):
    pos = (i % nb) * tt + lax.broadcasted_iota(jnp.int32, (tt, 1), 0)
    return jnp.minimum(pos + 1, win).astype(F32)


def pool_fwd(x, g, wp, scale, *, S, tt=256):
    T = x.shape[0]
    nb = S // tt
    r = tt // PH

    def body(xc_ref, xp_ref, g_ref, wp_ref, sc_ref, xo_ref, m_ref, hbuf):
        i = pl.program_id(0)
        first = (i % nb) == 0
        gv = g_ref[...]
        hbuf[0:PH, :] = jnp.where(first, 0.0, _rms(xp_ref[...], gv))
        xc = xc_ref[...]
        hbuf[PH:, :] = _rms(xc, gv)
        for gi, win in enumerate(POOL_WIN):
            gs = slice(gi * POOL_G, (gi + 1) * POOL_G)
            acc = hbuf[PH:PH + tt, gs]
            for j in range(1, win):
                acc = acc + hbuf[PH - j:PH - j + tt, gs]
            m = (acc / _pool_cnt(i, nb, tt, win) - hbuf[PH:PH + tt, gs]).astype(MXU)
            m_ref[:, gs] = m
            xo_ref[:, gs] = xc[:, gs] + _dot(m, wp_ref[gi]) * sc_ref[:, gs]

    row = pl.BlockSpec((1, D), lambda i: (0, 0))
    blk = pl.BlockSpec((tt, D), lambda i: (i, 0))
    return _call(
        body, name="pool_fwd", grid=(T // tt,),
        in_specs=[blk, pl.BlockSpec((PH, D), lambda i: (jnp.maximum(i * r - 1, 0), 0)), row,
                  pl.BlockSpec((len(POOL_WIN), POOL_G, POOL_G), lambda i: (0, 0, 0)), row],
        out_specs=[blk, blk],
        out_shape=[_sds((T, D), F32), _sds((T, D), MXU)],
        scratch=[pltpu.VMEM((tt + PH, D), F32)],
    )(x, x, g, wp, scale)


def pool_bwd_rows(dy, m, wp, scale, *, S, tt=256):
    T = dy.shape[0]
    nb = S // tt

    def body(dy_ref, m_ref, wp_ref, sc_ref, dmc_ref, dyp_ref, dsc_ref):
        i = pl.program_id(0)

        @pl.when(i == 0)
        def _():
            dsc_ref[...] = jnp.zeros_like(dsc_ref)

        for gi, win in enumerate(POOL_WIN):
            gs = slice(gi * POOL_G, (gi + 1) * POOL_G)
            dyg = dy_ref[:, gs]
            w = wp_ref[gi]
            dsc_ref[:, gs] += jnp.sum(dyg * _dot(m_ref[:, gs], w), axis=0, keepdims=True)
            dyp = (dyg * sc_ref[:, gs]).astype(MXU)
            dyp_ref[:, gs] = dyp
            dmc_ref[:, gs] = _dot_nt(dyp, w) / _pool_cnt(i, nb, tt, win)

    row = pl.BlockSpec((1, D), lambda i: (0, 0))
    blk = pl.BlockSpec((tt, D), lambda i: (i, 0))
    return _call(
        body, name="pool_bwd_rows", grid=(T // tt,),
        in_specs=[blk, blk, pl.BlockSpec((len(POOL_WIN), POOL_G, POOL_G), lambda i: (0, 0, 0)), row],
        out_specs=[blk, blk, row],
        out_shape=[_sds((T, D), F32), _sds((T, D), MXU), _sds((1, D), F32)],
    )(dy, m, wp, scale)


def pool_bwd_core(dmc, x, g, dres, *, S, tt=256):
    T = x.shape[0]
    nb = S // tt
    r = tt // PH
    last_halo = T // PH - 1

    def body(dc_ref, dn_ref, x_ref, g_ref, dres_ref, dx_ref, dg_ref, dbuf, dh_buf):
        i = pl.program_id(0)

        @pl.when(i == 0)
        def _():
            dg_ref[...] = jnp.zeros_like(dg_ref)

        last = (i % nb) == nb - 1
        dbuf[0:tt, :] = dc_ref[...]
        dbuf[tt:, :] = jnp.where(last, 0.0, dn_ref[...])
        for gi, win in enumerate(POOL_WIN):
            gs = slice(gi * POOL_G, (gi + 1) * POOL_G)
            cur = dbuf[0:tt, gs]
            acc = cur
            for j in range(1, win):
                acc = acc + dbuf[j:j + tt, gs]
            dh_buf[:, gs] = acc - cur * _pool_cnt(i, nb, tt, win)
        dxn, dg = _rms_bwd(dh_buf[...], x_ref[...], g_ref[...])
        dx_ref[...] = dres_ref[...] + dxn
        dg_ref[...] += dg

    row = pl.BlockSpec((1, D), lambda i: (0, 0))
    blk = pl.BlockSpec((tt, D), lambda i: (i, 0))
    return _call(
        body, name="pool_bwd_core", grid=(T // tt,),
        in_specs=[blk, pl.BlockSpec((PH, D), lambda i: (jnp.minimum((i + 1) * r, last_halo), 0)), blk, row, blk],
        out_specs=[blk, row],
        out_shape=[_sds((T, D), F32), _sds((1, D), F32)],
        scratch=[pltpu.VMEM((tt + PH, D), F32), pltpu.VMEM((tt, D), F32)],
    )(dmc, dmc, x, g, dres)


NEG = -1e30


def _tri(n, upper=False):
    r = lax.broadcasted_iota(jnp.int32, (n, n), 0)
    c = lax.broadcasted_iota(jnp.int32, (n, n), 1)
    return (r <= c if upper else r >= c).astype(F32)


def _dot_hi(a, b):
    return jnp.dot(a, b, preferred_element_type=F32, precision=lax.Precision.HIGHEST)


def _log_sigmoid(z):
    return jnp.minimum(z, 0.0) - jnp.log(1.0 + jnp.exp(-jnp.abs(z)))


def fox_cum(fl, bf, *, S, tt=256):
    T = fl.shape[0]
    nb = S // tt

    def body(fl_ref, bf_ref, c_ref, ct_ref, carry):
        i = pl.program_id(0)

        @pl.when((i % nb) == 0)
        def _():
            carry[...] = jnp.zeros_like(carry)

        lf = _log_sigmoid(fl_ref[...] + bf_ref[...])
        c = _dot_hi(_tri(tt), lf) + carry[...]
        c_ref[...] = c
        carry[...] = c[tt - 1:tt, :]
        ct_ref[...] = c.T[0:FOX_H, :]

    return _call(
        body, name="fox_cum", grid=(T // tt,),
        in_specs=[pl.BlockSpec((tt, 128), lambda i: (i, 0)), pl.BlockSpec((1, 128), lambda i: (0, 0))],
        out_specs=[pl.BlockSpec((tt, 128), lambda i: (i, 0)),
                   pl.BlockSpec((None, FOX_H, tt), lambda i: (i // nb, 0, i % nb))],
        out_shape=[_sds((T, 128), F32), _sds((T // S, FOX_H, S), F32)],
        scratch=[pltpu.VMEM((1, 128), F32)],
    )(fl, bf)


def _fox_logits(q_ref, k_ref, c_ref, ct_ref, h, diag, tq):
    hs = slice(h * FOX_DH, (h + 1) * FOX_DH)
    s = _dot_nt(q_ref[:, hs], k_ref[:, hs]) * (FOX_DH ** -0.5) + (c_ref[:, h:h + 1] - ct_ref[h:h + 1, :])
    if diag:
        r = lax.broadcasted_iota(jnp.int32, (tq, tq), 0)
        c = lax.broadcasted_iota(jnp.int32, (tq, tq), 1)
        s = jnp.where(r >= c, s, NEG)
    return s


def _fox_specs(tq, nq, q_of, k_of):
    qrow = lambda col: pl.BlockSpec((tq, D), lambda b, i, j: (b * nq + q_of(i, j), col))
    krow = lambda col: pl.BlockSpec((tq, D), lambda b, i, j: (b * nq + k_of(i, j), col))
    qvec = pl.BlockSpec((tq, 128), lambda b, i, j: (b * nq + q_of(i, j), 0))
    kvec = pl.BlockSpec((tq, 128), lambda b, i, j: (b * nq + k_of(i, j), 0))
    ct = pl.BlockSpec((None, FOX_H, tq), lambda b, i, j: (b, 0, k_of(i, j)))
    return qrow, krow, qvec, kvec, ct


def fox_fwd(qkv, c, ct, *, S, tq=256):
    T = qkv.shape[0]
    nq = S // tq

    def body(q_ref, k_ref, v_ref, c_ref, ct_ref, o_ref, o32_ref, lse_ref, m_sc, l_sc, acc, acc_lo):
        qi = pl.program_id(1)
        ki = pl.program_id(2)

        @pl.when(ki == 0)
        def _():
            m_sc[...] = jnp.full_like(m_sc, NEG)
            l_sc[...] = jnp.zeros_like(l_sc)
            acc[...] = jnp.zeros_like(acc)
            acc_lo[...] = jnp.zeros_like(acc_lo)

        def step(diag):
            for h in range(FOX_H):
                hs = slice(h * FOX_DH, (h + 1) * FOX_DH)
                s = _fox_logits(q_ref, k_ref, c_ref, ct_ref, h, diag, tq)
                m_prev = m_sc[:, h:h + 1]
                m_new = jnp.maximum(m_prev, jnp.max(s, axis=-1, keepdims=True))
                alpha = jnp.exp(m_prev - m_new)
                p = jnp.exp(s - m_new)
                l_sc[:, h:h + 1] = alpha * l_sc[:, h:h + 1] + jnp.sum(p, axis=-1, keepdims=True)
                hi, lo = _split(p)
                acc[:, hs] = alpha * acc[:, hs] + _dot(hi, v_ref[:, hs])
                acc_lo[:, hs] = alpha * acc_lo[:, hs] + _dot(lo, v_ref[:, hs])
                m_sc[:, h:h + 1] = m_new

        @pl.when(ki < qi)
        def _():
            step(False)

        @pl.when(ki == qi)
        def _():
            step(True)
            for h in range(FOX_H):
                hs = slice(h * FOX_DH, (h + 1) * FOX_DH)
                o_ref[:, hs] = (acc[:, hs] / l_sc[:, h:h + 1]).astype(MXU)
                o32_ref[:, hs] = (acc[:, hs] + acc_lo[:, hs]) / l_sc[:, h:h + 1]
            lse_ref[...] = m_sc[...] + jnp.log(jnp.maximum(l_sc[...], 1e-37))

    qrow, krow, qvec, kvec, ctspec = _fox_specs(tq, nq, lambda i, j: i, lambda i, j: jnp.minimum(i, j))
    return _call(
        body, name="fox_fwd", grid=(T // S, nq, nq),
        in_specs=[qrow(0), krow(1), krow(2), qvec, ctspec],
        out_specs=[qrow(0), qrow(0), qvec],
        out_shape=[_sds((T, D), MXU), _sds((T, D), F32), _sds((T, 128), F32)],
        scratch=[pltpu.VMEM((tq, 128), F32), pltpu.VMEM((tq, 128), F32), pltpu.VMEM((tq, D), F32),
                 pltpu.VMEM((tq, D), F32)],
    )(qkv, qkv, qkv, c, ct)


def fox_bwd_dq(qkv, do, o, lse, c, ct, *, S, tq=256):
    T = qkv.shape[0]
    nq = S // tq

    def body(q_ref, k_ref, v_ref, do32_ref, o_ref, lse_ref, c_ref, ct_ref, dq_ref, dl_ref, do_ref, acc):
        qi = pl.program_id(1)
        ki = pl.program_id(2)

        @pl.when(ki == 0)
        def _():
            acc[...] = jnp.zeros_like(acc)
            dl_ref[...] = jnp.zeros_like(dl_ref)
            do_ref[...] = do32_ref[...].astype(MXU)
            for h in range(FOX_H):
                hs = slice(h * FOX_DH, (h + 1) * FOX_DH)
                dl_ref[:, h:h + 1] = jnp.sum(do_ref[:, hs].astype(F32) * o_ref[:, hs], axis=-1, keepdims=True)

        def step(diag):
            for h in range(FOX_H):
                hs = slice(h * FOX_DH, (h + 1) * FOX_DH)
                s = _fox_logits(q_ref, k_ref, c_ref, ct_ref, h, diag, tq)
                p = jnp.exp(s - lse_ref[:, h:h + 1])
                dp = _dot_nt(do_ref[:, hs], v_ref[:, hs])
                hi, lo = _split(p * (dp - dl_ref[:, h:h + 1]))
                acc[:, hs] += _dot(hi, k_ref[:, hs]) + _dot(lo, k_ref[:, hs])

        @pl.when(ki < qi)
        def _():
            step(False)

        @pl.when(ki == qi)
        def _():
            step(True)
            dq_ref[...] = (acc[...] * (FOX_DH ** -0.5)).astype(MXU)

    qrow, krow, qvec, kvec, ctspec = _fox_specs(tq, nq, lambda i, j: i, lambda i, j: jnp.minimum(i, j))
    orow = pl.BlockSpec((tq, D), lambda b, i, j: (b * nq + i, 0))
    return _call(
        body, name="fox_bwd_dq", grid=(T // S, nq, nq),
        in_specs=[qrow(0), krow(1), krow(2), orow, orow, qvec, qvec, ctspec],
        out_specs=[orow, qvec, orow],
        out_shape=[_sds((T, D), MXU), _sds((T, 128), F32), _sds((T, D), MXU)],
        scratch=[pltpu.VMEM((tq, D), F32)],
    )(qkv, qkv, qkv, do, o, lse, c, ct)


def fox_bwd_dkv(qkv, do, lse, delta, c, ct, *, S, tq=256):
    T = qkv.shape[0]
    nq = S // tq

    def body(q_ref, k_ref, v_ref, do_ref, lse_ref, dl_ref, c_ref, ct_ref, dk_ref, dv_ref, dck_ref, dk_acc, dv_acc):
        ki = pl.program_id(1)
        qi = pl.program_id(2)

        @pl.when(qi == 0)
        def _():
            dk_acc[...] = jnp.zeros_like(dk_acc)
            dv_acc[...] = jnp.zeros_like(dv_acc)
            dck_ref[...] = jnp.zeros_like(dck_ref)

        def step(diag):
            ones = jnp.ones((tq, 128), MXU)
            for h in range(FOX_H):
                hs = slice(h * FOX_DH, (h + 1) * FOX_DH)
                s = _fox_logits(q_ref, k_ref, c_ref, ct_ref, h, diag, tq)
                p = jnp.exp(s - lse_ref[:, h:h + 1])
                doh = do_ref[:, hs]
                dv_acc[:, hs] += _dot_tn(p.astype(MXU), doh)
                dp = _dot_nt(doh, v_ref[:, hs])
                hi, lo = _split(p * (dp - dl_ref[:, h:h + 1]))
                dk_acc[:, hs] += _dot_tn(hi, q_ref[:, hs]) + _dot_tn(lo, q_ref[:, hs])
                dck_ref[:, h:h + 1] += (_dot_tn(hi, ones) + _dot_tn(lo, ones))[:, 0:1]

        @pl.when(qi > ki)
        def _():
            step(False)

        @pl.when(qi == ki)
        def _():
            step(True)

        @pl.when(qi == nq - 1)
        def _():
            dk_ref[...] = (dk_acc[...] * (FOX_DH ** -0.5)).astype(MXU)
            dv_ref[...] = dv_acc[...].astype(MXU)

    qrow, krow, qvec, kvec, ctspec = _fox_specs(tq, nq, lambda i, j: jnp.maximum(i, j), lambda i, j: i)
    qo = pl.BlockSpec((tq, D), lambda b, i, j: (b * nq + jnp.maximum(i, j), 0))
    ko = pl.BlockSpec((tq, D), lambda b, i, j: (b * nq + i, 0))
    return _call(
        body, name="fox_bwd_dkv", grid=(T // S, nq, nq),
        in_specs=[qrow(0), krow(1), krow(2), qo, qvec, qvec, qvec, ctspec],
        out_specs=[ko, ko, kvec],
        out_shape=[_sds((T, D), MXU), _sds((T, D), MXU), _sds((T, 128), F32)],
        scratch=[pltpu.VMEM((tq, D), F32), pltpu.VMEM((tq, D), F32)],
    )(qkv, qkv, qkv, do, lse, delta, c, ct)


def fox_fin(dck, fl, bf, *, S, tt=256):
    T = fl.shape[0]
    nb = S // tt
    nblk = T // tt

    def body(dck_ref, fl_ref, bf_ref, dfl_ref, dbf_ref, carry):
        i = pl.program_id(0)

        @pl.when(i == 0)
        def _():
            dbf_ref[...] = jnp.zeros_like(dbf_ref)

        @pl.when((i % nb) == 0)
        def _():
            carry[...] = jnp.zeros_like(carry)

        lane = lax.broadcasted_iota(jnp.int32, (tt, 128), 1)
        dc = jnp.where(lane < FOX_H, -dck_ref[...], 0.0)
        dlf = _dot_hi(_tri(tt, upper=True), dc) + carry[...]
        carry[...] = dlf[0:1, :]
        dfl = dlf * _sigmoid(-(fl_ref[...] + bf_ref[...]))
        dfl_ref[...] = dfl.astype(MXU)
        dbf_ref[...] += jnp.sum(dfl, axis=0, keepdims=True)

    rev = pl.BlockSpec((tt, 128), lambda i: (nblk - 1 - i, 0))
    row = pl.BlockSpec((1, 128), lambda i: (0, 0))
    return _call(
        body, name="fox_fin", grid=(nblk,),
        in_specs=[rev, rev, row],
        out_specs=[rev, row],
        out_shape=[_sds((T, 128), MXU), _sds((1, 128), F32)],
        scratch=[pltpu.VMEM((1, 128), F32)],
    )(dck, fl, bf)


def lb_fwd(logits):
    def body(l_ref, lb_ref):
        lv = l_ref[...]
        e = jnp.exp(lv - jnp.max(lv, axis=0, keepdims=True))
        p = e / jnp.sum(e, axis=0, keepdims=True)
        lb_ref[...] = p[1:2, :] + p[2:3, :]

    return _call(body, name="lb_fwd", grid=(1,),
                 in_specs=[pl.BlockSpec((DEPTH, D), lambda i: (0, 0))],
                 out_specs=pl.BlockSpec((1, D), lambda i: (0, 0)),
                 out_shape=_sds((1, D), F32))(logits)


def lb_bwd(logits, dlb):
    def body(l_ref, d_ref, o_ref):
        lv = l_ref[...]
        e = jnp.exp(lv - jnp.max(lv, axis=0, keepdims=True))
        p = e / jnp.sum(e, axis=0, keepdims=True)
        lb = p[1:2, :] + p[2:3, :]
        row = lax.broadcasted_iota(jnp.int32, (DEPTH, D), 0)
        sel = ((row == 1) | (row == 2)).astype(F32)
        o_ref[...] = p * (sel - lb) * d_ref[...]

    return _call(body, name="lb_bwd", grid=(1,),
                 in_specs=[pl.BlockSpec((DEPTH, D), lambda i: (0, 0)), pl.BlockSpec((1, D), lambda i: (0, 0))],
                 out_specs=pl.BlockSpec((DEPTH, D), lambda i: (0, 0)),
                 out_shape=_sds((DEPTH, D), F32))(logits, dlb)


def _hgrn_gates(qr, fr, lb):
    sg = _sigmoid(fr)
    sneg = _sigmoid(-fr)
    f = lb + (1.0 - lb) * sg
    kk = (1.0 - lb) * sneg
    G = _dot_hi(_tri(HG_C), jnp.log(f))
    eG = jnp.exp(G)
    einv = jnp.exp(-G)
    elast = jnp.exp(G[HG_C - 1:HG_C, :] - G)
    q = qr * _sigmoid(qr)
    return dict(q=q, kk=kk, f=f, sg=sg, sneg=sneg, eG=eG, einv=einv, elast=elast,
                qg=q * eG, kinv=kk * einv, khat=kk * elast, glast=jnp.exp(G[HG_C - 1:HG_C, :]))


def _tril_mask(x):
    r = lax.broadcasted_iota(jnp.int32, x.shape, 0)
    c = lax.broadcasted_iota(jnp.int32, x.shape, 1)
    return jnp.where(r >= c, x, 0.0)


def hgrn_fwd(p, lb, ng, *, S, R=128):
    T = p.shape[0]
    nr = S // R
    ncr = R // HG_C

    def body(q_ref, f_ref, v_ref, gt_ref, lb_ref, ng_ref, y_ref, o_ref, st_ref, st):
        @pl.when(pl.program_id(1) == 0)
        def _():
            st[...] = jnp.zeros_like(st)

        lbv = lb_ref[...]
        for ch in range(ncr):
            rows = slice(ch * HG_C, (ch + 1) * HG_C)
            gt = _hgrn_gates(q_ref[rows, :], f_ref[rows, :], lbv)
            for h in range(HG_H):
                hs = slice(h * HG_DK, (h + 1) * HG_DK)
                sp = st[h]
                st_ref[ch, h] = sp
                qg = gt["qg"][:, hs].astype(MXU)
                vh = v_ref[rows, hs].astype(MXU)
                A = _tril_mask(_dot_nt(qg, gt["kinv"][:, hs].astype(MXU)))
                o_ref[rows, hs] = _dot_nt(qg, sp.astype(MXU)) + _dot(A.astype(MXU), vh)
                st[h] = sp * gt["glast"][:, hs] + _dot_tn(vh, gt["khat"][:, hs].astype(MXU))
        gate = gt_ref[...]
        sgate = gate * _sigmoid(gate)
        for h in range(HG_H):
            hs = slice(h * HG_DK, (h + 1) * HG_DK)
            oh = o_ref[:, hs]
            r = lax.rsqrt(jnp.mean(oh * oh, axis=-1, keepdims=True) + RMS_EPS)
            y_ref[:, hs] = (oh * r * ng_ref[:, hs] * sgate[:, hs]).astype(MXU)

    col = lambda c: pl.BlockSpec((R, D), lambda b, i: (b * nr + i, c))
    row = pl.BlockSpec((1, D), lambda b, i: (0, 0))
    return _call(
        body, name="hgrn_fwd", grid=(T // S, nr),
        in_specs=[col(0), col(1), col(2), col(3), row, row],
        out_specs=[col(0), col(0),
                   pl.BlockSpec((ncr, HG_H, HG_DK, HG_DK), lambda b, i: (b * nr + i, 0, 0, 0))],
        out_shape=[_sds((T, D), MXU), _sds((T, D), F32), _sds((T // HG_C, HG_H, HG_DK, HG_DK), F32)],
        scratch=[pltpu.VMEM((HG_H, HG_DK, HG_DK), F32)],
    )(p, p, p, p, lb, ng)


def hgrn_bwd(p, o, dyo, states, lb, ng, *, S, R=128):
    T = p.shape[0]
    nr = S // R
    ncr = R // HG_C

    def body(q_ref, f_ref, v_ref, gt_ref, o_ref, dy_ref, st_ref, lb_ref, ng_ref,
             dp_ref, dlb_ref, dng_ref, dst, do_buf, dG_buf, dqb, dkb):
        b = pl.program_id(0)
        i = pl.program_id(1)

        @pl.when(i == 0)
        def _():
            dst[...] = jnp.zeros_like(dst)

        @pl.when((b == 0) & (i == 0))
        def _():
            dlb_ref[...] = jnp.zeros_like(dlb_ref)
            dng_ref[...] = jnp.zeros_like(dng_ref)

        lbv = lb_ref[...]
        gate = gt_ref[...]
        sg_gate = _sigmoid(gate)
        silu_gate = gate * sg_gate
        for h in range(HG_H):
            hs = slice(h * HG_DK, (h + 1) * HG_DK)
            oh = o_ref[:, hs]
            r = lax.rsqrt(jnp.mean(oh * oh, axis=-1, keepdims=True) + RMS_EPS)
            ohat = oh * r
            dyh = dy_ref[:, hs]
            ngh = ng_ref[:, hs]
            dng_ref[:, hs] += jnp.sum(dyh * silu_gate[:, hs] * ohat, axis=0, keepdims=True)
            dp_ref[:, 3 * D + h * HG_DK:3 * D + (h + 1) * HG_DK] = (
                dyh * ohat * ngh * (sg_gate[:, hs] * (1.0 + gate[:, hs] * (1.0 - sg_gate[:, hs])))).astype(MXU)
            dn = dyh * ngh * silu_gate[:, hs]
            do_buf[:, hs] = r * (dn - ohat * jnp.mean(dn * ohat, axis=-1, keepdims=True))

        lastrow = lax.broadcasted_iota(jnp.int32, (HG_C, HG_DK), 0) == HG_C - 1
        for ch in reversed(range(ncr)):
            rows = slice(ch * HG_C, (ch + 1) * HG_C)
            qr = q_ref[rows, :]
            gt = _hgrn_gates(qr, f_ref[rows, :], lbv)
            for h in range(HG_H):
                hs = slice(h * HG_DK, (h + 1) * HG_DK)
                sp = st_ref[ch, h]
                ds = dst[h]
                qg32, kinv32, khat32 = gt["qg"][:, hs], gt["kinv"][:, hs], gt["khat"][:, hs]
                qg, kinv, khat = qg32.astype(MXU), kinv32.astype(MXU), khat32.astype(MXU)
                vh = v_ref[rows, hs].astype(MXU)
                doh = do_buf[rows, hs].astype(MXU)
                dsb = ds.astype(MXU)
                A = _tril_mask(_dot_nt(qg, kinv)).astype(MXU)
                dA = _tril_mask(_dot_nt(doh, vh)).astype(MXU)
                dqg = _dot(doh, sp.astype(MXU)) + _dot(dA, kinv)
                dkinv = _dot_tn(dA, qg)
                dp_ref[rows, 2 * D + h * HG_DK:2 * D + (h + 1) * HG_DK] = (
                    _dot_tn(A, doh) + _dot_nt(khat, dsb)).astype(MXU)
                dkhat = _dot(vh, dsb)
                glast = gt["glast"][:, hs]
                qg32, kinv32, khat32 = qg.astype(F32), kinv.astype(F32), khat.astype(F32)
                extra = (glast * jnp.sum(dsb.astype(F32) * sp.astype(MXU).astype(F32), axis=0, keepdims=True)
                         + jnp.sum(dkhat * khat32, axis=0, keepdims=True))
                dst[h] = ds * glast + _dot_tn(doh, qg)
                dG = dqg * qg32 - dkinv * kinv32 - dkhat * khat32
                dG_buf[:, hs] = dG + jnp.where(lastrow, extra, 0.0)
                dqb[:, hs] = dqg * gt["eG"][:, hs]
                dkb[:, hs] = dkinv * gt["einv"][:, hs] + dkhat * gt["elast"][:, hs]
            dg = _dot_hi(_tri(HG_C, upper=True), dG_buf[...])
            dk = dkb[...]
            sneg, f = gt["sneg"], gt["f"]
            c1 = (1.0 - lbv) * gt["sg"] * sneg
            dp_ref[rows, D:2 * D] = (dg * c1 / f - dk * c1).astype(MXU)
            dlb_ref[...] += jnp.sum(dg * sneg / f - dk * sneg, axis=0, keepdims=True)
            sq = _sigmoid(qr)
            dp_ref[rows, 0:D] = (dqb[...] * (sq * (1.0 + qr * (1.0 - sq)))).astype(MXU)

    rev = lambda b, i: b * nr + nr - 1 - i
    col = lambda c: pl.BlockSpec((R, D), lambda b, i: (rev(b, i), c))
    row = pl.BlockSpec((1, D), lambda b, i: (0, 0))
    return _call(
        body, name="hgrn_bwd", grid=(T // S, nr),
        in_specs=[col(0), col(1), col(2), col(3), col(0), col(0),
                  pl.BlockSpec((ncr, HG_H, HG_DK, HG_DK), lambda b, i: (rev(b, i), 0, 0, 0)), row, row],
        out_specs=[pl.BlockSpec((R, 4 * D), lambda b, i: (rev(b, i), 0)), row, row],
        out_shape=[_sds((T, 4 * D), MXU), _sds((1, D), F32), _sds((1, D), F32)],
        scratch=[pltpu.VMEM((HG_H, HG_DK, HG_DK), F32), pltpu.VMEM((R, D), F32), pltpu.VMEM((HG_C, D), F32),
                 pltpu.VMEM((HG_C, D), F32), pltpu.VMEM((HG_C, D), F32)],
    )(p, p, p, p, o, dyo, states, lb, ng)


def _mm_tn_stack(a, b, *, name, G, M, N, tk, stack):
    E, e, buf = stack
    T = a.shape[-2]

    def spec(arr, width):
        if arr.ndim == 3:
            return pl.BlockSpec((None, tk, width), lambda g, k: (g, k, 0))
        return pl.BlockSpec((tk, width), lambda g, k: (k, 0))

    def body(*refs):
        a_ref, b_ref, o_ref = refs[0], refs[1], refs[-1]

        @pl.when(pl.program_id(1) == 0)
        def _():
            o_ref[...] = jnp.zeros_like(o_ref)

        o_ref[...] += _dot_tn(a_ref[...], b_ref[...])

    in_specs = [spec(a, M), spec(b, N)]
    args = [a, b]
    aliases = {}
    if buf is not None:
        in_specs.append(pl.BlockSpec(memory_space=pl.ANY))
        args.append(buf)
        aliases = {2: 0}
    return pl.pallas_call(
        body, name=name, grid=(G, T // tk), in_specs=in_specs,
        out_specs=pl.BlockSpec((None, None, M, N), lambda g, k: (g, e, 0, 0)),
        out_shape=_sds((G, E, M, N), F32), input_output_aliases=aliases,
        compiler_params=pltpu.CompilerParams(dimension_semantics=("arbitrary", "arbitrary"),
                                             vmem_limit_bytes=VMEM_LIMIT))(*args)


MESH = pl.DeviceIdType.MESH
ANY = pl.BlockSpec(memory_space=pl.ANY)


def _pos():
    return lax.axis_index("x"), lax.axis_index("y"), lax.axis_index("c")


def _other_chips(x, y):
    return [(1 - x, y), (x, 1 - y), (1 - x, 1 - y)]


def _comm_call(body, *, name, args, out_shape, n_sem):
    return pl.pallas_call(
        body, name=name, in_specs=[ANY] * len(args), out_specs=[ANY] * len(out_shape), out_shape=out_shape,
        scratch_shapes=[pltpu.SemaphoreType.DMA((n_sem,)), pltpu.SemaphoreType.DMA((n_sem,)),
                        pltpu.SemaphoreType.DMA((len(args),))],
    )(*args)


def all_gather_chips(xs):
    n = len(xs)

    def body(*refs):
        x_refs, o_refs = refs[:n], refs[n:2 * n]
        ssem, rsem, lsem = refs[2 * n:]
        x, y, c = _pos()
        me = 2 * x + y
        chips = _other_chips(x, y)
        sib = (x, y, 1 - c)

        def rc(src, dst, idx, dev):
            return pltpu.make_async_remote_copy(src_ref=src, dst_ref=dst, send_sem=ssem.at[idx], recv_sem=rsem.at[idx],
                                                device_id=dev, device_id_type=MESH)

        local, started = [], []
        for t in range(n):
            hr = xs[t].shape[0] // 2
            mine = pl.ds(c * hr, hr)
            cp = pltpu.make_async_copy(x_refs[t], o_refs[t].at[me], lsem.at[t])
            cp.start()
            local.append(cp)
            for k, (cx, cy) in enumerate(chips):
                cp = rc(x_refs[t].at[mine], o_refs[t].at[me, mine], 6 * t + k, (cx, cy, c))
                cp.start()
                started.append(cp)
        for t in range(n):
            hr = xs[t].shape[0] // 2
            mine = pl.ds(c * hr, hr)
            for k, (cx, cy) in enumerate(chips):
                landed = o_refs[t].at[2 * cx + cy, mine]
                rc(landed, landed, 6 * t + k, (cx, cy, c)).wait_recv()
                cp = rc(landed, landed, 6 * t + 3 + k, sib)
                cp.start()
                started.append(cp)
        for t in range(n):
            hr = xs[t].shape[0] // 2
            theirs = pl.ds((1 - c) * hr, hr)
            for k, (cx, cy) in enumerate(chips):
                other = o_refs[t].at[2 * cx + cy, theirs]
                rc(other, other, 6 * t + 3 + k, sib).wait_recv()
        for cp in started:
            cp.wait_send()
        for cp in local:
            cp.wait()

    outs = _comm_call(body, name="all_gather_chips", args=list(xs),
                      out_shape=[_sds((NSH,) + a.shape, a.dtype) for a in xs], n_sem=6 * n)
    return list(outs)


def sibling_half_exchange(gs):
    n = len(gs)

    def body(*refs):
        g_refs, o_refs = refs[:n], refs[n:2 * n]
        ssem, rsem, _ = refs[2 * n:]
        x, y, c = _pos()
        cps = []
        for t in range(n):
            hr = gs[t].shape[1] // 2
            for j in range(NSH):
                cp = pltpu.make_async_remote_copy(
                    src_ref=g_refs[t].at[j, pl.ds((1 - c) * hr, hr)], dst_ref=o_refs[t].at[j],
                    send_sem=ssem.at[NSH * t + j], recv_sem=rsem.at[NSH * t + j],
                    device_id=(x, y, 1 - c), device_id_type=MESH)
                cp.start()
                cps.append(cp)
        for cp in cps:
            cp.wait()

    outs = _comm_call(body, name="sibling_half_exchange", args=list(gs),
                      out_shape=[_sds((NSH, g.shape[1] // 2, g.shape[2]), g.dtype) for g in gs], n_sem=NSH * n)
    return list(outs)


def chip_scatter(ss):
    n = len(ss)

    def body(*refs):
        s_refs, o_refs = refs[:n], refs[n:2 * n]
        ssem, rsem, _ = refs[2 * n:]
        x, y, c = _pos()
        cps = []
        for t in range(n):
            for k, (cx, cy) in enumerate(_other_chips(x, y)):
                cp = pltpu.make_async_remote_copy(
                    src_ref=s_refs[t].at[2 * cx + cy], dst_ref=o_refs[t].at[k],
                    send_sem=ssem.at[3 * t + k], recv_sem=rsem.at[3 * t + k],
                    device_id=(cx, cy, c), device_id_type=MESH)
                cp.start()
                cps.append(cp)
        for cp in cps:
            cp.wait()

    outs = _comm_call(body, name="chip_scatter", args=list(ss),
                      out_shape=[_sds((3,) + s.shape[1:], s.dtype) for s in ss], n_sem=3 * n)
    return list(outs)


def sibling_exchange(rs):
    n = len(rs)

    def body(*refs):
        r_refs, o_refs = refs[:n], refs[n:2 * n]
        ssem, rsem, _ = refs[2 * n:]
        x, y, c = _pos()
        cps = []
        for t in range(n):
            cp = pltpu.make_async_remote_copy(
                src_ref=r_refs[t], dst_ref=o_refs[t], send_sem=ssem.at[t], recv_sem=rsem.at[t],
                device_id=(x, y, 1 - c), device_id_type=MESH)
            cp.start()
            cps.append(cp)
        for cp in cps:
            cp.wait()

    outs = _comm_call(body, name="sibling_exchange", args=list(rs),
                      out_shape=[_sds(r.shape, r.dtype) for r in rs], n_sem=n)
    return list(outs)


def all_gather_devices(v):
    def body(v_ref, o_ref, ssem, rsem, lsem):
        x, y, c = _pos()
        me = 4 * x + 2 * y + c
        loc = pltpu.make_async_copy(v_ref, o_ref.at[me], lsem.at[0])
        loc.start()
        cps = []
        k = 0
        for fx in range(2):
            for fy in range(2):
                for fc in range(2):
                    if fx == fy == fc == 0:
                        continue
                    cp = pltpu.make_async_remote_copy(
                        src_ref=v_ref, dst_ref=o_ref.at[me], send_sem=ssem.at[k], recv_sem=rsem.at[k],
                        device_id=(x ^ fx, y ^ fy, c ^ fc), device_id_type=MESH)
                    cp.start()
                    src = 4 * (x ^ fx) + 2 * (y ^ fy) + (c ^ fc)
                    cps.append((cp, o_ref.at[src], k))
                    k += 1
        for cp, landed, k in cps:
            cp.wait_send()
            pltpu.make_async_remote_copy(
                src_ref=landed, dst_ref=landed, send_sem=ssem.at[k], recv_sem=rsem.at[k],
                device_id=(x, y, c), device_id_type=MESH).wait_recv()
        loc.wait()

    return _comm_call(body, name="all_gather_devices", args=[v],
                      out_shape=[_sds((8,) + v.shape, v.dtype)], n_sem=7)[0]


def _call_sp(body, *, name, grid, in_specs, out_specs, out_shape, pos, args):
    return pl.pallas_call(
        body, name=name,
        grid_spec=pltpu.PrefetchScalarGridSpec(num_scalar_prefetch=1, grid=grid, in_specs=in_specs,
                                               out_specs=out_specs),
        out_shape=out_shape,
        compiler_params=pltpu.CompilerParams(dimension_semantics=("arbitrary",) * len(grid),
                                             vmem_limit_bytes=VMEM_LIMIT))(pos, *args)


def _rows_tile(r):
    for t in (512, 256, 128, 64, 32, 16, 8):
        if r % t == 0:
            return t
    raise ValueError(r)


def pair_sum(g, r, pos):
    _, R, C = g.shape
    hr = R // 2
    tr = _rows_tile(hr)
    nbh = hr // tr

    def body(p_ref, g_ref, r_ref, o_ref):
        o_ref[...] = (g_ref[...] + r_ref[...]).astype(MXU)

    return _call_sp(
        body, name="pair_sum", grid=(NSH, nbh), pos=pos, args=[g, r],
        in_specs=[pl.BlockSpec((None, tr, C), lambda j, i, p: (j, p[0] * nbh + i, 0)),
                  pl.BlockSpec((None, tr, C), lambda j, i, p: (j, i, 0))],
        out_specs=pl.BlockSpec((None, tr, C), lambda j, i, p: (j, i, 0)),
        out_shape=_sds((NSH, hr, C), MXU))


def reduce_own(g, r_sib, r_ici, pos):
    _, R, C = g.shape
    hr = R // 2
    tr = _rows_tile(hr)
    nbh = hr // tr

    def body(p_ref, g_ref, rs_ref, ri_ref, o_ref):
        s = g_ref[...] + rs_ref[...]
        for k in range(3):
            s = s + ri_ref[k].astype(F32)
        o_ref[...] = s

    return _call_sp(
        body, name="reduce_own", grid=(nbh,), pos=pos, args=[g, r_sib, r_ici],
        in_specs=[pl.BlockSpec((None, tr, C), lambda i, p: (p[1], p[0] * nbh + i, 0)),
                  pl.BlockSpec((None, tr, C), lambda i, p: (p[1], i, 0)),
                  pl.BlockSpec((3, tr, C), lambda i, p: (0, i, 0))],
        out_specs=pl.BlockSpec((tr, C), lambda i, p: (i, 0)),
        out_shape=_sds((hr, C), F32))


def _adamw_math(w, g, m, v):
    m = ADAM_B1 * m + (1.0 - ADAM_B1) * g
    v = ADAM_B2 * v + (1.0 - ADAM_B2) * (g * g)
    m_hat = m / (1.0 - ADAM_B1 ** ADAM_STEP)
    v_hat = v / (1.0 - ADAM_B2 ** ADAM_STEP)
    delta = -ADAM_LR * (m_hat / (jnp.sqrt(v_hat) + ADAM_EPS) + ADAM_WD * w)
    return delta, m, v


def adamw_halves(w, m, v, ga, gb, pos):
    R, C = w.shape
    hr = R // 2
    tr = _rows_tile(hr)
    nbh = hr // tr

    def body(p_ref, w_ref, m_ref, v_ref, ga_ref, gb_ref, g_ref, d_ref, mo_ref, vo_ref):
        mine = (pl.program_id(0) // nbh) == p_ref[0]
        g = jnp.where(mine, ga_ref[...], gb_ref[...])
        g_ref[...] = g
        d_ref[...], mo_ref[...], vo_ref[...] = _adamw_math(w_ref[...], g, m_ref[...], v_ref[...])

    blk = pl.BlockSpec((tr, C), lambda i, p: (i, 0))
    return _call_sp(
        body, name="adamw_halves", grid=(R // tr,), pos=pos, args=[w, m, v, ga, gb],
        in_specs=[blk, blk, blk,
                  pl.BlockSpec((tr, C), lambda i, p: (jnp.where(i // nbh == p[0], i % nbh, 0), 0)),
                  pl.BlockSpec((tr, C), lambda i, p: (jnp.where(i // nbh == p[0], 0, i % nbh), 0))],
        out_specs=[blk, blk, blk, blk],
        out_shape=[_sds((R, C), F32)] * 4)


def adamw_sum(gall, w, m, v):
    n, R, C = gall.shape

    def body(ga_ref, w_ref, m_ref, v_ref, g_ref, d_ref, mo_ref, vo_ref):
        g = ga_ref[0]
        for k in range(1, n):
            g = g + ga_ref[k]
        g_ref[...] = g
        d_ref[...], mo_ref[...], vo_ref[...] = _adamw_math(w_ref[...], g, m_ref[...], v_ref[...])

    blk = pl.BlockSpec((R, C), lambda i: (0, 0))
    return _call(body, name="adamw_sum", grid=(1,),
                 in_specs=[pl.BlockSpec((n, R, C), lambda i: (0, 0, 0)), blk, blk, blk],
                 out_specs=[blk, blk, blk, blk], out_shape=[_sds((R, C), F32)] * 4)(gall, w, m, v)


_WEIGHTS = ['ffn_norm', 'ffn_w_gate', 'ffn_w_up', 'ffn_w_down', 'mix_norm', 'final_norm', 'conv_w_in', 'conv_b_in',
            'conv_dw', 'conv_dw_b', 'conv_ln_g', 'conv_ln_b', 'conv_w_out', 'fox_w_in', 'fox_b_f', 'fox_w_out',
            'hgrn_w_in', 'hgrn_lb_logits', 'hgrn_norm', 'hgrn_w_out', 'pool_w', 'pool_scale']
_BIG = ['ffn_w_gate', 'ffn_w_up', 'ffn_w_down', 'conv_w_in', 'conv_w_out', 'fox_w_in', 'fox_w_out',
        'hgrn_w_in', 'hgrn_w_out', 'pool_w']
_SHARDED_SMALL = ['ffn_norm', 'conv_dw', 'hgrn_norm', 'pool_scale']
_REPLICATED = ['mix_norm', 'final_norm', 'conv_b_in', 'conv_dw_b', 'conv_ln_g', 'conv_ln_b', 'fox_b_f', 'hgrn_lb_logits']
FOX_N = 3 * D + FOX_H
FOX_NP = 3200
QS = D // NSH


def _pad_rows(a, rows):
    return jnp.pad(a, ((0, rows - a.shape[0]), (0, 0)))


def _pack_sharded_small(get):
    return jnp.concatenate([get('ffn_norm').reshape(8, -1), _pad_rows(get('conv_dw')[0], 32),
                            get('hgrn_norm'), get('pool_scale'), jnp.zeros((6, get('pool_scale').shape[1]), F32)], axis=0)


def _pack_replicated(get):
    return jnp.concatenate([get('mix_norm'), get('final_norm').reshape(1, D), get('conv_b_in').reshape(2, D),
                            get('conv_dw_b'), get('conv_ln_g'), get('conv_ln_b'),
                            jnp.pad(get('fox_b_f'), ((0, 0), (0, D - FOX_H))), get('hgrn_lb_logits'),
                            jnp.zeros((9, D), F32)], axis=0)


def _unpack_replicated(p):
    return {'mix_norm': p[0:4], 'final_norm': p[4], 'conv_b_in': p[5:7].reshape(1, 2 * D), 'conv_dw_b': p[7:8],
            'conv_ln_g': p[8:9], 'conv_ln_b': p[9:10], 'fox_b_f': p[10:11, :FOX_H], 'hgrn_lb_logits': p[11:15]}


def _unpack_sharded_small(p):
    return {'ffn_norm': p[0:8].reshape(DEPTH, 2, -1), 'conv_dw': p[8:8 + CONV_W][None],
            'hgrn_norm': p[40:41], 'pool_scale': p[41:42]}


def kernel(x, ffn_norm, ffn_w_gate, ffn_w_up, ffn_w_down, mix_norm, final_norm, conv_w_in, conv_b_in, conv_dw, conv_dw_b, conv_ln_g, conv_ln_b, conv_w_out, fox_w_in, fox_b_f, fox_w_out, hgrn_w_in, hgrn_lb_logits, hgrn_norm, hgrn_w_out, pool_w, pool_scale, loss_target, m_ffn_norm, m_ffn_w_gate, m_ffn_w_up, m_ffn_w_down, m_mix_norm, m_final_norm, m_conv_w_in, m_conv_b_in, m_conv_dw, m_conv_dw_b, m_conv_ln_g, m_conv_ln_b, m_conv_w_out, m_fox_w_in, m_fox_b_f, m_fox_w_out, m_hgrn_w_in, m_hgrn_lb_logits, m_hgrn_norm, m_hgrn_w_out, m_pool_w, m_pool_scale, v_ffn_norm, v_ffn_w_gate, v_ffn_w_up, v_ffn_w_down, v_mix_norm, v_final_norm, v_conv_w_in, v_conv_b_in, v_conv_dw, v_conv_dw_b, v_conv_ln_g, v_conv_ln_b, v_conv_w_out, v_fox_w_in, v_fox_b_f, v_fox_w_out, v_hgrn_w_in, v_hgrn_lb_logits, v_hgrn_norm, v_hgrn_w_out, v_pool_w, v_pool_scale):
    W = dict(ffn_norm=ffn_norm, ffn_w_gate=ffn_w_gate, ffn_w_up=ffn_w_up, ffn_w_down=ffn_w_down, mix_norm=mix_norm, final_norm=final_norm, conv_w_in=conv_w_in, conv_b_in=conv_b_in, conv_dw=conv_dw, conv_dw_b=conv_dw_b, conv_ln_g=conv_ln_g, conv_ln_b=conv_ln_b, conv_w_out=conv_w_out, fox_w_in=fox_w_in, fox_b_f=fox_b_f, fox_w_out=fox_w_out, hgrn_w_in=hgrn_w_in, hgrn_lb_logits=hgrn_lb_logits, hgrn_norm=hgrn_norm, hgrn_w_out=hgrn_w_out, pool_w=pool_w, pool_scale=pool_scale)
    M = dict(ffn_norm=m_ffn_norm, ffn_w_gate=m_ffn_w_gate, ffn_w_up=m_ffn_w_up, ffn_w_down=m_ffn_w_down, mix_norm=m_mix_norm, final_norm=m_final_norm, conv_w_in=m_conv_w_in, conv_b_in=m_conv_b_in, conv_dw=m_conv_dw, conv_dw_b=m_conv_dw_b, conv_ln_g=m_conv_ln_g, conv_ln_b=m_conv_ln_b, conv_w_out=m_conv_w_out, fox_w_in=m_fox_w_in, fox_b_f=m_fox_b_f, fox_w_out=m_fox_w_out, hgrn_w_in=m_hgrn_w_in, hgrn_lb_logits=m_hgrn_lb_logits, hgrn_norm=m_hgrn_norm, hgrn_w_out=m_hgrn_w_out, pool_w=m_pool_w, pool_scale=m_pool_scale)
    V = dict(ffn_norm=v_ffn_norm, ffn_w_gate=v_ffn_w_gate, ffn_w_up=v_ffn_w_up, ffn_w_down=v_ffn_w_down, mix_norm=v_mix_norm, final_norm=v_final_norm, conv_w_in=v_conv_w_in, conv_b_in=v_conv_b_in, conv_dw=v_conv_dw, conv_dw_b=v_conv_dw_b, conv_ln_g=v_conv_ln_g, conv_ln_b=v_conv_ln_b, conv_w_out=v_conv_w_out, fox_w_in=v_fox_w_in, fox_b_f=v_fox_b_f, fox_w_out=v_fox_w_out, hgrn_w_in=v_hgrn_w_in, hgrn_lb_logits=v_hgrn_lb_logits, hgrn_norm=v_hgrn_norm, hgrn_w_out=v_hgrn_w_out, pool_w=v_pool_w, pool_scale=v_pool_scale)

    px, py, pc = _pos()
    jme = 2 * px + py
    pos = jnp.stack([pc, jme]).astype(jnp.int32)
    S = x.shape[1]
    T = x.shape[0] * S
    x2 = x.reshape(T, D)
    tgt = loss_target.reshape(T, D)

    flat = lambda a: a.reshape(-1, a.shape[-1])
    gathered = all_gather_chips([flat(W[n]).astype(MXU) for n in _BIG] + [_pack_sharded_small(W.get)])
    G = dict(zip(_BIG, gathered[:-1]))
    small = gathered[-1].transpose(1, 0, 2).reshape(48, D)
    ffn_norm_f, conv_dw_f = small[0:8], small[8:40]
    hgrn_norm_f, pool_scale_f = small[40:41], small[41:42]
    wg_all = G['ffn_w_gate'].reshape(NSH, 2 * DEPTH, D, FS)
    wu_all = G['ffn_w_up'].reshape(NSH, 2 * DEPTH, D, FS)
    wd_all = G['ffn_w_down'].reshape(NSH, 2 * DEPTH, FS, D)
    conv_wi = G['conv_w_in']
    conv_wo = G['conv_w_out'].reshape(D, D)
    fox_full = jnp.pad(G['fox_w_in'].transpose(1, 0, 2).reshape(D, FOX_N), ((0, 0), (0, FOX_NP - FOX_N)))
    fox_w5 = fox_full.reshape(D, 5, FOX_NP // 5).transpose(1, 0, 2)
    fox_wf = fox_full[:, 3 * D:][None]
    fox_bf = jnp.pad(fox_b_f, ((0, 0), (0, 128 - FOX_H)))
    fox_wo = G['fox_w_out'].reshape(D, D)
    hgrn_wi = G['hgrn_w_in']
    hgrn_wo = G['hgrn_w_out'].reshape(D, D)
    pool_wf = G['pool_w'].reshape(NSH, 4, 64, POOL_G).transpose(1, 0, 2, 3).reshape(4, POOL_G, POOL_G)
    conv_bi = conv_b_in.reshape(NSH, 1, 2 * D // NSH)

    def ffn_f(xs, e):
        xo, h, a, b = ffn_fwd(xs, ffn_norm_f[e:e + 1], wg_all, wu_all, wd_all, e)
        return xo, (xs, h, a, b)

    saved = []
    xs = x2
    lb = lb_fwd(hgrn_lb_logits)
    for i in range(DEPTH):
        xs, r0 = ffn_f(xs, 2 * i)
        gm = mix_norm[i:i + 1]
        xin = xs
        if i == 0:
            p, h = norm_mm(xin, gm, conv_wi, conv_bi, name="conv_in", out_dtype=F32)
            u2, u4 = conv_fwd_core(p, conv_dw_f, conv_dw_b, conv_ln_g, conv_ln_b, S=S)
            xs = mm_res(u4, conv_wo, xin, name="conv_out")
            rm = (xin, p, h, u2, u4)
        elif i == 1:
            p, h = norm_mm(xin, gm, fox_w5, None, name="fox_in", out_dtype=MXU)
            fl, _ = norm_mm(xin, gm, fox_wf, None, name="fox_in_f", out_dtype=F32)
            cq, ct = fox_cum(fl, fox_bf, S=S)
            o, o32, lse = fox_fwd(p, cq, ct, S=S)
            xs = mm_res(o, fox_wo, xin, name="fox_out")
            rm = (xin, p, h, fl, cq, ct, o, o32, lse)
        elif i == 2:
            p, h = norm_mm(xin, gm, hgrn_wi, None, name="hgrn_in", out_dtype=F32)
            yh, oh, st = hgrn_fwd(p, lb, hgrn_norm_f, S=S)
            xs = mm_res(yh, hgrn_wo, xin, name="hgrn_out")
            rm = (xin, p, h, yh, oh, st)
        else:
            xs, mp = pool_fwd(xin, gm, pool_wf, pool_scale_f, S=S)
            rm = (xin, mp)
        xs, r1 = ffn_f(xs, 2 * i + 1)
        saved.append((r0, rm, r1))

    loss8, dx, d_final = loss_head(xs, final_norm.reshape(1, D), tgt)

    gb = {'g': None, 'u': None, 'd': None}
    d_ffn_norm = [None] * (2 * DEPTH)
    d_mix_norm = [None] * DEPTH
    gbig = {}
    gsm = {}

    def ffn_b(dy, res, e):
        xin, h, a, b = res
        dxo, da, db, z, dyh, dg = ffn_bwd_dx(xin, ffn_norm_f[e:e + 1], dy, a, b, wg_all, wu_all, wd_all, e)
        gb['g'] = mm_tn(h, da, name="ffn_dwg", G=NSH, M=D, N=FS, stack=(2 * DEPTH, e, gb['g']))
        gb['u'] = mm_tn(h, db, name="ffn_dwu", G=NSH, M=D, N=FS, stack=(2 * DEPTH, e, gb['u']))
        gb['d'] = mm_tn(z, dyh, name="ffn_dwd", G=NSH, M=FS, N=D, stack=(2 * DEPTH, e, gb['d']))
        d_ffn_norm[e] = dg
        return dxo

    for i in reversed(range(DEPTH)):
        r0, rm, r1 = saved[i]
        dx = ffn_b(dx, r1, 2 * i + 1)
        gm = mix_norm[i:i + 1]
        if i == 0:
            xin, p, h, u2, u4 = rm
            du2, dyb, gsm['conv_ln_g'], gsm['conv_ln_b'], gsm['conv_dw_b'] = conv_bwd_rows(dx, conv_wo, u2, conv_ln_g, conv_ln_b)
            dp, gsm['conv_b_in'], ddw = conv_bwd_core(du2, p, conv_dw_f, S=S)
            gsm['conv_dw'] = ddw
            gbig['conv_w_in'] = mm_tn(h, dp, name="conv_dwin", G=NSH, M=D, N=2 * D // NSH, b_step=1)
            gbig['conv_w_out'] = mm_tn(u4, dyb, name="conv_dwout", G=NSH, M=QS, N=D, a_step=1)
            dx, d_mix_norm[i] = inproj_bwd(dp, conv_wi, xin, gm, dx, name="conv_in_bwd")
        elif i == 1:
            xin, p, h, fl, cq, ct, o, o32, lse = rm
            do, dyb = mm_nt(dx, fox_wo, name="fox_out_bwd")
            dq, delta, dob = fox_bwd_dq(p, do, o32, lse, cq, ct, S=S)
            dk, dv, dck = fox_bwd_dkv(p, dob, lse, delta, cq, ct, S=S)
            dfl, dbf = fox_fin(dck, fl, fox_bf, S=S)
            gsm['fox_b_f'] = dbf
            dp = jnp.concatenate([dq, dk, dv, dfl], axis=1)
            dw5 = mm_tn(h, dp, name="fox_dwin", G=5, M=D, N=FOX_NP // 5, b_step=1)
            dwf = dw5.transpose(1, 0, 2).reshape(D, FOX_NP)[:, :FOX_N]
            gbig['fox_w_in'] = dwf.reshape(D, NSH, FOX_N // NSH).transpose(1, 0, 2)
            gbig['fox_w_out'] = mm_tn(o, dyb, name="fox_dwout", G=NSH, M=QS, N=D, a_step=1)
            dx, d_mix_norm[i] = inproj_bwd(dp, fox_w5, xin, gm, dx, name="fox_in_bwd")
        elif i == 2:
            xin, p, h, yh, oh, st = rm
            dyo, dyb = mm_nt(dx, hgrn_wo, name="hgrn_out_bwd")
            dp, dlb, gsm['hgrn_norm'] = hgrn_bwd(p, oh, dyo, st, lb, hgrn_norm_f, S=S)
            gsm['hgrn_lb_logits'] = lb_bwd(hgrn_lb_logits, dlb)
            gbig['hgrn_w_in'] = mm_tn(h, dp, name="hgrn_dwin", G=NSH, M=D, N=D, b_step=1)
            gbig['hgrn_w_out'] = mm_tn(yh, dyb, name="hgrn_dwout", G=NSH, M=QS, N=D, a_step=1)
            dx, d_mix_norm[i] = inproj_bwd(dp, hgrn_wi, xin, gm, dx, name="hgrn_in_bwd")
        else:
            xin, mp = rm
            dmc, dyp, gsm['pool_scale'] = pool_bwd_rows(dx, mp, pool_wf, pool_scale_f, S=S)
            dwp = mm_tn(mp, dyp, name="pool_dw", G=4, M=POOL_G, N=POOL_G, a_step=1, b_step=1)
            gbig['pool_w'] = dwp.reshape(4, NSH, 64, POOL_G).transpose(1, 0, 2, 3).reshape(NSH, 4 * 64, POOL_G)
            dx, d_mix_norm[i] = pool_bwd_core(dmc, xin, gm, dx, S=S)
        dx = ffn_b(dx, r0, 2 * i)

    gbig['ffn_w_gate'] = gb['g'].reshape(NSH, 2 * DEPTH * D, FS)
    gbig['ffn_w_up'] = gb['u'].reshape(NSH, 2 * DEPTH * D, FS)
    gbig['ffn_w_down'] = gb['d'].reshape(NSH, 2 * DEPTH * FS, D)

    gl = [gbig[n] for n in _BIG]
    r_sib = sibling_half_exchange(gl)
    s16 = [pair_sum(g, r, pos) for g, r in zip(gl, r_sib)]
    r_ici = chip_scatter(s16)
    red = [reduce_own(g, rs, ri, pos) for g, rs, ri in zip(gl, r_sib, r_ici)]
    oth = sibling_exchange(red)
    out = {}
    for n, ga, gb_ in zip(_BIG, red, oth):
        res = adamw_halves(flat(W[n]), flat(M[n]), flat(V[n]), ga, gb_, pos)
        out[n] = [r.reshape(W[n].shape) for r in res]

    gfull = {'mix_norm': jnp.concatenate(d_mix_norm, axis=0), 'final_norm': d_final,
             'conv_b_in': gsm['conv_b_in'], 'conv_dw_b': gsm['conv_dw_b'], 'conv_ln_g': gsm['conv_ln_g'],
             'conv_ln_b': gsm['conv_ln_b'], 'fox_b_f': gsm['fox_b_f'][:, :FOX_H], 'hgrn_lb_logits': gsm['hgrn_lb_logits'],
             'ffn_norm': jnp.concatenate(d_ffn_norm, axis=0), 'conv_dw': gsm['conv_dw'][None, :CONV_W],
             'hgrn_norm': gsm['hgrn_norm'], 'pool_scale': gsm['pool_scale']}
    gpack = jnp.concatenate([_pack_replicated(gfull.get), _pack_sharded_small(gfull.get)], axis=0)
    gall = all_gather_devices(gpack)
    rep = adamw_sum(gall[:, :24], _pack_replicated(W.get), _pack_replicated(M.get), _pack_replicated(V.get))
    rep = [_unpack_replicated(r) for r in rep]
    for n in _REPLICATED:
        out[n] = [r[n].reshape(W[n].shape) for r in rep]
    gsh = lax.dynamic_slice_in_dim(gall[:, 24:], jme * QS, QS, axis=2)
    shd = adamw_sum(gsh, _pack_sharded_small(W.get), _pack_sharded_small(M.get), _pack_sharded_small(V.get))
    shd = [_unpack_sharded_small(r) for r in shd]
    for n in _SHARDED_SMALL:
        out[n] = [r[n].reshape(W[n].shape) for r in shd]

    loss = lax.psum(loss8[0, 0], ("x", "y", "c"))
    res = [loss, dx.reshape(x.shape)]
    for k in range(4):
        res += [out[n][k] for n in _WEIGHTS]
    return tuple(res)
```

```python
import functools

import jax
import jax.numpy as jnp
from jax import lax
from jax.experimental import pallas as pl
from jax.experimental.pallas import tpu as pltpu

D = 1024
F = 2816
NSH = 4
FS = F // NSH
DEPTH = 4
RMS_EPS = 1e-6
LN_EPS = 1e-5
CONV_W = 31
HALO = 32
FOX_H = 16
FOX_DH = 64
HG_H = 8
HG_DK = 128
HG_C = 32
POOL_WIN = (2, 4, 8, 16)
POOL_G = 256
MXU = jnp.bfloat16
F32 = jnp.float32
VMEM_LIMIT = 52 * 1024 * 1024

ADAM_LR = 0.001
ADAM_B1 = 0.9
ADAM_B2 = 0.999
ADAM_EPS = 1e-08
ADAM_WD = 0.01
ADAM_STEP = 10


def _call(body, *, name, grid, in_specs, out_specs, out_shape, scratch=()):
    return pl.pallas_call(
        body, name=name, grid=grid, in_specs=in_specs, out_specs=out_specs, out_shape=out_shape,
        scratch_shapes=list(scratch),
        compiler_params=pltpu.CompilerParams(dimension_semantics=("arbitrary",) * len(grid),
                                             vmem_limit_bytes=VMEM_LIMIT))


def _dot(a, b):
    return jnp.dot(a, b, preferred_element_type=F32)


def _dot_nt(a, b):
    return lax.dot_general(a, b, (((1,), (1,)), ((), ())), preferred_element_type=F32)


def _dot_tn(a, b):
    return lax.dot_general(a, b, (((0,), (0,)), ((), ())), preferred_element_type=F32)


def _split(x):
    hi = x.astype(MXU)
    return hi, (x - hi.astype(F32)).astype(MXU)


def _sigmoid(x):
    return 1.0 / (1.0 + jnp.exp(-x))


def _rms(x, g):
    r = lax.rsqrt(jnp.mean(x * x, axis=-1, keepdims=True) + RMS_EPS)
    return x * r * g


def _rms_bwd(dh, x, g):
    r = lax.rsqrt(jnp.mean(x * x, axis=-1, keepdims=True) + RMS_EPS)
    xh = x * r
    dhg = dh * g
    dx = r * (dhg - xh * jnp.mean(dhg * xh, axis=-1, keepdims=True))
    return dx, jnp.sum(dh * xh, axis=0, keepdims=True)


def _sds(shape, dtype):
    return jax.ShapeDtypeStruct(shape, dtype)


def _wspec(w, e):
    if w.ndim == 3:
        return pl.BlockSpec((None,) + w.shape[1:], lambda i, j: (j, 0, 0))
    return pl.BlockSpec((None, None) + w.shape[2:], lambda i, j: (j, e, 0, 0))


def ffn_fwd(x, g, wg, wu, wd, e=0, *, tm=512):
    T = x.shape[0]

    def body(x_ref, g_ref, wg_ref, wu_ref, wd_ref, xo_ref, h_ref, a_ref, b_ref, acc_ref):
        j = pl.program_id(1)

        @pl.when(j == 0)
        def _():
            h_ref[...] = _rms(x_ref[...], g_ref[...]).astype(MXU)
            acc_ref[...] = jnp.zeros_like(acc_ref)

        h = h_ref[...]
        a = _dot(h, wg_ref[...])
        b = _dot(h, wu_ref[...])
        a_ref[...] = a.astype(MXU)
        b_ref[...] = b.astype(MXU)
        z = (a * _sigmoid(a) * b).astype(MXU)
        acc_ref[...] += _dot(z, wd_ref[...])

        @pl.when(j == NSH - 1)
        def _():
            xo_ref[...] = x_ref[...] + 0.5 * acc_ref[...]

    return _call(
        body, name="ffn_fwd", grid=(T // tm, NSH),
        in_specs=[pl.BlockSpec((tm, D), lambda i, j: (i, 0)),
                  pl.BlockSpec((1, D), lambda i, j: (0, 0)),
                  _wspec(wg, e), _wspec(wu, e), _wspec(wd, e)],
        out_specs=[pl.BlockSpec((tm, D), lambda i, j: (i, 0)),
                   pl.BlockSpec((tm, D), lambda i, j: (i, 0)),
                   pl.BlockSpec((None, tm, FS), lambda i, j: (j, i, 0)),
                   pl.BlockSpec((None, tm, FS), lambda i, j: (j, i, 0))],
        out_shape=[_sds((T, D), F32), _sds((T, D), MXU), _sds((NSH, T, FS), MXU), _sds((NSH, T, FS), MXU)],
        scratch=[pltpu.VMEM((tm, D), F32)],
    )(x, g, wg, wu, wd)


def ffn_bwd_dx(x, g, dy, a, b, wg, wu, wd, e=0, *, tm=512):
    T = x.shape[0]

    def body(x_ref, g_ref, dy_ref, a_ref, b_ref, wg_ref, wu_ref, wd_ref,
             dx_ref, da_ref, db_ref, z_ref, dyh_ref, dg_ref, acc_ref):
        i = pl.program_id(0)
        j = pl.program_id(1)

        @pl.when(j == 0)
        def _():
            dyh_ref[...] = (0.5 * dy_ref[...]).astype(MXU)
            acc_ref[...] = jnp.zeros_like(acc_ref)

        @pl.when((i == 0) & (j == 0))
        def _():
            dg_ref[...] = jnp.zeros_like(dg_ref)

        dz = _dot_nt(dyh_ref[...], wd_ref[...])
        av = a_ref[...].astype(F32)
        bv = b_ref[...].astype(F32)
        s = _sigmoid(av)
        sa = av * s
        da = (dz * bv * (s * (1.0 + av * (1.0 - s)))).astype(MXU)
        db = (dz * sa).astype(MXU)
        da_ref[...] = da
        db_ref[...] = db
        z_ref[...] = (sa * bv).astype(MXU)
        acc_ref[...] += _dot_nt(da, wg_ref[...]) + _dot_nt(db, wu_ref[...])

        @pl.when(j == NSH - 1)
        def _():
            dxn, dg = _rms_bwd(acc_ref[...], x_ref[...], g_ref[...])
            dx_ref[...] = dy_ref[...] + dxn
            dg_ref[...] += dg

    return _call(
        body, name="ffn_bwd_dx", grid=(T // tm, NSH),
        in_specs=[pl.BlockSpec((tm, D), lambda i, j: (i, 0)),
                  pl.BlockSpec((1, D), lambda i, j: (0, 0)),
                  pl.BlockSpec((tm, D), lambda i, j: (i, 0)),
                  pl.BlockSpec((None, tm, FS), lambda i, j: (j, i, 0)),
                  pl.BlockSpec((None, tm, FS), lambda i, j: (j, i, 0)),
                  _wspec(wg, e), _wspec(wu, e), _wspec(wd, e)],
        out_specs=[pl.BlockSpec((tm, D), lambda i, j: (i, 0)),
                   pl.BlockSpec((None, tm, FS), lambda i, j: (j, i, 0)),
                   pl.BlockSpec((None, tm, FS), lambda i, j: (j, i, 0)),
                   pl.BlockSpec((None, tm, FS), lambda i, j: (j, i, 0)),
                   pl.BlockSpec((tm, D), lambda i, j: (i, 0)),
                   pl.BlockSpec((1, D), lambda i, j: (0, 0))],
        out_shape=[_sds((T, D), F32), _sds((NSH, T, FS), MXU), _sds((NSH, T, FS), MXU),
                   _sds((NSH, T, FS), MXU), _sds((T, D), MXU), _sds((1, D), F32)],
        scratch=[pltpu.VMEM((tm, D), F32)],
    )(x, g, dy, a, b, wg, wu, wd)


def mm_tn(a, b, *, name, G, M, N, a_step=0, b_step=0, tk=512, stack=None):
    T = a.shape[-2]
    if stack is not None:
        return _mm_tn_stack(a, b, name=name, G=G, M=M, N=N, tk=tk, stack=stack)

    def spec(arr, width, step):
        if arr.ndim == 3:
            return pl.BlockSpec((None, tk, width), lambda g, k: (g, k, 0))
        return pl.BlockSpec((tk, width), lambda g, k: (k, g * step))

    def body(a_ref, b_ref, o_ref):
        @pl.when(pl.program_id(1) == 0)
        def _():
            o_ref[...] = jnp.zeros_like(o_ref)

        o_ref[...] += _dot_tn(a_ref[...], b_ref[...])

    return _call(
        body, name=name, grid=(G, T // tk),
        in_specs=[spec(a, M, a_step), spec(b, N, b_step)],
        out_specs=pl.BlockSpec((None, M, N), lambda g, k: (g, 0, 0)),
        out_shape=_sds((G, M, N), F32),
    )(a, b)


def norm_mm(x, g, wb, bias, *, name, out_dtype, tm=512):
    T = x.shape[0]
    G, _, ns = wb.shape
    has_bias = bias is not None

    def body(*refs):
        if has_bias:
            x_ref, g_ref, w_ref, bias_ref, p_ref, h_ref = refs
        else:
            x_ref, g_ref, w_ref, p_ref, h_ref = refs

        @pl.when(pl.program_id(1) == 0)
        def _():
            h_ref[...] = _rms(x_ref[...], g_ref[...]).astype(MXU)

        p = _dot(h_ref[...], w_ref[...])
        if has_bias:
            p = p + bias_ref[...]
        p_ref[...] = p.astype(out_dtype)

    in_specs = [pl.BlockSpec((tm, D), lambda i, j: (i, 0)),
                pl.BlockSpec((1, D), lambda i, j: (0, 0)),
                pl.BlockSpec((None, D, ns), lambda i, j: (j, 0, 0))]
    args = [x, g, wb]
    if has_bias:
        in_specs.append(pl.BlockSpec((None, 1, ns), lambda i, j: (j, 0, 0)))
        args.append(bias)
    return _call(
        body, name=name, grid=(T // tm, G), in_specs=in_specs,
        out_specs=[pl.BlockSpec((tm, ns), lambda i, j: (i, j)),
                   pl.BlockSpec((tm, D), lambda i, j: (i, 0))],
        out_shape=[_sds((T, G * ns), out_dtype), _sds((T, D), MXU)],
    )(*args)


def mm_res(y, w, x, *, name, tm=512):
    T, K = y.shape

    def body(y_ref, w_ref, x_ref, o_ref):
        o_ref[...] = x_ref[...] + _dot(y_ref[...], w_ref[...])

    return _call(
        body, name=name, grid=(T // tm,),
        in_specs=[pl.BlockSpec((tm, K), lambda i: (i, 0)),
                  pl.BlockSpec((K, D), lambda i: (0, 0)),
                  pl.BlockSpec((tm, D), lambda i: (i, 0))],
        out_specs=pl.BlockSpec((tm, D), lambda i: (i, 0)),
        out_shape=_sds((T, D), F32),
    )(y, w, x)


def mm_nt(a, w, *, name, tm=512):
    T, K = a.shape
    N = w.shape[0]

    def body(a_ref, w_ref, o_ref, ab_ref):
        ab = a_ref[...].astype(MXU)
        ab_ref[...] = ab
        o_ref[...] = _dot_nt(ab, w_ref[...])

    return _call(
        body, name=name, grid=(T // tm,),
        in_specs=[pl.BlockSpec((tm, K), lambda i: (i, 0)),
                  pl.BlockSpec((N, K), lambda i: (0, 0))],
        out_specs=[pl.BlockSpec((tm, N), lambda i: (i, 0)),
                   pl.BlockSpec((tm, K), lambda i: (i, 0))],
        out_shape=[_sds((T, N), F32), _sds((T, K), MXU)],
    )(a, w)


def inproj_bwd(dp, wb, x, g, dres, *, name, tm=512):
    T = x.shape[0]
    G, _, ns = wb.shape

    def body(dp_ref, w_ref, x_ref, g_ref, dres_ref, dx_ref, dg_ref, acc_ref):
        i = pl.program_id(0)
        j = pl.program_id(1)

        @pl.when(j == 0)
        def _():
            acc_ref[...] = jnp.zeros_like(acc_ref)

        @pl.when((i == 0) & (j == 0))
        def _():
            dg_ref[...] = jnp.zeros_like(dg_ref)

        acc_ref[...] += _dot_nt(dp_ref[...], w_ref[...])

        @pl.when(j == G - 1)
        def _():
            dxn, dg = _rms_bwd(acc_ref[...], x_ref[...], g_ref[...])
            dx_ref[...] = dres_ref[...] + dxn
            dg_ref[...] += dg

    return _call(
        body, name=name, grid=(T // tm, G),
        in_specs=[pl.BlockSpec((tm, ns), lambda i, j: (i, j)),
                  pl.BlockSpec((None, D, ns), lambda i, j: (j, 0, 0)),
                  pl.BlockSpec((tm, D), lambda i, j: (i, 0)),
                  pl.BlockSpec((1, D), lambda i, j: (0, 0)),
                  pl.BlockSpec((tm, D), lambda i, j: (i, 0))],
        out_specs=[pl.BlockSpec((tm, D), lambda i, j: (i, 0)),
                   pl.BlockSpec((1, D), lambda i, j: (0, 0))],
        out_shape=[_sds((T, D), F32), _sds((1, D), F32)],
        scratch=[pltpu.VMEM((tm, D), F32)],
    )(dp, wb, x, g, dres)


def loss_head(x, gf, tgt, *, tm=512):
    T = x.shape[0]

    def body(x_ref, g_ref, t_ref, loss_ref, dx_ref, dg_ref):
        @pl.when(pl.program_id(0) == 0)
        def _():
            loss_ref[...] = jnp.zeros_like(loss_ref)
            dg_ref[...] = jnp.zeros_like(dg_ref)

        xv = x_ref[...]
        gv = g_ref[...]
        e = _rms(xv, gv) - t_ref[...]
        loss_ref[...] += (0.5 / D) * jnp.sum(e * e)
        dxn, dg = _rms_bwd(e * (1.0 / D), xv, gv)
        dx_ref[...] = dxn
        dg_ref[...] += dg

    return _call(
        body, name="loss_head", grid=(T // tm,),
        in_specs=[pl.BlockSpec((tm, D), lambda i: (i, 0)),
                  pl.BlockSpec((1, D), lambda i: (0, 0)),
                  pl.BlockSpec((tm, D), lambda i: (i, 0))],
        out_specs=[pl.BlockSpec((8, 128), lambda i: (0, 0)),
                   pl.BlockSpec((tm, D), lambda i: (i, 0)),
                   pl.BlockSpec((1, D), lambda i: (0, 0))],
        out_shape=[_sds((8, 128), F32), _sds((T, D), F32), _sds((1, D), F32)],
    )(x, gf, tgt)


def _glu(p):
    return p[:, :D] * _sigmoid(p[:, D:])


def _ln_stats(u):
    mu = jnp.mean(u, axis=-1, keepdims=True)
    xc = u - mu
    rstd = lax.rsqrt(jnp.mean(xc * xc, axis=-1, keepdims=True) + LN_EPS)
    return xc * rstd, rstd


def conv_fwd_core(p, dw, dwb, lng, lnb, *, S, tt=256):
    T = p.shape[0]
    nb = S // tt
    r = tt // HALO

    def body(pc_ref, pp_ref, dw_ref, dwb_ref, lng_ref, lnb_ref, u2_ref, u4_ref, ubuf):
        first = (pl.program_id(0) % nb) == 0
        ubuf[0:HALO, :] = jnp.where(first, 0.0, _glu(pp_ref[...]))
        ubuf[HALO:, :] = _glu(pc_ref[...])
        for c in range(D // 128):
            cs = slice(c * 128, (c + 1) * 128)
            acc = jnp.zeros((tt, 128), F32)
            for k in range(CONV_W):
                acc = acc + dw_ref[k:k + 1, cs] * ubuf[k + 2:k + 2 + tt, cs]
            u2_ref[:, cs] = acc + dwb_ref[:, cs]
        xh, _ = _ln_stats(u2_ref[...])
        u3 = xh * lng_ref[...] + lnb_ref[...]
        u4_ref[...] = (u3 * _sigmoid(u3)).astype(MXU)

    row = pl.BlockSpec((1, D), lambda i: (0, 0))
    return _call(
        body, name="conv_fwd_core", grid=(T // tt,),
        in_specs=[pl.BlockSpec((tt, 2 * D), lambda i: (i, 0)),
                  pl.BlockSpec((HALO, 2 * D), lambda i: (jnp.maximum(i * r - 1, 0), 0)),
                  pl.BlockSpec((HALO, D), lambda i: (0, 0)), row, row, row],
        out_specs=[pl.BlockSpec((tt, D), lambda i: (i, 0)), pl.BlockSpec((tt, D), lambda i: (i, 0))],
        out_shape=[_sds((T, D), F32), _sds((T, D), MXU)],
        scratch=[pltpu.VMEM((tt + HALO, D), F32)],
    )(p, p, dw, dwb, lng, lnb)


def conv_bwd_rows(dy, wout, u2, lng, lnb, *, tm=512):
    T = dy.shape[0]

    def body(dy_ref, w_ref, u2_ref, lng_ref, lnb_ref, du2_ref, dyb_ref, dlng_ref, dlnb_ref, ddwb_ref):
        @pl.when(pl.program_id(0) == 0)
        def _():
            dlng_ref[...] = jnp.zeros_like(dlng_ref)
            dlnb_ref[...] = jnp.zeros_like(dlnb_ref)
            ddwb_ref[...] = jnp.zeros_like(ddwb_ref)

        dyb = dy_ref[...].astype(MXU)
        dyb_ref[...] = dyb
        du4 = _dot_nt(dyb, w_ref[...])
        xh, rstd = _ln_stats(u2_ref[...])
        lng_v = lng_ref[...]
        u3 = xh * lng_v + lnb_ref[...]
        s = _sigmoid(u3)
        du3 = du4 * (s * (1.0 + u3 * (1.0 - s)))
        dlng_ref[...] += jnp.sum(du3 * xh, axis=0, keepdims=True)
        dlnb_ref[...] += jnp.sum(du3, axis=0, keepdims=True)
        dxh = du3 * lng_v
        du2 = rstd * (dxh - jnp.mean(dxh, axis=-1, keepdims=True)
                      - xh * jnp.mean(dxh * xh, axis=-1, keepdims=True))
        du2_ref[...] = du2
        ddwb_ref[...] += jnp.sum(du2, axis=0, keepdims=True)

    row = pl.BlockSpec((1, D), lambda i: (0, 0))
    blk = pl.BlockSpec((tm, D), lambda i: (i, 0))
    return _call(
        body, name="conv_bwd_rows", grid=(T // tm,),
        in_specs=[blk, pl.BlockSpec((D, D), lambda i: (0, 0)), blk, row, row],
        out_specs=[blk, blk, row, row, row],
        out_shape=[_sds((T, D), F32), _sds((T, D), MXU), _sds((1, D), F32), _sds((1, D), F32), _sds((1, D), F32)],
    )(dy, wout, u2, lng, lnb)


def conv_bwd_core(du2, p, dw, *, S, tt=256):
    T = p.shape[0]
    nb = S // tt
    r = tt // HALO
    last_halo = T // HALO - 1

    def body(dc_ref, dn_ref, pc_ref, pp_ref, dw_ref, dp_ref, dbin_ref, ddw_ref, ubuf, dbuf):
        i = pl.program_id(0)

        @pl.when(i == 0)
        def _():
            dbin_ref[...] = jnp.zeros_like(dbin_ref)
            ddw_ref[...] = jnp.zeros_like(ddw_ref)

        first = (i % nb) == 0
        last = (i % nb) == nb - 1
        ubuf[0:HALO, :] = jnp.where(first, 0.0, _glu(pp_ref[...]))
        ubuf[HALO:, :] = _glu(pc_ref[...])
        dbuf[0:tt, :] = dc_ref[...]
        dbuf[tt:, :] = jnp.where(last, 0.0, dn_ref[...])
        pc = pc_ref[...]
        for c in range(D // 128):
            cs = slice(c * 128, (c + 1) * 128)
            dcur = dbuf[0:tt, cs]
            du = jnp.zeros((tt, 128), F32)
            for k in range(CONV_W):
                ddw_ref[k:k + 1, cs] += jnp.sum(dcur * ubuf[k + 2:k + 2 + tt, cs], axis=0, keepdims=True)
                du = du + dw_ref[k:k + 1, cs] * dbuf[CONV_W - 1 - k:CONV_W - 1 - k + tt, cs]
            a = pc[:, c * 128:(c + 1) * 128]
            sb = _sigmoid(pc[:, D + c * 128:D + (c + 1) * 128])
            da = du * sb
            db = du * a * sb * (1.0 - sb)
            dp_ref[:, cs] = da.astype(MXU)
            dp_ref[:, D + c * 128:D + (c + 1) * 128] = db.astype(MXU)
            dbin_ref[:, cs] += jnp.sum(da, axis=0, keepdims=True)
            dbin_ref[:, D + c * 128:D + (c + 1) * 128] += jnp.sum(db, axis=0, keepdims=True)

    return _call(
        body, name="conv_bwd_core", grid=(T // tt,),
        in_specs=[pl.BlockSpec((tt, D), lambda i: (i, 0)),
                  pl.BlockSpec((HALO, D), lambda i: (jnp.minimum((i + 1) * r, last_halo), 0)),
                  pl.BlockSpec((tt, 2 * D), lambda i: (i, 0)),
                  pl.BlockSpec((HALO, 2 * D), lambda i: (jnp.maximum(i * r - 1, 0), 0)),
                  pl.BlockSpec((HALO, D), lambda i: (0, 0))],
        out_specs=[pl.BlockSpec((tt, 2 * D), lambda i: (i, 0)),
                   pl.BlockSpec((1, 2 * D), lambda i: (0, 0)),
                   pl.BlockSpec((HALO, D), lambda i: (0, 0))],
        out_shape=[_sds((T, 2 * D), MXU), _sds((1, 2 * D), F32), _sds((HALO, D), F32)],
        scratch=[pltpu.VMEM((tt + HALO, D), F32), pltpu.VMEM((tt + HALO, D), F32)],
    )(du2, du2, p, p, dw)


PH = 16


def _pool_cnt(i, nb, tt, win):
    pos = (i % nb) * tt + lax.broadcasted_iota(jnp.int32, (tt, 1), 0)
    return jnp.minimum(pos + 1, win).astype(F32)


def pool_fwd(x, g, wp, scale, *, S, tt=256):
    T = x.shape[0]
    nb = S // tt
    r = tt // PH

    def body(xc_ref, xp_ref, g_ref, wp_ref, sc_ref, xo_ref, m_ref, hbuf):
        i = pl.program_id(0)
        first = (i % nb) == 0
        gv = g_ref[...]
        hbuf[0:PH, :] = jnp.where(first, 0.0, _rms(xp_ref[...], gv))
        xc = xc_ref[...]
        hbuf[PH:, :] = _rms(xc, gv)
        for gi, win in enumerate(POOL_WIN):
            gs = slice(gi * POOL_G, (gi + 1) * POOL_G)
            acc = hbuf[PH:PH + tt, gs]
            for j in range(1, win):
                acc = acc + hbuf[PH - j:PH - j + tt, gs]
            m = (acc / _pool_cnt(i, nb, tt, win) - hbuf[PH:PH + tt, gs]).astype(MXU)
            m_ref[:, gs] = m
            xo_ref[:, gs] = xc[:, gs] + _dot(m, wp_ref[gi]) * sc_ref[:, gs]

    row = pl.BlockSpec((1, D), lambda i: (0, 0))
    blk = pl.BlockSpec((tt, D), lambda i: (i, 0))
    return _call(
        body, name="pool_fwd", grid=(T // tt,),
        in_specs=[blk, pl.BlockSpec((PH, D), lambda i: (jnp.maximum(i * r - 1, 0), 0)), row,
                  pl.BlockSpec((len(POOL_WIN), POOL_G, POOL_G), lambda i: (0, 0, 0)), row],
        out_specs=[blk, blk],
        out_shape=[_sds((T, D), F32), _sds((T, D), MXU)],
        scratch=[pltpu.VMEM((tt + PH, D), F32)],
    )(x, x, g, wp, scale)


def pool_bwd_rows(dy, m, wp, scale, *, S, tt=256):
    T = dy.shape[0]
    nb = S // tt

    def body(dy_ref, m_ref, wp_ref, sc_ref, dmc_ref, dyp_ref, dsc_ref):
        i = pl.program_id(0)

        @pl.when(i == 0)
        def _():
            dsc_ref[...] = jnp.zeros_like(dsc_ref)

        for gi, win in enumerate(POOL_WIN):
            gs = slice(gi * POOL_G, (gi + 1) * POOL_G)
            dyg = dy_ref[:, gs]
            w = wp_ref[gi]
            dsc_ref[:, gs] += jnp.sum(dyg * _dot(m_ref[:, gs], w), axis=0, keepdims=True)
            dyp = (dyg * sc_ref[:, gs]).astype(MXU)
            dyp_ref[:, gs] = dyp
            dmc_ref[:, gs] = _dot_nt(dyp, w) / _pool_cnt(i, nb, tt, win)

    row = pl.BlockSpec((1, D), lambda i: (0, 0))
    blk = pl.BlockSpec((tt, D), lambda i: (i, 0))
    return _call(
        body, name="pool_bwd_rows", grid=(T // tt,),
        in_specs=[blk, blk, pl.BlockSpec((len(POOL_WIN), POOL_G, POOL_G), lambda i: (0, 0, 0)), row],
        out_specs=[blk, blk, row],
        out_shape=[_sds((T, D), F32), _sds((T, D), MXU), _sds((1, D), F32)],
    )(dy, m, wp, scale)


def pool_bwd_core(dmc, x, g, dres, *, S, tt=256):
    T = x.shape[0]
    nb = S // tt
    r = tt // PH
    last_halo = T // PH - 1

    def body(dc_ref, dn_ref, x_ref, g_ref, dres_ref, dx_ref, dg_ref, dbuf, dh_buf):
        i = pl.program_id(0)

        @pl.when(i == 0)
        def _():
            dg_ref[...] = jnp.zeros_like(dg_ref)

        last = (i % nb) == nb - 1
        dbuf[0:tt, :] = dc_ref[...]
        dbuf[tt:, :] = jnp.where(last, 0.0, dn_ref[...])
        for gi, win in enumerate(POOL_WIN):
            gs = slice(gi * POOL_G, (gi + 1) * POOL_G)
            cur = dbuf[0:tt, gs]
            acc = cur
            for j in range(1, win):
                acc = acc + dbuf[j:j + tt, gs]
            dh_buf[:, gs] = acc - cur * _pool_cnt(i, nb, tt, win)
        dxn, dg = _rms_bwd(dh_buf[...], x_ref[...], g_ref[...])
        dx_ref[...] = dres_ref[...] + dxn
        dg_ref[...] += dg

    row = pl.BlockSpec((1, D), lambda i: (0, 0))
    blk = pl.BlockSpec((tt, D), lambda i: (i, 0))
    return _call(
        body, name="pool_bwd_core", grid=(T // tt,),
        in_specs=[blk, pl.BlockSpec((PH, D), lambda i: (jnp.minimum((i + 1) * r, last_halo), 0)), blk, row, blk],
        out_specs=[blk, row],
        out_shape=[_sds((T, D), F32), _sds((1, D), F32)],
        scratch=[pltpu.VMEM((tt + PH, D), F32), pltpu.VMEM((tt, D), F32)],
    )(dmc, dmc, x, g, dres)


NEG = -1e30


def _tri(n, upper=False):
    r = lax.broadcasted_iota(jnp.int32, (n, n), 0)
    c = lax.broadcasted_iota(jnp.int32, (n, n), 1)
    return (r <= c if upper else r >= c).astype(F32)


def _dot_hi(a, b):
    return jnp.dot(a, b, preferred_element_type=F32, precision=lax.Precision.HIGHEST)


def _log_sigmoid(z):
    return jnp.minimum(z, 0.0) - jnp.log(1.0 + jnp.exp(-jnp.abs(z)))


def fox_cum(fl, bf, *, S, tt=256):
    T = fl.shape[0]
    nb = S // tt

    def body(fl_ref, bf_ref, c_ref, ct_ref, carry):
        i = pl.program_id(0)

        @pl.when((i % nb) == 0)
        def _():
            carry[...] = jnp.zeros_like(carry)

        lf = _log_sigmoid(fl_ref[...] + bf_ref[...])
        c = _dot_hi(_tri(tt), lf) + carry[...]
        c_ref[...] = c
        carry[...] = c[tt - 1:tt, :]
        ct_ref[...] = c.T[0:FOX_H, :]

    return _call(
        body, name="fox_cum", grid=(T // tt,),
        in_specs=[pl.BlockSpec((tt, 128), lambda i: (i, 0)), pl.BlockSpec((1, 128), lambda i: (0, 0))],
        out_specs=[pl.BlockSpec((tt, 128), lambda i: (i, 0)),
                   pl.BlockSpec((None, FOX_H, tt), lambda i: (i // nb, 0, i % nb))],
        out_shape=[_sds((T, 128), F32), _sds((T // S, FOX_H, S), F32)],
        scratch=[pltpu.VMEM((1, 128), F32)],
    )(fl, bf)


def _fox_logits(q_ref, k_ref, c_ref, ct_ref, h, diag, tq):
    hs = slice(h * FOX_DH, (h + 1) * FOX_DH)
    s = _dot_nt(q_ref[:, hs], k_ref[:, hs]) * (FOX_DH ** -0.5) + (c_ref[:, h:h + 1] - ct_ref[h:h + 1, :])
    if diag:
        r = lax.broadcasted_iota(jnp.int32, (tq, tq), 0)
        c = lax.broadcasted_iota(jnp.int32, (tq, tq), 1)
        s = jnp.where(r >= c, s, NEG)
    return s


def _fox_specs(tq, nq, q_of, k_of):
    qrow = lambda col: pl.BlockSpec((tq, D), lambda b, i, j: (b * nq + q_of(i, j), col))
    krow = lambda col: pl.BlockSpec((tq, D), lambda b, i, j: (b * nq + k_of(i, j), col))
    qvec = pl.BlockSpec((tq, 128), lambda b, i, j: (b * nq + q_of(i, j), 0))
    kvec = pl.BlockSpec((tq, 128), lambda b, i, j: (b * nq + k_of(i, j), 0))
    ct = pl.BlockSpec((None, FOX_H, tq), lambda b, i, j: (b, 0, k_of(i, j)))
    return qrow, krow, qvec, kvec, ct


def fox_fwd(qkv, c, ct, *, S, tq=256):
    T = qkv.shape[0]
    nq = S // tq

    def body(q_ref, k_ref, v_ref, c_ref, ct_ref, o_ref, o32_ref, lse_ref, m_sc, l_sc, acc, acc_lo):
        qi = pl.program_id(1)
        ki = pl.program_id(2)

        @pl.when(ki == 0)
        def _():
            m_sc[...] = jnp.full_like(m_sc, NEG)
            l_sc[...] = jnp.zeros_like(l_sc)
            acc[...] = jnp.zeros_like(acc)
            acc_lo[...] = jnp.zeros_like(acc_lo)

        def step(diag):
            for h in range(FOX_H):
                hs = slice(h * FOX_DH, (h + 1) * FOX_DH)
                s = _fox_logits(q_ref, k_ref, c_ref, ct_ref, h, diag, tq)
                m_prev = m_sc[:, h:h + 1]
                m_new = jnp.maximum(m_prev, jnp.max(s, axis=-1, keepdims=True))
                alpha = jnp.exp(m_prev - m_new)
                p = jnp.exp(s - m_new)
                l_sc[:, h:h + 1] = alpha * l_sc[:, h:h + 1] + jnp.sum(p, axis=-1, keepdims=True)
                hi, lo = _split(p)
                acc[:, hs] = alpha * acc[:, hs] + _dot(hi, v_ref[:, hs])
                acc_lo[:, hs] = alpha * acc_lo[:, hs] + _dot(lo, v_ref[:, hs])
                m_sc[:, h:h + 1] = m_new

        @pl.when(ki < qi)
        def _():
            step(False)

        @pl.when(ki == qi)
        def _():
            step(True)
            for h in range(FOX_H):
                hs = slice(h * FOX_DH, (h + 1) * FOX_DH)
                o_ref[:, hs] = (acc[:, hs] / l_sc[:, h:h + 1]).astype(MXU)
                o32_ref[:, hs] = (acc[:, hs] + acc_lo[:, hs]) / l_sc[:, h:h + 1]
            lse_ref[...] = m_sc[...] + jnp.log(jnp.maximum(l_sc[...], 1e-37))

    qrow, krow, qvec, kvec, ctspec = _fox_specs(tq, nq, lambda i, j: i, lambda i, j: jnp.minimum(i, j))
    return _call(
        body, name="fox_fwd", grid=(T // S, nq, nq),
        in_specs=[qrow(0), krow(1), krow(2), qvec, ctspec],
        out_specs=[qrow(0), qrow(0), qvec],
        out_shape=[_sds((T, D), MXU), _sds((T, D), F32), _sds((T, 128), F32)],
        scratch=[pltpu.VMEM((tq, 128), F32), pltpu.VMEM((tq, 128), F32), pltpu.VMEM((tq, D), F32),
                 pltpu.VMEM((tq, D), F32)],
    )(qkv, qkv, qkv, c, ct)


def fox_bwd_dq(qkv, do, o, lse, c, ct, *, S, tq=256):
    T = qkv.shape[0]
    nq = S // tq

    def body(q_ref, k_ref, v_ref, do32_ref, o_ref, lse_ref, c_ref, ct_ref, dq_ref, dl_ref, do_ref, acc):
        qi = pl.program_id(1)
        ki = pl.program_id(2)

        @pl.when(ki == 0)
        def _():
            acc[...] = jnp.zeros_like(acc)
            dl_ref[...] = jnp.zeros_like(dl_ref)
            do_ref[...] = do32_ref[...].astype(MXU)
            for h in range(FOX_H):
                hs = slice(h * FOX_DH, (h + 1) * FOX_DH)
                dl_ref[:, h:h + 1] = jnp.sum(do_ref[:, hs].astype(F32) * o_ref[:, hs], axis=-1, keepdims=True)

        def step(diag):
            for h in range(FOX_H):
                hs = slice(h * FOX_DH, (h + 1) * FOX_DH)
                s = _fox_logits(q_ref, k_ref, c_ref, ct_ref, h, diag, tq)
                p = jnp.exp(s - lse_ref[:, h:h + 1])
                dp = _dot_nt(do_ref[:, hs], v_ref[:, hs])
                hi, lo = _split(p * (dp - dl_ref[:, h:h + 1]))
                acc[:, hs] += _dot(hi, k_ref[:, hs]) + _dot(lo, k_ref[:, hs])

        @pl.when(ki < qi)
        def _():
            step(False)

        @pl.when(ki == qi)
        def _():
            step(True)
            dq_ref[...] = (acc[...] * (FOX_DH ** -0.5)).astype(MXU)

    qrow, krow, qvec, kvec, ctspec = _fox_specs(tq, nq, lambda i, j: i, lambda i, j: jnp.minimum(i, j))
    orow = pl.BlockSpec((tq, D), lambda b, i, j: (b * nq + i, 0))
    return _call(
        body, name="fox_bwd_dq", grid=(T // S, nq, nq),
        in_specs=[qrow(0), krow(1), krow(2), orow, orow, qvec, qvec, ctspec],
        out_specs=[orow, qvec, orow],
        out_shape=[_sds((T, D), MXU), _sds((T, 128), F32), _sds((T, D), MXU)],
        scratch=[pltpu.VMEM((tq, D), F32)],
    )(qkv, qkv, qkv, do, o, lse, c, ct)


def fox_bwd_dkv(qkv, do, lse, delta, c, ct, *, S, tq=256):
    T = qkv.shape[0]
    nq = S // tq

    def body(q_ref, k_ref, v_ref, do_ref, lse_ref, dl_ref, c_ref, ct_ref, dk_ref, dv_ref, dck_ref, dk_acc, dv_acc):
        ki = pl.program_id(1)
        qi = pl.program_id(2)

        @pl.when(qi == 0)
        def _():
            dk_acc[...] = jnp.zeros_like(dk_acc)
            dv_acc[...] = jnp.zeros_like(dv_acc)
            dck_ref[...] = jnp.zeros_like(dck_ref)

        def step(diag):
            ones = jnp.ones((tq, 128), MXU)
            for h in range(FOX_H):
                hs = slice(h * FOX_DH, (h + 1) * FOX_DH)
                s = _fox_logits(q_ref, k_ref, c_ref, ct_ref, h, diag, tq)
                p = jnp.exp(s - lse_ref[:, h:h + 1])
                doh = do_ref[:, hs]
                dv_acc[:, hs] += _dot_tn(p.astype(MXU), doh)
                dp = _dot_nt(doh, v_ref[:, hs])
                hi, lo = _split(p * (dp - dl_ref[:, h:h + 1]))
                dk_acc[:, hs] += _dot_tn(hi, q_ref[:, hs]) + _dot_tn(lo, q_ref[:, hs])
                dck_ref[:, h:h + 1] += (_dot_tn(hi, ones) + _dot_tn(lo, ones))[:, 0:1]

        @pl.when(qi > ki)
        def _():
            step(False)

        @pl.when(qi == ki)
        def _():
            step(True)

        @pl.when(qi == nq - 1)
        def _():
            dk_ref[...] = (dk_acc[...] * (FOX_DH ** -0.5)).astype(MXU)
            dv_ref[...] = dv_acc[...].astype(MXU)

    qrow, krow, qvec, kvec, ctspec = _fox_specs(tq, nq, lambda i, j: jnp.maximum(i, j), lambda i, j: i)
    qo = pl.BlockSpec((tq, D), lambda b, i, j: (b * nq + jnp.maximum(i, j), 0))
    ko = pl.BlockSpec((tq, D), lambda b, i, j: (b * nq + i, 0))
    return _call(
        body, name="fox_bwd_dkv", grid=(T // S, nq, nq),
        in_specs=[qrow(0), krow(1), krow(2), qo, qvec, qvec, qvec, ctspec],
        out_specs=[ko, ko, kvec],
        out_shape=[_sds((T, D), MXU), _sds((T, D), MXU), _sds((T, 128), F32)],
        scratch=[pltpu.VMEM((tq, D), F32), pltpu.VMEM((tq, D), F32)],
    )(qkv, qkv, qkv, do, lse, delta, c, ct)


HW = 128
COL_ONE = FOX_DH + 3
COL_LSE = FOX_DH + 6


def _parts(x):
    hi = x.astype(MXU).astype(F32)
    mid = (x - hi).astype(MXU).astype(F32)
    lo = (x - hi - mid).astype(MXU).astype(F32)
    return [hi, mid, lo]


def _aug(n, cols):
    lane = lax.broadcasted_iota(jnp.int32, (n, HW - FOX_DH), 1)
    out = jnp.zeros((n, HW - FOX_DH), F32)
    for i, cval in enumerate(cols):
        out = jnp.where(lane == i, cval, out)
    return out


def fox_prep(p, c, *, tt=256):
    T = p.shape[0]

    def body(q_ref, k_ref, v_ref, c_ref, qa_ref, ka_ref, va_ref):
        ones = [1.0, 1.0, 1.0]
        for h in range(FOX_H):
            hs = slice(h * FOX_DH, (h + 1) * FOX_DH)
            lo, mid = h * HW, h * HW + FOX_DH
            cp = _parts(c_ref[:, h:h + 1])
            qa_ref[:, lo:mid] = (q_ref[:, hs].astype(F32) * (FOX_DH ** -0.5)).astype(MXU)
            qa_ref[:, mid:lo + HW] = _aug(tt, cp + ones).astype(MXU)
            ka_ref[:, lo:mid] = k_ref[:, hs]
            ka_ref[:, mid:lo + HW] = _aug(tt, ones + [-x for x in cp] + ones).astype(MXU)
            va_ref[:, lo:mid] = v_ref[:, hs]
            va_ref[:, mid:lo + HW] = _aug(tt, ones).astype(MXU)

    wide = pl.BlockSpec((tt, FOX_H * HW), lambda i: (i, 0))
    col = lambda k: pl.BlockSpec((tt, D), lambda i: (i, k))
    return _call(body, name="fox_prep", grid=(T // tt,),
                 in_specs=[col(0), col(1), col(2), pl.BlockSpec((tt, 128), lambda i: (i, 0))],
                 out_specs=[wide, wide, wide], out_shape=[_sds((T, FOX_H * HW), MXU)] * 3)(p, p, p, c)


def _causal(x, fill):
    r = lax.broadcasted_iota(jnp.int32, x.shape, 0)
    c = lax.broadcasted_iota(jnp.int32, x.shape, 1)
    return jnp.where(r >= c, x, fill)


def _wide_specs(tq, nq, q_of, k_of):
    qs = pl.BlockSpec((tq, FOX_H * HW), lambda b, i, j: (b * nq + q_of(i, j), 0))
    ks = pl.BlockSpec((tq, FOX_H * HW), lambda b, i, j: (b * nq + k_of(i, j), 0))
    return qs, ks


def fox2_fwd(qa, ka, va, *, S, tq=256):
    T = qa.shape[0]
    nq = S // tq
    rep = tq // 128 if tq >= 128 else 1

    def body(q_ref, k_ref, v_ref, o_ref, o32_ref, lse_ref, m_sc, acc, acc_lo):
        qi = pl.program_id(1)
        ki = pl.program_id(2)

        @pl.when(ki == 0)
        def _():
            m_sc[...] = jnp.full_like(m_sc, NEG)
            acc[...] = jnp.zeros_like(acc)
            acc_lo[...] = jnp.zeros_like(acc_lo)

        def step(diag):
            for h in range(FOX_H):
                ws = slice(h * HW, (h + 1) * HW)
                s = _dot_nt(q_ref[:, ws], k_ref[:, ws])
                if diag:
                    s = _causal(s, NEG)
                m_prev = m_sc[h]
                m_new = jnp.maximum(m_prev, jnp.max(s, axis=-1, keepdims=True))
                alpha = jnp.exp(m_prev - m_new)
                hi, lo = _split(jnp.exp(s - jnp.tile(m_new, (1, tq // 128))))
                acc[h] = alpha * acc[h] + _dot(hi, v_ref[:, ws])
                acc_lo[h] = alpha * acc_lo[h] + _dot(lo, v_ref[:, ws])
                m_sc[h] = m_new

        @pl.when(ki < qi)
        def _():
            step(False)

        @pl.when(ki == qi)
        def _():
            step(True)
            for h in range(FOX_H):
                hs = slice(h * FOX_DH, (h + 1) * FOX_DH)
                full = acc[h] + acc_lo[h]
                l = full[:, FOX_DH:FOX_DH + 1]
                o_ref[:, hs] = (acc[h][:, :FOX_DH] / l).astype(MXU)
                o32_ref[:, hs] = full[:, :FOX_DH] / l
                lse_ref[:, h:h + 1] = m_sc[h][:, 0:1] + jnp.log(l)

    qs, ks = _wide_specs(tq, nq, lambda i, j: i, lambda i, j: jnp.minimum(i, j))
    orow = pl.BlockSpec((tq, D), lambda b, i, j: (b * nq + i, 0))
    return _call(
        body, name="fox_fwd", grid=(T // S, nq, nq),
        in_specs=[qs, ks, ks],
        out_specs=[orow, orow, pl.BlockSpec((tq, 128), lambda b, i, j: (b * nq + i, 0))],
        out_shape=[_sds((T, D), MXU), _sds((T, D), F32), _sds((T, 128), F32)],
        scratch=[pltpu.VMEM((FOX_H, tq, 128), F32), pltpu.VMEM((FOX_H, tq, HW), F32),
                 pltpu.VMEM((FOX_H, tq, HW), F32)],
    )(qa, ka, va)


def fox2_prep_bwd(do, o32, lse, qa, *, tt=256):
    T = do.shape[0]

    def body(do_ref, o_ref, lse_ref, qa_ref, qb_ref, da_ref):
        lane = lax.broadcasted_iota(jnp.int32, (tt, HW), 1)
        dob = do_ref[...].astype(MXU)
        for h in range(FOX_H):
            hs = slice(h * FOX_DH, (h + 1) * FOX_DH)
            ws = slice(h * HW, (h + 1) * HW)
            doh = dob[:, hs]
            delta = jnp.sum(doh.astype(F32) * o_ref[:, hs], axis=-1, keepdims=True)
            da_ref[:, h * HW:h * HW + FOX_DH] = doh
            da_ref[:, h * HW + FOX_DH:(h + 1) * HW] = _aug(tt, [-x for x in _parts(delta)]).astype(MXU)
            tile = qa_ref[:, ws]
            for i, part in enumerate(_parts(lse_ref[:, h:h + 1])):
                tile = jnp.where(lane == COL_LSE + i, (-part).astype(MXU), tile)
            qb_ref[:, ws] = tile

    wide = pl.BlockSpec((tt, FOX_H * HW), lambda i: (i, 0))
    blk = pl.BlockSpec((tt, D), lambda i: (i, 0))
    return _call(body, name="fox_prep_bwd", grid=(T // tt,),
                 in_specs=[blk, blk, pl.BlockSpec((tt, 128), lambda i: (i, 0)), wide],
                 out_specs=[wide, wide],
                 out_shape=[_sds((T, FOX_H * HW), MXU), _sds((T, FOX_H * HW), MXU)],
                 )(do, o32, lse, qa)


def fox2_dq(qb, ka, va, da, *, S, tq=256):
    T = qb.shape[0]
    nq = S // tq

    def body(q_ref, k_ref, v_ref, d_ref, dq_ref, acc):
        qi = pl.program_id(1)
        ki = pl.program_id(2)

        @pl.when(ki == 0)
        def _():
            acc[...] = jnp.zeros_like(acc)

        def step(diag):
            for h in range(FOX_H):
                ws = slice(h * HW, (h + 1) * HW)
                kh = k_ref[:, ws]
                p = jnp.exp(_dot_nt(q_ref[:, ws], kh))
                if diag:
                    p = _causal(p, 0.0)
                hi, lo = _split(p * _dot_nt(d_ref[:, ws], v_ref[:, ws]))
                acc[h] += _dot(hi, kh) + _dot(lo, kh)

        @pl.when(ki < qi)
        def _():
            step(False)

        @pl.when(ki == qi)
        def _():
            step(True)
            for h in range(FOX_H):
                dq_ref[:, h * FOX_DH:(h + 1) * FOX_DH] = (acc[h][:, :FOX_DH] * (FOX_DH ** -0.5)).astype(MXU)

    qs, ks = _wide_specs(tq, nq, lambda i, j: i, lambda i, j: jnp.minimum(i, j))
    return _call(
        body, name="fox_bwd_dq", grid=(T // S, nq, nq),
        in_specs=[qs, ks, ks, qs],
        out_specs=pl.BlockSpec((tq, D), lambda b, i, j: (b * nq + i, 0)),
        out_shape=_sds((T, D), MXU),
        scratch=[pltpu.VMEM((FOX_H, tq, HW), F32)],
    )(qb, ka, va, da)


def fox2_dkv(qb, ka, va, da, *, S, tq=256):
    T = qb.shape[0]
    nq = S // tq

    def body(q_ref, k_ref, v_ref, d_ref, dk_ref, dv_ref, dck_ref, dk_acc, dv_acc):
        ki = pl.program_id(1)
        qi = pl.program_id(2)

        @pl.when(qi == 0)
        def _():
            dk_acc[...] = jnp.zeros_like(dk_acc)
            dv_acc[...] = jnp.zeros_like(dv_acc)

        def step(diag):
            for h in range(FOX_H):
                ws = slice(h * HW, (h + 1) * HW)
                qh = q_ref[:, ws]
                dh = d_ref[:, ws]
                p = jnp.exp(_dot_nt(qh, k_ref[:, ws]))
                if diag:
                    p = _causal(p, 0.0)
                dv_acc[h] += _dot_tn(p.astype(MXU), dh)
                hi, lo = _split(p * _dot_nt(dh, v_ref[:, ws]))
                dk_acc[h] += _dot_tn(hi, qh) + _dot_tn(lo, qh)

        @pl.when(qi > ki)
        def _():
            step(False)

        @pl.when(qi == ki)
        def _():
            step(True)

        @pl.when(qi == nq - 1)
        def _():
            dck_ref[...] = jnp.zeros_like(dck_ref)
            for h in range(FOX_H):
                hs = slice(h * FOX_DH, (h + 1) * FOX_DH)
                dk_ref[:, hs] = dk_acc[h][:, :FOX_DH].astype(MXU)
                dv_ref[:, hs] = dv_acc[h][:, :FOX_DH].astype(MXU)
                dck_ref[:, h:h + 1] = dk_acc[h][:, COL_ONE:COL_ONE + 1]

    qs, ks = _wide_specs(tq, nq, lambda i, j: jnp.maximum(i, j), lambda i, j: i)
    ko = pl.BlockSpec((tq, D), lambda b, i, j: (b * nq + i, 0))
    return _call(
        body, name="fox_bwd_dkv", grid=(T // S, nq, nq),
        in_specs=[qs, ks, ks, qs],
        out_specs=[ko, ko, pl.BlockSpec((tq, 128), lambda b, i, j: (b * nq + i, 0))],
        out_shape=[_sds((T, D), MXU), _sds((T, D), MXU), _sds((T, 128), F32)],
        scratch=[pltpu.VMEM((FOX_H, tq, HW), F32), pltpu.VMEM((FOX_H, tq, HW), F32)],
    )(qb, ka, va, da)


def fox_fin(dck, fl, bf, *, S, tt=256):
    T = fl.shape[0]
    nb = S // tt
    nblk = T // tt

    def body(dck_ref, fl_ref, bf_ref, dfl_ref, dbf_ref, carry):
        i = pl.program_id(0)

        @pl.when(i == 0)
        def _():
            dbf_ref[...] = jnp.zeros_like(dbf_ref)

        @pl.when((i % nb) == 0)
        def _():
            carry[...] = jnp.zeros_like(carry)

        lane = lax.broadcasted_iota(jnp.int32, (tt, 128), 1)
        dc = jnp.where(lane < FOX_H, -dck_ref[...], 0.0)
        dlf = _dot_hi(_tri(tt, upper=True), dc) + carry[...]
        carry[...] = dlf[0:1, :]
        dfl = dlf * _sigmoid(-(fl_ref[...] + bf_ref[...]))
        dfl_ref[...] = dfl.astype(MXU)
        dbf_ref[...] += jnp.sum(dfl, axis=0, keepdims=True)

    rev = pl.BlockSpec((tt, 128), lambda i: (nblk - 1 - i, 0))
    row = pl.BlockSpec((1, 128), lambda i: (0, 0))
    return _call(
        body, name="fox_fin", grid=(nblk,),
        in_specs=[rev, rev, row],
        out_specs=[rev, row],
        out_shape=[_sds((T, 128), MXU), _sds((1, 128), F32)],
        scratch=[pltpu.VMEM((1, 128), F32)],
    )(dck, fl, bf)


def lb_fwd(logits):
    def body(l_ref, lb_ref):
        lv = l_ref[...]
        e = jnp.exp(lv - jnp.max(lv, axis=0, keepdims=True))
        p = e / jnp.sum(e, axis=0, keepdims=True)
        lb_ref[...] = p[1:2, :] + p[2:3, :]

    return _call(body, name="lb_fwd", grid=(1,),
                 in_specs=[pl.BlockSpec((DEPTH, D), lambda i: (0, 0))],
                 out_specs=pl.BlockSpec((1, D), lambda i: (0, 0)),
                 out_shape=_sds((1, D), F32))(logits)


def lb_bwd(logits, dlb):
    def body(l_ref, d_ref, o_ref):
        lv = l_ref[...]
        e = jnp.exp(lv - jnp.max(lv, axis=0, keepdims=True))
        p = e / jnp.sum(e, axis=0, keepdims=True)
        lb = p[1:2, :] + p[2:3, :]
        row = lax.broadcasted_iota(jnp.int32, (DEPTH, D), 0)
        sel = ((row == 1) | (row == 2)).astype(F32)
        o_ref[...] = p * (sel - lb) * d_ref[...]

    return _call(body, name="lb_bwd", grid=(1,),
                 in_specs=[pl.BlockSpec((DEPTH, D), lambda i: (0, 0)), pl.BlockSpec((1, D), lambda i: (0, 0))],
                 out_specs=pl.BlockSpec((DEPTH, D), lambda i: (0, 0)),
                 out_shape=_sds((DEPTH, D), F32))(logits, dlb)


def _hgrn_gates(qr, fr, lb):
    sg = _sigmoid(fr)
    sneg = _sigmoid(-fr)
    f = lb + (1.0 - lb) * sg
    kk = (1.0 - lb) * sneg
    G = _dot_hi(_tri(HG_C), jnp.log(f))
    eG = jnp.exp(G)
    einv = jnp.exp(-G)
    elast = jnp.exp(G[HG_C - 1:HG_C, :] - G)
    q = qr * _sigmoid(qr)
    return dict(q=q, kk=kk, f=f, sg=sg, sneg=sneg, eG=eG, einv=einv, elast=elast,
                qg=q * eG, kinv=kk * einv, khat=kk * elast, glast=jnp.exp(G[HG_C - 1:HG_C, :]))


def _tril_mask(x):
    r = lax.broadcasted_iota(jnp.int32, x.shape, 0)
    c = lax.broadcasted_iota(jnp.int32, x.shape, 1)
    return jnp.where(r >= c, x, 0.0)


def hgrn_fwd(p, lb, ng, *, S, R=128):
    T = p.shape[0]
    nr = S // R
    ncr = R // HG_C

    def body(q_ref, f_ref, v_ref, gt_ref, lb_ref, ng_ref, y_ref, o_ref, st_ref, st):
        @pl.when(pl.program_id(1) == 0)
        def _():
            st[...] = jnp.zeros_like(st)

        lbv = lb_ref[...]
        for ch in range(ncr):
            rows = slice(ch * HG_C, (ch + 1) * HG_C)
            gt = _hgrn_gates(q_ref[rows, :], f_ref[rows, :], lbv)
            for h in range(HG_H):
                hs = slice(h * HG_DK, (h + 1) * HG_DK)
                sp = st[h]
                st_ref[ch, h] = sp
                qg = gt["qg"][:, hs].astype(MXU)
                vh = v_ref[rows, hs].astype(MXU)
                A = _tril_mask(_dot_nt(qg, gt["kinv"][:, hs].astype(MXU)))
                o_ref[rows, hs] = _dot_nt(qg, sp.astype(MXU)) + _dot(A.astype(MXU), vh)
                st[h] = sp * gt["glast"][:, hs] + _dot_tn(vh, gt["khat"][:, hs].astype(MXU))
        gate = gt_ref[...]
        sgate = gate * _sigmoid(gate)
        for h in range(HG_H):
            hs = slice(h * HG_DK, (h + 1) * HG_DK)
            oh = o_ref[:, hs]
            r = lax.rsqrt(jnp.mean(oh * oh, axis=-1, keepdims=True) + RMS_EPS)
            y_ref[:, hs] = (oh * r * ng_ref[:, hs] * sgate[:, hs]).astype(MXU)

    col = lambda c: pl.BlockSpec((R, D), lambda b, i: (b * nr + i, c))
    row = pl.BlockSpec((1, D), lambda b, i: (0, 0))
    return _call(
        body, name="hgrn_fwd", grid=(T // S, nr),
        in_specs=[col(0), col(1), col(2), col(3), row, row],
        out_specs=[col(0), col(0),
                   pl.BlockSpec((ncr, HG_H, HG_DK, HG_DK), lambda b, i: (b * nr + i, 0, 0, 0))],
        out_shape=[_sds((T, D), MXU), _sds((T, D), F32), _sds((T // HG_C, HG_H, HG_DK, HG_DK), F32)],
        scratch=[pltpu.VMEM((HG_H, HG_DK, HG_DK), F32)],
    )(p, p, p, p, lb, ng)


def hgrn_bwd(p, o, dyo, states, lb, ng, *, S, R=128):
    T = p.shape[0]
    nr = S // R
    ncr = R // HG_C

    def body(q_ref, f_ref, v_ref, gt_ref, o_ref, dy_ref, st_ref, lb_ref, ng_ref,
             dp_ref, dlb_ref, dng_ref, dst, do_buf, dG_buf, dqb, dkb):
        b = pl.program_id(0)
        i = pl.program_id(1)

        @pl.when(i == 0)
        def _():
            dst[...] = jnp.zeros_like(dst)

        @pl.when((b == 0) & (i == 0))
        def _():
            dlb_ref[...] = jnp.zeros_like(dlb_ref)
            dng_ref[...] = jnp.zeros_like(dng_ref)

        lbv = lb_ref[...]
        gate = gt_ref[...]
        sg_gate = _sigmoid(gate)
        silu_gate = gate * sg_gate
        for h in range(HG_H):
            hs = slice(h * HG_DK, (h + 1) * HG_DK)
            oh = o_ref[:, hs]
            r = lax.rsqrt(jnp.mean(oh * oh, axis=-1, keepdims=True) + RMS_EPS)
            ohat = oh * r
            dyh = dy_ref[:, hs]
            ngh = ng_ref[:, hs]
            dng_ref[:, hs] += jnp.sum(dyh * silu_gate[:, hs] * ohat, axis=0, keepdims=True)
            dp_ref[:, 3 * D + h * HG_DK:3 * D + (h + 1) * HG_DK] = (
                dyh * ohat * ngh * (sg_gate[:, hs] * (1.0 + gate[:, hs] * (1.0 - sg_gate[:, hs])))).astype(MXU)
            dn = dyh * ngh * silu_gate[:, hs]
            do_buf[:, hs] = r * (dn - ohat * jnp.mean(dn * ohat, axis=-1, keepdims=True))

        lastrow = lax.broadcasted_iota(jnp.int32, (HG_C, HG_DK), 0) == HG_C - 1
        for ch in reversed(range(ncr)):
            rows = slice(ch * HG_C, (ch + 1) * HG_C)
            qr = q_ref[rows, :]
            gt = _hgrn_gates(qr, f_ref[rows, :], lbv)
            for h in range(HG_H):
                hs = slice(h * HG_DK, (h + 1) * HG_DK)
                sp = st_ref[ch, h]
                ds = dst[h]
                qg32, kinv32, khat32 = gt["qg"][:, hs], gt["kinv"][:, hs], gt["khat"][:, hs]
                qg, kinv, khat = qg32.astype(MXU), kinv32.astype(MXU), khat32.astype(MXU)
                vh = v_ref[rows, hs].astype(MXU)
                doh = do_buf[rows, hs].astype(MXU)
                dsb = ds.astype(MXU)
                A = _tril_mask(_dot_nt(qg, kinv)).astype(MXU)
                dA = _tril_mask(_dot_nt(doh, vh)).astype(MXU)
                dqg = _dot(doh, sp.astype(MXU)) + _dot(dA, kinv)
                dkinv = _dot_tn(dA, qg)
                dp_ref[rows, 2 * D + h * HG_DK:2 * D + (h + 1) * HG_DK] = (
                    _dot_tn(A, doh) + _dot_nt(khat, dsb)).astype(MXU)
                dkhat = _dot(vh, dsb)
                glast = gt["glast"][:, hs]
                qg32, kinv32, khat32 = qg.astype(F32), kinv.astype(F32), khat.astype(F32)
                extra = (glast * jnp.sum(dsb.astype(F32) * sp.astype(MXU).astype(F32), axis=0, keepdims=True)
                         + jnp.sum(dkhat * khat32, axis=0, keepdims=True))
                dst[h] = ds * glast + _dot_tn(doh, qg)
                dG = dqg * qg32 - dkinv * kinv32 - dkhat * khat32
                dG_buf[:, hs] = dG + jnp.where(lastrow, extra, 0.0)
                dqb[:, hs] = dqg * gt["eG"][:, hs]
                dkb[:, hs] = dkinv * gt["einv"][:, hs] + dkhat * gt["elast"][:, hs]
            dg = _dot_hi(_tri(HG_C, upper=True), dG_buf[...])
            dk = dkb[...]
            sneg, f = gt["sneg"], gt["f"]
            c1 = (1.0 - lbv) * gt["sg"] * sneg
            dp_ref[rows, D:2 * D] = (dg * c1 / f - dk * c1).astype(MXU)
            dlb_ref[...] += jnp.sum(dg * sneg / f - dk * sneg, axis=0, keepdims=True)
            sq = _sigmoid(qr)
            dp_ref[rows, 0:D] = (dqb[...] * (sq * (1.0 + qr * (1.0 - sq)))).astype(MXU)

    rev = lambda b, i: b * nr + nr - 1 - i
    col = lambda c: pl.BlockSpec((R, D), lambda b, i: (rev(b, i), c))
    row = pl.BlockSpec((1, D), lambda b, i: (0, 0))
    return _call(
        body, name="hgrn_bwd", grid=(T // S, nr),
        in_specs=[col(0), col(1), col(2), col(3), col(0), col(0),
                  pl.BlockSpec((ncr, HG_H, HG_DK, HG_DK), lambda b, i: (rev(b, i), 0, 0, 0)), row, row],
        out_specs=[pl.BlockSpec((R, 4 * D), lambda b, i: (rev(b, i), 0)), row, row],
        out_shape=[_sds((T, 4 * D), MXU), _sds((1, D), F32), _sds((1, D), F32)],
        scratch=[pltpu.VMEM((HG_H, HG_DK, HG_DK), F32), pltpu.VMEM((R, D), F32), pltpu.VMEM((HG_C, D), F32),
                 pltpu.VMEM((HG_C, D), F32), pltpu.VMEM((HG_C, D), F32)],
    )(p, p, p, p, o, dyo, states, lb, ng)


def _mm_tn_stack(a, b, *, name, G, M, N, tk, stack):
    E, e, buf = stack
    T = a.shape[-2]

    def spec(arr, width):
        if arr.ndim == 3:
            return pl.BlockSpec((None, tk, width), lambda g, k: (g, k, 0))
        return pl.BlockSpec((tk, width), lambda g, k: (k, 0))

    def body(*refs):
        a_ref, b_ref, o_ref = refs[0], refs[1], refs[-1]

        @pl.when(pl.program_id(1) == 0)
        def _():
            o_ref[...] = jnp.zeros_like(o_ref)

        o_ref[...] += _dot_tn(a_ref[...], b_ref[...])

    in_specs = [spec(a, M), spec(b, N)]
    args = [a, b]
    aliases = {}
    if buf is not None:
        in_specs.append(pl.BlockSpec(memory_space=pl.ANY))
        args.append(buf)
        aliases = {2: 0}
    return pl.pallas_call(
        body, name=name, grid=(G, T // tk), in_specs=in_specs,
        out_specs=pl.BlockSpec((None, None, M, N), lambda g, k: (g, e, 0, 0)),
        out_shape=_sds((G, E, M, N), F32), input_output_aliases=aliases,
        compiler_params=pltpu.CompilerParams(dimension_semantics=("arbitrary", "arbitrary"),
                                             vmem_limit_bytes=VMEM_LIMIT))(*args)


MESH = pl.DeviceIdType.MESH
ANY = pl.BlockSpec(memory_space=pl.ANY)


def _pos():
    return lax.axis_index("x"), lax.axis_index("y"), lax.axis_index("c")


def _other_chips(x, y):
    return [(1 - x, y), (x, 1 - y), (1 - x, 1 - y)]


def _comm_call(body, *, name, args, out_shape, n_sem):
    return pl.pallas_call(
        body, name=name, in_specs=[ANY] * len(args), out_specs=[ANY] * len(out_shape), out_shape=out_shape,
        scratch_shapes=[pltpu.SemaphoreType.DMA((n_sem,)), pltpu.SemaphoreType.DMA((n_sem,)),
                        pltpu.SemaphoreType.DMA((len(args),))],
    )(*args)


def all_gather_chips(xs):
    n = len(xs)

    def body(*refs):
        x_refs, o_refs = refs[:n], refs[n:2 * n]
        ssem, rsem, lsem = refs[2 * n:]
        x, y, c = _pos()
        me = 2 * x + y
        chips = _other_chips(x, y)
        sib = (x, y, 1 - c)

        def rc(src, dst, idx, dev):
            return pltpu.make_async_remote_copy(src_ref=src, dst_ref=dst, send_sem=ssem.at[idx], recv_sem=rsem.at[idx],
                                                device_id=dev, device_id_type=MESH)

        local, started = [], []
        for t in range(n):
            hr = xs[t].shape[0] // 2
            mine = pl.ds(c * hr, hr)
            cp = pltpu.make_async_copy(x_refs[t], o_refs[t].at[me], lsem.at[t])
            cp.start()
            local.append(cp)
            for k, (cx, cy) in enumerate(chips):
                cp = rc(x_refs[t].at[mine], o_refs[t].at[me, mine], 6 * t + k, (cx, cy, c))
                cp.start()
                started.append(cp)
        for t in range(n):
            hr = xs[t].shape[0] // 2
            mine = pl.ds(c * hr, hr)
            for k, (cx, cy) in enumerate(chips):
                landed = o_refs[t].at[2 * cx + cy, mine]
                rc(landed, landed, 6 * t + k, (cx, cy, c)).wait_recv()
                cp = rc(landed, landed, 6 * t + 3 + k, sib)
                cp.start()
                started.append(cp)
        for t in range(n):
            hr = xs[t].shape[0] // 2
            theirs = pl.ds((1 - c) * hr, hr)
            for k, (cx, cy) in enumerate(chips):
                other = o_refs[t].at[2 * cx + cy, theirs]
                rc(other, other, 6 * t + 3 + k, sib).wait_recv()
        for cp in started:
            cp.wait_send()
        for cp in local:
            cp.wait()

    outs = _comm_call(body, name="all_gather_chips", args=list(xs),
                      out_shape=[_sds((NSH,) + a.shape, a.dtype) for a in xs], n_sem=6 * n)
    return list(outs)


def sibling_half_exchange(gs):
    n = len(gs)

    def body(*refs):
        g_refs, o_refs = refs[:n], refs[n:2 * n]
        ssem, rsem, _ = refs[2 * n:]
        x, y, c = _pos()
        cps = []
        for t in range(n):
            hr = gs[t].shape[1] // 2
            for j in range(NSH):
                cp = pltpu.make_async_remote_copy(
                    src_ref=g_refs[t].at[j, pl.ds((1 - c) * hr, hr)], dst_ref=o_refs[t].at[j],
                    send_sem=ssem.at[NSH * t + j], recv_sem=rsem.at[NSH * t + j],
                    device_id=(x, y, 1 - c), device_id_type=MESH)
                cp.start()
                cps.append(cp)
        for cp in cps:
            cp.wait()

    outs = _comm_call(body, name="sibling_half_exchange", args=list(gs),
                      out_shape=[_sds((NSH, g.shape[1] // 2, g.shape[2]), g.dtype) for g in gs], n_sem=NSH * n)
    return list(outs)


def chip_scatter(ss):
    n = len(ss)

    def body(*refs):
        s_refs, o_refs = refs[:n], refs[n:2 * n]
        ssem, rsem, _ = refs[2 * n:]
        x, y, c = _pos()
        cps = []
        for t in range(n):
            for k, (cx, cy) in enumerate(_other_chips(x, y)):
                cp = pltpu.make_async_remote_copy(
                    src_ref=s_refs[t].at[2 * cx + cy], dst_ref=o_refs[t].at[k],
                    send_sem=ssem.at[3 * t + k], recv_sem=rsem.at[3 * t + k],
                    device_id=(cx, cy, c), device_id_type=MESH)
                cp.start()
                cps.append(cp)
        for cp in cps:
            cp.wait()

    outs = _comm_call(body, name="chip_scatter", args=list(ss),
                      out_shape=[_sds((3,) + s.shape[1:], s.dtype) for s in ss], n_sem=3 * n)
    return list(outs)


def sibling_exchange(rs):
    n = len(rs)

    def body(*refs):
        r_refs, o_refs = refs[:n], refs[n:2 * n]
        ssem, rsem, _ = refs[2 * n:]
        x, y, c = _pos()
        cps = []
        for t in range(n):
            cp = pltpu.make_async_remote_copy(
                src_ref=r_refs[t], dst_ref=o_refs[t], send_sem=ssem.at[t], recv_sem=rsem.at[t],
                device_id=(x, y, 1 - c), device_id_type=MESH)
            cp.start()
            cps.append(cp)
        for cp in cps:
            cp.wait()

    outs = _comm_call(body, name="sibling_exchange", args=list(rs),
                      out_shape=[_sds(r.shape, r.dtype) for r in rs], n_sem=n)
    return list(outs)


def all_gather_devices(v):
    def body(v_ref, o_ref, ssem, rsem, lsem):
        x, y, c = _pos()
        me = 4 * x + 2 * y + c
        loc = pltpu.make_async_copy(v_ref, o_ref.at[me], lsem.at[0])
        loc.start()
        cps = []
        k = 0
        for fx in range(2):
            for fy in range(2):
                for fc in range(2):
                    if fx == fy == fc == 0:
                        continue
                    cp = pltpu.make_async_remote_copy(
                        src_ref=v_ref, dst_ref=o_ref.at[me], send_sem=ssem.at[k], recv_sem=rsem.at[k],
                        device_id=(x ^ fx, y ^ fy, c ^ fc), device_id_type=MESH)
                    cp.start()
                    src = 4 * (x ^ fx) + 2 * (y ^ fy) + (c ^ fc)
                    cps.append((cp, o_ref.at[src], k))
                    k += 1
        for cp, landed, k in cps:
            cp.wait_send()
            pltpu.make_async_remote_copy(
                src_ref=landed, dst_ref=landed, send_sem=ssem.at[k], recv_sem=rsem.at[k],
                device_id=(x, y, c), device_id_type=MESH).wait_recv()
        loc.wait()

    return _comm_call(body, name="all_gather_devices", args=[v],
                      out_shape=[_sds((8,) + v.shape, v.dtype)], n_sem=7)[0]


def _call_sp(body, *, name, grid, in_specs, out_specs, out_shape, pos, args):
    return pl.pallas_call(
        body, name=name,
        grid_spec=pltpu.PrefetchScalarGridSpec(num_scalar_prefetch=1, grid=grid, in_specs=in_specs,
                                               out_specs=out_specs),
        out_shape=out_shape,
        compiler_params=pltpu.CompilerParams(dimension_semantics=("arbitrary",) * len(grid),
                                             vmem_limit_bytes=VMEM_LIMIT))(pos, *args)


def _rows_tile(r):
    for t in (512, 256, 128, 64, 32, 16, 8):
        if r % t == 0:
            return t
    raise ValueError(r)


def pair_sum(g, r, pos):
    _, R, C = g.shape
    hr = R // 2
    tr = _rows_tile(hr)
    nbh = hr // tr

    def body(p_ref, g_ref, r_ref, o_ref):
        o_ref[...] = (g_ref[...] + r_ref[...]).astype(MXU)

    return _call_sp(
        body, name="pair_sum", grid=(NSH, nbh), pos=pos, args=[g, r],
        in_specs=[pl.BlockSpec((None, tr, C), lambda j, i, p: (j, p[0] * nbh + i, 0)),
                  pl.BlockSpec((None, tr, C), lambda j, i, p: (j, i, 0))],
        out_specs=pl.BlockSpec((None, tr, C), lambda j, i, p: (j, i, 0)),
        out_shape=_sds((NSH, hr, C), MXU))


def reduce_own(g, r_sib, r_ici, pos):
    _, R, C = g.shape
    hr = R // 2
    tr = _rows_tile(hr)
    nbh = hr // tr

    def body(p_ref, g_ref, rs_ref, ri_ref, o_ref):
        s = g_ref[...] + rs_ref[...]
        for k in range(3):
            s = s + ri_ref[k].astype(F32)
        o_ref[...] = s

    return _call_sp(
        body, name="reduce_own", grid=(nbh,), pos=pos, args=[g, r_sib, r_ici],
        in_specs=[pl.BlockSpec((None, tr, C), lambda i, p: (p[1], p[0] * nbh + i, 0)),
                  pl.BlockSpec((None, tr, C), lambda i, p: (p[1], i, 0)),
                  pl.BlockSpec((3, tr, C), lambda i, p: (0, i, 0))],
        out_specs=pl.BlockSpec((tr, C), lambda i, p: (i, 0)),
        out_shape=_sds((hr, C), F32))


def _adamw_math(w, g, m, v):
    m = ADAM_B1 * m + (1.0 - ADAM_B1) * g
    v = ADAM_B2 * v + (1.0 - ADAM_B2) * (g * g)
    m_hat = m / (1.0 - ADAM_B1 ** ADAM_STEP)
    v_hat = v / (1.0 - ADAM_B2 ** ADAM_STEP)
    delta = -ADAM_LR * (m_hat / (jnp.sqrt(v_hat) + ADAM_EPS) + ADAM_WD * w)
    return delta, m, v


def adamw_halves(w, m, v, ga, gb, pos):
    R, C = w.shape
    hr = R // 2
    tr = _rows_tile(hr)
    nbh = hr // tr

    def body(p_ref, w_ref, m_ref, v_ref, ga_ref, gb_ref, g_ref, d_ref, mo_ref, vo_ref):
        mine = (pl.program_id(0) // nbh) == p_ref[0]
        g = jnp.where(mine, ga_ref[...], gb_ref[...])
        g_ref[...] = g
        d_ref[...], mo_ref[...], vo_ref[...] = _adamw_math(w_ref[...], g, m_ref[...], v_ref[...])

    blk = pl.BlockSpec((tr, C), lambda i, p: (i, 0))
    return _call_sp(
        body, name="adamw_halves", grid=(R // tr,), pos=pos, args=[w, m, v, ga, gb],
        in_specs=[blk, blk, blk,
                  pl.BlockSpec((tr, C), lambda i, p: (jnp.where(i // nbh == p[0], i % nbh, 0), 0)),
                  pl.BlockSpec((tr, C), lambda i, p: (jnp.where(i // nbh == p[0], 0, i % nbh), 0))],
        out_specs=[blk, blk, blk, blk],
        out_shape=[_sds((R, C), F32)] * 4)


def adamw_sum(gall, w, m, v):
    n, R, C = gall.shape

    def body(ga_ref, w_ref, m_ref, v_ref, g_ref, d_ref, mo_ref, vo_ref):
        g = ga_ref[0]
        for k in range(1, n):
            g = g + ga_ref[k]
        g_ref[...] = g
        d_ref[...], mo_ref[...], vo_ref[...] = _adamw_math(w_ref[...], g, m_ref[...], v_ref[...])

    blk = pl.BlockSpec((R, C), lambda i: (0, 0))
    return _call(body, name="adamw_sum", grid=(1,),
                 in_specs=[pl.BlockSpec((n, R, C), lambda i: (0, 0, 0)), blk, blk, blk],
                 out_specs=[blk, blk, blk, blk], out_shape=[_sds((R, C), F32)] * 4)(gall, w, m, v)


_WEIGHTS = ['ffn_norm', 'ffn_w_gate', 'ffn_w_up', 'ffn_w_down', 'mix_norm', 'final_norm', 'conv_w_in', 'conv_b_in',
            'conv_dw', 'conv_dw_b', 'conv_ln_g', 'conv_ln_b', 'conv_w_out', 'fox_w_in', 'fox_b_f', 'fox_w_out',
            'hgrn_w_in', 'hgrn_lb_logits', 'hgrn_norm', 'hgrn_w_out', 'pool_w', 'pool_scale']
_BIG = ['ffn_w_gate', 'ffn_w_up', 'ffn_w_down', 'conv_w_in', 'conv_w_out', 'fox_w_in', 'fox_w_out',
        'hgrn_w_in', 'hgrn_w_out', 'pool_w']
_SHARDED_SMALL = ['ffn_norm', 'conv_dw', 'hgrn_norm', 'pool_scale']
_REPLICATED = ['mix_norm', 'final_norm', 'conv_b_in', 'conv_dw_b', 'conv_ln_g', 'conv_ln_b', 'fox_b_f', 'hgrn_lb_logits']
FOX_N = 3 * D + FOX_H
FOX_NP = 3200
QS = D // NSH


def _pad_rows(a, rows):
    return jnp.pad(a, ((0, rows - a.shape[0]), (0, 0)))


def _pack_sharded_small(get):
    return jnp.concatenate([get('ffn_norm').reshape(8, -1), _pad_rows(get('conv_dw')[0], 32),
                            get('hgrn_norm'), get('pool_scale'), jnp.zeros((6, get('pool_scale').shape[1]), F32)], axis=0)


def _pack_replicated(get):
    return jnp.concatenate([get('mix_norm'), get('final_norm').reshape(1, D), get('conv_b_in').reshape(2, D),
                            get('conv_dw_b'), get('conv_ln_g'), get('conv_ln_b'),
                            jnp.pad(get('fox_b_f'), ((0, 0), (0, D - FOX_H))), get('hgrn_lb_logits'),
                            jnp.zeros((9, D), F32)], axis=0)


def _unpack_replicated(p):
    return {'mix_norm': p[0:4], 'final_norm': p[4], 'conv_b_in': p[5:7].reshape(1, 2 * D), 'conv_dw_b': p[7:8],
            'conv_ln_g': p[8:9], 'conv_ln_b': p[9:10], 'fox_b_f': p[10:11, :FOX_H], 'hgrn_lb_logits': p[11:15]}


def _unpack_sharded_small(p):
    return {'ffn_norm': p[0:8].reshape(DEPTH, 2, -1), 'conv_dw': p[8:8 + CONV_W][None],
            'hgrn_norm': p[40:41], 'pool_scale': p[41:42]}


def kernel(x, ffn_norm, ffn_w_gate, ffn_w_up, ffn_w_down, mix_norm, final_norm, conv_w_in, conv_b_in, conv_dw, conv_dw_b, conv_ln_g, conv_ln_b, conv_w_out, fox_w_in, fox_b_f, fox_w_out, hgrn_w_in, hgrn_lb_logits, hgrn_norm, hgrn_w_out, pool_w, pool_scale, loss_target, m_ffn_norm, m_ffn_w_gate, m_ffn_w_up, m_ffn_w_down, m_mix_norm, m_final_norm, m_conv_w_in, m_conv_b_in, m_conv_dw, m_conv_dw_b, m_conv_ln_g, m_conv_ln_b, m_conv_w_out, m_fox_w_in, m_fox_b_f, m_fox_w_out, m_hgrn_w_in, m_hgrn_lb_logits, m_hgrn_norm, m_hgrn_w_out, m_pool_w, m_pool_scale, v_ffn_norm, v_ffn_w_gate, v_ffn_w_up, v_ffn_w_down, v_mix_norm, v_final_norm, v_conv_w_in, v_conv_b_in, v_conv_dw, v_conv_dw_b, v_conv_ln_g, v_conv_ln_b, v_conv_w_out, v_fox_w_in, v_fox_b_f, v_fox_w_out, v_hgrn_w_in, v_hgrn_lb_logits, v_hgrn_norm, v_hgrn_w_out, v_pool_w, v_pool_scale):
    W = dict(ffn_norm=ffn_norm, ffn_w_gate=ffn_w_gate, ffn_w_up=ffn_w_up, ffn_w_down=ffn_w_down, mix_norm=mix_norm, final_norm=final_norm, conv_w_in=conv_w_in, conv_b_in=conv_b_in, conv_dw=conv_dw, conv_dw_b=conv_dw_b, conv_ln_g=conv_ln_g, conv_ln_b=conv_ln_b, conv_w_out=conv_w_out, fox_w_in=fox_w_in, fox_b_f=fox_b_f, fox_w_out=fox_w_out, hgrn_w_in=hgrn_w_in, hgrn_lb_logits=hgrn_lb_logits, hgrn_norm=hgrn_norm, hgrn_w_out=hgrn_w_out, pool_w=pool_w, pool_scale=pool_scale)
    M = dict(ffn_norm=m_ffn_norm, ffn_w_gate=m_ffn_w_gate, ffn_w_up=m_ffn_w_up, ffn_w_down=m_ffn_w_down, mix_norm=m_mix_norm, final_norm=m_final_norm, conv_w_in=m_conv_w_in, conv_b_in=m_conv_b_in, conv_dw=m_conv_dw, conv_dw_b=m_conv_dw_b, conv_ln_g=m_conv_ln_g, conv_ln_b=m_conv_ln_b, conv_w_out=m_conv_w_out, fox_w_in=m_fox_w_in, fox_b_f=m_fox_b_f, fox_w_out=m_fox_w_out, hgrn_w_in=m_hgrn_w_in, hgrn_lb_logits=m_hgrn_lb_logits, hgrn_norm=m_hgrn_norm, hgrn_w_out=m_hgrn_w_out, pool_w=m_pool_w, pool_scale=m_pool_scale)
    V = dict(ffn_norm=v_ffn_norm, ffn_w_gate=v_ffn_w_gate, ffn_w_up=v_ffn_w_up, ffn_w_down=v_ffn_w_down, mix_norm=v_mix_norm, final_norm=v_final_norm, conv_w_in=v_conv_w_in, conv_b_in=v_conv_b_in, conv_dw=v_conv_dw, conv_dw_b=v_conv_dw_b, conv_ln_g=v_conv_ln_g, conv_ln_b=v_conv_ln_b, conv_w_out=v_conv_w_out, fox_w_in=v_fox_w_in, fox_b_f=v_fox_b_f, fox_w_out=v_fox_w_out, hgrn_w_in=v_hgrn_w_in, hgrn_lb_logits=v_hgrn_lb_logits, hgrn_norm=v_hgrn_norm, hgrn_w_out=v_hgrn_w_out, pool_w=v_pool_w, pool_scale=v_pool_scale)

    px, py, pc = _pos()
    jme = 2 * px + py
    pos = jnp.stack([pc, jme]).astype(jnp.int32)
    S = x.shape[1]
    T = x.shape[0] * S
    x2 = x.reshape(T, D)
    tgt = loss_target.reshape(T, D)

    flat = lambda a: a.reshape(-1, a.shape[-1])
    gathered = all_gather_chips([flat(W[n]).astype(MXU) for n in _BIG] + [_pack_sharded_small(W.get)])
    G = dict(zip(_BIG, gathered[:-1]))
    small = gathered[-1].transpose(1, 0, 2).reshape(48, D)
    ffn_norm_f, conv_dw_f = small[0:8], small[8:40]
    hgrn_norm_f, pool_scale_f = small[40:41], small[41:42]
    wg_all = G['ffn_w_gate'].reshape(NSH, 2 * DEPTH, D, FS)
    wu_all = G['ffn_w_up'].reshape(NSH, 2 * DEPTH, D, FS)
    wd_all = G['ffn_w_down'].reshape(NSH, 2 * DEPTH, FS, D)
    conv_wi = G['conv_w_in']
    conv_wo = G['conv_w_out'].reshape(D, D)
    fox_full = jnp.pad(G['fox_w_in'].transpose(1, 0, 2).reshape(D, FOX_N), ((0, 0), (0, FOX_NP - FOX_N)))
    fox_w5 = fox_full.reshape(D, 5, FOX_NP // 5).transpose(1, 0, 2)
    fox_wf = fox_full[:, 3 * D:][None]
    fox_bf = jnp.pad(fox_b_f, ((0, 0), (0, 128 - FOX_H)))
    fox_wo = G['fox_w_out'].reshape(D, D)
    hgrn_wi = G['hgrn_w_in']
    hgrn_wo = G['hgrn_w_out'].reshape(D, D)
    pool_wf = G['pool_w'].reshape(NSH, 4, 64, POOL_G).transpose(1, 0, 2, 3).reshape(4, POOL_G, POOL_G)
    conv_bi = conv_b_in.reshape(NSH, 1, 2 * D // NSH)

    def ffn_f(xs, e):
        xo, h, a, b = ffn_fwd(xs, ffn_norm_f[e:e + 1], wg_all, wu_all, wd_all, e)
        return xo, (xs, h, a, b)

    saved = []
    xs = x2
    lb = lb_fwd(hgrn_lb_logits)
    for i in range(DEPTH):
        xs, r0 = ffn_f(xs, 2 * i)
        gm = mix_norm[i:i + 1]
        xin = xs
        if i == 0:
            p, h = norm_mm(xin, gm, conv_wi, conv_bi, name="conv_in", out_dtype=F32)
            u2, u4 = conv_fwd_core(p, conv_dw_f, conv_dw_b, conv_ln_g, conv_ln_b, S=S)
            xs = mm_res(u4, conv_wo, xin, name="conv_out")
            rm = (xin, p, h, u2, u4)
        elif i == 1:
            p, h = norm_mm(xin, gm, fox_w5, None, name="fox_in", out_dtype=MXU)
            fl, _ = norm_mm(xin, gm, fox_wf, None, name="fox_in_f", out_dtype=F32)
            cq, _ = fox_cum(fl, fox_bf, S=S)
            qa, ka, va = fox_prep(p, cq)
            o, o32, lse = fox2_fwd(qa, ka, va, S=S)
            xs = mm_res(o, fox_wo, xin, name="fox_out")
            rm = (xin, h, fl, qa, ka, va, o, o32, lse)
        elif i == 2:
            p, h = norm_mm(xin, gm, hgrn_wi, None, name="hgrn_in", out_dtype=F32)
            yh, oh, st = hgrn_fwd(p, lb, hgrn_norm_f, S=S)
            xs = mm_res(yh, hgrn_wo, xin, name="hgrn_out")
            rm = (xin, p, h, yh, oh, st)
        else:
            xs, mp = pool_fwd(xin, gm, pool_wf, pool_scale_f, S=S)
            rm = (xin, mp)
        xs, r1 = ffn_f(xs, 2 * i + 1)
        saved.append((r0, rm, r1))

    loss8, dx, d_final = loss_head(xs, final_norm.reshape(1, D), tgt)

    gb = {'g': None, 'u': None, 'd': None}
    d_ffn_norm = [None] * (2 * DEPTH)
    d_mix_norm = [None] * DEPTH
    gbig = {}
    gsm = {}

    def ffn_b(dy, res, e):
        xin, h, a, b = res
        dxo, da, db, z, dyh, dg = ffn_bwd_dx(xin, ffn_norm_f[e:e + 1], dy, a, b, wg_all, wu_all, wd_all, e)
        gb['g'] = mm_tn(h, da, name="ffn_dwg", G=NSH, M=D, N=FS, stack=(2 * DEPTH, e, gb['g']))
        gb['u'] = mm_tn(h, db, name="ffn_dwu", G=NSH, M=D, N=FS, stack=(2 * DEPTH, e, gb['u']))
        gb['d'] = mm_tn(z, dyh, name="ffn_dwd", G=NSH, M=FS, N=D, stack=(2 * DEPTH, e, gb['d']))
        d_ffn_norm[e] = dg
        return dxo

    for i in reversed(range(DEPTH)):
        r0, rm, r1 = saved[i]
        dx = ffn_b(dx, r1, 2 * i + 1)
        gm = mix_norm[i:i + 1]
        if i == 0:
            xin, p, h, u2, u4 = rm
            du2, dyb, gsm['conv_ln_g'], gsm['conv_ln_b'], gsm['conv_dw_b'] = conv_bwd_rows(dx, conv_wo, u2, conv_ln_g, conv_ln_b)
            dp, gsm['conv_b_in'], ddw = conv_bwd_core(du2, p, conv_dw_f, S=S)
            gsm['conv_dw'] = ddw
            gbig['conv_w_in'] = mm_tn(h, dp, name="conv_dwin", G=NSH, M=D, N=2 * D // NSH, b_step=1)
            gbig['conv_w_out'] = mm_tn(u4, dyb, name="conv_dwout", G=NSH, M=QS, N=D, a_step=1)
            dx, d_mix_norm[i] = inproj_bwd(dp, conv_wi, xin, gm, dx, name="conv_in_bwd")
        elif i == 1:
            xin, h, fl, qa, ka, va, o, o32, lse = rm
            do, dyb = mm_nt(dx, fox_wo, name="fox_out_bwd")
            qb, da = fox2_prep_bwd(do, o32, lse, qa)
            dq = fox2_dq(qb, ka, va, da, S=S)
            dk, dv, dck = fox2_dkv(qb, ka, va, da, S=S)
            dfl, dbf = fox_fin(dck, fl, fox_bf, S=S)
            gsm['fox_b_f'] = dbf
            dp = jnp.concatenate([dq, dk, dv, dfl], axis=1)
            dw5 = mm_tn(h, dp, name="fox_dwin", G=5, M=D, N=FOX_NP // 5, b_step=1)
            dwf = dw5.transpose(1, 0, 2).reshape(D, FOX_NP)[:, :FOX_N]
            gbig['fox_w_in'] = dwf.reshape(D, NSH, FOX_N // NSH).transpose(1, 0, 2)
            gbig['fox_w_out'] = mm_tn(o, dyb, name="fox_dwout", G=NSH, M=QS, N=D, a_step=1)
            dx, d_mix_norm[i] = inproj_bwd(dp, fox_w5, xin, gm, dx, name="fox_in_bwd")
        elif i == 2:
            xin, p, h, yh, oh, st = rm
            dyo, dyb = mm_nt(dx, hgrn_wo, name="hgrn_out_bwd")
            dp, dlb, gsm['hgrn_norm'] = hgrn_bwd(p, oh, dyo, st, lb, hgrn_norm_f, S=S)
            gsm['hgrn_lb_logits'] = lb_bwd(hgrn_lb_logits, dlb)
            gbig['hgrn_w_in'] = mm_tn(h, dp, name="hgrn_dwin", G=NSH, M=D, N=D, b_step=1)
            gbig['hgrn_w_out'] = mm_tn(yh, dyb, name="hgrn_dwout", G=NSH, M=QS, N=D, a_step=1)
            dx, d_mix_norm[i] = inproj_bwd(dp, hgrn_wi, xin, gm, dx, name="hgrn_in_bwd")
        else:
            xin, mp = rm
            dmc, dyp, gsm['pool_scale'] = pool_bwd_rows(dx, mp, pool_wf, pool_scale_f, S=S)
            dwp = mm_tn(mp, dyp, name="pool_dw", G=4, M=POOL_G, N=POOL_G, a_step=1, b_step=1)
            gbig['pool_w'] = dwp.reshape(4, NSH, 64, POOL_G).transpose(1, 0, 2, 3).reshape(NSH, 4 * 64, POOL_G)
            dx, d_mix_norm[i] = pool_bwd_core(dmc, xin, gm, dx, S=S)
        dx = ffn_b(dx, r0, 2 * i)

    gbig['ffn_w_gate'] = gb['g'].reshape(NSH, 2 * DEPTH * D, FS)
    gbig['ffn_w_up'] = gb['u'].reshape(NSH, 2 * DEPTH * D, FS)
    gbig['ffn_w_down'] = gb['d'].reshape(NSH, 2 * DEPTH * FS, D)

    gl = [gbig[n] for n in _BIG]
    r_sib = sibling_half_exchange(gl)
    s16 = [pair_sum(g, r, pos) for g, r in zip(gl, r_sib)]
    r_ici = chip_scatter(s16)
    red = [reduce_own(g, rs, ri, pos) for g, rs, ri in zip(gl, r_sib, r_ici)]
    oth = sibling_exchange(red)
    out = {}
    for n, ga, gb_ in zip(_BIG, red, oth):
        res = adamw_halves(flat(W[n]), flat(M[n]), flat(V[n]), ga, gb_, pos)
        out[n] = [r.reshape(W[n].shape) for r in res]

    gfull = {'mix_norm': jnp.concatenate(d_mix_norm, axis=0), 'final_norm': d_final,
             'conv_b_in': gsm['conv_b_in'], 'conv_dw_b': gsm['conv_dw_b'], 'conv_ln_g': gsm['conv_ln_g'],
             'conv_ln_b': gsm['conv_ln_b'], 'fox_b_f': gsm['fox_b_f'][:, :FOX_H], 'hgrn_lb_logits': gsm['hgrn_lb_logits'],
             'ffn_norm': jnp.concatenate(d_ffn_norm, axis=0), 'conv_dw': gsm['conv_dw'][None, :CONV_W],
             'hgrn_norm': gsm['hgrn_norm'], 'pool_scale': gsm['pool_scale']}
    gpack = jnp.concatenate([_pack_replicated(gfull.get), _pack_sharded_small(gfull.get)], axis=0)
    gall = all_gather_devices(gpack)
    rep = adamw_sum(gall[:, :24], _pack_replicated(W.get), _pack_replicated(M.get), _pack_replicated(V.get))
    rep = [_unpack_replicated(r) for r in rep]
    for n in _REPLICATED:
        out[n] = [r[n].reshape(W[n].shape) for r in rep]
    gsh = lax.dynamic_slice_in_dim(gall[:, 24:], jme * QS, QS, axis=2)
    shd = adamw_sum(gsh, _pack_sharded_small(W.get), _pack_sharded_small(M.get), _pack_sharded_small(V.get))
    shd = [_unpack_sharded_small(r) for r in shd]
    for n in _SHARDED_SMALL:
        out[n] = [r[n].reshape(W[n].shape) for r in shd]

    loss = lax.psum(loss8[0, 0], ("x", "y", "c"))
    res = [loss, dx.reshape(x.shape)]
    for k in range(4):
        res += [out[n][k] for n in _WEIGHTS]
    return tuple(res)
```

```python
import functools

import jax
import jax.numpy as jnp
from jax import lax
from jax.experimental import pallas as pl
from jax.experimental.pallas import tpu as pltpu

D = 1024
F = 2816
NSH = 4
FS = F // NSH
DEPTH = 4
RMS_EPS = 1e-6
LN_EPS = 1e-5
CONV_W = 31
HALO = 32
FOX_H = 16
FOX_DH = 64
HG_H = 8
HG_DK = 128
HG_C = 32
POOL_WIN = (2, 4, 8, 16)
POOL_G = 256
MXU = jnp.bfloat16
F32 = jnp.float32
VMEM_LIMIT = 52 * 1024 * 1024

ADAM_LR = 0.001
ADAM_B1 = 0.9
ADAM_B2 = 0.999
ADAM_EPS = 1e-08
ADAM_WD = 0.01
ADAM_STEP = 10


def _call(body, *, name, grid, in_specs, out_specs, out_shape, scratch=()):
    return pl.pallas_call(
        body, name=name, grid=grid, in_specs=in_specs, out_specs=out_specs, out_shape=out_shape,
        scratch_shapes=list(scratch),
        compiler_params=pltpu.CompilerParams(dimension_semantics=("arbitrary",) * len(grid),
                                             vmem_limit_bytes=VMEM_LIMIT))


def _dot(a, b):
    return jnp.dot(a, b, preferred_element_type=F32)


def _dot_nt(a, b):
    return lax.dot_general(a, b, (((1,), (1,)), ((), ())), preferred_element_type=F32)


def _dot_tn(a, b):
    return lax.dot_general(a, b, (((0,), (0,)), ((), ())), preferred_element_type=F32)


def _split(x):
    hi = x.astype(MXU)
    return hi, (x - hi.astype(F32)).astype(MXU)


def _sigmoid(x):
    return 1.0 / (1.0 + jnp.exp(-x))


def _rms(x, g):
    r = lax.rsqrt(jnp.mean(x * x, axis=-1, keepdims=True) + RMS_EPS)
    return x * r * g


def _rms_bwd(dh, x, g):
    r = lax.rsqrt(jnp.mean(x * x, axis=-1, keepdims=True) + RMS_EPS)
    xh = x * r
    dhg = dh * g
    dx = r * (dhg - xh * jnp.mean(dhg * xh, axis=-1, keepdims=True))
    return dx, jnp.sum(dh * xh, axis=0, keepdims=True)


def _sds(shape, dtype):
    return jax.ShapeDtypeStruct(shape, dtype)


def _wspec(w, e):
    if w.ndim == 3:
        return pl.BlockSpec((None,) + w.shape[1:], lambda i, j: (j, 0, 0))
    return pl.BlockSpec((None, None) + w.shape[2:], lambda i, j: (j, e, 0, 0))


def ffn_fwd(x, g, wg, wu, wd, e=0, *, tm=512):
    T = x.shape[0]

    def body(x_ref, g_ref, wg_ref, wu_ref, wd_ref, xo_ref, h_ref, a_ref, b_ref, acc_ref):
        j = pl.program_id(1)

        @pl.when(j == 0)
        def _():
            h_ref[...] = _rms(x_ref[...], g_ref[...]).astype(MXU)
            acc_ref[...] = jnp.zeros_like(acc_ref)

        h = h_ref[...]
        a = _dot(h, wg_ref[...])
        b = _dot(h, wu_ref[...])
        a_ref[...] = a.astype(MXU)
        b_ref[...] = b.astype(MXU)
        z = (a * _sigmoid(a) * b).astype(MXU)
        acc_ref[...] += _dot(z, wd_ref[...])

        @pl.when(j == NSH - 1)
        def _():
            xo_ref[...] = x_ref[...] + 0.5 * acc_ref[...]

    return _call(
        body, name="ffn_fwd", grid=(T // tm, NSH),
        in_specs=[pl.BlockSpec((tm, D), lambda i, j: (i, 0)),
                  pl.BlockSpec((1, D), lambda i, j: (0, 0)),
                  _wspec(wg, e), _wspec(wu, e), _wspec(wd, e)],
        out_specs=[pl.BlockSpec((tm, D), lambda i, j: (i, 0)),
                   pl.BlockSpec((tm, D), lambda i, j: (i, 0)),
                   pl.BlockSpec((None, tm, FS), lambda i, j: (j, i, 0)),
                   pl.BlockSpec((None, tm, FS), lambda i, j: (j, i, 0))],
        out_shape=[_sds((T, D), F32), _sds((T, D), MXU), _sds((NSH, T, FS), MXU), _sds((NSH, T, FS), MXU)],
        scratch=[pltpu.VMEM((tm, D), F32)],
    )(x, g, wg, wu, wd)


def ffn_bwd_dx(x, g, dy, a, b, wg, wu, wd, e=0, *, tm=512):
    T = x.shape[0]

    def body(x_ref, g_ref, dy_ref, a_ref, b_ref, wg_ref, wu_ref, wd_ref,
             dx_ref, da_ref, db_ref, z_ref, dyh_ref, dg_ref, acc_ref):
        i = pl.program_id(0)
        j = pl.program_id(1)

        @pl.when(j == 0)
        def _():
            dyh_ref[...] = (0.5 * dy_ref[...]).astype(MXU)
            acc_ref[...] = jnp.zeros_like(acc_ref)

        @pl.when((i == 0) & (j == 0))
        def _():
            dg_ref[...] = jnp.zeros_like(dg_ref)

        dz = _dot(dyh_ref[...], wd_ref[...])
        av = a_ref[...].astype(F32)
        bv = b_ref[...].astype(F32)
        s = _sigmoid(av)
        sa = av * s
        da = (dz * bv * (s * (1.0 + av * (1.0 - s)))).astype(MXU)
        db = (dz * sa).astype(MXU)
        da_ref[...] = da
        db_ref[...] = db
        z_ref[...] = (sa * bv).astype(MXU)
        acc_ref[...] += _dot(da, wg_ref[...]) + _dot(db, wu_ref[...])

        @pl.when(j == NSH - 1)
        def _():
            dxn, dg = _rms_bwd(acc_ref[...], x_ref[...], g_ref[...])
            dx_ref[...] = dy_ref[...] + dxn
            dg_ref[...] += dg

    return _call(
        body, name="ffn_bwd_dx", grid=(T // tm, NSH),
        in_specs=[pl.BlockSpec((tm, D), lambda i, j: (i, 0)),
                  pl.BlockSpec((1, D), lambda i, j: (0, 0)),
                  pl.BlockSpec((tm, D), lambda i, j: (i, 0)),
                  pl.BlockSpec((None, tm, FS), lambda i, j: (j, i, 0)),
                  pl.BlockSpec((None, tm, FS), lambda i, j: (j, i, 0)),
                  _wspec(wg, e), _wspec(wu, e), _wspec(wd, e)],
        out_specs=[pl.BlockSpec((tm, D), lambda i, j: (i, 0)),
                   pl.BlockSpec((None, tm, FS), lambda i, j: (j, i, 0)),
                   pl.BlockSpec((None, tm, FS), lambda i, j: (j, i, 0)),
                   pl.BlockSpec((None, tm, FS), lambda i, j: (j, i, 0)),
                   pl.BlockSpec((tm, D), lambda i, j: (i, 0)),
                   pl.BlockSpec((1, D), lambda i, j: (0, 0))],
        out_shape=[_sds((T, D), F32), _sds((NSH, T, FS), MXU), _sds((NSH, T, FS), MXU),
                   _sds((NSH, T, FS), MXU), _sds((T, D), MXU), _sds((1, D), F32)],
        scratch=[pltpu.VMEM((tm, D), F32)],
    )(x, g, dy, a, b, wg, wu, wd)


def mm_tn(a, b, *, name, G, M, N, a_step=0, b_step=0, tk=512, stack=None):
    T = a.shape[-2]
    if stack is not None:
        return _mm_tn_stack(a, b, name=name, G=G, M=M, N=N, tk=tk, stack=stack)

    def spec(arr, width, step):
        if arr.ndim == 3:
            return pl.BlockSpec((None, tk, width), lambda g, k: (g, k, 0))
        return pl.BlockSpec((tk, width), lambda g, k: (k, g * step))

    def body(a_ref, b_ref, o_ref):
        @pl.when(pl.program_id(1) == 0)
        def _():
            o_ref[...] = jnp.zeros_like(o_ref)

        o_ref[...] += _dot_tn(a_ref[...], b_ref[...])

    return _call(
        body, name=name, grid=(G, T // tk),
        in_specs=[spec(a, M, a_step), spec(b, N, b_step)],
        out_specs=pl.BlockSpec((None, M, N), lambda g, k: (g, 0, 0)),
        out_shape=_sds((G, M, N), F32),
    )(a, b)


def norm_mm(x, g, wb, bias, *, name, out_dtype, tm=512):
    T = x.shape[0]
    G, _, ns = wb.shape
    has_bias = bias is not None

    def body(*refs):
        if has_bias:
            x_ref, g_ref, w_ref, bias_ref, p_ref, h_ref = refs
        else:
            x_ref, g_ref, w_ref, p_ref, h_ref = refs

        @pl.when(pl.program_id(1) == 0)
        def _():
            h_ref[...] = _rms(x_ref[...], g_ref[...]).astype(MXU)

        p = _dot(h_ref[...], w_ref[...])
        if has_bias:
            p = p + bias_ref[...]
        p_ref[...] = p.astype(out_dtype)

    in_specs = [pl.BlockSpec((tm, D), lambda i, j: (i, 0)),
                pl.BlockSpec((1, D), lambda i, j: (0, 0)),
                pl.BlockSpec((None, D, ns), lambda i, j: (j, 0, 0))]
    args = [x, g, wb]
    if has_bias:
        in_specs.append(pl.BlockSpec((None, 1, ns), lambda i, j: (j, 0, 0)))
        args.append(bias)
    return _call(
        body, name=name, grid=(T // tm, G), in_specs=in_specs,
        out_specs=[pl.BlockSpec((tm, ns), lambda i, j: (i, j)),
                   pl.BlockSpec((tm, D), lambda i, j: (i, 0))],
        out_shape=[_sds((T, G * ns), out_dtype), _sds((T, D), MXU)],
    )(*args)


def mm_res(y, w, x, *, name, tm=512):
    T, K = y.shape

    def body(y_ref, w_ref, x_ref, o_ref):
        o_ref[...] = x_ref[...] + _dot(y_ref[...], w_ref[...])

    return _call(
        body, name=name, grid=(T // tm,),
        in_specs=[pl.BlockSpec((tm, K), lambda i: (i, 0)),
                  pl.BlockSpec((K, D), lambda i: (0, 0)),
                  pl.BlockSpec((tm, D), lambda i: (i, 0))],
        out_specs=pl.BlockSpec((tm, D), lambda i: (i, 0)),
        out_shape=_sds((T, D), F32),
    )(y, w, x)


def mm_nt(a, wt, *, name, tm=512):
    T, K = a.shape
    N = wt.shape[1]
    w = wt

    def body(a_ref, w_ref, o_ref, ab_ref):
        ab = a_ref[...].astype(MXU)
        ab_ref[...] = ab
        o_ref[...] = _dot(ab, w_ref[...])

    return _call(
        body, name=name, grid=(T // tm,),
        in_specs=[pl.BlockSpec((tm, K), lambda i: (i, 0)),
                  pl.BlockSpec((K, N), lambda i: (0, 0))],
        out_specs=[pl.BlockSpec((tm, N), lambda i: (i, 0)),
                   pl.BlockSpec((tm, K), lambda i: (i, 0))],
        out_shape=[_sds((T, N), F32), _sds((T, K), MXU)],
    )(a, w)


def inproj_bwd(dp, wb, x, g, dres, *, name, tm=512):
    T = x.shape[0]
    G, ns, _ = wb.shape

    def body(dp_ref, w_ref, x_ref, g_ref, dres_ref, dx_ref, dg_ref, acc_ref):
        i = pl.program_id(0)
        j = pl.program_id(1)

        @pl.when(j == 0)
        def _():
            acc_ref[...] = jnp.zeros_like(acc_ref)

        @pl.when((i == 0) & (j == 0))
        def _():
            dg_ref[...] = jnp.zeros_like(dg_ref)

        acc_ref[...] += _dot(dp_ref[...], w_ref[...])

        @pl.when(j == G - 1)
        def _():
            dxn, dg = _rms_bwd(acc_ref[...], x_ref[...], g_ref[...])
            dx_ref[...] = dres_ref[...] + dxn
            dg_ref[...] += dg

    return _call(
        body, name=name, grid=(T // tm, G),
        in_specs=[pl.BlockSpec((tm, ns), lambda i, j: (i, j)),
                  pl.BlockSpec((None, ns, D), lambda i, j: (j, 0, 0)),
                  pl.BlockSpec((tm, D), lambda i, j: (i, 0)),
                  pl.BlockSpec((1, D), lambda i, j: (0, 0)),
                  pl.BlockSpec((tm, D), lambda i, j: (i, 0))],
        out_specs=[pl.BlockSpec((tm, D), lambda i, j: (i, 0)),
                   pl.BlockSpec((1, D), lambda i, j: (0, 0))],
        out_shape=[_sds((T, D), F32), _sds((1, D), F32)],
        scratch=[pltpu.VMEM((tm, D), F32)],
    )(dp, wb, x, g, dres)


def loss_head(x, gf, tgt, *, tm=512):
    T = x.shape[0]

    def body(x_ref, g_ref, t_ref, loss_ref, dx_ref, dg_ref):
        @pl.when(pl.program_id(0) == 0)
        def _():
            loss_ref[...] = jnp.zeros_like(loss_ref)
            dg_ref[...] = jnp.zeros_like(dg_ref)

        xv = x_ref[...]
        gv = g_ref[...]
        e = _rms(xv, gv) - t_ref[...]
        loss_ref[...] += (0.5 / D) * jnp.sum(e * e)
        dxn, dg = _rms_bwd(e * (1.0 / D), xv, gv)
        dx_ref[...] = dxn
        dg_ref[...] += dg

    return _call(
        body, name="loss_head", grid=(T // tm,),
        in_specs=[pl.BlockSpec((tm, D), lambda i: (i, 0)),
                  pl.BlockSpec((1, D), lambda i: (0, 0)),
                  pl.BlockSpec((tm, D), lambda i: (i, 0))],
        out_specs=[pl.BlockSpec((8, 128), lambda i: (0, 0)),
                   pl.BlockSpec((tm, D), lambda i: (i, 0)),
                   pl.BlockSpec((1, D), lambda i: (0, 0))],
        out_shape=[_sds((8, 128), F32), _sds((T, D), F32), _sds((1, D), F32)],
    )(x, gf, tgt)


def _glu(p):
    return p[:, :D] * _sigmoid(p[:, D:])


def _ln_stats(u):
    mu = jnp.mean(u, axis=-1, keepdims=True)
    xc = u - mu
    rstd = lax.rsqrt(jnp.mean(xc * xc, axis=-1, keepdims=True) + LN_EPS)
    return xc * rstd, rstd


def conv_fwd_core(p, dw, dwb, lng, lnb, *, S, tt=256):
    T = p.shape[0]
    nb = S // tt
    r = tt // HALO

    def body(pc_ref, pp_ref, dw_ref, dwb_ref, lng_ref, lnb_ref, u2_ref, u4_ref, ubuf):
        first = (pl.program_id(0) % nb) == 0
        ubuf[0:HALO, :] = jnp.where(first, 0.0, _glu(pp_ref[...]))
        ubuf[HALO:, :] = _glu(pc_ref[...])
        for c in range(D // 128):
            cs = slice(c * 128, (c + 1) * 128)
            acc = jnp.zeros((tt, 128), F32)
            for k in range(CONV_W):
                acc = acc + dw_ref[k:k + 1, cs] * ubuf[k + 2:k + 2 + tt, cs]
            u2_ref[:, cs] = acc + dwb_ref[:, cs]
        xh, _ = _ln_stats(u2_ref[...])
        u3 = xh * lng_ref[...] + lnb_ref[...]
        u4_ref[...] = (u3 * _sigmoid(u3)).astype(MXU)

    row = pl.BlockSpec((1, D), lambda i: (0, 0))
    return _call(
        body, name="conv_fwd_core", grid=(T // tt,),
        in_specs=[pl.BlockSpec((tt, 2 * D), lambda i: (i, 0)),
                  pl.BlockSpec((HALO, 2 * D), lambda i: (jnp.maximum(i * r - 1, 0), 0)),
                  pl.BlockSpec((HALO, D), lambda i: (0, 0)), row, row, row],
        out_specs=[pl.BlockSpec((tt, D), lambda i: (i, 0)), pl.BlockSpec((tt, D), lambda i: (i, 0))],
        out_shape=[_sds((T, D), F32), _sds((T, D), MXU)],
        scratch=[pltpu.VMEM((tt + HALO, D), F32)],
    )(p, p, dw, dwb, lng, lnb)


def conv_bwd_rows(dy, wout, u2, lng, lnb, *, tm=512):
    T = dy.shape[0]

    def body(dy_ref, w_ref, u2_ref, lng_ref, lnb_ref, du2_ref, dyb_ref, dlng_ref, dlnb_ref, ddwb_ref):
        @pl.when(pl.program_id(0) == 0)
        def _():
            dlng_ref[...] = jnp.zeros_like(dlng_ref)
            dlnb_ref[...] = jnp.zeros_like(dlnb_ref)
            ddwb_ref[...] = jnp.zeros_like(ddwb_ref)

        dyb = dy_ref[...].astype(MXU)
        dyb_ref[...] = dyb
        du4 = _dot(dyb, w_ref[...])
        xh, rstd = _ln_stats(u2_ref[...])
        lng_v = lng_ref[...]
        u3 = xh * lng_v + lnb_ref[...]
        s = _sigmoid(u3)
        du3 = du4 * (s * (1.0 + u3 * (1.0 - s)))
        dlng_ref[...] += jnp.sum(du3 * xh, axis=0, keepdims=True)
        dlnb_ref[...] += jnp.sum(du3, axis=0, keepdims=True)
        dxh = du3 * lng_v
        du2 = rstd * (dxh - jnp.mean(dxh, axis=-1, keepdims=True)
                      - xh * jnp.mean(dxh * xh, axis=-1, keepdims=True))
        du2_ref[...] = du2
        ddwb_ref[...] += jnp.sum(du2, axis=0, keepdims=True)

    row = pl.BlockSpec((1, D), lambda i: (0, 0))
    blk = pl.BlockSpec((tm, D), lambda i: (i, 0))
    return _call(
        body, name="conv_bwd_rows", grid=(T // tm,),
        in_specs=[blk, pl.BlockSpec((D, D), lambda i: (0, 0)), blk, row, row],
        out_specs=[blk, blk, row, row, row],
        out_shape=[_sds((T, D), F32), _sds((T, D), MXU), _sds((1, D), F32), _sds((1, D), F32), _sds((1, D), F32)],
    )(dy, wout, u2, lng, lnb)


def conv_bwd_core(du2, p, dw, *, S, tt=256):
    T = p.shape[0]
    nb = S // tt
    r = tt // HALO
    last_halo = T // HALO - 1

    def body(dc_ref, dn_ref, pc_ref, pp_ref, dw_ref, dp_ref, dbin_ref, ddw_ref, ubuf, dbuf):
        i = pl.program_id(0)

        @pl.when(i == 0)
        def _():
            dbin_ref[...] = jnp.zeros_like(dbin_ref)
            ddw_ref[...] = jnp.zeros_like(ddw_ref)

        first = (i % nb) == 0
        last = (i % nb) == nb - 1
        ubuf[0:HALO, :] = jnp.where(first, 0.0, _glu(pp_ref[...]))
        ubuf[HALO:, :] = _glu(pc_ref[...])
        dbuf[0:tt, :] = dc_ref[...]
        dbuf[tt:, :] = jnp.where(last, 0.0, dn_ref[...])
        pc = pc_ref[...]
        for c in range(D // 128):
            cs = slice(c * 128, (c + 1) * 128)
            dcur = dbuf[0:tt, cs]
            du = jnp.zeros((tt, 128), F32)
            for k in range(CONV_W):
                ddw_ref[k:k + 1, cs] += jnp.sum(dcur * ubuf[k + 2:k + 2 + tt, cs], axis=0, keepdims=True)
                du = du + dw_ref[k:k + 1, cs] * dbuf[CONV_W - 1 - k:CONV_W - 1 - k + tt, cs]
            a = pc[:, c * 128:(c + 1) * 128]
            sb = _sigmoid(pc[:, D + c * 128:D + (c + 1) * 128])
            da = du * sb
            db = du * a * sb * (1.0 - sb)
            dp_ref[:, cs] = da.astype(MXU)
            dp_ref[:, D + c * 128:D + (c + 1) * 128] = db.astype(MXU)
            dbin_ref[:, cs] += jnp.sum(da, axis=0, keepdims=True)
            dbin_ref[:, D + c * 128:D + (c + 1) * 128] += jnp.sum(db, axis=0, keepdims=True)

    return _call(
        body, name="conv_bwd_core", grid=(T // tt,),
        in_specs=[pl.BlockSpec((tt, D), lambda i: (i, 0)),
                  pl.BlockSpec((HALO, D), lambda i: (jnp.minimum((i + 1) * r, last_halo), 0)),
                  pl.BlockSpec((tt, 2 * D), lambda i: (i, 0)),
                  pl.BlockSpec((HALO, 2 * D), lambda i: (jnp.maximum(i * r - 1, 0), 0)),
                  pl.BlockSpec((HALO, D), lambda i: (0, 0))],
        out_specs=[pl.BlockSpec((tt, 2 * D), lambda i: (i, 0)),
                   pl.BlockSpec((1, 2 * D), lambda i: (0, 0)),
                   pl.BlockSpec((HALO, D), lambda i: (0, 0))],
        out_shape=[_sds((T, 2 * D), MXU), _sds((1, 2 * D), F32), _sds((HALO, D), F32)],
        scratch=[pltpu.VMEM((tt + HALO, D), F32), pltpu.VMEM((tt + HALO, D), F32)],
    )(du2, du2, p, p, dw)


PH = 16


def _pool_cnt(i, nb, tt, win):
    pos = (i % nb) * tt + lax.broadcasted_iota(jnp.int32, (tt, 1), 0)
    return jnp.minimum(pos + 1, win).astype(F32)


def pool_fwd(x, g, wp, scale, *, S, tt=256):
    T = x.shape[0]
    nb = S // tt
    r = tt // PH

    def body(xc_ref, xp_ref, g_ref, wp_ref, sc_ref, xo_ref, m_ref, hbuf):
        i = pl.program_id(0)
        first = (i % nb) == 0
        gv = g_ref[...]
        hbuf[0:PH, :] = jnp.where(first, 0.0, _rms(xp_ref[...], gv))
        xc = xc_ref[...]
        hbuf[PH:, :] = _rms(xc, gv)
        for gi, win in enumerate(POOL_WIN):
            gs = slice(gi * POOL_G, (gi + 1) * POOL_G)
            acc = hbuf[PH:PH + tt, gs]
            for j in range(1, win):
                acc = acc + hbuf[PH - j:PH - j + tt, gs]
            m = (acc / _pool_cnt(i, nb, tt, win) - hbuf[PH:PH + tt, gs]).astype(MXU)
            m_ref[:, gs] = m
            xo_ref[:, gs] = xc[:, gs] + _dot(m, wp_ref[gi]) * sc_ref[:, gs]

    row = pl.BlockSpec((1, D), lambda i: (0, 0))
    blk = pl.BlockSpec((tt, D), lambda i: (i, 0))
    return _call(
        body, name="pool_fwd", grid=(T // tt,),
        in_specs=[blk, pl.BlockSpec((PH, D), lambda i: (jnp.maximum(i * r - 1, 0), 0)), row,
                  pl.BlockSpec((len(POOL_WIN), POOL_G, POOL_G), lambda i: (0, 0, 0)), row],
        out_specs=[blk, blk],
        out_shape=[_sds((T, D), F32), _sds((T, D), MXU)],
        scratch=[pltpu.VMEM((tt + PH, D), F32)],
    )(x, x, g, wp, scale)


def pool_bwd_rows(dy, m, wp, scale, *, S, tt=256):
    T = dy.shape[0]
    nb = S // tt

    def body(dy_ref, m_ref, wp_ref, sc_ref, dmc_ref, dyp_ref, dsc_ref):
        i = pl.program_id(0)

        @pl.when(i == 0)
        def _():
            dsc_ref[...] = jnp.zeros_like(dsc_ref)

        for gi, win in enumerate(POOL_WIN):
            gs = slice(gi * POOL_G, (gi + 1) * POOL_G)
            dyg = dy_ref[:, gs]
            w = wp_ref[gi]
            dsc_ref[:, gs] += jnp.sum(dyg * _dot(m_ref[:, gs], w), axis=0, keepdims=True)
            dyp = (dyg * sc_ref[:, gs]).astype(MXU)
            dyp_ref[:, gs] = dyp
            dmc_ref[:, gs] = _dot_nt(dyp, w) / _pool_cnt(i, nb, tt, win)

    row = pl.BlockSpec((1, D), lambda i: (0, 0))
    blk = pl.BlockSpec((tt, D), lambda i: (i, 0))
    return _call(
        body, name="pool_bwd_rows", grid=(T // tt,),
        in_specs=[blk, blk, pl.BlockSpec((len(POOL_WIN), POOL_G, POOL_G), lambda i: (0, 0, 0)), row],
        out_specs=[blk, blk, row],
        out_shape=[_sds((T, D), F32), _sds((T, D), MXU), _sds((1, D), F32)],
    )(dy, m, wp, scale)


def pool_bwd_core(dmc, x, g, dres, *, S, tt=256):
    T = x.shape[0]
    nb = S // tt
    r = tt // PH
    last_halo = T // PH - 1

    def body(dc_ref, dn_ref, x_ref, g_ref, dres_ref, dx_ref, dg_ref, dbuf, dh_buf):
        i = pl.program_id(0)

        @pl.when(i == 0)
        def _():
            dg_ref[...] = jnp.zeros_like(dg_ref)

        last = (i % nb) == nb - 1
        dbuf[0:tt, :] = dc_ref[...]
        dbuf[tt:, :] = jnp.where(last, 0.0, dn_ref[...])
        for gi, win in enumerate(POOL_WIN):
            gs = slice(gi * POOL_G, (gi + 1) * POOL_G)
            cur = dbuf[0:tt, gs]
            acc = cur
            for j in range(1, win):
                acc = acc + dbuf[j:j + tt, gs]
            dh_buf[:, gs] = acc - cur * _pool_cnt(i, nb, tt, win)
        dxn, dg = _rms_bwd(dh_buf[...], x_ref[...], g_ref[...])
        dx_ref[...] = dres_ref[...] + dxn
        dg_ref[...] += dg

    row = pl.BlockSpec((1, D), lambda i: (0, 0))
    blk = pl.BlockSpec((tt, D), lambda i: (i, 0))
    return _call(
        body, name="pool_bwd_core", grid=(T // tt,),
        in_specs=[blk, pl.BlockSpec((PH, D), lambda i: (jnp.minimum((i + 1) * r, last_halo), 0)), blk, row, blk],
        out_specs=[blk, row],
        out_shape=[_sds((T, D), F32), _sds((1, D), F32)],
        scratch=[pltpu.VMEM((tt + PH, D), F32), pltpu.VMEM((tt, D), F32)],
    )(dmc, dmc, x, g, dres)


NEG = -1e30


def _tri(n, upper=False):
    r = lax.broadcasted_iota(jnp.int32, (n, n), 0)
    c = lax.broadcasted_iota(jnp.int32, (n, n), 1)
    return (r <= c if upper else r >= c).astype(F32)


def _dot_hi(a, b):
    return jnp.dot(a, b, preferred_element_type=F32, precision=lax.Precision.HIGHEST)


def _log_sigmoid(z):
    return jnp.minimum(z, 0.0) - jnp.log(1.0 + jnp.exp(-jnp.abs(z)))


def fox_cum(fl, bf, *, S, tt=256):
    T = fl.shape[0]
    nb = S // tt

    def body(fl_ref, bf_ref, c_ref, ct_ref, carry):
        i = pl.program_id(0)

        @pl.when((i % nb) == 0)
        def _():
            carry[...] = jnp.zeros_like(carry)

        lf = _log_sigmoid(fl_ref[...] + bf_ref[...])
        c = _dot_hi(_tri(tt), lf) + carry[...]
        c_ref[...] = c
        carry[...] = c[tt - 1:tt, :]
        ct_ref[...] = c.T[0:FOX_H, :]

    return _call(
        body, name="fox_cum", grid=(T // tt,),
        in_specs=[pl.BlockSpec((tt, 128), lambda i: (i, 0)), pl.BlockSpec((1, 128), lambda i: (0, 0))],
        out_specs=[pl.BlockSpec((tt, 128), lambda i: (i, 0)),
                   pl.BlockSpec((None, FOX_H, tt), lambda i: (i // nb, 0, i % nb))],
        out_shape=[_sds((T, 128), F32), _sds((T // S, FOX_H, S), F32)],
        scratch=[pltpu.VMEM((1, 128), F32)],
    )(fl, bf)


def _fox_logits(q_ref, k_ref, c_ref, ct_ref, h, diag, tq):
    hs = slice(h * FOX_DH, (h + 1) * FOX_DH)
    s = _dot_nt(q_ref[:, hs], k_ref[:, hs]) * (FOX_DH ** -0.5) + (c_ref[:, h:h + 1] - ct_ref[h:h + 1, :])
    if diag:
        r = lax.broadcasted_iota(jnp.int32, (tq, tq), 0)
        c = lax.broadcasted_iota(jnp.int32, (tq, tq), 1)
        s = jnp.where(r >= c, s, NEG)
    return s


def _fox_specs(tq, nq, q_of, k_of):
    qrow = lambda col: pl.BlockSpec((tq, D), lambda b, i, j: (b * nq + q_of(i, j), col))
    krow = lambda col: pl.BlockSpec((tq, D), lambda b, i, j: (b * nq + k_of(i, j), col))
    qvec = pl.BlockSpec((tq, 128), lambda b, i, j: (b * nq + q_of(i, j), 0))
    kvec = pl.BlockSpec((tq, 128), lambda b, i, j: (b * nq + k_of(i, j), 0))
    ct = pl.BlockSpec((None, FOX_H, tq), lambda b, i, j: (b, 0, k_of(i, j)))
    return qrow, krow, qvec, kvec, ct


def fox_fwd(qkv, c, ct, *, S, tq=256):
    T = qkv.shape[0]
    nq = S // tq

    def body(q_ref, k_ref, v_ref, c_ref, ct_ref, o_ref, o32_ref, lse_ref, m_sc, l_sc, acc, acc_lo):
        qi = pl.program_id(1)
        ki = pl.program_id(2)

        @pl.when(ki == 0)
        def _():
            m_sc[...] = jnp.full_like(m_sc, NEG)
            l_sc[...] = jnp.zeros_like(l_sc)
            acc[...] = jnp.zeros_like(acc)
            acc_lo[...] = jnp.zeros_like(acc_lo)

        def step(diag):
            for h in range(FOX_H):
                hs = slice(h * FOX_DH, (h + 1) * FOX_DH)
                s = _fox_logits(q_ref, k_ref, c_ref, ct_ref, h, diag, tq)
                m_prev = m_sc[:, h:h + 1]
                m_new = jnp.maximum(m_prev, jnp.max(s, axis=-1, keepdims=True))
                alpha = jnp.exp(m_prev - m_new)
                p = jnp.exp(s - m_new)
                l_sc[:, h:h + 1] = alpha * l_sc[:, h:h + 1] + jnp.sum(p, axis=-1, keepdims=True)
                hi, lo = _split(p)
                acc[:, hs] = alpha * acc[:, hs] + _dot(hi, v_ref[:, hs])
                acc_lo[:, hs] = alpha * acc_lo[:, hs] + _dot(lo, v_ref[:, hs])
                m_sc[:, h:h + 1] = m_new

        @pl.when(ki < qi)
        def _():
            step(False)

        @pl.when(ki == qi)
        def _():
            step(True)
            for h in range(FOX_H):
                hs = slice(h * FOX_DH, (h + 1) * FOX_DH)
                o_ref[:, hs] = (acc[:, hs] / l_sc[:, h:h + 1]).astype(MXU)
                o32_ref[:, hs] = (acc[:, hs] + acc_lo[:, hs]) / l_sc[:, h:h + 1]
            lse_ref[...] = m_sc[...] + jnp.log(jnp.maximum(l_sc[...], 1e-37))

    qrow, krow, qvec, kvec, ctspec = _fox_specs(tq, nq, lambda i, j: i, lambda i, j: jnp.minimum(i, j))
    return _call(
        body, name="fox_fwd", grid=(T // S, nq, nq),
        in_specs=[qrow(0), krow(1), krow(2), qvec, ctspec],
        out_specs=[qrow(0), qrow(0), qvec],
        out_shape=[_sds((T, D), MXU), _sds((T, D), F32), _sds((T, 128), F32)],
        scratch=[pltpu.VMEM((tq, 128), F32), pltpu.VMEM((tq, 128), F32), pltpu.VMEM((tq, D), F32),
                 pltpu.VMEM((tq, D), F32)],
    )(qkv, qkv, qkv, c, ct)


def fox_bwd_dq(qkv, do, o, lse, c, ct, *, S, tq=256):
    T = qkv.shape[0]
    nq = S // tq

    def body(q_ref, k_ref, v_ref, do32_ref, o_ref, lse_ref, c_ref, ct_ref, dq_ref, dl_ref, do_ref, acc):
        qi = pl.program_id(1)
        ki = pl.program_id(2)

        @pl.when(ki == 0)
        def _():
            acc[...] = jnp.zeros_like(acc)
            dl_ref[...] = jnp.zeros_like(dl_ref)
            do_ref[...] = do32_ref[...].astype(MXU)
            for h in range(FOX_H):
                hs = slice(h * FOX_DH, (h + 1) * FOX_DH)
                dl_ref[:, h:h + 1] = jnp.sum(do_ref[:, hs].astype(F32) * o_ref[:, hs], axis=-1, keepdims=True)

        def step(diag):
            for h in range(FOX_H):
                hs = slice(h * FOX_DH, (h + 1) * FOX_DH)
                s = _fox_logits(q_ref, k_ref, c_ref, ct_ref, h, diag, tq)
                p = jnp.exp(s - lse_ref[:, h:h + 1])
                dp = _dot_nt(do_ref[:, hs], v_ref[:, hs])
                hi, lo = _split(p * (dp - dl_ref[:, h:h + 1]))
                acc[:, hs] += _dot(hi, k_ref[:, hs]) + _dot(lo, k_ref[:, hs])

        @pl.when(ki < qi)
        def _():
            step(False)

        @pl.when(ki == qi)
        def _():
            step(True)
            dq_ref[...] = (acc[...] * (FOX_DH ** -0.5)).astype(MXU)

    qrow, krow, qvec, kvec, ctspec = _fox_specs(tq, nq, lambda i, j: i, lambda i, j: jnp.minimum(i, j))
    orow = pl.BlockSpec((tq, D), lambda b, i, j: (b * nq + i, 0))
    return _call(
        body, name="fox_bwd_dq", grid=(T // S, nq, nq),
        in_specs=[qrow(0), krow(1), krow(2), orow, orow, qvec, qvec, ctspec],
        out_specs=[orow, qvec, orow],
        out_shape=[_sds((T, D), MXU), _sds((T, 128), F32), _sds((T, D), MXU)],
        scratch=[pltpu.VMEM((tq, D), F32)],
    )(qkv, qkv, qkv, do, o, lse, c, ct)


def fox_bwd_dkv(qkv, do, lse, delta, c, ct, *, S, tq=256):
    T = qkv.shape[0]
    nq = S // tq

    def body(q_ref, k_ref, v_ref, do_ref, lse_ref, dl_ref, c_ref, ct_ref, dk_ref, dv_ref, dck_ref, dk_acc, dv_acc):
        ki = pl.program_id(1)
        qi = pl.program_id(2)

        @pl.when(qi == 0)
        def _():
            dk_acc[...] = jnp.zeros_like(dk_acc)
            dv_acc[...] = jnp.zeros_like(dv_acc)
            dck_ref[...] = jnp.zeros_like(dck_ref)

        def step(diag):
            ones = jnp.ones((tq, 128), MXU)
            for h in range(FOX_H):
                hs = slice(h * FOX_DH, (h + 1) * FOX_DH)
                s = _fox_logits(q_ref, k_ref, c_ref, ct_ref, h, diag, tq)
                p = jnp.exp(s - lse_ref[:, h:h + 1])
                doh = do_ref[:, hs]
                dv_acc[:, hs] += _dot_tn(p.astype(MXU), doh)
                dp = _dot_nt(doh, v_ref[:, hs])
                hi, lo = _split(p * (dp - dl_ref[:, h:h + 1]))
                dk_acc[:, hs] += _dot_tn(hi, q_ref[:, hs]) + _dot_tn(lo, q_ref[:, hs])
                dck_ref[:, h:h + 1] += (_dot_tn(hi, ones) + _dot_tn(lo, ones))[:, 0:1]

        @pl.when(qi > ki)
        def _():
            step(False)

        @pl.when(qi == ki)
        def _():
            step(True)

        @pl.when(qi == nq - 1)
        def _():
            dk_ref[...] = (dk_acc[...] * (FOX_DH ** -0.5)).astype(MXU)
            dv_ref[...] = dv_acc[...].astype(MXU)

    qrow, krow, qvec, kvec, ctspec = _fox_specs(tq, nq, lambda i, j: jnp.maximum(i, j), lambda i, j: i)
    qo = pl.BlockSpec((tq, D), lambda b, i, j: (b * nq + jnp.maximum(i, j), 0))
    ko = pl.BlockSpec((tq, D), lambda b, i, j: (b * nq + i, 0))
    return _call(
        body, name="fox_bwd_dkv", grid=(T // S, nq, nq),
        in_specs=[qrow(0), krow(1), krow(2), qo, qvec, qvec, qvec, ctspec],
        out_specs=[ko, ko, kvec],
        out_shape=[_sds((T, D), MXU), _sds((T, D), MXU), _sds((T, 128), F32)],
        scratch=[pltpu.VMEM((tq, D), F32), pltpu.VMEM((tq, D), F32)],
    )(qkv, qkv, qkv, do, lse, delta, c, ct)


HW = 128
COL_ONE = FOX_DH + 3
COL_LSE = FOX_DH + 6


def _parts(x):
    hi = x.astype(MXU).astype(F32)
    mid = (x - hi).astype(MXU).astype(F32)
    lo = (x - hi - mid).astype(MXU).astype(F32)
    return [hi, mid, lo]


def _aug(n, cols):
    lane = lax.broadcasted_iota(jnp.int32, (n, HW - FOX_DH), 1)
    out = jnp.zeros((n, HW - FOX_DH), F32)
    for i, cval in enumerate(cols):
        out = jnp.where(lane == i, cval, out)
    return out


def fox_prep(p, c, *, tt=256):
    T = p.shape[0]

    def body(q_ref, k_ref, v_ref, c_ref, qa_ref, ka_ref, va_ref):
        ones = [1.0, 1.0, 1.0]
        for h in range(FOX_H):
            hs = slice(h * FOX_DH, (h + 1) * FOX_DH)
            lo, mid = h * HW, h * HW + FOX_DH
            cp = _parts(c_ref[:, h:h + 1])
            qa_ref[:, lo:mid] = (q_ref[:, hs].astype(F32) * (FOX_DH ** -0.5)).astype(MXU)
            qa_ref[:, mid:lo + HW] = _aug(tt, cp + ones).astype(MXU)
            ka_ref[:, lo:mid] = k_ref[:, hs]
            ka_ref[:, mid:lo + HW] = _aug(tt, ones + [-x for x in cp] + ones).astype(MXU)
            va_ref[:, lo:mid] = v_ref[:, hs]
            va_ref[:, mid:lo + HW] = _aug(tt, ones).astype(MXU)

    wide = pl.BlockSpec((tt, FOX_H * HW), lambda i: (i, 0))
    col = lambda k: pl.BlockSpec((tt, D), lambda i: (i, k))
    return _call(body, name="fox_prep", grid=(T // tt,),
                 in_specs=[col(0), col(1), col(2), pl.BlockSpec((tt, 128), lambda i: (i, 0))],
                 out_specs=[wide, wide, wide], out_shape=[_sds((T, FOX_H * HW), MXU)] * 3)(p, p, p, c)


def _causal(x, fill):
    r = lax.broadcasted_iota(jnp.int32, x.shape, 0)
    c = lax.broadcasted_iota(jnp.int32, x.shape, 1)
    return jnp.where(r >= c, x, fill)


def _wide_specs(tq, nq, q_of, k_of):
    qs = pl.BlockSpec((tq, FOX_H * HW), lambda b, i, j: (b * nq + q_of(i, j), 0))
    ks = pl.BlockSpec((tq, FOX_H * HW), lambda b, i, j: (b * nq + k_of(i, j), 0))
    return qs, ks


def fox2_fwd(qa, ka, va, *, S, tq=256, tk=512):
    T = qa.shape[0]
    tk = min(tk, S)
    nq = S // tq
    nk = S // tk
    r = tk // tq

    def body(q_ref, k_ref, v_ref, o_ref, o32_ref, lse_ref, m_sc, acc, acc_lo):
        qi = pl.program_id(1)
        ki = pl.program_id(2)
        last = qi // r

        @pl.when(ki == 0)
        def _():
            m_sc[...] = jnp.full_like(m_sc, NEG)
            acc[...] = jnp.zeros_like(acc)
            acc_lo[...] = jnp.zeros_like(acc_lo)

        def step(diag):
            for h in range(FOX_H):
                ws = slice(h * HW, (h + 1) * HW)
                s = _dot_nt(q_ref[:, ws], k_ref[:, ws])
                if diag:
                    row = qi * tq + lax.broadcasted_iota(jnp.int32, s.shape, 0)
                    col = ki * tk + lax.broadcasted_iota(jnp.int32, s.shape, 1)
                    s = jnp.where(row >= col, s, NEG)
                m_prev = m_sc[h]
                m_new = jnp.maximum(m_prev, jnp.max(s, axis=-1, keepdims=True))
                alpha = jnp.exp(m_prev - m_new)
                hi, lo = _split(jnp.exp(s - jnp.tile(m_new, (1, tk // 128))))
                acc[h] = alpha * acc[h] + _dot(hi, v_ref[:, ws])
                acc_lo[h] = alpha * acc_lo[h] + _dot(lo, v_ref[:, ws])
                m_sc[h] = m_new

        @pl.when(ki < last)
        def _():
            step(False)

        @pl.when(ki == last)
        def _():
            step(True)
            for h in range(FOX_H):
                hs = slice(h * FOX_DH, (h + 1) * FOX_DH)
                full = acc[h] + acc_lo[h]
                l = full[:, FOX_DH:FOX_DH + 1]
                o_ref[:, hs] = (acc[h][:, :FOX_DH] / l).astype(MXU)
                o32_ref[:, hs] = full[:, :FOX_DH] / l
                lse_ref[:, h:h + 1] = m_sc[h][:, 0:1] + jnp.log(l)

    qs = pl.BlockSpec((tq, FOX_H * HW), lambda b, i, j: (b * nq + i, 0))
    ks = pl.BlockSpec((tk, FOX_H * HW), lambda b, i, j: (b * nk + jnp.minimum(j, i // r), 0))
    orow = pl.BlockSpec((tq, D), lambda b, i, j: (b * nq + i, 0))
    return _call(
        body, name="fox_fwd", grid=(T // S, nq, nk),
        in_specs=[qs, ks, ks],
        out_specs=[orow, orow, pl.BlockSpec((tq, 128), lambda b, i, j: (b * nq + i, 0))],
        out_shape=[_sds((T, D), MXU), _sds((T, D), F32), _sds((T, 128), F32)],
        scratch=[pltpu.VMEM((FOX_H, tq, 128), F32), pltpu.VMEM((FOX_H, tq, HW), F32),
                 pltpu.VMEM((FOX_H, tq, HW), F32)],
    )(qa, ka, va)


def fox2_prep_bwd(do, o32, lse, qa, *, tt=256):
    T = do.shape[0]

    def body(do_ref, o_ref, lse_ref, qa_ref, qb_ref, da_ref):
        lane = lax.broadcasted_iota(jnp.int32, (tt, HW), 1)
        dob = do_ref[...].astype(MXU)
        for h in range(FOX_H):
            hs = slice(h * FOX_DH, (h + 1) * FOX_DH)
            ws = slice(h * HW, (h + 1) * HW)
            doh = dob[:, hs]
            delta = jnp.sum(doh.astype(F32) * o_ref[:, hs], axis=-1, keepdims=True)
            da_ref[:, h * HW:h * HW + FOX_DH] = doh
            da_ref[:, h * HW + FOX_DH:(h + 1) * HW] = _aug(tt, [-x for x in _parts(delta)]).astype(MXU)
            tile = qa_ref[:, ws]
            for i, part in enumerate(_parts(lse_ref[:, h:h + 1])):
                tile = jnp.where(lane == COL_LSE + i, (-part).astype(MXU), tile)
            qb_ref[:, ws] = tile

    wide = pl.BlockSpec((tt, FOX_H * HW), lambda i: (i, 0))
    blk = pl.BlockSpec((tt, D), lambda i: (i, 0))
    return _call(body, name="fox_prep_bwd", grid=(T // tt,),
                 in_specs=[blk, blk, pl.BlockSpec((tt, 128), lambda i: (i, 0)), wide],
                 out_specs=[wide, wide],
                 out_shape=[_sds((T, FOX_H * HW), MXU), _sds((T, FOX_H * HW), MXU)],
                 )(do, o32, lse, qa)


def fox2_dq(qb, ka, va, da, *, S, tq=256):
    T = qb.shape[0]
    nq = S // tq

    def body(q_ref, k_ref, v_ref, d_ref, dq_ref, acc):
        qi = pl.program_id(1)
        ki = pl.program_id(2)

        @pl.when(ki == 0)
        def _():
            acc[...] = jnp.zeros_like(acc)

        def step(diag):
            for h in range(FOX_H):
                ws = slice(h * HW, (h + 1) * HW)
                kh = k_ref[:, ws]
                p = jnp.exp(_dot_nt(q_ref[:, ws], kh))
                if diag:
                    p = _causal(p, 0.0)
                hi, lo = _split(p * _dot_nt(d_ref[:, ws], v_ref[:, ws]))
                acc[h] += _dot(hi, kh) + _dot(lo, kh)

        @pl.when(ki < qi)
        def _():
            step(False)

        @pl.when(ki == qi)
        def _():
            step(True)
            for h in range(FOX_H):
                dq_ref[:, h * FOX_DH:(h + 1) * FOX_DH] = (acc[h][:, :FOX_DH] * (FOX_DH ** -0.5)).astype(MXU)

    qs, ks = _wide_specs(tq, nq, lambda i, j: i, lambda i, j: jnp.minimum(i, j))
    return _call(
        body, name="fox_bwd_dq", grid=(T // S, nq, nq),
        in_specs=[qs, ks, ks, qs],
        out_specs=pl.BlockSpec((tq, D), lambda b, i, j: (b * nq + i, 0)),
        out_shape=_sds((T, D), MXU),
        scratch=[pltpu.VMEM((FOX_H, tq, HW), F32)],
    )(qb, ka, va, da)


def fox2_dkv(qb, ka, va, da, *, S, tq=256):
    T = qb.shape[0]
    nq = S // tq

    def body(q_ref, k_ref, v_ref, d_ref, dk_ref, dv_ref, dck_ref, dk_acc, dv_acc):
        ki = pl.program_id(1)
        qi = pl.program_id(2)

        @pl.when(qi == 0)
        def _():
            dk_acc[...] = jnp.zeros_like(dk_acc)
            dv_acc[...] = jnp.zeros_like(dv_acc)

        def step(diag):
            for h in range(FOX_H):
                ws = slice(h * HW, (h + 1) * HW)
                qh = q_ref[:, ws]
                dh = d_ref[:, ws]
                pt = jnp.exp(_dot_nt(k_ref[:, ws], qh))
                if diag:
                    r = lax.broadcasted_iota(jnp.int32, pt.shape, 0)
                    c = lax.broadcasted_iota(jnp.int32, pt.shape, 1)
                    pt = jnp.where(r <= c, pt, 0.0)
                dv_acc[h] += _dot(pt.astype(MXU), dh)
                hi, lo = _split(pt * _dot_nt(v_ref[:, ws], dh))
                dk_acc[h] += _dot(hi, qh) + _dot(lo, qh)

        @pl.when(qi > ki)
        def _():
            step(False)

        @pl.when(qi == ki)
        def _():
            step(True)

        @pl.when(qi == nq - 1)
        def _():
            dck_ref[...] = jnp.zeros_like(dck_ref)
            for h in range(FOX_H):
                hs = slice(h * FOX_DH, (h + 1) * FOX_DH)
                dk_ref[:, hs] = dk_acc[h][:, :FOX_DH].astype(MXU)
                dv_ref[:, hs] = dv_acc[h][:, :FOX_DH].astype(MXU)
                dck_ref[:, h:h + 1] = dk_acc[h][:, COL_ONE:COL_ONE + 1]

    qs, ks = _wide_specs(tq, nq, lambda i, j: jnp.maximum(i, j), lambda i, j: i)
    ko = pl.BlockSpec((tq, D), lambda b, i, j: (b * nq + i, 0))
    return _call(
        body, name="fox_bwd_dkv", grid=(T // S, nq, nq),
        in_specs=[qs, ks, ks, qs],
        out_specs=[ko, ko, pl.BlockSpec((tq, 128), lambda b, i, j: (b * nq + i, 0))],
        out_shape=[_sds((T, D), MXU), _sds((T, D), MXU), _sds((T, 128), F32)],
        scratch=[pltpu.VMEM((FOX_H, tq, HW), F32), pltpu.VMEM((FOX_H, tq, HW), F32)],
    )(qb, ka, va, da)


def fox_fin(dck, fl, bf, *, S, tt=256):
    T = fl.shape[0]
    nb = S // tt
    nblk = T // tt

    def body(dck_ref, fl_ref, bf_ref, dfl_ref, dbf_ref, carry):
        i = pl.program_id(0)

        @pl.when(i == 0)
        def _():
            dbf_ref[...] = jnp.zeros_like(dbf_ref)

        @pl.when((i % nb) == 0)
        def _():
            carry[...] = jnp.zeros_like(carry)

        lane = lax.broadcasted_iota(jnp.int32, (tt, 128), 1)
        dc = jnp.where(lane < FOX_H, -dck_ref[...], 0.0)
        dlf = _dot_hi(_tri(tt, upper=True), dc) + carry[...]
        carry[...] = dlf[0:1, :]
        dfl = dlf * _sigmoid(-(fl_ref[...] + bf_ref[...]))
        dfl_ref[...] = dfl.astype(MXU)
        dbf_ref[...] += jnp.sum(dfl, axis=0, keepdims=True)

    rev = pl.BlockSpec((tt, 128), lambda i: (nblk - 1 - i, 0))
    row = pl.BlockSpec((1, 128), lambda i: (0, 0))
    return _call(
        body, name="fox_fin", grid=(nblk,),
        in_specs=[rev, rev, row],
        out_specs=[rev, row],
        out_shape=[_sds((T, 128), MXU), _sds((1, 128), F32)],
        scratch=[pltpu.VMEM((1, 128), F32)],
    )(dck, fl, bf)


def lb_fwd(logits):
    def body(l_ref, lb_ref):
        lv = l_ref[...]
        e = jnp.exp(lv - jnp.max(lv, axis=0, keepdims=True))
        p = e / jnp.sum(e, axis=0, keepdims=True)
        lb_ref[...] = p[1:2, :] + p[2:3, :]

    return _call(body, name="lb_fwd", grid=(1,),
                 in_specs=[pl.BlockSpec((DEPTH, D), lambda i: (0, 0))],
                 out_specs=pl.BlockSpec((1, D), lambda i: (0, 0)),
                 out_shape=_sds((1, D), F32))(logits)


def lb_bwd(logits, dlb):
    def body(l_ref, d_ref, o_ref):
        lv = l_ref[...]
        e = jnp.exp(lv - jnp.max(lv, axis=0, keepdims=True))
        p = e / jnp.sum(e, axis=0, keepdims=True)
        lb = p[1:2, :] + p[2:3, :]
        row = lax.broadcasted_iota(jnp.int32, (DEPTH, D), 0)
        sel = ((row == 1) | (row == 2)).astype(F32)
        o_ref[...] = p * (sel - lb) * d_ref[...]

    return _call(body, name="lb_bwd", grid=(1,),
                 in_specs=[pl.BlockSpec((DEPTH, D), lambda i: (0, 0)), pl.BlockSpec((1, D), lambda i: (0, 0))],
                 out_specs=pl.BlockSpec((DEPTH, D), lambda i: (0, 0)),
                 out_shape=_sds((DEPTH, D), F32))(logits, dlb)


def _hgrn_gates(qr, fr, lb):
    sg = _sigmoid(fr)
    sneg = _sigmoid(-fr)
    f = lb + (1.0 - lb) * sg
    kk = (1.0 - lb) * sneg
    G = _dot_hi(_tri(HG_C), jnp.log(f))
    eG = jnp.exp(G)
    einv = jnp.exp(-G)
    elast = jnp.exp(G[HG_C - 1:HG_C, :] - G)
    q = qr * _sigmoid(qr)
    return dict(q=q, kk=kk, f=f, sg=sg, sneg=sneg, eG=eG, einv=einv, elast=elast,
                qg=q * eG, kinv=kk * einv, khat=kk * elast, glast=jnp.exp(G[HG_C - 1:HG_C, :]))


def _tril_mask(x):
    r = lax.broadcasted_iota(jnp.int32, x.shape, 0)
    c = lax.broadcasted_iota(jnp.int32, x.shape, 1)
    return jnp.where(r >= c, x, 0.0)


def hgrn_fwd(p, lb, ng, *, S, R=128):
    T = p.shape[0]
    nr = S // R
    ncr = R // HG_C

    def body(q_ref, f_ref, v_ref, gt_ref, lb_ref, ng_ref, y_ref, o_ref, st_ref, st):
        @pl.when(pl.program_id(1) == 0)
        def _():
            st[...] = jnp.zeros_like(st)

        lbv = lb_ref[...]
        for ch in range(ncr):
            rows = slice(ch * HG_C, (ch + 1) * HG_C)
            gt = _hgrn_gates(q_ref[rows, :], f_ref[rows, :], lbv)
            for h in range(HG_H):
                hs = slice(h * HG_DK, (h + 1) * HG_DK)
                sp = st[h]
                st_ref[ch, h] = sp
                qg = gt["qg"][:, hs].astype(MXU)
                vh = v_ref[rows, hs].astype(MXU)
                A = _tril_mask(_dot_nt(qg, gt["kinv"][:, hs].astype(MXU)))
                o_ref[rows, hs] = _dot_nt(qg, sp.astype(MXU)) + _dot(A.astype(MXU), vh)
                st[h] = sp * gt["glast"][:, hs] + _dot_tn(vh, gt["khat"][:, hs].astype(MXU))
        gate = gt_ref[...]
        sgate = gate * _sigmoid(gate)
        for h in range(HG_H):
            hs = slice(h * HG_DK, (h + 1) * HG_DK)
            oh = o_ref[:, hs]
            r = lax.rsqrt(jnp.mean(oh * oh, axis=-1, keepdims=True) + RMS_EPS)
            y_ref[:, hs] = (oh * r * ng_ref[:, hs] * sgate[:, hs]).astype(MXU)

    col = lambda c: pl.BlockSpec((R, D), lambda b, i: (b * nr + i, c))
    row = pl.BlockSpec((1, D), lambda b, i: (0, 0))
    return _call(
        body, name="hgrn_fwd", grid=(T // S, nr),
        in_specs=[col(0), col(1), col(2), col(3), row, row],
        out_specs=[col(0), col(0),
                   pl.BlockSpec((ncr, HG_H, HG_DK, HG_DK), lambda b, i: (b * nr + i, 0, 0, 0))],
        out_shape=[_sds((T, D), MXU), _sds((T, D), F32), _sds((T // HG_C, HG_H, HG_DK, HG_DK), F32)],
        scratch=[pltpu.VMEM((HG_H, HG_DK, HG_DK), F32)],
    )(p, p, p, p, lb, ng)


def hgrn_bwd(p, o, dyo, states, lb, ng, *, S, R=128):
    T = p.shape[0]
    nr = S // R
    ncr = R // HG_C

    def body(q_ref, f_ref, v_ref, gt_ref, o_ref, dy_ref, st_ref, lb_ref, ng_ref,
             dp_ref, dlb_ref, dng_ref, dst, do_buf, dG_buf, dqb, dkb):
        b = pl.program_id(0)
        i = pl.program_id(1)

        @pl.when(i == 0)
        def _():
            dst[...] = jnp.zeros_like(dst)

        @pl.when((b == 0) & (i == 0))
        def _():
            dlb_ref[...] = jnp.zeros_like(dlb_ref)
            dng_ref[...] = jnp.zeros_like(dng_ref)

        lbv = lb_ref[...]
        gate = gt_ref[...]
        sg_gate = _sigmoid(gate)
        silu_gate = gate * sg_gate
        for h in range(HG_H):
            hs = slice(h * HG_DK, (h + 1) * HG_DK)
            oh = o_ref[:, hs]
            r = lax.rsqrt(jnp.mean(oh * oh, axis=-1, keepdims=True) + RMS_EPS)
            ohat = oh * r
            dyh = dy_ref[:, hs]
            ngh = ng_ref[:, hs]
            dng_ref[:, hs] += jnp.sum(dyh * silu_gate[:, hs] * ohat, axis=0, keepdims=True)
            dp_ref[:, 3 * D + h * HG_DK:3 * D + (h + 1) * HG_DK] = (
                dyh * ohat * ngh * (sg_gate[:, hs] * (1.0 + gate[:, hs] * (1.0 - sg_gate[:, hs])))).astype(MXU)
            dn = dyh * ngh * silu_gate[:, hs]
            do_buf[:, hs] = r * (dn - ohat * jnp.mean(dn * ohat, axis=-1, keepdims=True))

        lastrow = lax.broadcasted_iota(jnp.int32, (HG_C, HG_DK), 0) == HG_C - 1
        for ch in reversed(range(ncr)):
            rows = slice(ch * HG_C, (ch + 1) * HG_C)
            qr = q_ref[rows, :]
            gt = _hgrn_gates(qr, f_ref[rows, :], lbv)
            for h in range(HG_H):
                hs = slice(h * HG_DK, (h + 1) * HG_DK)
                sp = st_ref[ch, h]
                ds = dst[h]
                qg32, kinv32, khat32 = gt["qg"][:, hs], gt["kinv"][:, hs], gt["khat"][:, hs]
                qg, kinv, khat = qg32.astype(MXU), kinv32.astype(MXU), khat32.astype(MXU)
                vh = v_ref[rows, hs].astype(MXU)
                doh = do_buf[rows, hs].astype(MXU)
                dsb = ds.astype(MXU)
                A = _tril_mask(_dot_nt(qg, kinv)).astype(MXU)
                dA = _tril_mask(_dot_nt(doh, vh)).astype(MXU)
                dqg = _dot(doh, sp.astype(MXU)) + _dot(dA, kinv)
                dkinv = _dot_tn(dA, qg)
                dp_ref[rows, 2 * D + h * HG_DK:2 * D + (h + 1) * HG_DK] = (
                    _dot_tn(A, doh) + _dot_nt(khat, dsb)).astype(MXU)
                dkhat = _dot(vh, dsb)
                glast = gt["glast"][:, hs]
                qg32, kinv32, khat32 = qg.astype(F32), kinv.astype(F32), khat.astype(F32)
                extra = (glast * jnp.sum(dsb.astype(F32) * sp.astype(MXU).astype(F32), axis=0, keepdims=True)
                         + jnp.sum(dkhat * khat32, axis=0, keepdims=True))
                dst[h] = ds * glast + _dot_tn(doh, qg)
                dG = dqg * qg32 - dkinv * kinv32 - dkhat * khat32
                dG_buf[:, hs] = dG + jnp.where(lastrow, extra, 0.0)
                dqb[:, hs] = dqg * gt["eG"][:, hs]
                dkb[:, hs] = dkinv * gt["einv"][:, hs] + dkhat * gt["elast"][:, hs]
            dg = _dot_hi(_tri(HG_C, upper=True), dG_buf[...])
            dk = dkb[...]
            sneg, f = gt["sneg"], gt["f"]
            c1 = (1.0 - lbv) * gt["sg"] * sneg
            dp_ref[rows, D:2 * D] = (dg * c1 / f - dk * c1).astype(MXU)
            dlb_ref[...] += jnp.sum(dg * sneg / f - dk * sneg, axis=0, keepdims=True)
            sq = _sigmoid(qr)
            dp_ref[rows, 0:D] = (dqb[...] * (sq * (1.0 + qr * (1.0 - sq)))).astype(MXU)

    rev = lambda b, i: b * nr + nr - 1 - i
    col = lambda c: pl.BlockSpec((R, D), lambda b, i: (rev(b, i), c))
    row = pl.BlockSpec((1, D), lambda b, i: (0, 0))
    return _call(
        body, name="hgrn_bwd", grid=(T // S, nr),
        in_specs=[col(0), col(1), col(2), col(3), col(0), col(0),
                  pl.BlockSpec((ncr, HG_H, HG_DK, HG_DK), lambda b, i: (rev(b, i), 0, 0, 0)), row, row],
        out_specs=[pl.BlockSpec((R, 4 * D), lambda b, i: (rev(b, i), 0)), row, row],
        out_shape=[_sds((T, 4 * D), MXU), _sds((1, D), F32), _sds((1, D), F32)],
        scratch=[pltpu.VMEM((HG_H, HG_DK, HG_DK), F32), pltpu.VMEM((R, D), F32), pltpu.VMEM((HG_C, D), F32),
                 pltpu.VMEM((HG_C, D), F32), pltpu.VMEM((HG_C, D), F32)],
    )(p, p, p, p, o, dyo, states, lb, ng)


def _mm_tn_stack(a, b, *, name, G, M, N, tk, stack):
    E, e, buf = stack
    T = a.shape[-2]

    def spec(arr, width):
        if arr.ndim == 3:
            return pl.BlockSpec((None, tk, width), lambda g, k: (g, k, 0))
        return pl.BlockSpec((tk, width), lambda g, k: (k, 0))

    def body(*refs):
        a_ref, b_ref, o_ref = refs[0], refs[1], refs[-1]

        @pl.when(pl.program_id(1) == 0)
        def _():
            o_ref[...] = jnp.zeros_like(o_ref)

        o_ref[...] += _dot_tn(a_ref[...], b_ref[...])

    in_specs = [spec(a, M), spec(b, N)]
    args = [a, b]
    aliases = {}
    if buf is not None:
        in_specs.append(pl.BlockSpec(memory_space=pl.ANY))
        args.append(buf)
        aliases = {2: 0}
    return pl.pallas_call(
        body, name=name, grid=(G, T // tk), in_specs=in_specs,
        out_specs=pl.BlockSpec((None, None, M, N), lambda g, k: (g, e, 0, 0)),
        out_shape=_sds((G, E, M, N), F32), input_output_aliases=aliases,
        compiler_params=pltpu.CompilerParams(dimension_semantics=("arbitrary", "arbitrary"),
                                             vmem_limit_bytes=VMEM_LIMIT))(*args)


MESH = pl.DeviceIdType.MESH
ANY = pl.BlockSpec(memory_space=pl.ANY)


def _pos():
    return lax.axis_index("x"), lax.axis_index("y"), lax.axis_index("c")


def _other_chips(x, y):
    return [(1 - x, y), (x, 1 - y), (1 - x, 1 - y)]


def _comm_call(body, *, name, args, out_shape, n_sem):
    return pl.pallas_call(
        body, name=name, in_specs=[ANY] * len(args), out_specs=[ANY] * len(out_shape), out_shape=out_shape,
        scratch_shapes=[pltpu.SemaphoreType.DMA((n_sem,)), pltpu.SemaphoreType.DMA((n_sem,)),
                        pltpu.SemaphoreType.DMA((len(args),))],
    )(*args)


def all_gather_chips(xs):
    n = len(xs)

    def body(*refs):
        x_refs, o_refs = refs[:n], refs[n:2 * n]
        ssem, rsem, lsem = refs[2 * n:]
        x, y, c = _pos()
        me = 2 * x + y
        chips = _other_chips(x, y)
        sib = (x, y, 1 - c)

        def rc(src, dst, idx, dev):
            return pltpu.make_async_remote_copy(src_ref=src, dst_ref=dst, send_sem=ssem.at[idx], recv_sem=rsem.at[idx],
                                                device_id=dev, device_id_type=MESH)

        local, started = [], []
        for t in range(n):
            hr = xs[t].shape[0] // 2
            mine = pl.ds(c * hr, hr)
            cp = pltpu.make_async_copy(x_refs[t], o_refs[t].at[me], lsem.at[t])
            cp.start()
            local.append(cp)
            for k, (cx, cy) in enumerate(chips):
                cp = rc(x_refs[t].at[mine], o_refs[t].at[me, mine], 6 * t + k, (cx, cy, c))
                cp.start()
                started.append(cp)
        for t in range(n):
            hr = xs[t].shape[0] // 2
            mine = pl.ds(c * hr, hr)
            for k, (cx, cy) in enumerate(chips):
                landed = o_refs[t].at[2 * cx + cy, mine]
                rc(landed, landed, 6 * t + k, (cx, cy, c)).wait_recv()
                cp = rc(landed, landed, 6 * t + 3 + k, sib)
                cp.start()
                started.append(cp)
        for t in range(n):
            hr = xs[t].shape[0] // 2
            theirs = pl.ds((1 - c) * hr, hr)
            for k, (cx, cy) in enumerate(chips):
                other = o_refs[t].at[2 * cx + cy, theirs]
                rc(other, other, 6 * t + 3 + k, sib).wait_recv()
        for cp in started:
            cp.wait_send()
        for cp in local:
            cp.wait()

    outs = _comm_call(body, name="all_gather_chips", args=list(xs),
                      out_shape=[_sds((NSH,) + a.shape, a.dtype) for a in xs], n_sem=6 * n)
    return list(outs)


def sibling_half_exchange(gs):
    n = len(gs)

    def body(*refs):
        g_refs, o_refs = refs[:n], refs[n:2 * n]
        ssem, rsem, _ = refs[2 * n:]
        x, y, c = _pos()
        cps = []
        for t in range(n):
            hr = gs[t].shape[1] // 2
            for j in range(NSH):
                cp = pltpu.make_async_remote_copy(
                    src_ref=g_refs[t].at[j, pl.ds((1 - c) * hr, hr)], dst_ref=o_refs[t].at[j],
                    send_sem=ssem.at[NSH * t + j], recv_sem=rsem.at[NSH * t + j],
                    device_id=(x, y, 1 - c), device_id_type=MESH)
                cp.start()
                cps.append(cp)
        for cp in cps:
            cp.wait()

    outs = _comm_call(body, name="sibling_half_exchange", args=list(gs),
                      out_shape=[_sds((NSH, g.shape[1] // 2, g.shape[2]), g.dtype) for g in gs], n_sem=NSH * n)
    return list(outs)


def chip_scatter(ss):
    n = len(ss)

    def body(*refs):
        s_refs, o_refs = refs[:n], refs[n:2 * n]
        ssem, rsem, _ = refs[2 * n:]
        x, y, c = _pos()
        cps = []
        for t in range(n):
            for k, (cx, cy) in enumerate(_other_chips(x, y)):
                cp = pltpu.make_async_remote_copy(
                    src_ref=s_refs[t].at[2 * cx + cy], dst_ref=o_refs[t].at[k],
                    send_sem=ssem.at[3 * t + k], recv_sem=rsem.at[3 * t + k],
                    device_id=(cx, cy, c), device_id_type=MESH)
                cp.start()
                cps.append(cp)
        for cp in cps:
            cp.wait()

    outs = _comm_call(body, name="chip_scatter", args=list(ss),
                      out_shape=[_sds((3,) + s.shape[1:], s.dtype) for s in ss], n_sem=3 * n)
    return list(outs)


def sibling_exchange(rs):
    n = len(rs)

    def body(*refs):
        r_refs, o_refs = refs[:n], refs[n:2 * n]
        ssem, rsem, _ = refs[2 * n:]
        x, y, c = _pos()
        cps = []
        for t in range(n):
            cp = pltpu.make_async_remote_copy(
                src_ref=r_refs[t], dst_ref=o_refs[t], send_sem=ssem.at[t], recv_sem=rsem.at[t],
                device_id=(x, y, 1 - c), device_id_type=MESH)
            cp.start()
            cps.append(cp)
        for cp in cps:
            cp.wait()

    outs = _comm_call(body, name="sibling_exchange", args=list(rs),
                      out_shape=[_sds(r.shape, r.dtype) for r in rs], n_sem=n)
    return list(outs)


def all_gather_devices(v):
    def body(v_ref, o_ref, ssem, rsem, lsem):
        x, y, c = _pos()
        me = 4 * x + 2 * y + c
        loc = pltpu.make_async_copy(v_ref, o_ref.at[me], lsem.at[0])
        loc.start()
        cps = []
        k = 0
        for fx in range(2):
            for fy in range(2):
                for fc in range(2):
                    if fx == fy == fc == 0:
                        continue
                    cp = pltpu.make_async_remote_copy(
                        src_ref=v_ref, dst_ref=o_ref.at[me], send_sem=ssem.at[k], recv_sem=rsem.at[k],
                        device_id=(x ^ fx, y ^ fy, c ^ fc), device_id_type=MESH)
                    cp.start()
                    src = 4 * (x ^ fx) + 2 * (y ^ fy) + (c ^ fc)
                    cps.append((cp, o_ref.at[src], k))
                    k += 1
        for cp, landed, k in cps:
            cp.wait_send()
            pltpu.make_async_remote_copy(
                src_ref=landed, dst_ref=landed, send_sem=ssem.at[k], recv_sem=rsem.at[k],
                device_id=(x, y, c), device_id_type=MESH).wait_recv()
        loc.wait()

    return _comm_call(body, name="all_gather_devices", args=[v],
                      out_shape=[_sds((8,) + v.shape, v.dtype)], n_sem=7)[0]


def _call_sp(body, *, name, grid, in_specs, out_specs, out_shape, pos, args):
    return pl.pallas_call(
        body, name=name,
        grid_spec=pltpu.PrefetchScalarGridSpec(num_scalar_prefetch=1, grid=grid, in_specs=in_specs,
                                               out_specs=out_specs),
        out_shape=out_shape,
        compiler_params=pltpu.CompilerParams(dimension_semantics=("arbitrary",) * len(grid),
                                             vmem_limit_bytes=VMEM_LIMIT))(pos, *args)


def _rows_tile(r):
    for t in (512, 256, 128, 64, 32, 16, 8):
        if r % t == 0:
            return t
    raise ValueError(r)


def pair_sum(g, r, pos):
    _, R, C = g.shape
    hr = R // 2
    tr = _rows_tile(hr)
    nbh = hr // tr

    def body(p_ref, g_ref, r_ref, o_ref):
        o_ref[...] = (g_ref[...] + r_ref[...]).astype(MXU)

    return _call_sp(
        body, name="pair_sum", grid=(NSH, nbh), pos=pos, args=[g, r],
        in_specs=[pl.BlockSpec((None, tr, C), lambda j, i, p: (j, p[0] * nbh + i, 0)),
                  pl.BlockSpec((None, tr, C), lambda j, i, p: (j, i, 0))],
        out_specs=pl.BlockSpec((None, tr, C), lambda j, i, p: (j, i, 0)),
        out_shape=_sds((NSH, hr, C), MXU))


def reduce_own(g, r_sib, r_ici, pos):
    _, R, C = g.shape
    hr = R // 2
    tr = _rows_tile(hr)
    nbh = hr // tr

    def body(p_ref, g_ref, rs_ref, ri_ref, o_ref):
        s = g_ref[...] + rs_ref[...]
        for k in range(3):
            s = s + ri_ref[k].astype(F32)
        o_ref[...] = s

    return _call_sp(
        body, name="reduce_own", grid=(nbh,), pos=pos, args=[g, r_sib, r_ici],
        in_specs=[pl.BlockSpec((None, tr, C), lambda i, p: (p[1], p[0] * nbh + i, 0)),
                  pl.BlockSpec((None, tr, C), lambda i, p: (p[1], i, 0)),
                  pl.BlockSpec((3, tr, C), lambda i, p: (0, i, 0))],
        out_specs=pl.BlockSpec((tr, C), lambda i, p: (i, 0)),
        out_shape=_sds((hr, C), F32))


def _adamw_math(w, g, m, v):
    m = ADAM_B1 * m + (1.0 - ADAM_B1) * g
    v = ADAM_B2 * v + (1.0 - ADAM_B2) * (g * g)
    m_hat = m / (1.0 - ADAM_B1 ** ADAM_STEP)
    v_hat = v / (1.0 - ADAM_B2 ** ADAM_STEP)
    delta = -ADAM_LR * (m_hat / (jnp.sqrt(v_hat) + ADAM_EPS) + ADAM_WD * w)
    return delta, m, v


def adamw_halves(w, m, v, ga, gb, pos):
    R, C = w.shape
    hr = R // 2
    tr = _rows_tile(hr)
    nbh = hr // tr

    def body(p_ref, w_ref, m_ref, v_ref, ga_ref, gb_ref, g_ref, d_ref, mo_ref, vo_ref):
        mine = (pl.program_id(0) // nbh) == p_ref[0]
        g = jnp.where(mine, ga_ref[...], gb_ref[...])
        g_ref[...] = g
        d_ref[...], mo_ref[...], vo_ref[...] = _adamw_math(w_ref[...], g, m_ref[...], v_ref[...])

    blk = pl.BlockSpec((tr, C), lambda i, p: (i, 0))
    return _call_sp(
        body, name="adamw_halves", grid=(R // tr,), pos=pos, args=[w, m, v, ga, gb],
        in_specs=[blk, blk, blk,
                  pl.BlockSpec((tr, C), lambda i, p: (jnp.where(i // nbh == p[0], i % nbh, 0), 0)),
                  pl.BlockSpec((tr, C), lambda i, p: (jnp.where(i // nbh == p[0], 0, i % nbh), 0))],
        out_specs=[blk, blk, blk, blk],
        out_shape=[_sds((R, C), F32)] * 4)


def adamw_sum(gall, w, m, v):
    n, R, C = gall.shape

    def body(ga_ref, w_ref, m_ref, v_ref, g_ref, d_ref, mo_ref, vo_ref):
        g = ga_ref[0]
        for k in range(1, n):
            g = g + ga_ref[k]
        g_ref[...] = g
        d_ref[...], mo_ref[...], vo_ref[...] = _adamw_math(w_ref[...], g, m_ref[...], v_ref[...])

    blk = pl.BlockSpec((R, C), lambda i: (0, 0))
    return _call(body, name="adamw_sum", grid=(1,),
                 in_specs=[pl.BlockSpec((n, R, C), lambda i: (0, 0, 0)), blk, blk, blk],
                 out_specs=[blk, blk, blk, blk], out_shape=[_sds((R, C), F32)] * 4)(gall, w, m, v)


_WEIGHTS = ['ffn_norm', 'ffn_w_gate', 'ffn_w_up', 'ffn_w_down', 'mix_norm', 'final_norm', 'conv_w_in', 'conv_b_in',
            'conv_dw', 'conv_dw_b', 'conv_ln_g', 'conv_ln_b', 'conv_w_out', 'fox_w_in', 'fox_b_f', 'fox_w_out',
            'hgrn_w_in', 'hgrn_lb_logits', 'hgrn_norm', 'hgrn_w_out', 'pool_w', 'pool_scale']
_BIG = ['ffn_w_gate', 'ffn_w_up', 'ffn_w_down', 'conv_w_in', 'conv_w_out', 'fox_w_in', 'fox_w_out',
        'hgrn_w_in', 'hgrn_w_out', 'pool_w']
_SHARDED_SMALL = ['ffn_norm', 'conv_dw', 'hgrn_norm', 'pool_scale']
_REPLICATED = ['mix_norm', 'final_norm', 'conv_b_in', 'conv_dw_b', 'conv_ln_g', 'conv_ln_b', 'fox_b_f', 'hgrn_lb_logits']
FOX_N = 3 * D + FOX_H
FOX_NP = 3200
QS = D // NSH


def _pad_rows(a, rows):
    return jnp.pad(a, ((0, rows - a.shape[0]), (0, 0)))


def _pack_sharded_small(get):
    return jnp.concatenate([get('ffn_norm').reshape(8, -1), _pad_rows(get('conv_dw')[0], 32),
                            get('hgrn_norm'), get('pool_scale'), jnp.zeros((6, get('pool_scale').shape[1]), F32)], axis=0)


def _pack_replicated(get):
    return jnp.concatenate([get('mix_norm'), get('final_norm').reshape(1, D), get('conv_b_in').reshape(2, D),
                            get('conv_dw_b'), get('conv_ln_g'), get('conv_ln_b'),
                            jnp.pad(get('fox_b_f'), ((0, 0), (0, D - FOX_H))), get('hgrn_lb_logits'),
                            jnp.zeros((9, D), F32)], axis=0)


def _unpack_replicated(p):
    return {'mix_norm': p[0:4], 'final_norm': p[4], 'conv_b_in': p[5:7].reshape(1, 2 * D), 'conv_dw_b': p[7:8],
            'conv_ln_g': p[8:9], 'conv_ln_b': p[9:10], 'fox_b_f': p[10:11, :FOX_H], 'hgrn_lb_logits': p[11:15]}


def _unpack_sharded_small(p):
    return {'ffn_norm': p[0:8].reshape(DEPTH, 2, -1), 'conv_dw': p[8:8 + CONV_W][None],
            'hgrn_norm': p[40:41], 'pool_scale': p[41:42]}


def kernel(x, ffn_norm, ffn_w_gate, ffn_w_up, ffn_w_down, mix_norm, final_norm, conv_w_in, conv_b_in, conv_dw, conv_dw_b, conv_ln_g, conv_ln_b, conv_w_out, fox_w_in, fox_b_f, fox_w_out, hgrn_w_in, hgrn_lb_logits, hgrn_norm, hgrn_w_out, pool_w, pool_scale, loss_target, m_ffn_norm, m_ffn_w_gate, m_ffn_w_up, m_ffn_w_down, m_mix_norm, m_final_norm, m_conv_w_in, m_conv_b_in, m_conv_dw, m_conv_dw_b, m_conv_ln_g, m_conv_ln_b, m_conv_w_out, m_fox_w_in, m_fox_b_f, m_fox_w_out, m_hgrn_w_in, m_hgrn_lb_logits, m_hgrn_norm, m_hgrn_w_out, m_pool_w, m_pool_scale, v_ffn_norm, v_ffn_w_gate, v_ffn_w_up, v_ffn_w_down, v_mix_norm, v_final_norm, v_conv_w_in, v_conv_b_in, v_conv_dw, v_conv_dw_b, v_conv_ln_g, v_conv_ln_b, v_conv_w_out, v_fox_w_in, v_fox_b_f, v_fox_w_out, v_hgrn_w_in, v_hgrn_lb_logits, v_hgrn_norm, v_hgrn_w_out, v_pool_w, v_pool_scale):
    W = dict(ffn_norm=ffn_norm, ffn_w_gate=ffn_w_gate, ffn_w_up=ffn_w_up, ffn_w_down=ffn_w_down, mix_norm=mix_norm, final_norm=final_norm, conv_w_in=conv_w_in, conv_b_in=conv_b_in, conv_dw=conv_dw, conv_dw_b=conv_dw_b, conv_ln_g=conv_ln_g, conv_ln_b=conv_ln_b, conv_w_out=conv_w_out, fox_w_in=fox_w_in, fox_b_f=fox_b_f, fox_w_out=fox_w_out, hgrn_w_in=hgrn_w_in, hgrn_lb_logits=hgrn_lb_logits, hgrn_norm=hgrn_norm, hgrn_w_out=hgrn_w_out, pool_w=pool_w, pool_scale=pool_scale)
    M = dict(ffn_norm=m_ffn_norm, ffn_w_gate=m_ffn_w_gate, ffn_w_up=m_ffn_w_up, ffn_w_down=m_ffn_w_down, mix_norm=m_mix_norm, final_norm=m_final_norm, conv_w_in=m_conv_w_in, conv_b_in=m_conv_b_in, conv_dw=m_conv_dw, conv_dw_b=m_conv_dw_b, conv_ln_g=m_conv_ln_g, conv_ln_b=m_conv_ln_b, conv_w_out=m_conv_w_out, fox_w_in=m_fox_w_in, fox_b_f=m_fox_b_f, fox_w_out=m_fox_w_out, hgrn_w_in=m_hgrn_w_in, hgrn_lb_logits=m_hgrn_lb_logits, hgrn_norm=m_hgrn_norm, hgrn_w_out=m_hgrn_w_out, pool_w=m_pool_w, pool_scale=m_pool_scale)
    V = dict(ffn_norm=v_ffn_norm, ffn_w_gate=v_ffn_w_gate, ffn_w_up=v_ffn_w_up, ffn_w_down=v_ffn_w_down, mix_norm=v_mix_norm, final_norm=v_final_norm, conv_w_in=v_conv_w_in, conv_b_in=v_conv_b_in, conv_dw=v_conv_dw, conv_dw_b=v_conv_dw_b, conv_ln_g=v_conv_ln_g, conv_ln_b=v_conv_ln_b, conv_w_out=v_conv_w_out, fox_w_in=v_fox_w_in, fox_b_f=v_fox_b_f, fox_w_out=v_fox_w_out, hgrn_w_in=v_hgrn_w_in, hgrn_lb_logits=v_hgrn_lb_logits, hgrn_norm=v_hgrn_norm, hgrn_w_out=v_hgrn_w_out, pool_w=v_pool_w, pool_scale=v_pool_scale)

    px, py, pc = _pos()
    jme = 2 * px + py
    pos = jnp.stack([pc, jme]).astype(jnp.int32)
    S = x.shape[1]
    T = x.shape[0] * S
    x2 = x.reshape(T, D)
    tgt = loss_target.reshape(T, D)

    flat = lambda a: a.reshape(-1, a.shape[-1])
    gathered = all_gather_chips([flat(W[n]).astype(MXU) for n in _BIG] + [_pack_sharded_small(W.get)])
    G = dict(zip(_BIG, gathered[:-1]))
    small = gathered[-1].transpose(1, 0, 2).reshape(48, D)
    ffn_norm_f, conv_dw_f = small[0:8], small[8:40]
    hgrn_norm_f, pool_scale_f = small[40:41], small[41:42]
    wg_all = G['ffn_w_gate'].reshape(NSH, 2 * DEPTH, D, FS)
    wu_all = G['ffn_w_up'].reshape(NSH, 2 * DEPTH, D, FS)
    wd_all = G['ffn_w_down'].reshape(NSH, 2 * DEPTH, FS, D)
    conv_wi = G['conv_w_in']
    conv_wo = G['conv_w_out'].reshape(D, D)
    fox_full = jnp.pad(G['fox_w_in'].transpose(1, 0, 2).reshape(D, FOX_N), ((0, 0), (0, FOX_NP - FOX_N)))
    fox_w5 = fox_full.reshape(D, 5, FOX_NP // 5).transpose(1, 0, 2)
    fox_wf = fox_full[:, 3 * D:][None]
    fox_bf = jnp.pad(fox_b_f, ((0, 0), (0, 128 - FOX_H)))
    fox_wo = G['fox_w_out'].reshape(D, D)
    hgrn_wi = G['hgrn_w_in']
    hgrn_wo = G['hgrn_w_out'].reshape(D, D)
    pool_wf = G['pool_w'].reshape(NSH, 4, 64, POOL_G).transpose(1, 0, 2, 3).reshape(4, POOL_G, POOL_G)
    conv_bi = conv_b_in.reshape(NSH, 1, 2 * D // NSH)

    def ffn_f(xs, e):
        xo, h, a, b = ffn_fwd(xs, ffn_norm_f[e:e + 1], wg_all, wu_all, wd_all, e)
        return xo, (xs, h, a, b)

    saved = []
    xs = x2
    lb = lb_fwd(hgrn_lb_logits)
    for i in range(DEPTH):
        xs, r0 = ffn_f(xs, 2 * i)
        gm = mix_norm[i:i + 1]
        xin = xs
        if i == 0:
            p, h = norm_mm(xin, gm, conv_wi, conv_bi, name="conv_in", out_dtype=F32)
            u2, u4 = conv_fwd_core(p, conv_dw_f, conv_dw_b, conv_ln_g, conv_ln_b, S=S)
            xs = mm_res(u4, conv_wo, xin, name="conv_out")
            rm = (xin, p, h, u2, u4)
        elif i == 1:
            p, h = norm_mm(xin, gm, fox_w5, None, name="fox_in", out_dtype=MXU)
            fl, _ = norm_mm(xin, gm, fox_wf, None, name="fox_in_f", out_dtype=F32)
            cq, _ = fox_cum(fl, fox_bf, S=S)
            qa, ka, va = fox_prep(p, cq)
            o, o32, lse = fox2_fwd(qa, ka, va, S=S)
            xs = mm_res(o, fox_wo, xin, name="fox_out")
            rm = (xin, h, fl, qa, ka, va, o, o32, lse)
        elif i == 2:
            p, h = norm_mm(xin, gm, hgrn_wi, None, name="hgrn_in", out_dtype=F32)
            yh, oh, st = hgrn_fwd(p, lb, hgrn_norm_f, S=S)
            xs = mm_res(yh, hgrn_wo, xin, name="hgrn_out")
            rm = (xin, p, h, yh, oh, st)
        else:
            xs, mp = pool_fwd(xin, gm, pool_wf, pool_scale_f, S=S)
            rm = (xin, mp)
        xs, r1 = ffn_f(xs, 2 * i + 1)
        saved.append((r0, rm, r1))

    loss8, dx, d_final = loss_head(xs, final_norm.reshape(1, D), tgt)

    wgt_all, wut_all, wdt_all = (w.transpose(0, 1, 3, 2) for w in (wg_all, wu_all, wd_all))
    conv_wit, fox_w5t, hgrn_wit = (w.transpose(0, 2, 1) for w in (conv_wi, fox_w5, hgrn_wi))
    conv_wot, fox_wot, hgrn_wot = conv_wo.T, fox_wo.T, hgrn_wo.T
    gb = {'g': None, 'u': None, 'd': None}
    d_ffn_norm = [None] * (2 * DEPTH)
    d_mix_norm = [None] * DEPTH
    gbig = {}
    gsm = {}

    def ffn_b(dy, res, e):
        xin, h, a, b = res
        dxo, da, db, z, dyh, dg = ffn_bwd_dx(xin, ffn_norm_f[e:e + 1], dy, a, b, wgt_all, wut_all, wdt_all, e)
        tk = min(2048, xin.shape[0])
        gb['g'] = mm_tn(h, da, name="ffn_dwg", G=NSH, M=D, N=FS, tk=tk, stack=(2 * DEPTH, e, gb['g']))
        gb['u'] = mm_tn(h, db, name="ffn_dwu", G=NSH, M=D, N=FS, tk=tk, stack=(2 * DEPTH, e, gb['u']))
        gb['d'] = mm_tn(z, dyh, name="ffn_dwd", G=NSH, M=FS, N=D, tk=tk, stack=(2 * DEPTH, e, gb['d']))
        d_ffn_norm[e] = dg
        return dxo

    for i in reversed(range(DEPTH)):
        r0, rm, r1 = saved[i]
        dx = ffn_b(dx, r1, 2 * i + 1)
        gm = mix_norm[i:i + 1]
        if i == 0:
            xin, p, h, u2, u4 = rm
            du2, dyb, gsm['conv_ln_g'], gsm['conv_ln_b'], gsm['conv_dw_b'] = conv_bwd_rows(dx, conv_wot, u2, conv_ln_g, conv_ln_b)
            dp, gsm['conv_b_in'], ddw = conv_bwd_core(du2, p, conv_dw_f, S=S)
            gsm['conv_dw'] = ddw
            gbig['conv_w_in'] = mm_tn(h, dp, name="conv_dwin", G=NSH, M=D, N=2 * D // NSH, b_step=1)
            gbig['conv_w_out'] = mm_tn(u4, dyb, name="conv_dwout", G=NSH, M=QS, N=D, a_step=1)
            dx, d_mix_norm[i] = inproj_bwd(dp, conv_wit, xin, gm, dx, name="conv_in_bwd")
        elif i == 1:
            xin, h, fl, qa, ka, va, o, o32, lse = rm
            do, dyb = mm_nt(dx, fox_wot, name="fox_out_bwd")
            qb, da = fox2_prep_bwd(do, o32, lse, qa)
            dq = fox2_dq(qb, ka, va, da, S=S)
            dk, dv, dck = fox2_dkv(qb, ka, va, da, S=S)
            dfl, dbf = fox_fin(dck, fl, fox_bf, S=S)
            gsm['fox_b_f'] = dbf
            dp = jnp.concatenate([dq, dk, dv, dfl], axis=1)
            dw5 = mm_tn(h, dp, name="fox_dwin", G=5, M=D, N=FOX_NP // 5, b_step=1)
            dwf = dw5.transpose(1, 0, 2).reshape(D, FOX_NP)[:, :FOX_N]
            gbig['fox_w_in'] = dwf.reshape(D, NSH, FOX_N // NSH).transpose(1, 0, 2)
            gbig['fox_w_out'] = mm_tn(o, dyb, name="fox_dwout", G=NSH, M=QS, N=D, a_step=1)
            dx, d_mix_norm[i] = inproj_bwd(dp, fox_w5t, xin, gm, dx, name="fox_in_bwd")
        elif i == 2:
            xin, p, h, yh, oh, st = rm
            dyo, dyb = mm_nt(dx, hgrn_wot, name="hgrn_out_bwd")
            dp, dlb, gsm['hgrn_norm'] = hgrn_bwd(p, oh, dyo, st, lb, hgrn_norm_f, S=S)
            gsm['hgrn_lb_logits'] = lb_bwd(hgrn_lb_logits, dlb)
            gbig['hgrn_w_in'] = mm_tn(h, dp, name="hgrn_dwin", G=NSH, M=D, N=D, b_step=1)
            gbig['hgrn_w_out'] = mm_tn(yh, dyb, name="hgrn_dwout", G=NSH, M=QS, N=D, a_step=1)
            dx, d_mix_norm[i] = inproj_bwd(dp, hgrn_wit, xin, gm, dx, name="hgrn_in_bwd")
        else:
            xin, mp = rm
            dmc, dyp, gsm['pool_scale'] = pool_bwd_rows(dx, mp, pool_wf, pool_scale_f, S=S)
            dwp = mm_tn(mp, dyp, name="pool_dw", G=4, M=POOL_G, N=POOL_G, a_step=1, b_step=1)
            gbig['pool_w'] = dwp.reshape(4, NSH, 64, POOL_G).transpose(1, 0, 2, 3).reshape(NSH, 4 * 64, POOL_G)
            dx, d_mix_norm[i] = pool_bwd_core(dmc, xin, gm, dx, S=S)
        dx = ffn_b(dx, r0, 2 * i)

    gbig['ffn_w_gate'] = gb['g'].reshape(NSH, 2 * DEPTH * D, FS)
    gbig['ffn_w_up'] = gb['u'].reshape(NSH, 2 * DEPTH * D, FS)
    gbig['ffn_w_down'] = gb['d'].reshape(NSH, 2 * DEPTH * FS, D)

    gl = [gbig[n] for n in _BIG]
    r_sib = sibling_half_exchange(gl)
    s16 = [pair_sum(g, r, pos) for g, r in zip(gl, r_sib)]
    r_ici = chip_scatter(s16)
    red = [reduce_own(g, rs, ri, pos) for g, rs, ri in zip(gl, r_sib, r_ici)]
    oth = sibling_exchange(red)
    out = {}
    for n, ga, gb_ in zip(_BIG, red, oth):
        res = adamw_halves(flat(W[n]), flat(M[n]), flat(V[n]), ga, gb_, pos)
        out[n] = [r.reshape(W[n].shape) for r in res]

    gfull = {'mix_norm': jnp.concatenate(d_mix_norm, axis=0), 'final_norm': d_final,
             'conv_b_in': gsm['conv_b_in'], 'conv_dw_b': gsm['conv_dw_b'], 'conv_ln_g': gsm['conv_ln_g'],
             'conv_ln_b': gsm['conv_ln_b'], 'fox_b_f': gsm['fox_b_f'][:, :FOX_H], 'hgrn_lb_logits': gsm['hgrn_lb_logits'],
             'ffn_norm': jnp.concatenate(d_ffn_norm, axis=0), 'conv_dw': gsm['conv_dw'][None, :CONV_W],
             'hgrn_norm': gsm['hgrn_norm'], 'pool_scale': gsm['pool_scale']}
    gpack = jnp.concatenate([_pack_replicated(gfull.get), _pack_sharded_small(gfull.get)], axis=0)
    gall = all_gather_devices(gpack)
    rep = adamw_sum(gall[:, :24], _pack_replicated(W.get), _pack_replicated(M.get), _pack_replicated(V.get))
    rep = [_unpack_replicated(r) for r in rep]
    for n in _REPLICATED:
        out[n] = [r[n].reshape(W[n].shape) for r in rep]
    gsh = lax.dynamic_slice_in_dim(gall[:, 24:], jme * QS, QS, axis=2)
    shd = adamw_sum(gsh, _pack_sharded_small(W.get), _pack_sharded_small(M.get), _pack_sharded_small(V.get))
    shd = [_unpack_sharded_small(r) for r in shd]
    for n in _SHARDED_SMALL:
        out[n] = [r[n].reshape(W[n].shape) for r in shd]

    loss = lax.psum(loss8[0, 0], ("x", "y", "c"))
    res = [loss, dx.reshape(x.shape)]
    for k in range(4):
        res += [out[n][k] for n in _WEIGHTS]
    return tuple(res)
```

```python
import functools

import jax
import jax.numpy as jnp
from jax import lax
from jax.experimental import pallas as pl
from jax.experimental.pallas import tpu as pltpu

D = 1024
F = 2816
NSH = 4
FS = F // NSH
DEPTH = 4
RMS_EPS = 1e-6
LN_EPS = 1e-5
CONV_W = 31
HALO = 32
FOX_H = 16
FOX_DH = 64
HG_H = 8
HG_DK = 128
HG_C = 32
POOL_WIN = (2, 4, 8, 16)
POOL_G = 256
MXU = jnp.bfloat16
F32 = jnp.float32
VMEM_LIMIT = 52 * 1024 * 1024

ADAM_LR = 0.001
ADAM_B1 = 0.9
ADAM_B2 = 0.999
ADAM_EPS = 1e-08
ADAM_WD = 0.01
ADAM_STEP = 10


def _call(body, *, name, grid, in_specs, out_specs, out_shape, scratch=()):
    return pl.pallas_call(
        body, name=name, grid=grid, in_specs=in_specs, out_specs=out_specs, out_shape=out_shape,
        scratch_shapes=list(scratch),
        compiler_params=pltpu.CompilerParams(dimension_semantics=("arbitrary",) * len(grid),
                                             vmem_limit_bytes=VMEM_LIMIT))


def _dot(a, b):
    return jnp.dot(a, b, preferred_element_type=F32)


def _dot_nt(a, b):
    return lax.dot_general(a, b, (((1,), (1,)), ((), ())), preferred_element_type=F32)


def _dot_tn(a, b):
    return lax.dot_general(a, b, (((0,), (0,)), ((), ())), preferred_element_type=F32)


def _split(x):
    hi = x.astype(MXU)
    return hi, (x - hi.astype(F32)).astype(MXU)


def _sigmoid(x):
    return 1.0 / (1.0 + jnp.exp(-x))


def _rms(x, g):
    r = lax.rsqrt(jnp.mean(x * x, axis=-1, keepdims=True) + RMS_EPS)
    return x * r * g


def _rms_bwd(dh, x, g):
    r = lax.rsqrt(jnp.mean(x * x, axis=-1, keepdims=True) + RMS_EPS)
    xh = x * r
    dhg = dh * g
    dx = r * (dhg - xh * jnp.mean(dhg * xh, axis=-1, keepdims=True))
    return dx, jnp.sum(dh * xh, axis=0, keepdims=True)


def _sds(shape, dtype):
    return jax.ShapeDtypeStruct(shape, dtype)


def _wspec(w, e):
    if w.ndim == 3:
        return pl.BlockSpec((None,) + w.shape[1:], lambda i, j: (j, 0, 0))
    return pl.BlockSpec((None, None) + w.shape[2:], lambda i, j: (j, e, 0, 0))


def ffn_fwd(x, g, wg, wu, wd, e=0, *, tm=512):
    T = x.shape[0]

    def body(x_ref, g_ref, wg_ref, wu_ref, wd_ref, xo_ref, h_ref, a_ref, b_ref, acc_ref):
        j = pl.program_id(1)

        @pl.when(j == 0)
        def _():
            h_ref[...] = _rms(x_ref[...], g_ref[...]).astype(MXU)
            acc_ref[...] = jnp.zeros_like(acc_ref)

        h = h_ref[...]
        a = _dot(h, wg_ref[...])
        b = _dot(h, wu_ref[...])
        a_ref[...] = a.astype(MXU)
        b_ref[...] = b.astype(MXU)
        z = (a * _sigmoid(a) * b).astype(MXU)
        acc_ref[...] += _dot(z, wd_ref[...])

        @pl.when(j == NSH - 1)
        def _():
            xo_ref[...] = x_ref[...] + 0.5 * acc_ref[...]

    return _call(
        body, name="ffn_fwd", grid=(T // tm, NSH),
        in_specs=[pl.BlockSpec((tm, D), lambda i, j: (i, 0)),
                  pl.BlockSpec((1, D), lambda i, j: (0, 0)),
                  _wspec(wg, e), _wspec(wu, e), _wspec(wd, e)],
        out_specs=[pl.BlockSpec((tm, D), lambda i, j: (i, 0)),
                   pl.BlockSpec((tm, D), lambda i, j: (i, 0)),
                   pl.BlockSpec((None, tm, FS), lambda i, j: (j, i, 0)),
                   pl.BlockSpec((None, tm, FS), lambda i, j: (j, i, 0))],
        out_shape=[_sds((T, D), F32), _sds((T, D), MXU), _sds((NSH, T, FS), MXU), _sds((NSH, T, FS), MXU)],
        scratch=[pltpu.VMEM((tm, D), F32)],
    )(x, g, wg, wu, wd)


def ffn_bwd_dx(x, g, dy, a, b, wg, wu, wd, e=0, *, tm=512):
    T = x.shape[0]

    def body(x_ref, g_ref, dy_ref, a_ref, b_ref, wg_ref, wu_ref, wd_ref,
             dx_ref, da_ref, db_ref, z_ref, dyh_ref, dg_ref, acc_ref):
        i = pl.program_id(0)
        j = pl.program_id(1)

        @pl.when(j == 0)
        def _():
            dyh_ref[...] = (0.5 * dy_ref[...]).astype(MXU)
            acc_ref[...] = jnp.zeros_like(acc_ref)

        @pl.when((i == 0) & (j == 0))
        def _():
            dg_ref[...] = jnp.zeros_like(dg_ref)

        dz = _dot(dyh_ref[...], wd_ref[...])
        av = a_ref[...].astype(F32)
        bv = b_ref[...].astype(F32)
        s = _sigmoid(av)
        sa = av * s
        da = (dz * bv * (s * (1.0 + av * (1.0 - s)))).astype(MXU)
        db = (dz * sa).astype(MXU)
        da_ref[...] = da
        db_ref[...] = db
        z_ref[...] = (sa * bv).astype(MXU)
        acc_ref[...] += _dot(da, wg_ref[...]) + _dot(db, wu_ref[...])

        @pl.when(j == NSH - 1)
        def _():
            dxn, dg = _rms_bwd(acc_ref[...], x_ref[...], g_ref[...])
            dx_ref[...] = dy_ref[...] + dxn
            dg_ref[...] += dg

    return _call(
        body, name="ffn_bwd_dx", grid=(T // tm, NSH),
        in_specs=[pl.BlockSpec((tm, D), lambda i, j: (i, 0)),
                  pl.BlockSpec((1, D), lambda i, j: (0, 0)),
                  pl.BlockSpec((tm, D), lambda i, j: (i, 0)),
                  pl.BlockSpec((None, tm, FS), lambda i, j: (j, i, 0)),
                  pl.BlockSpec((None, tm, FS), lambda i, j: (j, i, 0)),
                  _wspec(wg, e), _wspec(wu, e), _wspec(wd, e)],
        out_specs=[pl.BlockSpec((tm, D), lambda i, j: (i, 0)),
                   pl.BlockSpec((None, tm, FS), lambda i, j: (j, i, 0)),
                   pl.BlockSpec((None, tm, FS), lambda i, j: (j, i, 0)),
                   pl.BlockSpec((None, tm, FS), lambda i, j: (j, i, 0)),
                   pl.BlockSpec((tm, D), lambda i, j: (i, 0)),
                   pl.BlockSpec((1, D), lambda i, j: (0, 0))],
        out_shape=[_sds((T, D), F32), _sds((NSH, T, FS), MXU), _sds((NSH, T, FS), MXU),
                   _sds((NSH, T, FS), MXU), _sds((T, D), MXU), _sds((1, D), F32)],
        scratch=[pltpu.VMEM((tm, D), F32)],
    )(x, g, dy, a, b, wg, wu, wd)


def mm_tn(a, b, *, name, G, M, N, a_step=0, b_step=0, tk=512, stack=None):
    T = a.shape[-2]
    if stack is not None:
        return _mm_tn_stack(a, b, name=name, G=G, M=M, N=N, tk=tk, stack=stack)

    def spec(arr, width, step):
        if arr.ndim == 3:
            return pl.BlockSpec((None, tk, width), lambda g, k: (g, k, 0))
        return pl.BlockSpec((tk, width), lambda g, k: (k, g * step))

    def body(a_ref, b_ref, o_ref):
        @pl.when(pl.program_id(1) == 0)
        def _():
            o_ref[...] = jnp.zeros_like(o_ref)

        o_ref[...] += _dot_tn(a_ref[...], b_ref[...])

    return _call(
        body, name=name, grid=(G, T // tk),
        in_specs=[spec(a, M, a_step), spec(b, N, b_step)],
        out_specs=pl.BlockSpec((None, M, N), lambda g, k: (g, 0, 0)),
        out_shape=_sds((G, M, N), F32),
    )(a, b)


def norm_mm(x, g, wb, bias, *, name, out_dtype, tm=512):
    T = x.shape[0]
    G, _, ns = wb.shape
    has_bias = bias is not None

    def body(*refs):
        if has_bias:
            x_ref, g_ref, w_ref, bias_ref, p_ref, h_ref = refs
        else:
            x_ref, g_ref, w_ref, p_ref, h_ref = refs

        @pl.when(pl.program_id(1) == 0)
        def _():
            h_ref[...] = _rms(x_ref[...], g_ref[...]).astype(MXU)

        p = _dot(h_ref[...], w_ref[...])
        if has_bias:
            p = p + bias_ref[...]
        p_ref[...] = p.astype(out_dtype)

    in_specs = [pl.BlockSpec((tm, D), lambda i, j: (i, 0)),
                pl.BlockSpec((1, D), lambda i, j: (0, 0)),
                pl.BlockSpec((None, D, ns), lambda i, j: (j, 0, 0))]
    args = [x, g, wb]
    if has_bias:
        in_specs.append(pl.BlockSpec((None, 1, ns), lambda i, j: (j, 0, 0)))
        args.append(bias)
    return _call(
        body, name=name, grid=(T // tm, G), in_specs=in_specs,
        out_specs=[pl.BlockSpec((tm, ns), lambda i, j: (i, j)),
                   pl.BlockSpec((tm, D), lambda i, j: (i, 0))],
        out_shape=[_sds((T, G * ns), out_dtype), _sds((T, D), MXU)],
    )(*args)


def mm_res(y, w, x, *, name, tm=512):
    T, K = y.shape

    def body(y_ref, w_ref, x_ref, o_ref):
        o_ref[...] = x_ref[...] + _dot(y_ref[...], w_ref[...])

    return _call(
        body, name=name, grid=(T // tm,),
        in_specs=[pl.BlockSpec((tm, K), lambda i: (i, 0)),
                  pl.BlockSpec((K, D), lambda i: (0, 0)),
                  pl.BlockSpec((tm, D), lambda i: (i, 0))],
        out_specs=pl.BlockSpec((tm, D), lambda i: (i, 0)),
        out_shape=_sds((T, D), F32),
    )(y, w, x)


def mm_nt(a, wt, *, name, tm=512):
    T, K = a.shape
    N = wt.shape[1]
    w = wt

    def body(a_ref, w_ref, o_ref, ab_ref):
        ab = a_ref[...].astype(MXU)
        ab_ref[...] = ab
        o_ref[...] = _dot(ab, w_ref[...])

    return _call(
        body, name=name, grid=(T // tm,),
        in_specs=[pl.BlockSpec((tm, K), lambda i: (i, 0)),
                  pl.BlockSpec((K, N), lambda i: (0, 0))],
        out_specs=[pl.BlockSpec((tm, N), lambda i: (i, 0)),
                   pl.BlockSpec((tm, K), lambda i: (i, 0))],
        out_shape=[_sds((T, N), F32), _sds((T, K), MXU)],
    )(a, w)


def inproj_bwd(dp, wb, x, g, dres, *, name, tm=512):
    T = x.shape[0]
    G, ns, _ = wb.shape

    def body(dp_ref, w_ref, x_ref, g_ref, dres_ref, dx_ref, dg_ref, acc_ref):
        i = pl.program_id(0)
        j = pl.program_id(1)

        @pl.when(j == 0)
        def _():
            acc_ref[...] = jnp.zeros_like(acc_ref)

        @pl.when((i == 0) & (j == 0))
        def _():
            dg_ref[...] = jnp.zeros_like(dg_ref)

        acc_ref[...] += _dot(dp_ref[...], w_ref[...])

        @pl.when(j == G - 1)
        def _():
            dxn, dg = _rms_bwd(acc_ref[...], x_ref[...], g_ref[...])
            dx_ref[...] = dres_ref[...] + dxn
            dg_ref[...] += dg

    return _call(
        body, name=name, grid=(T // tm, G),
        in_specs=[pl.BlockSpec((tm, ns), lambda i, j: (i, j)),
                  pl.BlockSpec((None, ns, D), lambda i, j: (j, 0, 0)),
                  pl.BlockSpec((tm, D), lambda i, j: (i, 0)),
                  pl.BlockSpec((1, D), lambda i, j: (0, 0)),
                  pl.BlockSpec((tm, D), lambda i, j: (i, 0))],
        out_specs=[pl.BlockSpec((tm, D), lambda i, j: (i, 0)),
                   pl.BlockSpec((1, D), lambda i, j: (0, 0))],
        out_shape=[_sds((T, D), F32), _sds((1, D), F32)],
        scratch=[pltpu.VMEM((tm, D), F32)],
    )(dp, wb, x, g, dres)


def loss_head(x, gf, tgt, *, tm=512):
    T = x.shape[0]

    def body(x_ref, g_ref, t_ref, loss_ref, dx_ref, dg_ref):
        @pl.when(pl.program_id(0) == 0)
        def _():
            loss_ref[...] = jnp.zeros_like(loss_ref)
            dg_ref[...] = jnp.zeros_like(dg_ref)

        xv = x_ref[...]
        gv = g_ref[...]
        e = _rms(xv, gv) - t_ref[...]
        loss_ref[...] += (0.5 / D) * jnp.sum(e * e)
        dxn, dg = _rms_bwd(e * (1.0 / D), xv, gv)
        dx_ref[...] = dxn
        dg_ref[...] += dg

    return _call(
        body, name="loss_head", grid=(T // tm,),
        in_specs=[pl.BlockSpec((tm, D), lambda i: (i, 0)),
                  pl.BlockSpec((1, D), lambda i: (0, 0)),
                  pl.BlockSpec((tm, D), lambda i: (i, 0))],
        out_specs=[pl.BlockSpec((8, 128), lambda i: (0, 0)),
                   pl.BlockSpec((tm, D), lambda i: (i, 0)),
                   pl.BlockSpec((1, D), lambda i: (0, 0))],
        out_shape=[_sds((8, 128), F32), _sds((T, D), F32), _sds((1, D), F32)],
    )(x, gf, tgt)


def _glu(p):
    return p[:, :D] * _sigmoid(p[:, D:])


def _ln_stats(u):
    mu = jnp.mean(u, axis=-1, keepdims=True)
    xc = u - mu
    rstd = lax.rsqrt(jnp.mean(xc * xc, axis=-1, keepdims=True) + LN_EPS)
    return xc * rstd, rstd


def conv_fwd_core(p, dw, dwb, lng, lnb, *, S, tt=256):
    T = p.shape[0]
    nb = S // tt
    r = tt // HALO

    def body(pc_ref, pp_ref, dw_ref, dwb_ref, lng_ref, lnb_ref, u2_ref, u4_ref, ubuf):
        first = (pl.program_id(0) % nb) == 0
        ubuf[0:HALO, :] = jnp.where(first, 0.0, _glu(pp_ref[...]))
        ubuf[HALO:, :] = _glu(pc_ref[...])
        for c in range(D // 128):
            cs = slice(c * 128, (c + 1) * 128)
            acc = jnp.zeros((tt, 128), F32)
            for k in range(CONV_W):
                acc = acc + dw_ref[k:k + 1, cs] * ubuf[k + 2:k + 2 + tt, cs]
            u2_ref[:, cs] = acc + dwb_ref[:, cs]
        xh, _ = _ln_stats(u2_ref[...])
        u3 = xh * lng_ref[...] + lnb_ref[...]
        u4_ref[...] = (u3 * _sigmoid(u3)).astype(MXU)

    row = pl.BlockSpec((1, D), lambda i: (0, 0))
    return _call(
        body, name="conv_fwd_core", grid=(T // tt,),
        in_specs=[pl.BlockSpec((tt, 2 * D), lambda i: (i, 0)),
                  pl.BlockSpec((HALO, 2 * D), lambda i: (jnp.maximum(i * r - 1, 0), 0)),
                  pl.BlockSpec((HALO, D), lambda i: (0, 0)), row, row, row],
        out_specs=[pl.BlockSpec((tt, D), lambda i: (i, 0)), pl.BlockSpec((tt, D), lambda i: (i, 0))],
        out_shape=[_sds((T, D), F32), _sds((T, D), MXU)],
        scratch=[pltpu.VMEM((tt + HALO, D), F32)],
    )(p, p, dw, dwb, lng, lnb)


def conv_bwd_rows(dy, wout, u2, lng, lnb, *, tm=512):
    T = dy.shape[0]

    def body(dy_ref, w_ref, u2_ref, lng_ref, lnb_ref, du2_ref, dyb_ref, dlng_ref, dlnb_ref, ddwb_ref):
        @pl.when(pl.program_id(0) == 0)
        def _():
            dlng_ref[...] = jnp.zeros_like(dlng_ref)
            dlnb_ref[...] = jnp.zeros_like(dlnb_ref)
            ddwb_ref[...] = jnp.zeros_like(ddwb_ref)

        dyb = dy_ref[...].astype(MXU)
        dyb_ref[...] = dyb
        du4 = _dot(dyb, w_ref[...])
        xh, rstd = _ln_stats(u2_ref[...])
        lng_v = lng_ref[...]
        u3 = xh * lng_v + lnb_ref[...]
        s = _sigmoid(u3)
        du3 = du4 * (s * (1.0 + u3 * (1.0 - s)))
        dlng_ref[...] += jnp.sum(du3 * xh, axis=0, keepdims=True)
        dlnb_ref[...] += jnp.sum(du3, axis=0, keepdims=True)
        dxh = du3 * lng_v
        du2 = rstd * (dxh - jnp.mean(dxh, axis=-1, keepdims=True)
                      - xh * jnp.mean(dxh * xh, axis=-1, keepdims=True))
        du2_ref[...] = du2
        ddwb_ref[...] += jnp.sum(du2, axis=0, keepdims=True)

    row = pl.BlockSpec((1, D), lambda i: (0, 0))
    blk = pl.BlockSpec((tm, D), lambda i: (i, 0))
    return _call(
        body, name="conv_bwd_rows", grid=(T // tm,),
        in_specs=[blk, pl.BlockSpec((D, D), lambda i: (0, 0)), blk, row, row],
        out_specs=[blk, blk, row, row, row],
        out_shape=[_sds((T, D), F32), _sds((T, D), MXU), _sds((1, D), F32), _sds((1, D), F32), _sds((1, D), F32)],
    )(dy, wout, u2, lng, lnb)


def conv_bwd_core(du2, p, dw, *, S, tt=256):
    T = p.shape[0]
    nb = S // tt
    r = tt // HALO
    last_halo = T // HALO - 1

    def body(dc_ref, dn_ref, pc_ref, pp_ref, dw_ref, dp_ref, dbin_ref, ddw_ref, ubuf, dbuf):
        i = pl.program_id(0)

        @pl.when(i == 0)
        def _():
            dbin_ref[...] = jnp.zeros_like(dbin_ref)
            ddw_ref[...] = jnp.zeros_like(ddw_ref)

        first = (i % nb) == 0
        last = (i % nb) == nb - 1
        ubuf[0:HALO, :] = jnp.where(first, 0.0, _glu(pp_ref[...]))
        ubuf[HALO:, :] = _glu(pc_ref[...])
        dbuf[0:tt, :] = dc_ref[...]
        dbuf[tt:, :] = jnp.where(last, 0.0, dn_ref[...])
        pc = pc_ref[...]
        for c in range(D // 128):
            cs = slice(c * 128, (c + 1) * 128)
            dcur = dbuf[0:tt, cs]
            du = jnp.zeros((tt, 128), F32)
            for k in range(CONV_W):
                ddw_ref[k:k + 1, cs] += jnp.sum(dcur * ubuf[k + 2:k + 2 + tt, cs], axis=0, keepdims=True)
                du = du + dw_ref[k:k + 1, cs] * dbuf[CONV_W - 1 - k:CONV_W - 1 - k + tt, cs]
            a = pc[:, c * 128:(c + 1) * 128]
            sb = _sigmoid(pc[:, D + c * 128:D + (c + 1) * 128])
            da = du * sb
            db = du * a * sb * (1.0 - sb)
            dp_ref[:, cs] = da.astype(MXU)
            dp_ref[:, D + c * 128:D + (c + 1) * 128] = db.astype(MXU)
            dbin_ref[:, cs] += jnp.sum(da, axis=0, keepdims=True)
            dbin_ref[:, D + c * 128:D + (c + 1) * 128] += jnp.sum(db, axis=0, keepdims=True)

    return _call(
        body, name="conv_bwd_core", grid=(T // tt,),
        in_specs=[pl.BlockSpec((tt, D), lambda i: (i, 0)),
                  pl.BlockSpec((HALO, D), lambda i: (jnp.minimum((i + 1) * r, last_halo), 0)),
                  pl.BlockSpec((tt, 2 * D), lambda i: (i, 0)),
                  pl.BlockSpec((HALO, 2 * D), lambda i: (jnp.maximum(i * r - 1, 0), 0)),
                  pl.BlockSpec((HALO, D), lambda i: (0, 0))],
        out_specs=[pl.BlockSpec((tt, 2 * D), lambda i: (i, 0)),
                   pl.BlockSpec((1, 2 * D), lambda i: (0, 0)),
                   pl.BlockSpec((HALO, D), lambda i: (0, 0))],
        out_shape=[_sds((T, 2 * D), MXU), _sds((1, 2 * D), F32), _sds((HALO, D), F32)],
        scratch=[pltpu.VMEM((tt + HALO, D), F32), pltpu.VMEM((tt + HALO, D), F32)],
    )(du2, du2, p, p, dw)


PH = 16


def _pool_cnt(i, nb, tt, win):
    pos = (i % nb) * tt + lax.broadcasted_iota(jnp.int32, (tt, 1), 0)
    return jnp.minimum(pos + 1, win).astype(F32)


def pool_fwd(x, g, wp, scale, *, S, tt=256):
    T = x.shape[0]
    nb = S // tt
    r = tt // PH

    def body(xc_ref, xp_ref, g_ref, wp_ref, sc_ref, xo_ref, m_ref, hbuf):
        i = pl.program_id(0)
        first = (i % nb) == 0
        gv = g_ref[...]
        hbuf[0:PH, :] = jnp.where(first, 0.0, _rms(xp_ref[...], gv))
        xc = xc_ref[...]
        hbuf[PH:, :] = _rms(xc, gv)
        for gi, win in enumerate(POOL_WIN):
            gs = slice(gi * POOL_G, (gi + 1) * POOL_G)
            acc = hbuf[PH:PH + tt, gs]
            for j in range(1, win):
                acc = acc + hbuf[PH - j:PH - j + tt, gs]
            m = (acc / _pool_cnt(i, nb, tt, win) - hbuf[PH:PH + tt, gs]).astype(MXU)
            m_ref[:, gs] = m
            xo_ref[:, gs] = xc[:, gs] + _dot(m, wp_ref[gi]) * sc_ref[:, gs]

    row = pl.BlockSpec((1, D), lambda i: (0, 0))
    blk = pl.BlockSpec((tt, D), lambda i: (i, 0))
    return _call(
        body, name="pool_fwd", grid=(T // tt,),
        in_specs=[blk, pl.BlockSpec((PH, D), lambda i: (jnp.maximum(i * r - 1, 0), 0)), row,
                  pl.BlockSpec((len(POOL_WIN), POOL_G, POOL_G), lambda i: (0, 0, 0)), row],
        out_specs=[blk, blk],
        out_shape=[_sds((T, D), F32), _sds((T, D), MXU)],
        scratch=[pltpu.VMEM((tt + PH, D), F32)],
    )(x, x, g, wp, scale)


def pool_bwd_rows(dy, m, wp, scale, *, S, tt=256):
    T = dy.shape[0]
    nb = S // tt

    def body(dy_ref, m_ref, wp_ref, sc_ref, dmc_ref, dyp_ref, dsc_ref):
        i = pl.program_id(0)

        @pl.when(i == 0)
        def _():
            dsc_ref[...] = jnp.zeros_like(dsc_ref)

        for gi, win in enumerate(POOL_WIN):
            gs = slice(gi * POOL_G, (gi + 1) * POOL_G)
            dyg = dy_ref[:, gs]
            w = wp_ref[gi]
            dsc_ref[:, gs] += jnp.sum(dyg * _dot(m_ref[:, gs], w), axis=0, keepdims=True)
            dyp = (dyg * sc_ref[:, gs]).astype(MXU)
            dyp_ref[:, gs] = dyp
            dmc_ref[:, gs] = _dot_nt(dyp, w) / _pool_cnt(i, nb, tt, win)

    row = pl.BlockSpec((1, D), lambda i: (0, 0))
    blk = pl.BlockSpec((tt, D), lambda i: (i, 0))
    return _call(
        body, name="pool_bwd_rows", grid=(T // tt,),
        in_specs=[blk, blk, pl.BlockSpec((len(POOL_WIN), POOL_G, POOL_G), lambda i: (0, 0, 0)), row],
        out_specs=[blk, blk, row],
        out_shape=[_sds((T, D), F32), _sds((T, D), MXU), _sds((1, D), F32)],
    )(dy, m, wp, scale)


def pool_bwd_core(dmc, x, g, dres, *, S, tt=256):
    T = x.shape[0]
    nb = S // tt
    r = tt // PH
    last_halo = T // PH - 1

    def body(dc_ref, dn_ref, x_ref, g_ref, dres_ref, dx_ref, dg_ref, dbuf, dh_buf):
        i = pl.program_id(0)

        @pl.when(i == 0)
        def _():
            dg_ref[...] = jnp.zeros_like(dg_ref)

        last = (i % nb) == nb - 1
        dbuf[0:tt, :] = dc_ref[...]
        dbuf[tt:, :] = jnp.where(last, 0.0, dn_ref[...])
        for gi, win in enumerate(POOL_WIN):
            gs = slice(gi * POOL_G, (gi + 1) * POOL_G)
            cur = dbuf[0:tt, gs]
            acc = cur
            for j in range(1, win):
                acc = acc + dbuf[j:j + tt, gs]
            dh_buf[:, gs] = acc - cur * _pool_cnt(i, nb, tt, win)
        dxn, dg = _rms_bwd(dh_buf[...], x_ref[...], g_ref[...])
        dx_ref[...] = dres_ref[...] + dxn
        dg_ref[...] += dg

    row = pl.BlockSpec((1, D), lambda i: (0, 0))
    blk = pl.BlockSpec((tt, D), lambda i: (i, 0))
    return _call(
        body, name="pool_bwd_core", grid=(T // tt,),
        in_specs=[blk, pl.BlockSpec((PH, D), lambda i: (jnp.minimum((i + 1) * r, last_halo), 0)), blk, row, blk],
        out_specs=[blk, row],
        out_shape=[_sds((T, D), F32), _sds((1, D), F32)],
        scratch=[pltpu.VMEM((tt + PH, D), F32), pltpu.VMEM((tt, D), F32)],
    )(dmc, dmc, x, g, dres)


NEG = -1e30


def _tri(n, upper=False):
    r = lax.broadcasted_iota(jnp.int32, (n, n), 0)
    c = lax.broadcasted_iota(jnp.int32, (n, n), 1)
    return (r <= c if upper else r >= c).astype(F32)


def _dot_hi(a, b):
    return jnp.dot(a, b, preferred_element_type=F32, precision=lax.Precision.HIGHEST)


def _log_sigmoid(z):
    return jnp.minimum(z, 0.0) - jnp.log(1.0 + jnp.exp(-jnp.abs(z)))


def fox_cum(fl, bf, *, S, tt=256):
    T = fl.shape[0]
    nb = S // tt

    def body(fl_ref, bf_ref, c_ref, ct_ref, carry):
        i = pl.program_id(0)

        @pl.when((i % nb) == 0)
        def _():
            carry[...] = jnp.zeros_like(carry)

        lf = _log_sigmoid(fl_ref[...] + bf_ref[...])
        c = _dot_hi(_tri(tt), lf) + carry[...]
        c_ref[...] = c
        carry[...] = c[tt - 1:tt, :]
        ct_ref[...] = c.T[0:FOX_H, :]

    return _call(
        body, name="fox_cum", grid=(T // tt,),
        in_specs=[pl.BlockSpec((tt, 128), lambda i: (i, 0)), pl.BlockSpec((1, 128), lambda i: (0, 0))],
        out_specs=[pl.BlockSpec((tt, 128), lambda i: (i, 0)),
                   pl.BlockSpec((None, FOX_H, tt), lambda i: (i // nb, 0, i % nb))],
        out_shape=[_sds((T, 128), F32), _sds((T // S, FOX_H, S), F32)],
        scratch=[pltpu.VMEM((1, 128), F32)],
    )(fl, bf)


def _fox_logits(q_ref, k_ref, c_ref, ct_ref, h, diag, tq):
    hs = slice(h * FOX_DH, (h + 1) * FOX_DH)
    s = _dot_nt(q_ref[:, hs], k_ref[:, hs]) * (FOX_DH ** -0.5) + (c_ref[:, h:h + 1] - ct_ref[h:h + 1, :])
    if diag:
        r = lax.broadcasted_iota(jnp.int32, (tq, tq), 0)
        c = lax.broadcasted_iota(jnp.int32, (tq, tq), 1)
        s = jnp.where(r >= c, s, NEG)
    return s


def _fox_specs(tq, nq, q_of, k_of):
    qrow = lambda col: pl.BlockSpec((tq, D), lambda b, i, j: (b * nq + q_of(i, j), col))
    krow = lambda col: pl.BlockSpec((tq, D), lambda b, i, j: (b * nq + k_of(i, j), col))
    qvec = pl.BlockSpec((tq, 128), lambda b, i, j: (b * nq + q_of(i, j), 0))
    kvec = pl.BlockSpec((tq, 128), lambda b, i, j: (b * nq + k_of(i, j), 0))
    ct = pl.BlockSpec((None, FOX_H, tq), lambda b, i, j: (b, 0, k_of(i, j)))
    return qrow, krow, qvec, kvec, ct


def fox_fwd(qkv, c, ct, *, S, tq=256):
    T = qkv.shape[0]
    nq = S // tq

    def body(q_ref, k_ref, v_ref, c_ref, ct_ref, o_ref, o32_ref, lse_ref, m_sc, l_sc, acc, acc_lo):
        qi = pl.program_id(1)
        ki = pl.program_id(2)

        @pl.when(ki == 0)
        def _():
            m_sc[...] = jnp.full_like(m_sc, NEG)
            l_sc[...] = jnp.zeros_like(l_sc)
            acc[...] = jnp.zeros_like(acc)
            acc_lo[...] = jnp.zeros_like(acc_lo)

        def step(diag):
            for h in range(FOX_H):
                hs = slice(h * FOX_DH, (h + 1) * FOX_DH)
                s = _fox_logits(q_ref, k_ref, c_ref, ct_ref, h, diag, tq)
                m_prev = m_sc[:, h:h + 1]
                m_new = jnp.maximum(m_prev, jnp.max(s, axis=-1, keepdims=True))
                alpha = jnp.exp(m_prev - m_new)
                p = jnp.exp(s - m_new)
                l_sc[:, h:h + 1] = alpha * l_sc[:, h:h + 1] + jnp.sum(p, axis=-1, keepdims=True)
                hi, lo = _split(p)
                acc[:, hs] = alpha * acc[:, hs] + _dot(hi, v_ref[:, hs])
                acc_lo[:, hs] = alpha * acc_lo[:, hs] + _dot(lo, v_ref[:, hs])
                m_sc[:, h:h + 1] = m_new

        @pl.when(ki < qi)
        def _():
            step(False)

        @pl.when(ki == qi)
        def _():
            step(True)
            for h in range(FOX_H):
                hs = slice(h * FOX_DH, (h + 1) * FOX_DH)
                o_ref[:, hs] = (acc[:, hs] / l_sc[:, h:h + 1]).astype(MXU)
                o32_ref[:, hs] = (acc[:, hs] + acc_lo[:, hs]) / l_sc[:, h:h + 1]
            lse_ref[...] = m_sc[...] + jnp.log(jnp.maximum(l_sc[...], 1e-37))

    qrow, krow, qvec, kvec, ctspec = _fox_specs(tq, nq, lambda i, j: i, lambda i, j: jnp.minimum(i, j))
    return _call(
        body, name="fox_fwd", grid=(T // S, nq, nq),
        in_specs=[qrow(0), krow(1), krow(2), qvec, ctspec],
        out_specs=[qrow(0), qrow(0), qvec],
        out_shape=[_sds((T, D), MXU), _sds((T, D), F32), _sds((T, 128), F32)],
        scratch=[pltpu.VMEM((tq, 128), F32), pltpu.VMEM((tq, 128), F32), pltpu.VMEM((tq, D), F32),
                 pltpu.VMEM((tq, D), F32)],
    )(qkv, qkv, qkv, c, ct)


def fox_bwd_dq(qkv, do, o, lse, c, ct, *, S, tq=256):
    T = qkv.shape[0]
    nq = S // tq

    def body(q_ref, k_ref, v_ref, do32_ref, o_ref, lse_ref, c_ref, ct_ref, dq_ref, dl_ref, do_ref, acc):
        qi = pl.program_id(1)
        ki = pl.program_id(2)

        @pl.when(ki == 0)
        def _():
            acc[...] = jnp.zeros_like(acc)
            dl_ref[...] = jnp.zeros_like(dl_ref)
            do_ref[...] = do32_ref[...].astype(MXU)
            for h in range(FOX_H):
                hs = slice(h * FOX_DH, (h + 1) * FOX_DH)
                dl_ref[:, h:h + 1] = jnp.sum(do_ref[:, hs].astype(F32) * o_ref[:, hs], axis=-1, keepdims=True)

        def step(diag):
            for h in range(FOX_H):
                hs = slice(h * FOX_DH, (h + 1) * FOX_DH)
                s = _fox_logits(q_ref, k_ref, c_ref, ct_ref, h, diag, tq)
                p = jnp.exp(s - lse_ref[:, h:h + 1])
                dp = _dot_nt(do_ref[:, hs], v_ref[:, hs])
                hi, lo = _split(p * (dp - dl_ref[:, h:h + 1]))
                acc[:, hs] += _dot(hi, k_ref[:, hs]) + _dot(lo, k_ref[:, hs])

        @pl.when(ki < qi)
        def _():
            step(False)

        @pl.when(ki == qi)
        def _():
            step(True)
            dq_ref[...] = (acc[...] * (FOX_DH ** -0.5)).astype(MXU)

    qrow, krow, qvec, kvec, ctspec = _fox_specs(tq, nq, lambda i, j: i, lambda i, j: jnp.minimum(i, j))
    orow = pl.BlockSpec((tq, D), lambda b, i, j: (b * nq + i, 0))
    return _call(
        body, name="fox_bwd_dq", grid=(T // S, nq, nq),
        in_specs=[qrow(0), krow(1), krow(2), orow, orow, qvec, qvec, ctspec],
        out_specs=[orow, qvec, orow],
        out_shape=[_sds((T, D), MXU), _sds((T, 128), F32), _sds((T, D), MXU)],
        scratch=[pltpu.VMEM((tq, D), F32)],
    )(qkv, qkv, qkv, do, o, lse, c, ct)


def fox_bwd_dkv(qkv, do, lse, delta, c, ct, *, S, tq=256):
    T = qkv.shape[0]
    nq = S // tq

    def body(q_ref, k_ref, v_ref, do_ref, lse_ref, dl_ref, c_ref, ct_ref, dk_ref, dv_ref, dck_ref, dk_acc, dv_acc):
        ki = pl.program_id(1)
        qi = pl.program_id(2)

        @pl.when(qi == 0)
        def _():
            dk_acc[...] = jnp.zeros_like(dk_acc)
            dv_acc[...] = jnp.zeros_like(dv_acc)
            dck_ref[...] = jnp.zeros_like(dck_ref)

        def step(diag):
            ones = jnp.ones((tq, 128), MXU)
            for h in range(FOX_H):
                hs = slice(h * FOX_DH, (h + 1) * FOX_DH)
                s = _fox_logits(q_ref, k_ref, c_ref, ct_ref, h, diag, tq)
                p = jnp.exp(s - lse_ref[:, h:h + 1])
                doh = do_ref[:, hs]
                dv_acc[:, hs] += _dot_tn(p.astype(MXU), doh)
                dp = _dot_nt(doh, v_ref[:, hs])
                hi, lo = _split(p * (dp - dl_ref[:, h:h + 1]))
                dk_acc[:, hs] += _dot_tn(hi, q_ref[:, hs]) + _dot_tn(lo, q_ref[:, hs])
                dck_ref[:, h:h + 1] += (_dot_tn(hi, ones) + _dot_tn(lo, ones))[:, 0:1]

        @pl.when(qi > ki)
        def _():
            step(False)

        @pl.when(qi == ki)
        def _():
            step(True)

        @pl.when(qi == nq - 1)
        def _():
            dk_ref[...] = (dk_acc[...] * (FOX_DH ** -0.5)).astype(MXU)
            dv_ref[...] = dv_acc[...].astype(MXU)

    qrow, krow, qvec, kvec, ctspec = _fox_specs(tq, nq, lambda i, j: jnp.maximum(i, j), lambda i, j: i)
    qo = pl.BlockSpec((tq, D), lambda b, i, j: (b * nq + jnp.maximum(i, j), 0))
    ko = pl.BlockSpec((tq, D), lambda b, i, j: (b * nq + i, 0))
    return _call(
        body, name="fox_bwd_dkv", grid=(T // S, nq, nq),
        in_specs=[qrow(0), krow(1), krow(2), qo, qvec, qvec, qvec, ctspec],
        out_specs=[ko, ko, kvec],
        out_shape=[_sds((T, D), MXU), _sds((T, D), MXU), _sds((T, 128), F32)],
        scratch=[pltpu.VMEM((tq, D), F32), pltpu.VMEM((tq, D), F32)],
    )(qkv, qkv, qkv, do, lse, delta, c, ct)


HW = 128
FOX_T = 512
COL_ONE = FOX_DH + 3
COL_LSE = FOX_DH + 6


def _parts(x):
    hi = x.astype(MXU).astype(F32)
    mid = (x - hi).astype(MXU).astype(F32)
    lo = (x - hi - mid).astype(MXU).astype(F32)
    return [hi, mid, lo]


def _aug(n, cols):
    lane = lax.broadcasted_iota(jnp.int32, (n, HW - FOX_DH), 1)
    out = jnp.zeros((n, HW - FOX_DH), F32)
    for i, cval in enumerate(cols):
        out = jnp.where(lane == i, cval, out)
    return out


def fox_prep(p, c, *, tt=256):
    T = p.shape[0]

    def body(q_ref, k_ref, v_ref, c_ref, qa_ref, ka_ref, va_ref):
        ones = [1.0, 1.0, 1.0]
        for h in range(FOX_H):
            hs = slice(h * FOX_DH, (h + 1) * FOX_DH)
            lo, mid = h * HW, h * HW + FOX_DH
            cp = _parts(c_ref[:, h:h + 1])
            qa_ref[:, lo:mid] = (q_ref[:, hs].astype(F32) * (FOX_DH ** -0.5)).astype(MXU)
            qa_ref[:, mid:lo + HW] = _aug(tt, cp + ones).astype(MXU)
            ka_ref[:, lo:mid] = k_ref[:, hs]
            ka_ref[:, mid:lo + HW] = _aug(tt, ones + [-x for x in cp] + ones).astype(MXU)
            va_ref[:, lo:mid] = v_ref[:, hs]
            va_ref[:, mid:lo + HW] = _aug(tt, ones).astype(MXU)

    wide = pl.BlockSpec((tt, FOX_H * HW), lambda i: (i, 0))
    col = lambda k: pl.BlockSpec((tt, D), lambda i: (i, k))
    return _call(body, name="fox_prep", grid=(T // tt,),
                 in_specs=[col(0), col(1), col(2), pl.BlockSpec((tt, 128), lambda i: (i, 0))],
                 out_specs=[wide, wide, wide], out_shape=[_sds((T, FOX_H * HW), MXU)] * 3)(p, p, p, c)


def _causal(x, fill):
    r = lax.broadcasted_iota(jnp.int32, x.shape, 0)
    c = lax.broadcasted_iota(jnp.int32, x.shape, 1)
    return jnp.where(r >= c, x, fill)


def _wide_specs(tq, nq, q_of, k_of):
    qs = pl.BlockSpec((tq, FOX_H * HW), lambda b, i, j: (b * nq + q_of(i, j), 0))
    ks = pl.BlockSpec((tq, FOX_H * HW), lambda b, i, j: (b * nq + k_of(i, j), 0))
    return qs, ks


def fox2_fwd(qa, ka, va, *, S, tq=256, tk=512):
    T = qa.shape[0]
    tk = min(tk, S)
    nq = S // tq
    nk = S // tk
    r = tk // tq

    def body(q_ref, k_ref, v_ref, o_ref, o32_ref, lse_ref, m_sc, acc, acc_lo):
        qi = pl.program_id(1)
        ki = pl.program_id(2)
        last = qi // r

        @pl.when(ki == 0)
        def _():
            m_sc[...] = jnp.full_like(m_sc, NEG)
            acc[...] = jnp.zeros_like(acc)
            acc_lo[...] = jnp.zeros_like(acc_lo)

        def step(diag):
            for h in range(FOX_H):
                ws = slice(h * HW, (h + 1) * HW)
                s = _dot_nt(q_ref[:, ws], k_ref[:, ws])
                if diag:
                    row = qi * tq + lax.broadcasted_iota(jnp.int32, s.shape, 0)
                    col = ki * tk + lax.broadcasted_iota(jnp.int32, s.shape, 1)
                    s = jnp.where(row >= col, s, NEG)
                m_prev = m_sc[h]
                m_new = jnp.maximum(m_prev, jnp.max(s, axis=-1, keepdims=True))
                alpha = jnp.exp(m_prev - m_new)
                hi, lo = _split(jnp.exp(s - jnp.tile(m_new, (1, tk // 128))))
                acc[h] = alpha * acc[h] + _dot(hi, v_ref[:, ws])
                acc_lo[h] = alpha * acc_lo[h] + _dot(lo, v_ref[:, ws])
                m_sc[h] = m_new

        @pl.when(ki < last)
        def _():
            step(False)

        @pl.when(ki == last)
        def _():
            step(True)
            for h in range(FOX_H):
                hs = slice(h * FOX_DH, (h + 1) * FOX_DH)
                full = acc[h] + acc_lo[h]
                l = full[:, FOX_DH:FOX_DH + 1]
                o_ref[:, hs] = (acc[h][:, :FOX_DH] / l).astype(MXU)
                o32_ref[:, hs] = full[:, :FOX_DH] / l
                lse_ref[:, h:h + 1] = m_sc[h][:, 0:1] + jnp.log(l)

    qs = pl.BlockSpec((tq, FOX_H * HW), lambda b, i, j: (b * nq + i, 0))
    ks = pl.BlockSpec((tk, FOX_H * HW), lambda b, i, j: (b * nk + jnp.minimum(j, i // r), 0))
    orow = pl.BlockSpec((tq, D), lambda b, i, j: (b * nq + i, 0))
    return _call(
        body, name="fox_fwd", grid=(T // S, nq, nk),
        in_specs=[qs, ks, ks],
        out_specs=[orow, orow, pl.BlockSpec((tq, 128), lambda b, i, j: (b * nq + i, 0))],
        out_shape=[_sds((T, D), MXU), _sds((T, D), F32), _sds((T, 128), F32)],
        scratch=[pltpu.VMEM((FOX_H, tq, 128), F32), pltpu.VMEM((FOX_H, tq, HW), F32),
                 pltpu.VMEM((FOX_H, tq, HW), F32)],
    )(qa, ka, va)


def fox2_prep_bwd(do, o32, lse, qa, *, tt=256):
    T = do.shape[0]

    def body(do_ref, o_ref, lse_ref, qa_ref, qb_ref, da_ref):
        lane = lax.broadcasted_iota(jnp.int32, (tt, HW), 1)
        dob = do_ref[...].astype(MXU)
        for h in range(FOX_H):
            hs = slice(h * FOX_DH, (h + 1) * FOX_DH)
            ws = slice(h * HW, (h + 1) * HW)
            doh = dob[:, hs]
            delta = jnp.sum(doh.astype(F32) * o_ref[:, hs], axis=-1, keepdims=True)
            da_ref[:, h * HW:h * HW + FOX_DH] = doh
            da_ref[:, h * HW + FOX_DH:(h + 1) * HW] = _aug(tt, [-x for x in _parts(delta)]).astype(MXU)
            tile = qa_ref[:, ws]
            for i, part in enumerate(_parts(lse_ref[:, h:h + 1])):
                tile = jnp.where(lane == COL_LSE + i, (-part).astype(MXU), tile)
            qb_ref[:, ws] = tile

    wide = pl.BlockSpec((tt, FOX_H * HW), lambda i: (i, 0))
    blk = pl.BlockSpec((tt, D), lambda i: (i, 0))
    return _call(body, name="fox_prep_bwd", grid=(T // tt,),
                 in_specs=[blk, blk, pl.BlockSpec((tt, 128), lambda i: (i, 0)), wide],
                 out_specs=[wide, wide],
                 out_shape=[_sds((T, FOX_H * HW), MXU), _sds((T, FOX_H * HW), MXU)],
                 )(do, o32, lse, qa)


def fox2_dq(qb, ka, va, da, *, S, tq=256):
    T = qb.shape[0]
    nq = S // tq

    def body(q_ref, k_ref, v_ref, d_ref, dq_ref, acc):
        qi = pl.program_id(1)
        ki = pl.program_id(2)

        @pl.when(ki == 0)
        def _():
            acc[...] = jnp.zeros_like(acc)

        def step(diag):
            for h in range(FOX_H):
                ws = slice(h * HW, (h + 1) * HW)
                kh = k_ref[:, ws]
                p = jnp.exp(_dot_nt(q_ref[:, ws], kh))
                if diag:
                    p = _causal(p, 0.0)
                hi, lo = _split(p * _dot_nt(d_ref[:, ws], v_ref[:, ws]))
                acc[h] += _dot(hi, kh) + _dot(lo, kh)

        @pl.when(ki < qi)
        def _():
            step(False)

        @pl.when(ki == qi)
        def _():
            step(True)
            for h in range(FOX_H):
                dq_ref[:, h * FOX_DH:(h + 1) * FOX_DH] = (acc[h][:, :FOX_DH] * (FOX_DH ** -0.5)).astype(MXU)

    qs, ks = _wide_specs(tq, nq, lambda i, j: i, lambda i, j: jnp.minimum(i, j))
    return _call(
        body, name="fox_bwd_dq", grid=(T // S, nq, nq),
        in_specs=[qs, ks, ks, qs],
        out_specs=pl.BlockSpec((tq, D), lambda b, i, j: (b * nq + i, 0)),
        out_shape=_sds((T, D), MXU),
        scratch=[pltpu.VMEM((FOX_H, tq, HW), F32)],
    )(qb, ka, va, da)


def fox2_dkv(qb, ka, va, da, *, S, tq=256):
    T = qb.shape[0]
    nq = S // tq

    def body(q_ref, k_ref, v_ref, d_ref, dk_ref, dv_ref, dck_ref, dk_acc, dv_acc):
        ki = pl.program_id(1)
        qi = pl.program_id(2)

        @pl.when(qi == 0)
        def _():
            dk_acc[...] = jnp.zeros_like(dk_acc)
            dv_acc[...] = jnp.zeros_like(dv_acc)

        def step(diag):
            for h in range(FOX_H):
                ws = slice(h * HW, (h + 1) * HW)
                qh = q_ref[:, ws]
                dh = d_ref[:, ws]
                pt = jnp.exp(_dot_nt(k_ref[:, ws], qh))
                if diag:
                    r = lax.broadcasted_iota(jnp.int32, pt.shape, 0)
                    c = lax.broadcasted_iota(jnp.int32, pt.shape, 1)
                    pt = jnp.where(r <= c, pt, 0.0)
                dv_acc[h] += _dot(pt.astype(MXU), dh)
                hi, lo = _split(pt * _dot_nt(v_ref[:, ws], dh))
                dk_acc[h] += _dot(hi, qh) + _dot(lo, qh)

        @pl.when(qi > ki)
        def _():
            step(False)

        @pl.when(qi == ki)
        def _():
            step(True)

        @pl.when(qi == nq - 1)
        def _():
            dck_ref[...] = jnp.zeros_like(dck_ref)
            for h in range(FOX_H):
                hs = slice(h * FOX_DH, (h + 1) * FOX_DH)
                dk_ref[:, hs] = dk_acc[h][:, :FOX_DH].astype(MXU)
                dv_ref[:, hs] = dv_acc[h][:, :FOX_DH].astype(MXU)
                dck_ref[:, h:h + 1] = dk_acc[h][:, COL_ONE:COL_ONE + 1]

    qs, ks = _wide_specs(tq, nq, lambda i, j: jnp.maximum(i, j), lambda i, j: i)
    ko = pl.BlockSpec((tq, D), lambda b, i, j: (b * nq + i, 0))
    return _call(
        body, name="fox_bwd_dkv", grid=(T // S, nq, nq),
        in_specs=[qs, ks, ks, qs],
        out_specs=[ko, ko, pl.BlockSpec((tq, 128), lambda b, i, j: (b * nq + i, 0))],
        out_shape=[_sds((T, D), MXU), _sds((T, D), MXU), _sds((T, 128), F32)],
        scratch=[pltpu.VMEM((FOX_H, tq, HW), F32), pltpu.VMEM((FOX_H, tq, HW), F32)],
    )(qb, ka, va, da)


def fox_fin(dck, fl, bf, *, S, tt=256):
    T = fl.shape[0]
    nb = S // tt
    nblk = T // tt

    def body(dck_ref, fl_ref, bf_ref, dfl_ref, dbf_ref, carry):
        i = pl.program_id(0)

        @pl.when(i == 0)
        def _():
            dbf_ref[...] = jnp.zeros_like(dbf_ref)

        @pl.when((i % nb) == 0)
        def _():
            carry[...] = jnp.zeros_like(carry)

        lane = lax.broadcasted_iota(jnp.int32, (tt, 128), 1)
        dc = jnp.where(lane < FOX_H, -dck_ref[...], 0.0)
        dlf = _dot_hi(_tri(tt, upper=True), dc) + carry[...]
        carry[...] = dlf[0:1, :]
        dfl = dlf * _sigmoid(-(fl_ref[...] + bf_ref[...]))
        dfl_ref[...] = dfl.astype(MXU)
        dbf_ref[...] += jnp.sum(dfl, axis=0, keepdims=True)

    rev = pl.BlockSpec((tt, 128), lambda i: (nblk - 1 - i, 0))
    row = pl.BlockSpec((1, 128), lambda i: (0, 0))
    return _call(
        body, name="fox_fin", grid=(nblk,),
        in_specs=[rev, rev, row],
        out_specs=[rev, row],
        out_shape=[_sds((T, 128), MXU), _sds((1, 128), F32)],
        scratch=[pltpu.VMEM((1, 128), F32)],
    )(dck, fl, bf)


def lb_fwd(logits):
    def body(l_ref, lb_ref):
        lv = l_ref[...]
        e = jnp.exp(lv - jnp.max(lv, axis=0, keepdims=True))
        p = e / jnp.sum(e, axis=0, keepdims=True)
        lb_ref[...] = p[1:2, :] + p[2:3, :]

    return _call(body, name="lb_fwd", grid=(1,),
                 in_specs=[pl.BlockSpec((DEPTH, D), lambda i: (0, 0))],
                 out_specs=pl.BlockSpec((1, D), lambda i: (0, 0)),
                 out_shape=_sds((1, D), F32))(logits)


def lb_bwd(logits, dlb):
    def body(l_ref, d_ref, o_ref):
        lv = l_ref[...]
        e = jnp.exp(lv - jnp.max(lv, axis=0, keepdims=True))
        p = e / jnp.sum(e, axis=0, keepdims=True)
        lb = p[1:2, :] + p[2:3, :]
        row = lax.broadcasted_iota(jnp.int32, (DEPTH, D), 0)
        sel = ((row == 1) | (row == 2)).astype(F32)
        o_ref[...] = p * (sel - lb) * d_ref[...]

    return _call(body, name="lb_bwd", grid=(1,),
                 in_specs=[pl.BlockSpec((DEPTH, D), lambda i: (0, 0)), pl.BlockSpec((1, D), lambda i: (0, 0))],
                 out_specs=pl.BlockSpec((DEPTH, D), lambda i: (0, 0)),
                 out_shape=_sds((DEPTH, D), F32))(logits, dlb)


def _hgrn_gates(qr, fr, lb):
    sg = _sigmoid(fr)
    sneg = _sigmoid(-fr)
    f = lb + (1.0 - lb) * sg
    kk = (1.0 - lb) * sneg
    G = _dot_hi(_tri(HG_C), jnp.log(f))
    eG = jnp.exp(G)
    einv = jnp.exp(-G)
    elast = jnp.exp(G[HG_C - 1:HG_C, :] - G)
    q = qr * _sigmoid(qr)
    return dict(q=q, kk=kk, f=f, sg=sg, sneg=sneg, eG=eG, einv=einv, elast=elast,
                qg=q * eG, kinv=kk * einv, khat=kk * elast, glast=jnp.exp(G[HG_C - 1:HG_C, :]))


def _tril_mask(x):
    r = lax.broadcasted_iota(jnp.int32, x.shape, 0)
    c = lax.broadcasted_iota(jnp.int32, x.shape, 1)
    return jnp.where(r >= c, x, 0.0)


def hgrn_fwd(p, lb, ng, *, S, R=128):
    T = p.shape[0]
    nr = S // R
    ncr = R // HG_C

    def body(q_ref, f_ref, v_ref, gt_ref, lb_ref, ng_ref, y_ref, o_ref, st_ref, st):
        @pl.when(pl.program_id(1) == 0)
        def _():
            st[...] = jnp.zeros_like(st)

        lbv = lb_ref[...]
        for ch in range(ncr):
            rows = slice(ch * HG_C, (ch + 1) * HG_C)
            gt = _hgrn_gates(q_ref[rows, :], f_ref[rows, :], lbv)
            for h in range(HG_H):
                hs = slice(h * HG_DK, (h + 1) * HG_DK)
                sp = st[h]
                st_ref[ch, h] = sp
                qg = gt["qg"][:, hs].astype(MXU)
                vh = v_ref[rows, hs].astype(MXU)
                A = _tril_mask(_dot_nt(qg, gt["kinv"][:, hs].astype(MXU)))
                o_ref[rows, hs] = _dot_nt(qg, sp.astype(MXU)) + _dot(A.astype(MXU), vh)
                st[h] = sp * gt["glast"][:, hs] + _dot_tn(vh, gt["khat"][:, hs].astype(MXU))
        gate = gt_ref[...]
        sgate = gate * _sigmoid(gate)
        for h in range(HG_H):
            hs = slice(h * HG_DK, (h + 1) * HG_DK)
            oh = o_ref[:, hs]
            r = lax.rsqrt(jnp.mean(oh * oh, axis=-1, keepdims=True) + RMS_EPS)
            y_ref[:, hs] = (oh * r * ng_ref[:, hs] * sgate[:, hs]).astype(MXU)

    col = lambda c: pl.BlockSpec((R, D), lambda b, i: (b * nr + i, c))
    row = pl.BlockSpec((1, D), lambda b, i: (0, 0))
    return _call(
        body, name="hgrn_fwd", grid=(T // S, nr),
        in_specs=[col(0), col(1), col(2), col(3), row, row],
        out_specs=[col(0), col(0),
                   pl.BlockSpec((ncr, HG_H, HG_DK, HG_DK), lambda b, i: (b * nr + i, 0, 0, 0))],
        out_shape=[_sds((T, D), MXU), _sds((T, D), F32), _sds((T // HG_C, HG_H, HG_DK, HG_DK), F32)],
        scratch=[pltpu.VMEM((HG_H, HG_DK, HG_DK), F32)],
    )(p, p, p, p, lb, ng)


def hgrn_bwd(p, o, dyo, states, lb, ng, *, S, R=128):
    T = p.shape[0]
    nr = S // R
    ncr = R // HG_C

    def body(q_ref, f_ref, v_ref, gt_ref, o_ref, dy_ref, st_ref, lb_ref, ng_ref,
             dp_ref, dlb_ref, dng_ref, dst, do_buf, dG_buf, dqb, dkb):
        b = pl.program_id(0)
        i = pl.program_id(1)

        @pl.when(i == 0)
        def _():
            dst[...] = jnp.zeros_like(dst)

        @pl.when((b == 0) & (i == 0))
        def _():
            dlb_ref[...] = jnp.zeros_like(dlb_ref)
            dng_ref[...] = jnp.zeros_like(dng_ref)

        lbv = lb_ref[...]
        gate = gt_ref[...]
        sg_gate = _sigmoid(gate)
        silu_gate = gate * sg_gate
        for h in range(HG_H):
            hs = slice(h * HG_DK, (h + 1) * HG_DK)
            oh = o_ref[:, hs]
            r = lax.rsqrt(jnp.mean(oh * oh, axis=-1, keepdims=True) + RMS_EPS)
            ohat = oh * r
            dyh = dy_ref[:, hs]
            ngh = ng_ref[:, hs]
            dng_ref[:, hs] += jnp.sum(dyh * silu_gate[:, hs] * ohat, axis=0, keepdims=True)
            dp_ref[:, 3 * D + h * HG_DK:3 * D + (h + 1) * HG_DK] = (
                dyh * ohat * ngh * (sg_gate[:, hs] * (1.0 + gate[:, hs] * (1.0 - sg_gate[:, hs])))).astype(MXU)
            dn = dyh * ngh * silu_gate[:, hs]
            do_buf[:, hs] = r * (dn - ohat * jnp.mean(dn * ohat, axis=-1, keepdims=True))

        lastrow = lax.broadcasted_iota(jnp.int32, (HG_C, HG_DK), 0) == HG_C - 1
        for ch in reversed(range(ncr)):
            rows = slice(ch * HG_C, (ch + 1) * HG_C)
            qr = q_ref[rows, :]
            gt = _hgrn_gates(qr, f_ref[rows, :], lbv)
            for h in range(HG_H):
                hs = slice(h * HG_DK, (h + 1) * HG_DK)
                sp = st_ref[ch, h]
                ds = dst[h]
                qg32, kinv32, khat32 = gt["qg"][:, hs], gt["kinv"][:, hs], gt["khat"][:, hs]
                qg, kinv, khat = qg32.astype(MXU), kinv32.astype(MXU), khat32.astype(MXU)
                vh = v_ref[rows, hs].astype(MXU)
                doh = do_buf[rows, hs].astype(MXU)
                dsb = ds.astype(MXU)
                A = _tril_mask(_dot_nt(qg, kinv)).astype(MXU)
                dA = _tril_mask(_dot_nt(doh, vh)).astype(MXU)
                dqg = _dot(doh, sp.astype(MXU)) + _dot(dA, kinv)
                dkinv = _dot_tn(dA, qg)
                dp_ref[rows, 2 * D + h * HG_DK:2 * D + (h + 1) * HG_DK] = (
                    _dot_tn(A, doh) + _dot_nt(khat, dsb)).astype(MXU)
                dkhat = _dot(vh, dsb)
                glast = gt["glast"][:, hs]
                qg32, kinv32, khat32 = qg.astype(F32), kinv.astype(F32), khat.astype(F32)
                extra = (glast * jnp.sum(dsb.astype(F32) * sp.astype(MXU).astype(F32), axis=0, keepdims=True)
                         + jnp.sum(dkhat * khat32, axis=0, keepdims=True))
                dst[h] = ds * glast + _dot_tn(doh, qg)
                dG = dqg * qg32 - dkinv * kinv32 - dkhat * khat32
                dG_buf[:, hs] = dG + jnp.where(lastrow, extra, 0.0)
                dqb[:, hs] = dqg * gt["eG"][:, hs]
                dkb[:, hs] = dkinv * gt["einv"][:, hs] + dkhat * gt["elast"][:, hs]
            dg = _dot_hi(_tri(HG_C, upper=True), dG_buf[...])
            dk = dkb[...]
            sneg, f = gt["sneg"], gt["f"]
            c1 = (1.0 - lbv) * gt["sg"] * sneg
            dp_ref[rows, D:2 * D] = (dg * c1 / f - dk * c1).astype(MXU)
            dlb_ref[...] += jnp.sum(dg * sneg / f - dk * sneg, axis=0, keepdims=True)
            sq = _sigmoid(qr)
            dp_ref[rows, 0:D] = (dqb[...] * (sq * (1.0 + qr * (1.0 - sq)))).astype(MXU)

    rev = lambda b, i: b * nr + nr - 1 - i
    col = lambda c: pl.BlockSpec((R, D), lambda b, i: (rev(b, i), c))
    row = pl.BlockSpec((1, D), lambda b, i: (0, 0))
    return _call(
        body, name="hgrn_bwd", grid=(T // S, nr),
        in_specs=[col(0), col(1), col(2), col(3), col(0), col(0),
                  pl.BlockSpec((ncr, HG_H, HG_DK, HG_DK), lambda b, i: (rev(b, i), 0, 0, 0)), row, row],
        out_specs=[pl.BlockSpec((R, 4 * D), lambda b, i: (rev(b, i), 0)), row, row],
        out_shape=[_sds((T, 4 * D), MXU), _sds((1, D), F32), _sds((1, D), F32)],
        scratch=[pltpu.VMEM((HG_H, HG_DK, HG_DK), F32), pltpu.VMEM((R, D), F32), pltpu.VMEM((HG_C, D), F32),
                 pltpu.VMEM((HG_C, D), F32), pltpu.VMEM((HG_C, D), F32)],
    )(p, p, p, p, o, dyo, states, lb, ng)


def _mm_tn_stack(a, b, *, name, G, M, N, tk, stack):
    E, e, buf = stack
    T = a.shape[-2]

    def spec(arr, width):
        if arr.ndim == 3:
            return pl.BlockSpec((None, tk, width), lambda g, k: (g, k, 0))
        return pl.BlockSpec((tk, width), lambda g, k: (k, 0))

    def body(*refs):
        a_ref, b_ref, o_ref = refs[0], refs[1], refs[-1]

        @pl.when(pl.program_id(1) == 0)
        def _():
            o_ref[...] = jnp.zeros_like(o_ref)

        o_ref[...] += _dot_tn(a_ref[...], b_ref[...])

    in_specs = [spec(a, M), spec(b, N)]
    args = [a, b]
    aliases = {}
    if buf is not None:
        in_specs.append(pl.BlockSpec(memory_space=pl.ANY))
        args.append(buf)
        aliases = {2: 0}
    return pl.pallas_call(
        body, name=name, grid=(G, T // tk), in_specs=in_specs,
        out_specs=pl.BlockSpec((None, None, M, N), lambda g, k: (g, e, 0, 0)),
        out_shape=_sds((G, E, M, N), F32), input_output_aliases=aliases,
        compiler_params=pltpu.CompilerParams(dimension_semantics=("arbitrary", "arbitrary"),
                                             vmem_limit_bytes=VMEM_LIMIT))(*args)


MESH = pl.DeviceIdType.MESH
ANY = pl.BlockSpec(memory_space=pl.ANY)


def _pos():
    return lax.axis_index("x"), lax.axis_index("y"), lax.axis_index("c")


def _other_chips(x, y):
    return [(1 - x, y), (x, 1 - y), (1 - x, 1 - y)]


def _comm_call(body, *, name, args, out_shape, n_sem):
    return pl.pallas_call(
        body, name=name, in_specs=[ANY] * len(args), out_specs=[ANY] * len(out_shape), out_shape=out_shape,
        scratch_shapes=[pltpu.SemaphoreType.DMA((n_sem,)), pltpu.SemaphoreType.DMA((n_sem,)),
                        pltpu.SemaphoreType.DMA((len(args),))],
    )(*args)


def all_gather_chips(xs):
    n = len(xs)

    def body(*refs):
        x_refs, o_refs = refs[:n], refs[n:2 * n]
        ssem, rsem, lsem = refs[2 * n:]
        x, y, c = _pos()
        me = 2 * x + y
        chips = _other_chips(x, y)
        sib = (x, y, 1 - c)

        def rc(src, dst, idx, dev):
            return pltpu.make_async_remote_copy(src_ref=src, dst_ref=dst, send_sem=ssem.at[idx], recv_sem=rsem.at[idx],
                                                device_id=dev, device_id_type=MESH)

        local, started = [], []
        for t in range(n):
            hr = xs[t].shape[0] // 2
            mine = pl.ds(c * hr, hr)
            cp = pltpu.make_async_copy(x_refs[t], o_refs[t].at[me], lsem.at[t])
            cp.start()
            local.append(cp)
            for k, (cx, cy) in enumerate(chips):
                cp = rc(x_refs[t].at[mine], o_refs[t].at[me, mine], 6 * t + k, (cx, cy, c))
                cp.start()
                started.append(cp)
        for t in range(n):
            hr = xs[t].shape[0] // 2
            mine = pl.ds(c * hr, hr)
            for k, (cx, cy) in enumerate(chips):
                landed = o_refs[t].at[2 * cx + cy, mine]
                rc(landed, landed, 6 * t + k, (cx, cy, c)).wait_recv()
                cp = rc(landed, landed, 6 * t + 3 + k, sib)
                cp.start()
                started.append(cp)
        for t in range(n):
            hr = xs[t].shape[0] // 2
            theirs = pl.ds((1 - c) * hr, hr)
            for k, (cx, cy) in enumerate(chips):
                other = o_refs[t].at[2 * cx + cy, theirs]
                rc(other, other, 6 * t + 3 + k, sib).wait_recv()
        for cp in started:
            cp.wait_send()
        for cp in local:
            cp.wait()

    outs = _comm_call(body, name="all_gather_chips", args=list(xs),
                      out_shape=[_sds((NSH,) + a.shape, a.dtype) for a in xs], n_sem=6 * n)
    return list(outs)


def sibling_half_exchange(gs):
    n = len(gs)

    def body(*refs):
        g_refs, o_refs = refs[:n], refs[n:2 * n]
        ssem, rsem, _ = refs[2 * n:]
        x, y, c = _pos()
        cps = []
        for t in range(n):
            hr = gs[t].shape[1] // 2
            for j in range(NSH):
                cp = pltpu.make_async_remote_copy(
                    src_ref=g_refs[t].at[j, pl.ds((1 - c) * hr, hr)], dst_ref=o_refs[t].at[j],
                    send_sem=ssem.at[NSH * t + j], recv_sem=rsem.at[NSH * t + j],
                    device_id=(x, y, 1 - c), device_id_type=MESH)
                cp.start()
                cps.append(cp)
        for cp in cps:
            cp.wait()

    outs = _comm_call(body, name="sibling_half_exchange", args=list(gs),
                      out_shape=[_sds((NSH, g.shape[1] // 2, g.shape[2]), g.dtype) for g in gs], n_sem=NSH * n)
    return list(outs)


def chip_scatter(ss):
    n = len(ss)

    def body(*refs):
        s_refs, o_refs = refs[:n], refs[n:2 * n]
        ssem, rsem, _ = refs[2 * n:]
        x, y, c = _pos()
        cps = []
        for t in range(n):
            for k, (cx, cy) in enumerate(_other_chips(x, y)):
                cp = pltpu.make_async_remote_copy(
                    src_ref=s_refs[t].at[2 * cx + cy], dst_ref=o_refs[t].at[k],
                    send_sem=ssem.at[3 * t + k], recv_sem=rsem.at[3 * t + k],
                    device_id=(cx, cy, c), device_id_type=MESH)
                cp.start()
                cps.append(cp)
        for cp in cps:
            cp.wait()

    outs = _comm_call(body, name="chip_scatter", args=list(ss),
                      out_shape=[_sds((3,) + s.shape[1:], s.dtype) for s in ss], n_sem=3 * n)
    return list(outs)


def sibling_exchange(rs):
    n = len(rs)

    def body(*refs):
        r_refs, o_refs = refs[:n], refs[n:2 * n]
        ssem, rsem, _ = refs[2 * n:]
        x, y, c = _pos()
        cps = []
        for t in range(n):
            cp = pltpu.make_async_remote_copy(
                src_ref=r_refs[t], dst_ref=o_refs[t], send_sem=ssem.at[t], recv_sem=rsem.at[t],
                device_id=(x, y, 1 - c), device_id_type=MESH)
            cp.start()
            cps.append(cp)
        for cp in cps:
            cp.wait()

    outs = _comm_call(body, name="sibling_exchange", args=list(rs),
                      out_shape=[_sds(r.shape, r.dtype) for r in rs], n_sem=n)
    return list(outs)


def all_gather_devices(v):
    def body(v_ref, o_ref, ssem, rsem, lsem):
        x, y, c = _pos()
        me = 4 * x + 2 * y + c
        loc = pltpu.make_async_copy(v_ref, o_ref.at[me], lsem.at[0])
        loc.start()
        cps = []
        k = 0
        for fx in range(2):
            for fy in range(2):
                for fc in range(2):
                    if fx == fy == fc == 0:
                        continue
                    cp = pltpu.make_async_remote_copy(
                        src_ref=v_ref, dst_ref=o_ref.at[me], send_sem=ssem.at[k], recv_sem=rsem.at[k],
                        device_id=(x ^ fx, y ^ fy, c ^ fc), device_id_type=MESH)
                    cp.start()
                    src = 4 * (x ^ fx) + 2 * (y ^ fy) + (c ^ fc)
                    cps.append((cp, o_ref.at[src], k))
                    k += 1
        for cp, landed, k in cps:
            cp.wait_send()
            pltpu.make_async_remote_copy(
                src_ref=landed, dst_ref=landed, send_sem=ssem.at[k], recv_sem=rsem.at[k],
                device_id=(x, y, c), device_id_type=MESH).wait_recv()
        loc.wait()

    return _comm_call(body, name="all_gather_devices", args=[v],
                      out_shape=[_sds((8,) + v.shape, v.dtype)], n_sem=7)[0]


def _call_sp(body, *, name, grid, in_specs, out_specs, out_shape, pos, args):
    return pl.pallas_call(
        body, name=name,
        grid_spec=pltpu.PrefetchScalarGridSpec(num_scalar_prefetch=1, grid=grid, in_specs=in_specs,
                                               out_specs=out_specs),
        out_shape=out_shape,
        compiler_params=pltpu.CompilerParams(dimension_semantics=("arbitrary",) * len(grid),
                                             vmem_limit_bytes=VMEM_LIMIT))(pos, *args)


def _rows_tile(r):
    for t in (512, 256, 128, 64, 32, 16, 8):
        if r % t == 0:
            return t
    raise ValueError(r)


def pair_sum(g, r, pos):
    _, R, C = g.shape
    hr = R // 2
    tr = _rows_tile(hr)
    nbh = hr // tr

    def body(p_ref, g_ref, r_ref, o_ref):
        o_ref[...] = (g_ref[...] + r_ref[...]).astype(MXU)

    return _call_sp(
        body, name="pair_sum", grid=(NSH, nbh), pos=pos, args=[g, r],
        in_specs=[pl.BlockSpec((None, tr, C), lambda j, i, p: (j, p[0] * nbh + i, 0)),
                  pl.BlockSpec((None, tr, C), lambda j, i, p: (j, i, 0))],
        out_specs=pl.BlockSpec((None, tr, C), lambda j, i, p: (j, i, 0)),
        out_shape=_sds((NSH, hr, C), MXU))


def reduce_own(g, r_sib, r_ici, pos):
    _, R, C = g.shape
    hr = R // 2
    tr = _rows_tile(hr)
    nbh = hr // tr

    def body(p_ref, g_ref, rs_ref, ri_ref, o_ref):
        s = g_ref[...] + rs_ref[...]
        for k in range(3):
            s = s + ri_ref[k].astype(F32)
        o_ref[...] = s

    return _call_sp(
        body, name="reduce_own", grid=(nbh,), pos=pos, args=[g, r_sib, r_ici],
        in_specs=[pl.BlockSpec((None, tr, C), lambda i, p: (p[1], p[0] * nbh + i, 0)),
                  pl.BlockSpec((None, tr, C), lambda i, p: (p[1], i, 0)),
                  pl.BlockSpec((3, tr, C), lambda i, p: (0, i, 0))],
        out_specs=pl.BlockSpec((tr, C), lambda i, p: (i, 0)),
        out_shape=_sds((hr, C), F32))


def _adamw_math(w, g, m, v):
    m = ADAM_B1 * m + (1.0 - ADAM_B1) * g
    v = ADAM_B2 * v + (1.0 - ADAM_B2) * (g * g)
    m_hat = m / (1.0 - ADAM_B1 ** ADAM_STEP)
    v_hat = v / (1.0 - ADAM_B2 ** ADAM_STEP)
    delta = -ADAM_LR * (m_hat / (jnp.sqrt(v_hat) + ADAM_EPS) + ADAM_WD * w)
    return delta, m, v


def adamw_halves(w, m, v, ga, gb, pos):
    R, C = w.shape
    hr = R // 2
    tr = _rows_tile(hr)
    nbh = hr // tr

    def body(p_ref, w_ref, m_ref, v_ref, ga_ref, gb_ref, g_ref, d_ref, mo_ref, vo_ref):
        mine = (pl.program_id(0) // nbh) == p_ref[0]
        g = jnp.where(mine, ga_ref[...], gb_ref[...])
        g_ref[...] = g
        d_ref[...], mo_ref[...], vo_ref[...] = _adamw_math(w_ref[...], g, m_ref[...], v_ref[...])

    blk = pl.BlockSpec((tr, C), lambda i, p: (i, 0))
    return _call_sp(
        body, name="adamw_halves", grid=(R // tr,), pos=pos, args=[w, m, v, ga, gb],
        in_specs=[blk, blk, blk,
                  pl.BlockSpec((tr, C), lambda i, p: (jnp.where(i // nbh == p[0], i % nbh, 0), 0)),
                  pl.BlockSpec((tr, C), lambda i, p: (jnp.where(i // nbh == p[0], 0, i % nbh), 0))],
        out_specs=[blk, blk, blk, blk],
        out_shape=[_sds((R, C), F32)] * 4)


def adamw_sum(gall, w, m, v):
    n, R, C = gall.shape

    def body(ga_ref, w_ref, m_ref, v_ref, g_ref, d_ref, mo_ref, vo_ref):
        g = ga_ref[0]
        for k in range(1, n):
            g = g + ga_ref[k]
        g_ref[...] = g
        d_ref[...], mo_ref[...], vo_ref[...] = _adamw_math(w_ref[...], g, m_ref[...], v_ref[...])

    blk = pl.BlockSpec((R, C), lambda i: (0, 0))
    return _call(body, name="adamw_sum", grid=(1,),
                 in_specs=[pl.BlockSpec((n, R, C), lambda i: (0, 0, 0)), blk, blk, blk],
                 out_specs=[blk, blk, blk, blk], out_shape=[_sds((R, C), F32)] * 4)(gall, w, m, v)


_WEIGHTS = ['ffn_norm', 'ffn_w_gate', 'ffn_w_up', 'ffn_w_down', 'mix_norm', 'final_norm', 'conv_w_in', 'conv_b_in',
            'conv_dw', 'conv_dw_b', 'conv_ln_g', 'conv_ln_b', 'conv_w_out', 'fox_w_in', 'fox_b_f', 'fox_w_out',
            'hgrn_w_in', 'hgrn_lb_logits', 'hgrn_norm', 'hgrn_w_out', 'pool_w', 'pool_scale']
_BIG = ['ffn_w_gate', 'ffn_w_up', 'ffn_w_down', 'conv_w_in', 'conv_w_out', 'fox_w_in', 'fox_w_out',
        'hgrn_w_in', 'hgrn_w_out', 'pool_w']
_SHARDED_SMALL = ['ffn_norm', 'conv_dw', 'hgrn_norm', 'pool_scale']
_REPLICATED = ['mix_norm', 'final_norm', 'conv_b_in', 'conv_dw_b', 'conv_ln_g', 'conv_ln_b', 'fox_b_f', 'hgrn_lb_logits']
FOX_N = 3 * D + FOX_H
FOX_NP = 3200
QS = D // NSH


def _pad_rows(a, rows):
    return jnp.pad(a, ((0, rows - a.shape[0]), (0, 0)))


def _pack_sharded_small(get):
    return jnp.concatenate([get('ffn_norm').reshape(8, -1), _pad_rows(get('conv_dw')[0], 32),
                            get('hgrn_norm'), get('pool_scale'), jnp.zeros((6, get('pool_scale').shape[1]), F32)], axis=0)


def _pack_replicated(get):
    return jnp.concatenate([get('mix_norm'), get('final_norm').reshape(1, D), get('conv_b_in').reshape(2, D),
                            get('conv_dw_b'), get('conv_ln_g'), get('conv_ln_b'),
                            jnp.pad(get('fox_b_f'), ((0, 0), (0, D - FOX_H))), get('hgrn_lb_logits'),
                            jnp.zeros((9, D), F32)], axis=0)


def _unpack_replicated(p):
    return {'mix_norm': p[0:4], 'final_norm': p[4], 'conv_b_in': p[5:7].reshape(1, 2 * D), 'conv_dw_b': p[7:8],
            'conv_ln_g': p[8:9], 'conv_ln_b': p[9:10], 'fox_b_f': p[10:11, :FOX_H], 'hgrn_lb_logits': p[11:15]}


def _unpack_sharded_small(p):
    return {'ffn_norm': p[0:8].reshape(DEPTH, 2, -1), 'conv_dw': p[8:8 + CONV_W][None],
            'hgrn_norm': p[40:41], 'pool_scale': p[41:42]}


def kernel(x, ffn_norm, ffn_w_gate, ffn_w_up, ffn_w_down, mix_norm, final_norm, conv_w_in, conv_b_in, conv_dw, conv_dw_b, conv_ln_g, conv_ln_b, conv_w_out, fox_w_in, fox_b_f, fox_w_out, hgrn_w_in, hgrn_lb_logits, hgrn_norm, hgrn_w_out, pool_w, pool_scale, loss_target, m_ffn_norm, m_ffn_w_gate, m_ffn_w_up, m_ffn_w_down, m_mix_norm, m_final_norm, m_conv_w_in, m_conv_b_in, m_conv_dw, m_conv_dw_b, m_conv_ln_g, m_conv_ln_b, m_conv_w_out, m_fox_w_in, m_fox_b_f, m_fox_w_out, m_hgrn_w_in, m_hgrn_lb_logits, m_hgrn_norm, m_hgrn_w_out, m_pool_w, m_pool_scale, v_ffn_norm, v_ffn_w_gate, v_ffn_w_up, v_ffn_w_down, v_mix_norm, v_final_norm, v_conv_w_in, v_conv_b_in, v_conv_dw, v_conv_dw_b, v_conv_ln_g, v_conv_ln_b, v_conv_w_out, v_fox_w_in, v_fox_b_f, v_fox_w_out, v_hgrn_w_in, v_hgrn_lb_logits, v_hgrn_norm, v_hgrn_w_out, v_pool_w, v_pool_scale):
    W = dict(ffn_norm=ffn_norm, ffn_w_gate=ffn_w_gate, ffn_w_up=ffn_w_up, ffn_w_down=ffn_w_down, mix_norm=mix_norm, final_norm=final_norm, conv_w_in=conv_w_in, conv_b_in=conv_b_in, conv_dw=conv_dw, conv_dw_b=conv_dw_b, conv_ln_g=conv_ln_g, conv_ln_b=conv_ln_b, conv_w_out=conv_w_out, fox_w_in=fox_w_in, fox_b_f=fox_b_f, fox_w_out=fox_w_out, hgrn_w_in=hgrn_w_in, hgrn_lb_logits=hgrn_lb_logits, hgrn_norm=hgrn_norm, hgrn_w_out=hgrn_w_out, pool_w=pool_w, pool_scale=pool_scale)
    M = dict(ffn_norm=m_ffn_norm, ffn_w_gate=m_ffn_w_gate, ffn_w_up=m_ffn_w_up, ffn_w_down=m_ffn_w_down, mix_norm=m_mix_norm, final_norm=m_final_norm, conv_w_in=m_conv_w_in, conv_b_in=m_conv_b_in, conv_dw=m_conv_dw, conv_dw_b=m_conv_dw_b, conv_ln_g=m_conv_ln_g, conv_ln_b=m_conv_ln_b, conv_w_out=m_conv_w_out, fox_w_in=m_fox_w_in, fox_b_f=m_fox_b_f, fox_w_out=m_fox_w_out, hgrn_w_in=m_hgrn_w_in, hgrn_lb_logits=m_hgrn_lb_logits, hgrn_norm=m_hgrn_norm, hgrn_w_out=m_hgrn_w_out, pool_w=m_pool_w, pool_scale=m_pool_scale)
    V = dict(ffn_norm=v_ffn_norm, ffn_w_gate=v_ffn_w_gate, ffn_w_up=v_ffn_w_up, ffn_w_down=v_ffn_w_down, mix_norm=v_mix_norm, final_norm=v_final_norm, conv_w_in=v_conv_w_in, conv_b_in=v_conv_b_in, conv_dw=v_conv_dw, conv_dw_b=v_conv_dw_b, conv_ln_g=v_conv_ln_g, conv_ln_b=v_conv_ln_b, conv_w_out=v_conv_w_out, fox_w_in=v_fox_w_in, fox_b_f=v_fox_b_f, fox_w_out=v_fox_w_out, hgrn_w_in=v_hgrn_w_in, hgrn_lb_logits=v_hgrn_lb_logits, hgrn_norm=v_hgrn_norm, hgrn_w_out=v_hgrn_w_out, pool_w=v_pool_w, pool_scale=v_pool_scale)

    px, py, pc = _pos()
    jme = 2 * px + py
    pos = jnp.stack([pc, jme]).astype(jnp.int32)
    S = x.shape[1]
    T = x.shape[0] * S
    x2 = x.reshape(T, D)
    tgt = loss_target.reshape(T, D)

    flat = lambda a: a.reshape(-1, a.shape[-1])
    gathered = all_gather_chips([flat(W[n]).astype(MXU) for n in _BIG] + [_pack_sharded_small(W.get)])
    G = dict(zip(_BIG, gathered[:-1]))
    small = gathered[-1].transpose(1, 0, 2).reshape(48, D)
    ffn_norm_f, conv_dw_f = small[0:8], small[8:40]
    hgrn_norm_f, pool_scale_f = small[40:41], small[41:42]
    wg_all = G['ffn_w_gate'].reshape(NSH, 2 * DEPTH, D, FS)
    wu_all = G['ffn_w_up'].reshape(NSH, 2 * DEPTH, D, FS)
    wd_all = G['ffn_w_down'].reshape(NSH, 2 * DEPTH, FS, D)
    conv_wi = G['conv_w_in']
    conv_wo = G['conv_w_out'].reshape(D, D)
    fox_full = jnp.pad(G['fox_w_in'].transpose(1, 0, 2).reshape(D, FOX_N), ((0, 0), (0, FOX_NP - FOX_N)))
    fox_w5 = fox_full.reshape(D, 5, FOX_NP // 5).transpose(1, 0, 2)
    fox_wf = fox_full[:, 3 * D:][None]
    fox_bf = jnp.pad(fox_b_f, ((0, 0), (0, 128 - FOX_H)))
    fox_wo = G['fox_w_out'].reshape(D, D)
    hgrn_wi = G['hgrn_w_in']
    hgrn_wo = G['hgrn_w_out'].reshape(D, D)
    pool_wf = G['pool_w'].reshape(NSH, 4, 64, POOL_G).transpose(1, 0, 2, 3).reshape(4, POOL_G, POOL_G)
    conv_bi = conv_b_in.reshape(NSH, 1, 2 * D // NSH)

    def ffn_f(xs, e):
        xo, h, a, b = ffn_fwd(xs, ffn_norm_f[e:e + 1], wg_all, wu_all, wd_all, e, tm=min(1024, xs.shape[0]))
        return xo, (xs, h, a, b)

    saved = []
    xs = x2
    lb = lb_fwd(hgrn_lb_logits)
    for i in range(DEPTH):
        xs, r0 = ffn_f(xs, 2 * i)
        gm = mix_norm[i:i + 1]
        xin = xs
        if i == 0:
            p, h = norm_mm(xin, gm, conv_wi, conv_bi, name="conv_in", out_dtype=F32)
            u2, u4 = conv_fwd_core(p, conv_dw_f, conv_dw_b, conv_ln_g, conv_ln_b, S=S)
            xs = mm_res(u4, conv_wo, xin, name="conv_out")
            rm = (xin, p, h, u2, u4)
        elif i == 1:
            p, h = norm_mm(xin, gm, fox_w5, None, name="fox_in", out_dtype=MXU)
            fl, _ = norm_mm(xin, gm, fox_wf, None, name="fox_in_f", out_dtype=F32)
            cq, _ = fox_cum(fl, fox_bf, S=S)
            qa, ka, va = fox_prep(p, cq)
            o, o32, lse = fox2_fwd(qa, ka, va, S=S, tq=min(FOX_T, S), tk=min(FOX_T, S))
            xs = mm_res(o, fox_wo, xin, name="fox_out")
            rm = (xin, h, fl, qa, ka, va, o, o32, lse)
        elif i == 2:
            p, h = norm_mm(xin, gm, hgrn_wi, None, name="hgrn_in", out_dtype=F32)
            yh, oh, st = hgrn_fwd(p, lb, hgrn_norm_f, S=S)
            xs = mm_res(yh, hgrn_wo, xin, name="hgrn_out")
            rm = (xin, p, h, yh, oh, st)
        else:
            xs, mp = pool_fwd(xin, gm, pool_wf, pool_scale_f, S=S)
            rm = (xin, mp)
        xs, r1 = ffn_f(xs, 2 * i + 1)
        saved.append((r0, rm, r1))

    loss8, dx, d_final = loss_head(xs, final_norm.reshape(1, D), tgt)

    wgt_all, wut_all, wdt_all = (w.transpose(0, 1, 3, 2) for w in (wg_all, wu_all, wd_all))
    conv_wit, fox_w5t, hgrn_wit = (w.transpose(0, 2, 1) for w in (conv_wi, fox_w5, hgrn_wi))
    conv_wot, fox_wot, hgrn_wot = conv_wo.T, fox_wo.T, hgrn_wo.T
    gb = {'g': None, 'u': None, 'd': None}
    d_ffn_norm = [None] * (2 * DEPTH)
    d_mix_norm = [None] * DEPTH
    gbig = {}
    gsm = {}

    def ffn_b(dy, res, e):
        xin, h, a, b = res
        dxo, da, db, z, dyh, dg = ffn_bwd_dx(xin, ffn_norm_f[e:e + 1], dy, a, b, wgt_all, wut_all, wdt_all, e)
        tk = min(2048, xin.shape[0])
        gb['g'] = mm_tn(h, da, name="ffn_dwg", G=NSH, M=D, N=FS, tk=tk, stack=(2 * DEPTH, e, gb['g']))
        gb['u'] = mm_tn(h, db, name="ffn_dwu", G=NSH, M=D, N=FS, tk=tk, stack=(2 * DEPTH, e, gb['u']))
        gb['d'] = mm_tn(z, dyh, name="ffn_dwd", G=NSH, M=FS, N=D, tk=tk, stack=(2 * DEPTH, e, gb['d']))
        d_ffn_norm[e] = dg
        return dxo

    for i in reversed(range(DEPTH)):
        r0, rm, r1 = saved[i]
        dx = ffn_b(dx, r1, 2 * i + 1)
        gm = mix_norm[i:i + 1]
        if i == 0:
            xin, p, h, u2, u4 = rm
            du2, dyb, gsm['conv_ln_g'], gsm['conv_ln_b'], gsm['conv_dw_b'] = conv_bwd_rows(dx, conv_wot, u2, conv_ln_g, conv_ln_b)
            dp, gsm['conv_b_in'], ddw = conv_bwd_core(du2, p, conv_dw_f, S=S)
            gsm['conv_dw'] = ddw
            gbig['conv_w_in'] = mm_tn(h, dp, name="conv_dwin", G=NSH, M=D, N=2 * D // NSH, b_step=1)
            gbig['conv_w_out'] = mm_tn(u4, dyb, name="conv_dwout", G=NSH, M=QS, N=D, a_step=1)
            dx, d_mix_norm[i] = inproj_bwd(dp, conv_wit, xin, gm, dx, name="conv_in_bwd")
        elif i == 1:
            xin, h, fl, qa, ka, va, o, o32, lse = rm
            do, dyb = mm_nt(dx, fox_wot, name="fox_out_bwd")
            qb, da = fox2_prep_bwd(do, o32, lse, qa)
            dq = fox2_dq(qb, ka, va, da, S=S, tq=min(FOX_T, S))
            dk, dv, dck = fox2_dkv(qb, ka, va, da, S=S, tq=min(FOX_T, S))
            dfl, dbf = fox_fin(dck, fl, fox_bf, S=S)
            gsm['fox_b_f'] = dbf
            dp = jnp.concatenate([dq, dk, dv, dfl], axis=1)
            dw5 = mm_tn(h, dp, name="fox_dwin", G=5, M=D, N=FOX_NP // 5, b_step=1)
            dwf = dw5.transpose(1, 0, 2).reshape(D, FOX_NP)[:, :FOX_N]
            gbig['fox_w_in'] = dwf.reshape(D, NSH, FOX_N // NSH).transpose(1, 0, 2)
            gbig['fox_w_out'] = mm_tn(o, dyb, name="fox_dwout", G=NSH, M=QS, N=D, a_step=1)
            dx, d_mix_norm[i] = inproj_bwd(dp, fox_w5t, xin, gm, dx, name="fox_in_bwd")
        elif i == 2:
            xin, p, h, yh, oh, st = rm
            dyo, dyb = mm_nt(dx, hgrn_wot, name="hgrn_out_bwd")
            dp, dlb, gsm['hgrn_norm'] = hgrn_bwd(p, oh, dyo, st, lb, hgrn_norm_f, S=S)
            gsm['hgrn_lb_logits'] = lb_bwd(hgrn_lb_logits, dlb)
            gbig['hgrn_w_in'] = mm_tn(h, dp, name="hgrn_dwin", G=NSH, M=D, N=D, b_step=1)
            gbig['hgrn_w_out'] = mm_tn(yh, dyb, name="hgrn_dwout", G=NSH, M=QS, N=D, a_step=1)
            dx, d_mix_norm[i] = inproj_bwd(dp, hgrn_wit, xin, gm, dx, name="hgrn_in_bwd")
        else:
            xin, mp = rm
            dmc, dyp, gsm['pool_scale'] = pool_bwd_rows(dx, mp, pool_wf, pool_scale_f, S=S)
            dwp = mm_tn(mp, dyp, name="pool_dw", G=4, M=POOL_G, N=POOL_G, a_step=1, b_step=1)
            gbig['pool_w'] = dwp.reshape(4, NSH, 64, POOL_G).transpose(1, 0, 2, 3).reshape(NSH, 4 * 64, POOL_G)
            dx, d_mix_norm[i] = pool_bwd_core(dmc, xin, gm, dx, S=S)
        dx = ffn_b(dx, r0, 2 * i)

    gbig['ffn_w_gate'] = gb['g'].reshape(NSH, 2 * DEPTH * D, FS)
    gbig['ffn_w_up'] = gb['u'].reshape(NSH, 2 * DEPTH * D, FS)
    gbig['ffn_w_down'] = gb['d'].reshape(NSH, 2 * DEPTH * FS, D)

    gl = [gbig[n] for n in _BIG]
    r_sib = sibling_half_exchange(gl)
    s16 = [pair_sum(g, r, pos) for g, r in zip(gl, r_sib)]
    r_ici = chip_scatter(s16)
    red = [reduce_own(g, rs, ri, pos) for g, rs, ri in zip(gl, r_sib, r_ici)]
    oth = sibling_exchange(red)
    out = {}
    for n, ga, gb_ in zip(_BIG, red, oth):
        res = adamw_halves(flat(W[n]), flat(M[n]), flat(V[n]), ga, gb_, pos)
        out[n] = [r.reshape(W[n].shape) for r in res]

    gfull = {'mix_norm': jnp.concatenate(d_mix_norm, axis=0), 'final_norm': d_final,
             'conv_b_in': gsm['conv_b_in'], 'conv_dw_b': gsm['conv_dw_b'], 'conv_ln_g': gsm['conv_ln_g'],
             'conv_ln_b': gsm['conv_ln_b'], 'fox_b_f': gsm['fox_b_f'][:, :FOX_H], 'hgrn_lb_logits': gsm['hgrn_lb_logits'],
             'ffn_norm': jnp.concatenate(d_ffn_norm, axis=0), 'conv_dw': gsm['conv_dw'][None, :CONV_W],
             'hgrn_norm': gsm['hgrn_norm'], 'pool_scale': gsm['pool_scale']}
    gpack = jnp.concatenate([_pack_replicated(gfull.get), _pack_sharded_small(gfull.get)], axis=0)
    gall = all_gather_devices(gpack)
    rep = adamw_sum(gall[:, :24], _pack_replicated(W.get), _pack_replicated(M.get), _pack_replicated(V.get))
    rep = [_unpack_replicated(r) for r in rep]
    for n in _REPLICATED:
        out[n] = [r[n].reshape(W[n].shape) for r in rep]
    gsh = lax.dynamic_slice_in_dim(gall[:, 24:], jme * QS, QS, axis=2)
    shd = adamw_sum(gsh, _pack_sharded_small(W.get), _pack_sharded_small(M.get), _pack_sharded_small(V.get))
    shd = [_unpack_sharded_small(r) for r in shd]
    for n in _SHARDED_SMALL:
        out[n] = [r[n].reshape(W[n].shape) for r in shd]

    loss = lax.psum(loss8[0, 0], ("x", "y", "c"))
    res = [loss, dx.reshape(x.shape)]
    for k in range(4):
        res += [out[n][k] for n in _WEIGHTS]
    return tuple(res)
```

```python
import functools

import jax
import jax.numpy as jnp
from jax import lax
from jax.experimental import pallas as pl
from jax.experimental.pallas import tpu as pltpu

D = 1024
F = 2816
NSH = 4
FS = F // NSH
DEPTH = 4
RMS_EPS = 1e-6
LN_EPS = 1e-5
CONV_W = 31
HALO = 32
FOX_H = 16
FOX_DH = 64
HG_H = 8
HG_DK = 128
HG_C = 32
POOL_WIN = (2, 4, 8, 16)
POOL_G = 256
MXU = jnp.bfloat16
F32 = jnp.float32
VMEM_LIMIT = 52 * 1024 * 1024

ADAM_LR = 0.001
ADAM_B1 = 0.9
ADAM_B2 = 0.999
ADAM_EPS = 1e-08
ADAM_WD = 0.01
ADAM_STEP = 10


def _call(body, *, name, grid, in_specs, out_specs, out_shape, scratch=()):
    return pl.pallas_call(
        body, name=name, grid=grid, in_specs=in_specs, out_specs=out_specs, out_shape=out_shape,
        scratch_shapes=list(scratch),
        compiler_params=pltpu.CompilerParams(dimension_semantics=("arbitrary",) * len(grid),
                                             vmem_limit_bytes=VMEM_LIMIT))


def _dot(a, b):
    return jnp.dot(a, b, preferred_element_type=F32)


def _dot_nt(a, b):
    return lax.dot_general(a, b, (((1,), (1,)), ((), ())), preferred_element_type=F32)


def _dot_tn(a, b):
    return lax.dot_general(a, b, (((0,), (0,)), ((), ())), preferred_element_type=F32)


def _split(x):
    hi = x.astype(MXU)
    return hi, (x - hi.astype(F32)).astype(MXU)


def _sigmoid(x):
    return 1.0 / (1.0 + jnp.exp(-x))


def _rms(x, g):
    r = lax.rsqrt(jnp.mean(x * x, axis=-1, keepdims=True) + RMS_EPS)
    return x * r * g


def _rms_bwd(dh, x, g):
    r = lax.rsqrt(jnp.mean(x * x, axis=-1, keepdims=True) + RMS_EPS)
    xh = x * r
    dhg = dh * g
    dx = r * (dhg - xh * jnp.mean(dhg * xh, axis=-1, keepdims=True))
    return dx, jnp.sum(dh * xh, axis=0, keepdims=True)


def _sds(shape, dtype):
    return jax.ShapeDtypeStruct(shape, dtype)


def _wspec(w, e):
    if w.ndim == 3:
        return pl.BlockSpec((None,) + w.shape[1:], lambda i, j: (j, 0, 0))
    return pl.BlockSpec((None, None) + w.shape[2:], lambda i, j: (j, e, 0, 0))


def ffn_fwd(x, g, wg, wu, wd, e=0, *, tm=512):
    T = x.shape[0]

    def body(x_ref, g_ref, wg_ref, wu_ref, wd_ref, xo_ref, h_ref, a_ref, b_ref, acc_ref):
        j = pl.program_id(1)

        @pl.when(j == 0)
        def _():
            h_ref[...] = _rms(x_ref[...], g_ref[...]).astype(MXU)
            acc_ref[...] = jnp.zeros_like(acc_ref)

        h = h_ref[...]
        a = _dot(h, wg_ref[...])
        b = _dot(h, wu_ref[...])
        a_ref[...] = a.astype(MXU)
        b_ref[...] = b.astype(MXU)
        z = (a * _sigmoid(a) * b).astype(MXU)
        acc_ref[...] += _dot(z, wd_ref[...])

        @pl.when(j == NSH - 1)
        def _():
            xo_ref[...] = x_ref[...] + 0.5 * acc_ref[...]

    return _call(
        body, name="ffn_fwd", grid=(T // tm, NSH),
        in_specs=[pl.BlockSpec((tm, D), lambda i, j: (i, 0)),
                  pl.BlockSpec((1, D), lambda i, j: (0, 0)),
                  _wspec(wg, e), _wspec(wu, e), _wspec(wd, e)],
        out_specs=[pl.BlockSpec((tm, D), lambda i, j: (i, 0)),
                   pl.BlockSpec((tm, D), lambda i, j: (i, 0)),
                   pl.BlockSpec((None, tm, FS), lambda i, j: (j, i, 0)),
                   pl.BlockSpec((None, tm, FS), lambda i, j: (j, i, 0))],
        out_shape=[_sds((T, D), F32), _sds((T, D), MXU), _sds((NSH, T, FS), MXU), _sds((NSH, T, FS), MXU)],
        scratch=[pltpu.VMEM((tm, D), F32)],
    )(x, g, wg, wu, wd)


def ffn_bwd_dx(x, g, dy, a, b, wg, wu, wd, e=0, *, tm=512):
    T = x.shape[0]

    def body(x_ref, g_ref, dy_ref, a_ref, b_ref, wg_ref, wu_ref, wd_ref,
             dx_ref, da_ref, db_ref, z_ref, dyh_ref, dg_ref, acc_ref):
        i = pl.program_id(0)
        j = pl.program_id(1)

        @pl.when(j == 0)
        def _():
            dyh_ref[...] = (0.5 * dy_ref[...]).astype(MXU)
            acc_ref[...] = jnp.zeros_like(acc_ref)

        @pl.when((i == 0) & (j == 0))
        def _():
            dg_ref[...] = jnp.zeros_like(dg_ref)

        dz = _dot_nt(dyh_ref[...], wd_ref[...])
        av = a_ref[...].astype(F32)
        bv = b_ref[...].astype(F32)
        s = _sigmoid(av)
        sa = av * s
        da = (dz * bv * (s * (1.0 + av * (1.0 - s)))).astype(MXU)
        db = (dz * sa).astype(MXU)
        da_ref[...] = da
        db_ref[...] = db
        z_ref[...] = (sa * bv).astype(MXU)
        acc_ref[...] += _dot_nt(da, wg_ref[...]) + _dot_nt(db, wu_ref[...])

        @pl.when(j == NSH - 1)
        def _():
            dxn, dg = _rms_bwd(acc_ref[...], x_ref[...], g_ref[...])
            dx_ref[...] = dy_ref[...] + dxn
            dg_ref[...] += dg

    return _call(
        body, name="ffn_bwd_dx", grid=(T // tm, NSH),
        in_specs=[pl.BlockSpec((tm, D), lambda i, j: (i, 0)),
                  pl.BlockSpec((1, D), lambda i, j: (0, 0)),
                  pl.BlockSpec((tm, D), lambda i, j: (i, 0)),
                  pl.BlockSpec((None, tm, FS), lambda i, j: (j, i, 0)),
                  pl.BlockSpec((None, tm, FS), lambda i, j: (j, i, 0)),
                  _wspec(wg, e), _wspec(wu, e), _wspec(wd, e)],
        out_specs=[pl.BlockSpec((tm, D), lambda i, j: (i, 0)),
                   pl.BlockSpec((None, tm, FS), lambda i, j: (j, i, 0)),
                   pl.BlockSpec((None, tm, FS), lambda i, j: (j, i, 0)),
                   pl.BlockSpec((None, tm, FS), lambda i, j: (j, i, 0)),
                   pl.BlockSpec((tm, D), lambda i, j: (i, 0)),
                   pl.BlockSpec((1, D), lambda i, j: (0, 0))],
        out_shape=[_sds((T, D), F32), _sds((NSH, T, FS), MXU), _sds((NSH, T, FS), MXU),
                   _sds((NSH, T, FS), MXU), _sds((T, D), MXU), _sds((1, D), F32)],
        scratch=[pltpu.VMEM((tm, D), F32)],
    )(x, g, dy, a, b, wg, wu, wd)


def mm_tn(a, b, *, name, G, M, N, a_step=0, b_step=0, tk=512, stack=None):
    T = a.shape[-2]
    if stack is not None:
        return _mm_tn_stack(a, b, name=name, G=G, M=M, N=N, tk=tk, stack=stack)

    def spec(arr, width, step):
        if arr.ndim == 3:
            return pl.BlockSpec((None, tk, width), lambda g, k: (g, k, 0))
        return pl.BlockSpec((tk, width), lambda g, k: (k, g * step))

    def body(a_ref, b_ref, o_ref):
        @pl.when(pl.program_id(1) == 0)
        def _():
            o_ref[...] = jnp.zeros_like(o_ref)

        o_ref[...] += _dot_tn(a_ref[...], b_ref[...])

    return _call(
        body, name=name, grid=(G, T // tk),
        in_specs=[spec(a, M, a_step), spec(b, N, b_step)],
        out_specs=pl.BlockSpec((None, M, N), lambda g, k: (g, 0, 0)),
        out_shape=_sds((G, M, N), F32),
    )(a, b)


def norm_mm(x, g, wb, bias, *, name, out_dtype, tm=1024):
    T = x.shape[0]
    tm = min(tm, T)
    G, _, ns = wb.shape
    has_bias = bias is not None

    def body(*refs):
        if has_bias:
            x_ref, g_ref, w_ref, bias_ref, p_ref, h_ref = refs
        else:
            x_ref, g_ref, w_ref, p_ref, h_ref = refs

        @pl.when(pl.program_id(1) == 0)
        def _():
            h_ref[...] = _rms(x_ref[...], g_ref[...]).astype(MXU)

        p = _dot(h_ref[...], w_ref[...])
        if has_bias:
            p = p + bias_ref[...]
        p_ref[...] = p.astype(out_dtype)

    in_specs = [pl.BlockSpec((tm, D), lambda i, j: (i, 0)),
                pl.BlockSpec((1, D), lambda i, j: (0, 0)),
                pl.BlockSpec((None, D, ns), lambda i, j: (j, 0, 0))]
    args = [x, g, wb]
    if has_bias:
        in_specs.append(pl.BlockSpec((None, 1, ns), lambda i, j: (j, 0, 0)))
        args.append(bias)
    return _call(
        body, name=name, grid=(T // tm, G), in_specs=in_specs,
        out_specs=[pl.BlockSpec((tm, ns), lambda i, j: (i, j)),
                   pl.BlockSpec((tm, D), lambda i, j: (i, 0))],
        out_shape=[_sds((T, G * ns), out_dtype), _sds((T, D), MXU)],
    )(*args)


def mm_res(y, w, x, *, name, tm=1024):
    T, K = y.shape
    tm = min(tm, T)

    def body(y_ref, w_ref, x_ref, o_ref):
        o_ref[...] = x_ref[...] + _dot(y_ref[...], w_ref[...])

    return _call(
        body, name=name, grid=(T // tm,),
        in_specs=[pl.BlockSpec((tm, K), lambda i: (i, 0)),
                  pl.BlockSpec((K, D), lambda i: (0, 0)),
                  pl.BlockSpec((tm, D), lambda i: (i, 0))],
        out_specs=pl.BlockSpec((tm, D), lambda i: (i, 0)),
        out_shape=_sds((T, D), F32),
    )(y, w, x)


def mm_nt(a, wt, *, name, tm=1024):
    T, K = a.shape
    tm = min(tm, T)
    N = wt.shape[1]
    w = wt

    def body(a_ref, w_ref, o_ref, ab_ref):
        ab = a_ref[...].astype(MXU)
        ab_ref[...] = ab
        o_ref[...] = _dot(ab, w_ref[...])

    return _call(
        body, name=name, grid=(T // tm,),
        in_specs=[pl.BlockSpec((tm, K), lambda i: (i, 0)),
                  pl.BlockSpec((K, N), lambda i: (0, 0))],
        out_specs=[pl.BlockSpec((tm, N), lambda i: (i, 0)),
                   pl.BlockSpec((tm, K), lambda i: (i, 0))],
        out_shape=[_sds((T, N), F32), _sds((T, K), MXU)],
    )(a, w)


def inproj_bwd(dp, wb, x, g, dres, *, name, tm=1024):
    T = x.shape[0]
    tm = min(tm, T)
    G, ns, _ = wb.shape

    def body(dp_ref, w_ref, x_ref, g_ref, dres_ref, dx_ref, dg_ref, acc_ref):
        i = pl.program_id(0)
        j = pl.program_id(1)

        @pl.when(j == 0)
        def _():
            acc_ref[...] = jnp.zeros_like(acc_ref)

        @pl.when((i == 0) & (j == 0))
        def _():
            dg_ref[...] = jnp.zeros_like(dg_ref)

        acc_ref[...] += _dot(dp_ref[...], w_ref[...])

        @pl.when(j == G - 1)
        def _():
            dxn, dg = _rms_bwd(acc_ref[...], x_ref[...], g_ref[...])
            dx_ref[...] = dres_ref[...] + dxn
            dg_ref[...] += dg

    return _call(
        body, name=name, grid=(T // tm, G),
        in_specs=[pl.BlockSpec((tm, ns), lambda i, j: (i, j)),
                  pl.BlockSpec((None, ns, D), lambda i, j: (j, 0, 0)),
                  pl.BlockSpec((tm, D), lambda i, j: (i, 0)),
                  pl.BlockSpec((1, D), lambda i, j: (0, 0)),
                  pl.BlockSpec((tm, D), lambda i, j: (i, 0))],
        out_specs=[pl.BlockSpec((tm, D), lambda i, j: (i, 0)),
                   pl.BlockSpec((1, D), lambda i, j: (0, 0))],
        out_shape=[_sds((T, D), F32), _sds((1, D), F32)],
        scratch=[pltpu.VMEM((tm, D), F32)],
    )(dp, wb, x, g, dres)


def loss_head(x, gf, tgt, *, tm=512):
    T = x.shape[0]

    def body(x_ref, g_ref, t_ref, loss_ref, dx_ref, dg_ref):
        @pl.when(pl.program_id(0) == 0)
        def _():
            loss_ref[...] = jnp.zeros_like(loss_ref)
            dg_ref[...] = jnp.zeros_like(dg_ref)

        xv = x_ref[...]
        gv = g_ref[...]
        e = _rms(xv, gv) - t_ref[...]
        loss_ref[...] += (0.5 / D) * jnp.sum(e * e)
        dxn, dg = _rms_bwd(e * (1.0 / D), xv, gv)
        dx_ref[...] = dxn
        dg_ref[...] += dg

    return _call(
        body, name="loss_head", grid=(T // tm,),
        in_specs=[pl.BlockSpec((tm, D), lambda i: (i, 0)),
                  pl.BlockSpec((1, D), lambda i: (0, 0)),
                  pl.BlockSpec((tm, D), lambda i: (i, 0))],
        out_specs=[pl.BlockSpec((8, 128), lambda i: (0, 0)),
                   pl.BlockSpec((tm, D), lambda i: (i, 0)),
                   pl.BlockSpec((1, D), lambda i: (0, 0))],
        out_shape=[_sds((8, 128), F32), _sds((T, D), F32), _sds((1, D), F32)],
    )(x, gf, tgt)


def _glu(p):
    return p[:, :D] * _sigmoid(p[:, D:])


def _ln_stats(u):
    mu = jnp.mean(u, axis=-1, keepdims=True)
    xc = u - mu
    rstd = lax.rsqrt(jnp.mean(xc * xc, axis=-1, keepdims=True) + LN_EPS)
    return xc * rstd, rstd


def conv_fwd_core(p, dw, dwb, lng, lnb, *, S, tt=256):
    T = p.shape[0]
    nb = S // tt
    r = tt // HALO

    def body(pc_ref, pp_ref, dw_ref, dwb_ref, lng_ref, lnb_ref, u2_ref, u4_ref, ubuf):
        first = (pl.program_id(0) % nb) == 0
        ubuf[0:HALO, :] = jnp.where(first, 0.0, _glu(pp_ref[...]))
        ubuf[HALO:, :] = _glu(pc_ref[...])
        for c in range(D // 128):
            cs = slice(c * 128, (c + 1) * 128)
            acc = jnp.zeros((tt, 128), F32)
            for k in range(CONV_W):
                acc = acc + dw_ref[k:k + 1, cs] * ubuf[k + 2:k + 2 + tt, cs]
            u2_ref[:, cs] = acc + dwb_ref[:, cs]
        xh, _ = _ln_stats(u2_ref[...])
        u3 = xh * lng_ref[...] + lnb_ref[...]
        u4_ref[...] = (u3 * _sigmoid(u3)).astype(MXU)

    row = pl.BlockSpec((1, D), lambda i: (0, 0))
    return _call(
        body, name="conv_fwd_core", grid=(T // tt,),
        in_specs=[pl.BlockSpec((tt, 2 * D), lambda i: (i, 0)),
                  pl.BlockSpec((HALO, 2 * D), lambda i: (jnp.maximum(i * r - 1, 0), 0)),
                  pl.BlockSpec((HALO, D), lambda i: (0, 0)), row, row, row],
        out_specs=[pl.BlockSpec((tt, D), lambda i: (i, 0)), pl.BlockSpec((tt, D), lambda i: (i, 0))],
        out_shape=[_sds((T, D), F32), _sds((T, D), MXU)],
        scratch=[pltpu.VMEM((tt + HALO, D), F32)],
    )(p, p, dw, dwb, lng, lnb)


def conv_bwd_rows(dy, wout, u2, lng, lnb, *, tm=512):
    T = dy.shape[0]

    def body(dy_ref, w_ref, u2_ref, lng_ref, lnb_ref, du2_ref, dyb_ref, dlng_ref, dlnb_ref, ddwb_ref):
        @pl.when(pl.program_id(0) == 0)
        def _():
            dlng_ref[...] = jnp.zeros_like(dlng_ref)
            dlnb_ref[...] = jnp.zeros_like(dlnb_ref)
            ddwb_ref[...] = jnp.zeros_like(ddwb_ref)

        dyb = dy_ref[...].astype(MXU)
        dyb_ref[...] = dyb
        du4 = _dot(dyb, w_ref[...])
        xh, rstd = _ln_stats(u2_ref[...])
        lng_v = lng_ref[...]
        u3 = xh * lng_v + lnb_ref[...]
        s = _sigmoid(u3)
        du3 = du4 * (s * (1.0 + u3 * (1.0 - s)))
        dlng_ref[...] += jnp.sum(du3 * xh, axis=0, keepdims=True)
        dlnb_ref[...] += jnp.sum(du3, axis=0, keepdims=True)
        dxh = du3 * lng_v
        du2 = rstd * (dxh - jnp.mean(dxh, axis=-1, keepdims=True)
                      - xh * jnp.mean(dxh * xh, axis=-1, keepdims=True))
        du2_ref[...] = du2
        ddwb_ref[...] += jnp.sum(du2, axis=0, keepdims=True)

    row = pl.BlockSpec((1, D), lambda i: (0, 0))
    blk = pl.BlockSpec((tm, D), lambda i: (i, 0))
    return _call(
        body, name="conv_bwd_rows", grid=(T // tm,),
        in_specs=[blk, pl.BlockSpec((D, D), lambda i: (0, 0)), blk, row, row],
        out_specs=[blk, blk, row, row, row],
        out_shape=[_sds((T, D), F32), _sds((T, D), MXU), _sds((1, D), F32), _sds((1, D), F32), _sds((1, D), F32)],
    )(dy, wout, u2, lng, lnb)


def conv_bwd_core(du2, p, dw, *, S, tt=256):
    T = p.shape[0]
    nb = S // tt
    r = tt // HALO
    last_halo = T // HALO - 1

    def body(dc_ref, dn_ref, pc_ref, pp_ref, dw_ref, dp_ref, dbin_ref, ddw_ref, ubuf, dbuf):
        i = pl.program_id(0)

        @pl.when(i == 0)
        def _():
            dbin_ref[...] = jnp.zeros_like(dbin_ref)
            ddw_ref[...] = jnp.zeros_like(ddw_ref)

        first = (i % nb) == 0
        last = (i % nb) == nb - 1
        ubuf[0:HALO, :] = jnp.where(first, 0.0, _glu(pp_ref[...]))
        ubuf[HALO:, :] = _glu(pc_ref[...])
        dbuf[0:tt, :] = dc_ref[...]
        dbuf[tt:, :] = jnp.where(last, 0.0, dn_ref[...])
        pc = pc_ref[...]
        for c in range(D // 128):
            cs = slice(c * 128, (c + 1) * 128)
            dcur = dbuf[0:tt, cs]
            du = jnp.zeros((tt, 128), F32)
            for k in range(CONV_W):
                ddw_ref[k:k + 1, cs] += jnp.sum(dcur * ubuf[k + 2:k + 2 + tt, cs], axis=0, keepdims=True)
                du = du + dw_ref[k:k + 1, cs] * dbuf[CONV_W - 1 - k:CONV_W - 1 - k + tt, cs]
            a = pc[:, c * 128:(c + 1) * 128]
            sb = _sigmoid(pc[:, D + c * 128:D + (c + 1) * 128])
            da = du * sb
            db = du * a * sb * (1.0 - sb)
            dp_ref[:, cs] = da.astype(MXU)
            dp_ref[:, D + c * 128:D + (c + 1) * 128] = db.astype(MXU)
            dbin_ref[:, cs] += jnp.sum(da, axis=0, keepdims=True)
            dbin_ref[:, D + c * 128:D + (c + 1) * 128] += jnp.sum(db, axis=0, keepdims=True)

    return _call(
        body, name="conv_bwd_core", grid=(T // tt,),
        in_specs=[pl.BlockSpec((tt, D), lambda i: (i, 0)),
                  pl.BlockSpec((HALO, D), lambda i: (jnp.minimum((i + 1) * r, last_halo), 0)),
                  pl.BlockSpec((tt, 2 * D), lambda i: (i, 0)),
                  pl.BlockSpec((HALO, 2 * D), lambda i: (jnp.maximum(i * r - 1, 0), 0)),
                  pl.BlockSpec((HALO, D), lambda i: (0, 0))],
        out_specs=[pl.BlockSpec((tt, 2 * D), lambda i: (i, 0)),
                   pl.BlockSpec((1, 2 * D), lambda i: (0, 0)),
                   pl.BlockSpec((HALO, D), lambda i: (0, 0))],
        out_shape=[_sds((T, 2 * D), MXU), _sds((1, 2 * D), F32), _sds((HALO, D), F32)],
        scratch=[pltpu.VMEM((tt + HALO, D), F32), pltpu.VMEM((tt + HALO, D), F32)],
    )(du2, du2, p, p, dw)


PH = 16


def _pool_cnt(i, nb, tt, win):
    pos = (i % nb) * tt + lax.broadcasted_iota(jnp.int32, (tt, 1), 0)
    return jnp.minimum(pos + 1, win).astype(F32)


def pool_fwd(x, g, wp, scale, *, S, tt=256):
    T = x.shape[0]
    nb = S // tt
    r = tt // PH

    def body(xc_ref, xp_ref, g_ref, wp_ref, sc_ref, xo_ref, m_ref, hbuf):
        i = pl.program_id(0)
        first = (i % nb) == 0
        gv = g_ref[...]
        hbuf[0:PH, :] = jnp.where(first, 0.0, _rms(xp_ref[...], gv))
        xc = xc_ref[...]
        hbuf[PH:, :] = _rms(xc, gv)
        for gi, win in enumerate(POOL_WIN):
            gs = slice(gi * POOL_G, (gi + 1) * POOL_G)
            acc = hbuf[PH:PH + tt, gs]
            for j in range(1, win):
                acc = acc + hbuf[PH - j:PH - j + tt, gs]
            m = (acc / _pool_cnt(i, nb, tt, win) - hbuf[PH:PH + tt, gs]).astype(MXU)
            m_ref[:, gs] = m
            xo_ref[:, gs] = xc[:, gs] + _dot(m, wp_ref[gi]) * sc_ref[:, gs]

    row = pl.BlockSpec((1, D), lambda i: (0, 0))
    blk = pl.BlockSpec((tt, D), lambda i: (i, 0))
    return _call(
        body, name="pool_fwd", grid=(T // tt,),
        in_specs=[blk, pl.BlockSpec((PH, D), lambda i: (jnp.maximum(i * r - 1, 0), 0)), row,
                  pl.BlockSpec((len(POOL_WIN), POOL_G, POOL_G), lambda i: (0, 0, 0)), row],
        out_specs=[blk, blk],
        out_shape=[_sds((T, D), F32), _sds((T, D), MXU)],
        scratch=[pltpu.VMEM((tt + PH, D), F32)],
    )(x, x, g, wp, scale)


def pool_bwd_rows(dy, m, wp, scale, *, S, tt=256):
    T = dy.shape[0]
    nb = S // tt

    def body(dy_ref, m_ref, wp_ref, sc_ref, dmc_ref, dyp_ref, dsc_ref):
        i = pl.program_id(0)

        @pl.when(i == 0)
        def _():
            dsc_ref[...] = jnp.zeros_like(dsc_ref)

        for gi, win in enumerate(POOL_WIN):
            gs = slice(gi * POOL_G, (gi + 1) * POOL_G)
            dyg = dy_ref[:, gs]
            w = wp_ref[gi]
            dsc_ref[:, gs] += jnp.sum(dyg * _dot(m_ref[:, gs], w), axis=0, keepdims=True)
            dyp = (dyg * sc_ref[:, gs]).astype(MXU)
            dyp_ref[:, gs] = dyp
            dmc_ref[:, gs] = _dot_nt(dyp, w) / _pool_cnt(i, nb, tt, win)

    row = pl.BlockSpec((1, D), lambda i: (0, 0))
    blk = pl.BlockSpec((tt, D), lambda i: (i, 0))
    return _call(
        body, name="pool_bwd_rows", grid=(T // tt,),
        in_specs=[blk, blk, pl.BlockSpec((len(POOL_WIN), POOL_G, POOL_G), lambda i: (0, 0, 0)), row],
        out_specs=[blk, blk, row],
        out_shape=[_sds((T, D), F32), _sds((T, D), MXU), _sds((1, D), F32)],
    )(dy, m, wp, scale)


def pool_bwd_core(dmc, x, g, dres, *, S, tt=256):
    T = x.shape[0]
    nb = S // tt
    r = tt // PH
    last_halo = T // PH - 1

    def body(dc_ref, dn_ref, x_ref, g_ref, dres_ref, dx_ref, dg_ref, dbuf, dh_buf):
        i = pl.program_id(0)

        @pl.when(i == 0)
        def _():
            dg_ref[...] = jnp.zeros_like(dg_ref)

        last = (i % nb) == nb - 1
        dbuf[0:tt, :] = dc_ref[...]
        dbuf[tt:, :] = jnp.where(last, 0.0, dn_ref[...])
        for gi, win in enumerate(POOL_WIN):
            gs = slice(gi * POOL_G, (gi + 1) * POOL_G)
            cur = dbuf[0:tt, gs]
            acc = cur
            for j in range(1, win):
                acc = acc + dbuf[j:j + tt, gs]
            dh_buf[:, gs] = acc - cur * _pool_cnt(i, nb, tt, win)
        dxn, dg = _rms_bwd(dh_buf[...], x_ref[...], g_ref[...])
        dx_ref[...] = dres_ref[...] + dxn
        dg_ref[...] += dg

    row = pl.BlockSpec((1, D), lambda i: (0, 0))
    blk = pl.BlockSpec((tt, D), lambda i: (i, 0))
    return _call(
        body, name="pool_bwd_core", grid=(T // tt,),
        in_specs=[blk, pl.BlockSpec((PH, D), lambda i: (jnp.minimum((i + 1) * r, last_halo), 0)), blk, row, blk],
        out_specs=[blk, row],
        out_shape=[_sds((T, D), F32), _sds((1, D), F32)],
        scratch=[pltpu.VMEM((tt + PH, D), F32), pltpu.VMEM((tt, D), F32)],
    )(dmc, dmc, x, g, dres)


NEG = -1e30


def _tri(n, upper=False):
    r = lax.broadcasted_iota(jnp.int32, (n, n), 0)
    c = lax.broadcasted_iota(jnp.int32, (n, n), 1)
    return (r <= c if upper else r >= c).astype(F32)


def _dot_hi(a, b):
    return jnp.dot(a, b, preferred_element_type=F32, precision=lax.Precision.HIGHEST)


def _log_sigmoid(z):
    return jnp.minimum(z, 0.0) - jnp.log(1.0 + jnp.exp(-jnp.abs(z)))


def fox_cum(fl, bf, *, S, tt=256):
    T = fl.shape[0]
    nb = S // tt

    def body(fl_ref, bf_ref, c_ref, ct_ref, carry):
        i = pl.program_id(0)

        @pl.when((i % nb) == 0)
        def _():
            carry[...] = jnp.zeros_like(carry)

        lf = _log_sigmoid(fl_ref[...] + bf_ref[...])
        c = _dot_hi(_tri(tt), lf) + carry[...]
        c_ref[...] = c
        carry[...] = c[tt - 1:tt, :]
        ct_ref[...] = c.T[0:FOX_H, :]

    return _call(
        body, name="fox_cum", grid=(T // tt,),
        in_specs=[pl.BlockSpec((tt, 128), lambda i: (i, 0)), pl.BlockSpec((1, 128), lambda i: (0, 0))],
        out_specs=[pl.BlockSpec((tt, 128), lambda i: (i, 0)),
                   pl.BlockSpec((None, FOX_H, tt), lambda i: (i // nb, 0, i % nb))],
        out_shape=[_sds((T, 128), F32), _sds((T // S, FOX_H, S), F32)],
        scratch=[pltpu.VMEM((1, 128), F32)],
    )(fl, bf)


def _fox_logits(q_ref, k_ref, c_ref, ct_ref, h, diag, tq):
    hs = slice(h * FOX_DH, (h + 1) * FOX_DH)
    s = _dot_nt(q_ref[:, hs], k_ref[:, hs]) * (FOX_DH ** -0.5) + (c_ref[:, h:h + 1] - ct_ref[h:h + 1, :])
    if diag:
        r = lax.broadcasted_iota(jnp.int32, (tq, tq), 0)
        c = lax.broadcasted_iota(jnp.int32, (tq, tq), 1)
        s = jnp.where(r >= c, s, NEG)
    return s


def _fox_specs(tq, nq, q_of, k_of):
    qrow = lambda col: pl.BlockSpec((tq, D), lambda b, i, j: (b * nq + q_of(i, j), col))
    krow = lambda col: pl.BlockSpec((tq, D), lambda b, i, j: (b * nq + k_of(i, j), col))
    qvec = pl.BlockSpec((tq, 128), lambda b, i, j: (b * nq + q_of(i, j), 0))
    kvec = pl.BlockSpec((tq, 128), lambda b, i, j: (b * nq + k_of(i, j), 0))
    ct = pl.BlockSpec((None, FOX_H, tq), lambda b, i, j: (b, 0, k_of(i, j)))
    return qrow, krow, qvec, kvec, ct


def fox_fwd(qkv, c, ct, *, S, tq=256):
    T = qkv.shape[0]
    nq = S // tq

    def body(q_ref, k_ref, v_ref, c_ref, ct_ref, o_ref, o32_ref, lse_ref, m_sc, l_sc, acc, acc_lo):
        qi = pl.program_id(1)
        ki = pl.program_id(2)

        @pl.when(ki == 0)
        def _():
            m_sc[...] = jnp.full_like(m_sc, NEG)
            l_sc[...] = jnp.zeros_like(l_sc)
            acc[...] = jnp.zeros_like(acc)
            acc_lo[...] = jnp.zeros_like(acc_lo)

        def step(diag):
            for h in range(FOX_H):
                hs = slice(h * FOX_DH, (h + 1) * FOX_DH)
                s = _fox_logits(q_ref, k_ref, c_ref, ct_ref, h, diag, tq)
                m_prev = m_sc[:, h:h + 1]
                m_new = jnp.maximum(m_prev, jnp.max(s, axis=-1, keepdims=True))
                alpha = jnp.exp(m_prev - m_new)
                p = jnp.exp(s - m_new)
                l_sc[:, h:h + 1] = alpha * l_sc[:, h:h + 1] + jnp.sum(p, axis=-1, keepdims=True)
                hi, lo = _split(p)
                acc[:, hs] = alpha * acc[:, hs] + _dot(hi, v_ref[:, hs])
                acc_lo[:, hs] = alpha * acc_lo[:, hs] + _dot(lo, v_ref[:, hs])
                m_sc[:, h:h + 1] = m_new

        @pl.when(ki < qi)
        def _():
            step(False)

        @pl.when(ki == qi)
        def _():
            step(True)
            for h in range(FOX_H):
                hs = slice(h * FOX_DH, (h + 1) * FOX_DH)
                o_ref[:, hs] = (acc[:, hs] / l_sc[:, h:h + 1]).astype(MXU)
                o32_ref[:, hs] = (acc[:, hs] + acc_lo[:, hs]) / l_sc[:, h:h + 1]
            lse_ref[...] = m_sc[...] + jnp.log(jnp.maximum(l_sc[...], 1e-37))

    qrow, krow, qvec, kvec, ctspec = _fox_specs(tq, nq, lambda i, j: i, lambda i, j: jnp.minimum(i, j))
    return _call(
        body, name="fox_fwd", grid=(T // S, nq, nq),
        in_specs=[qrow(0), krow(1), krow(2), qvec, ctspec],
        out_specs=[qrow(0), qrow(0), qvec],
        out_shape=[_sds((T, D), MXU), _sds((T, D), F32), _sds((T, 128), F32)],
        scratch=[pltpu.VMEM((tq, 128), F32), pltpu.VMEM((tq, 128), F32), pltpu.VMEM((tq, D), F32),
                 pltpu.VMEM((tq, D), F32)],
    )(qkv, qkv, qkv, c, ct)


def fox_bwd_dq(qkv, do, o, lse, c, ct, *, S, tq=256):
    T = qkv.shape[0]
    nq = S // tq

    def body(q_ref, k_ref, v_ref, do32_ref, o_ref, lse_ref, c_ref, ct_ref, dq_ref, dl_ref, do_ref, acc):
        qi = pl.program_id(1)
        ki = pl.program_id(2)

        @pl.when(ki == 0)
        def _():
            acc[...] = jnp.zeros_like(acc)
            dl_ref[...] = jnp.zeros_like(dl_ref)
            do_ref[...] = do32_ref[...].astype(MXU)
            for h in range(FOX_H):
                hs = slice(h * FOX_DH, (h + 1) * FOX_DH)
                dl_ref[:, h:h + 1] = jnp.sum(do_ref[:, hs].astype(F32) * o_ref[:, hs], axis=-1, keepdims=True)

        def step(diag):
            for h in range(FOX_H):
                hs = slice(h * FOX_DH, (h + 1) * FOX_DH)
                s = _fox_logits(q_ref, k_ref, c_ref, ct_ref, h, diag, tq)
                p = jnp.exp(s - lse_ref[:, h:h + 1])
                dp = _dot_nt(do_ref[:, hs], v_ref[:, hs])
                hi, lo = _split(p * (dp - dl_ref[:, h:h + 1]))
                acc[:, hs] += _dot(hi, k_ref[:, hs]) + _dot(lo, k_ref[:, hs])

        @pl.when(ki < qi)
        def _():
            step(False)

        @pl.when(ki == qi)
        def _():
            step(True)
            dq_ref[...] = (acc[...] * (FOX_DH ** -0.5)).astype(MXU)

    qrow, krow, qvec, kvec, ctspec = _fox_specs(tq, nq, lambda i, j: i, lambda i, j: jnp.minimum(i, j))
    orow = pl.BlockSpec((tq, D), lambda b, i, j: (b * nq + i, 0))
    return _call(
        body, name="fox_bwd_dq", grid=(T // S, nq, nq),
        in_specs=[qrow(0), krow(1), krow(2), orow, orow, qvec, qvec, ctspec],
        out_specs=[orow, qvec, orow],
        out_shape=[_sds((T, D), MXU), _sds((T, 128), F32), _sds((T, D), MXU)],
        scratch=[pltpu.VMEM((tq, D), F32)],
    )(qkv, qkv, qkv, do, o, lse, c, ct)


def fox_bwd_dkv(qkv, do, lse, delta, c, ct, *, S, tq=256):
    T = qkv.shape[0]
    nq = S // tq

    def body(q_ref, k_ref, v_ref, do_ref, lse_ref, dl_ref, c_ref, ct_ref, dk_ref, dv_ref, dck_ref, dk_acc, dv_acc):
        ki = pl.program_id(1)
        qi = pl.program_id(2)

        @pl.when(qi == 0)
        def _():
            dk_acc[...] = jnp.zeros_like(dk_acc)
            dv_acc[...] = jnp.zeros_like(dv_acc)
            dck_ref[...] = jnp.zeros_like(dck_ref)

        def step(diag):
            ones = jnp.ones((tq, 128), MXU)
            for h in range(FOX_H):
                hs = slice(h * FOX_DH, (h + 1) * FOX_DH)
                s = _fox_logits(q_ref, k_ref, c_ref, ct_ref, h, diag, tq)
                p = jnp.exp(s - lse_ref[:, h:h + 1])
                doh = do_ref[:, hs]
                dv_acc[:, hs] += _dot_tn(p.astype(MXU), doh)
                dp = _dot_nt(doh, v_ref[:, hs])
                hi, lo = _split(p * (dp - dl_ref[:, h:h + 1]))
                dk_acc[:, hs] += _dot_tn(hi, q_ref[:, hs]) + _dot_tn(lo, q_ref[:, hs])
                dck_ref[:, h:h + 1] += (_dot_tn(hi, ones) + _dot_tn(lo, ones))[:, 0:1]

        @pl.when(qi > ki)
        def _():
            step(False)

        @pl.when(qi == ki)
        def _():
            step(True)

        @pl.when(qi == nq - 1)
        def _():
            dk_ref[...] = (dk_acc[...] * (FOX_DH ** -0.5)).astype(MXU)
            dv_ref[...] = dv_acc[...].astype(MXU)

    qrow, krow, qvec, kvec, ctspec = _fox_specs(tq, nq, lambda i, j: jnp.maximum(i, j), lambda i, j: i)
    qo = pl.BlockSpec((tq, D), lambda b, i, j: (b * nq + jnp.maximum(i, j), 0))
    ko = pl.BlockSpec((tq, D), lambda b, i, j: (b * nq + i, 0))
    return _call(
        body, name="fox_bwd_dkv", grid=(T // S, nq, nq),
        in_specs=[qrow(0), krow(1), krow(2), qo, qvec, qvec, qvec, ctspec],
        out_specs=[ko, ko, kvec],
        out_shape=[_sds((T, D), MXU), _sds((T, D), MXU), _sds((T, 128), F32)],
        scratch=[pltpu.VMEM((tq, D), F32), pltpu.VMEM((tq, D), F32)],
    )(qkv, qkv, qkv, do, lse, delta, c, ct)


HW = 128
FOX_T = 512
COL_ONE = FOX_DH + 3
COL_LSE = FOX_DH + 6


def _parts(x):
    hi = x.astype(MXU).astype(F32)
    mid = (x - hi).astype(MXU).astype(F32)
    lo = (x - hi - mid).astype(MXU).astype(F32)
    return [hi, mid, lo]


def _aug(n, cols):
    lane = lax.broadcasted_iota(jnp.int32, (n, HW - FOX_DH), 1)
    out = jnp.zeros((n, HW - FOX_DH), F32)
    for i, cval in enumerate(cols):
        out = jnp.where(lane == i, cval, out)
    return out


def fox_prep(p, c, *, tt=256):
    T = p.shape[0]

    def body(q_ref, k_ref, v_ref, c_ref, qa_ref, ka_ref, va_ref):
        ones = [1.0, 1.0, 1.0]
        for h in range(FOX_H):
            hs = slice(h * FOX_DH, (h + 1) * FOX_DH)
            lo, mid = h * HW, h * HW + FOX_DH
            cp = _parts(c_ref[:, h:h + 1])
            qa_ref[:, lo:mid] = (q_ref[:, hs].astype(F32) * (FOX_DH ** -0.5)).astype(MXU)
            qa_ref[:, mid:lo + HW] = _aug(tt, cp + ones).astype(MXU)
            ka_ref[:, lo:mid] = k_ref[:, hs]
            ka_ref[:, mid:lo + HW] = _aug(tt, ones + [-x for x in cp] + ones).astype(MXU)
            va_ref[:, lo:mid] = v_ref[:, hs]
            va_ref[:, mid:lo + HW] = _aug(tt, ones).astype(MXU)

    wide = pl.BlockSpec((tt, FOX_H * HW), lambda i: (i, 0))
    col = lambda k: pl.BlockSpec((tt, D), lambda i: (i, k))
    return _call(body, name="fox_prep", grid=(T // tt,),
                 in_specs=[col(0), col(1), col(2), pl.BlockSpec((tt, 128), lambda i: (i, 0))],
                 out_specs=[wide, wide, wide], out_shape=[_sds((T, FOX_H * HW), MXU)] * 3)(p, p, p, c)


def _causal(x, fill):
    r = lax.broadcasted_iota(jnp.int32, x.shape, 0)
    c = lax.broadcasted_iota(jnp.int32, x.shape, 1)
    return jnp.where(r >= c, x, fill)


def _wide_specs(tq, nq, q_of, k_of):
    qs = pl.BlockSpec((tq, FOX_H * HW), lambda b, i, j: (b * nq + q_of(i, j), 0))
    ks = pl.BlockSpec((tq, FOX_H * HW), lambda b, i, j: (b * nq + k_of(i, j), 0))
    return qs, ks


def fox2_fwd(qa, ka, va, *, S, tq=256, tk=512):
    T = qa.shape[0]
    tk = min(tk, S)
    nq = S // tq
    nk = S // tk
    r = tk // tq

    def body(q_ref, k_ref, v_ref, o_ref, o32_ref, lse_ref, m_sc, acc, acc_lo):
        qi = pl.program_id(1)
        ki = pl.program_id(2)
        last = qi // r

        @pl.when(ki == 0)
        def _():
            m_sc[...] = jnp.full_like(m_sc, NEG)
            acc[...] = jnp.zeros_like(acc)
            acc_lo[...] = jnp.zeros_like(acc_lo)

        def step(diag):
            for h in range(FOX_H):
                ws = slice(h * HW, (h + 1) * HW)
                s = _dot_nt(q_ref[:, ws], k_ref[:, ws])
                if diag:
                    row = qi * tq + lax.broadcasted_iota(jnp.int32, s.shape, 0)
                    col = ki * tk + lax.broadcasted_iota(jnp.int32, s.shape, 1)
                    s = jnp.where(row >= col, s, NEG)
                m_prev = m_sc[h]
                m_new = jnp.maximum(m_prev, jnp.max(s, axis=-1, keepdims=True))
                alpha = jnp.exp(m_prev - m_new)
                hi, lo = _split(jnp.exp(s - jnp.tile(m_new, (1, tk // 128))))
                acc[h] = alpha * acc[h] + _dot(hi, v_ref[:, ws])
                acc_lo[h] = alpha * acc_lo[h] + _dot(lo, v_ref[:, ws])
                m_sc[h] = m_new

        @pl.when(ki < last)
        def _():
            step(False)

        @pl.when(ki == last)
        def _():
            step(True)
            for h in range(FOX_H):
                hs = slice(h * FOX_DH, (h + 1) * FOX_DH)
                full = acc[h] + acc_lo[h]
                l = full[:, FOX_DH:FOX_DH + 1]
                o_ref[:, hs] = (acc[h][:, :FOX_DH] / l).astype(MXU)
                o32_ref[:, hs] = full[:, :FOX_DH] / l
                lse_ref[:, h:h + 1] = m_sc[h][:, 0:1] + jnp.log(l)

    qs = pl.BlockSpec((tq, FOX_H * HW), lambda b, i, j: (b * nq + i, 0))
    ks = pl.BlockSpec((tk, FOX_H * HW), lambda b, i, j: (b * nk + jnp.minimum(j, i // r), 0))
    orow = pl.BlockSpec((tq, D), lambda b, i, j: (b * nq + i, 0))
    return _call(
        body, name="fox_fwd", grid=(T // S, nq, nk),
        in_specs=[qs, ks, ks],
        out_specs=[orow, orow, pl.BlockSpec((tq, 128), lambda b, i, j: (b * nq + i, 0))],
        out_shape=[_sds((T, D), MXU), _sds((T, D), F32), _sds((T, 128), F32)],
        scratch=[pltpu.VMEM((FOX_H, tq, 128), F32), pltpu.VMEM((FOX_H, tq, HW), F32),
                 pltpu.VMEM((FOX_H, tq, HW), F32)],
    )(qa, ka, va)


def fox2_prep_bwd(do, o32, lse, qa, *, tt=256):
    T = do.shape[0]

    def body(do_ref, o_ref, lse_ref, qa_ref, qb_ref, da_ref):
        lane = lax.broadcasted_iota(jnp.int32, (tt, HW), 1)
        dob = do_ref[...].astype(MXU)
        for h in range(FOX_H):
            hs = slice(h * FOX_DH, (h + 1) * FOX_DH)
            ws = slice(h * HW, (h + 1) * HW)
            doh = dob[:, hs]
            delta = jnp.sum(doh.astype(F32) * o_ref[:, hs], axis=-1, keepdims=True)
            da_ref[:, h * HW:h * HW + FOX_DH] = doh
            da_ref[:, h * HW + FOX_DH:(h + 1) * HW] = _aug(tt, [-x for x in _parts(delta)]).astype(MXU)
            tile = qa_ref[:, ws]
            for i, part in enumerate(_parts(lse_ref[:, h:h + 1])):
                tile = jnp.where(lane == COL_LSE + i, (-part).astype(MXU), tile)
            qb_ref[:, ws] = tile

    wide = pl.BlockSpec((tt, FOX_H * HW), lambda i: (i, 0))
    blk = pl.BlockSpec((tt, D), lambda i: (i, 0))
    return _call(body, name="fox_prep_bwd", grid=(T // tt,),
                 in_specs=[blk, blk, pl.BlockSpec((tt, 128), lambda i: (i, 0)), wide],
                 out_specs=[wide, wide],
                 out_shape=[_sds((T, FOX_H * HW), MXU), _sds((T, FOX_H * HW), MXU)],
                 )(do, o32, lse, qa)


def fox2_dq(qb, ka, va, da, *, S, tq=256):
    T = qb.shape[0]
    nq = S // tq

    def body(q_ref, k_ref, v_ref, d_ref, dq_ref, acc):
        qi = pl.program_id(1)
        ki = pl.program_id(2)

        @pl.when(ki == 0)
        def _():
            acc[...] = jnp.zeros_like(acc)

        def step(diag):
            for h in range(FOX_H):
                ws = slice(h * HW, (h + 1) * HW)
                kh = k_ref[:, ws]
                p = jnp.exp(_dot_nt(q_ref[:, ws], kh))
                if diag:
                    p = _causal(p, 0.0)
                hi, lo = _split(p * _dot_nt(d_ref[:, ws], v_ref[:, ws]))
                acc[h] += _dot(hi, kh) + _dot(lo, kh)

        @pl.when(ki < qi)
        def _():
            step(False)

        @pl.when(ki == qi)
        def _():
            step(True)
            for h in range(FOX_H):
                dq_ref[:, h * FOX_DH:(h + 1) * FOX_DH] = (acc[h][:, :FOX_DH] * (FOX_DH ** -0.5)).astype(MXU)

    qs, ks = _wide_specs(tq, nq, lambda i, j: i, lambda i, j: jnp.minimum(i, j))
    return _call(
        body, name="fox_bwd_dq", grid=(T // S, nq, nq),
        in_specs=[qs, ks, ks, qs],
        out_specs=pl.BlockSpec((tq, D), lambda b, i, j: (b * nq + i, 0)),
        out_shape=_sds((T, D), MXU),
        scratch=[pltpu.VMEM((FOX_H, tq, HW), F32)],
    )(qb, ka, va, da)


def fox2_dkv(qb, ka, va, da, *, S, tq=256):
    T = qb.shape[0]
    nq = S // tq

    def body(q_ref, k_ref, v_ref, d_ref, dk_ref, dv_ref, dck_ref, dk_acc, dv_acc):
        ki = pl.program_id(1)
        qi = pl.program_id(2)

        @pl.when(qi == 0)
        def _():
            dk_acc[...] = jnp.zeros_like(dk_acc)
            dv_acc[...] = jnp.zeros_like(dv_acc)

        def step(diag):
            for h in range(FOX_H):
                ws = slice(h * HW, (h + 1) * HW)
                qh = q_ref[:, ws]
                dh = d_ref[:, ws]
                pt = jnp.exp(_dot_nt(k_ref[:, ws], qh))
                if diag:
                    r = lax.broadcasted_iota(jnp.int32, pt.shape, 0)
                    c = lax.broadcasted_iota(jnp.int32, pt.shape, 1)
                    pt = jnp.where(r <= c, pt, 0.0)
                dv_acc[h] += _dot(pt.astype(MXU), dh)
                hi, lo = _split(pt * _dot_nt(v_ref[:, ws], dh))
                dk_acc[h] += _dot(hi, qh) + _dot(lo, qh)

        @pl.when(qi > ki)
        def _():
            step(False)

        @pl.when(qi == ki)
        def _():
            step(True)

        @pl.when(qi == nq - 1)
        def _():
            dck_ref[...] = jnp.zeros_like(dck_ref)
            for h in range(FOX_H):
                hs = slice(h * FOX_DH, (h + 1) * FOX_DH)
                dk_ref[:, hs] = dk_acc[h][:, :FOX_DH].astype(MXU)
                dv_ref[:, hs] = dv_acc[h][:, :FOX_DH].astype(MXU)
                dck_ref[:, h:h + 1] = dk_acc[h][:, COL_ONE:COL_ONE + 1]

    qs, ks = _wide_specs(tq, nq, lambda i, j: jnp.maximum(i, j), lambda i, j: i)
    ko = pl.BlockSpec((tq, D), lambda b, i, j: (b * nq + i, 0))
    return _call(
        body, name="fox_bwd_dkv", grid=(T // S, nq, nq),
        in_specs=[qs, ks, ks, qs],
        out_specs=[ko, ko, pl.BlockSpec((tq, 128), lambda b, i, j: (b * nq + i, 0))],
        out_shape=[_sds((T, D), MXU), _sds((T, D), MXU), _sds((T, 128), F32)],
        scratch=[pltpu.VMEM((FOX_H, tq, HW), F32), pltpu.VMEM((FOX_H, tq, HW), F32)],
    )(qb, ka, va, da)


def fox_fin(dck, fl, bf, *, S, tt=256):
    T = fl.shape[0]
    nb = S // tt
    nblk = T // tt

    def body(dck_ref, fl_ref, bf_ref, dfl_ref, dbf_ref, carry):
        i = pl.program_id(0)

        @pl.when(i == 0)
        def _():
            dbf_ref[...] = jnp.zeros_like(dbf_ref)

        @pl.when((i % nb) == 0)
        def _():
            carry[...] = jnp.zeros_like(carry)

        lane = lax.broadcasted_iota(jnp.int32, (tt, 128), 1)
        dc = jnp.where(lane < FOX_H, -dck_ref[...], 0.0)
        dlf = _dot_hi(_tri(tt, upper=True), dc) + carry[...]
        carry[...] = dlf[0:1, :]
        dfl = dlf * _sigmoid(-(fl_ref[...] + bf_ref[...]))
        dfl_ref[...] = dfl.astype(MXU)
        dbf_ref[...] += jnp.sum(dfl, axis=0, keepdims=True)

    rev = pl.BlockSpec((tt, 128), lambda i: (nblk - 1 - i, 0))
    row = pl.BlockSpec((1, 128), lambda i: (0, 0))
    return _call(
        body, name="fox_fin", grid=(nblk,),
        in_specs=[rev, rev, row],
        out_specs=[rev, row],
        out_shape=[_sds((T, 128), MXU), _sds((1, 128), F32)],
        scratch=[pltpu.VMEM((1, 128), F32)],
    )(dck, fl, bf)


def lb_fwd(logits):
    def body(l_ref, lb_ref):
        lv = l_ref[...]
        e = jnp.exp(lv - jnp.max(lv, axis=0, keepdims=True))
        p = e / jnp.sum(e, axis=0, keepdims=True)
        lb_ref[...] = p[1:2, :] + p[2:3, :]

    return _call(body, name="lb_fwd", grid=(1,),
                 in_specs=[pl.BlockSpec((DEPTH, D), lambda i: (0, 0))],
                 out_specs=pl.BlockSpec((1, D), lambda i: (0, 0)),
                 out_shape=_sds((1, D), F32))(logits)


def lb_bwd(logits, dlb):
    def body(l_ref, d_ref, o_ref):
        lv = l_ref[...]
        e = jnp.exp(lv - jnp.max(lv, axis=0, keepdims=True))
        p = e / jnp.sum(e, axis=0, keepdims=True)
        lb = p[1:2, :] + p[2:3, :]
        row = lax.broadcasted_iota(jnp.int32, (DEPTH, D), 0)
        sel = ((row == 1) | (row == 2)).astype(F32)
        o_ref[...] = p * (sel - lb) * d_ref[...]

    return _call(body, name="lb_bwd", grid=(1,),
                 in_specs=[pl.BlockSpec((DEPTH, D), lambda i: (0, 0)), pl.BlockSpec((1, D), lambda i: (0, 0))],
                 out_specs=pl.BlockSpec((DEPTH, D), lambda i: (0, 0)),
                 out_shape=_sds((DEPTH, D), F32))(logits, dlb)


def _hgrn_gates(qr, fr, lb):
    sg = _sigmoid(fr)
    sneg = _sigmoid(-fr)
    f = lb + (1.0 - lb) * sg
    kk = (1.0 - lb) * sneg
    G = _dot_hi(_tri(HG_C), jnp.log(f))
    eG = jnp.exp(G)
    einv = jnp.exp(-G)
    elast = jnp.exp(G[HG_C - 1:HG_C, :] - G)
    q = qr * _sigmoid(qr)
    return dict(q=q, kk=kk, f=f, sg=sg, sneg=sneg, eG=eG, einv=einv, elast=elast,
                qg=q * eG, kinv=kk * einv, khat=kk * elast, glast=jnp.exp(G[HG_C - 1:HG_C, :]))


def _tril_mask(x):
    r = lax.broadcasted_iota(jnp.int32, x.shape, 0)
    c = lax.broadcasted_iota(jnp.int32, x.shape, 1)
    return jnp.where(r >= c, x, 0.0)


def hgrn_fwd(p, lb, ng, *, S, R=256):
    T = p.shape[0]
    R = min(R, S)
    nr = S // R
    ncr = R // HG_C

    def body(q_ref, f_ref, v_ref, gt_ref, lb_ref, ng_ref, y_ref, o_ref, st_ref, st):
        @pl.when(pl.program_id(1) == 0)
        def _():
            st[...] = jnp.zeros_like(st)

        lbv = lb_ref[...]
        for ch in range(ncr):
            rows = slice(ch * HG_C, (ch + 1) * HG_C)
            gt = _hgrn_gates(q_ref[rows, :], f_ref[rows, :], lbv)
            for h in range(HG_H):
                hs = slice(h * HG_DK, (h + 1) * HG_DK)
                sp = st[h]
                st_ref[ch, h] = sp
                qg = gt["qg"][:, hs].astype(MXU)
                vh = v_ref[rows, hs].astype(MXU)
                A = _tril_mask(_dot_nt(qg, gt["kinv"][:, hs].astype(MXU)))
                o_ref[rows, hs] = _dot_nt(qg, sp.astype(MXU)) + _dot(A.astype(MXU), vh)
                st[h] = sp * gt["glast"][:, hs] + _dot_tn(vh, gt["khat"][:, hs].astype(MXU))
        gate = gt_ref[...]
        sgate = gate * _sigmoid(gate)
        for h in range(HG_H):
            hs = slice(h * HG_DK, (h + 1) * HG_DK)
            oh = o_ref[:, hs]
            r = lax.rsqrt(jnp.mean(oh * oh, axis=-1, keepdims=True) + RMS_EPS)
            y_ref[:, hs] = (oh * r * ng_ref[:, hs] * sgate[:, hs]).astype(MXU)

    col = lambda c: pl.BlockSpec((R, D), lambda b, i: (b * nr + i, c))
    row = pl.BlockSpec((1, D), lambda b, i: (0, 0))
    return _call(
        body, name="hgrn_fwd", grid=(T // S, nr),
        in_specs=[col(0), col(1), col(2), col(3), row, row],
        out_specs=[col(0), col(0),
                   pl.BlockSpec((ncr, HG_H, HG_DK, HG_DK), lambda b, i: (b * nr + i, 0, 0, 0))],
        out_shape=[_sds((T, D), MXU), _sds((T, D), F32), _sds((T // HG_C, HG_H, HG_DK, HG_DK), F32)],
        scratch=[pltpu.VMEM((HG_H, HG_DK, HG_DK), F32)],
    )(p, p, p, p, lb, ng)


def hgrn_bwd(p, o, dyo, states, lb, ng, *, S, R=256):
    T = p.shape[0]
    R = min(R, S)
    nr = S // R
    ncr = R // HG_C

    def body(q_ref, f_ref, v_ref, gt_ref, o_ref, dy_ref, st_ref, lb_ref, ng_ref,
             dp_ref, dlb_ref, dng_ref, dst, do_buf, dG_buf, dqb, dkb):
        b = pl.program_id(0)
        i = pl.program_id(1)

        @pl.when(i == 0)
        def _():
            dst[...] = jnp.zeros_like(dst)

        @pl.when((b == 0) & (i == 0))
        def _():
            dlb_ref[...] = jnp.zeros_like(dlb_ref)
            dng_ref[...] = jnp.zeros_like(dng_ref)

        lbv = lb_ref[...]
        gate = gt_ref[...]
        sg_gate = _sigmoid(gate)
        silu_gate = gate * sg_gate
        for h in range(HG_H):
            hs = slice(h * HG_DK, (h + 1) * HG_DK)
            oh = o_ref[:, hs]
            r = lax.rsqrt(jnp.mean(oh * oh, axis=-1, keepdims=True) + RMS_EPS)
            ohat = oh * r
            dyh = dy_ref[:, hs]
            ngh = ng_ref[:, hs]
            dng_ref[:, hs] += jnp.sum(dyh * silu_gate[:, hs] * ohat, axis=0, keepdims=True)
            dp_ref[:, 3 * D + h * HG_DK:3 * D + (h + 1) * HG_DK] = (
                dyh * ohat * ngh * (sg_gate[:, hs] * (1.0 + gate[:, hs] * (1.0 - sg_gate[:, hs])))).astype(MXU)
            dn = dyh * ngh * silu_gate[:, hs]
            do_buf[:, hs] = r * (dn - ohat * jnp.mean(dn * ohat, axis=-1, keepdims=True))

        lastrow = lax.broadcasted_iota(jnp.int32, (HG_C, HG_DK), 0) == HG_C - 1
        for ch in reversed(range(ncr)):
            rows = slice(ch * HG_C, (ch + 1) * HG_C)
            qr = q_ref[rows, :]
            gt = _hgrn_gates(qr, f_ref[rows, :], lbv)
            for h in range(HG_H):
                hs = slice(h * HG_DK, (h + 1) * HG_DK)
                sp = st_ref[ch, h]
                ds = dst[h]
                qg32, kinv32, khat32 = gt["qg"][:, hs], gt["kinv"][:, hs], gt["khat"][:, hs]
                qg, kinv, khat = qg32.astype(MXU), kinv32.astype(MXU), khat32.astype(MXU)
                vh = v_ref[rows, hs].astype(MXU)
                doh = do_buf[rows, hs].astype(MXU)
                dsb = ds.astype(MXU)
                A = _tril_mask(_dot_nt(qg, kinv)).astype(MXU)
                dA = _tril_mask(_dot_nt(doh, vh)).astype(MXU)
                dqg = _dot(doh, sp.astype(MXU)) + _dot(dA, kinv)
                dkinv = _dot_tn(dA, qg)
                dp_ref[rows, 2 * D + h * HG_DK:2 * D + (h + 1) * HG_DK] = (
                    _dot_tn(A, doh) + _dot_nt(khat, dsb)).astype(MXU)
                dkhat = _dot(vh, dsb)
                glast = gt["glast"][:, hs]
                qg32, kinv32, khat32 = qg.astype(F32), kinv.astype(F32), khat.astype(F32)
                extra = (glast * jnp.sum(dsb.astype(F32) * sp.astype(MXU).astype(F32), axis=0, keepdims=True)
                         + jnp.sum(dkhat * khat32, axis=0, keepdims=True))
                dst[h] = ds * glast + _dot_tn(doh, qg)
                dG = dqg * qg32 - dkinv * kinv32 - dkhat * khat32
                dG_buf[:, hs] = dG + jnp.where(lastrow, extra, 0.0)
                dqb[:, hs] = dqg * gt["eG"][:, hs]
                dkb[:, hs] = dkinv * gt["einv"][:, hs] + dkhat * gt["elast"][:, hs]
            dg = _dot_hi(_tri(HG_C, upper=True), dG_buf[...])
            dk = dkb[...]
            sneg, f = gt["sneg"], gt["f"]
            c1 = (1.0 - lbv) * gt["sg"] * sneg
            dp_ref[rows, D:2 * D] = (dg * c1 / f - dk * c1).astype(MXU)
            dlb_ref[...] += jnp.sum(dg * sneg / f - dk * sneg, axis=0, keepdims=True)
            sq = _sigmoid(qr)
            dp_ref[rows, 0:D] = (dqb[...] * (sq * (1.0 + qr * (1.0 - sq)))).astype(MXU)

    rev = lambda b, i: b * nr + nr - 1 - i
    col = lambda c: pl.BlockSpec((R, D), lambda b, i: (rev(b, i), c))
    row = pl.BlockSpec((1, D), lambda b, i: (0, 0))
    return _call(
        body, name="hgrn_bwd", grid=(T // S, nr),
        in_specs=[col(0), col(1), col(2), col(3), col(0), col(0),
                  pl.BlockSpec((ncr, HG_H, HG_DK, HG_DK), lambda b, i: (rev(b, i), 0, 0, 0)), row, row],
        out_specs=[pl.BlockSpec((R, 4 * D), lambda b, i: (rev(b, i), 0)), row, row],
        out_shape=[_sds((T, 4 * D), MXU), _sds((1, D), F32), _sds((1, D), F32)],
        scratch=[pltpu.VMEM((HG_H, HG_DK, HG_DK), F32), pltpu.VMEM((R, D), F32), pltpu.VMEM((HG_C, D), F32),
                 pltpu.VMEM((HG_C, D), F32), pltpu.VMEM((HG_C, D), F32)],
    )(p, p, p, p, o, dyo, states, lb, ng)


def _mm_tn_stack(a, b, *, name, G, M, N, tk, stack):
    E, e, buf = stack
    T = a.shape[-2]

    def spec(arr, width):
        if arr.ndim == 3:
            return pl.BlockSpec((None, tk, width), lambda g, k: (g, k, 0))
        return pl.BlockSpec((tk, width), lambda g, k: (k, 0))

    def body(*refs):
        a_ref, b_ref, o_ref = refs[0], refs[1], refs[-1]

        @pl.when(pl.program_id(1) == 0)
        def _():
            o_ref[...] = jnp.zeros_like(o_ref)

        o_ref[...] += _dot_tn(a_ref[...], b_ref[...])

    in_specs = [spec(a, M), spec(b, N)]
    args = [a, b]
    aliases = {}
    if buf is not None:
        in_specs.append(pl.BlockSpec(memory_space=pl.ANY))
        args.append(buf)
        aliases = {2: 0}
    return pl.pallas_call(
        body, name=name, grid=(G, T // tk), in_specs=in_specs,
        out_specs=pl.BlockSpec((None, None, M, N), lambda g, k: (g, e, 0, 0)),
        out_shape=_sds((G, E, M, N), F32), input_output_aliases=aliases,
        compiler_params=pltpu.CompilerParams(dimension_semantics=("arbitrary", "arbitrary"),
                                             vmem_limit_bytes=VMEM_LIMIT))(*args)


MESH = pl.DeviceIdType.MESH
ANY = pl.BlockSpec(memory_space=pl.ANY)


def _pos():
    return lax.axis_index("x"), lax.axis_index("y"), lax.axis_index("c")


def _other_chips(x, y):
    return [(1 - x, y), (x, 1 - y), (1 - x, 1 - y)]


def _comm_call(body, *, name, args, out_shape, n_sem):
    return pl.pallas_call(
        body, name=name, in_specs=[ANY] * len(args), out_specs=[ANY] * len(out_shape), out_shape=out_shape,
        scratch_shapes=[pltpu.SemaphoreType.DMA((n_sem,)), pltpu.SemaphoreType.DMA((n_sem,)),
                        pltpu.SemaphoreType.DMA((len(args),))],
    )(*args)


def all_gather_chips(xs):
    n = len(xs)

    def body(*refs):
        x_refs, o_refs = refs[:n], refs[n:2 * n]
        ssem, rsem, lsem = refs[2 * n:]
        x, y, c = _pos()
        me = 2 * x + y
        chips = _other_chips(x, y)
        sib = (x, y, 1 - c)

        def rc(src, dst, idx, dev):
            return pltpu.make_async_remote_copy(src_ref=src, dst_ref=dst, send_sem=ssem.at[idx], recv_sem=rsem.at[idx],
                                                device_id=dev, device_id_type=MESH)

        via = jnp.where(c == 0, 2 * (1 - x) + y, 2 * x + (1 - y))
        to = (jnp.where(c == 0, x, 1 - x), jnp.where(c == 0, 1 - y, y), c)
        local, started = [], []
        for t in range(n):
            hr = xs[t].shape[0] // 2
            mine = pl.ds(c * hr, hr)
            cp = pltpu.make_async_copy(x_refs[t], o_refs[t].at[me], lsem.at[t])
            cp.start()
            local.append(cp)
            for k, (cx, cy) in enumerate(chips[:2]):
                cp = rc(x_refs[t].at[mine], o_refs[t].at[me, mine], 6 * t + k, (cx, cy, c))
                cp.start()
                started.append(cp)
        for t in range(n):
            hr = xs[t].shape[0] // 2
            mine = pl.ds(c * hr, hr)
            for k, (cx, cy) in enumerate(chips[:2]):
                landed = o_refs[t].at[2 * cx + cy, mine]
                rc(landed, landed, 6 * t + k, (cx, cy, c)).wait_recv()
            passed = o_refs[t].at[via, mine]
            cp = rc(passed, passed, 6 * t + 2, to)
            cp.start()
            started.append(cp)
            for k, (cx, cy) in enumerate(chips[:2]):
                landed = o_refs[t].at[2 * cx + cy, mine]
                cp = rc(landed, landed, 6 * t + 3 + k, sib)
                cp.start()
                started.append(cp)
        for t in range(n):
            hr = xs[t].shape[0] // 2
            mine = pl.ds(c * hr, hr)
            cx, cy = chips[2]
            landed = o_refs[t].at[2 * cx + cy, mine]
            rc(landed, landed, 6 * t + 2, to).wait_recv()
            cp = rc(landed, landed, 6 * t + 5, sib)
            cp.start()
            started.append(cp)
        for t in range(n):
            hr = xs[t].shape[0] // 2
            theirs = pl.ds((1 - c) * hr, hr)
            for k, (cx, cy) in enumerate(chips):
                other = o_refs[t].at[2 * cx + cy, theirs]
                rc(other, other, 6 * t + 3 + k, sib).wait_recv()
        for cp in started:
            cp.wait_send()
        for cp in local:
            cp.wait()

    outs = _comm_call(body, name="all_gather_chips", args=list(xs),
                      out_shape=[_sds((NSH,) + a.shape, a.dtype) for a in xs], n_sem=6 * n)
    return list(outs)


def sibling_half_exchange(gs):
    n = len(gs)

    def body(*refs):
        g_refs, o_refs = refs[:n], refs[n:2 * n]
        ssem, rsem, _ = refs[2 * n:]
        x, y, c = _pos()
        cps = []
        for t in range(n):
            hr = gs[t].shape[1] // 2
            for j in range(NSH):
                cp = pltpu.make_async_remote_copy(
                    src_ref=g_refs[t].at[j, pl.ds((1 - c) * hr, hr)], dst_ref=o_refs[t].at[j],
                    send_sem=ssem.at[NSH * t + j], recv_sem=rsem.at[NSH * t + j],
                    device_id=(x, y, 1 - c), device_id_type=MESH)
                cp.start()
                cps.append(cp)
        for cp in cps:
            cp.wait()

    outs = _comm_call(body, name="sibling_half_exchange", args=list(gs),
                      out_shape=[_sds((NSH, g.shape[1] // 2, g.shape[2]), g.dtype) for g in gs], n_sem=NSH * n)
    return list(outs)


def chip_scatter(ss):
    n = len(ss)

    def body(*refs):
        s_refs, o_refs = refs[:n], refs[n:2 * n]
        ssem, rsem, _ = refs[2 * n:]
        x, y, c = _pos()
        cps = []
        for t in range(n):
            for k, (cx, cy) in enumerate(_other_chips(x, y)):
                cp = pltpu.make_async_remote_copy(
                    src_ref=s_refs[t].at[2 * cx + cy], dst_ref=o_refs[t].at[k],
                    send_sem=ssem.at[3 * t + k], recv_sem=rsem.at[3 * t + k],
                    device_id=(cx, cy, c), device_id_type=MESH)
                cp.start()
                cps.append(cp)
        for cp in cps:
            cp.wait()

    outs = _comm_call(body, name="chip_scatter", args=list(ss),
                      out_shape=[_sds((3,) + s.shape[1:], s.dtype) for s in ss], n_sem=3 * n)
    return list(outs)


def sibling_exchange(rs):
    n = len(rs)

    def body(*refs):
        r_refs, o_refs = refs[:n], refs[n:2 * n]
        ssem, rsem, _ = refs[2 * n:]
        x, y, c = _pos()
        cps = []
        for t in range(n):
            cp = pltpu.make_async_remote_copy(
                src_ref=r_refs[t], dst_ref=o_refs[t], send_sem=ssem.at[t], recv_sem=rsem.at[t],
                device_id=(x, y, 1 - c), device_id_type=MESH)
            cp.start()
            cps.append(cp)
        for cp in cps:
            cp.wait()

    outs = _comm_call(body, name="sibling_exchange", args=list(rs),
                      out_shape=[_sds(r.shape, r.dtype) for r in rs], n_sem=n)
    return list(outs)


def all_gather_devices(v):
    def body(v_ref, o_ref, ssem, rsem, lsem):
        x, y, c = _pos()
        me = 4 * x + 2 * y + c
        loc = pltpu.make_async_copy(v_ref, o_ref.at[me], lsem.at[0])
        loc.start()
        cps = []
        k = 0
        for fx in range(2):
            for fy in range(2):
                for fc in range(2):
                    if fx == fy == fc == 0:
                        continue
                    cp = pltpu.make_async_remote_copy(
                        src_ref=v_ref, dst_ref=o_ref.at[me], send_sem=ssem.at[k], recv_sem=rsem.at[k],
                        device_id=(x ^ fx, y ^ fy, c ^ fc), device_id_type=MESH)
                    cp.start()
                    src = 4 * (x ^ fx) + 2 * (y ^ fy) + (c ^ fc)
                    cps.append((cp, o_ref.at[src], k))
                    k += 1
        for cp, landed, k in cps:
            cp.wait_send()
            pltpu.make_async_remote_copy(
                src_ref=landed, dst_ref=landed, send_sem=ssem.at[k], recv_sem=rsem.at[k],
                device_id=(x, y, c), device_id_type=MESH).wait_recv()
        loc.wait()

    return _comm_call(body, name="all_gather_devices", args=[v],
                      out_shape=[_sds((8,) + v.shape, v.dtype)], n_sem=7)[0]


def _call_sp(body, *, name, grid, in_specs, out_specs, out_shape, pos, args):
    return pl.pallas_call(
        body, name=name,
        grid_spec=pltpu.PrefetchScalarGridSpec(num_scalar_prefetch=1, grid=grid, in_specs=in_specs,
                                               out_specs=out_specs),
        out_shape=out_shape,
        compiler_params=pltpu.CompilerParams(dimension_semantics=("arbitrary",) * len(grid),
                                             vmem_limit_bytes=VMEM_LIMIT))(pos, *args)


def _rows_tile(r):
    for t in (512, 256, 128, 64, 32, 16, 8):
        if r % t == 0:
            return t
    raise ValueError(r)


def pair_sum(g, r, pos):
    _, R, C = g.shape
    hr = R // 2
    tr = _rows_tile(hr)
    nbh = hr // tr

    def body(p_ref, g_ref, r_ref, o_ref):
        o_ref[...] = (g_ref[...] + r_ref[...]).astype(MXU)

    return _call_sp(
        body, name="pair_sum", grid=(NSH, nbh), pos=pos, args=[g, r],
        in_specs=[pl.BlockSpec((None, tr, C), lambda j, i, p: (j, p[0] * nbh + i, 0)),
                  pl.BlockSpec((None, tr, C), lambda j, i, p: (j, i, 0))],
        out_specs=pl.BlockSpec((None, tr, C), lambda j, i, p: (j, i, 0)),
        out_shape=_sds((NSH, hr, C), MXU))


def reduce_own(g, r_sib, r_ici, pos):
    _, R, C = g.shape
    hr = R // 2
    tr = _rows_tile(hr)
    nbh = hr // tr

    def body(p_ref, g_ref, rs_ref, ri_ref, o_ref):
        s = g_ref[...] + rs_ref[...]
        for k in range(3):
            s = s + ri_ref[k].astype(F32)
        o_ref[...] = s

    return _call_sp(
        body, name="reduce_own", grid=(nbh,), pos=pos, args=[g, r_sib, r_ici],
        in_specs=[pl.BlockSpec((None, tr, C), lambda i, p: (p[1], p[0] * nbh + i, 0)),
                  pl.BlockSpec((None, tr, C), lambda i, p: (p[1], i, 0)),
                  pl.BlockSpec((3, tr, C), lambda i, p: (0, i, 0))],
        out_specs=pl.BlockSpec((tr, C), lambda i, p: (i, 0)),
        out_shape=_sds((hr, C), F32))


def _adamw_math(w, g, m, v):
    m = ADAM_B1 * m + (1.0 - ADAM_B1) * g
    v = ADAM_B2 * v + (1.0 - ADAM_B2) * (g * g)
    m_hat = m / (1.0 - ADAM_B1 ** ADAM_STEP)
    v_hat = v / (1.0 - ADAM_B2 ** ADAM_STEP)
    delta = -ADAM_LR * (m_hat / (jnp.sqrt(v_hat) + ADAM_EPS) + ADAM_WD * w)
    return delta, m, v


def adamw_halves(w, m, v, ga, gb, pos):
    R, C = w.shape
    hr = R // 2
    tr = _rows_tile(hr)
    nbh = hr // tr

    def body(p_ref, w_ref, m_ref, v_ref, ga_ref, gb_ref, g_ref, d_ref, mo_ref, vo_ref):
        mine = (pl.program_id(0) // nbh) == p_ref[0]
        g = jnp.where(mine, ga_ref[...], gb_ref[...])
        g_ref[...] = g
        d_ref[...], mo_ref[...], vo_ref[...] = _adamw_math(w_ref[...], g, m_ref[...], v_ref[...])

    blk = pl.BlockSpec((tr, C), lambda i, p: (i, 0))
    return _call_sp(
        body, name="adamw_halves", grid=(R // tr,), pos=pos, args=[w, m, v, ga, gb],
        in_specs=[blk, blk, blk,
                  pl.BlockSpec((tr, C), lambda i, p: (jnp.where(i // nbh == p[0], i % nbh, 0), 0)),
                  pl.BlockSpec((tr, C), lambda i, p: (jnp.where(i // nbh == p[0], 0, i % nbh), 0))],
        out_specs=[blk, blk, blk, blk],
        out_shape=[_sds((R, C), F32)] * 4)


def adamw_sum(gall, w, m, v):
    n, R, C = gall.shape

    def body(ga_ref, w_ref, m_ref, v_ref, g_ref, d_ref, mo_ref, vo_ref):
        g = ga_ref[0]
        for k in range(1, n):
            g = g + ga_ref[k]
        g_ref[...] = g
        d_ref[...], mo_ref[...], vo_ref[...] = _adamw_math(w_ref[...], g, m_ref[...], v_ref[...])

    blk = pl.BlockSpec((R, C), lambda i: (0, 0))
    return _call(body, name="adamw_sum", grid=(1,),
                 in_specs=[pl.BlockSpec((n, R, C), lambda i: (0, 0, 0)), blk, blk, blk],
                 out_specs=[blk, blk, blk, blk], out_shape=[_sds((R, C), F32)] * 4)(gall, w, m, v)


_WEIGHTS = ['ffn_norm', 'ffn_w_gate', 'ffn_w_up', 'ffn_w_down', 'mix_norm', 'final_norm', 'conv_w_in', 'conv_b_in',
            'conv_dw', 'conv_dw_b', 'conv_ln_g', 'conv_ln_b', 'conv_w_out', 'fox_w_in', 'fox_b_f', 'fox_w_out',
            'hgrn_w_in', 'hgrn_lb_logits', 'hgrn_norm', 'hgrn_w_out', 'pool_w', 'pool_scale']
_BIG = ['ffn_w_gate', 'ffn_w_up', 'ffn_w_down', 'conv_w_in', 'conv_w_out', 'fox_w_in', 'fox_w_out',
        'hgrn_w_in', 'hgrn_w_out', 'pool_w']
_SHARDED_SMALL = ['ffn_norm', 'conv_dw', 'hgrn_norm', 'pool_scale']
_REPLICATED = ['mix_norm', 'final_norm', 'conv_b_in', 'conv_dw_b', 'conv_ln_g', 'conv_ln_b', 'fox_b_f', 'hgrn_lb_logits']
FOX_N = 3 * D + FOX_H
FOX_NP = 3200
QS = D // NSH


def _pad_rows(a, rows):
    return jnp.pad(a, ((0, rows - a.shape[0]), (0, 0)))


def _pack_sharded_small(get):
    return jnp.concatenate([get('ffn_norm').reshape(8, -1), _pad_rows(get('conv_dw')[0], 32),
                            get('hgrn_norm'), get('pool_scale'), jnp.zeros((6, get('pool_scale').shape[1]), F32)], axis=0)


def _pack_replicated(get):
    return jnp.concatenate([get('mix_norm'), get('final_norm').reshape(1, D), get('conv_b_in').reshape(2, D),
                            get('conv_dw_b'), get('conv_ln_g'), get('conv_ln_b'),
                            jnp.pad(get('fox_b_f'), ((0, 0), (0, D - FOX_H))), get('hgrn_lb_logits'),
                            jnp.zeros((9, D), F32)], axis=0)


def _unpack_replicated(p):
    return {'mix_norm': p[0:4], 'final_norm': p[4], 'conv_b_in': p[5:7].reshape(1, 2 * D), 'conv_dw_b': p[7:8],
            'conv_ln_g': p[8:9], 'conv_ln_b': p[9:10], 'fox_b_f': p[10:11, :FOX_H], 'hgrn_lb_logits': p[11:15]}


def _unpack_sharded_small(p):
    return {'ffn_norm': p[0:8].reshape(DEPTH, 2, -1), 'conv_dw': p[8:8 + CONV_W][None],
            'hgrn_norm': p[40:41], 'pool_scale': p[41:42]}


def kernel(x, ffn_norm, ffn_w_gate, ffn_w_up, ffn_w_down, mix_norm, final_norm, conv_w_in, conv_b_in, conv_dw, conv_dw_b, conv_ln_g, conv_ln_b, conv_w_out, fox_w_in, fox_b_f, fox_w_out, hgrn_w_in, hgrn_lb_logits, hgrn_norm, hgrn_w_out, pool_w, pool_scale, loss_target, m_ffn_norm, m_ffn_w_gate, m_ffn_w_up, m_ffn_w_down, m_mix_norm, m_final_norm, m_conv_w_in, m_conv_b_in, m_conv_dw, m_conv_dw_b, m_conv_ln_g, m_conv_ln_b, m_conv_w_out, m_fox_w_in, m_fox_b_f, m_fox_w_out, m_hgrn_w_in, m_hgrn_lb_logits, m_hgrn_norm, m_hgrn_w_out, m_pool_w, m_pool_scale, v_ffn_norm, v_ffn_w_gate, v_ffn_w_up, v_ffn_w_down, v_mix_norm, v_final_norm, v_conv_w_in, v_conv_b_in, v_conv_dw, v_conv_dw_b, v_conv_ln_g, v_conv_ln_b, v_conv_w_out, v_fox_w_in, v_fox_b_f, v_fox_w_out, v_hgrn_w_in, v_hgrn_lb_logits, v_hgrn_norm, v_hgrn_w_out, v_pool_w, v_pool_scale):
    W = dict(ffn_norm=ffn_norm, ffn_w_gate=ffn_w_gate, ffn_w_up=ffn_w_up, ffn_w_down=ffn_w_down, mix_norm=mix_norm, final_norm=final_norm, conv_w_in=conv_w_in, conv_b_in=conv_b_in, conv_dw=conv_dw, conv_dw_b=conv_dw_b, conv_ln_g=conv_ln_g, conv_ln_b=conv_ln_b, conv_w_out=conv_w_out, fox_w_in=fox_w_in, fox_b_f=fox_b_f, fox_w_out=fox_w_out, hgrn_w_in=hgrn_w_in, hgrn_lb_logits=hgrn_lb_logits, hgrn_norm=hgrn_norm, hgrn_w_out=hgrn_w_out, pool_w=pool_w, pool_scale=pool_scale)
    M = dict(ffn_norm=m_ffn_norm, ffn_w_gate=m_ffn_w_gate, ffn_w_up=m_ffn_w_up, ffn_w_down=m_ffn_w_down, mix_norm=m_mix_norm, final_norm=m_final_norm, conv_w_in=m_conv_w_in, conv_b_in=m_conv_b_in, conv_dw=m_conv_dw, conv_dw_b=m_conv_dw_b, conv_ln_g=m_conv_ln_g, conv_ln_b=m_conv_ln_b, conv_w_out=m_conv_w_out, fox_w_in=m_fox_w_in, fox_b_f=m_fox_b_f, fox_w_out=m_fox_w_out, hgrn_w_in=m_hgrn_w_in, hgrn_lb_logits=m_hgrn_lb_logits, hgrn_norm=m_hgrn_norm, hgrn_w_out=m_hgrn_w_out, pool_w=m_pool_w, pool_scale=m_pool_scale)
    V = dict(ffn_norm=v_ffn_norm, ffn_w_gate=v_ffn_w_gate, ffn_w_up=v_ffn_w_up, ffn_w_down=v_ffn_w_down, mix_norm=v_mix_norm, final_norm=v_final_norm, conv_w_in=v_conv_w_in, conv_b_in=v_conv_b_in, conv_dw=v_conv_dw, conv_dw_b=v_conv_dw_b, conv_ln_g=v_conv_ln_g, conv_ln_b=v_conv_ln_b, conv_w_out=v_conv_w_out, fox_w_in=v_fox_w_in, fox_b_f=v_fox_b_f, fox_w_out=v_fox_w_out, hgrn_w_in=v_hgrn_w_in, hgrn_lb_logits=v_hgrn_lb_logits, hgrn_norm=v_hgrn_norm, hgrn_w_out=v_hgrn_w_out, pool_w=v_pool_w, pool_scale=v_pool_scale)

    px, py, pc = _pos()
    jme = 2 * px + py
    pos = jnp.stack([pc, jme]).astype(jnp.int32)
    S = x.shape[1]
    T = x.shape[0] * S
    x2 = x.reshape(T, D)
    tgt = loss_target.reshape(T, D)

    flat = lambda a: a.reshape(-1, a.shape[-1])
    gathered = all_gather_chips([flat(W[n]).astype(MXU) for n in _BIG] + [_pack_sharded_small(W.get)])
    G = dict(zip(_BIG, gathered[:-1]))
    small = gathered[-1].transpose(1, 0, 2).reshape(48, D)
    ffn_norm_f, conv_dw_f = small[0:8], small[8:40]
    hgrn_norm_f, pool_scale_f = small[40:41], small[41:42]
    wg_all = G['ffn_w_gate'].reshape(NSH, 2 * DEPTH, D, FS)
    wu_all = G['ffn_w_up'].reshape(NSH, 2 * DEPTH, D, FS)
    wd_all = G['ffn_w_down'].reshape(NSH, 2 * DEPTH, FS, D)
    conv_wi = G['conv_w_in']
    conv_wo = G['conv_w_out'].reshape(D, D)
    fox_full = jnp.pad(G['fox_w_in'].transpose(1, 0, 2).reshape(D, FOX_N), ((0, 0), (0, FOX_NP - FOX_N)))
    fox_w5 = fox_full.reshape(D, 5, FOX_NP // 5).transpose(1, 0, 2)
    fox_wf = fox_full[:, 3 * D:][None]
    fox_bf = jnp.pad(fox_b_f, ((0, 0), (0, 128 - FOX_H)))
    fox_wo = G['fox_w_out'].reshape(D, D)
    hgrn_wi = G['hgrn_w_in']
    hgrn_wo = G['hgrn_w_out'].reshape(D, D)
    pool_wf = G['pool_w'].reshape(NSH, 4, 64, POOL_G).transpose(1, 0, 2, 3).reshape(4, POOL_G, POOL_G)
    conv_bi = conv_b_in.reshape(NSH, 1, 2 * D // NSH)

    def ffn_f(xs, e):
        xo, h, a, b = ffn_fwd(xs, ffn_norm_f[e:e + 1], wg_all, wu_all, wd_all, e, tm=min(1024, xs.shape[0]))
        return xo, (xs, h, a, b)

    saved = []
    xs = x2
    lb = lb_fwd(hgrn_lb_logits)
    for i in range(DEPTH):
        xs, r0 = ffn_f(xs, 2 * i)
        gm = mix_norm[i:i + 1]
        xin = xs
        if i == 0:
            p, h = norm_mm(xin, gm, conv_wi, conv_bi, name="conv_in", out_dtype=F32)
            u2, u4 = conv_fwd_core(p, conv_dw_f, conv_dw_b, conv_ln_g, conv_ln_b, S=S)
            xs = mm_res(u4, conv_wo, xin, name="conv_out")
            rm = (xin, p, h, u2, u4)
        elif i == 1:
            p, h = norm_mm(xin, gm, fox_w5, None, name="fox_in", out_dtype=MXU)
            fl, _ = norm_mm(xin, gm, fox_wf, None, name="fox_in_f", out_dtype=F32)
            cq, _ = fox_cum(fl, fox_bf, S=S)
            qa, ka, va = fox_prep(p, cq)
            o, o32, lse = fox2_fwd(qa, ka, va, S=S, tq=min(FOX_T, S), tk=min(FOX_T, S))
            xs = mm_res(o, fox_wo, xin, name="fox_out")
            rm = (xin, h, fl, qa, ka, va, o, o32, lse)
        elif i == 2:
            p, h = norm_mm(xin, gm, hgrn_wi, None, name="hgrn_in", out_dtype=F32)
            yh, oh, st = hgrn_fwd(p, lb, hgrn_norm_f, S=S)
            xs = mm_res(yh, hgrn_wo, xin, name="hgrn_out")
            rm = (xin, p, h, yh, oh, st)
        else:
            xs, mp = pool_fwd(xin, gm, pool_wf, pool_scale_f, S=S)
            rm = (xin, mp)
        xs, r1 = ffn_f(xs, 2 * i + 1)
        saved.append((r0, rm, r1))

    loss8, dx, d_final = loss_head(xs, final_norm.reshape(1, D), tgt)

    conv_wit, fox_w5t, hgrn_wit = (w.transpose(0, 2, 1) for w in (conv_wi, fox_w5, hgrn_wi))
    conv_wot, fox_wot, hgrn_wot = conv_wo.T, fox_wo.T, hgrn_wo.T
    gb = {'g': None, 'u': None, 'd': None}
    d_ffn_norm = [None] * (2 * DEPTH)
    d_mix_norm = [None] * DEPTH
    gbig = {}
    gsm = {}

    def ffn_b(dy, res, e):
        xin, h, a, b = res
        dxo, da, db, z, dyh, dg = ffn_bwd_dx(xin, ffn_norm_f[e:e + 1], dy, a, b, wg_all, wu_all, wd_all, e)
        tk = min(2048, xin.shape[0])
        gb['g'] = mm_tn(h, da, name="ffn_dwg", G=NSH, M=D, N=FS, tk=tk, stack=(2 * DEPTH, e, gb['g']))
        gb['u'] = mm_tn(h, db, name="ffn_dwu", G=NSH, M=D, N=FS, tk=tk, stack=(2 * DEPTH, e, gb['u']))
        gb['d'] = mm_tn(z, dyh, name="ffn_dwd", G=NSH, M=FS, N=D, tk=tk, stack=(2 * DEPTH, e, gb['d']))
        d_ffn_norm[e] = dg
        return dxo

    for i in reversed(range(DEPTH)):
        r0, rm, r1 = saved[i]
        dx = ffn_b(dx, r1, 2 * i + 1)
        gm = mix_norm[i:i + 1]
        if i == 0:
            xin, p, h, u2, u4 = rm
            du2, dyb, gsm['conv_ln_g'], gsm['conv_ln_b'], gsm['conv_dw_b'] = conv_bwd_rows(dx, conv_wot, u2, conv_ln_g, conv_ln_b)
            dp, gsm['conv_b_in'], ddw = conv_bwd_core(du2, p, conv_dw_f, S=S)
            gsm['conv_dw'] = ddw
            gbig['conv_w_in'] = mm_tn(h, dp, name="conv_dwin", G=NSH, M=D, N=2 * D // NSH, b_step=1)
            gbig['conv_w_out'] = mm_tn(u4, dyb, name="conv_dwout", G=NSH, M=QS, N=D, a_step=1)
            dx, d_mix_norm[i] = inproj_bwd(dp, conv_wit, xin, gm, dx, name="conv_in_bwd")
        elif i == 1:
            xin, h, fl, qa, ka, va, o, o32, lse = rm
            do, dyb = mm_nt(dx, fox_wot, name="fox_out_bwd")
            qb, da = fox2_prep_bwd(do, o32, lse, qa)
            dq = fox2_dq(qb, ka, va, da, S=S, tq=min(FOX_T, S))
            dk, dv, dck = fox2_dkv(qb, ka, va, da, S=S, tq=min(FOX_T, S))
            dfl, dbf = fox_fin(dck, fl, fox_bf, S=S)
            gsm['fox_b_f'] = dbf
            dp = jnp.concatenate([dq, dk, dv, dfl], axis=1)
            dw5 = mm_tn(h, dp, name="fox_dwin", G=5, M=D, N=FOX_NP // 5, b_step=1)
            dwf = dw5.transpose(1, 0, 2).reshape(D, FOX_NP)[:, :FOX_N]
            gbig['fox_w_in'] = dwf.reshape(D, NSH, FOX_N // NSH).transpose(1, 0, 2)
            gbig['fox_w_out'] = mm_tn(o, dyb, name="fox_dwout", G=NSH, M=QS, N=D, a_step=1)
            dx, d_mix_norm[i] = inproj_bwd(dp, fox_w5t, xin, gm, dx, name="fox_in_bwd")
        elif i == 2:
            xin, p, h, yh, oh, st = rm
            dyo, dyb = mm_nt(dx, hgrn_wot, name="hgrn_out_bwd")
            dp, dlb, gsm['hgrn_norm'] = hgrn_bwd(p, oh, dyo, st, lb, hgrn_norm_f, S=S)
            gsm['hgrn_lb_logits'] = lb_bwd(hgrn_lb_logits, dlb)
            gbig['hgrn_w_in'] = mm_tn(h, dp, name="hgrn_dwin", G=NSH, M=D, N=D, b_step=1)
            gbig['hgrn_w_out'] = mm_tn(yh, dyb, name="hgrn_dwout", G=NSH, M=QS, N=D, a_step=1)
            dx, d_mix_norm[i] = inproj_bwd(dp, hgrn_wit, xin, gm, dx, name="hgrn_in_bwd")
        else:
            xin, mp = rm
            dmc, dyp, gsm['pool_scale'] = pool_bwd_rows(dx, mp, pool_wf, pool_scale_f, S=S)
            dwp = mm_tn(mp, dyp, name="pool_dw", G=4, M=POOL_G, N=POOL_G, a_step=1, b_step=1)
            gbig['pool_w'] = dwp.reshape(4, NSH, 64, POOL_G).transpose(1, 0, 2, 3).reshape(NSH, 4 * 64, POOL_G)
            dx, d_mix_norm[i] = pool_bwd_core(dmc, xin, gm, dx, S=S)
        dx = ffn_b(dx, r0, 2 * i)

    gbig['ffn_w_gate'] = gb['g'].reshape(NSH, 2 * DEPTH * D, FS)
    gbig['ffn_w_up'] = gb['u'].reshape(NSH, 2 * DEPTH * D, FS)
    gbig['ffn_w_down'] = gb['d'].reshape(NSH, 2 * DEPTH * FS, D)

    gl = [gbig[n] for n in _BIG]
    r_sib = sibling_half_exchange(gl)
    s16 = [pair_sum(g, r, pos) for g, r in zip(gl, r_sib)]
    r_ici = chip_scatter(s16)
    red = [reduce_own(g, rs, ri, pos) for g, rs, ri in zip(gl, r_sib, r_ici)]
    oth = sibling_exchange(red)
    out = {}
    for n, ga, gb_ in zip(_BIG, red, oth):
        res = adamw_halves(flat(W[n]), flat(M[n]), flat(V[n]), ga, gb_, pos)
        out[n] = [r.reshape(W[n].shape) for r in res]

    gfull = {'mix_norm': jnp.concatenate(d_mix_norm, axis=0), 'final_norm': d_final,
             'conv_b_in': gsm['conv_b_in'], 'conv_dw_b': gsm['conv_dw_b'], 'conv_ln_g': gsm['conv_ln_g'],
             'conv_ln_b': gsm['conv_ln_b'], 'fox_b_f': gsm['fox_b_f'][:, :FOX_H], 'hgrn_lb_logits': gsm['hgrn_lb_logits'],
             'ffn_norm': jnp.concatenate(d_ffn_norm, axis=0), 'conv_dw': gsm['conv_dw'][None, :CONV_W],
             'hgrn_norm': gsm['hgrn_norm'], 'pool_scale': gsm['pool_scale']}
    gpack = jnp.concatenate([_pack_replicated(gfull.get), _pack_sharded_small(gfull.get)], axis=0)
    gall = all_gather_devices(gpack)
    rep = adamw_sum(gall[:, :24], _pack_replicated(W.get), _pack_replicated(M.get), _pack_replicated(V.get))
    rep = [_unpack_replicated(r) for r in rep]
    for n in _REPLICATED:
        out[n] = [r[n].reshape(W[n].shape) for r in rep]
    gsh = lax.dynamic_slice_in_dim(gall[:, 24:], jme * QS, QS, axis=2)
    shd = adamw_sum(gsh, _pack_sharded_small(W.get), _pack_sharded_small(M.get), _pack_sharded_small(V.get))
    shd = [_unpack_sharded_small(r) for r in shd]
    for n in _SHARDED_SMALL:
        out[n] = [r[n].reshape(W[n].shape) for r in shd]

    loss = lax.psum(loss8[0, 0], ("x", "y", "c"))
    res = [loss, dx.reshape(x.shape)]
    for k in range(4):
        res += [out[n][k] for n in _WEIGHTS]
    return tuple(res)
```

```python
import functools

import jax
import jax.numpy as jnp
from jax import lax
from jax.experimental import pallas as pl
from jax.experimental.pallas import tpu as pltpu

D = 1024
F = 2816
NSH = 4
FS = F // NSH
DEPTH = 4
RMS_EPS = 1e-6
LN_EPS = 1e-5
CONV_W = 31
HALO = 32
FOX_H = 16
FOX_DH = 64
HG_H = 8
HG_DK = 128
HG_C = 32
POOL_WIN = (2, 4, 8, 16)
POOL_G = 256
MXU = jnp.bfloat16
F32 = jnp.float32
VMEM_LIMIT = 52 * 1024 * 1024

ADAM_LR = 0.001
ADAM_B1 = 0.9
ADAM_B2 = 0.999
ADAM_EPS = 1e-08
ADAM_WD = 0.01
ADAM_STEP = 10


def _call(body, *, name, grid, in_specs, out_specs, out_shape, scratch=()):
    return pl.pallas_call(
        body, name=name, grid=grid, in_specs=in_specs, out_specs=out_specs, out_shape=out_shape,
        scratch_shapes=list(scratch),
        compiler_params=pltpu.CompilerParams(dimension_semantics=("arbitrary",) * len(grid),
                                             vmem_limit_bytes=VMEM_LIMIT))


def _dot(a, b):
    return jnp.dot(a, b, preferred_element_type=F32)


def _dot_nt(a, b):
    return lax.dot_general(a, b, (((1,), (1,)), ((), ())), preferred_element_type=F32)


def _dot_tn(a, b):
    return lax.dot_general(a, b, (((0,), (0,)), ((), ())), preferred_element_type=F32)


def _split(x):
    hi = x.astype(MXU)
    return hi, (x - hi.astype(F32)).astype(MXU)


def _sigmoid(x):
    return 1.0 / (1.0 + jnp.exp(-x))


def _rms(x, g):
    r = lax.rsqrt(jnp.mean(x * x, axis=-1, keepdims=True) + RMS_EPS)
    return x * r * g


def _rms_bwd(dh, x, g):
    r = lax.rsqrt(jnp.mean(x * x, axis=-1, keepdims=True) + RMS_EPS)
    xh = x * r
    dhg = dh * g
    dx = r * (dhg - xh * jnp.mean(dhg * xh, axis=-1, keepdims=True))
    return dx, jnp.sum(dh * xh, axis=0, keepdims=True)


def _sds(shape, dtype):
    return jax.ShapeDtypeStruct(shape, dtype)


def _wspec(w, e):
    if w.ndim == 3:
        return pl.BlockSpec((None,) + w.shape[1:], lambda i, j: (j, 0, 0))
    return pl.BlockSpec((None, None) + w.shape[2:], lambda i, j: (j, e, 0, 0))


def ffn_fwd(x, g, wg, wu, wd, e=0, *, tm=512):
    T = x.shape[0]

    def body(x_ref, g_ref, wg_ref, wu_ref, wd_ref, xo_ref, h_ref, u_ref, sa_ref, z_ref, acc_ref):
        j = pl.program_id(1)

        @pl.when(j == 0)
        def _():
            h_ref[...] = _rms(x_ref[...], g_ref[...]).astype(MXU)
            acc_ref[...] = jnp.zeros_like(acc_ref)

        h = h_ref[...]
        a = _dot(h, wg_ref[...])
        b = _dot(h, wu_ref[...])
        s = _sigmoid(a)
        sa = a * s
        u_ref[...] = (b * (s * (1.0 + a * (1.0 - s)))).astype(MXU)
        sa_ref[...] = sa.astype(MXU)
        z = (sa * b).astype(MXU)
        z_ref[...] = z
        acc_ref[...] += _dot(z, wd_ref[...])

        @pl.when(j == NSH - 1)
        def _():
            xo_ref[...] = x_ref[...] + 0.5 * acc_ref[...]

    return _call(
        body, name="ffn_fwd", grid=(T // tm, NSH),
        in_specs=[pl.BlockSpec((tm, D), lambda i, j: (i, 0)),
                  pl.BlockSpec((1, D), lambda i, j: (0, 0)),
                  _wspec(wg, e), _wspec(wu, e), _wspec(wd, e)],
        out_specs=[pl.BlockSpec((tm, D), lambda i, j: (i, 0)),
                   pl.BlockSpec((tm, D), lambda i, j: (i, 0)),
                   pl.BlockSpec((None, tm, FS), lambda i, j: (j, i, 0)),
                   pl.BlockSpec((None, tm, FS), lambda i, j: (j, i, 0)),
                   pl.BlockSpec((None, tm, FS), lambda i, j: (j, i, 0))],
        out_shape=[_sds((T, D), F32), _sds((T, D), MXU)] + [_sds((NSH, T, FS), MXU)] * 3,
        scratch=[pltpu.VMEM((tm, D), F32)],
    )(x, g, wg, wu, wd)


def ffn_bwd_dx(x, g, dy, u, sa, wg, wu, wd, e=0, *, tm=512):
    T = x.shape[0]

    def body(x_ref, g_ref, dy_ref, u_ref, sa_ref, wg_ref, wu_ref, wd_ref,
             dx_ref, da_ref, db_ref, dyh_ref, dg_ref, acc_ref):
        i = pl.program_id(0)
        j = pl.program_id(1)

        @pl.when(j == 0)
        def _():
            dyh_ref[...] = (0.5 * dy_ref[...]).astype(MXU)
            acc_ref[...] = jnp.zeros_like(acc_ref)

        @pl.when((i == 0) & (j == 0))
        def _():
            dg_ref[...] = jnp.zeros_like(dg_ref)

        dz = _dot_nt(dyh_ref[...], wd_ref[...])
        da = (dz * u_ref[...].astype(F32)).astype(MXU)
        db = (dz * sa_ref[...].astype(F32)).astype(MXU)
        da_ref[...] = da
        db_ref[...] = db
        acc_ref[...] += _dot_nt(da, wg_ref[...]) + _dot_nt(db, wu_ref[...])

        @pl.when(j == NSH - 1)
        def _():
            dxn, dg = _rms_bwd(acc_ref[...], x_ref[...], g_ref[...])
            dx_ref[...] = dy_ref[...] + dxn
            dg_ref[...] += dg

    return _call(
        body, name="ffn_bwd_dx", grid=(T // tm, NSH),
        in_specs=[pl.BlockSpec((tm, D), lambda i, j: (i, 0)),
                  pl.BlockSpec((1, D), lambda i, j: (0, 0)),
                  pl.BlockSpec((tm, D), lambda i, j: (i, 0)),
                  pl.BlockSpec((None, tm, FS), lambda i, j: (j, i, 0)),
                  pl.BlockSpec((None, tm, FS), lambda i, j: (j, i, 0)),
                  _wspec(wg, e), _wspec(wu, e), _wspec(wd, e)],
        out_specs=[pl.BlockSpec((tm, D), lambda i, j: (i, 0)),
                   pl.BlockSpec((None, tm, FS), lambda i, j: (j, i, 0)),
                   pl.BlockSpec((None, tm, FS), lambda i, j: (j, i, 0)),
                   pl.BlockSpec((tm, D), lambda i, j: (i, 0)),
                   pl.BlockSpec((1, D), lambda i, j: (0, 0))],
        out_shape=[_sds((T, D), F32), _sds((NSH, T, FS), MXU), _sds((NSH, T, FS), MXU),
                   _sds((T, D), MXU), _sds((1, D), F32)],
        scratch=[pltpu.VMEM((tm, D), F32)],
    )(x, g, dy, u, sa, wg, wu, wd)


def mm_tn(a, b, *, name, G, M, N, a_step=0, b_step=0, tk=512, stack=None):
    T = a.shape[-2]
    if stack is not None:
        return _mm_tn_stack(a, b, name=name, G=G, M=M, N=N, tk=tk, stack=stack)

    def spec(arr, width, step):
        if arr.ndim == 3:
            return pl.BlockSpec((None, tk, width), lambda g, k: (g, k, 0))
        return pl.BlockSpec((tk, width), lambda g, k: (k, g * step))

    def body(a_ref, b_ref, o_ref):
        @pl.when(pl.program_id(1) == 0)
        def _():
            o_ref[...] = jnp.zeros_like(o_ref)

        o_ref[...] += _dot_tn(a_ref[...], b_ref[...])

    return _call(
        body, name=name, grid=(G, T // tk),
        in_specs=[spec(a, M, a_step), spec(b, N, b_step)],
        out_specs=pl.BlockSpec((None, M, N), lambda g, k: (g, 0, 0)),
        out_shape=_sds((G, M, N), F32),
    )(a, b)


def norm_mm(x, g, wb, bias, *, name, out_dtype, tm=1024):
    T = x.shape[0]
    tm = min(tm, T)
    G, _, ns = wb.shape
    has_bias = bias is not None

    def body(*refs):
        if has_bias:
            x_ref, g_ref, w_ref, bias_ref, p_ref, h_ref = refs
        else:
            x_ref, g_ref, w_ref, p_ref, h_ref = refs

        @pl.when(pl.program_id(1) == 0)
        def _():
            h_ref[...] = _rms(x_ref[...], g_ref[...]).astype(MXU)

        p = _dot(h_ref[...], w_ref[...])
        if has_bias:
            p = p + bias_ref[...]
        p_ref[...] = p.astype(out_dtype)

    in_specs = [pl.BlockSpec((tm, D), lambda i, j: (i, 0)),
                pl.BlockSpec((1, D), lambda i, j: (0, 0)),
                pl.BlockSpec((None, D, ns), lambda i, j: (j, 0, 0))]
    args = [x, g, wb]
    if has_bias:
        in_specs.append(pl.BlockSpec((None, 1, ns), lambda i, j: (j, 0, 0)))
        args.append(bias)
    return _call(
        body, name=name, grid=(T // tm, G), in_specs=in_specs,
        out_specs=[pl.BlockSpec((tm, ns), lambda i, j: (i, j)),
                   pl.BlockSpec((tm, D), lambda i, j: (i, 0))],
        out_shape=[_sds((T, G * ns), out_dtype), _sds((T, D), MXU)],
    )(*args)


def mm_res(y, w, x, *, name, tm=1024):
    T, K = y.shape
    tm = min(tm, T)

    def body(y_ref, w_ref, x_ref, o_ref):
        o_ref[...] = x_ref[...] + _dot(y_ref[...], w_ref[...])

    return _call(
        body, name=name, grid=(T // tm,),
        in_specs=[pl.BlockSpec((tm, K), lambda i: (i, 0)),
                  pl.BlockSpec((K, D), lambda i: (0, 0)),
                  pl.BlockSpec((tm, D), lambda i: (i, 0))],
        out_specs=pl.BlockSpec((tm, D), lambda i: (i, 0)),
        out_shape=_sds((T, D), F32),
    )(y, w, x)


def mm_nt(a, wt, *, name, tm=1024):
    T, K = a.shape
    tm = min(tm, T)
    N = wt.shape[1]
    w = wt

    def body(a_ref, w_ref, o_ref, ab_ref):
        ab = a_ref[...].astype(MXU)
        ab_ref[...] = ab
        o_ref[...] = _dot(ab, w_ref[...])

    return _call(
        body, name=name, grid=(T // tm,),
        in_specs=[pl.BlockSpec((tm, K), lambda i: (i, 0)),
                  pl.BlockSpec((K, N), lambda i: (0, 0))],
        out_specs=[pl.BlockSpec((tm, N), lambda i: (i, 0)),
                   pl.BlockSpec((tm, K), lambda i: (i, 0))],
        out_shape=[_sds((T, N), F32), _sds((T, K), MXU)],
    )(a, w)


def inproj_bwd(dp, wb, x, g, dres, *, name, tm=1024):
    T = x.shape[0]
    tm = min(tm, T)
    G, ns, _ = wb.shape

    def body(dp_ref, w_ref, x_ref, g_ref, dres_ref, dx_ref, dg_ref, acc_ref):
        i = pl.program_id(0)
        j = pl.program_id(1)

        @pl.when(j == 0)
        def _():
            acc_ref[...] = jnp.zeros_like(acc_ref)

        @pl.when((i == 0) & (j == 0))
        def _():
            dg_ref[...] = jnp.zeros_like(dg_ref)

        acc_ref[...] += _dot(dp_ref[...], w_ref[...])

        @pl.when(j == G - 1)
        def _():
            dxn, dg = _rms_bwd(acc_ref[...], x_ref[...], g_ref[...])
            dx_ref[...] = dres_ref[...] + dxn
            dg_ref[...] += dg

    return _call(
        body, name=name, grid=(T // tm, G),
        in_specs=[pl.BlockSpec((tm, ns), lambda i, j: (i, j)),
                  pl.BlockSpec((None, ns, D), lambda i, j: (j, 0, 0)),
                  pl.BlockSpec((tm, D), lambda i, j: (i, 0)),
                  pl.BlockSpec((1, D), lambda i, j: (0, 0)),
                  pl.BlockSpec((tm, D), lambda i, j: (i, 0))],
        out_specs=[pl.BlockSpec((tm, D), lambda i, j: (i, 0)),
                   pl.BlockSpec((1, D), lambda i, j: (0, 0))],
        out_shape=[_sds((T, D), F32), _sds((1, D), F32)],
        scratch=[pltpu.VMEM((tm, D), F32)],
    )(dp, wb, x, g, dres)


def loss_head(x, gf, tgt, *, tm=512):
    T = x.shape[0]

    def body(x_ref, g_ref, t_ref, loss_ref, dx_ref, dg_ref):
        @pl.when(pl.program_id(0) == 0)
        def _():
            loss_ref[...] = jnp.zeros_like(loss_ref)
            dg_ref[...] = jnp.zeros_like(dg_ref)

        xv = x_ref[...]
        gv = g_ref[...]
        e = _rms(xv, gv) - t_ref[...]
        loss_ref[...] += (0.5 / D) * jnp.sum(e * e)
        dxn, dg = _rms_bwd(e * (1.0 / D), xv, gv)
        dx_ref[...] = dxn
        dg_ref[...] += dg

    return _call(
        body, name="loss_head", grid=(T // tm,),
        in_specs=[pl.BlockSpec((tm, D), lambda i: (i, 0)),
                  pl.BlockSpec((1, D), lambda i: (0, 0)),
                  pl.BlockSpec((tm, D), lambda i: (i, 0))],
        out_specs=[pl.BlockSpec((8, 128), lambda i: (0, 0)),
                   pl.BlockSpec((tm, D), lambda i: (i, 0)),
                   pl.BlockSpec((1, D), lambda i: (0, 0))],
        out_shape=[_sds((8, 128), F32), _sds((T, D), F32), _sds((1, D), F32)],
    )(x, gf, tgt)


def _glu(p):
    return p[:, :D] * _sigmoid(p[:, D:])


def _ln_stats(u):
    mu = jnp.mean(u, axis=-1, keepdims=True)
    xc = u - mu
    rstd = lax.rsqrt(jnp.mean(xc * xc, axis=-1, keepdims=True) + LN_EPS)
    return xc * rstd, rstd


def conv_fwd_core(p, dw, dwb, lng, lnb, *, S, tt=256):
    T = p.shape[0]
    nb = S // tt
    r = tt // HALO

    def body(pc_ref, pp_ref, dw_ref, dwb_ref, lng_ref, lnb_ref, u2_ref, u4_ref, ubuf):
        first = (pl.program_id(0) % nb) == 0
        ubuf[0:HALO, :] = jnp.where(first, 0.0, _glu(pp_ref[...]))
        ubuf[HALO:, :] = _glu(pc_ref[...])
        for c in range(D // 128):
            cs = slice(c * 128, (c + 1) * 128)
            acc = jnp.zeros((tt, 128), F32)
            for k in range(CONV_W):
                acc = acc + dw_ref[k:k + 1, cs] * ubuf[k + 2:k + 2 + tt, cs]
            u2_ref[:, cs] = acc + dwb_ref[:, cs]
        xh, _ = _ln_stats(u2_ref[...])
        u3 = xh * lng_ref[...] + lnb_ref[...]
        u4_ref[...] = (u3 * _sigmoid(u3)).astype(MXU)

    row = pl.BlockSpec((1, D), lambda i: (0, 0))
    return _call(
        body, name="conv_fwd_core", grid=(T // tt,),
        in_specs=[pl.BlockSpec((tt, 2 * D), lambda i: (i, 0)),
                  pl.BlockSpec((HALO, 2 * D), lambda i: (jnp.maximum(i * r - 1, 0), 0)),
                  pl.BlockSpec((HALO, D), lambda i: (0, 0)), row, row, row],
        out_specs=[pl.BlockSpec((tt, D), lambda i: (i, 0)), pl.BlockSpec((tt, D), lambda i: (i, 0))],
        out_shape=[_sds((T, D), F32), _sds((T, D), MXU)],
        scratch=[pltpu.VMEM((tt + HALO, D), F32)],
    )(p, p, dw, dwb, lng, lnb)


def conv_bwd_rows(dy, wout, u2, lng, lnb, *, tm=512):
    T = dy.shape[0]

    def body(dy_ref, w_ref, u2_ref, lng_ref, lnb_ref, du2_ref, dyb_ref, dlng_ref, dlnb_ref, ddwb_ref):
        @pl.when(pl.program_id(0) == 0)
        def _():
            dlng_ref[...] = jnp.zeros_like(dlng_ref)
            dlnb_ref[...] = jnp.zeros_like(dlnb_ref)
            ddwb_ref[...] = jnp.zeros_like(ddwb_ref)

        dyb = dy_ref[...].astype(MXU)
        dyb_ref[...] = dyb
        du4 = _dot(dyb, w_ref[...])
        xh, rstd = _ln_stats(u2_ref[...])
        lng_v = lng_ref[...]
        u3 = xh * lng_v + lnb_ref[...]
        s = _sigmoid(u3)
        du3 = du4 * (s * (1.0 + u3 * (1.0 - s)))
        dlng_ref[...] += jnp.sum(du3 * xh, axis=0, keepdims=True)
        dlnb_ref[...] += jnp.sum(du3, axis=0, keepdims=True)
        dxh = du3 * lng_v
        du2 = rstd * (dxh - jnp.mean(dxh, axis=-1, keepdims=True)
                      - xh * jnp.mean(dxh * xh, axis=-1, keepdims=True))
        du2_ref[...] = du2
        ddwb_ref[...] += jnp.sum(du2, axis=0, keepdims=True)

    row = pl.BlockSpec((1, D), lambda i: (0, 0))
    blk = pl.BlockSpec((tm, D), lambda i: (i, 0))
    return _call(
        body, name="conv_bwd_rows", grid=(T // tm,),
        in_specs=[blk, pl.BlockSpec((D, D), lambda i: (0, 0)), blk, row, row],
        out_specs=[blk, blk, row, row, row],
        out_shape=[_sds((T, D), F32), _sds((T, D), MXU), _sds((1, D), F32), _sds((1, D), F32), _sds((1, D), F32)],
    )(dy, wout, u2, lng, lnb)


def conv_bwd_core(du2, p, dw, *, S, tt=256):
    T = p.shape[0]
    nb = S // tt
    r = tt // HALO
    last_halo = T // HALO - 1

    def body(dc_ref, dn_ref, pc_ref, pp_ref, dw_ref, dp_ref, dbin_ref, ddw_ref, ubuf, dbuf):
        i = pl.program_id(0)

        @pl.when(i == 0)
        def _():
            dbin_ref[...] = jnp.zeros_like(dbin_ref)
            ddw_ref[...] = jnp.zeros_like(ddw_ref)

        first = (i % nb) == 0
        last = (i % nb) == nb - 1
        ubuf[0:HALO, :] = jnp.where(first, 0.0, _glu(pp_ref[...]))
        ubuf[HALO:, :] = _glu(pc_ref[...])
        dbuf[0:tt, :] = dc_ref[...]
        dbuf[tt:, :] = jnp.where(last, 0.0, dn_ref[...])
        pc = pc_ref[...]
        for c in range(D // 128):
            cs = slice(c * 128, (c + 1) * 128)
            dcur = dbuf[0:tt, cs]
            du = jnp.zeros((tt, 128), F32)
            for k in range(CONV_W):
                ddw_ref[k:k + 1, cs] += jnp.sum(dcur * ubuf[k + 2:k + 2 + tt, cs], axis=0, keepdims=True)
                du = du + dw_ref[k:k + 1, cs] * dbuf[CONV_W - 1 - k:CONV_W - 1 - k + tt, cs]
            a = pc[:, c * 128:(c + 1) * 128]
            sb = _sigmoid(pc[:, D + c * 128:D + (c + 1) * 128])
            da = du * sb
            db = du * a * sb * (1.0 - sb)
            dp_ref[:, cs] = da.astype(MXU)
            dp_ref[:, D + c * 128:D + (c + 1) * 128] = db.astype(MXU)
            dbin_ref[:, cs] += jnp.sum(da, axis=0, keepdims=True)
            dbin_ref[:, D + c * 128:D + (c + 1) * 128] += jnp.sum(db, axis=0, keepdims=True)

    return _call(
        body, name="conv_bwd_core", grid=(T // tt,),
        in_specs=[pl.BlockSpec((tt, D), lambda i: (i, 0)),
                  pl.BlockSpec((HALO, D), lambda i: (jnp.minimum((i + 1) * r, last_halo), 0)),
                  pl.BlockSpec((tt, 2 * D), lambda i: (i, 0)),
                  pl.BlockSpec((HALO, 2 * D), lambda i: (jnp.maximum(i * r - 1, 0), 0)),
                  pl.BlockSpec((HALO, D), lambda i: (0, 0))],
        out_specs=[pl.BlockSpec((tt, 2 * D), lambda i: (i, 0)),
                   pl.BlockSpec((1, 2 * D), lambda i: (0, 0)),
                   pl.BlockSpec((HALO, D), lambda i: (0, 0))],
        out_shape=[_sds((T, 2 * D), MXU), _sds((1, 2 * D), F32), _sds((HALO, D), F32)],
        scratch=[pltpu.VMEM((tt + HALO, D), F32), pltpu.VMEM((tt + HALO, D), F32)],
    )(du2, du2, p, p, dw)


PH = 16


def _pool_cnt(i, nb, tt, win):
    pos = (i % nb) * tt + lax.broadcasted_iota(jnp.int32, (tt, 1), 0)
    return jnp.minimum(pos + 1, win).astype(F32)


def pool_fwd(x, g, wp, scale, *, S, tt=256):
    T = x.shape[0]
    nb = S // tt
    r = tt // PH

    def body(xc_ref, xp_ref, g_ref, wp_ref, sc_ref, xo_ref, m_ref, hbuf):
        i = pl.program_id(0)
        first = (i % nb) == 0
        gv = g_ref[...]
        hbuf[0:PH, :] = jnp.where(first, 0.0, _rms(xp_ref[...], gv))
        xc = xc_ref[...]
        hbuf[PH:, :] = _rms(xc, gv)
        for gi, win in enumerate(POOL_WIN):
            gs = slice(gi * POOL_G, (gi + 1) * POOL_G)
            acc = hbuf[PH:PH + tt, gs]
            for j in range(1, win):
                acc = acc + hbuf[PH - j:PH - j + tt, gs]
            m = (acc / _pool_cnt(i, nb, tt, win) - hbuf[PH:PH + tt, gs]).astype(MXU)
            m_ref[:, gs] = m
            xo_ref[:, gs] = xc[:, gs] + _dot(m, wp_ref[gi]) * sc_ref[:, gs]

    row = pl.BlockSpec((1, D), lambda i: (0, 0))
    blk = pl.BlockSpec((tt, D), lambda i: (i, 0))
    return _call(
        body, name="pool_fwd", grid=(T // tt,),
        in_specs=[blk, pl.BlockSpec((PH, D), lambda i: (jnp.maximum(i * r - 1, 0), 0)), row,
                  pl.BlockSpec((len(POOL_WIN), POOL_G, POOL_G), lambda i: (0, 0, 0)), row],
        out_specs=[blk, blk],
        out_shape=[_sds((T, D), F32), _sds((T, D), MXU)],
        scratch=[pltpu.VMEM((tt + PH, D), F32)],
    )(x, x, g, wp, scale)


def pool_bwd_rows(dy, m, wp, scale, *, S, tt=256):
    T = dy.shape[0]
    nb = S // tt

    def body(dy_ref, m_ref, wp_ref, sc_ref, dmc_ref, dyp_ref, dsc_ref):
        i = pl.program_id(0)

        @pl.when(i == 0)
        def _():
            dsc_ref[...] = jnp.zeros_like(dsc_ref)

        for gi, win in enumerate(POOL_WIN):
            gs = slice(gi * POOL_G, (gi + 1) * POOL_G)
            dyg = dy_ref[:, gs]
            w = wp_ref[gi]
            dsc_ref[:, gs] += jnp.sum(dyg * _dot(m_ref[:, gs], w), axis=0, keepdims=True)
            dyp = (dyg * sc_ref[:, gs]).astype(MXU)
            dyp_ref[:, gs] = dyp
            dmc_ref[:, gs] = _dot_nt(dyp, w) / _pool_cnt(i, nb, tt, win)

    row = pl.BlockSpec((1, D), lambda i: (0, 0))
    blk = pl.BlockSpec((tt, D), lambda i: (i, 0))
    return _call(
        body, name="pool_bwd_rows", grid=(T // tt,),
        in_specs=[blk, blk, pl.BlockSpec((len(POOL_WIN), POOL_G, POOL_G), lambda i: (0, 0, 0)), row],
        out_specs=[blk, blk, row],
        out_shape=[_sds((T, D), F32), _sds((T, D), MXU), _sds((1, D), F32)],
    )(dy, m, wp, scale)


def pool_bwd_core(dmc, x, g, dres, *, S, tt=256):
    T = x.shape[0]
    nb = S // tt
    r = tt // PH
    last_halo = T // PH - 1

    def body(dc_ref, dn_ref, x_ref, g_ref, dres_ref, dx_ref, dg_ref, dbuf, dh_buf):
        i = pl.program_id(0)

        @pl.when(i == 0)
        def _():
            dg_ref[...] = jnp.zeros_like(dg_ref)

        last = (i % nb) == nb - 1
        dbuf[0:tt, :] = dc_ref[...]
        dbuf[tt:, :] = jnp.where(last, 0.0, dn_ref[...])
        for gi, win in enumerate(POOL_WIN):
            gs = slice(gi * POOL_G, (gi + 1) * POOL_G)
            cur = dbuf[0:tt, gs]
            acc = cur
            for j in range(1, win):
                acc = acc + dbuf[j:j + tt, gs]
            dh_buf[:, gs] = acc - cur * _pool_cnt(i, nb, tt, win)
        dxn, dg = _rms_bwd(dh_buf[...], x_ref[...], g_ref[...])
        dx_ref[...] = dres_ref[...] + dxn
        dg_ref[...] += dg

    row = pl.BlockSpec((1, D), lambda i: (0, 0))
    blk = pl.BlockSpec((tt, D), lambda i: (i, 0))
    return _call(
        body, name="pool_bwd_core", grid=(T // tt,),
        in_specs=[blk, pl.BlockSpec((PH, D), lambda i: (jnp.minimum((i + 1) * r, last_halo), 0)), blk, row, blk],
        out_specs=[blk, row],
        out_shape=[_sds((T, D), F32), _sds((1, D), F32)],
        scratch=[pltpu.VMEM((tt + PH, D), F32), pltpu.VMEM((tt, D), F32)],
    )(dmc, dmc, x, g, dres)


NEG = -1e30


def _tri(n, upper=False):
    r = lax.broadcasted_iota(jnp.int32, (n, n), 0)
    c = lax.broadcasted_iota(jnp.int32, (n, n), 1)
    return (r <= c if upper else r >= c).astype(F32)


def _dot_hi(a, b):
    return jnp.dot(a, b, preferred_element_type=F32, precision=lax.Precision.HIGHEST)


def _log_sigmoid(z):
    return jnp.minimum(z, 0.0) - jnp.log(1.0 + jnp.exp(-jnp.abs(z)))


def fox_cum(fl, bf, *, S, tt=256):
    T = fl.shape[0]
    nb = S // tt

    def body(fl_ref, bf_ref, c_ref, ct_ref, carry):
        i = pl.program_id(0)

        @pl.when((i % nb) == 0)
        def _():
            carry[...] = jnp.zeros_like(carry)

        lf = _log_sigmoid(fl_ref[...] + bf_ref[...])
        c = _dot_hi(_tri(tt), lf) + carry[...]
        c_ref[...] = c
        carry[...] = c[tt - 1:tt, :]
        ct_ref[...] = c.T[0:FOX_H, :]

    return _call(
        body, name="fox_cum", grid=(T // tt,),
        in_specs=[pl.BlockSpec((tt, 128), lambda i: (i, 0)), pl.BlockSpec((1, 128), lambda i: (0, 0))],
        out_specs=[pl.BlockSpec((tt, 128), lambda i: (i, 0)),
                   pl.BlockSpec((None, FOX_H, tt), lambda i: (i // nb, 0, i % nb))],
        out_shape=[_sds((T, 128), F32), _sds((T // S, FOX_H, S), F32)],
        scratch=[pltpu.VMEM((1, 128), F32)],
    )(fl, bf)


def _fox_logits(q_ref, k_ref, c_ref, ct_ref, h, diag, tq):
    hs = slice(h * FOX_DH, (h + 1) * FOX_DH)
    s = _dot_nt(q_ref[:, hs], k_ref[:, hs]) * (FOX_DH ** -0.5) + (c_ref[:, h:h + 1] - ct_ref[h:h + 1, :])
    if diag:
        r = lax.broadcasted_iota(jnp.int32, (tq, tq), 0)
        c = lax.broadcasted_iota(jnp.int32, (tq, tq), 1)
        s = jnp.where(r >= c, s, NEG)
    return s


def _fox_specs(tq, nq, q_of, k_of):
    qrow = lambda col: pl.BlockSpec((tq, D), lambda b, i, j: (b * nq + q_of(i, j), col))
    krow = lambda col: pl.BlockSpec((tq, D), lambda b, i, j: (b * nq + k_of(i, j), col))
    qvec = pl.BlockSpec((tq, 128), lambda b, i, j: (b * nq + q_of(i, j), 0))
    kvec = pl.BlockSpec((tq, 128), lambda b, i, j: (b * nq + k_of(i, j), 0))
    ct = pl.BlockSpec((None, FOX_H, tq), lambda b, i, j: (b, 0, k_of(i, j)))
    return qrow, krow, qvec, kvec, ct


def fox_fwd(qkv, c, ct, *, S, tq=256):
    T = qkv.shape[0]
    nq = S // tq

    def body(q_ref, k_ref, v_ref, c_ref, ct_ref, o_ref, o32_ref, lse_ref, m_sc, l_sc, acc, acc_lo):
        qi = pl.program_id(1)
        ki = pl.program_id(2)

        @pl.when(ki == 0)
        def _():
            m_sc[...] = jnp.full_like(m_sc, NEG)
            l_sc[...] = jnp.zeros_like(l_sc)
            acc[...] = jnp.zeros_like(acc)
            acc_lo[...] = jnp.zeros_like(acc_lo)

        def step(diag):
            for h in range(FOX_H):
                hs = slice(h * FOX_DH, (h + 1) * FOX_DH)
                s = _fox_logits(q_ref, k_ref, c_ref, ct_ref, h, diag, tq)
                m_prev = m_sc[:, h:h + 1]
                m_new = jnp.maximum(m_prev, jnp.max(s, axis=-1, keepdims=True))
                alpha = jnp.exp(m_prev - m_new)
                p = jnp.exp(s - m_new)
                l_sc[:, h:h + 1] = alpha * l_sc[:, h:h + 1] + jnp.sum(p, axis=-1, keepdims=True)
                hi, lo = _split(p)
                acc[:, hs] = alpha * acc[:, hs] + _dot(hi, v_ref[:, hs])
                acc_lo[:, hs] = alpha * acc_lo[:, hs] + _dot(lo, v_ref[:, hs])
                m_sc[:, h:h + 1] = m_new

        @pl.when(ki < qi)
        def _():
            step(False)

        @pl.when(ki == qi)
        def _():
            step(True)
            for h in range(FOX_H):
                hs = slice(h * FOX_DH, (h + 1) * FOX_DH)
                o_ref[:, hs] = (acc[:, hs] / l_sc[:, h:h + 1]).astype(MXU)
                o32_ref[:, hs] = (acc[:, hs] + acc_lo[:, hs]) / l_sc[:, h:h + 1]
            lse_ref[...] = m_sc[...] + jnp.log(jnp.maximum(l_sc[...], 1e-37))

    qrow, krow, qvec, kvec, ctspec = _fox_specs(tq, nq, lambda i, j: i, lambda i, j: jnp.minimum(i, j))
    return _call(
        body, name="fox_fwd", grid=(T // S, nq, nq),
        in_specs=[qrow(0), krow(1), krow(2), qvec, ctspec],
        out_specs=[qrow(0), qrow(0), qvec],
        out_shape=[_sds((T, D), MXU), _sds((T, D), F32), _sds((T, 128), F32)],
        scratch=[pltpu.VMEM((tq, 128), F32), pltpu.VMEM((tq, 128), F32), pltpu.VMEM((tq, D), F32),
                 pltpu.VMEM((tq, D), F32)],
    )(qkv, qkv, qkv, c, ct)


def fox_bwd_dq(qkv, do, o, lse, c, ct, *, S, tq=256):
    T = qkv.shape[0]
    nq = S // tq

    def body(q_ref, k_ref, v_ref, do32_ref, o_ref, lse_ref, c_ref, ct_ref, dq_ref, dl_ref, do_ref, acc):
        qi = pl.program_id(1)
        ki = pl.program_id(2)

        @pl.when(ki == 0)
        def _():
            acc[...] = jnp.zeros_like(acc)
            dl_ref[...] = jnp.zeros_like(dl_ref)
            do_ref[...] = do32_ref[...].astype(MXU)
            for h in range(FOX_H):
                hs = slice(h * FOX_DH, (h + 1) * FOX_DH)
                dl_ref[:, h:h + 1] = jnp.sum(do_ref[:, hs].astype(F32) * o_ref[:, hs], axis=-1, keepdims=True)

        def step(diag):
            for h in range(FOX_H):
                hs = slice(h * FOX_DH, (h + 1) * FOX_DH)
                s = _fox_logits(q_ref, k_ref, c_ref, ct_ref, h, diag, tq)
                p = jnp.exp(s - lse_ref[:, h:h + 1])
                dp = _dot_nt(do_ref[:, hs], v_ref[:, hs])
                hi, lo = _split(p * (dp - dl_ref[:, h:h + 1]))
                acc[:, hs] += _dot(hi, k_ref[:, hs]) + _dot(lo, k_ref[:, hs])

        @pl.when(ki < qi)
        def _():
            step(False)

        @pl.when(ki == qi)
        def _():
            step(True)
            dq_ref[...] = (acc[...] * (FOX_DH ** -0.5)).astype(MXU)

    qrow, krow, qvec, kvec, ctspec = _fox_specs(tq, nq, lambda i, j: i, lambda i, j: jnp.minimum(i, j))
    orow = pl.BlockSpec((tq, D), lambda b, i, j: (b * nq + i, 0))
    return _call(
        body, name="fox_bwd_dq", grid=(T // S, nq, nq),
        in_specs=[qrow(0), krow(1), krow(2), orow, orow, qvec, qvec, ctspec],
        out_specs=[orow, qvec, orow],
        out_shape=[_sds((T, D), MXU), _sds((T, 128), F32), _sds((T, D), MXU)],
        scratch=[pltpu.VMEM((tq, D), F32)],
    )(qkv, qkv, qkv, do, o, lse, c, ct)


def fox_bwd_dkv(qkv, do, lse, delta, c, ct, *, S, tq=256):
    T = qkv.shape[0]
    nq = S // tq

    def body(q_ref, k_ref, v_ref, do_ref, lse_ref, dl_ref, c_ref, ct_ref, dk_ref, dv_ref, dck_ref, dk_acc, dv_acc):
        ki = pl.program_id(1)
        qi = pl.program_id(2)

        @pl.when(qi == 0)
        def _():
            dk_acc[...] = jnp.zeros_like(dk_acc)
            dv_acc[...] = jnp.zeros_like(dv_acc)
            dck_ref[...] = jnp.zeros_like(dck_ref)

        def step(diag):
            ones = jnp.ones((tq, 128), MXU)
            for h in range(FOX_H):
                hs = slice(h * FOX_DH, (h + 1) * FOX_DH)
                s = _fox_logits(q_ref, k_ref, c_ref, ct_ref, h, diag, tq)
                p = jnp.exp(s - lse_ref[:, h:h + 1])
                doh = do_ref[:, hs]
                dv_acc[:, hs] += _dot_tn(p.astype(MXU), doh)
                dp = _dot_nt(doh, v_ref[:, hs])
                hi, lo = _split(p * (dp - dl_ref[:, h:h + 1]))
                dk_acc[:, hs] += _dot_tn(hi, q_ref[:, hs]) + _dot_tn(lo, q_ref[:, hs])
                dck_ref[:, h:h + 1] += (_dot_tn(hi, ones) + _dot_tn(lo, ones))[:, 0:1]

        @pl.when(qi > ki)
        def _():
            step(False)

        @pl.when(qi == ki)
        def _():
            step(True)

        @pl.when(qi == nq - 1)
        def _():
            dk_ref[...] = (dk_acc[...] * (FOX_DH ** -0.5)).astype(MXU)
            dv_ref[...] = dv_acc[...].astype(MXU)

    qrow, krow, qvec, kvec, ctspec = _fox_specs(tq, nq, lambda i, j: jnp.maximum(i, j), lambda i, j: i)
    qo = pl.BlockSpec((tq, D), lambda b, i, j: (b * nq + jnp.maximum(i, j), 0))
    ko = pl.BlockSpec((tq, D), lambda b, i, j: (b * nq + i, 0))
    return _call(
        body, name="fox_bwd_dkv", grid=(T // S, nq, nq),
        in_specs=[qrow(0), krow(1), krow(2), qo, qvec, qvec, qvec, ctspec],
        out_specs=[ko, ko, kvec],
        out_shape=[_sds((T, D), MXU), _sds((T, D), MXU), _sds((T, 128), F32)],
        scratch=[pltpu.VMEM((tq, D), F32), pltpu.VMEM((tq, D), F32)],
    )(qkv, qkv, qkv, do, lse, delta, c, ct)


HW = 128
FOX_T = 512
COL_ONE = FOX_DH + 3
COL_LSE = FOX_DH + 6


def _parts(x):
    hi = x.astype(MXU).astype(F32)
    mid = (x - hi).astype(MXU).astype(F32)
    lo = (x - hi - mid).astype(MXU).astype(F32)
    return [hi, mid, lo]


def _aug(n, cols):
    lane = lax.broadcasted_iota(jnp.int32, (n, HW - FOX_DH), 1)
    out = jnp.zeros((n, HW - FOX_DH), F32)
    for i, cval in enumerate(cols):
        out = jnp.where(lane == i, cval, out)
    return out


def fox_prep(p, c, *, tt=256):
    T = p.shape[0]

    def body(q_ref, k_ref, v_ref, c_ref, qa_ref, ka_ref, va_ref):
        ones = [1.0, 1.0, 1.0]
        for h in range(FOX_H):
            hs = slice(h * FOX_DH, (h + 1) * FOX_DH)
            lo, mid = h * HW, h * HW + FOX_DH
            cp = _parts(c_ref[:, h:h + 1])
            qa_ref[:, lo:mid] = (q_ref[:, hs].astype(F32) * (FOX_DH ** -0.5)).astype(MXU)
            qa_ref[:, mid:lo + HW] = _aug(tt, cp + ones).astype(MXU)
            ka_ref[:, lo:mid] = k_ref[:, hs]
            ka_ref[:, mid:lo + HW] = _aug(tt, ones + [-x for x in cp] + ones).astype(MXU)
            va_ref[:, lo:mid] = v_ref[:, hs]
            va_ref[:, mid:lo + HW] = _aug(tt, ones).astype(MXU)

    wide = pl.BlockSpec((tt, FOX_H * HW), lambda i: (i, 0))
    col = lambda k: pl.BlockSpec((tt, D), lambda i: (i, k))
    return _call(body, name="fox_prep", grid=(T // tt,),
                 in_specs=[col(0), col(1), col(2), pl.BlockSpec((tt, 128), lambda i: (i, 0))],
                 out_specs=[wide, wide, wide], out_shape=[_sds((T, FOX_H * HW), MXU)] * 3)(p, p, p, c)


def _causal(x, fill):
    r = lax.broadcasted_iota(jnp.int32, x.shape, 0)
    c = lax.broadcasted_iota(jnp.int32, x.shape, 1)
    return jnp.where(r >= c, x, fill)


def _wide_specs(tq, nq, q_of, k_of):
    qs = pl.BlockSpec((tq, FOX_H * HW), lambda b, i, j: (b * nq + q_of(i, j), 0))
    ks = pl.BlockSpec((tq, FOX_H * HW), lambda b, i, j: (b * nq + k_of(i, j), 0))
    return qs, ks


def fox2_fwd(qa, ka, va, *, S, tq=256, tk=512):
    T = qa.shape[0]
    tk = min(tk, S)
    nq = S // tq
    nk = S // tk
    r = tk // tq

    def body(q_ref, k_ref, v_ref, o_ref, o32_ref, lse_ref, m_sc, acc, acc_lo):
        qi = pl.program_id(1)
        ki = pl.program_id(2)
        last = qi // r

        @pl.when(ki == 0)
        def _():
            m_sc[...] = jnp.full_like(m_sc, NEG)
            acc[...] = jnp.zeros_like(acc)
            acc_lo[...] = jnp.zeros_like(acc_lo)

        def step(diag):
            for h in range(FOX_H):
                ws = slice(h * HW, (h + 1) * HW)
                s = _dot_nt(q_ref[:, ws], k_ref[:, ws])
                if diag:
                    row = qi * tq + lax.broadcasted_iota(jnp.int32, s.shape, 0)
                    col = ki * tk + lax.broadcasted_iota(jnp.int32, s.shape, 1)
                    s = jnp.where(row >= col, s, NEG)
                m_prev = m_sc[h]
                m_new = jnp.maximum(m_prev, jnp.max(s, axis=-1, keepdims=True))
                alpha = jnp.exp(m_prev - m_new)
                hi, lo = _split(jnp.exp(s - jnp.tile(m_new, (1, tk // 128))))
                acc[h] = alpha * acc[h] + _dot(hi, v_ref[:, ws])
                acc_lo[h] = alpha * acc_lo[h] + _dot(lo, v_ref[:, ws])
                m_sc[h] = m_new

        @pl.when(ki < last)
        def _():
            step(False)

        @pl.when(ki == last)
        def _():
            step(True)
            for h in range(FOX_H):
                hs = slice(h * FOX_DH, (h + 1) * FOX_DH)
                full = acc[h] + acc_lo[h]
                l = full[:, FOX_DH:FOX_DH + 1]
                o_ref[:, hs] = (acc[h][:, :FOX_DH] / l).astype(MXU)
                o32_ref[:, hs] = full[:, :FOX_DH] / l
                lse_ref[:, h:h + 1] = m_sc[h][:, 0:1] + jnp.log(l)

    qs = pl.BlockSpec((tq, FOX_H * HW), lambda b, i, j: (b * nq + i, 0))
    ks = pl.BlockSpec((tk, FOX_H * HW), lambda b, i, j: (b * nk + jnp.minimum(j, i // r), 0))
    orow = pl.BlockSpec((tq, D), lambda b, i, j: (b * nq + i, 0))
    return _call(
        body, name="fox_fwd", grid=(T // S, nq, nk),
        in_specs=[qs, ks, ks],
        out_specs=[orow, orow, pl.BlockSpec((tq, 128), lambda b, i, j: (b * nq + i, 0))],
        out_shape=[_sds((T, D), MXU), _sds((T, D), F32), _sds((T, 128), F32)],
        scratch=[pltpu.VMEM((FOX_H, tq, 128), F32), pltpu.VMEM((FOX_H, tq, HW), F32),
                 pltpu.VMEM((FOX_H, tq, HW), F32)],
    )(qa, ka, va)


def fox2_prep_bwd(do, o32, lse, qa, *, tt=256):
    T = do.shape[0]

    def body(do_ref, o_ref, lse_ref, qa_ref, qb_ref, da_ref):
        lane = lax.broadcasted_iota(jnp.int32, (tt, HW), 1)
        dob = do_ref[...].astype(MXU)
        for h in range(FOX_H):
            hs = slice(h * FOX_DH, (h + 1) * FOX_DH)
            ws = slice(h * HW, (h + 1) * HW)
            doh = dob[:, hs]
            delta = jnp.sum(doh.astype(F32) * o_ref[:, hs], axis=-1, keepdims=True)
            da_ref[:, h * HW:h * HW + FOX_DH] = doh
            da_ref[:, h * HW + FOX_DH:(h + 1) * HW] = _aug(tt, [-x for x in _parts(delta)]).astype(MXU)
            tile = qa_ref[:, ws]
            for i, part in enumerate(_parts(lse_ref[:, h:h + 1])):
                tile = jnp.where(lane == COL_LSE + i, (-part).astype(MXU), tile)
            qb_ref[:, ws] = tile

    wide = pl.BlockSpec((tt, FOX_H * HW), lambda i: (i, 0))
    blk = pl.BlockSpec((tt, D), lambda i: (i, 0))
    return _call(body, name="fox_prep_bwd", grid=(T // tt,),
                 in_specs=[blk, blk, pl.BlockSpec((tt, 128), lambda i: (i, 0)), wide],
                 out_specs=[wide, wide],
                 out_shape=[_sds((T, FOX_H * HW), MXU), _sds((T, FOX_H * HW), MXU)],
                 )(do, o32, lse, qa)


def fox2_dq(qb, ka, va, da, *, S, tq=256):
    T = qb.shape[0]
    nq = S // tq

    def body(q_ref, k_ref, v_ref, d_ref, dq_ref, acc):
        qi = pl.program_id(1)
        ki = pl.program_id(2)

        @pl.when(ki == 0)
        def _():
            acc[...] = jnp.zeros_like(acc)

        def step(diag):
            for h in range(FOX_H):
                ws = slice(h * HW, (h + 1) * HW)
                kh = k_ref[:, ws]
                p = jnp.exp(_dot_nt(q_ref[:, ws], kh))
                if diag:
                    p = _causal(p, 0.0)
                hi, lo = _split(p * _dot_nt(d_ref[:, ws], v_ref[:, ws]))
                acc[h] += _dot(hi, kh) + _dot(lo, kh)

        @pl.when(ki < qi)
        def _():
            step(False)

        @pl.when(ki == qi)
        def _():
            step(True)
            for h in range(FOX_H):
                dq_ref[:, h * FOX_DH:(h + 1) * FOX_DH] = (acc[h][:, :FOX_DH] * (FOX_DH ** -0.5)).astype(MXU)

    qs, ks = _wide_specs(tq, nq, lambda i, j: i, lambda i, j: jnp.minimum(i, j))
    return _call(
        body, name="fox_bwd_dq", grid=(T // S, nq, nq),
        in_specs=[qs, ks, ks, qs],
        out_specs=pl.BlockSpec((tq, D), lambda b, i, j: (b * nq + i, 0)),
        out_shape=_sds((T, D), MXU),
        scratch=[pltpu.VMEM((FOX_H, tq, HW), F32)],
    )(qb, ka, va, da)


def fox2_dkv(qb, ka, va, da, *, S, tq=256):
    T = qb.shape[0]
    nq = S // tq

    def body(q_ref, k_ref, v_ref, d_ref, dk_ref, dv_ref, dck_ref, dk_acc, dv_acc):
        ki = pl.program_id(1)
        qi = pl.program_id(2)

        @pl.when(qi == 0)
        def _():
            dk_acc[...] = jnp.zeros_like(dk_acc)
            dv_acc[...] = jnp.zeros_like(dv_acc)

        def step(diag):
            for h in range(FOX_H):
                ws = slice(h * HW, (h + 1) * HW)
                qh = q_ref[:, ws]
                dh = d_ref[:, ws]
                pt = jnp.exp(_dot_nt(k_ref[:, ws], qh))
                if diag:
                    r = lax.broadcasted_iota(jnp.int32, pt.shape, 0)
                    c = lax.broadcasted_iota(jnp.int32, pt.shape, 1)
                    pt = jnp.where(r <= c, pt, 0.0)
                dv_acc[h] += _dot(pt.astype(MXU), dh)
                hi, lo = _split(pt * _dot_nt(v_ref[:, ws], dh))
                dk_acc[h] += _dot(hi, qh) + _dot(lo, qh)

        @pl.when(qi > ki)
        def _():
            step(False)

        @pl.when(qi == ki)
        def _():
            step(True)

        @pl.when(qi == nq - 1)
        def _():
            dck_ref[...] = jnp.zeros_like(dck_ref)
            for h in range(FOX_H):
                hs = slice(h * FOX_DH, (h + 1) * FOX_DH)
                dk_ref[:, hs] = dk_acc[h][:, :FOX_DH].astype(MXU)
                dv_ref[:, hs] = dv_acc[h][:, :FOX_DH].astype(MXU)
                dck_ref[:, h:h + 1] = dk_acc[h][:, COL_ONE:COL_ONE + 1]

    qs, ks = _wide_specs(tq, nq, lambda i, j: jnp.maximum(i, j), lambda i, j: i)
    ko = pl.BlockSpec((tq, D), lambda b, i, j: (b * nq + i, 0))
    return _call(
        body, name="fox_bwd_dkv", grid=(T // S, nq, nq),
        in_specs=[qs, ks, ks, qs],
        out_specs=[ko, ko, pl.BlockSpec((tq, 128), lambda b, i, j: (b * nq + i, 0))],
        out_shape=[_sds((T, D), MXU), _sds((T, D), MXU), _sds((T, 128), F32)],
        scratch=[pltpu.VMEM((FOX_H, tq, HW), F32), pltpu.VMEM((FOX_H, tq, HW), F32)],
    )(qb, ka, va, da)


def fox_fin(dck, fl, bf, *, S, tt=256):
    T = fl.shape[0]
    nb = S // tt
    nblk = T // tt

    def body(dck_ref, fl_ref, bf_ref, dfl_ref, dbf_ref, carry):
        i = pl.program_id(0)

        @pl.when(i == 0)
        def _():
            dbf_ref[...] = jnp.zeros_like(dbf_ref)

        @pl.when((i % nb) == 0)
        def _():
            carry[...] = jnp.zeros_like(carry)

        lane = lax.broadcasted_iota(jnp.int32, (tt, 128), 1)
        dc = jnp.where(lane < FOX_H, -dck_ref[...], 0.0)
        dlf = _dot_hi(_tri(tt, upper=True), dc) + carry[...]
        carry[...] = dlf[0:1, :]
        dfl = dlf * _sigmoid(-(fl_ref[...] + bf_ref[...]))
        dfl_ref[...] = dfl.astype(MXU)
        dbf_ref[...] += jnp.sum(dfl, axis=0, keepdims=True)

    rev = pl.BlockSpec((tt, 128), lambda i: (nblk - 1 - i, 0))
    row = pl.BlockSpec((1, 128), lambda i: (0, 0))
    return _call(
        body, name="fox_fin", grid=(nblk,),
        in_specs=[rev, rev, row],
        out_specs=[rev, row],
        out_shape=[_sds((T, 128), MXU), _sds((1, 128), F32)],
        scratch=[pltpu.VMEM((1, 128), F32)],
    )(dck, fl, bf)


def lb_fwd(logits):
    def body(l_ref, lb_ref):
        lv = l_ref[...]
        e = jnp.exp(lv - jnp.max(lv, axis=0, keepdims=True))
        p = e / jnp.sum(e, axis=0, keepdims=True)
        lb_ref[...] = p[1:2, :] + p[2:3, :]

    return _call(body, name="lb_fwd", grid=(1,),
                 in_specs=[pl.BlockSpec((DEPTH, D), lambda i: (0, 0))],
                 out_specs=pl.BlockSpec((1, D), lambda i: (0, 0)),
                 out_shape=_sds((1, D), F32))(logits)


def lb_bwd(logits, dlb):
    def body(l_ref, d_ref, o_ref):
        lv = l_ref[...]
        e = jnp.exp(lv - jnp.max(lv, axis=0, keepdims=True))
        p = e / jnp.sum(e, axis=0, keepdims=True)
        lb = p[1:2, :] + p[2:3, :]
        row = lax.broadcasted_iota(jnp.int32, (DEPTH, D), 0)
        sel = ((row == 1) | (row == 2)).astype(F32)
        o_ref[...] = p * (sel - lb) * d_ref[...]

    return _call(body, name="lb_bwd", grid=(1,),
                 in_specs=[pl.BlockSpec((DEPTH, D), lambda i: (0, 0)), pl.BlockSpec((1, D), lambda i: (0, 0))],
                 out_specs=pl.BlockSpec((DEPTH, D), lambda i: (0, 0)),
                 out_shape=_sds((DEPTH, D), F32))(logits, dlb)


def _hgrn_gates(qr, fr, lb):
    sg = _sigmoid(fr)
    sneg = _sigmoid(-fr)
    f = lb + (1.0 - lb) * sg
    kk = (1.0 - lb) * sneg
    G = _dot_hi(_tri(HG_C), jnp.log(f))
    eG = jnp.exp(G)
    einv = jnp.exp(-G)
    elast = jnp.exp(G[HG_C - 1:HG_C, :] - G)
    q = qr * _sigmoid(qr)
    return dict(q=q, kk=kk, f=f, sg=sg, sneg=sneg, eG=eG, einv=einv, elast=elast,
                qg=q * eG, kinv=kk * einv, khat=kk * elast, glast=jnp.exp(G[HG_C - 1:HG_C, :]))


def _tril_mask(x):
    r = lax.broadcasted_iota(jnp.int32, x.shape, 0)
    c = lax.broadcasted_iota(jnp.int32, x.shape, 1)
    return jnp.where(r >= c, x, 0.0)


def hgrn_fwd(p, lb, ng, *, S, R=256):
    T = p.shape[0]
    R = min(R, S)
    nr = S // R
    ncr = R // HG_C

    def body(q_ref, f_ref, v_ref, gt_ref, lb_ref, ng_ref, y_ref, o_ref, st_ref, st):
        @pl.when(pl.program_id(1) == 0)
        def _():
            st[...] = jnp.zeros_like(st)

        lbv = lb_ref[...]
        for ch in range(ncr):
            rows = slice(ch * HG_C, (ch + 1) * HG_C)
            gt = _hgrn_gates(q_ref[rows, :], f_ref[rows, :], lbv)
            for h in range(HG_H):
                hs = slice(h * HG_DK, (h + 1) * HG_DK)
                sp = st[h]
                st_ref[ch, h] = sp
                qg = gt["qg"][:, hs].astype(MXU)
                vh = v_ref[rows, hs].astype(MXU)
                A = _tril_mask(_dot_nt(qg, gt["kinv"][:, hs].astype(MXU)))
                o_ref[rows, hs] = _dot_nt(qg, sp.astype(MXU)) + _dot(A.astype(MXU), vh)
                st[h] = sp * gt["glast"][:, hs] + _dot_tn(vh, gt["khat"][:, hs].astype(MXU))
        gate = gt_ref[...]
        sgate = gate * _sigmoid(gate)
        for h in range(HG_H):
            hs = slice(h * HG_DK, (h + 1) * HG_DK)
            oh = o_ref[:, hs]
            r = lax.rsqrt(jnp.mean(oh * oh, axis=-1, keepdims=True) + RMS_EPS)
            y_ref[:, hs] = (oh * r * ng_ref[:, hs] * sgate[:, hs]).astype(MXU)

    col = lambda c: pl.BlockSpec((R, D), lambda b, i: (b * nr + i, c))
    row = pl.BlockSpec((1, D), lambda b, i: (0, 0))
    return _call(
        body, name="hgrn_fwd", grid=(T // S, nr),
        in_specs=[col(0), col(1), col(2), col(3), row, row],
        out_specs=[col(0), col(0),
                   pl.BlockSpec((ncr, HG_H, HG_DK, HG_DK), lambda b, i: (b * nr + i, 0, 0, 0))],
        out_shape=[_sds((T, D), MXU), _sds((T, D), F32), _sds((T // HG_C, HG_H, HG_DK, HG_DK), F32)],
        scratch=[pltpu.VMEM((HG_H, HG_DK, HG_DK), F32)],
    )(p, p, p, p, lb, ng)


def hgrn_bwd(p, o, dyo, states, lb, ng, *, S, R=256):
    T = p.shape[0]
    R = min(R, S)
    nr = S // R
    ncr = R // HG_C

    def body(q_ref, f_ref, v_ref, gt_ref, o_ref, dy_ref, st_ref, lb_ref, ng_ref,
             dp_ref, dlb_ref, dng_ref, dst, do_buf, dG_buf, dqb, dkb):
        b = pl.program_id(0)
        i = pl.program_id(1)

        @pl.when(i == 0)
        def _():
            dst[...] = jnp.zeros_like(dst)

        @pl.when((b == 0) & (i == 0))
        def _():
            dlb_ref[...] = jnp.zeros_like(dlb_ref)
            dng_ref[...] = jnp.zeros_like(dng_ref)

        lbv = lb_ref[...]
        gate = gt_ref[...]
        sg_gate = _sigmoid(gate)
        silu_gate = gate * sg_gate
        for h in range(HG_H):
            hs = slice(h * HG_DK, (h + 1) * HG_DK)
            oh = o_ref[:, hs]
            r = lax.rsqrt(jnp.mean(oh * oh, axis=-1, keepdims=True) + RMS_EPS)
            ohat = oh * r
            dyh = dy_ref[:, hs]
            ngh = ng_ref[:, hs]
            dng_ref[:, hs] += jnp.sum(dyh * silu_gate[:, hs] * ohat, axis=0, keepdims=True)
            dp_ref[:, 3 * D + h * HG_DK:3 * D + (h + 1) * HG_DK] = (
                dyh * ohat * ngh * (sg_gate[:, hs] * (1.0 + gate[:, hs] * (1.0 - sg_gate[:, hs])))).astype(MXU)
            dn = dyh * ngh * silu_gate[:, hs]
            do_buf[:, hs] = r * (dn - ohat * jnp.mean(dn * ohat, axis=-1, keepdims=True))

        lastrow = lax.broadcasted_iota(jnp.int32, (HG_C, HG_DK), 0) == HG_C - 1
        for ch in reversed(range(ncr)):
            rows = slice(ch * HG_C, (ch + 1) * HG_C)
            qr = q_ref[rows, :]
            gt = _hgrn_gates(qr, f_ref[rows, :], lbv)
            for h in range(HG_H):
                hs = slice(h * HG_DK, (h + 1) * HG_DK)
                sp = st_ref[ch, h]
                ds = dst[h]
                qg32, kinv32, khat32 = gt["qg"][:, hs], gt["kinv"][:, hs], gt["khat"][:, hs]
                qg, kinv, khat = qg32.astype(MXU), kinv32.astype(MXU), khat32.astype(MXU)
                vh = v_ref[rows, hs].astype(MXU)
                doh = do_buf[rows, hs].astype(MXU)
                dsb = ds.astype(MXU)
                A = _tril_mask(_dot_nt(qg, kinv)).astype(MXU)
                dA = _tril_mask(_dot_nt(doh, vh)).astype(MXU)
                dqg = _dot(doh, sp.astype(MXU)) + _dot(dA, kinv)
                dkinv = _dot_tn(dA, qg)
                dp_ref[rows, 2 * D + h * HG_DK:2 * D + (h + 1) * HG_DK] = (
                    _dot_tn(A, doh) + _dot_nt(khat, dsb)).astype(MXU)
                dkhat = _dot(vh, dsb)
                glast = gt["glast"][:, hs]
                qg32, kinv32, khat32 = qg.astype(F32), kinv.astype(F32), khat.astype(F32)
                extra = (glast * jnp.sum(dsb.astype(F32) * sp.astype(MXU).astype(F32), axis=0, keepdims=True)
                         + jnp.sum(dkhat * khat32, axis=0, keepdims=True))
                dst[h] = ds * glast + _dot_tn(doh, qg)
                dG = dqg * qg32 - dkinv * kinv32 - dkhat * khat32
                dG_buf[:, hs] = dG + jnp.where(lastrow, extra, 0.0)
                dqb[:, hs] = dqg * gt["eG"][:, hs]
                dkb[:, hs] = dkinv * gt["einv"][:, hs] + dkhat * gt["elast"][:, hs]
            dg = _dot_hi(_tri(HG_C, upper=True), dG_buf[...])
            dk = dkb[...]
            sneg, f = gt["sneg"], gt["f"]
            c1 = (1.0 - lbv) * gt["sg"] * sneg
            dp_ref[rows, D:2 * D] = (dg * c1 / f - dk * c1).astype(MXU)
            dlb_ref[...] += jnp.sum(dg * sneg / f - dk * sneg, axis=0, keepdims=True)
            sq = _sigmoid(qr)
            dp_ref[rows, 0:D] = (dqb[...] * (sq * (1.0 + qr * (1.0 - sq)))).astype(MXU)

    rev = lambda b, i: b * nr + nr - 1 - i
    col = lambda c: pl.BlockSpec((R, D), lambda b, i: (rev(b, i), c))
    row = pl.BlockSpec((1, D), lambda b, i: (0, 0))
    return _call(
        body, name="hgrn_bwd", grid=(T // S, nr),
        in_specs=[col(0), col(1), col(2), col(3), col(0), col(0),
                  pl.BlockSpec((ncr, HG_H, HG_DK, HG_DK), lambda b, i: (rev(b, i), 0, 0, 0)), row, row],
        out_specs=[pl.BlockSpec((R, 4 * D), lambda b, i: (rev(b, i), 0)), row, row],
        out_shape=[_sds((T, 4 * D), MXU), _sds((1, D), F32), _sds((1, D), F32)],
        scratch=[pltpu.VMEM((HG_H, HG_DK, HG_DK), F32), pltpu.VMEM((R, D), F32), pltpu.VMEM((HG_C, D), F32),
                 pltpu.VMEM((HG_C, D), F32), pltpu.VMEM((HG_C, D), F32)],
    )(p, p, p, p, o, dyo, states, lb, ng)


def _mm_tn_stack(a, b, *, name, G, M, N, tk, stack):
    E, e, buf = stack
    T = a.shape[-2]

    def spec(arr, width):
        if arr.ndim == 3:
            return pl.BlockSpec((None, tk, width), lambda g, k: (g, k, 0))
        return pl.BlockSpec((tk, width), lambda g, k: (k, 0))

    def body(*refs):
        a_ref, b_ref, o_ref = refs[0], refs[1], refs[-1]

        @pl.when(pl.program_id(1) == 0)
        def _():
            o_ref[...] = jnp.zeros_like(o_ref)

        o_ref[...] += _dot_tn(a_ref[...], b_ref[...])

    in_specs = [spec(a, M), spec(b, N)]
    args = [a, b]
    aliases = {}
    if buf is not None:
        in_specs.append(pl.BlockSpec(memory_space=pl.ANY))
        args.append(buf)
        aliases = {2: 0}
    return pl.pallas_call(
        body, name=name, grid=(G, T // tk), in_specs=in_specs,
        out_specs=pl.BlockSpec((None, None, M, N), lambda g, k: (g, e, 0, 0)),
        out_shape=_sds((G, E, M, N), F32), input_output_aliases=aliases,
        compiler_params=pltpu.CompilerParams(dimension_semantics=("arbitrary", "arbitrary"),
                                             vmem_limit_bytes=VMEM_LIMIT))(*args)


MESH = pl.DeviceIdType.MESH
ANY = pl.BlockSpec(memory_space=pl.ANY)


def _pos():
    return lax.axis_index("x"), lax.axis_index("y"), lax.axis_index("c")


def _other_chips(x, y):
    return [(1 - x, y), (x, 1 - y), (1 - x, 1 - y)]


def _comm_call(body, *, name, args, out_shape, n_sem):
    return pl.pallas_call(
        body, name=name, in_specs=[ANY] * len(args), out_specs=[ANY] * len(out_shape), out_shape=out_shape,
        scratch_shapes=[pltpu.SemaphoreType.DMA((n_sem,)), pltpu.SemaphoreType.DMA((n_sem,)),
                        pltpu.SemaphoreType.DMA((len(args),))],
    )(*args)


def all_gather_chips(xs):
    n = len(xs)

    def body(*refs):
        x_refs, o_refs = refs[:n], refs[n:2 * n]
        ssem, rsem, lsem = refs[2 * n:]
        x, y, c = _pos()
        me = 2 * x + y
        chips = _other_chips(x, y)
        sib = (x, y, 1 - c)

        def rc(src, dst, idx, dev):
            return pltpu.make_async_remote_copy(src_ref=src, dst_ref=dst, send_sem=ssem.at[idx], recv_sem=rsem.at[idx],
                                                device_id=dev, device_id_type=MESH)

        via = jnp.where(c == 0, 2 * (1 - x) + y, 2 * x + (1 - y))
        to = (jnp.where(c == 0, x, 1 - x), jnp.where(c == 0, 1 - y, y), c)
        local, started = [], []
        for t in range(n):
            hr = xs[t].shape[0] // 2
            mine = pl.ds(c * hr, hr)
            cp = pltpu.make_async_copy(x_refs[t], o_refs[t].at[me], lsem.at[t])
            cp.start()
            local.append(cp)
            for k, (cx, cy) in enumerate(chips[:2]):
                cp = rc(x_refs[t].at[mine], o_refs[t].at[me, mine], 6 * t + k, (cx, cy, c))
                cp.start()
                started.append(cp)
        for t in range(n):
            hr = xs[t].shape[0] // 2
            mine = pl.ds(c * hr, hr)
            for k, (cx, cy) in enumerate(chips[:2]):
                landed = o_refs[t].at[2 * cx + cy, mine]
                rc(landed, landed, 6 * t + k, (cx, cy, c)).wait_recv()
            passed = o_refs[t].at[via, mine]
            cp = rc(passed, passed, 6 * t + 2, to)
            cp.start()
            started.append(cp)
            for k, (cx, cy) in enumerate(chips[:2]):
                landed = o_refs[t].at[2 * cx + cy, mine]
                cp = rc(landed, landed, 6 * t + 3 + k, sib)
                cp.start()
                started.append(cp)
        for t in range(n):
            hr = xs[t].shape[0] // 2
            mine = pl.ds(c * hr, hr)
            cx, cy = chips[2]
            landed = o_refs[t].at[2 * cx + cy, mine]
            rc(landed, landed, 6 * t + 2, to).wait_recv()
            cp = rc(landed, landed, 6 * t + 5, sib)
            cp.start()
            started.append(cp)
        for t in range(n):
            hr = xs[t].shape[0] // 2
            theirs = pl.ds((1 - c) * hr, hr)
            for k, (cx, cy) in enumerate(chips):
                other = o_refs[t].at[2 * cx + cy, theirs]
                rc(other, other, 6 * t + 3 + k, sib).wait_recv()
        for cp in started:
            cp.wait_send()
        for cp in local:
            cp.wait()

    outs = _comm_call(body, name="all_gather_chips", args=list(xs),
                      out_shape=[_sds((NSH,) + a.shape, a.dtype) for a in xs], n_sem=6 * n)
    return list(outs)


def sibling_half_exchange(gs):
    n = len(gs)

    def body(*refs):
        g_refs, o_refs = refs[:n], refs[n:2 * n]
        ssem, rsem, _ = refs[2 * n:]
        x, y, c = _pos()
        cps = []
        for t in range(n):
            hr = gs[t].shape[1] // 2
            for j in range(NSH):
                cp = pltpu.make_async_remote_copy(
                    src_ref=g_refs[t].at[j, pl.ds((1 - c) * hr, hr)], dst_ref=o_refs[t].at[j],
                    send_sem=ssem.at[NSH * t + j], recv_sem=rsem.at[NSH * t + j],
                    device_id=(x, y, 1 - c), device_id_type=MESH)
                cp.start()
                cps.append(cp)
        for cp in cps:
            cp.wait()

    outs = _comm_call(body, name="sibling_half_exchange", args=list(gs),
                      out_shape=[_sds((NSH, g.shape[1] // 2, g.shape[2]), g.dtype) for g in gs], n_sem=NSH * n)
    return list(outs)


def chip_scatter(ss):
    n = len(ss)

    def body(*refs):
        s_refs, o_refs = refs[:n], refs[n:2 * n]
        ssem, rsem, _ = refs[2 * n:]
        x, y, c = _pos()
        cps = []
        for t in range(n):
            for k, (cx, cy) in enumerate(_other_chips(x, y)):
                cp = pltpu.make_async_remote_copy(
                    src_ref=s_refs[t].at[2 * cx + cy], dst_ref=o_refs[t].at[k],
                    send_sem=ssem.at[3 * t + k], recv_sem=rsem.at[3 * t + k],
                    device_id=(cx, cy, c), device_id_type=MESH)
                cp.start()
                cps.append(cp)
        for cp in cps:
            cp.wait()

    outs = _comm_call(body, name="chip_scatter", args=list(ss),
                      out_shape=[_sds((3,) + s.shape[1:], s.dtype) for s in ss], n_sem=3 * n)
    return list(outs)


def sibling_exchange(rs):
    n = len(rs)

    def body(*refs):
        r_refs, o_refs = refs[:n], refs[n:2 * n]
        ssem, rsem, _ = refs[2 * n:]
        x, y, c = _pos()
        cps = []
        for t in range(n):
            cp = pltpu.make_async_remote_copy(
                src_ref=r_refs[t], dst_ref=o_refs[t], send_sem=ssem.at[t], recv_sem=rsem.at[t],
                device_id=(x, y, 1 - c), device_id_type=MESH)
            cp.start()
            cps.append(cp)
        for cp in cps:
            cp.wait()

    outs = _comm_call(body, name="sibling_exchange", args=list(rs),
                      out_shape=[_sds(r.shape, r.dtype) for r in rs], n_sem=n)
    return list(outs)


def all_gather_devices(v):
    def body(v_ref, o_ref, ssem, rsem, lsem):
        x, y, c = _pos()
        me = 4 * x + 2 * y + c
        loc = pltpu.make_async_copy(v_ref, o_ref.at[me], lsem.at[0])
        loc.start()
        cps = []
        k = 0
        for fx in range(2):
            for fy in range(2):
                for fc in range(2):
                    if fx == fy == fc == 0:
                        continue
                    cp = pltpu.make_async_remote_copy(
                        src_ref=v_ref, dst_ref=o_ref.at[me], send_sem=ssem.at[k], recv_sem=rsem.at[k],
                        device_id=(x ^ fx, y ^ fy, c ^ fc), device_id_type=MESH)
                    cp.start()
                    src = 4 * (x ^ fx) + 2 * (y ^ fy) + (c ^ fc)
                    cps.append((cp, o_ref.at[src], k))
                    k += 1
        for cp, landed, k in cps:
            cp.wait_send()
            pltpu.make_async_remote_copy(
                src_ref=landed, dst_ref=landed, send_sem=ssem.at[k], recv_sem=rsem.at[k],
                device_id=(x, y, c), device_id_type=MESH).wait_recv()
        loc.wait()

    return _comm_call(body, name="all_gather_devices", args=[v],
                      out_shape=[_sds((8,) + v.shape, v.dtype)], n_sem=7)[0]


def _call_sp(body, *, name, grid, in_specs, out_specs, out_shape, pos, args):
    return pl.pallas_call(
        body, name=name,
        grid_spec=pltpu.PrefetchScalarGridSpec(num_scalar_prefetch=1, grid=grid, in_specs=in_specs,
                                               out_specs=out_specs),
        out_shape=out_shape,
        compiler_params=pltpu.CompilerParams(dimension_semantics=("arbitrary",) * len(grid),
                                             vmem_limit_bytes=VMEM_LIMIT))(pos, *args)


def _rows_tile(r):
    for t in (512, 256, 128, 64, 32, 16, 8):
        if r % t == 0:
            return t
    raise ValueError(r)


def pair_sum(g, r, pos):
    _, R, C = g.shape
    hr = R // 2
    tr = _rows_tile(hr)
    nbh = hr // tr

    def body(p_ref, g_ref, r_ref, o_ref):
        o_ref[...] = (g_ref[...] + r_ref[...]).astype(MXU)

    return _call_sp(
        body, name="pair_sum", grid=(NSH, nbh), pos=pos, args=[g, r],
        in_specs=[pl.BlockSpec((None, tr, C), lambda j, i, p: (j, p[0] * nbh + i, 0)),
                  pl.BlockSpec((None, tr, C), lambda j, i, p: (j, i, 0))],
        out_specs=pl.BlockSpec((None, tr, C), lambda j, i, p: (j, i, 0)),
        out_shape=_sds((NSH, hr, C), MXU))


def reduce_own(g, r_sib, r_ici, pos):
    _, R, C = g.shape
    hr = R // 2
    tr = _rows_tile(hr)
    nbh = hr // tr

    def body(p_ref, g_ref, rs_ref, ri_ref, o_ref):
        s = g_ref[...] + rs_ref[...]
        for k in range(3):
            s = s + ri_ref[k].astype(F32)
        o_ref[...] = s

    return _call_sp(
        body, name="reduce_own", grid=(nbh,), pos=pos, args=[g, r_sib, r_ici],
        in_specs=[pl.BlockSpec((None, tr, C), lambda i, p: (p[1], p[0] * nbh + i, 0)),
                  pl.BlockSpec((None, tr, C), lambda i, p: (p[1], i, 0)),
                  pl.BlockSpec((3, tr, C), lambda i, p: (0, i, 0))],
        out_specs=pl.BlockSpec((tr, C), lambda i, p: (i, 0)),
        out_shape=_sds((hr, C), F32))


def _adamw_math(w, g, m, v):
    m = ADAM_B1 * m + (1.0 - ADAM_B1) * g
    v = ADAM_B2 * v + (1.0 - ADAM_B2) * (g * g)
    m_hat = m / (1.0 - ADAM_B1 ** ADAM_STEP)
    v_hat = v / (1.0 - ADAM_B2 ** ADAM_STEP)
    delta = -ADAM_LR * (m_hat / (jnp.sqrt(v_hat) + ADAM_EPS) + ADAM_WD * w)
    return delta, m, v


def adamw_halves(w, m, v, ga, gb, pos):
    R, C = w.shape
    hr = R // 2
    tr = _rows_tile(hr)
    nbh = hr // tr

    def body(p_ref, w_ref, m_ref, v_ref, ga_ref, gb_ref, g_ref, d_ref, mo_ref, vo_ref):
        mine = (pl.program_id(0) // nbh) == p_ref[0]
        g = jnp.where(mine, ga_ref[...], gb_ref[...])
        g_ref[...] = g
        d_ref[...], mo_ref[...], vo_ref[...] = _adamw_math(w_ref[...], g, m_ref[...], v_ref[...])

    blk = pl.BlockSpec((tr, C), lambda i, p: (i, 0))
    return _call_sp(
        body, name="adamw_halves", grid=(R // tr,), pos=pos, args=[w, m, v, ga, gb],
        in_specs=[blk, blk, blk,
                  pl.BlockSpec((tr, C), lambda i, p: (jnp.where(i // nbh == p[0], i % nbh, 0), 0)),
                  pl.BlockSpec((tr, C), lambda i, p: (jnp.where(i // nbh == p[0], 0, i % nbh), 0))],
        out_specs=[blk, blk, blk, blk],
        out_shape=[_sds((R, C), F32)] * 4)


def adamw_sum(gall, w, m, v):
    n, R, C = gall.shape

    def body(ga_ref, w_ref, m_ref, v_ref, g_ref, d_ref, mo_ref, vo_ref):
        g = ga_ref[0]
        for k in range(1, n):
            g = g + ga_ref[k]
        g_ref[...] = g
        d_ref[...], mo_ref[...], vo_ref[...] = _adamw_math(w_ref[...], g, m_ref[...], v_ref[...])

    blk = pl.BlockSpec((R, C), lambda i: (0, 0))
    return _call(body, name="adamw_sum", grid=(1,),
                 in_specs=[pl.BlockSpec((n, R, C), lambda i: (0, 0, 0)), blk, blk, blk],
                 out_specs=[blk, blk, blk, blk], out_shape=[_sds((R, C), F32)] * 4)(gall, w, m, v)


_WEIGHTS = ['ffn_norm', 'ffn_w_gate', 'ffn_w_up', 'ffn_w_down', 'mix_norm', 'final_norm', 'conv_w_in', 'conv_b_in',
            'conv_dw', 'conv_dw_b', 'conv_ln_g', 'conv_ln_b', 'conv_w_out', 'fox_w_in', 'fox_b_f', 'fox_w_out',
            'hgrn_w_in', 'hgrn_lb_logits', 'hgrn_norm', 'hgrn_w_out', 'pool_w', 'pool_scale']
_BIG = ['ffn_w_gate', 'ffn_w_up', 'ffn_w_down', 'conv_w_in', 'conv_w_out', 'fox_w_in', 'fox_w_out',
        'hgrn_w_in', 'hgrn_w_out', 'pool_w']
_SHARDED_SMALL = ['ffn_norm', 'conv_dw', 'hgrn_norm', 'pool_scale']
_REPLICATED = ['mix_norm', 'final_norm', 'conv_b_in', 'conv_dw_b', 'conv_ln_g', 'conv_ln_b', 'fox_b_f', 'hgrn_lb_logits']
FOX_N = 3 * D + FOX_H
FOX_NP = 3200
QS = D // NSH


def _pad_rows(a, rows):
    return jnp.pad(a, ((0, rows - a.shape[0]), (0, 0)))


def _pack_sharded_small(get):
    return jnp.concatenate([get('ffn_norm').reshape(8, -1), _pad_rows(get('conv_dw')[0], 32),
                            get('hgrn_norm'), get('pool_scale'), jnp.zeros((6, get('pool_scale').shape[1]), F32)], axis=0)


def _pack_replicated(get):
    return jnp.concatenate([get('mix_norm'), get('final_norm').reshape(1, D), get('conv_b_in').reshape(2, D),
                            get('conv_dw_b'), get('conv_ln_g'), get('conv_ln_b'),
                            jnp.pad(get('fox_b_f'), ((0, 0), (0, D - FOX_H))), get('hgrn_lb_logits'),
                            jnp.zeros((9, D), F32)], axis=0)


def _unpack_replicated(p):
    return {'mix_norm': p[0:4], 'final_norm': p[4], 'conv_b_in': p[5:7].reshape(1, 2 * D), 'conv_dw_b': p[7:8],
            'conv_ln_g': p[8:9], 'conv_ln_b': p[9:10], 'fox_b_f': p[10:11, :FOX_H], 'hgrn_lb_logits': p[11:15]}


def _unpack_sharded_small(p):
    return {'ffn_norm': p[0:8].reshape(DEPTH, 2, -1), 'conv_dw': p[8:8 + CONV_W][None],
            'hgrn_norm': p[40:41], 'pool_scale': p[41:42]}


def kernel(x, ffn_norm, ffn_w_gate, ffn_w_up, ffn_w_down, mix_norm, final_norm, conv_w_in, conv_b_in, conv_dw, conv_dw_b, conv_ln_g, conv_ln_b, conv_w_out, fox_w_in, fox_b_f, fox_w_out, hgrn_w_in, hgrn_lb_logits, hgrn_norm, hgrn_w_out, pool_w, pool_scale, loss_target, m_ffn_norm, m_ffn_w_gate, m_ffn_w_up, m_ffn_w_down, m_mix_norm, m_final_norm, m_conv_w_in, m_conv_b_in, m_conv_dw, m_conv_dw_b, m_conv_ln_g, m_conv_ln_b, m_conv_w_out, m_fox_w_in, m_fox_b_f, m_fox_w_out, m_hgrn_w_in, m_hgrn_lb_logits, m_hgrn_norm, m_hgrn_w_out, m_pool_w, m_pool_scale, v_ffn_norm, v_ffn_w_gate, v_ffn_w_up, v_ffn_w_down, v_mix_norm, v_final_norm, v_conv_w_in, v_conv_b_in, v_conv_dw, v_conv_dw_b, v_conv_ln_g, v_conv_ln_b, v_conv_w_out, v_fox_w_in, v_fox_b_f, v_fox_w_out, v_hgrn_w_in, v_hgrn_lb_logits, v_hgrn_norm, v_hgrn_w_out, v_pool_w, v_pool_scale):
    W = dict(ffn_norm=ffn_norm, ffn_w_gate=ffn_w_gate, ffn_w_up=ffn_w_up, ffn_w_down=ffn_w_down, mix_norm=mix_norm, final_norm=final_norm, conv_w_in=conv_w_in, conv_b_in=conv_b_in, conv_dw=conv_dw, conv_dw_b=conv_dw_b, conv_ln_g=conv_ln_g, conv_ln_b=conv_ln_b, conv_w_out=conv_w_out, fox_w_in=fox_w_in, fox_b_f=fox_b_f, fox_w_out=fox_w_out, hgrn_w_in=hgrn_w_in, hgrn_lb_logits=hgrn_lb_logits, hgrn_norm=hgrn_norm, hgrn_w_out=hgrn_w_out, pool_w=pool_w, pool_scale=pool_scale)
    M = dict(ffn_norm=m_ffn_norm, ffn_w_gate=m_ffn_w_gate, ffn_w_up=m_ffn_w_up, ffn_w_down=m_ffn_w_down, mix_norm=m_mix_norm, final_norm=m_final_norm, conv_w_in=m_conv_w_in, conv_b_in=m_conv_b_in, conv_dw=m_conv_dw, conv_dw_b=m_conv_dw_b, conv_ln_g=m_conv_ln_g, conv_ln_b=m_conv_ln_b, conv_w_out=m_conv_w_out, fox_w_in=m_fox_w_in, fox_b_f=m_fox_b_f, fox_w_out=m_fox_w_out, hgrn_w_in=m_hgrn_w_in, hgrn_lb_logits=m_hgrn_lb_logits, hgrn_norm=m_hgrn_norm, hgrn_w_out=m_hgrn_w_out, pool_w=m_pool_w, pool_scale=m_pool_scale)
    V = dict(ffn_norm=v_ffn_norm, ffn_w_gate=v_ffn_w_gate, ffn_w_up=v_ffn_w_up, ffn_w_down=v_ffn_w_down, mix_norm=v_mix_norm, final_norm=v_final_norm, conv_w_in=v_conv_w_in, conv_b_in=v_conv_b_in, conv_dw=v_conv_dw, conv_dw_b=v_conv_dw_b, conv_ln_g=v_conv_ln_g, conv_ln_b=v_conv_ln_b, conv_w_out=v_conv_w_out, fox_w_in=v_fox_w_in, fox_b_f=v_fox_b_f, fox_w_out=v_fox_w_out, hgrn_w_in=v_hgrn_w_in, hgrn_lb_logits=v_hgrn_lb_logits, hgrn_norm=v_hgrn_norm, hgrn_w_out=v_hgrn_w_out, pool_w=v_pool_w, pool_scale=v_pool_scale)

    px, py, pc = _pos()
    jme = 2 * px + py
    pos = jnp.stack([pc, jme]).astype(jnp.int32)
    S = x.shape[1]
    T = x.shape[0] * S
    x2 = x.reshape(T, D)
    tgt = loss_target.reshape(T, D)

    flat = lambda a: a.reshape(-1, a.shape[-1])
    gathered = all_gather_chips([flat(W[n]).astype(MXU) for n in _BIG] + [_pack_sharded_small(W.get)])
    G = dict(zip(_BIG, gathered[:-1]))
    small = gathered[-1].transpose(1, 0, 2).reshape(48, D)
    ffn_norm_f, conv_dw_f = small[0:8], small[8:40]
    hgrn_norm_f, pool_scale_f = small[40:41], small[41:42]
    wg_all = G['ffn_w_gate'].reshape(NSH, 2 * DEPTH, D, FS)
    wu_all = G['ffn_w_up'].reshape(NSH, 2 * DEPTH, D, FS)
    wd_all = G['ffn_w_down'].reshape(NSH, 2 * DEPTH, FS, D)
    conv_wi = G['conv_w_in']
    conv_wo = G['conv_w_out'].reshape(D, D)
    fox_full = jnp.pad(G['fox_w_in'].transpose(1, 0, 2).reshape(D, FOX_N), ((0, 0), (0, FOX_NP - FOX_N)))
    fox_w5 = fox_full.reshape(D, 5, FOX_NP // 5).transpose(1, 0, 2)
    fox_wf = fox_full[:, 3 * D:][None]
    fox_bf = jnp.pad(fox_b_f, ((0, 0), (0, 128 - FOX_H)))
    fox_wo = G['fox_w_out'].reshape(D, D)
    hgrn_wi = G['hgrn_w_in']
    hgrn_wo = G['hgrn_w_out'].reshape(D, D)
    pool_wf = G['pool_w'].reshape(NSH, 4, 64, POOL_G).transpose(1, 0, 2, 3).reshape(4, POOL_G, POOL_G)
    conv_bi = conv_b_in.reshape(NSH, 1, 2 * D // NSH)

    def ffn_f(xs, e):
        xo, h, u, sa, z = ffn_fwd(xs, ffn_norm_f[e:e + 1], wg_all, wu_all, wd_all, e, tm=min(1024, xs.shape[0]))
        return xo, (xs, h, u, sa, z)

    saved = []
    xs = x2
    lb = lb_fwd(hgrn_lb_logits)
    for i in range(DEPTH):
        xs, r0 = ffn_f(xs, 2 * i)
        gm = mix_norm[i:i + 1]
        xin = xs
        if i == 0:
            p, h = norm_mm(xin, gm, conv_wi, conv_bi, name="conv_in", out_dtype=F32)
            u2, u4 = conv_fwd_core(p, conv_dw_f, conv_dw_b, conv_ln_g, conv_ln_b, S=S)
            xs = mm_res(u4, conv_wo, xin, name="conv_out")
            rm = (xin, p, h, u2, u4)
        elif i == 1:
            p, h = norm_mm(xin, gm, fox_w5, None, name="fox_in", out_dtype=MXU)
            fl, _ = norm_mm(xin, gm, fox_wf, None, name="fox_in_f", out_dtype=F32)
            cq, _ = fox_cum(fl, fox_bf, S=S)
            qa, ka, va = fox_prep(p, cq)
            o, o32, lse = fox2_fwd(qa, ka, va, S=S, tq=min(FOX_T, S), tk=min(FOX_T, S))
            xs = mm_res(o, fox_wo, xin, name="fox_out")
            rm = (xin, h, fl, qa, ka, va, o, o32, lse)
        elif i == 2:
            p, h = norm_mm(xin, gm, hgrn_wi, None, name="hgrn_in", out_dtype=F32)
            yh, oh, st = hgrn_fwd(p, lb, hgrn_norm_f, S=S)
            xs = mm_res(yh, hgrn_wo, xin, name="hgrn_out")
            rm = (xin, p, h, yh, oh, st)
        else:
            xs, mp = pool_fwd(xin, gm, pool_wf, pool_scale_f, S=S)
            rm = (xin, mp)
        xs, r1 = ffn_f(xs, 2 * i + 1)
        saved.append((r0, rm, r1))

    loss8, dx, d_final = loss_head(xs, final_norm.reshape(1, D), tgt)

    conv_wit, fox_w5t, hgrn_wit = (w.transpose(0, 2, 1) for w in (conv_wi, fox_w5, hgrn_wi))
    conv_wot, fox_wot, hgrn_wot = conv_wo.T, fox_wo.T, hgrn_wo.T
    gb = {'g': None, 'u': None, 'd': None}
    d_ffn_norm = [None] * (2 * DEPTH)
    d_mix_norm = [None] * DEPTH
    gbig = {}
    gsm = {}

    def ffn_b(dy, res, e):
        xin, h, u, sa, z = res
        dxo, da, db, dyh, dg = ffn_bwd_dx(xin, ffn_norm_f[e:e + 1], dy, u, sa, wg_all, wu_all, wd_all, e)
        tk = min(2048, xin.shape[0])
        gb['g'] = mm_tn(h, da, name="ffn_dwg", G=NSH, M=D, N=FS, tk=tk, stack=(2 * DEPTH, e, gb['g']))
        gb['u'] = mm_tn(h, db, name="ffn_dwu", G=NSH, M=D, N=FS, tk=tk, stack=(2 * DEPTH, e, gb['u']))
        gb['d'] = mm_tn(z, dyh, name="ffn_dwd", G=NSH, M=FS, N=D, tk=tk, stack=(2 * DEPTH, e, gb['d']))
        d_ffn_norm[e] = dg
        return dxo

    for i in reversed(range(DEPTH)):
        r0, rm, r1 = saved[i]
        dx = ffn_b(dx, r1, 2 * i + 1)
        gm = mix_norm[i:i + 1]
        if i == 0:
            xin, p, h, u2, u4 = rm
            du2, dyb, gsm['conv_ln_g'], gsm['conv_ln_b'], gsm['conv_dw_b'] = conv_bwd_rows(dx, conv_wot, u2, conv_ln_g, conv_ln_b)
            dp, gsm['conv_b_in'], ddw = conv_bwd_core(du2, p, conv_dw_f, S=S)
            gsm['conv_dw'] = ddw
            gbig['conv_w_in'] = mm_tn(h, dp, name="conv_dwin", G=NSH, M=D, N=2 * D // NSH, b_step=1)
            gbig['conv_w_out'] = mm_tn(u4, dyb, name="conv_dwout", G=NSH, M=QS, N=D, a_step=1)
            dx, d_mix_norm[i] = inproj_bwd(dp, conv_wit, xin, gm, dx, name="conv_in_bwd")
        elif i == 1:
            xin, h, fl, qa, ka, va, o, o32, lse = rm
            do, dyb = mm_nt(dx, fox_wot, name="fox_out_bwd")
            qb, da = fox2_prep_bwd(do, o32, lse, qa)
            dq = fox2_dq(qb, ka, va, da, S=S, tq=min(FOX_T, S))
            dk, dv, dck = fox2_dkv(qb, ka, va, da, S=S, tq=min(FOX_T, S))
            dfl, dbf = fox_fin(dck, fl, fox_bf, S=S)
            gsm['fox_b_f'] = dbf
            dp = jnp.concatenate([dq, dk, dv, dfl], axis=1)
            dw5 = mm_tn(h, dp, name="fox_dwin", G=5, M=D, N=FOX_NP // 5, b_step=1)
            dwf = dw5.transpose(1, 0, 2).reshape(D, FOX_NP)[:, :FOX_N]
            gbig['fox_w_in'] = dwf.reshape(D, NSH, FOX_N // NSH).transpose(1, 0, 2)
            gbig['fox_w_out'] = mm_tn(o, dyb, name="fox_dwout", G=NSH, M=QS, N=D, a_step=1)
            dx, d_mix_norm[i] = inproj_bwd(dp, fox_w5t, xin, gm, dx, name="fox_in_bwd")
        elif i == 2:
            xin, p, h, yh, oh, st = rm
            dyo, dyb = mm_nt(dx, hgrn_wot, name="hgrn_out_bwd")
            dp, dlb, gsm['hgrn_norm'] = hgrn_bwd(p, oh, dyo, st, lb, hgrn_norm_f, S=S)
            gsm['hgrn_lb_logits'] = lb_bwd(hgrn_lb_logits, dlb)
            gbig['hgrn_w_in'] = mm_tn(h, dp, name="hgrn_dwin", G=NSH, M=D, N=D, b_step=1)
            gbig['hgrn_w_out'] = mm_tn(yh, dyb, name="hgrn_dwout", G=NSH, M=QS, N=D, a_step=1)
            dx, d_mix_norm[i] = inproj_bwd(dp, hgrn_wit, xin, gm, dx, name="hgrn_in_bwd")
        else:
            xin, mp = rm
            dmc, dyp, gsm['pool_scale'] = pool_bwd_rows(dx, mp, pool_wf, pool_scale_f, S=S)
            dwp = mm_tn(mp, dyp, name="pool_dw", G=4, M=POOL_G, N=POOL_G, a_step=1, b_step=1)
            gbig['pool_w'] = dwp.reshape(4, NSH, 64, POOL_G).transpose(1, 0, 2, 3).reshape(NSH, 4 * 64, POOL_G)
            dx, d_mix_norm[i] = pool_bwd_core(dmc, xin, gm, dx, S=S)
        dx = ffn_b(dx, r0, 2 * i)

    gbig['ffn_w_gate'] = gb['g'].reshape(NSH, 2 * DEPTH * D, FS)
    gbig['ffn_w_up'] = gb['u'].reshape(NSH, 2 * DEPTH * D, FS)
    gbig['ffn_w_down'] = gb['d'].reshape(NSH, 2 * DEPTH * FS, D)

    gl = [gbig[n] for n in _BIG]
    r_sib = sibling_half_exchange(gl)
    s16 = [pair_sum(g, r, pos) for g, r in zip(gl, r_sib)]
    r_ici = chip_scatter(s16)
    red = [reduce_own(g, rs, ri, pos) for g, rs, ri in zip(gl, r_sib, r_ici)]
    oth = sibling_exchange(red)
    out = {}
    for n, ga, gb_ in zip(_BIG, red, oth):
        res = adamw_halves(flat(W[n]), flat(M[n]), flat(V[n]), ga, gb_, pos)
        out[n] = [r.reshape(W[n].shape) for r in res]

    gfull = {'mix_norm': jnp.concatenate(d_mix_norm, axis=0), 'final_norm': d_final,
             'conv_b_in': gsm['conv_b_in'], 'conv_dw_b': gsm['conv_dw_b'], 'conv_ln_g': gsm['conv_ln_g'],
             'conv_ln_b': gsm['conv_ln_b'], 'fox_b_f': gsm['fox_b_f'][:, :FOX_H], 'hgrn_lb_logits': gsm['hgrn_lb_logits'],
             'ffn_norm': jnp.concatenate(d_ffn_norm, axis=0), 'conv_dw': gsm['conv_dw'][None, :CONV_W],
             'hgrn_norm': gsm['hgrn_norm'], 'pool_scale': gsm['pool_scale']}
    gpack = jnp.concatenate([_pack_replicated(gfull.get), _pack_sharded_small(gfull.get)], axis=0)
    gall = all_gather_devices(gpack)
    rep = adamw_sum(gall[:, :24], _pack_replicated(W.get), _pack_replicated(M.get), _pack_replicated(V.get))
    rep = [_unpack_replicated(r) for r in rep]
    for n in _REPLICATED:
        out[n] = [r[n].reshape(W[n].shape) for r in rep]
    gsh = lax.dynamic_slice_in_dim(gall[:, 24:], jme * QS, QS, axis=2)
    shd = adamw_sum(gsh, _pack_sharded_small(W.get), _pack_sharded_small(M.get), _pack_sharded_small(V.get))
    shd = [_unpack_sharded_small(r) for r in shd]
    for n in _SHARDED_SMALL:
        out[n] = [r[n].reshape(W[n].shape) for r in shd]

    loss = lax.psum(loss8[0, 0], ("x", "y", "c"))
    res = [loss, dx.reshape(x.shape)]
    for k in range(4):
        res += [out[n][k] for n in _WEIGHTS]
    return tuple(res)
```

```python
import jax
import jax.numpy as jnp
from jax import lax
from jax.experimental import pallas as pl
from jax.experimental.pallas import tpu as pltpu

D = 1024
F = 2816
NSH = 4
FS = F // NSH
DEPTH = 4
RMS_EPS = 1e-6
LN_EPS = 1e-5
CONV_W = 31
HALO = 32
FOX_H = 16
FOX_DH = 64
HG_H = 8
HG_DK = 128
HG_C = 32
POOL_WIN = (2, 4, 8, 16)
POOL_G = 256
MXU = jnp.bfloat16
F32 = jnp.float32
VMEM_LIMIT = 52 * 1024 * 1024

ADAM_LR = 0.001
ADAM_B1 = 0.9
ADAM_B2 = 0.999
ADAM_EPS = 1e-08
ADAM_WD = 0.01
ADAM_STEP = 10


def _call(body, *, name, grid, in_specs, out_specs, out_shape, scratch=(), vmem=VMEM_LIMIT):
    return pl.pallas_call(
        body, name=name, grid=grid, in_specs=in_specs, out_specs=out_specs, out_shape=out_shape,
        scratch_shapes=list(scratch),
        compiler_params=pltpu.CompilerParams(dimension_semantics=("arbitrary",) * len(grid),
                                             vmem_limit_bytes=vmem))


def _dot(a, b):
    return jnp.dot(a, b, preferred_element_type=F32)


def _dot_nt(a, b):
    return lax.dot_general(a, b, (((1,), (1,)), ((), ())), preferred_element_type=F32)


def _dot_tn(a, b):
    return lax.dot_general(a, b, (((0,), (0,)), ((), ())), preferred_element_type=F32)


def _split(x):
    hi = x.astype(MXU)
    return hi, (x - hi.astype(F32)).astype(MXU)


def _sigmoid(x):
    return 1.0 / (1.0 + jnp.exp(-x))


def _rms(x, g):
    r = lax.rsqrt(jnp.mean(x * x, axis=-1, keepdims=True) + RMS_EPS)
    return x * r * g


def _rms_bwd(dh, x, g):
    r = lax.rsqrt(jnp.mean(x * x, axis=-1, keepdims=True) + RMS_EPS)
    xh = x * r
    dhg = dh * g
    dx = r * (dhg - xh * jnp.mean(dhg * xh, axis=-1, keepdims=True))
    return dx, jnp.sum(dh * xh, axis=0, keepdims=True)


def _sds(shape, dtype):
    return jax.ShapeDtypeStruct(shape, dtype)


def _wspec(w, e):
    if w.ndim == 3:
        return pl.BlockSpec((None,) + w.shape[1:], lambda i, j: (j, 0, 0))
    return pl.BlockSpec((None, None) + w.shape[2:], lambda i, j: (j, e, 0, 0))


def ffn_fwd(x, g, wg, wu, wd, e=0, *, tm=512):
    T = x.shape[0]

    def body(x_ref, g_ref, wg_ref, wu_ref, wd_ref, xo_ref, h_ref, u_ref, sa_ref, z_ref, acc_ref):
        j = pl.program_id(1)

        @pl.when(j == 0)
        def _():
            h_ref[...] = _rms(x_ref[...], g_ref[...]).astype(MXU)
            acc_ref[...] = jnp.zeros_like(acc_ref)

        h = h_ref[...]
        a = _dot(h, wg_ref[...])
        b = _dot(h, wu_ref[...])
        s = _sigmoid(a)
        sa = a * s
        u_ref[...] = (b * (s * (1.0 + a * (1.0 - s)))).astype(MXU)
        sa_ref[...] = sa.astype(MXU)
        z = (sa * b).astype(MXU)
        z_ref[...] = z
        acc_ref[...] += _dot(z, wd_ref[...])

        @pl.when(j == NSH - 1)
        def _():
            xo_ref[...] = x_ref[...] + 0.5 * acc_ref[...]

    return _call(
        body, name="ffn_fwd", grid=(T // tm, NSH),
        in_specs=[pl.BlockSpec((tm, D), lambda i, j: (i, 0)),
                  pl.BlockSpec((1, D), lambda i, j: (0, 0)),
                  _wspec(wg, e), _wspec(wu, e), _wspec(wd, e)],
        out_specs=[pl.BlockSpec((tm, D), lambda i, j: (i, 0)),
                   pl.BlockSpec((tm, D), lambda i, j: (i, 0)),
                   pl.BlockSpec((None, tm, FS), lambda i, j: (j, i, 0)),
                   pl.BlockSpec((None, tm, FS), lambda i, j: (j, i, 0)),
                   pl.BlockSpec((None, tm, FS), lambda i, j: (j, i, 0))],
        out_shape=[_sds((T, D), F32), _sds((T, D), MXU)] + [_sds((NSH, T, FS), MXU)] * 3,
        scratch=[pltpu.VMEM((tm, D), F32)],
    )(x, g, wg, wu, wd)


def ffn_bwd_dx(x, g, dy, u, sa, wg, wu, wd, e=0, *, tm=512):
    T = x.shape[0]

    def body(x_ref, g_ref, dy_ref, u_ref, sa_ref, wg_ref, wu_ref, wd_ref,
             dx_ref, da_ref, db_ref, dyh_ref, dg_ref):
        i = pl.program_id(0)
        j = pl.program_id(1)
        acc_ref = dx_ref

        @pl.when(j == 0)
        def _():
            dyh_ref[...] = (0.5 * dy_ref[...]).astype(MXU)
            acc_ref[...] = jnp.zeros_like(acc_ref)

        @pl.when((i == 0) & (j == 0))
        def _():
            dg_ref[...] = jnp.zeros_like(dg_ref)

        dz = _dot_nt(dyh_ref[...], wd_ref[...])
        da = (dz * u_ref[...].astype(F32)).astype(MXU)
        db = (dz * sa_ref[...].astype(F32)).astype(MXU)
        da_ref[...] = da
        db_ref[...] = db
        acc_ref[...] += _dot_nt(da, wg_ref[...]) + _dot_nt(db, wu_ref[...])

        @pl.when(j == NSH - 1)
        def _():
            dxn, dg = _rms_bwd(acc_ref[...], x_ref[...], g_ref[...])
            dx_ref[...] = dy_ref[...] + dxn
            dg_ref[...] += dg

    return _call(
        body, name="ffn_bwd_dx", grid=(T // tm, NSH),
        in_specs=[pl.BlockSpec((tm, D), lambda i, j: (i, 0)),
                  pl.BlockSpec((1, D), lambda i, j: (0, 0)),
                  pl.BlockSpec((tm, D), lambda i, j: (i, 0)),
                  pl.BlockSpec((None, tm, FS), lambda i, j: (j, i, 0)),
                  pl.BlockSpec((None, tm, FS), lambda i, j: (j, i, 0)),
                  _wspec(wg, e), _wspec(wu, e), _wspec(wd, e)],
        out_specs=[pl.BlockSpec((tm, D), lambda i, j: (i, 0)),
                   pl.BlockSpec((None, tm, FS), lambda i, j: (j, i, 0)),
                   pl.BlockSpec((None, tm, FS), lambda i, j: (j, i, 0)),
                   pl.BlockSpec((tm, D), lambda i, j: (i, 0)),
                   pl.BlockSpec((1, D), lambda i, j: (0, 0))],
        out_shape=[_sds((T, D), F32), _sds((NSH, T, FS), MXU), _sds((NSH, T, FS), MXU),
                   _sds((T, D), MXU), _sds((1, D), F32)],
    )(x, g, dy, u, sa, wg, wu, wd)


def mm_tn(a, b, *, name, G, M, N, a_step=0, b_step=0, tk=512, stack=None):
    T = a.shape[-2]
    if stack is not None:
        return _mm_tn_stack(a, b, name=name, G=G, M=M, N=N, tk=tk, stack=stack)

    def spec(arr, width, step):
        if arr.ndim == 3:
            return pl.BlockSpec((None, tk, width), lambda g, k: (g, k, 0))
        return pl.BlockSpec((tk, width), lambda g, k: (k, g * step))

    def body(a_ref, b_ref, o_ref):
        @pl.when(pl.program_id(1) == 0)
        def _():
            o_ref[...] = jnp.zeros_like(o_ref)

        o_ref[...] += _dot_tn(a_ref[...], b_ref[...])

    return _call(
        body, name=name, grid=(G, T // tk),
        in_specs=[spec(a, M, a_step), spec(b, N, b_step)],
        out_specs=pl.BlockSpec((None, M, N), lambda g, k: (g, 0, 0)),
        out_shape=_sds((G, M, N), F32),
    )(a, b)


def norm_mm(x, g, wb, bias, *, name, out_dtype, tm=1024):
    T = x.shape[0]
    tm = min(tm, T)
    G, _, ns = wb.shape
    has_bias = bias is not None

    def body(*refs):
        if has_bias:
            x_ref, g_ref, w_ref, bias_ref, p_ref, h_ref = refs
        else:
            x_ref, g_ref, w_ref, p_ref, h_ref = refs

        @pl.when(pl.program_id(1) == 0)
        def _():
            h_ref[...] = _rms(x_ref[...], g_ref[...]).astype(MXU)

        p = _dot(h_ref[...], w_ref[...])
        if has_bias:
            p = p + bias_ref[...]
        p_ref[...] = p.astype(out_dtype)

    in_specs = [pl.BlockSpec((tm, D), lambda i, j: (i, 0)),
                pl.BlockSpec((1, D), lambda i, j: (0, 0)),
                pl.BlockSpec((None, D, ns), lambda i, j: (j, 0, 0))]
    args = [x, g, wb]
    if has_bias:
        in_specs.append(pl.BlockSpec((None, 1, ns), lambda i, j: (j, 0, 0)))
        args.append(bias)
    return _call(
        body, name=name, grid=(T // tm, G), in_specs=in_specs,
        out_specs=[pl.BlockSpec((tm, ns), lambda i, j: (i, j)),
                   pl.BlockSpec((tm, D), lambda i, j: (i, 0))],
        out_shape=[_sds((T, G * ns), out_dtype), _sds((T, D), MXU)],
    )(*args)


def mm_res(y, w, x, *, name, tm=1024):
    T, K = y.shape
    tm = min(tm, T)

    def body(y_ref, w_ref, x_ref, o_ref):
        o_ref[...] = x_ref[...] + _dot(y_ref[...], w_ref[...])

    return _call(
        body, name=name, grid=(T // tm,),
        in_specs=[pl.BlockSpec((tm, K), lambda i: (i, 0)),
                  pl.BlockSpec((K, D), lambda i: (0, 0)),
                  pl.BlockSpec((tm, D), lambda i: (i, 0))],
        out_specs=pl.BlockSpec((tm, D), lambda i: (i, 0)),
        out_shape=_sds((T, D), F32),
    )(y, w, x)


def mm_nt(a, wt, *, name, tm=1024):
    T, K = a.shape
    tm = min(tm, T)
    N = wt.shape[1]
    w = wt

    def body(a_ref, w_ref, o_ref, ab_ref):
        ab = a_ref[...].astype(MXU)
        ab_ref[...] = ab
        o_ref[...] = _dot(ab, w_ref[...])

    return _call(
        body, name=name, grid=(T // tm,),
        in_specs=[pl.BlockSpec((tm, K), lambda i: (i, 0)),
                  pl.BlockSpec((K, N), lambda i: (0, 0))],
        out_specs=[pl.BlockSpec((tm, N), lambda i: (i, 0)),
                   pl.BlockSpec((tm, K), lambda i: (i, 0))],
        out_shape=[_sds((T, N), F32), _sds((T, K), MXU)],
    )(a, w)


def inproj_bwd(dp, wb, x, g, dres, *, name, tm=1024):
    T = x.shape[0]
    tm = min(tm, T)
    G, ns, _ = wb.shape

    def body(dp_ref, w_ref, x_ref, g_ref, dres_ref, dx_ref, dg_ref, acc_ref):
        i = pl.program_id(0)
        j = pl.program_id(1)

        @pl.when(j == 0)
        def _():
            acc_ref[...] = jnp.zeros_like(acc_ref)

        @pl.when((i == 0) & (j == 0))
        def _():
            dg_ref[...] = jnp.zeros_like(dg_ref)

        acc_ref[...] += _dot(dp_ref[...], w_ref[...])

        @pl.when(j == G - 1)
        def _():
            dxn, dg = _rms_bwd(acc_ref[...], x_ref[...], g_ref[...])
            dx_ref[...] = dres_ref[...] + dxn
            dg_ref[...] += dg

    return _call(
        body, name=name, grid=(T // tm, G),
        in_specs=[pl.BlockSpec((tm, ns), lambda i, j: (i, j)),
                  pl.BlockSpec((None, ns, D), lambda i, j: (j, 0, 0)),
                  pl.BlockSpec((tm, D), lambda i, j: (i, 0)),
                  pl.BlockSpec((1, D), lambda i, j: (0, 0)),
                  pl.BlockSpec((tm, D), lambda i, j: (i, 0))],
        out_specs=[pl.BlockSpec((tm, D), lambda i, j: (i, 0)),
                   pl.BlockSpec((1, D), lambda i, j: (0, 0))],
        out_shape=[_sds((T, D), F32), _sds((1, D), F32)],
        scratch=[pltpu.VMEM((tm, D), F32)],
    )(dp, wb, x, g, dres)


def loss_head(x, gf, tgt, *, tm=512):
    T = x.shape[0]

    def body(x_ref, g_ref, t_ref, loss_ref, dx_ref, dg_ref):
        @pl.when(pl.program_id(0) == 0)
        def _():
            loss_ref[...] = jnp.zeros_like(loss_ref)
            dg_ref[...] = jnp.zeros_like(dg_ref)

        xv = x_ref[...]
        gv = g_ref[...]
        e = _rms(xv, gv) - t_ref[...]
        loss_ref[...] += (0.5 / D) * jnp.sum(e * e)
        dxn, dg = _rms_bwd(e * (1.0 / D), xv, gv)
        dx_ref[...] = dxn
        dg_ref[...] += dg

    return _call(
        body, name="loss_head", grid=(T // tm,),
        in_specs=[pl.BlockSpec((tm, D), lambda i: (i, 0)),
                  pl.BlockSpec((1, D), lambda i: (0, 0)),
                  pl.BlockSpec((tm, D), lambda i: (i, 0))],
        out_specs=[pl.BlockSpec((8, 128), lambda i: (0, 0)),
                   pl.BlockSpec((tm, D), lambda i: (i, 0)),
                   pl.BlockSpec((1, D), lambda i: (0, 0))],
        out_shape=[_sds((8, 128), F32), _sds((T, D), F32), _sds((1, D), F32)],
    )(x, gf, tgt)


def _glu(p):
    return p[:, :D] * _sigmoid(p[:, D:])


def _ln_stats(u):
    mu = jnp.mean(u, axis=-1, keepdims=True)
    xc = u - mu
    rstd = lax.rsqrt(jnp.mean(xc * xc, axis=-1, keepdims=True) + LN_EPS)
    return xc * rstd, rstd


def conv_fwd_core(p, dw, dwb, lng, lnb, *, S, tt=256):
    T = p.shape[0]
    nb = S // tt
    r = tt // HALO

    def body(pc_ref, pp_ref, dw_ref, dwb_ref, lng_ref, lnb_ref, u2_ref, u4_ref, ubuf):
        first = (pl.program_id(0) % nb) == 0
        ubuf[0:HALO, :] = jnp.where(first, 0.0, _glu(pp_ref[...]))
        ubuf[HALO:, :] = _glu(pc_ref[...])
        for c in range(D // 128):
            cs = slice(c * 128, (c + 1) * 128)
            acc = jnp.zeros((tt, 128), F32)
            for k in range(CONV_W):
                acc = acc + dw_ref[k:k + 1, cs] * ubuf[k + 2:k + 2 + tt, cs]
            u2_ref[:, cs] = acc + dwb_ref[:, cs]
        xh, _ = _ln_stats(u2_ref[...])
        u3 = xh * lng_ref[...] + lnb_ref[...]
        u4_ref[...] = (u3 * _sigmoid(u3)).astype(MXU)

    row = pl.BlockSpec((1, D), lambda i: (0, 0))
    return _call(
        body, name="conv_fwd_core", grid=(T // tt,),
        in_specs=[pl.BlockSpec((tt, 2 * D), lambda i: (i, 0)),
                  pl.BlockSpec((HALO, 2 * D), lambda i: (jnp.maximum(i * r - 1, 0), 0)),
                  pl.BlockSpec((HALO, D), lambda i: (0, 0)), row, row, row],
        out_specs=[pl.BlockSpec((tt, D), lambda i: (i, 0)), pl.BlockSpec((tt, D), lambda i: (i, 0))],
        out_shape=[_sds((T, D), F32), _sds((T, D), MXU)],
        scratch=[pltpu.VMEM((tt + HALO, D), F32)],
    )(p, p, dw, dwb, lng, lnb)


def conv_bwd_rows(dy, wout, u2, lng, lnb, *, tm=512):
    T = dy.shape[0]

    def body(dy_ref, w_ref, u2_ref, lng_ref, lnb_ref, du2_ref, dyb_ref, dlng_ref, dlnb_ref, ddwb_ref):
        @pl.when(pl.program_id(0) == 0)
        def _():
            dlng_ref[...] = jnp.zeros_like(dlng_ref)
            dlnb_ref[...] = jnp.zeros_like(dlnb_ref)
            ddwb_ref[...] = jnp.zeros_like(ddwb_ref)

        dyb = dy_ref[...].astype(MXU)
        dyb_ref[...] = dyb
        du4 = _dot(dyb, w_ref[...])
        xh, rstd = _ln_stats(u2_ref[...])
        lng_v = lng_ref[...]
        u3 = xh * lng_v + lnb_ref[...]
        s = _sigmoid(u3)
        du3 = du4 * (s * (1.0 + u3 * (1.0 - s)))
        dlng_ref[...] += jnp.sum(du3 * xh, axis=0, keepdims=True)
        dlnb_ref[...] += jnp.sum(du3, axis=0, keepdims=True)
        dxh = du3 * lng_v
        du2 = rstd * (dxh - jnp.mean(dxh, axis=-1, keepdims=True)
                      - xh * jnp.mean(dxh * xh, axis=-1, keepdims=True))
        du2_ref[...] = du2
        ddwb_ref[...] += jnp.sum(du2, axis=0, keepdims=True)

    row = pl.BlockSpec((1, D), lambda i: (0, 0))
    blk = pl.BlockSpec((tm, D), lambda i: (i, 0))
    return _call(
        body, name="conv_bwd_rows", grid=(T // tm,),
        in_specs=[blk, pl.BlockSpec((D, D), lambda i: (0, 0)), blk, row, row],
        out_specs=[blk, blk, row, row, row],
        out_shape=[_sds((T, D), F32), _sds((T, D), MXU), _sds((1, D), F32), _sds((1, D), F32), _sds((1, D), F32)],
    )(dy, wout, u2, lng, lnb)


def conv_bwd_core(du2, p, dw, *, S, tt=256):
    T = p.shape[0]
    nb = S // tt
    r = tt // HALO
    last_halo = T // HALO - 1

    def body(dc_ref, dn_ref, pc_ref, pp_ref, dw_ref, dp_ref, dbin_ref, ddw_ref, ubuf, dbuf):
        i = pl.program_id(0)

        @pl.when(i == 0)
        def _():
            dbin_ref[...] = jnp.zeros_like(dbin_ref)
            ddw_ref[...] = jnp.zeros_like(ddw_ref)

        first = (i % nb) == 0
        last = (i % nb) == nb - 1
        ubuf[0:HALO, :] = jnp.where(first, 0.0, _glu(pp_ref[...]))
        ubuf[HALO:, :] = _glu(pc_ref[...])
        dbuf[0:tt, :] = dc_ref[...]
        dbuf[tt:, :] = jnp.where(last, 0.0, dn_ref[...])
        pc = pc_ref[...]
        for c in range(D // 128):
            cs = slice(c * 128, (c + 1) * 128)
            dcur = dbuf[0:tt, cs]
            du = jnp.zeros((tt, 128), F32)
            for k in range(CONV_W):
                ddw_ref[k:k + 1, cs] += jnp.sum(dcur * ubuf[k + 2:k + 2 + tt, cs], axis=0, keepdims=True)
                du = du + dw_ref[k:k + 1, cs] * dbuf[CONV_W - 1 - k:CONV_W - 1 - k + tt, cs]
            a = pc[:, c * 128:(c + 1) * 128]
            sb = _sigmoid(pc[:, D + c * 128:D + (c + 1) * 128])
            da = du * sb
            db = du * a * sb * (1.0 - sb)
            dp_ref[:, cs] = da.astype(MXU)
            dp_ref[:, D + c * 128:D + (c + 1) * 128] = db.astype(MXU)
            dbin_ref[:, cs] += jnp.sum(da, axis=0, keepdims=True)
            dbin_ref[:, D + c * 128:D + (c + 1) * 128] += jnp.sum(db, axis=0, keepdims=True)

    return _call(
        body, name="conv_bwd_core", grid=(T // tt,),
        in_specs=[pl.BlockSpec((tt, D), lambda i: (i, 0)),
                  pl.BlockSpec((HALO, D), lambda i: (jnp.minimum((i + 1) * r, last_halo), 0)),
                  pl.BlockSpec((tt, 2 * D), lambda i: (i, 0)),
                  pl.BlockSpec((HALO, 2 * D), lambda i: (jnp.maximum(i * r - 1, 0), 0)),
                  pl.BlockSpec((HALO, D), lambda i: (0, 0))],
        out_specs=[pl.BlockSpec((tt, 2 * D), lambda i: (i, 0)),
                   pl.BlockSpec((1, 2 * D), lambda i: (0, 0)),
                   pl.BlockSpec((HALO, D), lambda i: (0, 0))],
        out_shape=[_sds((T, 2 * D), MXU), _sds((1, 2 * D), F32), _sds((HALO, D), F32)],
        scratch=[pltpu.VMEM((tt + HALO, D), F32), pltpu.VMEM((tt + HALO, D), F32)],
    )(du2, du2, p, p, dw)


PH = 16


def _pool_cnt(i, nb, tt, win):
    pos = (i % nb) * tt + lax.broadcasted_iota(jnp.int32, (tt, 1), 0)
    return jnp.minimum(pos + 1, win).astype(F32)


def pool_fwd(x, g, wp, scale, *, S, tt=256):
    T = x.shape[0]
    nb = S // tt
    r = tt // PH

    def body(xc_ref, xp_ref, g_ref, wp_ref, sc_ref, xo_ref, m_ref, hbuf):
        i = pl.program_id(0)
        first = (i % nb) == 0
        gv = g_ref[...]
        hbuf[0:PH, :] = jnp.where(first, 0.0, _rms(xp_ref[...], gv))
        xc = xc_ref[...]
        hbuf[PH:, :] = _rms(xc, gv)
        for gi, win in enumerate(POOL_WIN):
            gs = slice(gi * POOL_G, (gi + 1) * POOL_G)
            acc = hbuf[PH:PH + tt, gs]
            for j in range(1, win):
                acc = acc + hbuf[PH - j:PH - j + tt, gs]
            m = (acc / _pool_cnt(i, nb, tt, win) - hbuf[PH:PH + tt, gs]).astype(MXU)
            m_ref[:, gs] = m
            xo_ref[:, gs] = xc[:, gs] + _dot(m, wp_ref[gi]) * sc_ref[:, gs]

    row = pl.BlockSpec((1, D), lambda i: (0, 0))
    blk = pl.BlockSpec((tt, D), lambda i: (i, 0))
    return _call(
        body, name="pool_fwd", grid=(T // tt,),
        in_specs=[blk, pl.BlockSpec((PH, D), lambda i: (jnp.maximum(i * r - 1, 0), 0)), row,
                  pl.BlockSpec((len(POOL_WIN), POOL_G, POOL_G), lambda i: (0, 0, 0)), row],
        out_specs=[blk, blk],
        out_shape=[_sds((T, D), F32), _sds((T, D), MXU)],
        scratch=[pltpu.VMEM((tt + PH, D), F32)],
    )(x, x, g, wp, scale)


def pool_bwd_rows(dy, m, wp, scale, *, S, tt=256):
    T = dy.shape[0]
    nb = S // tt

    def body(dy_ref, m_ref, wp_ref, sc_ref, dmc_ref, dyp_ref, dsc_ref):
        i = pl.program_id(0)

        @pl.when(i == 0)
        def _():
            dsc_ref[...] = jnp.zeros_like(dsc_ref)

        for gi, win in enumerate(POOL_WIN):
            gs = slice(gi * POOL_G, (gi + 1) * POOL_G)
            dyg = dy_ref[:, gs]
            w = wp_ref[gi]
            dsc_ref[:, gs] += jnp.sum(dyg * _dot(m_ref[:, gs], w), axis=0, keepdims=True)
            dyp = (dyg * sc_ref[:, gs]).astype(MXU)
            dyp_ref[:, gs] = dyp
            dmc_ref[:, gs] = _dot_nt(dyp, w) / _pool_cnt(i, nb, tt, win)

    row = pl.BlockSpec((1, D), lambda i: (0, 0))
    blk = pl.BlockSpec((tt, D), lambda i: (i, 0))
    return _call(
        body, name="pool_bwd_rows", grid=(T // tt,),
        in_specs=[blk, blk, pl.BlockSpec((len(POOL_WIN), POOL_G, POOL_G), lambda i: (0, 0, 0)), row],
        out_specs=[blk, blk, row],
        out_shape=[_sds((T, D), F32), _sds((T, D), MXU), _sds((1, D), F32)],
    )(dy, m, wp, scale)


def pool_bwd_core(dmc, x, g, dres, *, S, tt=256):
    T = x.shape[0]
    nb = S // tt
    r = tt // PH
    last_halo = T // PH - 1

    def body(dc_ref, dn_ref, x_ref, g_ref, dres_ref, dx_ref, dg_ref, dbuf, dh_buf):
        i = pl.program_id(0)

        @pl.when(i == 0)
        def _():
            dg_ref[...] = jnp.zeros_like(dg_ref)

        last = (i % nb) == nb - 1
        dbuf[0:tt, :] = dc_ref[...]
        dbuf[tt:, :] = jnp.where(last, 0.0, dn_ref[...])
        for gi, win in enumerate(POOL_WIN):
            gs = slice(gi * POOL_G, (gi + 1) * POOL_G)
            cur = dbuf[0:tt, gs]
            acc = cur
            for j in range(1, win):
                acc = acc + dbuf[j:j + tt, gs]
            dh_buf[:, gs] = acc - cur * _pool_cnt(i, nb, tt, win)
        dxn, dg = _rms_bwd(dh_buf[...], x_ref[...], g_ref[...])
        dx_ref[...] = dres_ref[...] + dxn
        dg_ref[...] += dg

    row = pl.BlockSpec((1, D), lambda i: (0, 0))
    blk = pl.BlockSpec((tt, D), lambda i: (i, 0))
    return _call(
        body, name="pool_bwd_core", grid=(T // tt,),
        in_specs=[blk, pl.BlockSpec((PH, D), lambda i: (jnp.minimum((i + 1) * r, last_halo), 0)), blk, row, blk],
        out_specs=[blk, row],
        out_shape=[_sds((T, D), F32), _sds((1, D), F32)],
        scratch=[pltpu.VMEM((tt + PH, D), F32), pltpu.VMEM((tt, D), F32)],
    )(dmc, dmc, x, g, dres)


NEG = -1e30


def _tri(n, upper=False):
    r = lax.broadcasted_iota(jnp.int32, (n, n), 0)
    c = lax.broadcasted_iota(jnp.int32, (n, n), 1)
    return (r <= c if upper else r >= c).astype(F32)


def _dot_hi(a, b):
    return jnp.dot(a, b, preferred_element_type=F32, precision=lax.Precision.HIGHEST)


def _log_sigmoid(z):
    return jnp.minimum(z, 0.0) - jnp.log(1.0 + jnp.exp(-jnp.abs(z)))


def fox_cum(fl, bf, *, S, tt=256):
    T = fl.shape[0]
    nb = S // tt

    def body(fl_ref, bf_ref, c_ref, carry):
        i = pl.program_id(0)

        @pl.when((i % nb) == 0)
        def _():
            carry[...] = jnp.zeros_like(carry)

        lf = _log_sigmoid(fl_ref[...] + bf_ref[...])
        c = _dot_hi(_tri(tt), lf) + carry[...]
        c_ref[...] = c
        carry[...] = c[tt - 1:tt, :]

    return _call(
        body, name="fox_cum", grid=(T // tt,),
        in_specs=[pl.BlockSpec((tt, 128), lambda i: (i, 0)), pl.BlockSpec((1, 128), lambda i: (0, 0))],
        out_specs=pl.BlockSpec((tt, 128), lambda i: (i, 0)),
        out_shape=_sds((T, 128), F32),
        scratch=[pltpu.VMEM((1, 128), F32)],
    )(fl, bf)


HW = 128
FOX_T = 512
COL_ONE = FOX_DH + 3
COL_LSE = FOX_DH + 6


def _parts(x):
    hi = x.astype(MXU).astype(F32)
    mid = (x - hi).astype(MXU).astype(F32)
    lo = (x - hi - mid).astype(MXU).astype(F32)
    return [hi, mid, lo]


def _aug(n, cols):
    lane = lax.broadcasted_iota(jnp.int32, (n, HW - FOX_DH), 1)
    out = jnp.zeros((n, HW - FOX_DH), F32)
    for i, cval in enumerate(cols):
        out = jnp.where(lane == i, cval, out)
    return out


def fox_prep(p, c, *, tt=256):
    T = p.shape[0]

    def body(q_ref, k_ref, v_ref, c_ref, qa_ref, ka_ref, va_ref):
        ones = [1.0, 1.0, 1.0]
        for h in range(FOX_H):
            hs = slice(h * FOX_DH, (h + 1) * FOX_DH)
            lo, mid = h * HW, h * HW + FOX_DH
            cp = _parts(c_ref[:, h:h + 1])
            qa_ref[:, lo:mid] = (q_ref[:, hs].astype(F32) * (FOX_DH ** -0.5)).astype(MXU)
            qa_ref[:, mid:lo + HW] = _aug(tt, cp + ones).astype(MXU)
            ka_ref[:, lo:mid] = k_ref[:, hs]
            ka_ref[:, mid:lo + HW] = _aug(tt, ones + [-x for x in cp] + ones).astype(MXU)
            va_ref[:, lo:mid] = v_ref[:, hs]
            va_ref[:, mid:lo + HW] = _aug(tt, ones).astype(MXU)

    wide = pl.BlockSpec((tt, FOX_H * HW), lambda i: (i, 0))
    col = lambda k: pl.BlockSpec((tt, D), lambda i: (i, k))
    return _call(body, name="fox_prep", grid=(T // tt,),
                 in_specs=[col(0), col(1), col(2), pl.BlockSpec((tt, 128), lambda i: (i, 0))],
                 out_specs=[wide, wide, wide], out_shape=[_sds((T, FOX_H * HW), MXU)] * 3)(p, p, p, c)


def _causal(x, fill):
    r = lax.broadcasted_iota(jnp.int32, x.shape, 0)
    c = lax.broadcasted_iota(jnp.int32, x.shape, 1)
    return jnp.where(r >= c, x, fill)


def _wide_specs(tq, nq, q_of, k_of):
    qs = pl.BlockSpec((tq, FOX_H * HW), lambda b, i, j: (b * nq + q_of(i, j), 0))
    ks = pl.BlockSpec((tq, FOX_H * HW), lambda b, i, j: (b * nq + k_of(i, j), 0))
    return qs, ks


def fox2_fwd(qa, ka, va, *, S, tq=256, tk=512):
    T = qa.shape[0]
    tk = min(tk, S)
    nq = S // tq
    nk = S // tk
    r = tk // tq

    def body(q_ref, k_ref, v_ref, o_ref, o32_ref, lse_ref, m_sc, acc, acc_lo):
        qi = pl.program_id(1)
        ki = pl.program_id(2)
        last = qi // r

        @pl.when(ki == 0)
        def _():
            m_sc[...] = jnp.full_like(m_sc, NEG)
            acc[...] = jnp.zeros_like(acc)
            acc_lo[...] = jnp.zeros_like(acc_lo)

        def step(diag):
            for h in range(FOX_H):
                ws = slice(h * HW, (h + 1) * HW)
                s = _dot_nt(q_ref[:, ws], k_ref[:, ws])
                if diag:
                    row = qi * tq + lax.broadcasted_iota(jnp.int32, s.shape, 0)
                    col = ki * tk + lax.broadcasted_iota(jnp.int32, s.shape, 1)
                    s = jnp.where(row >= col, s, NEG)
                m_prev = m_sc[h]
                m_new = jnp.maximum(m_prev, jnp.max(s, axis=-1, keepdims=True))
                alpha = jnp.exp(m_prev - m_new)
                hi, lo = _split(jnp.exp(s - jnp.tile(m_new, (1, tk // 128))))
                acc[h] = alpha * acc[h] + _dot(hi, v_ref[:, ws])
                acc_lo[h] = alpha * acc_lo[h] + _dot(lo, v_ref[:, ws])
                m_sc[h] = m_new

        @pl.when(ki < last)
        def _():
            step(False)

        @pl.when(ki == last)
        def _():
            step(True)
            for h in range(FOX_H):
                hs = slice(h * FOX_DH, (h + 1) * FOX_DH)
                full = acc[h] + acc_lo[h]
                l = full[:, FOX_DH:FOX_DH + 1]
                o_ref[:, hs] = (acc[h][:, :FOX_DH] / l).astype(MXU)
                o32_ref[:, hs] = full[:, :FOX_DH] / l
                lse_ref[:, h:h + 1] = m_sc[h][:, 0:1] + jnp.log(l)

    qs = pl.BlockSpec((tq, FOX_H * HW), lambda b, i, j: (b * nq + i, 0))
    ks = pl.BlockSpec((tk, FOX_H * HW), lambda b, i, j: (b * nk + jnp.minimum(j, i // r), 0))
    orow = pl.BlockSpec((tq, D), lambda b, i, j: (b * nq + i, 0))
    return _call(
        body, name="fox_fwd", grid=(T // S, nq, nk),
        in_specs=[qs, ks, ks],
        out_specs=[orow, orow, pl.BlockSpec((tq, 128), lambda b, i, j: (b * nq + i, 0))],
        out_shape=[_sds((T, D), MXU), _sds((T, D), F32), _sds((T, 128), F32)],
        scratch=[pltpu.VMEM((FOX_H, tq, 128), F32), pltpu.VMEM((FOX_H, tq, HW), F32),
                 pltpu.VMEM((FOX_H, tq, HW), F32)],
    )(qa, ka, va)


def fox2_prep_bwd(do, o32, lse, qa, *, tt=256):
    T = do.shape[0]

    def body(do_ref, o_ref, lse_ref, qa_ref, qb_ref, da_ref):
        lane = lax.broadcasted_iota(jnp.int32, (tt, HW), 1)
        dob = do_ref[...].astype(MXU)
        for h in range(FOX_H):
            hs = slice(h * FOX_DH, (h + 1) * FOX_DH)
            ws = slice(h * HW, (h + 1) * HW)
            doh = dob[:, hs]
            delta = jnp.sum(doh.astype(F32) * o_ref[:, hs], axis=-1, keepdims=True)
            da_ref[:, h * HW:h * HW + FOX_DH] = doh
            da_ref[:, h * HW + FOX_DH:(h + 1) * HW] = _aug(tt, [-x for x in _parts(delta)]).astype(MXU)
            tile = qa_ref[:, ws]
            for i, part in enumerate(_parts(lse_ref[:, h:h + 1])):
                tile = jnp.where(lane == COL_LSE + i, (-part).astype(MXU), tile)
            qb_ref[:, ws] = tile

    wide = pl.BlockSpec((tt, FOX_H * HW), lambda i: (i, 0))
    blk = pl.BlockSpec((tt, D), lambda i: (i, 0))
    return _call(body, name="fox_prep_bwd", grid=(T // tt,),
                 in_specs=[blk, blk, pl.BlockSpec((tt, 128), lambda i: (i, 0)), wide],
                 out_specs=[wide, wide],
                 out_shape=[_sds((T, FOX_H * HW), MXU), _sds((T, FOX_H * HW), MXU)],
                 )(do, o32, lse, qa)


def fox2_dq(qb, ka, va, da, *, S, tq=256):
    T = qb.shape[0]
    nq = S // tq

    def body(q_ref, k_ref, v_ref, d_ref, dq_ref, acc):
        qi = pl.program_id(1)
        ki = pl.program_id(2)

        @pl.when(ki == 0)
        def _():
            acc[...] = jnp.zeros_like(acc)

        def step(diag):
            for h in range(FOX_H):
                ws = slice(h * HW, (h + 1) * HW)
                kh = k_ref[:, ws]
                p = jnp.exp(_dot_nt(q_ref[:, ws], kh))
                if diag:
                    p = _causal(p, 0.0)
                hi, lo = _split(p * _dot_nt(d_ref[:, ws], v_ref[:, ws]))
                acc[h] += _dot(hi, kh) + _dot(lo, kh)

        @pl.when(ki < qi)
        def _():
            step(False)

        @pl.when(ki == qi)
        def _():
            step(True)
            for h in range(FOX_H):
                dq_ref[:, h * FOX_DH:(h + 1) * FOX_DH] = (acc[h][:, :FOX_DH] * (FOX_DH ** -0.5)).astype(MXU)

    qs, ks = _wide_specs(tq, nq, lambda i, j: i, lambda i, j: jnp.minimum(i, j))
    return _call(
        body, name="fox_bwd_dq", grid=(T // S, nq, nq),
        in_specs=[qs, ks, ks, qs],
        out_specs=pl.BlockSpec((tq, D), lambda b, i, j: (b * nq + i, 0)),
        out_shape=_sds((T, D), MXU),
        scratch=[pltpu.VMEM((FOX_H, tq, HW), F32)],
    )(qb, ka, va, da)


def fox2_dkv(qb, ka, va, da, *, S, tq=256):
    T = qb.shape[0]
    nq = S // tq

    def body(q_ref, k_ref, v_ref, d_ref, dk_ref, dv_ref, dck_ref, dk_acc, dv_acc):
        ki = pl.program_id(1)
        qi = pl.program_id(2)

        @pl.when(qi == 0)
        def _():
            dk_acc[...] = jnp.zeros_like(dk_acc)
            dv_acc[...] = jnp.zeros_like(dv_acc)

        def step(diag):
            for h in range(FOX_H):
                ws = slice(h * HW, (h + 1) * HW)
                qh = q_ref[:, ws]
                dh = d_ref[:, ws]
                pt = jnp.exp(_dot_nt(k_ref[:, ws], qh))
                if diag:
                    r = lax.broadcasted_iota(jnp.int32, pt.shape, 0)
                    c = lax.broadcasted_iota(jnp.int32, pt.shape, 1)
                    pt = jnp.where(r <= c, pt, 0.0)
                dv_acc[h] += _dot(pt.astype(MXU), dh)
                hi, lo = _split(pt * _dot_nt(v_ref[:, ws], dh))
                dk_acc[h] += _dot(hi, qh) + _dot(lo, qh)

        @pl.when(qi > ki)
        def _():
            step(False)

        @pl.when(qi == ki)
        def _():
            step(True)

        @pl.when(qi == nq - 1)
        def _():
            dck_ref[...] = jnp.zeros_like(dck_ref)
            for h in range(FOX_H):
                hs = slice(h * FOX_DH, (h + 1) * FOX_DH)
                dk_ref[:, hs] = dk_acc[h][:, :FOX_DH].astype(MXU)
                dv_ref[:, hs] = dv_acc[h][:, :FOX_DH].astype(MXU)
                dck_ref[:, h:h + 1] = dk_acc[h][:, COL_ONE:COL_ONE + 1]

    qs, ks = _wide_specs(tq, nq, lambda i, j: jnp.maximum(i, j), lambda i, j: i)
    ko = pl.BlockSpec((tq, D), lambda b, i, j: (b * nq + i, 0))
    return _call(
        body, name="fox_bwd_dkv", grid=(T // S, nq, nq),
        in_specs=[qs, ks, ks, qs],
        out_specs=[ko, ko, pl.BlockSpec((tq, 128), lambda b, i, j: (b * nq + i, 0))],
        out_shape=[_sds((T, D), MXU), _sds((T, D), MXU), _sds((T, 128), F32)],
        scratch=[pltpu.VMEM((FOX_H, tq, HW), F32), pltpu.VMEM((FOX_H, tq, HW), F32)],
    )(qb, ka, va, da)


def fox_fin(dck, fl, bf, *, S, tt=256):
    T = fl.shape[0]
    nb = S // tt
    nblk = T // tt

    def body(dck_ref, fl_ref, bf_ref, dfl_ref, dbf_ref, carry):
        i = pl.program_id(0)

        @pl.when(i == 0)
        def _():
            dbf_ref[...] = jnp.zeros_like(dbf_ref)

        @pl.when((i % nb) == 0)
        def _():
            carry[...] = jnp.zeros_like(carry)

        lane = lax.broadcasted_iota(jnp.int32, (tt, 128), 1)
        dc = jnp.where(lane < FOX_H, -dck_ref[...], 0.0)
        dlf = _dot_hi(_tri(tt, upper=True), dc) + carry[...]
        carry[...] = dlf[0:1, :]
        dfl = dlf * _sigmoid(-(fl_ref[...] + bf_ref[...]))
        dfl_ref[...] = dfl.astype(MXU)
        dbf_ref[...] += jnp.sum(dfl, axis=0, keepdims=True)

    rev = pl.BlockSpec((tt, 128), lambda i: (nblk - 1 - i, 0))
    row = pl.BlockSpec((1, 128), lambda i: (0, 0))
    return _call(
        body, name="fox_fin", grid=(nblk,),
        in_specs=[rev, rev, row],
        out_specs=[rev, row],
        out_shape=[_sds((T, 128), MXU), _sds((1, 128), F32)],
        scratch=[pltpu.VMEM((1, 128), F32)],
    )(dck, fl, bf)


def lb_fwd(logits):
    def body(l_ref, lb_ref):
        lv = l_ref[...]
        e = jnp.exp(lv - jnp.max(lv, axis=0, keepdims=True))
        p = e / jnp.sum(e, axis=0, keepdims=True)
        lb_ref[...] = p[1:2, :] + p[2:3, :]

    return _call(body, name="lb_fwd", grid=(1,),
                 in_specs=[pl.BlockSpec((DEPTH, D), lambda i: (0, 0))],
                 out_specs=pl.BlockSpec((1, D), lambda i: (0, 0)),
                 out_shape=_sds((1, D), F32))(logits)


def lb_bwd(logits, dlb):
    def body(l_ref, d_ref, o_ref):
        lv = l_ref[...]
        e = jnp.exp(lv - jnp.max(lv, axis=0, keepdims=True))
        p = e / jnp.sum(e, axis=0, keepdims=True)
        lb = p[1:2, :] + p[2:3, :]
        row = lax.broadcasted_iota(jnp.int32, (DEPTH, D), 0)
        sel = ((row == 1) | (row == 2)).astype(F32)
        o_ref[...] = p * (sel - lb) * d_ref[...]

    return _call(body, name="lb_bwd", grid=(1,),
                 in_specs=[pl.BlockSpec((DEPTH, D), lambda i: (0, 0)), pl.BlockSpec((1, D), lambda i: (0, 0))],
                 out_specs=pl.BlockSpec((DEPTH, D), lambda i: (0, 0)),
                 out_shape=_sds((DEPTH, D), F32))(logits, dlb)


def _hgrn_gates(qr, fr, lb):
    sg = _sigmoid(fr)
    sneg = _sigmoid(-fr)
    f = lb + (1.0 - lb) * sg
    kk = (1.0 - lb) * sneg
    G = _dot_hi(_tri(HG_C), jnp.log(f))
    eG = jnp.exp(G)
    einv = jnp.exp(-G)
    elast = jnp.exp(G[HG_C - 1:HG_C, :] - G)
    q = qr * _sigmoid(qr)
    return dict(q=q, kk=kk, f=f, sg=sg, sneg=sneg, eG=eG, einv=einv, elast=elast,
                qg=q * eG, kinv=kk * einv, khat=kk * elast, glast=jnp.exp(G[HG_C - 1:HG_C, :]))


def _tril_mask(x):
    r = lax.broadcasted_iota(jnp.int32, x.shape, 0)
    c = lax.broadcasted_iota(jnp.int32, x.shape, 1)
    return jnp.where(r >= c, x, 0.0)


def hgrn_fwd(p, lb, ng, *, S, R=256):
    T = p.shape[0]
    R = min(R, S)
    nr = S // R
    ncr = R // HG_C

    def body(q_ref, f_ref, v_ref, gt_ref, lb_ref, ng_ref, y_ref, o_ref, st_ref, st):
        @pl.when(pl.program_id(1) == 0)
        def _():
            st[...] = jnp.zeros_like(st)

        lbv = lb_ref[...]
        for ch in range(ncr):
            rows = slice(ch * HG_C, (ch + 1) * HG_C)
            gt = _hgrn_gates(q_ref[rows, :], f_ref[rows, :], lbv)
            for h in range(HG_H):
                hs = slice(h * HG_DK, (h + 1) * HG_DK)
                sp = st[h]
                st_ref[ch, h] = sp
                qg = gt["qg"][:, hs].astype(MXU)
                vh = v_ref[rows, hs].astype(MXU)
                A = _tril_mask(_dot_nt(qg, gt["kinv"][:, hs].astype(MXU)))
                o_ref[rows, hs] = _dot_nt(qg, sp.astype(MXU)) + _dot(A.astype(MXU), vh)
                st[h] = sp * gt["glast"][:, hs] + _dot_tn(vh, gt["khat"][:, hs].astype(MXU))
        gate = gt_ref[...]
        sgate = gate * _sigmoid(gate)
        for h in range(HG_H):
            hs = slice(h * HG_DK, (h + 1) * HG_DK)
            oh = o_ref[:, hs]
            r = lax.rsqrt(jnp.mean(oh * oh, axis=-1, keepdims=True) + RMS_EPS)
            y_ref[:, hs] = (oh * r * ng_ref[:, hs] * sgate[:, hs]).astype(MXU)

    col = lambda c: pl.BlockSpec((R, D), lambda b, i: (b * nr + i, c))
    row = pl.BlockSpec((1, D), lambda b, i: (0, 0))
    return _call(
        body, name="hgrn_fwd", grid=(T // S, nr),
        in_specs=[col(0), col(1), col(2), col(3), row, row],
        out_specs=[col(0), col(0),
                   pl.BlockSpec((ncr, HG_H, HG_DK, HG_DK), lambda b, i: (b * nr + i, 0, 0, 0))],
        out_shape=[_sds((T, D), MXU), _sds((T, D), F32), _sds((T // HG_C, HG_H, HG_DK, HG_DK), F32)],
        scratch=[pltpu.VMEM((HG_H, HG_DK, HG_DK), F32)],
    )(p, p, p, p, lb, ng)


def hgrn_bwd(p, o, dyo, states, lb, ng, *, S, R=256):
    T = p.shape[0]
    R = min(R, S)
    nr = S // R
    ncr = R // HG_C

    def body(q_ref, f_ref, v_ref, gt_ref, o_ref, dy_ref, st_ref, lb_ref, ng_ref,
             dp_ref, dlb_ref, dng_ref, dst, do_buf, dG_buf, dqb, dkb):
        b = pl.program_id(0)
        i = pl.program_id(1)

        @pl.when(i == 0)
        def _():
            dst[...] = jnp.zeros_like(dst)

        @pl.when((b == 0) & (i == 0))
        def _():
            dlb_ref[...] = jnp.zeros_like(dlb_ref)
            dng_ref[...] = jnp.zeros_like(dng_ref)

        lbv = lb_ref[...]
        gate = gt_ref[...]
        sg_gate = _sigmoid(gate)
        silu_gate = gate * sg_gate
        for h in range(HG_H):
            hs = slice(h * HG_DK, (h + 1) * HG_DK)
            oh = o_ref[:, hs]
            r = lax.rsqrt(jnp.mean(oh * oh, axis=-1, keepdims=True) + RMS_EPS)
            ohat = oh * r
            dyh = dy_ref[:, hs]
            ngh = ng_ref[:, hs]
            dng_ref[:, hs] += jnp.sum(dyh * silu_gate[:, hs] * ohat, axis=0, keepdims=True)
            dp_ref[:, 3 * D + h * HG_DK:3 * D + (h + 1) * HG_DK] = (
                dyh * ohat * ngh * (sg_gate[:, hs] * (1.0 + gate[:, hs] * (1.0 - sg_gate[:, hs])))).astype(MXU)
            dn = dyh * ngh * silu_gate[:, hs]
            do_buf[:, hs] = r * (dn - ohat * jnp.mean(dn * ohat, axis=-1, keepdims=True))

        lastrow = lax.broadcasted_iota(jnp.int32, (HG_C, HG_DK), 0) == HG_C - 1
        for ch in reversed(range(ncr)):
            rows = slice(ch * HG_C, (ch + 1) * HG_C)
            qr = q_ref[rows, :]
            gt = _hgrn_gates(qr, f_ref[rows, :], lbv)
            for h in range(HG_H):
                hs = slice(h * HG_DK, (h + 1) * HG_DK)
                sp = st_ref[ch, h]
                ds = dst[h]
                qg32, kinv32, khat32 = gt["qg"][:, hs], gt["kinv"][:, hs], gt["khat"][:, hs]
                qg, kinv, khat = qg32.astype(MXU), kinv32.astype(MXU), khat32.astype(MXU)
                vh = v_ref[rows, hs].astype(MXU)
                doh = do_buf[rows, hs].astype(MXU)
                dsb = ds.astype(MXU)
                A = _tril_mask(_dot_nt(qg, kinv)).astype(MXU)
                dA = _tril_mask(_dot_nt(doh, vh)).astype(MXU)
                dqg = _dot(doh, sp.astype(MXU)) + _dot(dA, kinv)
                dkinv = _dot_tn(dA, qg)
                dp_ref[rows, 2 * D + h * HG_DK:2 * D + (h + 1) * HG_DK] = (
                    _dot_tn(A, doh) + _dot_nt(khat, dsb)).astype(MXU)
                dkhat = _dot(vh, dsb)
                glast = gt["glast"][:, hs]
                qg32, kinv32, khat32 = qg.astype(F32), kinv.astype(F32), khat.astype(F32)
                extra = (glast * jnp.sum(dsb.astype(F32) * sp.astype(MXU).astype(F32), axis=0, keepdims=True)
                         + jnp.sum(dkhat * khat32, axis=0, keepdims=True))
                dst[h] = ds * glast + _dot_tn(doh, qg)
                dG = dqg * qg32 - dkinv * kinv32 - dkhat * khat32
                dG_buf[:, hs] = dG + jnp.where(lastrow, extra, 0.0)
                dqb[:, hs] = dqg * gt["eG"][:, hs]
                dkb[:, hs] = dkinv * gt["einv"][:, hs] + dkhat * gt["elast"][:, hs]
            dg = _dot_hi(_tri(HG_C, upper=True), dG_buf[...])
            dk = dkb[...]
            sneg, f = gt["sneg"], gt["f"]
            c1 = (1.0 - lbv) * gt["sg"] * sneg
            dp_ref[rows, D:2 * D] = (dg * c1 / f - dk * c1).astype(MXU)
            dlb_ref[...] += jnp.sum(dg * sneg / f - dk * sneg, axis=0, keepdims=True)
            sq = _sigmoid(qr)
            dp_ref[rows, 0:D] = (dqb[...] * (sq * (1.0 + qr * (1.0 - sq)))).astype(MXU)

    rev = lambda b, i: b * nr + nr - 1 - i
    col = lambda c: pl.BlockSpec((R, D), lambda b, i: (rev(b, i), c))
    row = pl.BlockSpec((1, D), lambda b, i: (0, 0))
    return _call(
        body, name="hgrn_bwd", grid=(T // S, nr),
        in_specs=[col(0), col(1), col(2), col(3), col(0), col(0),
                  pl.BlockSpec((ncr, HG_H, HG_DK, HG_DK), lambda b, i: (rev(b, i), 0, 0, 0)), row, row],
        out_specs=[pl.BlockSpec((R, 4 * D), lambda b, i: (rev(b, i), 0)), row, row],
        out_shape=[_sds((T, 4 * D), MXU), _sds((1, D), F32), _sds((1, D), F32)],
        scratch=[pltpu.VMEM((HG_H, HG_DK, HG_DK), F32), pltpu.VMEM((R, D), F32), pltpu.VMEM((HG_C, D), F32),
                 pltpu.VMEM((HG_C, D), F32), pltpu.VMEM((HG_C, D), F32)],
    )(p, p, p, p, o, dyo, states, lb, ng)


def _mm_tn_stack(a, b, *, name, G, M, N, tk, stack):
    E, e, buf = stack
    T = a.shape[-2]

    def spec(arr, width):
        if arr.ndim == 3:
            return pl.BlockSpec((None, tk, width), lambda g, k: (g, k, 0))
        return pl.BlockSpec((tk, width), lambda g, k: (k, 0))

    def body(*refs):
        a_ref, b_ref, o_ref = refs[0], refs[1], refs[-1]

        @pl.when(pl.program_id(1) == 0)
        def _():
            o_ref[...] = jnp.zeros_like(o_ref)

        o_ref[...] += _dot_tn(a_ref[...], b_ref[...])

    in_specs = [spec(a, M), spec(b, N)]
    args = [a, b]
    aliases = {}
    if buf is not None:
        in_specs.append(pl.BlockSpec(memory_space=pl.ANY))
        args.append(buf)
        aliases = {2: 0}
    return pl.pallas_call(
        body, name=name, grid=(G, T // tk), in_specs=in_specs,
        out_specs=pl.BlockSpec((None, None, M, N), lambda g, k: (g, e, 0, 0)),
        out_shape=_sds((G, E, M, N), F32), input_output_aliases=aliases,
        compiler_params=pltpu.CompilerParams(dimension_semantics=("arbitrary", "arbitrary"),
                                             vmem_limit_bytes=VMEM_LIMIT))(*args)


MESH = pl.DeviceIdType.MESH
ANY = pl.BlockSpec(memory_space=pl.ANY)


def _pos():
    return lax.axis_index("x"), lax.axis_index("y"), lax.axis_index("c")


def _other_chips(x, y):
    return [(1 - x, y), (x, 1 - y), (1 - x, 1 - y)]


def _comm_call(body, *, name, args, out_shape, n_sem):
    return pl.pallas_call(
        body, name=name, in_specs=[ANY] * len(args), out_specs=[ANY] * len(out_shape), out_shape=out_shape,
        scratch_shapes=[pltpu.SemaphoreType.DMA((n_sem,)), pltpu.SemaphoreType.DMA((n_sem,)),
                        pltpu.SemaphoreType.DMA((len(args),))],
    )(*args)


def all_gather_chips(xs):
    n = len(xs)

    def body(*refs):
        x_refs, o_refs = refs[:n], refs[n:2 * n]
        ssem, rsem, lsem = refs[2 * n:]
        x, y, c = _pos()
        me = 2 * x + y
        chips = _other_chips(x, y)
        sib = (x, y, 1 - c)

        def rc(src, dst, idx, dev):
            return pltpu.make_async_remote_copy(src_ref=src, dst_ref=dst, send_sem=ssem.at[idx], recv_sem=rsem.at[idx],
                                                device_id=dev, device_id_type=MESH)

        via = jnp.where(c == 0, 2 * (1 - x) + y, 2 * x + (1 - y))
        to = (jnp.where(c == 0, x, 1 - x), jnp.where(c == 0, 1 - y, y), c)
        local, started = [], []
        for t in range(n):
            hr = xs[t].shape[0] // 2
            mine = pl.ds(c * hr, hr)
            cp = pltpu.make_async_copy(x_refs[t], o_refs[t].at[me], lsem.at[t])
            cp.start()
            local.append(cp)
            for k, (cx, cy) in enumerate(chips[:2]):
                cp = rc(x_refs[t].at[mine], o_refs[t].at[me, mine], 6 * t + k, (cx, cy, c))
                cp.start()
                started.append(cp)
        for t in range(n):
            hr = xs[t].shape[0] // 2
            mine = pl.ds(c * hr, hr)
            for k, (cx, cy) in enumerate(chips[:2]):
                landed = o_refs[t].at[2 * cx + cy, mine]
                rc(landed, landed, 6 * t + k, (cx, cy, c)).wait_recv()
            passed = o_refs[t].at[via, mine]
            cp = rc(passed, passed, 6 * t + 2, to)
            cp.start()
            started.append(cp)
            for k, (cx, cy) in enumerate(chips[:2]):
                landed = o_refs[t].at[2 * cx + cy, mine]
                cp = rc(landed, landed, 6 * t + 3 + k, sib)
                cp.start()
                started.append(cp)
        for t in range(n):
            hr = xs[t].shape[0] // 2
            mine = pl.ds(c * hr, hr)
            cx, cy = chips[2]
            landed = o_refs[t].at[2 * cx + cy, mine]
            rc(landed, landed, 6 * t + 2, to).wait_recv()
            cp = rc(landed, landed, 6 * t + 5, sib)
            cp.start()
            started.append(cp)
        for t in range(n):
            hr = xs[t].shape[0] // 2
            theirs = pl.ds((1 - c) * hr, hr)
            for k, (cx, cy) in enumerate(chips):
                other = o_refs[t].at[2 * cx + cy, theirs]
                rc(other, other, 6 * t + 3 + k, sib).wait_recv()
        for cp in started:
            cp.wait_send()
        for cp in local:
            cp.wait()

    outs = _comm_call(body, name="all_gather_chips", args=list(xs),
                      out_shape=[_sds((NSH,) + a.shape, a.dtype) for a in xs], n_sem=6 * n)
    return list(outs)


def sibling_half_exchange(gs):
    n = len(gs)

    def body(*refs):
        g_refs, o_refs = refs[:n], refs[n:2 * n]
        ssem, rsem, _ = refs[2 * n:]
        x, y, c = _pos()
        cps = []
        for t in range(n):
            hr = gs[t].shape[1] // 2
            for j in range(NSH):
                cp = pltpu.make_async_remote_copy(
                    src_ref=g_refs[t].at[j, pl.ds((1 - c) * hr, hr)], dst_ref=o_refs[t].at[j],
                    send_sem=ssem.at[NSH * t + j], recv_sem=rsem.at[NSH * t + j],
                    device_id=(x, y, 1 - c), device_id_type=MESH)
                cp.start()
                cps.append(cp)
        for cp in cps:
            cp.wait()

    outs = _comm_call(body, name="sibling_half_exchange", args=list(gs),
                      out_shape=[_sds((NSH, g.shape[1] // 2, g.shape[2]), g.dtype) for g in gs], n_sem=NSH * n)
    return list(outs)


def chip_scatter(ss):
    n = len(ss)

    def body(*refs):
        s_refs, o_refs = refs[:n], refs[n:2 * n]
        ssem, rsem, _ = refs[2 * n:]
        x, y, c = _pos()
        cps = []
        for t in range(n):
            for k, (cx, cy) in enumerate(_other_chips(x, y)):
                cp = pltpu.make_async_remote_copy(
                    src_ref=s_refs[t].at[2 * cx + cy], dst_ref=o_refs[t].at[k],
                    send_sem=ssem.at[3 * t + k], recv_sem=rsem.at[3 * t + k],
                    device_id=(cx, cy, c), device_id_type=MESH)
                cp.start()
                cps.append(cp)
        for cp in cps:
            cp.wait()

    outs = _comm_call(body, name="chip_scatter", args=list(ss),
                      out_shape=[_sds((3,) + s.shape[1:], s.dtype) for s in ss], n_sem=3 * n)
    return list(outs)


def sibling_exchange(rs):
    n = len(rs)

    def body(*refs):
        r_refs, o_refs = refs[:n], refs[n:2 * n]
        ssem, rsem, _ = refs[2 * n:]
        x, y, c = _pos()
        cps = []
        for t in range(n):
            cp = pltpu.make_async_remote_copy(
                src_ref=r_refs[t], dst_ref=o_refs[t], send_sem=ssem.at[t], recv_sem=rsem.at[t],
                device_id=(x, y, 1 - c), device_id_type=MESH)
            cp.start()
            cps.append(cp)
        for cp in cps:
            cp.wait()

    outs = _comm_call(body, name="sibling_exchange", args=list(rs),
                      out_shape=[_sds(r.shape, r.dtype) for r in rs], n_sem=n)
    return list(outs)


def all_gather_devices(v):
    def body(v_ref, o_ref, ssem, rsem, lsem):
        x, y, c = _pos()
        me = 4 * x + 2 * y + c
        loc = pltpu.make_async_copy(v_ref, o_ref.at[me], lsem.at[0])
        loc.start()
        cps = []
        k = 0
        for fx in range(2):
            for fy in range(2):
                for fc in range(2):
                    if fx == fy == fc == 0:
                        continue
                    cp = pltpu.make_async_remote_copy(
                        src_ref=v_ref, dst_ref=o_ref.at[me], send_sem=ssem.at[k], recv_sem=rsem.at[k],
                        device_id=(x ^ fx, y ^ fy, c ^ fc), device_id_type=MESH)
                    cp.start()
                    src = 4 * (x ^ fx) + 2 * (y ^ fy) + (c ^ fc)
                    cps.append((cp, o_ref.at[src], k))
                    k += 1
        for cp, landed, k in cps:
            cp.wait_send()
            pltpu.make_async_remote_copy(
                src_ref=landed, dst_ref=landed, send_sem=ssem.at[k], recv_sem=rsem.at[k],
                device_id=(x, y, c), device_id_type=MESH).wait_recv()
        loc.wait()

    return _comm_call(body, name="all_gather_devices", args=[v],
                      out_shape=[_sds((8,) + v.shape, v.dtype)], n_sem=7)[0]


def _call_sp(body, *, name, grid, in_specs, out_specs, out_shape, pos, args):
    return pl.pallas_call(
        body, name=name,
        grid_spec=pltpu.PrefetchScalarGridSpec(num_scalar_prefetch=1, grid=grid, in_specs=in_specs,
                                               out_specs=out_specs),
        out_shape=out_shape,
        compiler_params=pltpu.CompilerParams(dimension_semantics=("arbitrary",) * len(grid),
                                             vmem_limit_bytes=VMEM_LIMIT))(pos, *args)


def _rows_tile(r):
    for t in (512, 256, 128, 64, 32, 16, 8):
        if r % t == 0:
            return t
    raise ValueError(r)


def pair_sum(g, r, pos):
    _, R, C = g.shape
    hr = R // 2
    tr = _rows_tile(hr)
    nbh = hr // tr

    def body(p_ref, g_ref, r_ref, o_ref):
        o_ref[...] = (g_ref[...] + r_ref[...]).astype(MXU)

    return _call_sp(
        body, name="pair_sum", grid=(NSH, nbh), pos=pos, args=[g, r],
        in_specs=[pl.BlockSpec((None, tr, C), lambda j, i, p: (j, p[0] * nbh + i, 0)),
                  pl.BlockSpec((None, tr, C), lambda j, i, p: (j, i, 0))],
        out_specs=pl.BlockSpec((None, tr, C), lambda j, i, p: (j, i, 0)),
        out_shape=_sds((NSH, hr, C), MXU))


def reduce_own(g, r_sib, r_ici, pos):
    _, R, C = g.shape
    hr = R // 2
    tr = _rows_tile(hr)
    nbh = hr // tr

    def body(p_ref, g_ref, rs_ref, ri_ref, o_ref):
        s = g_ref[...] + rs_ref[...]
        for k in range(3):
            s = s + ri_ref[k].astype(F32)
        o_ref[...] = s

    return _call_sp(
        body, name="reduce_own", grid=(nbh,), pos=pos, args=[g, r_sib, r_ici],
        in_specs=[pl.BlockSpec((None, tr, C), lambda i, p: (p[1], p[0] * nbh + i, 0)),
                  pl.BlockSpec((None, tr, C), lambda i, p: (p[1], i, 0)),
                  pl.BlockSpec((3, tr, C), lambda i, p: (0, i, 0))],
        out_specs=pl.BlockSpec((tr, C), lambda i, p: (i, 0)),
        out_shape=_sds((hr, C), F32))


def _adamw_math(w, g, m, v):
    m = ADAM_B1 * m + (1.0 - ADAM_B1) * g
    v = ADAM_B2 * v + (1.0 - ADAM_B2) * (g * g)
    m_hat = m / (1.0 - ADAM_B1 ** ADAM_STEP)
    v_hat = v / (1.0 - ADAM_B2 ** ADAM_STEP)
    delta = -ADAM_LR * (m_hat / (jnp.sqrt(v_hat) + ADAM_EPS) + ADAM_WD * w)
    return delta, m, v


def adamw_halves(w, m, v, ga, gb, pos):
    R, C = w.shape
    hr = R // 2
    tr = _rows_tile(hr)
    nbh = hr // tr

    def body(p_ref, w_ref, m_ref, v_ref, ga_ref, gb_ref, g_ref, d_ref, mo_ref, vo_ref):
        mine = (pl.program_id(0) // nbh) == p_ref[0]
        g = jnp.where(mine, ga_ref[...], gb_ref[...])
        g_ref[...] = g
        d_ref[...], mo_ref[...], vo_ref[...] = _adamw_math(w_ref[...], g, m_ref[...], v_ref[...])

    blk = pl.BlockSpec((tr, C), lambda i, p: (i, 0))
    return _call_sp(
        body, name="adamw_halves", grid=(R // tr,), pos=pos, args=[w, m, v, ga, gb],
        in_specs=[blk, blk, blk,
                  pl.BlockSpec((tr, C), lambda i, p: (jnp.where(i // nbh == p[0], i % nbh, 0), 0)),
                  pl.BlockSpec((tr, C), lambda i, p: (jnp.where(i // nbh == p[0], 0, i % nbh), 0))],
        out_specs=[blk, blk, blk, blk],
        out_shape=[_sds((R, C), F32)] * 4)


def adamw_sum(gall, w, m, v):
    n, R, C = gall.shape

    def body(ga_ref, w_ref, m_ref, v_ref, g_ref, d_ref, mo_ref, vo_ref):
        g = ga_ref[0]
        for k in range(1, n):
            g = g + ga_ref[k]
        g_ref[...] = g
        d_ref[...], mo_ref[...], vo_ref[...] = _adamw_math(w_ref[...], g, m_ref[...], v_ref[...])

    blk = pl.BlockSpec((R, C), lambda i: (0, 0))
    return _call(body, name="adamw_sum", grid=(1,),
                 in_specs=[pl.BlockSpec((n, R, C), lambda i: (0, 0, 0)), blk, blk, blk],
                 out_specs=[blk, blk, blk, blk], out_shape=[_sds((R, C), F32)] * 4)(gall, w, m, v)


_WEIGHTS = ['ffn_norm', 'ffn_w_gate', 'ffn_w_up', 'ffn_w_down', 'mix_norm', 'final_norm', 'conv_w_in', 'conv_b_in',
            'conv_dw', 'conv_dw_b', 'conv_ln_g', 'conv_ln_b', 'conv_w_out', 'fox_w_in', 'fox_b_f', 'fox_w_out',
            'hgrn_w_in', 'hgrn_lb_logits', 'hgrn_norm', 'hgrn_w_out', 'pool_w', 'pool_scale']
_BIG = ['ffn_w_gate', 'ffn_w_up', 'ffn_w_down', 'conv_w_in', 'conv_w_out', 'fox_w_in', 'fox_w_out',
        'hgrn_w_in', 'hgrn_w_out', 'pool_w']
_SHARDED_SMALL = ['ffn_norm', 'conv_dw', 'hgrn_norm', 'pool_scale']
_REPLICATED = ['mix_norm', 'final_norm', 'conv_b_in', 'conv_dw_b', 'conv_ln_g', 'conv_ln_b', 'fox_b_f', 'hgrn_lb_logits']
FOX_N = 3 * D + FOX_H
FOX_NP = 3200
QS = D // NSH


def _pad_rows(a, rows):
    return jnp.pad(a, ((0, rows - a.shape[0]), (0, 0)))


def _pack_sharded_small(get):
    return jnp.concatenate([get('ffn_norm').reshape(8, -1), _pad_rows(get('conv_dw')[0], 32),
                            get('hgrn_norm'), get('pool_scale'), jnp.zeros((6, get('pool_scale').shape[1]), F32)], axis=0)


def _pack_replicated(get):
    return jnp.concatenate([get('mix_norm'), get('final_norm').reshape(1, D), get('conv_b_in').reshape(2, D),
                            get('conv_dw_b'), get('conv_ln_g'), get('conv_ln_b'),
                            jnp.pad(get('fox_b_f'), ((0, 0), (0, D - FOX_H))), get('hgrn_lb_logits'),
                            jnp.zeros((9, D), F32)], axis=0)


def _unpack_replicated(p):
    return {'mix_norm': p[0:4], 'final_norm': p[4], 'conv_b_in': p[5:7].reshape(1, 2 * D), 'conv_dw_b': p[7:8],
            'conv_ln_g': p[8:9], 'conv_ln_b': p[9:10], 'fox_b_f': p[10:11, :FOX_H], 'hgrn_lb_logits': p[11:15]}


def _unpack_sharded_small(p):
    return {'ffn_norm': p[0:8].reshape(DEPTH, 2, -1), 'conv_dw': p[8:8 + CONV_W][None],
            'hgrn_norm': p[40:41], 'pool_scale': p[41:42]}


def kernel(x, ffn_norm, ffn_w_gate, ffn_w_up, ffn_w_down, mix_norm, final_norm, conv_w_in, conv_b_in, conv_dw, conv_dw_b, conv_ln_g, conv_ln_b, conv_w_out, fox_w_in, fox_b_f, fox_w_out, hgrn_w_in, hgrn_lb_logits, hgrn_norm, hgrn_w_out, pool_w, pool_scale, loss_target, m_ffn_norm, m_ffn_w_gate, m_ffn_w_up, m_ffn_w_down, m_mix_norm, m_final_norm, m_conv_w_in, m_conv_b_in, m_conv_dw, m_conv_dw_b, m_conv_ln_g, m_conv_ln_b, m_conv_w_out, m_fox_w_in, m_fox_b_f, m_fox_w_out, m_hgrn_w_in, m_hgrn_lb_logits, m_hgrn_norm, m_hgrn_w_out, m_pool_w, m_pool_scale, v_ffn_norm, v_ffn_w_gate, v_ffn_w_up, v_ffn_w_down, v_mix_norm, v_final_norm, v_conv_w_in, v_conv_b_in, v_conv_dw, v_conv_dw_b, v_conv_ln_g, v_conv_ln_b, v_conv_w_out, v_fox_w_in, v_fox_b_f, v_fox_w_out, v_hgrn_w_in, v_hgrn_lb_logits, v_hgrn_norm, v_hgrn_w_out, v_pool_w, v_pool_scale):
    W = dict(ffn_norm=ffn_norm, ffn_w_gate=ffn_w_gate, ffn_w_up=ffn_w_up, ffn_w_down=ffn_w_down, mix_norm=mix_norm, final_norm=final_norm, conv_w_in=conv_w_in, conv_b_in=conv_b_in, conv_dw=conv_dw, conv_dw_b=conv_dw_b, conv_ln_g=conv_ln_g, conv_ln_b=conv_ln_b, conv_w_out=conv_w_out, fox_w_in=fox_w_in, fox_b_f=fox_b_f, fox_w_out=fox_w_out, hgrn_w_in=hgrn_w_in, hgrn_lb_logits=hgrn_lb_logits, hgrn_norm=hgrn_norm, hgrn_w_out=hgrn_w_out, pool_w=pool_w, pool_scale=pool_scale)
    M = dict(ffn_norm=m_ffn_norm, ffn_w_gate=m_ffn_w_gate, ffn_w_up=m_ffn_w_up, ffn_w_down=m_ffn_w_down, mix_norm=m_mix_norm, final_norm=m_final_norm, conv_w_in=m_conv_w_in, conv_b_in=m_conv_b_in, conv_dw=m_conv_dw, conv_dw_b=m_conv_dw_b, conv_ln_g=m_conv_ln_g, conv_ln_b=m_conv_ln_b, conv_w_out=m_conv_w_out, fox_w_in=m_fox_w_in, fox_b_f=m_fox_b_f, fox_w_out=m_fox_w_out, hgrn_w_in=m_hgrn_w_in, hgrn_lb_logits=m_hgrn_lb_logits, hgrn_norm=m_hgrn_norm, hgrn_w_out=m_hgrn_w_out, pool_w=m_pool_w, pool_scale=m_pool_scale)
    V = dict(ffn_norm=v_ffn_norm, ffn_w_gate=v_ffn_w_gate, ffn_w_up=v_ffn_w_up, ffn_w_down=v_ffn_w_down, mix_norm=v_mix_norm, final_norm=v_final_norm, conv_w_in=v_conv_w_in, conv_b_in=v_conv_b_in, conv_dw=v_conv_dw, conv_dw_b=v_conv_dw_b, conv_ln_g=v_conv_ln_g, conv_ln_b=v_conv_ln_b, conv_w_out=v_conv_w_out, fox_w_in=v_fox_w_in, fox_b_f=v_fox_b_f, fox_w_out=v_fox_w_out, hgrn_w_in=v_hgrn_w_in, hgrn_lb_logits=v_hgrn_lb_logits, hgrn_norm=v_hgrn_norm, hgrn_w_out=v_hgrn_w_out, pool_w=v_pool_w, pool_scale=v_pool_scale)

    px, py, pc = _pos()
    jme = 2 * px + py
    pos = jnp.stack([pc, jme]).astype(jnp.int32)
    S = x.shape[1]
    T = x.shape[0] * S
    x2 = x.reshape(T, D)
    tgt = loss_target.reshape(T, D)

    flat = lambda a: a.reshape(-1, a.shape[-1])
    gathered = all_gather_chips([flat(W[n]).astype(MXU) for n in _BIG] + [_pack_sharded_small(W.get)])
    G = dict(zip(_BIG, gathered[:-1]))
    small = gathered[-1].transpose(1, 0, 2).reshape(48, D)
    ffn_norm_f, conv_dw_f = small[0:8], small[8:40]
    hgrn_norm_f, pool_scale_f = small[40:41], small[41:42]
    wg_all = G['ffn_w_gate'].reshape(NSH, 2 * DEPTH, D, FS)
    wu_all = G['ffn_w_up'].reshape(NSH, 2 * DEPTH, D, FS)
    wd_all = G['ffn_w_down'].reshape(NSH, 2 * DEPTH, FS, D)
    conv_wi = G['conv_w_in']
    conv_wo = G['conv_w_out'].reshape(D, D)
    fox_full = jnp.pad(G['fox_w_in'].transpose(1, 0, 2).reshape(D, FOX_N), ((0, 0), (0, FOX_NP - FOX_N)))
    fox_w5 = fox_full.reshape(D, 5, FOX_NP // 5).transpose(1, 0, 2)
    fox_wf = fox_full[:, 3 * D:][None]
    fox_bf = jnp.pad(fox_b_f, ((0, 0), (0, 128 - FOX_H)))
    fox_wo = G['fox_w_out'].reshape(D, D)
    hgrn_wi = G['hgrn_w_in']
    hgrn_wo = G['hgrn_w_out'].reshape(D, D)
    pool_wf = G['pool_w'].reshape(NSH, 4, 64, POOL_G).transpose(1, 0, 2, 3).reshape(4, POOL_G, POOL_G)
    conv_bi = conv_b_in.reshape(NSH, 1, 2 * D // NSH)

    def ffn_f(xs, e):
        xo, h, u, sa, z = ffn_fwd(xs, ffn_norm_f[e:e + 1], wg_all, wu_all, wd_all, e, tm=min(1024, xs.shape[0]))
        return xo, (xs, h, u, sa, z)

    saved = []
    xs = x2
    lb = lb_fwd(hgrn_lb_logits)
    for i in range(DEPTH):
        xs, r0 = ffn_f(xs, 2 * i)
        gm = mix_norm[i:i + 1]
        xin = xs
        if i == 0:
            p, h = norm_mm(xin, gm, conv_wi, conv_bi, name="conv_in", out_dtype=F32)
            u2, u4 = conv_fwd_core(p, conv_dw_f, conv_dw_b, conv_ln_g, conv_ln_b, S=S)
            xs = mm_res(u4, conv_wo, xin, name="conv_out")
            rm = (xin, p, h, u2, u4)
        elif i == 1:
            p, h = norm_mm(xin, gm, fox_w5, None, name="fox_in", out_dtype=MXU)
            fl, _ = norm_mm(xin, gm, fox_wf, None, name="fox_in_f", out_dtype=F32)
            cq = fox_cum(fl, fox_bf, S=S)
            qa, ka, va = fox_prep(p, cq)
            o, o32, lse = fox2_fwd(qa, ka, va, S=S, tq=min(FOX_T, S), tk=min(FOX_T, S))
            xs = mm_res(o, fox_wo, xin, name="fox_out")
            rm = (xin, h, fl, qa, ka, va, o, o32, lse)
        elif i == 2:
            p, h = norm_mm(xin, gm, hgrn_wi, None, name="hgrn_in", out_dtype=F32)
            yh, oh, st = hgrn_fwd(p, lb, hgrn_norm_f, S=S)
            xs = mm_res(yh, hgrn_wo, xin, name="hgrn_out")
            rm = (xin, p, h, yh, oh, st)
        else:
            xs, mp = pool_fwd(xin, gm, pool_wf, pool_scale_f, S=S)
            rm = (xin, mp)
        xs, r1 = ffn_f(xs, 2 * i + 1)
        saved.append((r0, rm, r1))

    loss8, dx, d_final = loss_head(xs, final_norm.reshape(1, D), tgt)

    conv_wit, fox_w5t, hgrn_wit = (w.transpose(0, 2, 1) for w in (conv_wi, fox_w5, hgrn_wi))
    conv_wot, fox_wot, hgrn_wot = conv_wo.T, fox_wo.T, hgrn_wo.T
    gb = {'g': None, 'u': None, 'd': None}
    d_ffn_norm = [None] * (2 * DEPTH)
    d_mix_norm = [None] * DEPTH
    gbig = {}
    gsm = {}

    def ffn_b(dy, res, e):
        xin, h, u, sa, z = res
        dxo, da, db, dyh, dg = ffn_bwd_dx(xin, ffn_norm_f[e:e + 1], dy, u, sa, wg_all, wu_all, wd_all, e)
        tk = min(4096, xin.shape[0])
        gb['g'] = mm_tn(h, da, name="ffn_dwg", G=NSH, M=D, N=FS, tk=tk, stack=(2 * DEPTH, e, gb['g']))
        gb['u'] = mm_tn(h, db, name="ffn_dwu", G=NSH, M=D, N=FS, tk=tk, stack=(2 * DEPTH, e, gb['u']))
        gb['d'] = mm_tn(z, dyh, name="ffn_dwd", G=NSH, M=FS, N=D, tk=tk, stack=(2 * DEPTH, e, gb['d']))
        d_ffn_norm[e] = dg
        return dxo

    for i in reversed(range(DEPTH)):
        r0, rm, r1 = saved[i]
        dx = ffn_b(dx, r1, 2 * i + 1)
        gm = mix_norm[i:i + 1]
        if i == 0:
            xin, p, h, u2, u4 = rm
            du2, dyb, gsm['conv_ln_g'], gsm['conv_ln_b'], gsm['conv_dw_b'] = conv_bwd_rows(dx, conv_wot, u2, conv_ln_g, conv_ln_b)
            dp, gsm['conv_b_in'], ddw = conv_bwd_core(du2, p, conv_dw_f, S=S)
            gsm['conv_dw'] = ddw
            gbig['conv_w_in'] = mm_tn(h, dp, name="conv_dwin", G=NSH, M=D, N=2 * D // NSH, b_step=1)
            gbig['conv_w_out'] = mm_tn(u4, dyb, name="conv_dwout", G=NSH, M=QS, N=D, a_step=1)
            dx, d_mix_norm[i] = inproj_bwd(dp, conv_wit, xin, gm, dx, name="conv_in_bwd")
        elif i == 1:
            xin, h, fl, qa, ka, va, o, o32, lse = rm
            do, dyb = mm_nt(dx, fox_wot, name="fox_out_bwd")
            qb, da = fox2_prep_bwd(do, o32, lse, qa)
            dq = fox2_dq(qb, ka, va, da, S=S, tq=min(FOX_T, S))
            dk, dv, dck = fox2_dkv(qb, ka, va, da, S=S, tq=min(FOX_T, S))
            dfl, dbf = fox_fin(dck, fl, fox_bf, S=S)
            gsm['fox_b_f'] = dbf
            dp = jnp.concatenate([dq, dk, dv, dfl], axis=1)
            dw5 = mm_tn(h, dp, name="fox_dwin", G=5, M=D, N=FOX_NP // 5, b_step=1)
            dwf = dw5.transpose(1, 0, 2).reshape(D, FOX_NP)[:, :FOX_N]
            gbig['fox_w_in'] = dwf.reshape(D, NSH, FOX_N // NSH).transpose(1, 0, 2)
            gbig['fox_w_out'] = mm_tn(o, dyb, name="fox_dwout", G=NSH, M=QS, N=D, a_step=1)
            dx, d_mix_norm[i] = inproj_bwd(dp, fox_w5t, xin, gm, dx, name="fox_in_bwd")
        elif i == 2:
            xin, p, h, yh, oh, st = rm
            dyo, dyb = mm_nt(dx, hgrn_wot, name="hgrn_out_bwd")
            dp, dlb, gsm['hgrn_norm'] = hgrn_bwd(p, oh, dyo, st, lb, hgrn_norm_f, S=S)
            gsm['hgrn_lb_logits'] = lb_bwd(hgrn_lb_logits, dlb)
            gbig['hgrn_w_in'] = mm_tn(h, dp, name="hgrn_dwin", G=NSH, M=D, N=D, b_step=1)
            gbig['hgrn_w_out'] = mm_tn(yh, dyb, name="hgrn_dwout", G=NSH, M=QS, N=D, a_step=1)
            dx, d_mix_norm[i] = inproj_bwd(dp, hgrn_wit, xin, gm, dx, name="hgrn_in_bwd")
        else:
            xin, mp = rm
            dmc, dyp, gsm['pool_scale'] = pool_bwd_rows(dx, mp, pool_wf, pool_scale_f, S=S)
            dwp = mm_tn(mp, dyp, name="pool_dw", G=4, M=POOL_G, N=POOL_G, a_step=1, b_step=1)
            gbig['pool_w'] = dwp.reshape(4, NSH, 64, POOL_G).transpose(1, 0, 2, 3).reshape(NSH, 4 * 64, POOL_G)
            dx, d_mix_norm[i] = pool_bwd_core(dmc, xin, gm, dx, S=S)
        dx = ffn_b(dx, r0, 2 * i)

    gbig['ffn_w_gate'] = gb['g'].reshape(NSH, 2 * DEPTH * D, FS)
    gbig['ffn_w_up'] = gb['u'].reshape(NSH, 2 * DEPTH * D, FS)
    gbig['ffn_w_down'] = gb['d'].reshape(NSH, 2 * DEPTH * FS, D)

    gl = [gbig[n] for n in _BIG]
    r_sib = sibling_half_exchange(gl)
    s16 = [pair_sum(g, r, pos) for g, r in zip(gl, r_sib)]
    r_ici = chip_scatter(s16)
    red = [reduce_own(g, rs, ri, pos) for g, rs, ri in zip(gl, r_sib, r_ici)]
    oth = sibling_exchange(red)
    out = {}
    for n, ga, gb_ in zip(_BIG, red, oth):
        res = adamw_halves(flat(W[n]), flat(M[n]), flat(V[n]), ga, gb_, pos)
        out[n] = [r.reshape(W[n].shape) for r in res]

    gfull = {'mix_norm': jnp.concatenate(d_mix_norm, axis=0), 'final_norm': d_final,
             'conv_b_in': gsm['conv_b_in'], 'conv_dw_b': gsm['conv_dw_b'], 'conv_ln_g': gsm['conv_ln_g'],
             'conv_ln_b': gsm['conv_ln_b'], 'fox_b_f': gsm['fox_b_f'][:, :FOX_H], 'hgrn_lb_logits': gsm['hgrn_lb_logits'],
             'ffn_norm': jnp.concatenate(d_ffn_norm, axis=0), 'conv_dw': gsm['conv_dw'][None, :CONV_W],
             'hgrn_norm': gsm['hgrn_norm'], 'pool_scale': gsm['pool_scale']}
    gpack = jnp.concatenate([_pack_replicated(gfull.get), _pack_sharded_small(gfull.get)], axis=0)
    gall = all_gather_devices(gpack)
    rep = adamw_sum(gall[:, :24], _pack_replicated(W.get), _pack_replicated(M.get), _pack_replicated(V.get))
    rep = [_unpack_replicated(r) for r in rep]
    for n in _REPLICATED:
        out[n] = [r[n].reshape(W[n].shape) for r in rep]
    gsh = lax.dynamic_slice_in_dim(gall[:, 24:], jme * QS, QS, axis=2)
    shd = adamw_sum(gsh, _pack_sharded_small(W.get), _pack_sharded_small(M.get), _pack_sharded_small(V.get))
    shd = [_unpack_sharded_small(r) for r in shd]
    for n in _SHARDED_SMALL:
        out[n] = [r[n].reshape(W[n].shape) for r in shd]

    loss = lax.psum(loss8[0, 0], ("x", "y", "c"))
    res = [loss, dx.reshape(x.shape)]
    for k in range(4):
        res += [out[n][k] for n in _WEIGHTS]
    return tuple(res)
```

```python
import jax
import jax.numpy as jnp
from jax import lax
from jax.experimental import pallas as pl
from jax.experimental.pallas import tpu as pltpu

D = 1024
F = 2816
NSH = 4
FS = F // NSH
DEPTH = 4
RMS_EPS = 1e-6
LN_EPS = 1e-5
CONV_W = 31
HALO = 32
FOX_H = 16
FOX_DH = 64
HG_H = 8
HG_DK = 128
HG_C = 32
POOL_WIN = (2, 4, 8, 16)
POOL_G = 256
MXU = jnp.bfloat16
F32 = jnp.float32
VMEM_LIMIT = 52 * 1024 * 1024

ADAM_LR = 0.001
ADAM_B1 = 0.9
ADAM_B2 = 0.999
ADAM_EPS = 1e-08
ADAM_WD = 0.01
ADAM_STEP = 10


def _call(body, *, name, grid, in_specs, out_specs, out_shape, scratch=(), vmem=VMEM_LIMIT):
    return pl.pallas_call(
        body, name=name, grid=grid, in_specs=in_specs, out_specs=out_specs, out_shape=out_shape,
        scratch_shapes=list(scratch),
        compiler_params=pltpu.CompilerParams(dimension_semantics=("arbitrary",) * len(grid),
                                             vmem_limit_bytes=vmem))


def _dot(a, b):
    return jnp.dot(a, b, preferred_element_type=F32)


def _dot_nt(a, b):
    return lax.dot_general(a, b, (((1,), (1,)), ((), ())), preferred_element_type=F32)


def _dot_tn(a, b):
    return lax.dot_general(a, b, (((0,), (0,)), ((), ())), preferred_element_type=F32)


def _split(x):
    hi = x.astype(MXU)
    return hi, (x - hi.astype(F32)).astype(MXU)


def _sigmoid(x):
    return 1.0 / (1.0 + jnp.exp(-x))


def _rms(x, g):
    r = lax.rsqrt(jnp.mean(x * x, axis=-1, keepdims=True) + RMS_EPS)
    return x * r * g


def _rms_bwd(dh, x, g):
    r = lax.rsqrt(jnp.mean(x * x, axis=-1, keepdims=True) + RMS_EPS)
    xh = x * r
    dhg = dh * g
    dx = r * (dhg - xh * jnp.mean(dhg * xh, axis=-1, keepdims=True))
    return dx, jnp.sum(dh * xh, axis=0, keepdims=True)


def _sds(shape, dtype):
    return jax.ShapeDtypeStruct(shape, dtype)


def _wspec(w, e):
    if w.ndim == 3:
        return pl.BlockSpec((None,) + w.shape[1:], lambda i, j: (j, 0, 0))
    return pl.BlockSpec((None, None) + w.shape[2:], lambda i, j: (j, e, 0, 0))


def ffn_fwd(x, g, wg, wu, wd, e=0, *, tm=512):
    T = x.shape[0]

    def body(x_ref, g_ref, wg_ref, wu_ref, wd_ref, xo_ref, h_ref, u_ref, sa_ref, z_ref, acc_ref):
        j = pl.program_id(1)

        @pl.when(j == 0)
        def _():
            h_ref[...] = _rms(x_ref[...], g_ref[...]).astype(MXU)
            acc_ref[...] = jnp.zeros_like(acc_ref)

        h = h_ref[...]
        a = _dot(h, wg_ref[...])
        b = _dot(h, wu_ref[...])
        s = _sigmoid(a)
        sa = a * s
        u_ref[...] = (b * (s * (1.0 + a * (1.0 - s)))).astype(MXU)
        sa_ref[...] = sa.astype(MXU)
        z = (sa * b).astype(MXU)
        z_ref[...] = z
        acc_ref[...] += _dot(z, wd_ref[...])

        @pl.when(j == NSH - 1)
        def _():
            xo_ref[...] = x_ref[...] + 0.5 * acc_ref[...]

    return _call(
        body, name="ffn_fwd", grid=(T // tm, NSH),
        in_specs=[pl.BlockSpec((tm, D), lambda i, j: (i, 0)),
                  pl.BlockSpec((1, D), lambda i, j: (0, 0)),
                  _wspec(wg, e), _wspec(wu, e), _wspec(wd, e)],
        out_specs=[pl.BlockSpec((tm, D), lambda i, j: (i, 0)),
                   pl.BlockSpec((tm, D), lambda i, j: (i, 0)),
                   pl.BlockSpec((None, tm, FS), lambda i, j: (j, i, 0)),
                   pl.BlockSpec((None, tm, FS), lambda i, j: (j, i, 0)),
                   pl.BlockSpec((None, tm, FS), lambda i, j: (j, i, 0))],
        out_shape=[_sds((T, D), F32), _sds((T, D), MXU)] + [_sds((NSH, T, FS), MXU)] * 3,
        scratch=[pltpu.VMEM((tm, D), F32)],
    )(x, g, wg, wu, wd)


def ffn_bwd_dx(x, g, dy, u, sa, wg, wu, wd, e=0, *, tm=512):
    T = x.shape[0]

    def body(x_ref, g_ref, dy_ref, u_ref, sa_ref, wg_ref, wu_ref, wd_ref,
             dx_ref, da_ref, db_ref, dyh_ref, dg_ref):
        i = pl.program_id(0)
        j = pl.program_id(1)
        acc_ref = dx_ref

        @pl.when(j == 0)
        def _():
            dyh_ref[...] = (0.5 * dy_ref[...]).astype(MXU)
            acc_ref[...] = jnp.zeros_like(acc_ref)

        @pl.when((i == 0) & (j == 0))
        def _():
            dg_ref[...] = jnp.zeros_like(dg_ref)

        dz = _dot_nt(dyh_ref[...], wd_ref[...])
        da = (dz * u_ref[...].astype(F32)).astype(MXU)
        db = (dz * sa_ref[...].astype(F32)).astype(MXU)
        da_ref[...] = da
        db_ref[...] = db
        acc_ref[...] += _dot_nt(da, wg_ref[...]) + _dot_nt(db, wu_ref[...])

        @pl.when(j == NSH - 1)
        def _():
            dxn, dg = _rms_bwd(acc_ref[...], x_ref[...], g_ref[...])
            dx_ref[...] = dy_ref[...] + dxn
            dg_ref[...] += dg

    return _call(
        body, name="ffn_bwd_dx", grid=(T // tm, NSH),
        in_specs=[pl.BlockSpec((tm, D), lambda i, j: (i, 0)),
                  pl.BlockSpec((1, D), lambda i, j: (0, 0)),
                  pl.BlockSpec((tm, D), lambda i, j: (i, 0)),
                  pl.BlockSpec((None, tm, FS), lambda i, j: (j, i, 0)),
                  pl.BlockSpec((None, tm, FS), lambda i, j: (j, i, 0)),
                  _wspec(wg, e), _wspec(wu, e), _wspec(wd, e)],
        out_specs=[pl.BlockSpec((tm, D), lambda i, j: (i, 0)),
                   pl.BlockSpec((None, tm, FS), lambda i, j: (j, i, 0)),
                   pl.BlockSpec((None, tm, FS), lambda i, j: (j, i, 0)),
                   pl.BlockSpec((tm, D), lambda i, j: (i, 0)),
                   pl.BlockSpec((1, D), lambda i, j: (0, 0))],
        out_shape=[_sds((T, D), F32), _sds((NSH, T, FS), MXU), _sds((NSH, T, FS), MXU),
                   _sds((T, D), MXU), _sds((1, D), F32)],
    )(x, g, dy, u, sa, wg, wu, wd)


def mm_tn(a, b, *, name, G, M, N, a_step=0, b_step=0, tk=512, stack=None):
    T = a.shape[-2]
    if stack is not None:
        return _mm_tn_stack(a, b, name=name, G=G, M=M, N=N, tk=tk, stack=stack)

    def spec(arr, width, step):
        if arr.ndim == 3:
            return pl.BlockSpec((None, tk, width), lambda g, k: (g, k, 0))
        return pl.BlockSpec((tk, width), lambda g, k: (k, g * step))

    def body(a_ref, b_ref, o_ref):
        @pl.when(pl.program_id(1) == 0)
        def _():
            o_ref[...] = jnp.zeros_like(o_ref)

        o_ref[...] += _dot_tn(a_ref[...], b_ref[...])

    return _call(
        body, name=name, grid=(G, T // tk),
        in_specs=[spec(a, M, a_step), spec(b, N, b_step)],
        out_specs=pl.BlockSpec((None, M, N), lambda g, k: (g, 0, 0)),
        out_shape=_sds((G, M, N), F32),
    )(a, b)


def norm_mm(x, g, wb, bias, *, name, out_dtype, tm=1024):
    T = x.shape[0]
    tm = min(tm, T)
    G, _, ns = wb.shape
    has_bias = bias is not None

    def body(*refs):
        if has_bias:
            x_ref, g_ref, w_ref, bias_ref, p_ref, h_ref = refs
        else:
            x_ref, g_ref, w_ref, p_ref, h_ref = refs

        @pl.when(pl.program_id(1) == 0)
        def _():
            h_ref[...] = _rms(x_ref[...], g_ref[...]).astype(MXU)

        p = _dot(h_ref[...], w_ref[...])
        if has_bias:
            p = p + bias_ref[...]
        p_ref[...] = p.astype(out_dtype)

    in_specs = [pl.BlockSpec((tm, D), lambda i, j: (i, 0)),
                pl.BlockSpec((1, D), lambda i, j: (0, 0)),
                pl.BlockSpec((None, D, ns), lambda i, j: (j, 0, 0))]
    args = [x, g, wb]
    if has_bias:
        in_specs.append(pl.BlockSpec((None, 1, ns), lambda i, j: (j, 0, 0)))
        args.append(bias)
    return _call(
        body, name=name, grid=(T // tm, G), in_specs=in_specs,
        out_specs=[pl.BlockSpec((tm, ns), lambda i, j: (i, j)),
                   pl.BlockSpec((tm, D), lambda i, j: (i, 0))],
        out_shape=[_sds((T, G * ns), out_dtype), _sds((T, D), MXU)],
    )(*args)


def mm_res(y, w, x, *, name, tm=1024):
    T, K = y.shape
    tm = min(tm, T)

    def body(y_ref, w_ref, x_ref, o_ref):
        o_ref[...] = x_ref[...] + _dot(y_ref[...], w_ref[...])

    return _call(
        body, name=name, grid=(T // tm,),
        in_specs=[pl.BlockSpec((tm, K), lambda i: (i, 0)),
                  pl.BlockSpec((K, D), lambda i: (0, 0)),
                  pl.BlockSpec((tm, D), lambda i: (i, 0))],
        out_specs=pl.BlockSpec((tm, D), lambda i: (i, 0)),
        out_shape=_sds((T, D), F32),
    )(y, w, x)


def mm_nt(a, wt, *, name, tm=1024):
    T, K = a.shape
    tm = min(tm, T)
    N = wt.shape[1]
    w = wt

    def body(a_ref, w_ref, o_ref, ab_ref):
        ab = a_ref[...].astype(MXU)
        ab_ref[...] = ab
        o_ref[...] = _dot(ab, w_ref[...])

    return _call(
        body, name=name, grid=(T // tm,),
        in_specs=[pl.BlockSpec((tm, K), lambda i: (i, 0)),
                  pl.BlockSpec((K, N), lambda i: (0, 0))],
        out_specs=[pl.BlockSpec((tm, N), lambda i: (i, 0)),
                   pl.BlockSpec((tm, K), lambda i: (i, 0))],
        out_shape=[_sds((T, N), F32), _sds((T, K), MXU)],
    )(a, w)


def inproj_bwd(dp, wb, x, g, dres, *, name, tm=1024):
    T = x.shape[0]
    tm = min(tm, T)
    G, ns, _ = wb.shape

    def body(dp_ref, w_ref, x_ref, g_ref, dres_ref, dx_ref, dg_ref, acc_ref):
        i = pl.program_id(0)
        j = pl.program_id(1)

        @pl.when(j == 0)
        def _():
            acc_ref[...] = jnp.zeros_like(acc_ref)

        @pl.when((i == 0) & (j == 0))
        def _():
            dg_ref[...] = jnp.zeros_like(dg_ref)

        acc_ref[...] += _dot(dp_ref[...], w_ref[...])

        @pl.when(j == G - 1)
        def _():
            dxn, dg = _rms_bwd(acc_ref[...], x_ref[...], g_ref[...])
            dx_ref[...] = dres_ref[...] + dxn
            dg_ref[...] += dg

    return _call(
        body, name=name, grid=(T // tm, G),
        in_specs=[pl.BlockSpec((tm, ns), lambda i, j: (i, j)),
                  pl.BlockSpec((None, ns, D), lambda i, j: (j, 0, 0)),
                  pl.BlockSpec((tm, D), lambda i, j: (i, 0)),
                  pl.BlockSpec((1, D), lambda i, j: (0, 0)),
                  pl.BlockSpec((tm, D), lambda i, j: (i, 0))],
        out_specs=[pl.BlockSpec((tm, D), lambda i, j: (i, 0)),
                   pl.BlockSpec((1, D), lambda i, j: (0, 0))],
        out_shape=[_sds((T, D), F32), _sds((1, D), F32)],
        scratch=[pltpu.VMEM((tm, D), F32)],
    )(dp, wb, x, g, dres)


def loss_head(x, gf, tgt, *, tm=512):
    T = x.shape[0]

    def body(x_ref, g_ref, t_ref, loss_ref, dx_ref, dg_ref):
        @pl.when(pl.program_id(0) == 0)
        def _():
            loss_ref[...] = jnp.zeros_like(loss_ref)
            dg_ref[...] = jnp.zeros_like(dg_ref)

        xv = x_ref[...]
        gv = g_ref[...]
        e = _rms(xv, gv) - t_ref[...]
        loss_ref[...] += (0.5 / D) * jnp.sum(e * e)
        dxn, dg = _rms_bwd(e * (1.0 / D), xv, gv)
        dx_ref[...] = dxn
        dg_ref[...] += dg

    return _call(
        body, name="loss_head", grid=(T // tm,),
        in_specs=[pl.BlockSpec((tm, D), lambda i: (i, 0)),
                  pl.BlockSpec((1, D), lambda i: (0, 0)),
                  pl.BlockSpec((tm, D), lambda i: (i, 0))],
        out_specs=[pl.BlockSpec((8, 128), lambda i: (0, 0)),
                   pl.BlockSpec((tm, D), lambda i: (i, 0)),
                   pl.BlockSpec((1, D), lambda i: (0, 0))],
        out_shape=[_sds((8, 128), F32), _sds((T, D), F32), _sds((1, D), F32)],
    )(x, gf, tgt)


def _glu(p):
    return p[:, :D] * _sigmoid(p[:, D:])


def _ln_stats(u):
    mu = jnp.mean(u, axis=-1, keepdims=True)
    xc = u - mu
    rstd = lax.rsqrt(jnp.mean(xc * xc, axis=-1, keepdims=True) + LN_EPS)
    return xc * rstd, rstd


def conv_fwd_core(p, dw, dwb, lng, lnb, *, S, tt=256):
    T = p.shape[0]
    nb = S // tt
    r = tt // HALO

    def body(pc_ref, pp_ref, dw_ref, dwb_ref, lng_ref, lnb_ref, u2_ref, u4_ref, ubuf):
        first = (pl.program_id(0) % nb) == 0
        ubuf[0:HALO, :] = jnp.where(first, 0.0, _glu(pp_ref[...]))
        ubuf[HALO:, :] = _glu(pc_ref[...])
        for c in range(D // 128):
            cs = slice(c * 128, (c + 1) * 128)
            acc = jnp.zeros((tt, 128), F32)
            for k in range(CONV_W):
                acc = acc + dw_ref[k:k + 1, cs] * ubuf[k + 2:k + 2 + tt, cs]
            u2_ref[:, cs] = acc + dwb_ref[:, cs]
        xh, _ = _ln_stats(u2_ref[...])
        u3 = xh * lng_ref[...] + lnb_ref[...]
        u4_ref[...] = (u3 * _sigmoid(u3)).astype(MXU)

    row = pl.BlockSpec((1, D), lambda i: (0, 0))
    return _call(
        body, name="conv_fwd_core", grid=(T // tt,),
        in_specs=[pl.BlockSpec((tt, 2 * D), lambda i: (i, 0)),
                  pl.BlockSpec((HALO, 2 * D), lambda i: (jnp.maximum(i * r - 1, 0), 0)),
                  pl.BlockSpec((HALO, D), lambda i: (0, 0)), row, row, row],
        out_specs=[pl.BlockSpec((tt, D), lambda i: (i, 0)), pl.BlockSpec((tt, D), lambda i: (i, 0))],
        out_shape=[_sds((T, D), F32), _sds((T, D), MXU)],
        scratch=[pltpu.VMEM((tt + HALO, D), F32)],
    )(p, p, dw, dwb, lng, lnb)


def conv_bwd_rows(dy, wout, u2, lng, lnb, *, tm=512):
    T = dy.shape[0]

    def body(dy_ref, w_ref, u2_ref, lng_ref, lnb_ref, du2_ref, dyb_ref, dlng_ref, dlnb_ref, ddwb_ref):
        @pl.when(pl.program_id(0) == 0)
        def _():
            dlng_ref[...] = jnp.zeros_like(dlng_ref)
            dlnb_ref[...] = jnp.zeros_like(dlnb_ref)
            ddwb_ref[...] = jnp.zeros_like(ddwb_ref)

        dyb = dy_ref[...].astype(MXU)
        dyb_ref[...] = dyb
        du4 = _dot(dyb, w_ref[...])
        xh, rstd = _ln_stats(u2_ref[...])
        lng_v = lng_ref[...]
        u3 = xh * lng_v + lnb_ref[...]
        s = _sigmoid(u3)
        du3 = du4 * (s * (1.0 + u3 * (1.0 - s)))
        dlng_ref[...] += jnp.sum(du3 * xh, axis=0, keepdims=True)
        dlnb_ref[...] += jnp.sum(du3, axis=0, keepdims=True)
        dxh = du3 * lng_v
        du2 = rstd * (dxh - jnp.mean(dxh, axis=-1, keepdims=True)
                      - xh * jnp.mean(dxh * xh, axis=-1, keepdims=True))
        du2_ref[...] = du2
        ddwb_ref[...] += jnp.sum(du2, axis=0, keepdims=True)

    row = pl.BlockSpec((1, D), lambda i: (0, 0))
    blk = pl.BlockSpec((tm, D), lambda i: (i, 0))
    return _call(
        body, name="conv_bwd_rows", grid=(T // tm,),
        in_specs=[blk, pl.BlockSpec((D, D), lambda i: (0, 0)), blk, row, row],
        out_specs=[blk, blk, row, row, row],
        out_shape=[_sds((T, D), F32), _sds((T, D), MXU), _sds((1, D), F32), _sds((1, D), F32), _sds((1, D), F32)],
    )(dy, wout, u2, lng, lnb)


def conv_bwd_core(du2, p, dw, *, S, tt=256):
    T = p.shape[0]
    nb = S // tt
    r = tt // HALO
    last_halo = T // HALO - 1

    def body(dc_ref, dn_ref, pc_ref, pp_ref, dw_ref, dp_ref, dbin_ref, ddw_ref, ubuf, dbuf):
        i = pl.program_id(0)

        @pl.when(i == 0)
        def _():
            dbin_ref[...] = jnp.zeros_like(dbin_ref)
            ddw_ref[...] = jnp.zeros_like(ddw_ref)

        first = (i % nb) == 0
        last = (i % nb) == nb - 1
        ubuf[0:HALO, :] = jnp.where(first, 0.0, _glu(pp_ref[...]))
        ubuf[HALO:, :] = _glu(pc_ref[...])
        dbuf[0:tt, :] = dc_ref[...]
        dbuf[tt:, :] = jnp.where(last, 0.0, dn_ref[...])
        pc = pc_ref[...]
        for c in range(D // 128):
            cs = slice(c * 128, (c + 1) * 128)
            dcur = dbuf[0:tt, cs]
            du = jnp.zeros((tt, 128), F32)
            for k in range(CONV_W):
                ddw_ref[k:k + 1, cs] += jnp.sum(dcur * ubuf[k + 2:k + 2 + tt, cs], axis=0, keepdims=True)
                du = du + dw_ref[k:k + 1, cs] * dbuf[CONV_W - 1 - k:CONV_W - 1 - k + tt, cs]
            a = pc[:, c * 128:(c + 1) * 128]
            sb = _sigmoid(pc[:, D + c * 128:D + (c + 1) * 128])
            da = du * sb
            db = du * a * sb * (1.0 - sb)
            dp_ref[:, cs] = da.astype(MXU)
            dp_ref[:, D + c * 128:D + (c + 1) * 128] = db.astype(MXU)
            dbin_ref[:, cs] += jnp.sum(da, axis=0, keepdims=True)
            dbin_ref[:, D + c * 128:D + (c + 1) * 128] += jnp.sum(db, axis=0, keepdims=True)

    return _call(
        body, name="conv_bwd_core", grid=(T // tt,),
        in_specs=[pl.BlockSpec((tt, D), lambda i: (i, 0)),
                  pl.BlockSpec((HALO, D), lambda i: (jnp.minimum((i + 1) * r, last_halo), 0)),
                  pl.BlockSpec((tt, 2 * D), lambda i: (i, 0)),
                  pl.BlockSpec((HALO, 2 * D), lambda i: (jnp.maximum(i * r - 1, 0), 0)),
                  pl.BlockSpec((HALO, D), lambda i: (0, 0))],
        out_specs=[pl.BlockSpec((tt, 2 * D), lambda i: (i, 0)),
                   pl.BlockSpec((1, 2 * D), lambda i: (0, 0)),
                   pl.BlockSpec((HALO, D), lambda i: (0, 0))],
        out_shape=[_sds((T, 2 * D), MXU), _sds((1, 2 * D), F32), _sds((HALO, D), F32)],
        scratch=[pltpu.VMEM((tt + HALO, D), F32), pltpu.VMEM((tt + HALO, D), F32)],
    )(du2, du2, p, p, dw)


PH = 16


def _pool_cnt(i, nb, tt, win):
    pos = (i % nb) * tt + lax.broadcasted_iota(jnp.int32, (tt, 1), 0)
    return jnp.minimum(pos + 1, win).astype(F32)


def pool_fwd(x, g, wp, scale, *, S, tt=256):
    T = x.shape[0]
    nb = S // tt
    r = tt // PH

    def body(xc_ref, xp_ref, g_ref, wp_ref, sc_ref, xo_ref, m_ref, hbuf):
        i = pl.program_id(0)
        first = (i % nb) == 0
        gv = g_ref[...]
        hbuf[0:PH, :] = jnp.where(first, 0.0, _rms(xp_ref[...], gv))
        xc = xc_ref[...]
        hbuf[PH:, :] = _rms(xc, gv)
        for gi, win in enumerate(POOL_WIN):
            gs = slice(gi * POOL_G, (gi + 1) * POOL_G)
            acc = hbuf[PH:PH + tt, gs]
            for j in range(1, win):
                acc = acc + hbuf[PH - j:PH - j + tt, gs]
            m = (acc / _pool_cnt(i, nb, tt, win) - hbuf[PH:PH + tt, gs]).astype(MXU)
            m_ref[:, gs] = m
            xo_ref[:, gs] = xc[:, gs] + _dot(m, wp_ref[gi]) * sc_ref[:, gs]

    row = pl.BlockSpec((1, D), lambda i: (0, 0))
    blk = pl.BlockSpec((tt, D), lambda i: (i, 0))
    return _call(
        body, name="pool_fwd", grid=(T // tt,),
        in_specs=[blk, pl.BlockSpec((PH, D), lambda i: (jnp.maximum(i * r - 1, 0), 0)), row,
                  pl.BlockSpec((len(POOL_WIN), POOL_G, POOL_G), lambda i: (0, 0, 0)), row],
        out_specs=[blk, blk],
        out_shape=[_sds((T, D), F32), _sds((T, D), MXU)],
        scratch=[pltpu.VMEM((tt + PH, D), F32)],
    )(x, x, g, wp, scale)


def pool_bwd_rows(dy, m, wp, scale, *, S, tt=256):
    T = dy.shape[0]
    nb = S // tt

    def body(dy_ref, m_ref, wp_ref, sc_ref, dmc_ref, dyp_ref, dsc_ref):
        i = pl.program_id(0)

        @pl.when(i == 0)
        def _():
            dsc_ref[...] = jnp.zeros_like(dsc_ref)

        for gi, win in enumerate(POOL_WIN):
            gs = slice(gi * POOL_G, (gi + 1) * POOL_G)
            dyg = dy_ref[:, gs]
            w = wp_ref[gi]
            dsc_ref[:, gs] += jnp.sum(dyg * _dot(m_ref[:, gs], w), axis=0, keepdims=True)
            dyp = (dyg * sc_ref[:, gs]).astype(MXU)
            dyp_ref[:, gs] = dyp
            dmc_ref[:, gs] = _dot_nt(dyp, w) / _pool_cnt(i, nb, tt, win)

    row = pl.BlockSpec((1, D), lambda i: (0, 0))
    blk = pl.BlockSpec((tt, D), lambda i: (i, 0))
    return _call(
        body, name="pool_bwd_rows", grid=(T // tt,),
        in_specs=[blk, blk, pl.BlockSpec((len(POOL_WIN), POOL_G, POOL_G), lambda i: (0, 0, 0)), row],
        out_specs=[blk, blk, row],
        out_shape=[_sds((T, D), F32), _sds((T, D), MXU), _sds((1, D), F32)],
    )(dy, m, wp, scale)


def pool_bwd_core(dmc, x, g, dres, *, S, tt=256):
    T = x.shape[0]
    nb = S // tt
    r = tt // PH
    last_halo = T // PH - 1

    def body(dc_ref, dn_ref, x_ref, g_ref, dres_ref, dx_ref, dg_ref, dbuf, dh_buf):
        i = pl.program_id(0)

        @pl.when(i == 0)
        def _():
            dg_ref[...] = jnp.zeros_like(dg_ref)

        last = (i % nb) == nb - 1
        dbuf[0:tt, :] = dc_ref[...]
        dbuf[tt:, :] = jnp.where(last, 0.0, dn_ref[...])
        for gi, win in enumerate(POOL_WIN):
            gs = slice(gi * POOL_G, (gi + 1) * POOL_G)
            cur = dbuf[0:tt, gs]
            acc = cur
            for j in range(1, win):
                acc = acc + dbuf[j:j + tt, gs]
            dh_buf[:, gs] = acc - cur * _pool_cnt(i, nb, tt, win)
        dxn, dg = _rms_bwd(dh_buf[...], x_ref[...], g_ref[...])
        dx_ref[...] = dres_ref[...] + dxn
        dg_ref[...] += dg

    row = pl.BlockSpec((1, D), lambda i: (0, 0))
    blk = pl.BlockSpec((tt, D), lambda i: (i, 0))
    return _call(
        body, name="pool_bwd_core", grid=(T // tt,),
        in_specs=[blk, pl.BlockSpec((PH, D), lambda i: (jnp.minimum((i + 1) * r, last_halo), 0)), blk, row, blk],
        out_specs=[blk, row],
        out_shape=[_sds((T, D), F32), _sds((1, D), F32)],
        scratch=[pltpu.VMEM((tt + PH, D), F32), pltpu.VMEM((tt, D), F32)],
    )(dmc, dmc, x, g, dres)


NEG = -1e30


def _tri(n, upper=False):
    r = lax.broadcasted_iota(jnp.int32, (n, n), 0)
    c = lax.broadcasted_iota(jnp.int32, (n, n), 1)
    return (r <= c if upper else r >= c).astype(F32)


def _dot_hi(a, b):
    return jnp.dot(a, b, preferred_element_type=F32, precision=lax.Precision.HIGHEST)


def _log_sigmoid(z):
    return jnp.minimum(z, 0.0) - jnp.log(1.0 + jnp.exp(-jnp.abs(z)))


def fox_cum(fl, bf, *, S, tt=256):
    T = fl.shape[0]
    nb = S // tt

    def body(fl_ref, bf_ref, c_ref, carry):
        i = pl.program_id(0)

        @pl.when((i % nb) == 0)
        def _():
            carry[...] = jnp.zeros_like(carry)

        lf = _log_sigmoid(fl_ref[...] + bf_ref[...])
        c = _dot_hi(_tri(tt), lf) + carry[...]
        c_ref[...] = c
        carry[...] = c[tt - 1:tt, :]

    return _call(
        body, name="fox_cum", grid=(T // tt,),
        in_specs=[pl.BlockSpec((tt, 128), lambda i: (i, 0)), pl.BlockSpec((1, 128), lambda i: (0, 0))],
        out_specs=pl.BlockSpec((tt, 128), lambda i: (i, 0)),
        out_shape=_sds((T, 128), F32),
        scratch=[pltpu.VMEM((1, 128), F32)],
    )(fl, bf)


HW = 128
FOX_T = 512
COL_ONE = FOX_DH + 3
COL_LSE = FOX_DH + 6


def _parts(x):
    hi = x.astype(MXU).astype(F32)
    mid = (x - hi).astype(MXU).astype(F32)
    lo = (x - hi - mid).astype(MXU).astype(F32)
    return [hi, mid, lo]


def _aug(n, cols):
    lane = lax.broadcasted_iota(jnp.int32, (n, HW - FOX_DH), 1)
    out = jnp.zeros((n, HW - FOX_DH), F32)
    for i, cval in enumerate(cols):
        out = jnp.where(lane == i, cval, out)
    return out


def fox_prep(p, c, *, tt=256):
    T = p.shape[0]

    def body(q_ref, k_ref, v_ref, c_ref, qa_ref, ka_ref, va_ref):
        ones = [1.0, 1.0, 1.0]
        for h in range(FOX_H):
            hs = slice(h * FOX_DH, (h + 1) * FOX_DH)
            lo, mid = h * HW, h * HW + FOX_DH
            cp = _parts(c_ref[:, h:h + 1])
            qa_ref[:, lo:mid] = (q_ref[:, hs].astype(F32) * (FOX_DH ** -0.5)).astype(MXU)
            qa_ref[:, mid:lo + HW] = _aug(tt, cp + ones).astype(MXU)
            ka_ref[:, lo:mid] = k_ref[:, hs]
            ka_ref[:, mid:lo + HW] = _aug(tt, ones + [-x for x in cp] + ones).astype(MXU)
            va_ref[:, lo:mid] = v_ref[:, hs]
            va_ref[:, mid:lo + HW] = _aug(tt, ones).astype(MXU)

    wide = pl.BlockSpec((tt, FOX_H * HW), lambda i: (i, 0))
    col = lambda k: pl.BlockSpec((tt, D), lambda i: (i, k))
    return _call(body, name="fox_prep", grid=(T // tt,),
                 in_specs=[col(0), col(1), col(2), pl.BlockSpec((tt, 128), lambda i: (i, 0))],
                 out_specs=[wide, wide, wide], out_shape=[_sds((T, FOX_H * HW), MXU)] * 3)(p, p, p, c)


def _causal(x, fill):
    r = lax.broadcasted_iota(jnp.int32, x.shape, 0)
    c = lax.broadcasted_iota(jnp.int32, x.shape, 1)
    return jnp.where(r >= c, x, fill)


def _wide_specs(tq, nq, q_of, k_of):
    qs = pl.BlockSpec((tq, FOX_H * HW), lambda b, i, j: (b * nq + q_of(i, j), 0))
    ks = pl.BlockSpec((tq, FOX_H * HW), lambda b, i, j: (b * nq + k_of(i, j), 0))
    return qs, ks


def fox2_fwd(qa, ka, va, *, S, tq=256, tk=512):
    T = qa.shape[0]
    tk = min(tk, S)
    nq = S // tq
    nk = S // tk
    r = tk // tq

    def body(q_ref, k_ref, v_ref, o_ref, o32_ref, lse_ref, m_sc, acc, acc_lo):
        qi = pl.program_id(1)
        ki = pl.program_id(2)
        last = qi // r

        @pl.when(ki == 0)
        def _():
            m_sc[...] = jnp.full_like(m_sc, NEG)
            acc[...] = jnp.zeros_like(acc)
            acc_lo[...] = jnp.zeros_like(acc_lo)

        def step(diag):
            for h in range(FOX_H):
                ws = slice(h * HW, (h + 1) * HW)
                s = _dot_nt(q_ref[:, ws], k_ref[:, ws])
                if diag:
                    row = qi * tq + lax.broadcasted_iota(jnp.int32, s.shape, 0)
                    col = ki * tk + lax.broadcasted_iota(jnp.int32, s.shape, 1)
                    s = jnp.where(row >= col, s, NEG)
                m_prev = m_sc[h]
                m_new = jnp.maximum(m_prev, jnp.max(s, axis=-1, keepdims=True))
                alpha = jnp.exp(m_prev - m_new)
                hi, lo = _split(jnp.exp(s - jnp.tile(m_new, (1, tk // 128))))
                acc[h] = alpha * acc[h] + _dot(hi, v_ref[:, ws])
                acc_lo[h] = alpha * acc_lo[h] + _dot(lo, v_ref[:, ws])
                m_sc[h] = m_new

        @pl.when(ki < last)
        def _():
            step(False)

        @pl.when(ki == last)
        def _():
            step(True)
            for h in range(FOX_H):
                hs = slice(h * FOX_DH, (h + 1) * FOX_DH)
                full = acc[h] + acc_lo[h]
                l = full[:, FOX_DH:FOX_DH + 1]
                o_ref[:, hs] = (acc[h][:, :FOX_DH] / l).astype(MXU)
                o32_ref[:, hs] = full[:, :FOX_DH] / l
                lse_ref[:, h:h + 1] = m_sc[h][:, 0:1] + jnp.log(l)

    qs = pl.BlockSpec((tq, FOX_H * HW), lambda b, i, j: (b * nq + i, 0))
    ks = pl.BlockSpec((tk, FOX_H * HW), lambda b, i, j: (b * nk + jnp.minimum(j, i // r), 0))
    orow = pl.BlockSpec((tq, D), lambda b, i, j: (b * nq + i, 0))
    return _call(
        body, name="fox_fwd", grid=(T // S, nq, nk),
        in_specs=[qs, ks, ks],
        out_specs=[orow, orow, pl.BlockSpec((tq, 128), lambda b, i, j: (b * nq + i, 0))],
        out_shape=[_sds((T, D), MXU), _sds((T, D), F32), _sds((T, 128), F32)],
        scratch=[pltpu.VMEM((FOX_H, tq, 128), F32), pltpu.VMEM((FOX_H, tq, HW), F32),
                 pltpu.VMEM((FOX_H, tq, HW), F32)],
    )(qa, ka, va)


def fox2_prep_bwd(do, o32, lse, qa, *, tt=256):
    T = do.shape[0]

    def body(do_ref, o_ref, lse_ref, qa_ref, qb_ref, da_ref):
        lane = lax.broadcasted_iota(jnp.int32, (tt, HW), 1)
        dob = do_ref[...].astype(MXU)
        for h in range(FOX_H):
            hs = slice(h * FOX_DH, (h + 1) * FOX_DH)
            ws = slice(h * HW, (h + 1) * HW)
            doh = dob[:, hs]
            delta = jnp.sum(doh.astype(F32) * o_ref[:, hs], axis=-1, keepdims=True)
            da_ref[:, h * HW:h * HW + FOX_DH] = doh
            da_ref[:, h * HW + FOX_DH:(h + 1) * HW] = _aug(tt, [-x for x in _parts(delta)]).astype(MXU)
            tile = qa_ref[:, ws]
            for i, part in enumerate(_parts(lse_ref[:, h:h + 1])):
                tile = jnp.where(lane == COL_LSE + i, (-part).astype(MXU), tile)
            qb_ref[:, ws] = tile

    wide = pl.BlockSpec((tt, FOX_H * HW), lambda i: (i, 0))
    blk = pl.BlockSpec((tt, D), lambda i: (i, 0))
    return _call(body, name="fox_prep_bwd", grid=(T // tt,),
                 in_specs=[blk, blk, pl.BlockSpec((tt, 128), lambda i: (i, 0)), wide],
                 out_specs=[wide, wide],
                 out_shape=[_sds((T, FOX_H * HW), MXU), _sds((T, FOX_H * HW), MXU)],
                 )(do, o32, lse, qa)


def fox2_dq(qb, ka, va, da, *, S, tq=256):
    T = qb.shape[0]
    nq = S // tq

    def body(q_ref, k_ref, v_ref, d_ref, dq_ref, acc):
        qi = pl.program_id(1)
        ki = pl.program_id(2)

        @pl.when(ki == 0)
        def _():
            acc[...] = jnp.zeros_like(acc)

        def step(diag):
            for h in range(FOX_H):
                ws = slice(h * HW, (h + 1) * HW)
                kh = k_ref[:, ws]
                p = jnp.exp(_dot_nt(q_ref[:, ws], kh))
                if diag:
                    p = _causal(p, 0.0)
                ds = (p * _dot_nt(d_ref[:, ws], v_ref[:, ws])).astype(MXU)
                acc[h] += _dot(ds, kh)

        @pl.when(ki < qi)
        def _():
            step(False)

        @pl.when(ki == qi)
        def _():
            step(True)
            for h in range(FOX_H):
                dq_ref[:, h * FOX_DH:(h + 1) * FOX_DH] = (acc[h][:, :FOX_DH] * (FOX_DH ** -0.5)).astype(MXU)

    qs, ks = _wide_specs(tq, nq, lambda i, j: i, lambda i, j: jnp.minimum(i, j))
    return _call(
        body, name="fox_bwd_dq", grid=(T // S, nq, nq),
        in_specs=[qs, ks, ks, qs],
        out_specs=pl.BlockSpec((tq, D), lambda b, i, j: (b * nq + i, 0)),
        out_shape=_sds((T, D), MXU),
        scratch=[pltpu.VMEM((FOX_H, tq, HW), F32)],
    )(qb, ka, va, da)


def fox2_dkv(qb, ka, va, da, *, S, tq=256):
    T = qb.shape[0]
    nq = S // tq

    def body(q_ref, k_ref, v_ref, d_ref, dk_ref, dv_ref, dck_ref, dk_acc, dv_acc):
        ki = pl.program_id(1)
        qi = pl.program_id(2)

        @pl.when(qi == 0)
        def _():
            dk_acc[...] = jnp.zeros_like(dk_acc)
            dv_acc[...] = jnp.zeros_like(dv_acc)

        def step(diag):
            for h in range(FOX_H):
                ws = slice(h * HW, (h + 1) * HW)
                qh = q_ref[:, ws]
                dh = d_ref[:, ws]
                pt = jnp.exp(_dot_nt(k_ref[:, ws], qh))
                if diag:
                    r = lax.broadcasted_iota(jnp.int32, pt.shape, 0)
                    c = lax.broadcasted_iota(jnp.int32, pt.shape, 1)
                    pt = jnp.where(r <= c, pt, 0.0)
                dv_acc[h] += _dot(pt.astype(MXU), dh)
                hi, lo = _split(pt * _dot_nt(v_ref[:, ws], dh))
                dk_acc[h] += _dot(hi, qh) + _dot(lo, qh)

        @pl.when(qi > ki)
        def _():
            step(False)

        @pl.when(qi == ki)
        def _():
            step(True)

        @pl.when(qi == nq - 1)
        def _():
            dck_ref[...] = jnp.zeros_like(dck_ref)
            for h in range(FOX_H):
                hs = slice(h * FOX_DH, (h + 1) * FOX_DH)
                dk_ref[:, hs] = dk_acc[h][:, :FOX_DH].astype(MXU)
                dv_ref[:, hs] = dv_acc[h][:, :FOX_DH].astype(MXU)
                dck_ref[:, h:h + 1] = dk_acc[h][:, COL_ONE:COL_ONE + 1]

    qs, ks = _wide_specs(tq, nq, lambda i, j: jnp.maximum(i, j), lambda i, j: i)
    ko = pl.BlockSpec((tq, D), lambda b, i, j: (b * nq + i, 0))
    return _call(
        body, name="fox_bwd_dkv", grid=(T // S, nq, nq),
        in_specs=[qs, ks, ks, qs],
        out_specs=[ko, ko, pl.BlockSpec((tq, 128), lambda b, i, j: (b * nq + i, 0))],
        out_shape=[_sds((T, D), MXU), _sds((T, D), MXU), _sds((T, 128), F32)],
        scratch=[pltpu.VMEM((FOX_H, tq, HW), F32), pltpu.VMEM((FOX_H, tq, HW), F32)],
    )(qb, ka, va, da)


def fox_fin(dck, fl, bf, *, S, tt=256):
    T = fl.shape[0]
    nb = S // tt
    nblk = T // tt

    def body(dck_ref, fl_ref, bf_ref, dfl_ref, dbf_ref, carry):
        i = pl.program_id(0)

        @pl.when(i == 0)
        def _():
            dbf_ref[...] = jnp.zeros_like(dbf_ref)

        @pl.when((i % nb) == 0)
        def _():
            carry[...] = jnp.zeros_like(carry)

        lane = lax.broadcasted_iota(jnp.int32, (tt, 128), 1)
        dc = jnp.where(lane < FOX_H, -dck_ref[...], 0.0)
        dlf = _dot_hi(_tri(tt, upper=True), dc) + carry[...]
        carry[...] = dlf[0:1, :]
        dfl = dlf * _sigmoid(-(fl_ref[...] + bf_ref[...]))
        dfl_ref[...] = dfl.astype(MXU)
        dbf_ref[...] += jnp.sum(dfl, axis=0, keepdims=True)

    rev = pl.BlockSpec((tt, 128), lambda i: (nblk - 1 - i, 0))
    row = pl.BlockSpec((1, 128), lambda i: (0, 0))
    return _call(
        body, name="fox_fin", grid=(nblk,),
        in_specs=[rev, rev, row],
        out_specs=[rev, row],
        out_shape=[_sds((T, 128), MXU), _sds((1, 128), F32)],
        scratch=[pltpu.VMEM((1, 128), F32)],
    )(dck, fl, bf)


def lb_fwd(logits):
    def body(l_ref, lb_ref):
        lv = l_ref[...]
        e = jnp.exp(lv - jnp.max(lv, axis=0, keepdims=True))
        p = e / jnp.sum(e, axis=0, keepdims=True)
        lb_ref[...] = p[1:2, :] + p[2:3, :]

    return _call(body, name="lb_fwd", grid=(1,),
                 in_specs=[pl.BlockSpec((DEPTH, D), lambda i: (0, 0))],
                 out_specs=pl.BlockSpec((1, D), lambda i: (0, 0)),
                 out_shape=_sds((1, D), F32))(logits)


def lb_bwd(logits, dlb):
    def body(l_ref, d_ref, o_ref):
        lv = l_ref[...]
        e = jnp.exp(lv - jnp.max(lv, axis=0, keepdims=True))
        p = e / jnp.sum(e, axis=0, keepdims=True)
        lb = p[1:2, :] + p[2:3, :]
        row = lax.broadcasted_iota(jnp.int32, (DEPTH, D), 0)
        sel = ((row == 1) | (row == 2)).astype(F32)
        o_ref[...] = p * (sel - lb) * d_ref[...]

    return _call(body, name="lb_bwd", grid=(1,),
                 in_specs=[pl.BlockSpec((DEPTH, D), lambda i: (0, 0)), pl.BlockSpec((1, D), lambda i: (0, 0))],
                 out_specs=pl.BlockSpec((DEPTH, D), lambda i: (0, 0)),
                 out_shape=_sds((DEPTH, D), F32))(logits, dlb)


def _hgrn_gates(qr, fr, lb):
    sg = _sigmoid(fr)
    sneg = _sigmoid(-fr)
    f = lb + (1.0 - lb) * sg
    kk = (1.0 - lb) * sneg
    G = _dot_hi(_tri(HG_C), jnp.log(f))
    eG = jnp.exp(G)
    einv = jnp.exp(-G)
    elast = jnp.exp(G[HG_C - 1:HG_C, :] - G)
    q = qr * _sigmoid(qr)
    return dict(q=q, kk=kk, f=f, sg=sg, sneg=sneg, eG=eG, einv=einv, elast=elast,
                qg=q * eG, kinv=kk * einv, khat=kk * elast, glast=jnp.exp(G[HG_C - 1:HG_C, :]))


def _tril_mask(x):
    r = lax.broadcasted_iota(jnp.int32, x.shape, 0)
    c = lax.broadcasted_iota(jnp.int32, x.shape, 1)
    return jnp.where(r >= c, x, 0.0)


def hgrn_fwd(p, lb, ng, *, S, R=256):
    T = p.shape[0]
    R = min(R, S)
    nr = S // R
    ncr = R // HG_C

    def body(q_ref, f_ref, v_ref, gt_ref, lb_ref, ng_ref, y_ref, o_ref, st_ref, st):
        @pl.when(pl.program_id(1) == 0)
        def _():
            st[...] = jnp.zeros_like(st)

        lbv = lb_ref[...]
        for ch in range(ncr):
            rows = slice(ch * HG_C, (ch + 1) * HG_C)
            gt = _hgrn_gates(q_ref[rows, :], f_ref[rows, :], lbv)
            for h in range(HG_H):
                hs = slice(h * HG_DK, (h + 1) * HG_DK)
                sp = st[h]
                st_ref[ch, h] = sp
                qg = gt["qg"][:, hs].astype(MXU)
                vh = v_ref[rows, hs].astype(MXU)
                A = _tril_mask(_dot_nt(qg, gt["kinv"][:, hs].astype(MXU)))
                o_ref[rows, hs] = _dot_nt(qg, sp.astype(MXU)) + _dot(A.astype(MXU), vh)
                st[h] = sp * gt["glast"][:, hs] + _dot_tn(vh, gt["khat"][:, hs].astype(MXU))
        gate = gt_ref[...]
        sgate = gate * _sigmoid(gate)
        for h in range(HG_H):
            hs = slice(h * HG_DK, (h + 1) * HG_DK)
            oh = o_ref[:, hs]
            r = lax.rsqrt(jnp.mean(oh * oh, axis=-1, keepdims=True) + RMS_EPS)
            y_ref[:, hs] = (oh * r * ng_ref[:, hs] * sgate[:, hs]).astype(MXU)

    col = lambda c: pl.BlockSpec((R, D), lambda b, i: (b * nr + i, c))
    row = pl.BlockSpec((1, D), lambda b, i: (0, 0))
    return _call(
        body, name="hgrn_fwd", grid=(T // S, nr),
        in_specs=[col(0), col(1), col(2), col(3), row, row],
        out_specs=[col(0), col(0),
                   pl.BlockSpec((ncr, HG_H, HG_DK, HG_DK), lambda b, i: (b * nr + i, 0, 0, 0))],
        out_shape=[_sds((T, D), MXU), _sds((T, D), F32), _sds((T // HG_C, HG_H, HG_DK, HG_DK), F32)],
        scratch=[pltpu.VMEM((HG_H, HG_DK, HG_DK), F32)],
    )(p, p, p, p, lb, ng)


def hgrn_bwd(p, o, dyo, states, lb, ng, *, S, R=256):
    T = p.shape[0]
    R = min(R, S)
    nr = S // R
    ncr = R // HG_C

    def body(q_ref, f_ref, v_ref, gt_ref, o_ref, dy_ref, st_ref, lb_ref, ng_ref,
             dp_ref, dlb_ref, dng_ref, dst, do_buf, dG_buf, dqb, dkb):
        b = pl.program_id(0)
        i = pl.program_id(1)

        @pl.when(i == 0)
        def _():
            dst[...] = jnp.zeros_like(dst)

        @pl.when((b == 0) & (i == 0))
        def _():
            dlb_ref[...] = jnp.zeros_like(dlb_ref)
            dng_ref[...] = jnp.zeros_like(dng_ref)

        lbv = lb_ref[...]
        gate = gt_ref[...]
        sg_gate = _sigmoid(gate)
        silu_gate = gate * sg_gate
        for h in range(HG_H):
            hs = slice(h * HG_DK, (h + 1) * HG_DK)
            oh = o_ref[:, hs]
            r = lax.rsqrt(jnp.mean(oh * oh, axis=-1, keepdims=True) + RMS_EPS)
            ohat = oh * r
            dyh = dy_ref[:, hs]
            ngh = ng_ref[:, hs]
            dng_ref[:, hs] += jnp.sum(dyh * silu_gate[:, hs] * ohat, axis=0, keepdims=True)
            dp_ref[:, 3 * D + h * HG_DK:3 * D + (h + 1) * HG_DK] = (
                dyh * ohat * ngh * (sg_gate[:, hs] * (1.0 + gate[:, hs] * (1.0 - sg_gate[:, hs])))).astype(MXU)
            dn = dyh * ngh * silu_gate[:, hs]
            do_buf[:, hs] = r * (dn - ohat * jnp.mean(dn * ohat, axis=-1, keepdims=True))

        lastrow = lax.broadcasted_iota(jnp.int32, (HG_C, HG_DK), 0) == HG_C - 1
        for ch in reversed(range(ncr)):
            rows = slice(ch * HG_C, (ch + 1) * HG_C)
            qr = q_ref[rows, :]
            gt = _hgrn_gates(qr, f_ref[rows, :], lbv)
            for h in range(HG_H):
                hs = slice(h * HG_DK, (h + 1) * HG_DK)
                sp = st_ref[ch, h]
                ds = dst[h]
                qg32, kinv32, khat32 = gt["qg"][:, hs], gt["kinv"][:, hs], gt["khat"][:, hs]
                qg, kinv, khat = qg32.astype(MXU), kinv32.astype(MXU), khat32.astype(MXU)
                vh = v_ref[rows, hs].astype(MXU)
                doh = do_buf[rows, hs].astype(MXU)
                dsb = ds.astype(MXU)
                A = _tril_mask(_dot_nt(qg, kinv)).astype(MXU)
                dA = _tril_mask(_dot_nt(doh, vh)).astype(MXU)
                dqg = _dot(doh, sp.astype(MXU)) + _dot(dA, kinv)
                dkinv = _dot_tn(dA, qg)
                dp_ref[rows, 2 * D + h * HG_DK:2 * D + (h + 1) * HG_DK] = (
                    _dot_tn(A, doh) + _dot_nt(khat, dsb)).astype(MXU)
                dkhat = _dot(vh, dsb)
                glast = gt["glast"][:, hs]
                qg32, kinv32, khat32 = qg.astype(F32), kinv.astype(F32), khat.astype(F32)
                extra = (glast * jnp.sum(dsb.astype(F32) * sp.astype(MXU).astype(F32), axis=0, keepdims=True)
                         + jnp.sum(dkhat * khat32, axis=0, keepdims=True))
                dst[h] = ds * glast + _dot_tn(doh, qg)
                dG = dqg * qg32 - dkinv * kinv32 - dkhat * khat32
                dG_buf[:, hs] = dG + jnp.where(lastrow, extra, 0.0)
                dqb[:, hs] = dqg * gt["eG"][:, hs]
                dkb[:, hs] = dkinv * gt["einv"][:, hs] + dkhat * gt["elast"][:, hs]
            dg = _dot_hi(_tri(HG_C, upper=True), dG_buf[...])
            dk = dkb[...]
            sneg, f = gt["sneg"], gt["f"]
            c1 = (1.0 - lbv) * gt["sg"] * sneg
            dp_ref[rows, D:2 * D] = (dg * c1 / f - dk * c1).astype(MXU)
            dlb_ref[...] += jnp.sum(dg * sneg / f - dk * sneg, axis=0, keepdims=True)
            sq = _sigmoid(qr)
            dp_ref[rows, 0:D] = (dqb[...] * (sq * (1.0 + qr * (1.0 - sq)))).astype(MXU)

    rev = lambda b, i: b * nr + nr - 1 - i
    col = lambda c: pl.BlockSpec((R, D), lambda b, i: (rev(b, i), c))
    row = pl.BlockSpec((1, D), lambda b, i: (0, 0))
    return _call(
        body, name="hgrn_bwd", grid=(T // S, nr),
        in_specs=[col(0), col(1), col(2), col(3), col(0), col(0),
                  pl.BlockSpec((ncr, HG_H, HG_DK, HG_DK), lambda b, i: (rev(b, i), 0, 0, 0)), row, row],
        out_specs=[pl.BlockSpec((R, 4 * D), lambda b, i: (rev(b, i), 0)), row, row],
        out_shape=[_sds((T, 4 * D), MXU), _sds((1, D), F32), _sds((1, D), F32)],
        scratch=[pltpu.VMEM((HG_H, HG_DK, HG_DK), F32), pltpu.VMEM((R, D), F32), pltpu.VMEM((HG_C, D), F32),
                 pltpu.VMEM((HG_C, D), F32), pltpu.VMEM((HG_C, D), F32)],
    )(p, p, p, p, o, dyo, states, lb, ng)


def _mm_tn_stack(a, b, *, name, G, M, N, tk, stack):
    E, e, buf = stack
    T = a.shape[-2]

    def spec(arr, width):
        if arr.ndim == 3:
            return pl.BlockSpec((None, tk, width), lambda g, k: (g, k, 0))
        return pl.BlockSpec((tk, width), lambda g, k: (k, 0))

    def body(*refs):
        a_ref, b_ref, o_ref = refs[0], refs[1], refs[-1]

        @pl.when(pl.program_id(1) == 0)
        def _():
            o_ref[...] = jnp.zeros_like(o_ref)

        o_ref[...] += _dot_tn(a_ref[...], b_ref[...])

    in_specs = [spec(a, M), spec(b, N)]
    args = [a, b]
    aliases = {}
    if buf is not None:
        in_specs.append(pl.BlockSpec(memory_space=pl.ANY))
        args.append(buf)
        aliases = {2: 0}
    return pl.pallas_call(
        body, name=name, grid=(G, T // tk), in_specs=in_specs,
        out_specs=pl.BlockSpec((None, None, M, N), lambda g, k: (g, e, 0, 0)),
        out_shape=_sds((G, E, M, N), F32), input_output_aliases=aliases,
        compiler_params=pltpu.CompilerParams(dimension_semantics=("arbitrary", "arbitrary"),
                                             vmem_limit_bytes=VMEM_LIMIT))(*args)


MESH = pl.DeviceIdType.MESH
ANY = pl.BlockSpec(memory_space=pl.ANY)


def _pos():
    return lax.axis_index("x"), lax.axis_index("y"), lax.axis_index("c")


def _other_chips(x, y):
    return [(1 - x, y), (x, 1 - y), (1 - x, 1 - y)]


def _comm_call(body, *, name, args, out_shape, n_sem):
    return pl.pallas_call(
        body, name=name, in_specs=[ANY] * len(args), out_specs=[ANY] * len(out_shape), out_shape=out_shape,
        scratch_shapes=[pltpu.SemaphoreType.DMA((n_sem,)), pltpu.SemaphoreType.DMA((n_sem,)),
                        pltpu.SemaphoreType.DMA((len(args),))],
    )(*args)


def all_gather_chips(xs):
    n = len(xs)

    def body(*refs):
        x_refs, o_refs = refs[:n], refs[n:2 * n]
        ssem, rsem, lsem = refs[2 * n:]
        x, y, c = _pos()
        me = 2 * x + y
        chips = _other_chips(x, y)
        sib = (x, y, 1 - c)

        def rc(src, dst, idx, dev):
            return pltpu.make_async_remote_copy(src_ref=src, dst_ref=dst, send_sem=ssem.at[idx], recv_sem=rsem.at[idx],
                                                device_id=dev, device_id_type=MESH)

        via = jnp.where(c == 0, 2 * (1 - x) + y, 2 * x + (1 - y))
        to = (jnp.where(c == 0, x, 1 - x), jnp.where(c == 0, 1 - y, y), c)
        local, started = [], []
        for t in range(n):
            hr = xs[t].shape[0] // 2
            mine = pl.ds(c * hr, hr)
            cp = pltpu.make_async_copy(x_refs[t], o_refs[t].at[me], lsem.at[t])
            cp.start()
            local.append(cp)
            for k, (cx, cy) in enumerate(chips[:2]):
                cp = rc(x_refs[t].at[mine], o_refs[t].at[me, mine], 6 * t + k, (cx, cy, c))
                cp.start()
                started.append(cp)
        for t in range(n):
            hr = xs[t].shape[0] // 2
            mine = pl.ds(c * hr, hr)
            for k, (cx, cy) in enumerate(chips[:2]):
                landed = o_refs[t].at[2 * cx + cy, mine]
                rc(landed, landed, 6 * t + k, (cx, cy, c)).wait_recv()
            passed = o_refs[t].at[via, mine]
            cp = rc(passed, passed, 6 * t + 2, to)
            cp.start()
            started.append(cp)
            for k, (cx, cy) in enumerate(chips[:2]):
                landed = o_refs[t].at[2 * cx + cy, mine]
                cp = rc(landed, landed, 6 * t + 3 + k, sib)
                cp.start()
                started.append(cp)
        for t in range(n):
            hr = xs[t].shape[0] // 2
            mine = pl.ds(c * hr, hr)
            cx, cy = chips[2]
            landed = o_refs[t].at[2 * cx + cy, mine]
            rc(landed, landed, 6 * t + 2, to).wait_recv()
            cp = rc(landed, landed, 6 * t + 5, sib)
            cp.start()
            started.append(cp)
        for t in range(n):
            hr = xs[t].shape[0] // 2
            theirs = pl.ds((1 - c) * hr, hr)
            for k, (cx, cy) in enumerate(chips):
                other = o_refs[t].at[2 * cx + cy, theirs]
                rc(other, other, 6 * t + 3 + k, sib).wait_recv()
        for cp in started:
            cp.wait_send()
        for cp in local:
            cp.wait()

    outs = _comm_call(body, name="all_gather_chips", args=list(xs),
                      out_shape=[_sds((NSH,) + a.shape, a.dtype) for a in xs], n_sem=6 * n)
    return list(outs)


def sibling_half_exchange(gs):
    n = len(gs)

    def body(*refs):
        g_refs, o_refs = refs[:n], refs[n:2 * n]
        ssem, rsem, _ = refs[2 * n:]
        x, y, c = _pos()
        cps = []
        for t in range(n):
            hr = gs[t].shape[1] // 2
            for j in range(NSH):
                cp = pltpu.make_async_remote_copy(
                    src_ref=g_refs[t].at[j, pl.ds((1 - c) * hr, hr)], dst_ref=o_refs[t].at[j],
                    send_sem=ssem.at[NSH * t + j], recv_sem=rsem.at[NSH * t + j],
                    device_id=(x, y, 1 - c), device_id_type=MESH)
                cp.start()
                cps.append(cp)
        for cp in cps:
            cp.wait()

    outs = _comm_call(body, name="sibling_half_exchange", args=list(gs),
                      out_shape=[_sds((NSH, g.shape[1] // 2, g.shape[2]), g.dtype) for g in gs], n_sem=NSH * n)
    return list(outs)


def chip_scatter(ss):
    n = len(ss)

    def body(*refs):
        s_refs, o_refs = refs[:n], refs[n:2 * n]
        ssem, rsem, _ = refs[2 * n:]
        x, y, c = _pos()
        cps = []
        for t in range(n):
            for k, (cx, cy) in enumerate(_other_chips(x, y)):
                cp = pltpu.make_async_remote_copy(
                    src_ref=s_refs[t].at[2 * cx + cy], dst_ref=o_refs[t].at[k],
                    send_sem=ssem.at[3 * t + k], recv_sem=rsem.at[3 * t + k],
                    device_id=(cx, cy, c), device_id_type=MESH)
                cp.start()
                cps.append(cp)
        for cp in cps:
            cp.wait()

    outs = _comm_call(body, name="chip_scatter", args=list(ss),
                      out_shape=[_sds((3,) + s.shape[1:], s.dtype) for s in ss], n_sem=3 * n)
    return list(outs)


def sibling_exchange(rs):
    n = len(rs)

    def body(*refs):
        r_refs, o_refs = refs[:n], refs[n:2 * n]
        ssem, rsem, _ = refs[2 * n:]
        x, y, c = _pos()
        cps = []
        for t in range(n):
            cp = pltpu.make_async_remote_copy(
                src_ref=r_refs[t], dst_ref=o_refs[t], send_sem=ssem.at[t], recv_sem=rsem.at[t],
                device_id=(x, y, 1 - c), device_id_type=MESH)
            cp.start()
            cps.append(cp)
        for cp in cps:
            cp.wait()

    outs = _comm_call(body, name="sibling_exchange", args=list(rs),
                      out_shape=[_sds(r.shape, r.dtype) for r in rs], n_sem=n)
    return list(outs)


def all_gather_devices(v):
    def body(v_ref, o_ref, ssem, rsem, lsem):
        x, y, c = _pos()
        me = 4 * x + 2 * y + c
        loc = pltpu.make_async_copy(v_ref, o_ref.at[me], lsem.at[0])
        loc.start()
        cps = []
        k = 0
        for fx in range(2):
            for fy in range(2):
                for fc in range(2):
                    if fx == fy == fc == 0:
                        continue
                    cp = pltpu.make_async_remote_copy(
                        src_ref=v_ref, dst_ref=o_ref.at[me], send_sem=ssem.at[k], recv_sem=rsem.at[k],
                        device_id=(x ^ fx, y ^ fy, c ^ fc), device_id_type=MESH)
                    cp.start()
                    src = 4 * (x ^ fx) + 2 * (y ^ fy) + (c ^ fc)
                    cps.append((cp, o_ref.at[src], k))
                    k += 1
        for cp, landed, k in cps:
            cp.wait_send()
            pltpu.make_async_remote_copy(
                src_ref=landed, dst_ref=landed, send_sem=ssem.at[k], recv_sem=rsem.at[k],
                device_id=(x, y, c), device_id_type=MESH).wait_recv()
        loc.wait()

    return _comm_call(body, name="all_gather_devices", args=[v],
                      out_shape=[_sds((8,) + v.shape, v.dtype)], n_sem=7)[0]


def _call_sp(body, *, name, grid, in_specs, out_specs, out_shape, pos, args):
    return pl.pallas_call(
        body, name=name,
        grid_spec=pltpu.PrefetchScalarGridSpec(num_scalar_prefetch=1, grid=grid, in_specs=in_specs,
                                               out_specs=out_specs),
        out_shape=out_shape,
        compiler_params=pltpu.CompilerParams(dimension_semantics=("arbitrary",) * len(grid),
                                             vmem_limit_bytes=VMEM_LIMIT))(pos, *args)


def _rows_tile(r):
    for t in (512, 256, 128, 64, 32, 16, 8):
        if r % t == 0:
            return t
    raise ValueError(r)


def pair_sum(g, r, pos):
    _, R, C = g.shape
    hr = R // 2
    tr = _rows_tile(hr)
    nbh = hr // tr

    def body(p_ref, g_ref, r_ref, o_ref):
        o_ref[...] = (g_ref[...] + r_ref[...]).astype(MXU)

    return _call_sp(
        body, name="pair_sum", grid=(NSH, nbh), pos=pos, args=[g, r],
        in_specs=[pl.BlockSpec((None, tr, C), lambda j, i, p: (j, p[0] * nbh + i, 0)),
                  pl.BlockSpec((None, tr, C), lambda j, i, p: (j, i, 0))],
        out_specs=pl.BlockSpec((None, tr, C), lambda j, i, p: (j, i, 0)),
        out_shape=_sds((NSH, hr, C), MXU))


def reduce_own(g, r_sib, r_ici, pos):
    _, R, C = g.shape
    hr = R // 2
    tr = _rows_tile(hr)
    nbh = hr // tr

    def body(p_ref, g_ref, rs_ref, ri_ref, o_ref):
        s = g_ref[...] + rs_ref[...]
        for k in range(3):
            s = s + ri_ref[k].astype(F32)
        o_ref[...] = s

    return _call_sp(
        body, name="reduce_own", grid=(nbh,), pos=pos, args=[g, r_sib, r_ici],
        in_specs=[pl.BlockSpec((None, tr, C), lambda i, p: (p[1], p[0] * nbh + i, 0)),
                  pl.BlockSpec((None, tr, C), lambda i, p: (p[1], i, 0)),
                  pl.BlockSpec((3, tr, C), lambda i, p: (0, i, 0))],
        out_specs=pl.BlockSpec((tr, C), lambda i, p: (i, 0)),
        out_shape=_sds((hr, C), F32))


def _adamw_math(w, g, m, v):
    m = ADAM_B1 * m + (1.0 - ADAM_B1) * g
    v = ADAM_B2 * v + (1.0 - ADAM_B2) * (g * g)
    m_hat = m / (1.0 - ADAM_B1 ** ADAM_STEP)
    v_hat = v / (1.0 - ADAM_B2 ** ADAM_STEP)
    delta = -ADAM_LR * (m_hat / (jnp.sqrt(v_hat) + ADAM_EPS) + ADAM_WD * w)
    return delta, m, v


def adamw_halves(w, m, v, ga, gb, pos):
    R, C = w.shape
    hr = R // 2
    tr = _rows_tile(hr)
    nbh = hr // tr

    def body(p_ref, w_ref, m_ref, v_ref, ga_ref, gb_ref, g_ref, d_ref, mo_ref, vo_ref):
        mine = (pl.program_id(0) // nbh) == p_ref[0]
        g = jnp.where(mine, ga_ref[...], gb_ref[...])
        g_ref[...] = g
        d_ref[...], mo_ref[...], vo_ref[...] = _adamw_math(w_ref[...], g, m_ref[...], v_ref[...])

    blk = pl.BlockSpec((tr, C), lambda i, p: (i, 0))
    return _call_sp(
        body, name="adamw_halves", grid=(R // tr,), pos=pos, args=[w, m, v, ga, gb],
        in_specs=[blk, blk, blk,
                  pl.BlockSpec((tr, C), lambda i, p: (jnp.where(i // nbh == p[0], i % nbh, 0), 0)),
                  pl.BlockSpec((tr, C), lambda i, p: (jnp.where(i // nbh == p[0], 0, i % nbh), 0))],
        out_specs=[blk, blk, blk, blk],
        out_shape=[_sds((R, C), F32)] * 4)


def adamw_sum(gall, w, m, v):
    n, R, C = gall.shape

    def body(ga_ref, w_ref, m_ref, v_ref, g_ref, d_ref, mo_ref, vo_ref):
        g = ga_ref[0]
        for k in range(1, n):
            g = g + ga_ref[k]
        g_ref[...] = g
        d_ref[...], mo_ref[...], vo_ref[...] = _adamw_math(w_ref[...], g, m_ref[...], v_ref[...])

    blk = pl.BlockSpec((R, C), lambda i: (0, 0))
    return _call(body, name="adamw_sum", grid=(1,),
                 in_specs=[pl.BlockSpec((n, R, C), lambda i: (0, 0, 0)), blk, blk, blk],
                 out_specs=[blk, blk, blk, blk], out_shape=[_sds((R, C), F32)] * 4)(gall, w, m, v)


_WEIGHTS = ['ffn_norm', 'ffn_w_gate', 'ffn_w_up', 'ffn_w_down', 'mix_norm', 'final_norm', 'conv_w_in', 'conv_b_in',
            'conv_dw', 'conv_dw_b', 'conv_ln_g', 'conv_ln_b', 'conv_w_out', 'fox_w_in', 'fox_b_f', 'fox_w_out',
            'hgrn_w_in', 'hgrn_lb_logits', 'hgrn_norm', 'hgrn_w_out', 'pool_w', 'pool_scale']
_BIG = ['ffn_w_gate', 'ffn_w_up', 'ffn_w_down', 'conv_w_in', 'conv_w_out', 'fox_w_in', 'fox_w_out',
        'hgrn_w_in', 'hgrn_w_out', 'pool_w']
_SHARDED_SMALL = ['ffn_norm', 'conv_dw', 'hgrn_norm', 'pool_scale']
_REPLICATED = ['mix_norm', 'final_norm', 'conv_b_in', 'conv_dw_b', 'conv_ln_g', 'conv_ln_b', 'fox_b_f', 'hgrn_lb_logits']
FOX_N = 3 * D + FOX_H
FOX_NP = 3200
QS = D // NSH


def _pad_rows(a, rows):
    return jnp.pad(a, ((0, rows - a.shape[0]), (0, 0)))


def _pack_sharded_small(get):
    return jnp.concatenate([get('ffn_norm').reshape(8, -1), _pad_rows(get('conv_dw')[0], 32),
                            get('hgrn_norm'), get('pool_scale'), jnp.zeros((6, get('pool_scale').shape[1]), F32)], axis=0)


def _pack_replicated(get):
    return jnp.concatenate([get('mix_norm'), get('final_norm').reshape(1, D), get('conv_b_in').reshape(2, D),
                            get('conv_dw_b'), get('conv_ln_g'), get('conv_ln_b'),
                            jnp.pad(get('fox_b_f'), ((0, 0), (0, D - FOX_H))), get('hgrn_lb_logits'),
                            jnp.zeros((9, D), F32)], axis=0)


def _unpack_replicated(p):
    return {'mix_norm': p[0:4], 'final_norm': p[4], 'conv_b_in': p[5:7].reshape(1, 2 * D), 'conv_dw_b': p[7:8],
            'conv_ln_g': p[8:9], 'conv_ln_b': p[9:10], 'fox_b_f': p[10:11, :FOX_H], 'hgrn_lb_logits': p[11:15]}


def _unpack_sharded_small(p):
    return {'ffn_norm': p[0:8].reshape(DEPTH, 2, -1), 'conv_dw': p[8:8 + CONV_W][None],
            'hgrn_norm': p[40:41], 'pool_scale': p[41:42]}


def kernel(x, ffn_norm, ffn_w_gate, ffn_w_up, ffn_w_down, mix_norm, final_norm, conv_w_in, conv_b_in, conv_dw, conv_dw_b, conv_ln_g, conv_ln_b, conv_w_out, fox_w_in, fox_b_f, fox_w_out, hgrn_w_in, hgrn_lb_logits, hgrn_norm, hgrn_w_out, pool_w, pool_scale, loss_target, m_ffn_norm, m_ffn_w_gate, m_ffn_w_up, m_ffn_w_down, m_mix_norm, m_final_norm, m_conv_w_in, m_conv_b_in, m_conv_dw, m_conv_dw_b, m_conv_ln_g, m_conv_ln_b, m_conv_w_out, m_fox_w_in, m_fox_b_f, m_fox_w_out, m_hgrn_w_in, m_hgrn_lb_logits, m_hgrn_norm, m_hgrn_w_out, m_pool_w, m_pool_scale, v_ffn_norm, v_ffn_w_gate, v_ffn_w_up, v_ffn_w_down, v_mix_norm, v_final_norm, v_conv_w_in, v_conv_b_in, v_conv_dw, v_conv_dw_b, v_conv_ln_g, v_conv_ln_b, v_conv_w_out, v_fox_w_in, v_fox_b_f, v_fox_w_out, v_hgrn_w_in, v_hgrn_lb_logits, v_hgrn_norm, v_hgrn_w_out, v_pool_w, v_pool_scale):
    W = dict(ffn_norm=ffn_norm, ffn_w_gate=ffn_w_gate, ffn_w_up=ffn_w_up, ffn_w_down=ffn_w_down, mix_norm=mix_norm, final_norm=final_norm, conv_w_in=conv_w_in, conv_b_in=conv_b_in, conv_dw=conv_dw, conv_dw_b=conv_dw_b, conv_ln_g=conv_ln_g, conv_ln_b=conv_ln_b, conv_w_out=conv_w_out, fox_w_in=fox_w_in, fox_b_f=fox_b_f, fox_w_out=fox_w_out, hgrn_w_in=hgrn_w_in, hgrn_lb_logits=hgrn_lb_logits, hgrn_norm=hgrn_norm, hgrn_w_out=hgrn_w_out, pool_w=pool_w, pool_scale=pool_scale)
    M = dict(ffn_norm=m_ffn_norm, ffn_w_gate=m_ffn_w_gate, ffn_w_up=m_ffn_w_up, ffn_w_down=m_ffn_w_down, mix_norm=m_mix_norm, final_norm=m_final_norm, conv_w_in=m_conv_w_in, conv_b_in=m_conv_b_in, conv_dw=m_conv_dw, conv_dw_b=m_conv_dw_b, conv_ln_g=m_conv_ln_g, conv_ln_b=m_conv_ln_b, conv_w_out=m_conv_w_out, fox_w_in=m_fox_w_in, fox_b_f=m_fox_b_f, fox_w_out=m_fox_w_out, hgrn_w_in=m_hgrn_w_in, hgrn_lb_logits=m_hgrn_lb_logits, hgrn_norm=m_hgrn_norm, hgrn_w_out=m_hgrn_w_out, pool_w=m_pool_w, pool_scale=m_pool_scale)
    V = dict(ffn_norm=v_ffn_norm, ffn_w_gate=v_ffn_w_gate, ffn_w_up=v_ffn_w_up, ffn_w_down=v_ffn_w_down, mix_norm=v_mix_norm, final_norm=v_final_norm, conv_w_in=v_conv_w_in, conv_b_in=v_conv_b_in, conv_dw=v_conv_dw, conv_dw_b=v_conv_dw_b, conv_ln_g=v_conv_ln_g, conv_ln_b=v_conv_ln_b, conv_w_out=v_conv_w_out, fox_w_in=v_fox_w_in, fox_b_f=v_fox_b_f, fox_w_out=v_fox_w_out, hgrn_w_in=v_hgrn_w_in, hgrn_lb_logits=v_hgrn_lb_logits, hgrn_norm=v_hgrn_norm, hgrn_w_out=v_hgrn_w_out, pool_w=v_pool_w, pool_scale=v_pool_scale)

    px, py, pc = _pos()
    jme = 2 * px + py
    pos = jnp.stack([pc, jme]).astype(jnp.int32)
    S = x.shape[1]
    T = x.shape[0] * S
    x2 = x.reshape(T, D)
    tgt = loss_target.reshape(T, D)

    flat = lambda a: a.reshape(-1, a.shape[-1])
    gathered = all_gather_chips([flat(W[n]).astype(MXU) for n in _BIG] + [_pack_sharded_small(W.get)])
    G = dict(zip(_BIG, gathered[:-1]))
    small = gathered[-1].transpose(1, 0, 2).reshape(48, D)
    ffn_norm_f, conv_dw_f = small[0:8], small[8:40]
    hgrn_norm_f, pool_scale_f = small[40:41], small[41:42]
    wg_all = G['ffn_w_gate'].reshape(NSH, 2 * DEPTH, D, FS)
    wu_all = G['ffn_w_up'].reshape(NSH, 2 * DEPTH, D, FS)
    wd_all = G['ffn_w_down'].reshape(NSH, 2 * DEPTH, FS, D)
    conv_wi = G['conv_w_in']
    conv_wo = G['conv_w_out'].reshape(D, D)
    fox_full = jnp.pad(G['fox_w_in'].transpose(1, 0, 2).reshape(D, FOX_N), ((0, 0), (0, FOX_NP - FOX_N)))
    fox_w5 = fox_full.reshape(D, 5, FOX_NP // 5).transpose(1, 0, 2)
    fox_wf = fox_full[:, 3 * D:][None]
    fox_bf = jnp.pad(fox_b_f, ((0, 0), (0, 128 - FOX_H)))
    fox_wo = G['fox_w_out'].reshape(D, D)
    hgrn_wi = G['hgrn_w_in']
    hgrn_wo = G['hgrn_w_out'].reshape(D, D)
    pool_wf = G['pool_w'].reshape(NSH, 4, 64, POOL_G).transpose(1, 0, 2, 3).reshape(4, POOL_G, POOL_G)
    conv_bi = conv_b_in.reshape(NSH, 1, 2 * D // NSH)

    def ffn_f(xs, e):
        xo, h, u, sa, z = ffn_fwd(xs, ffn_norm_f[e:e + 1], wg_all, wu_all, wd_all, e, tm=min(1024, xs.shape[0]))
        return xo, (xs, h, u, sa, z)

    saved = []
    xs = x2
    lb = lb_fwd(hgrn_lb_logits)
    for i in range(DEPTH):
        xs, r0 = ffn_f(xs, 2 * i)
        gm = mix_norm[i:i + 1]
        xin = xs
        if i == 0:
            p, h = norm_mm(xin, gm, conv_wi, conv_bi, name="conv_in", out_dtype=F32)
            u2, u4 = conv_fwd_core(p, conv_dw_f, conv_dw_b, conv_ln_g, conv_ln_b, S=S)
            xs = mm_res(u4, conv_wo, xin, name="conv_out")
            rm = (xin, p, h, u2, u4)
        elif i == 1:
            p, h = norm_mm(xin, gm, fox_w5, None, name="fox_in", out_dtype=MXU)
            fl, _ = norm_mm(xin, gm, fox_wf, None, name="fox_in_f", out_dtype=F32)
            cq = fox_cum(fl, fox_bf, S=S)
            qa, ka, va = fox_prep(p, cq)
            o, o32, lse = fox2_fwd(qa, ka, va, S=S, tq=min(FOX_T, S), tk=min(FOX_T, S))
            xs = mm_res(o, fox_wo, xin, name="fox_out")
            rm = (xin, h, fl, qa, ka, va, o, o32, lse)
        elif i == 2:
            p, h = norm_mm(xin, gm, hgrn_wi, None, name="hgrn_in", out_dtype=F32)
            yh, oh, st = hgrn_fwd(p, lb, hgrn_norm_f, S=S)
            xs = mm_res(yh, hgrn_wo, xin, name="hgrn_out")
            rm = (xin, p, h, yh, oh, st)
        else:
            xs, mp = pool_fwd(xin, gm, pool_wf, pool_scale_f, S=S)
            rm = (xin, mp)
        xs, r1 = ffn_f(xs, 2 * i + 1)
        saved.append((r0, rm, r1))

    loss8, dx, d_final = loss_head(xs, final_norm.reshape(1, D), tgt)

    conv_wit, fox_w5t, hgrn_wit = (w.transpose(0, 2, 1) for w in (conv_wi, fox_w5, hgrn_wi))
    conv_wot, fox_wot, hgrn_wot = conv_wo.T, fox_wo.T, hgrn_wo.T
    gb = {'g': None, 'u': None, 'd': None}
    d_ffn_norm = [None] * (2 * DEPTH)
    d_mix_norm = [None] * DEPTH
    gbig = {}
    gsm = {}

    def ffn_b(dy, res, e):
        xin, h, u, sa, z = res
        dxo, da, db, dyh, dg = ffn_bwd_dx(xin, ffn_norm_f[e:e + 1], dy, u, sa, wg_all, wu_all, wd_all, e)
        tk = min(4096, xin.shape[0])
        gb['g'] = mm_tn(h, da, name="ffn_dwg", G=NSH, M=D, N=FS, tk=tk, stack=(2 * DEPTH, e, gb['g']))
        gb['u'] = mm_tn(h, db, name="ffn_dwu", G=NSH, M=D, N=FS, tk=tk, stack=(2 * DEPTH, e, gb['u']))
        gb['d'] = mm_tn(z, dyh, name="ffn_dwd", G=NSH, M=FS, N=D, tk=tk, stack=(2 * DEPTH, e, gb['d']))
        d_ffn_norm[e] = dg
        return dxo

    for i in reversed(range(DEPTH)):
        r0, rm, r1 = saved[i]
        dx = ffn_b(dx, r1, 2 * i + 1)
        gm = mix_norm[i:i + 1]
        if i == 0:
            xin, p, h, u2, u4 = rm
            du2, dyb, gsm['conv_ln_g'], gsm['conv_ln_b'], gsm['conv_dw_b'] = conv_bwd_rows(dx, conv_wot, u2, conv_ln_g, conv_ln_b)
            dp, gsm['conv_b_in'], ddw = conv_bwd_core(du2, p, conv_dw_f, S=S)
            gsm['conv_dw'] = ddw
            gbig['conv_w_in'] = mm_tn(h, dp, name="conv_dwin", G=NSH, M=D, N=2 * D // NSH, b_step=1)
            gbig['conv_w_out'] = mm_tn(u4, dyb, name="conv_dwout", G=NSH, M=QS, N=D, a_step=1)
            dx, d_mix_norm[i] = inproj_bwd(dp, conv_wit, xin, gm, dx, name="conv_in_bwd")
        elif i == 1:
            xin, h, fl, qa, ka, va, o, o32, lse = rm
            do, dyb = mm_nt(dx, fox_wot, name="fox_out_bwd")
            qb, da = fox2_prep_bwd(do, o32, lse, qa)
            dq = fox2_dq(qb, ka, va, da, S=S, tq=min(FOX_T, S))
            dk, dv, dck = fox2_dkv(qb, ka, va, da, S=S, tq=min(FOX_T, S))
            dfl, dbf = fox_fin(dck, fl, fox_bf, S=S)
            gsm['fox_b_f'] = dbf
            dp = jnp.concatenate([dq, dk, dv, dfl], axis=1)
            dw5 = mm_tn(h, dp, name="fox_dwin", G=5, M=D, N=FOX_NP // 5, b_step=1)
            dwf = dw5.transpose(1, 0, 2).reshape(D, FOX_NP)[:, :FOX_N]
            gbig['fox_w_in'] = dwf.reshape(D, NSH, FOX_N // NSH).transpose(1, 0, 2)
            gbig['fox_w_out'] = mm_tn(o, dyb, name="fox_dwout", G=NSH, M=QS, N=D, a_step=1)
            dx, d_mix_norm[i] = inproj_bwd(dp, fox_w5t, xin, gm, dx, name="fox_in_bwd")
        elif i == 2:
            xin, p, h, yh, oh, st = rm
            dyo, dyb = mm_nt(dx, hgrn_wot, name="hgrn_out_bwd")
            dp, dlb, gsm['hgrn_norm'] = hgrn_bwd(p, oh, dyo, st, lb, hgrn_norm_f, S=S)
            gsm['hgrn_lb_logits'] = lb_bwd(hgrn_lb_logits, dlb)
            gbig['hgrn_w_in'] = mm_tn(h, dp, name="hgrn_dwin", G=NSH, M=D, N=D, b_step=1)
            gbig['hgrn_w_out'] = mm_tn(yh, dyb, name="hgrn_dwout", G=NSH, M=QS, N=D, a_step=1)
            dx, d_mix_norm[i] = inproj_bwd(dp, hgrn_wit, xin, gm, dx, name="hgrn_in_bwd")
        else:
            xin, mp = rm
            dmc, dyp, gsm['pool_scale'] = pool_bwd_rows(dx, mp, pool_wf, pool_scale_f, S=S)
            dwp = mm_tn(mp, dyp, name="pool_dw", G=4, M=POOL_G, N=POOL_G, a_step=1, b_step=1)
            gbig['pool_w'] = dwp.reshape(4, NSH, 64, POOL_G).transpose(1, 0, 2, 3).reshape(NSH, 4 * 64, POOL_G)
            dx, d_mix_norm[i] = pool_bwd_core(dmc, xin, gm, dx, S=S)
        dx = ffn_b(dx, r0, 2 * i)

    gbig['ffn_w_gate'] = gb['g'].reshape(NSH, 2 * DEPTH * D, FS)
    gbig['ffn_w_up'] = gb['u'].reshape(NSH, 2 * DEPTH * D, FS)
    gbig['ffn_w_down'] = gb['d'].reshape(NSH, 2 * DEPTH * FS, D)

    gl = [gbig[n] for n in _BIG]
    r_sib = sibling_half_exchange(gl)
    s16 = [pair_sum(g, r, pos) for g, r in zip(gl, r_sib)]
    r_ici = chip_scatter(s16)
    red = [reduce_own(g, rs, ri, pos) for g, rs, ri in zip(gl, r_sib, r_ici)]
    oth = sibling_exchange(red)
    out = {}
    for n, ga, gb_ in zip(_BIG, red, oth):
        res = adamw_halves(flat(W[n]), flat(M[n]), flat(V[n]), ga, gb_, pos)
        out[n] = [r.reshape(W[n].shape) for r in res]

    gfull = {'mix_norm': jnp.concatenate(d_mix_norm, axis=0), 'final_norm': d_final,
             'conv_b_in': gsm['conv_b_in'], 'conv_dw_b': gsm['conv_dw_b'], 'conv_ln_g': gsm['conv_ln_g'],
             'conv_ln_b': gsm['conv_ln_b'], 'fox_b_f': gsm['fox_b_f'][:, :FOX_H], 'hgrn_lb_logits': gsm['hgrn_lb_logits'],
             'ffn_norm': jnp.concatenate(d_ffn_norm, axis=0), 'conv_dw': gsm['conv_dw'][None, :CONV_W],
             'hgrn_norm': gsm['hgrn_norm'], 'pool_scale': gsm['pool_scale']}
    gpack = jnp.concatenate([_pack_replicated(gfull.get), _pack_sharded_small(gfull.get)], axis=0)
    gall = all_gather_devices(gpack)
    rep = adamw_sum(gall[:, :24], _pack_replicated(W.get), _pack_replicated(M.get), _pack_replicated(V.get))
    rep = [_unpack_replicated(r) for r in rep]
    for n in _REPLICATED:
        out[n] = [r[n].reshape(W[n].shape) for r in rep]
    gsh = lax.dynamic_slice_in_dim(gall[:, 24:], jme * QS, QS, axis=2)
    shd = adamw_sum(gsh, _pack_sharded_small(W.get), _pack_sharded_small(M.get), _pack_sharded_small(V.get))
    shd = [_unpack_sharded_small(r) for r in shd]
    for n in _SHARDED_SMALL:
        out[n] = [r[n].reshape(W[n].shape) for r in shd]

    loss = lax.psum(loss8[0, 0], ("x", "y", "c"))
    res = [loss, dx.reshape(x.shape)]
    for k in range(4):
        res += [out[n][k] for n in _WEIGHTS]
    return tuple(res)
```

```python
import jax
import jax.numpy as jnp
from jax import lax
from jax.experimental import pallas as pl
from jax.experimental.pallas import tpu as pltpu

D = 1024
F = 2816
NSH = 4
FS = F // NSH
DEPTH = 4
RMS_EPS = 1e-6
LN_EPS = 1e-5
CONV_W = 31
HALO = 32
FOX_H = 16
FOX_DH = 64
HG_H = 8
HG_DK = 128
HG_C = 32
POOL_WIN = (2, 4, 8, 16)
POOL_G = 256
MXU = jnp.bfloat16
F32 = jnp.float32
VMEM_LIMIT = 52 * 1024 * 1024

ADAM_LR = 0.001
ADAM_B1 = 0.9
ADAM_B2 = 0.999
ADAM_EPS = 1e-08
ADAM_WD = 0.01
ADAM_STEP = 10


def _call(body, *, name, grid, in_specs, out_specs, out_shape, scratch=(), vmem=VMEM_LIMIT):
    return pl.pallas_call(
        body, name=name, grid=grid, in_specs=in_specs, out_specs=out_specs, out_shape=out_shape,
        scratch_shapes=list(scratch),
        compiler_params=pltpu.CompilerParams(dimension_semantics=("arbitrary",) * len(grid),
                                             vmem_limit_bytes=vmem))


def _dot(a, b):
    return jnp.dot(a, b, preferred_element_type=F32)


def _dot_nt(a, b):
    return lax.dot_general(a, b, (((1,), (1,)), ((), ())), preferred_element_type=F32)


def _dot_tn(a, b):
    return lax.dot_general(a, b, (((0,), (0,)), ((), ())), preferred_element_type=F32)


def _split(x):
    hi = x.astype(MXU)
    return hi, (x - hi.astype(F32)).astype(MXU)


def _sigmoid(x):
    return 1.0 / (1.0 + jnp.exp(-x))


def _rms(x, g):
    r = lax.rsqrt(jnp.mean(x * x, axis=-1, keepdims=True) + RMS_EPS)
    return x * r * g


def _rms_bwd(dh, x, g):
    r = lax.rsqrt(jnp.mean(x * x, axis=-1, keepdims=True) + RMS_EPS)
    xh = x * r
    dhg = dh * g
    dx = r * (dhg - xh * jnp.mean(dhg * xh, axis=-1, keepdims=True))
    return dx, jnp.sum(dh * xh, axis=0, keepdims=True)


def _sds(shape, dtype):
    return jax.ShapeDtypeStruct(shape, dtype)


def _wspec(w, e):
    if w.ndim == 3:
        return pl.BlockSpec((None,) + w.shape[1:], lambda i, j: (j, 0, 0))
    return pl.BlockSpec((None, None) + w.shape[2:], lambda i, j: (j, e, 0, 0))


def ffn_fwd(x, g, wg, wu, wd, e=0, *, tm=512):
    T = x.shape[0]

    def body(x_ref, g_ref, wg_ref, wu_ref, wd_ref, xo_ref, h_ref, u_ref, sa_ref, z_ref, acc_ref):
        j = pl.program_id(1)

        @pl.when(j == 0)
        def _():
            h_ref[...] = _rms(x_ref[...], g_ref[...]).astype(MXU)
            acc_ref[...] = jnp.zeros_like(acc_ref)

        h = h_ref[...]
        a = _dot(h, wg_ref[...])
        b = _dot(h, wu_ref[...])
        s = _sigmoid(a)
        sa = a * s
        u_ref[...] = (b * (s * (1.0 + a * (1.0 - s)))).astype(MXU)
        sa_ref[...] = sa.astype(MXU)
        z = (sa * b).astype(MXU)
        z_ref[...] = z
        acc_ref[...] += _dot(z, wd_ref[...])

        @pl.when(j == NSH - 1)
        def _():
            xo_ref[...] = x_ref[...] + 0.5 * acc_ref[...]

    return _call(
        body, name="ffn_fwd", grid=(T // tm, NSH),
        in_specs=[pl.BlockSpec((tm, D), lambda i, j: (i, 0)),
                  pl.BlockSpec((1, D), lambda i, j: (0, 0)),
                  _wspec(wg, e), _wspec(wu, e), _wspec(wd, e)],
        out_specs=[pl.BlockSpec((tm, D), lambda i, j: (i, 0)),
                   pl.BlockSpec((tm, D), lambda i, j: (i, 0)),
                   pl.BlockSpec((None, tm, FS), lambda i, j: (j, i, 0)),
                   pl.BlockSpec((None, tm, FS), lambda i, j: (j, i, 0)),
                   pl.BlockSpec((None, tm, FS), lambda i, j: (j, i, 0))],
        out_shape=[_sds((T, D), F32), _sds((T, D), MXU)] + [_sds((NSH, T, FS), MXU)] * 3,
        scratch=[pltpu.VMEM((tm, D), F32)],
    )(x, g, wg, wu, wd)


def ffn_bwd_dx(x, g, dy, u, sa, wg, wu, wd, e=0, *, tm=512):
    T = x.shape[0]

    def body(x_ref, g_ref, dy_ref, u_ref, sa_ref, wg_ref, wu_ref, wd_ref,
             dx_ref, da_ref, db_ref, dyh_ref, dg_ref):
        i = pl.program_id(0)
        j = pl.program_id(1)
        acc_ref = dx_ref

        @pl.when(j == 0)
        def _():
            dyh_ref[...] = (0.5 * dy_ref[...]).astype(MXU)
            acc_ref[...] = jnp.zeros_like(acc_ref)

        @pl.when((i == 0) & (j == 0))
        def _():
            dg_ref[...] = jnp.zeros_like(dg_ref)

        dz = _dot_nt(dyh_ref[...], wd_ref[...])
        da = (dz * u_ref[...].astype(F32)).astype(MXU)
        db = (dz * sa_ref[...].astype(F32)).astype(MXU)
        da_ref[...] = da
        db_ref[...] = db
        acc_ref[...] += _dot_nt(da, wg_ref[...]) + _dot_nt(db, wu_ref[...])

        @pl.when(j == NSH - 1)
        def _():
            dxn, dg = _rms_bwd(acc_ref[...], x_ref[...], g_ref[...])
            dx_ref[...] = dy_ref[...] + dxn
            dg_ref[...] += dg

    return _call(
        body, name="ffn_bwd_dx", grid=(T // tm, NSH),
        in_specs=[pl.BlockSpec((tm, D), lambda i, j: (i, 0)),
                  pl.BlockSpec((1, D), lambda i, j: (0, 0)),
                  pl.BlockSpec((tm, D), lambda i, j: (i, 0)),
                  pl.BlockSpec((None, tm, FS), lambda i, j: (j, i, 0)),
                  pl.BlockSpec((None, tm, FS), lambda i, j: (j, i, 0)),
                  _wspec(wg, e), _wspec(wu, e), _wspec(wd, e)],
        out_specs=[pl.BlockSpec((tm, D), lambda i, j: (i, 0)),
                   pl.BlockSpec((None, tm, FS), lambda i, j: (j, i, 0)),
                   pl.BlockSpec((None, tm, FS), lambda i, j: (j, i, 0)),
                   pl.BlockSpec((tm, D), lambda i, j: (i, 0)),
                   pl.BlockSpec((1, D), lambda i, j: (0, 0))],
        out_shape=[_sds((T, D), F32), _sds((NSH, T, FS), MXU), _sds((NSH, T, FS), MXU),
                   _sds((T, D), MXU), _sds((1, D), F32)],
    )(x, g, dy, u, sa, wg, wu, wd)


def mm_tn(a, b, *, name, G, M, N, a_step=0, b_step=0, tk=512, stack=None):
    T = a.shape[-2]
    if stack is not None:
        return _mm_tn_stack(a, b, name=name, G=G, M=M, N=N, tk=tk, stack=stack)

    def spec(arr, width, step):
        if arr.ndim == 3:
            return pl.BlockSpec((None, tk, width), lambda g, k: (g, k, 0))
        return pl.BlockSpec((tk, width), lambda g, k: (k, g * step))

    def body(a_ref, b_ref, o_ref):
        @pl.when(pl.program_id(1) == 0)
        def _():
            o_ref[...] = jnp.zeros_like(o_ref)

        o_ref[...] += _dot_tn(a_ref[...], b_ref[...])

    return _call(
        body, name=name, grid=(G, T // tk),
        in_specs=[spec(a, M, a_step), spec(b, N, b_step)],
        out_specs=pl.BlockSpec((None, M, N), lambda g, k: (g, 0, 0)),
        out_shape=_sds((G, M, N), F32),
    )(a, b)


def norm_mm(x, g, wb, bias, *, name, out_dtype, tm=1024):
    T = x.shape[0]
    tm = min(tm, T)
    G, _, ns = wb.shape
    has_bias = bias is not None

    def body(*refs):
        if has_bias:
            x_ref, g_ref, w_ref, bias_ref, p_ref, h_ref = refs
        else:
            x_ref, g_ref, w_ref, p_ref, h_ref = refs

        @pl.when(pl.program_id(1) == 0)
        def _():
            h_ref[...] = _rms(x_ref[...], g_ref[...]).astype(MXU)

        p = _dot(h_ref[...], w_ref[...])
        if has_bias:
            p = p + bias_ref[...]
        p_ref[...] = p.astype(out_dtype)

    in_specs = [pl.BlockSpec((tm, D), lambda i, j: (i, 0)),
                pl.BlockSpec((1, D), lambda i, j: (0, 0)),
                pl.BlockSpec((None, D, ns), lambda i, j: (j, 0, 0))]
    args = [x, g, wb]
    if has_bias:
        in_specs.append(pl.BlockSpec((None, 1, ns), lambda i, j: (j, 0, 0)))
        args.append(bias)
    return _call(
        body, name=name, grid=(T // tm, G), in_specs=in_specs,
        out_specs=[pl.BlockSpec((tm, ns), lambda i, j: (i, j)),
                   pl.BlockSpec((tm, D), lambda i, j: (i, 0))],
        out_shape=[_sds((T, G * ns), out_dtype), _sds((T, D), MXU)],
    )(*args)


def mm_res(y, w, x, *, name, tm=1024):
    T, K = y.shape
    tm = min(tm, T)

    def body(y_ref, w_ref, x_ref, o_ref):
        o_ref[...] = x_ref[...] + _dot(y_ref[...], w_ref[...])

    return _call(
        body, name=name, grid=(T // tm,),
        in_specs=[pl.BlockSpec((tm, K), lambda i: (i, 0)),
                  pl.BlockSpec((K, D), lambda i: (0, 0)),
                  pl.BlockSpec((tm, D), lambda i: (i, 0))],
        out_specs=pl.BlockSpec((tm, D), lambda i: (i, 0)),
        out_shape=_sds((T, D), F32),
    )(y, w, x)


def mm_nt(a, wt, *, name, tm=1024):
    T, K = a.shape
    tm = min(tm, T)
    N = wt.shape[1]
    w = wt

    def body(a_ref, w_ref, o_ref, ab_ref):
        ab = a_ref[...].astype(MXU)
        ab_ref[...] = ab
        o_ref[...] = _dot(ab, w_ref[...])

    return _call(
        body, name=name, grid=(T // tm,),
        in_specs=[pl.BlockSpec((tm, K), lambda i: (i, 0)),
                  pl.BlockSpec((K, N), lambda i: (0, 0))],
        out_specs=[pl.BlockSpec((tm, N), lambda i: (i, 0)),
                   pl.BlockSpec((tm, K), lambda i: (i, 0))],
        out_shape=[_sds((T, N), F32), _sds((T, K), MXU)],
    )(a, w)


def inproj_bwd(dp, wb, x, g, dres, *, name, tm=1024):
    T = x.shape[0]
    tm = min(tm, T)
    G, ns, _ = wb.shape

    def body(dp_ref, w_ref, x_ref, g_ref, dres_ref, dx_ref, dg_ref, acc_ref):
        i = pl.program_id(0)
        j = pl.program_id(1)

        @pl.when(j == 0)
        def _():
            acc_ref[...] = jnp.zeros_like(acc_ref)

        @pl.when((i == 0) & (j == 0))
        def _():
            dg_ref[...] = jnp.zeros_like(dg_ref)

        acc_ref[...] += _dot(dp_ref[...], w_ref[...])

        @pl.when(j == G - 1)
        def _():
            dxn, dg = _rms_bwd(acc_ref[...], x_ref[...], g_ref[...])
            dx_ref[...] = dres_ref[...] + dxn
            dg_ref[...] += dg

    return _call(
        body, name=name, grid=(T // tm, G),
        in_specs=[pl.BlockSpec((tm, ns), lambda i, j: (i, j)),
                  pl.BlockSpec((None, ns, D), lambda i, j: (j, 0, 0)),
                  pl.BlockSpec((tm, D), lambda i, j: (i, 0)),
                  pl.BlockSpec((1, D), lambda i, j: (0, 0)),
                  pl.BlockSpec((tm, D), lambda i, j: (i, 0))],
        out_specs=[pl.BlockSpec((tm, D), lambda i, j: (i, 0)),
                   pl.BlockSpec((1, D), lambda i, j: (0, 0))],
        out_shape=[_sds((T, D), F32), _sds((1, D), F32)],
        scratch=[pltpu.VMEM((tm, D), F32)],
    )(dp, wb, x, g, dres)


def loss_head(x, gf, tgt, *, tm=512):
    T = x.shape[0]

    def body(x_ref, g_ref, t_ref, loss_ref, dx_ref, dg_ref):
        @pl.when(pl.program_id(0) == 0)
        def _():
            loss_ref[...] = jnp.zeros_like(loss_ref)
            dg_ref[...] = jnp.zeros_like(dg_ref)

        xv = x_ref[...]
        gv = g_ref[...]
        e = _rms(xv, gv) - t_ref[...]
        loss_ref[...] += (0.5 / D) * jnp.sum(e * e)
        dxn, dg = _rms_bwd(e * (1.0 / D), xv, gv)
        dx_ref[...] = dxn
        dg_ref[...] += dg

    return _call(
        body, name="loss_head", grid=(T // tm,),
        in_specs=[pl.BlockSpec((tm, D), lambda i: (i, 0)),
                  pl.BlockSpec((1, D), lambda i: (0, 0)),
                  pl.BlockSpec((tm, D), lambda i: (i, 0))],
        out_specs=[pl.BlockSpec((8, 128), lambda i: (0, 0)),
                   pl.BlockSpec((tm, D), lambda i: (i, 0)),
                   pl.BlockSpec((1, D), lambda i: (0, 0))],
        out_shape=[_sds((8, 128), F32), _sds((T, D), F32), _sds((1, D), F32)],
    )(x, gf, tgt)


def _glu(p):
    return p[:, :D] * _sigmoid(p[:, D:])


def _ln_stats(u):
    mu = jnp.mean(u, axis=-1, keepdims=True)
    xc = u - mu
    rstd = lax.rsqrt(jnp.mean(xc * xc, axis=-1, keepdims=True) + LN_EPS)
    return xc * rstd, rstd


def conv_fwd_core(p, dw, dwb, lng, lnb, *, S, tt=256):
    T = p.shape[0]
    nb = S // tt
    r = tt // HALO

    def body(pc_ref, pp_ref, dw_ref, dwb_ref, lng_ref, lnb_ref, u2_ref, u4_ref, ubuf):
        first = (pl.program_id(0) % nb) == 0
        ubuf[0:HALO, :] = jnp.where(first, 0.0, _glu(pp_ref[...]))
        ubuf[HALO:, :] = _glu(pc_ref[...])
        for c in range(D // 128):
            cs = slice(c * 128, (c + 1) * 128)
            acc = jnp.zeros((tt, 128), F32)
            for k in range(CONV_W):
                acc = acc + dw_ref[k:k + 1, cs] * ubuf[k + 2:k + 2 + tt, cs]
            u2_ref[:, cs] = acc + dwb_ref[:, cs]
        xh, _ = _ln_stats(u2_ref[...])
        u3 = xh * lng_ref[...] + lnb_ref[...]
        u4_ref[...] = (u3 * _sigmoid(u3)).astype(MXU)

    row = pl.BlockSpec((1, D), lambda i: (0, 0))
    return _call(
        body, name="conv_fwd_core", grid=(T // tt,),
        in_specs=[pl.BlockSpec((tt, 2 * D), lambda i: (i, 0)),
                  pl.BlockSpec((HALO, 2 * D), lambda i: (jnp.maximum(i * r - 1, 0), 0)),
                  pl.BlockSpec((HALO, D), lambda i: (0, 0)), row, row, row],
        out_specs=[pl.BlockSpec((tt, D), lambda i: (i, 0)), pl.BlockSpec((tt, D), lambda i: (i, 0))],
        out_shape=[_sds((T, D), F32), _sds((T, D), MXU)],
        scratch=[pltpu.VMEM((tt + HALO, D), F32)],
    )(p, p, dw, dwb, lng, lnb)


def conv_bwd_rows(dy, wout, u2, lng, lnb, *, tm=512):
    T = dy.shape[0]

    def body(dy_ref, w_ref, u2_ref, lng_ref, lnb_ref, du2_ref, dyb_ref, dlng_ref, dlnb_ref, ddwb_ref):
        @pl.when(pl.program_id(0) == 0)
        def _():
            dlng_ref[...] = jnp.zeros_like(dlng_ref)
            dlnb_ref[...] = jnp.zeros_like(dlnb_ref)
            ddwb_ref[...] = jnp.zeros_like(ddwb_ref)

        dyb = dy_ref[...].astype(MXU)
        dyb_ref[...] = dyb
        du4 = _dot(dyb, w_ref[...])
        xh, rstd = _ln_stats(u2_ref[...])
        lng_v = lng_ref[...]
        u3 = xh * lng_v + lnb_ref[...]
        s = _sigmoid(u3)
        du3 = du4 * (s * (1.0 + u3 * (1.0 - s)))
        dlng_ref[...] += jnp.sum(du3 * xh, axis=0, keepdims=True)
        dlnb_ref[...] += jnp.sum(du3, axis=0, keepdims=True)
        dxh = du3 * lng_v
        du2 = rstd * (dxh - jnp.mean(dxh, axis=-1, keepdims=True)
                      - xh * jnp.mean(dxh * xh, axis=-1, keepdims=True))
        du2_ref[...] = du2
        ddwb_ref[...] += jnp.sum(du2, axis=0, keepdims=True)

    row = pl.BlockSpec((1, D), lambda i: (0, 0))
    blk = pl.BlockSpec((tm, D), lambda i: (i, 0))
    return _call(
        body, name="conv_bwd_rows", grid=(T // tm,),
        in_specs=[blk, pl.BlockSpec((D, D), lambda i: (0, 0)), blk, row, row],
        out_specs=[blk, blk, row, row, row],
        out_shape=[_sds((T, D), F32), _sds((T, D), MXU), _sds((1, D), F32), _sds((1, D), F32), _sds((1, D), F32)],
    )(dy, wout, u2, lng, lnb)


def conv_bwd_core(du2, p, dw, *, S, tt=256):
    T = p.shape[0]
    nb = S // tt
    r = tt // HALO
    last_halo = T // HALO - 1

    def body(dc_ref, dn_ref, pc_ref, pp_ref, dw_ref, dp_ref, dbin_ref, ddw_ref, ubuf, dbuf, dd):
        i = pl.program_id(0)

        @pl.when(i == 0)
        def _():
            dbin_ref[...] = jnp.zeros_like(dbin_ref)
            dd[...] = jnp.zeros_like(dd)

        first = (i % nb) == 0
        last = (i % nb) == nb - 1
        ubuf[0:HALO, :] = jnp.where(first, 0.0, _glu(pp_ref[...]))
        ubuf[HALO:, :] = _glu(pc_ref[...])
        dbuf[0:tt, :] = dc_ref[...]
        dbuf[tt:, :] = jnp.where(last, 0.0, dn_ref[...])
        pc = pc_ref[...]
        for c in range(D // 128):
            cs = slice(c * 128, (c + 1) * 128)
            dcur = dbuf[0:tt, cs]
            du = jnp.zeros((tt, 128), F32)
            for k in range(CONV_W):
                dd[k, :, cs] += jnp.sum((dcur * ubuf[k + 2:k + 2 + tt, cs]).reshape(tt // 8, 8, 128), axis=0)
                du = du + dw_ref[k:k + 1, cs] * dbuf[CONV_W - 1 - k:CONV_W - 1 - k + tt, cs]
            a = pc[:, c * 128:(c + 1) * 128]
            sb = _sigmoid(pc[:, D + c * 128:D + (c + 1) * 128])
            da = du * sb
            db = du * a * sb * (1.0 - sb)
            dp_ref[:, cs] = da.astype(MXU)
            dp_ref[:, D + c * 128:D + (c + 1) * 128] = db.astype(MXU)
            dbin_ref[:, cs] += jnp.sum(da, axis=0, keepdims=True)
            dbin_ref[:, D + c * 128:D + (c + 1) * 128] += jnp.sum(db, axis=0, keepdims=True)

        @pl.when(i == T // tt - 1)
        def _():
            ddw_ref[...] = jnp.zeros_like(ddw_ref)
            for k in range(CONV_W):
                ddw_ref[k:k + 1, :] = jnp.sum(dd[k], axis=0, keepdims=True)

    return _call(
        body, name="conv_bwd_core", grid=(T // tt,),
        in_specs=[pl.BlockSpec((tt, D), lambda i: (i, 0)),
                  pl.BlockSpec((HALO, D), lambda i: (jnp.minimum((i + 1) * r, last_halo), 0)),
                  pl.BlockSpec((tt, 2 * D), lambda i: (i, 0)),
                  pl.BlockSpec((HALO, 2 * D), lambda i: (jnp.maximum(i * r - 1, 0), 0)),
                  pl.BlockSpec((HALO, D), lambda i: (0, 0))],
        out_specs=[pl.BlockSpec((tt, 2 * D), lambda i: (i, 0)),
                   pl.BlockSpec((1, 2 * D), lambda i: (0, 0)),
                   pl.BlockSpec((HALO, D), lambda i: (0, 0))],
        out_shape=[_sds((T, 2 * D), MXU), _sds((1, 2 * D), F32), _sds((HALO, D), F32)],
        scratch=[pltpu.VMEM((tt + HALO, D), F32), pltpu.VMEM((tt + HALO, D), F32), pltpu.VMEM((HALO, 8, D), F32)],
    )(du2, du2, p, p, dw)


PH = 16


def _pool_cnt(i, nb, tt, win):
    pos = (i % nb) * tt + lax.broadcasted_iota(jnp.int32, (tt, 1), 0)
    return jnp.minimum(pos + 1, win).astype(F32)


def pool_fwd(x, g, wp, scale, *, S, tt=256):
    T = x.shape[0]
    nb = S // tt
    r = tt // PH

    def body(xc_ref, xp_ref, g_ref, wp_ref, sc_ref, xo_ref, m_ref, hbuf):
        i = pl.program_id(0)
        first = (i % nb) == 0
        gv = g_ref[...]
        hbuf[0:PH, :] = jnp.where(first, 0.0, _rms(xp_ref[...], gv))
        xc = xc_ref[...]
        hbuf[PH:, :] = _rms(xc, gv)
        for gi, win in enumerate(POOL_WIN):
            gs = slice(gi * POOL_G, (gi + 1) * POOL_G)
            acc = hbuf[PH:PH + tt, gs]
            for j in range(1, win):
                acc = acc + hbuf[PH - j:PH - j + tt, gs]
            m = (acc / _pool_cnt(i, nb, tt, win) - hbuf[PH:PH + tt, gs]).astype(MXU)
            m_ref[:, gs] = m
            xo_ref[:, gs] = xc[:, gs] + _dot(m, wp_ref[gi]) * sc_ref[:, gs]

    row = pl.BlockSpec((1, D), lambda i: (0, 0))
    blk = pl.BlockSpec((tt, D), lambda i: (i, 0))
    return _call(
        body, name="pool_fwd", grid=(T // tt,),
        in_specs=[blk, pl.BlockSpec((PH, D), lambda i: (jnp.maximum(i * r - 1, 0), 0)), row,
                  pl.BlockSpec((len(POOL_WIN), POOL_G, POOL_G), lambda i: (0, 0, 0)), row],
        out_specs=[blk, blk],
        out_shape=[_sds((T, D), F32), _sds((T, D), MXU)],
        scratch=[pltpu.VMEM((tt + PH, D), F32)],
    )(x, x, g, wp, scale)


def pool_bwd_rows(dy, m, wp, scale, *, S, tt=256):
    T = dy.shape[0]
    nb = S // tt

    def body(dy_ref, m_ref, wp_ref, sc_ref, dmc_ref, dyp_ref, dsc_ref):
        i = pl.program_id(0)

        @pl.when(i == 0)
        def _():
            dsc_ref[...] = jnp.zeros_like(dsc_ref)

        for gi, win in enumerate(POOL_WIN):
            gs = slice(gi * POOL_G, (gi + 1) * POOL_G)
            dyg = dy_ref[:, gs]
            w = wp_ref[gi]
            dsc_ref[:, gs] += jnp.sum(dyg * _dot(m_ref[:, gs], w), axis=0, keepdims=True)
            dyp = (dyg * sc_ref[:, gs]).astype(MXU)
            dyp_ref[:, gs] = dyp
            dmc_ref[:, gs] = _dot_nt(dyp, w) / _pool_cnt(i, nb, tt, win)

    row = pl.BlockSpec((1, D), lambda i: (0, 0))
    blk = pl.BlockSpec((tt, D), lambda i: (i, 0))
    return _call(
        body, name="pool_bwd_rows", grid=(T // tt,),
        in_specs=[blk, blk, pl.BlockSpec((len(POOL_WIN), POOL_G, POOL_G), lambda i: (0, 0, 0)), row],
        out_specs=[blk, blk, row],
        out_shape=[_sds((T, D), F32), _sds((T, D), MXU), _sds((1, D), F32)],
    )(dy, m, wp, scale)


def pool_bwd_core(dmc, x, g, dres, *, S, tt=256):
    T = x.shape[0]
    nb = S // tt
    r = tt // PH
    last_halo = T // PH - 1

    def body(dc_ref, dn_ref, x_ref, g_ref, dres_ref, dx_ref, dg_ref, dbuf, dh_buf):
        i = pl.program_id(0)

        @pl.when(i == 0)
        def _():
            dg_ref[...] = jnp.zeros_like(dg_ref)

        last = (i % nb) == nb - 1
        dbuf[0:tt, :] = dc_ref[...]
        dbuf[tt:, :] = jnp.where(last, 0.0, dn_ref[...])
        for gi, win in enumerate(POOL_WIN):
            gs = slice(gi * POOL_G, (gi + 1) * POOL_G)
            cur = dbuf[0:tt, gs]
            acc = cur
            for j in range(1, win):
                acc = acc + dbuf[j:j + tt, gs]
            dh_buf[:, gs] = acc - cur * _pool_cnt(i, nb, tt, win)
        dxn, dg = _rms_bwd(dh_buf[...], x_ref[...], g_ref[...])
        dx_ref[...] = dres_ref[...] + dxn
        dg_ref[...] += dg

    row = pl.BlockSpec((1, D), lambda i: (0, 0))
    blk = pl.BlockSpec((tt, D), lambda i: (i, 0))
    return _call(
        body, name="pool_bwd_core", grid=(T // tt,),
        in_specs=[blk, pl.BlockSpec((PH, D), lambda i: (jnp.minimum((i + 1) * r, last_halo), 0)), blk, row, blk],
        out_specs=[blk, row],
        out_shape=[_sds((T, D), F32), _sds((1, D), F32)],
        scratch=[pltpu.VMEM((tt + PH, D), F32), pltpu.VMEM((tt, D), F32)],
    )(dmc, dmc, x, g, dres)


NEG = -1e30


def _tri(n, upper=False):
    r = lax.broadcasted_iota(jnp.int32, (n, n), 0)
    c = lax.broadcasted_iota(jnp.int32, (n, n), 1)
    return (r <= c if upper else r >= c).astype(F32)


def _dot_hi(a, b):
    return jnp.dot(a, b, preferred_element_type=F32, precision=lax.Precision.HIGHEST)


def _log_sigmoid(z):
    return jnp.minimum(z, 0.0) - jnp.log(1.0 + jnp.exp(-jnp.abs(z)))


def fox_cum(fl, bf, *, S, tt=256):
    T = fl.shape[0]
    nb = S // tt

    def body(fl_ref, bf_ref, c_ref, carry):
        i = pl.program_id(0)

        @pl.when((i % nb) == 0)
        def _():
            carry[...] = jnp.zeros_like(carry)

        lf = _log_sigmoid(fl_ref[...] + bf_ref[...])
        c = _dot_hi(_tri(tt), lf) + carry[...]
        c_ref[...] = c
        carry[...] = c[tt - 1:tt, :]

    return _call(
        body, name="fox_cum", grid=(T // tt,),
        in_specs=[pl.BlockSpec((tt, 128), lambda i: (i, 0)), pl.BlockSpec((1, 128), lambda i: (0, 0))],
        out_specs=pl.BlockSpec((tt, 128), lambda i: (i, 0)),
        out_shape=_sds((T, 128), F32),
        scratch=[pltpu.VMEM((1, 128), F32)],
    )(fl, bf)


HW = 128
FOX_T = 512
COL_ONE = FOX_DH + 3
COL_LSE = FOX_DH + 6


def _parts(x):
    hi = x.astype(MXU).astype(F32)
    mid = (x - hi).astype(MXU).astype(F32)
    lo = (x - hi - mid).astype(MXU).astype(F32)
    return [hi, mid, lo]


def _aug(n, cols):
    lane = lax.broadcasted_iota(jnp.int32, (n, HW - FOX_DH), 1)
    out = jnp.zeros((n, HW - FOX_DH), F32)
    for i, cval in enumerate(cols):
        out = jnp.where(lane == i, cval, out)
    return out


def fox_prep(p, c, *, tt=256):
    T = p.shape[0]

    def body(q_ref, k_ref, v_ref, c_ref, qa_ref, ka_ref, va_ref):
        ones = [1.0, 1.0, 1.0]
        for h in range(FOX_H):
            hs = slice(h * FOX_DH, (h + 1) * FOX_DH)
            lo, mid = h * HW, h * HW + FOX_DH
            cp = _parts(c_ref[:, h:h + 1])
            qa_ref[:, lo:mid] = (q_ref[:, hs].astype(F32) * (FOX_DH ** -0.5)).astype(MXU)
            qa_ref[:, mid:lo + HW] = _aug(tt, cp + ones).astype(MXU)
            ka_ref[:, lo:mid] = k_ref[:, hs]
            ka_ref[:, mid:lo + HW] = _aug(tt, ones + [-x for x in cp] + ones).astype(MXU)
            va_ref[:, lo:mid] = v_ref[:, hs]
            va_ref[:, mid:lo + HW] = _aug(tt, ones).astype(MXU)

    wide = pl.BlockSpec((tt, FOX_H * HW), lambda i: (i, 0))
    col = lambda k: pl.BlockSpec((tt, D), lambda i: (i, k))
    return _call(body, name="fox_prep", grid=(T // tt,),
                 in_specs=[col(0), col(1), col(2), pl.BlockSpec((tt, 128), lambda i: (i, 0))],
                 out_specs=[wide, wide, wide], out_shape=[_sds((T, FOX_H * HW), MXU)] * 3)(p, p, p, c)


def _causal(x, fill):
    r = lax.broadcasted_iota(jnp.int32, x.shape, 0)
    c = lax.broadcasted_iota(jnp.int32, x.shape, 1)
    return jnp.where(r >= c, x, fill)


def _wide_specs(tq, nq, q_of, k_of):
    qs = pl.BlockSpec((tq, FOX_H * HW), lambda b, i, j: (b * nq + q_of(i, j), 0))
    ks = pl.BlockSpec((tq, FOX_H * HW), lambda b, i, j: (b * nq + k_of(i, j), 0))
    return qs, ks


def fox2_fwd(qa, ka, va, *, S, tq=256, tk=512):
    T = qa.shape[0]
    tk = min(tk, S)
    nq = S // tq
    nk = S // tk
    r = tk // tq

    def body(q_ref, k_ref, v_ref, o_ref, o32_ref, lse_ref, m_sc, acc, acc_lo):
        qi = pl.program_id(1)
        ki = pl.program_id(2)
        last = qi // r

        @pl.when(ki == 0)
        def _():
            m_sc[...] = jnp.full_like(m_sc, NEG)
            acc[...] = jnp.zeros_like(acc)
            acc_lo[...] = jnp.zeros_like(acc_lo)

        def step(diag):
            for h in range(FOX_H):
                ws = slice(h * HW, (h + 1) * HW)
                s = _dot_nt(q_ref[:, ws], k_ref[:, ws])
                if diag:
                    row = qi * tq + lax.broadcasted_iota(jnp.int32, s.shape, 0)
                    col = ki * tk + lax.broadcasted_iota(jnp.int32, s.shape, 1)
                    s = jnp.where(row >= col, s, NEG)
                m_prev = m_sc[h]
                m_new = jnp.maximum(m_prev, jnp.max(s, axis=-1, keepdims=True))
                alpha = jnp.exp(m_prev - m_new)
                hi, lo = _split(jnp.exp(s - jnp.tile(m_new, (1, tk // 128))))
                acc[h] = alpha * acc[h] + _dot(hi, v_ref[:, ws])
                acc_lo[h] = alpha * acc_lo[h] + _dot(lo, v_ref[:, ws])
                m_sc[h] = m_new

        @pl.when(ki < last)
        def _():
            step(False)

        @pl.when(ki == last)
        def _():
            step(True)
            for h in range(FOX_H):
                hs = slice(h * FOX_DH, (h + 1) * FOX_DH)
                full = acc[h] + acc_lo[h]
                l = full[:, FOX_DH:FOX_DH + 1]
                o_ref[:, hs] = (acc[h][:, :FOX_DH] / l).astype(MXU)
                o32_ref[:, hs] = full[:, :FOX_DH] / l
                lse_ref[:, h:h + 1] = m_sc[h][:, 0:1] + jnp.log(l)

    qs = pl.BlockSpec((tq, FOX_H * HW), lambda b, i, j: (b * nq + i, 0))
    ks = pl.BlockSpec((tk, FOX_H * HW), lambda b, i, j: (b * nk + jnp.minimum(j, i // r), 0))
    orow = pl.BlockSpec((tq, D), lambda b, i, j: (b * nq + i, 0))
    return _call(
        body, name="fox_fwd", grid=(T // S, nq, nk),
        in_specs=[qs, ks, ks],
        out_specs=[orow, orow, pl.BlockSpec((tq, 128), lambda b, i, j: (b * nq + i, 0))],
        out_shape=[_sds((T, D), MXU), _sds((T, D), F32), _sds((T, 128), F32)],
        scratch=[pltpu.VMEM((FOX_H, tq, 128), F32), pltpu.VMEM((FOX_H, tq, HW), F32),
                 pltpu.VMEM((FOX_H, tq, HW), F32)],
    )(qa, ka, va)


def fox2_prep_bwd(do, o32, lse, qa, *, tt=256):
    T = do.shape[0]

    def body(do_ref, o_ref, lse_ref, qa_ref, qb_ref, da_ref):
        lane = lax.broadcasted_iota(jnp.int32, (tt, HW), 1)
        dob = do_ref[...].astype(MXU)
        for h in range(FOX_H):
            hs = slice(h * FOX_DH, (h + 1) * FOX_DH)
            ws = slice(h * HW, (h + 1) * HW)
            doh = dob[:, hs]
            delta = jnp.sum(doh.astype(F32) * o_ref[:, hs], axis=-1, keepdims=True)
            da_ref[:, h * HW:h * HW + FOX_DH] = doh
            da_ref[:, h * HW + FOX_DH:(h + 1) * HW] = _aug(tt, [-x for x in _parts(delta)]).astype(MXU)
            tile = qa_ref[:, ws]
            for i, part in enumerate(_parts(lse_ref[:, h:h + 1])):
                tile = jnp.where(lane == COL_LSE + i, (-part).astype(MXU), tile)
            qb_ref[:, ws] = tile

    wide = pl.BlockSpec((tt, FOX_H * HW), lambda i: (i, 0))
    blk = pl.BlockSpec((tt, D), lambda i: (i, 0))
    return _call(body, name="fox_prep_bwd", grid=(T // tt,),
                 in_specs=[blk, blk, pl.BlockSpec((tt, 128), lambda i: (i, 0)), wide],
                 out_specs=[wide, wide],
                 out_shape=[_sds((T, FOX_H * HW), MXU), _sds((T, FOX_H * HW), MXU)],
                 )(do, o32, lse, qa)


def fox2_dq(qb, ka, va, da, *, S, tq=256):
    T = qb.shape[0]
    nq = S // tq

    def body(q_ref, k_ref, v_ref, d_ref, dq_ref, acc):
        qi = pl.program_id(1)
        ki = pl.program_id(2)

        @pl.when(ki == 0)
        def _():
            acc[...] = jnp.zeros_like(acc)

        def step(diag):
            for h in range(FOX_H):
                ws = slice(h * HW, (h + 1) * HW)
                kh = k_ref[:, ws]
                p = jnp.exp(_dot_nt(q_ref[:, ws], kh))
                if diag:
                    p = _causal(p, 0.0)
                ds = (p * _dot_nt(d_ref[:, ws], v_ref[:, ws])).astype(MXU)
                acc[h] += _dot(ds, kh)

        @pl.when(ki < qi)
        def _():
            step(False)

        @pl.when(ki == qi)
        def _():
            step(True)
            for h in range(FOX_H):
                dq_ref[:, h * FOX_DH:(h + 1) * FOX_DH] = (acc[h][:, :FOX_DH] * (FOX_DH ** -0.5)).astype(MXU)

    qs, ks = _wide_specs(tq, nq, lambda i, j: i, lambda i, j: jnp.minimum(i, j))
    return _call(
        body, name="fox_bwd_dq", grid=(T // S, nq, nq),
        in_specs=[qs, ks, ks, qs],
        out_specs=pl.BlockSpec((tq, D), lambda b, i, j: (b * nq + i, 0)),
        out_shape=_sds((T, D), MXU),
        scratch=[pltpu.VMEM((FOX_H, tq, HW), F32)],
    )(qb, ka, va, da)


def fox2_dkv(qb, ka, va, da, *, S, tq=256):
    T = qb.shape[0]
    nq = S // tq

    def body(q_ref, k_ref, v_ref, d_ref, dk_ref, dv_ref, dck_ref, dk_acc, dv_acc):
        ki = pl.program_id(1)
        qi = pl.program_id(2)

        @pl.when(qi == 0)
        def _():
            dk_acc[...] = jnp.zeros_like(dk_acc)
            dv_acc[...] = jnp.zeros_like(dv_acc)

        def step(diag):
            for h in range(FOX_H):
                ws = slice(h * HW, (h + 1) * HW)
                qh = q_ref[:, ws]
                dh = d_ref[:, ws]
                pt = jnp.exp(_dot_nt(k_ref[:, ws], qh))
                if diag:
                    r = lax.broadcasted_iota(jnp.int32, pt.shape, 0)
                    c = lax.broadcasted_iota(jnp.int32, pt.shape, 1)
                    pt = jnp.where(r <= c, pt, 0.0)
                dv_acc[h] += _dot(pt.astype(MXU), dh)
                hi, lo = _split(pt * _dot_nt(v_ref[:, ws], dh))
                dk_acc[h] += _dot(hi, qh) + _dot(lo, qh)

        @pl.when(qi > ki)
        def _():
            step(False)

        @pl.when(qi == ki)
        def _():
            step(True)

        @pl.when(qi == nq - 1)
        def _():
            dck_ref[...] = jnp.zeros_like(dck_ref)
            for h in range(FOX_H):
                hs = slice(h * FOX_DH, (h + 1) * FOX_DH)
                dk_ref[:, hs] = dk_acc[h][:, :FOX_DH].astype(MXU)
                dv_ref[:, hs] = dv_acc[h][:, :FOX_DH].astype(MXU)
                dck_ref[:, h:h + 1] = dk_acc[h][:, COL_ONE:COL_ONE + 1]

    qs, ks = _wide_specs(tq, nq, lambda i, j: jnp.maximum(i, j), lambda i, j: i)
    ko = pl.BlockSpec((tq, D), lambda b, i, j: (b * nq + i, 0))
    return _call(
        body, name="fox_bwd_dkv", grid=(T // S, nq, nq),
        in_specs=[qs, ks, ks, qs],
        out_specs=[ko, ko, pl.BlockSpec((tq, 128), lambda b, i, j: (b * nq + i, 0))],
        out_shape=[_sds((T, D), MXU), _sds((T, D), MXU), _sds((T, 128), F32)],
        scratch=[pltpu.VMEM((FOX_H, tq, HW), F32), pltpu.VMEM((FOX_H, tq, HW), F32)],
    )(qb, ka, va, da)


def fox_fin(dck, fl, bf, *, S, tt=256):
    T = fl.shape[0]
    nb = S // tt
    nblk = T // tt

    def body(dck_ref, fl_ref, bf_ref, dfl_ref, dbf_ref, carry):
        i = pl.program_id(0)

        @pl.when(i == 0)
        def _():
            dbf_ref[...] = jnp.zeros_like(dbf_ref)

        @pl.when((i % nb) == 0)
        def _():
            carry[...] = jnp.zeros_like(carry)

        lane = lax.broadcasted_iota(jnp.int32, (tt, 128), 1)
        dc = jnp.where(lane < FOX_H, -dck_ref[...], 0.0)
        dlf = _dot_hi(_tri(tt, upper=True), dc) + carry[...]
        carry[...] = dlf[0:1, :]
        dfl = dlf * _sigmoid(-(fl_ref[...] + bf_ref[...]))
        dfl_ref[...] = dfl.astype(MXU)
        dbf_ref[...] += jnp.sum(dfl, axis=0, keepdims=True)

    rev = pl.BlockSpec((tt, 128), lambda i: (nblk - 1 - i, 0))
    row = pl.BlockSpec((1, 128), lambda i: (0, 0))
    return _call(
        body, name="fox_fin", grid=(nblk,),
        in_specs=[rev, rev, row],
        out_specs=[rev, row],
        out_shape=[_sds((T, 128), MXU), _sds((1, 128), F32)],
        scratch=[pltpu.VMEM((1, 128), F32)],
    )(dck, fl, bf)


def lb_fwd(logits):
    def body(l_ref, lb_ref):
        lv = l_ref[...]
        e = jnp.exp(lv - jnp.max(lv, axis=0, keepdims=True))
        p = e / jnp.sum(e, axis=0, keepdims=True)
        lb_ref[...] = p[1:2, :] + p[2:3, :]

    return _call(body, name="lb_fwd", grid=(1,),
                 in_specs=[pl.BlockSpec((DEPTH, D), lambda i: (0, 0))],
                 out_specs=pl.BlockSpec((1, D), lambda i: (0, 0)),
                 out_shape=_sds((1, D), F32))(logits)


def lb_bwd(logits, dlb):
    def body(l_ref, d_ref, o_ref):
        lv = l_ref[...]
        e = jnp.exp(lv - jnp.max(lv, axis=0, keepdims=True))
        p = e / jnp.sum(e, axis=0, keepdims=True)
        lb = p[1:2, :] + p[2:3, :]
        row = lax.broadcasted_iota(jnp.int32, (DEPTH, D), 0)
        sel = ((row == 1) | (row == 2)).astype(F32)
        o_ref[...] = p * (sel - lb) * d_ref[...]

    return _call(body, name="lb_bwd", grid=(1,),
                 in_specs=[pl.BlockSpec((DEPTH, D), lambda i: (0, 0)), pl.BlockSpec((1, D), lambda i: (0, 0))],
                 out_specs=pl.BlockSpec((DEPTH, D), lambda i: (0, 0)),
                 out_shape=_sds((DEPTH, D), F32))(logits, dlb)


def _hgrn_gates(qr, fr, lb):
    sg = _sigmoid(fr)
    sneg = _sigmoid(-fr)
    f = lb + (1.0 - lb) * sg
    kk = (1.0 - lb) * sneg
    G = _dot_hi(_tri(HG_C), jnp.log(f))
    eG = jnp.exp(G)
    einv = jnp.exp(-G)
    elast = jnp.exp(G[HG_C - 1:HG_C, :] - G)
    q = qr * _sigmoid(qr)
    return dict(q=q, kk=kk, f=f, sg=sg, sneg=sneg, eG=eG, einv=einv, elast=elast,
                qg=q * eG, kinv=kk * einv, khat=kk * elast, glast=jnp.exp(G[HG_C - 1:HG_C, :]))


def _tril_mask(x):
    r = lax.broadcasted_iota(jnp.int32, x.shape, 0)
    c = lax.broadcasted_iota(jnp.int32, x.shape, 1)
    return jnp.where(r >= c, x, 0.0)


def hgrn_fwd(p, lb, ng, *, S, R=256):
    T = p.shape[0]
    R = min(R, S)
    nr = S // R
    ncr = R // HG_C

    def body(q_ref, f_ref, v_ref, gt_ref, lb_ref, ng_ref, y_ref, o_ref, st_ref, st):
        @pl.when(pl.program_id(1) == 0)
        def _():
            st[...] = jnp.zeros_like(st)

        lbv = lb_ref[...]
        for ch in range(ncr):
            rows = slice(ch * HG_C, (ch + 1) * HG_C)
            gt = _hgrn_gates(q_ref[rows, :], f_ref[rows, :], lbv)
            for h in range(HG_H):
                hs = slice(h * HG_DK, (h + 1) * HG_DK)
                sp = st[h]
                st_ref[ch, h] = sp
                qg = gt["qg"][:, hs].astype(MXU)
                vh = v_ref[rows, hs].astype(MXU)
                A = _tril_mask(_dot_nt(qg, gt["kinv"][:, hs].astype(MXU)))
                o_ref[rows, hs] = _dot_nt(qg, sp.astype(MXU)) + _dot(A.astype(MXU), vh)
                st[h] = sp * gt["glast"][:, hs] + _dot_tn(vh, gt["khat"][:, hs].astype(MXU))
        gate = gt_ref[...]
        sgate = gate * _sigmoid(gate)
        for h in range(HG_H):
            hs = slice(h * HG_DK, (h + 1) * HG_DK)
            oh = o_ref[:, hs]
            r = lax.rsqrt(jnp.mean(oh * oh, axis=-1, keepdims=True) + RMS_EPS)
            y_ref[:, hs] = (oh * r * ng_ref[:, hs] * sgate[:, hs]).astype(MXU)

    col = lambda c: pl.BlockSpec((R, D), lambda b, i: (b * nr + i, c))
    row = pl.BlockSpec((1, D), lambda b, i: (0, 0))
    return _call(
        body, name="hgrn_fwd", grid=(T // S, nr),
        in_specs=[col(0), col(1), col(2), col(3), row, row],
        out_specs=[col(0), col(0),
                   pl.BlockSpec((ncr, HG_H, HG_DK, HG_DK), lambda b, i: (b * nr + i, 0, 0, 0))],
        out_shape=[_sds((T, D), MXU), _sds((T, D), F32), _sds((T // HG_C, HG_H, HG_DK, HG_DK), F32)],
        scratch=[pltpu.VMEM((HG_H, HG_DK, HG_DK), F32)],
    )(p, p, p, p, lb, ng)


def hgrn_bwd(p, o, dyo, states, lb, ng, *, S, R=256):
    T = p.shape[0]
    R = min(R, S)
    nr = S // R
    ncr = R // HG_C

    def body(q_ref, f_ref, v_ref, gt_ref, o_ref, dy_ref, st_ref, lb_ref, ng_ref,
             dp_ref, dlb_ref, dng_ref, dst, do_buf, dG_buf, dqb, dkb):
        b = pl.program_id(0)
        i = pl.program_id(1)

        @pl.when(i == 0)
        def _():
            dst[...] = jnp.zeros_like(dst)

        @pl.when((b == 0) & (i == 0))
        def _():
            dlb_ref[...] = jnp.zeros_like(dlb_ref)
            dng_ref[...] = jnp.zeros_like(dng_ref)

        lbv = lb_ref[...]
        gate = gt_ref[...]
        sg_gate = _sigmoid(gate)
        silu_gate = gate * sg_gate
        for h in range(HG_H):
            hs = slice(h * HG_DK, (h + 1) * HG_DK)
            oh = o_ref[:, hs]
            r = lax.rsqrt(jnp.mean(oh * oh, axis=-1, keepdims=True) + RMS_EPS)
            ohat = oh * r
            dyh = dy_ref[:, hs]
            ngh = ng_ref[:, hs]
            dng_ref[:, hs] += jnp.sum(dyh * silu_gate[:, hs] * ohat, axis=0, keepdims=True)
            dp_ref[:, 3 * D + h * HG_DK:3 * D + (h + 1) * HG_DK] = (
                dyh * ohat * ngh * (sg_gate[:, hs] * (1.0 + gate[:, hs] * (1.0 - sg_gate[:, hs])))).astype(MXU)
            dn = dyh * ngh * silu_gate[:, hs]
            do_buf[:, hs] = r * (dn - ohat * jnp.mean(dn * ohat, axis=-1, keepdims=True))

        lastrow = lax.broadcasted_iota(jnp.int32, (HG_C, HG_DK), 0) == HG_C - 1
        for ch in reversed(range(ncr)):
            rows = slice(ch * HG_C, (ch + 1) * HG_C)
            qr = q_ref[rows, :]
            gt = _hgrn_gates(qr, f_ref[rows, :], lbv)
            for h in range(HG_H):
                hs = slice(h * HG_DK, (h + 1) * HG_DK)
                sp = st_ref[ch, h]
                ds = dst[h]
                qg32, kinv32, khat32 = gt["qg"][:, hs], gt["kinv"][:, hs], gt["khat"][:, hs]
                qg, kinv, khat = qg32.astype(MXU), kinv32.astype(MXU), khat32.astype(MXU)
                vh = v_ref[rows, hs].astype(MXU)
                doh = do_buf[rows, hs].astype(MXU)
                dsb = ds.astype(MXU)
                A = _tril_mask(_dot_nt(qg, kinv)).astype(MXU)
                dA = _tril_mask(_dot_nt(doh, vh)).astype(MXU)
                dqg = _dot(doh, sp.astype(MXU)) + _dot(dA, kinv)
                dkinv = _dot_tn(dA, qg)
                dp_ref[rows, 2 * D + h * HG_DK:2 * D + (h + 1) * HG_DK] = (
                    _dot_tn(A, doh) + _dot_nt(khat, dsb)).astype(MXU)
                dkhat = _dot(vh, dsb)
                glast = gt["glast"][:, hs]
                qg32, kinv32, khat32 = qg.astype(F32), kinv.astype(F32), khat.astype(F32)
                extra = (glast * jnp.sum(dsb.astype(F32) * sp.astype(MXU).astype(F32), axis=0, keepdims=True)
                         + jnp.sum(dkhat * khat32, axis=0, keepdims=True))
                dst[h] = ds * glast + _dot_tn(doh, qg)
                dG = dqg * qg32 - dkinv * kinv32 - dkhat * khat32
                dG_buf[:, hs] = dG + jnp.where(lastrow, extra, 0.0)
                dqb[:, hs] = dqg * gt["eG"][:, hs]
                dkb[:, hs] = dkinv * gt["einv"][:, hs] + dkhat * gt["elast"][:, hs]
            dg = _dot_hi(_tri(HG_C, upper=True), dG_buf[...])
            dk = dkb[...]
            sneg, f = gt["sneg"], gt["f"]
            c1 = (1.0 - lbv) * gt["sg"] * sneg
            dp_ref[rows, D:2 * D] = (dg * c1 / f - dk * c1).astype(MXU)
            dlb_ref[...] += jnp.sum(dg * sneg / f - dk * sneg, axis=0, keepdims=True)
            sq = _sigmoid(qr)
            dp_ref[rows, 0:D] = (dqb[...] * (sq * (1.0 + qr * (1.0 - sq)))).astype(MXU)

    rev = lambda b, i: b * nr + nr - 1 - i
    col = lambda c: pl.BlockSpec((R, D), lambda b, i: (rev(b, i), c))
    row = pl.BlockSpec((1, D), lambda b, i: (0, 0))
    return _call(
        body, name="hgrn_bwd", grid=(T // S, nr),
        in_specs=[col(0), col(1), col(2), col(3), col(0), col(0),
                  pl.BlockSpec((ncr, HG_H, HG_DK, HG_DK), lambda b, i: (rev(b, i), 0, 0, 0)), row, row],
        out_specs=[pl.BlockSpec((R, 4 * D), lambda b, i: (rev(b, i), 0)), row, row],
        out_shape=[_sds((T, 4 * D), MXU), _sds((1, D), F32), _sds((1, D), F32)],
        scratch=[pltpu.VMEM((HG_H, HG_DK, HG_DK), F32), pltpu.VMEM((R, D), F32), pltpu.VMEM((HG_C, D), F32),
                 pltpu.VMEM((HG_C, D), F32), pltpu.VMEM((HG_C, D), F32)],
    )(p, p, p, p, o, dyo, states, lb, ng)


def _mm_tn_stack(a, b, *, name, G, M, N, tk, stack):
    E, e, buf = stack
    T = a.shape[-2]

    def spec(arr, width):
        if arr.ndim == 3:
            return pl.BlockSpec((None, tk, width), lambda g, k: (g, k, 0))
        return pl.BlockSpec((tk, width), lambda g, k: (k, 0))

    def body(*refs):
        a_ref, b_ref, o_ref = refs[0], refs[1], refs[-1]

        @pl.when(pl.program_id(1) == 0)
        def _():
            o_ref[...] = jnp.zeros_like(o_ref)

        o_ref[...] += _dot_tn(a_ref[...], b_ref[...])

    in_specs = [spec(a, M), spec(b, N)]
    args = [a, b]
    aliases = {}
    if buf is not None:
        in_specs.append(pl.BlockSpec(memory_space=pl.ANY))
        args.append(buf)
        aliases = {2: 0}
    return pl.pallas_call(
        body, name=name, grid=(G, T // tk), in_specs=in_specs,
        out_specs=pl.BlockSpec((None, None, M, N), lambda g, k: (g, e, 0, 0)),
        out_shape=_sds((G, E, M, N), F32), input_output_aliases=aliases,
        compiler_params=pltpu.CompilerParams(dimension_semantics=("arbitrary", "arbitrary"),
                                             vmem_limit_bytes=VMEM_LIMIT))(*args)


MESH = pl.DeviceIdType.MESH
ANY = pl.BlockSpec(memory_space=pl.ANY)


def _pos():
    return lax.axis_index("x"), lax.axis_index("y"), lax.axis_index("c")


def _other_chips(x, y):
    return [(1 - x, y), (x, 1 - y), (1 - x, 1 - y)]


def _comm_call(body, *, name, args, out_shape, n_sem):
    return pl.pallas_call(
        body, name=name, in_specs=[ANY] * len(args), out_specs=[ANY] * len(out_shape), out_shape=out_shape,
        scratch_shapes=[pltpu.SemaphoreType.DMA((n_sem,)), pltpu.SemaphoreType.DMA((n_sem,)),
                        pltpu.SemaphoreType.DMA((len(args),))],
    )(*args)


def all_gather_chips(xs):
    n = len(xs)

    def body(*refs):
        x_refs, o_refs = refs[:n], refs[n:2 * n]
        ssem, rsem, lsem = refs[2 * n:]
        x, y, c = _pos()
        me = 2 * x + y
        chips = _other_chips(x, y)
        sib = (x, y, 1 - c)

        def rc(src, dst, idx, dev):
            return pltpu.make_async_remote_copy(src_ref=src, dst_ref=dst, send_sem=ssem.at[idx], recv_sem=rsem.at[idx],
                                                device_id=dev, device_id_type=MESH)

        via = jnp.where(c == 0, 2 * (1 - x) + y, 2 * x + (1 - y))
        to = (jnp.where(c == 0, x, 1 - x), jnp.where(c == 0, 1 - y, y), c)
        local, started = [], []
        for t in range(n):
            hr = xs[t].shape[0] // 2
            mine = pl.ds(c * hr, hr)
            cp = pltpu.make_async_copy(x_refs[t], o_refs[t].at[me], lsem.at[t])
            cp.start()
            local.append(cp)
            for k, (cx, cy) in enumerate(chips[:2]):
                cp = rc(x_refs[t].at[mine], o_refs[t].at[me, mine], 6 * t + k, (cx, cy, c))
                cp.start()
                started.append(cp)
        for t in range(n):
            hr = xs[t].shape[0] // 2
            mine = pl.ds(c * hr, hr)
            for k, (cx, cy) in enumerate(chips[:2]):
                landed = o_refs[t].at[2 * cx + cy, mine]
                rc(landed, landed, 6 * t + k, (cx, cy, c)).wait_recv()
            passed = o_refs[t].at[via, mine]
            cp = rc(passed, passed, 6 * t + 2, to)
            cp.start()
            started.append(cp)
            for k, (cx, cy) in enumerate(chips[:2]):
                landed = o_refs[t].at[2 * cx + cy, mine]
                cp = rc(landed, landed, 6 * t + 3 + k, sib)
                cp.start()
                started.append(cp)
        for t in range(n):
            hr = xs[t].shape[0] // 2
            mine = pl.ds(c * hr, hr)
            cx, cy = chips[2]
            landed = o_refs[t].at[2 * cx + cy, mine]
            rc(landed, landed, 6 * t + 2, to).wait_recv()
            cp = rc(landed, landed, 6 * t + 5, sib)
            cp.start()
            started.append(cp)
        for t in range(n):
            hr = xs[t].shape[0] // 2
            theirs = pl.ds((1 - c) * hr, hr)
            for k, (cx, cy) in enumerate(chips):
                other = o_refs[t].at[2 * cx + cy, theirs]
                rc(other, other, 6 * t + 3 + k, sib).wait_recv()
        for cp in started:
            cp.wait_send()
        for cp in local:
            cp.wait()

    outs = _comm_call(body, name="all_gather_chips", args=list(xs),
                      out_shape=[_sds((NSH,) + a.shape, a.dtype) for a in xs], n_sem=6 * n)
    return list(outs)


def sibling_half_exchange(gs):
    n = len(gs)

    def body(*refs):
        g_refs, o_refs = refs[:n], refs[n:2 * n]
        ssem, rsem, _ = refs[2 * n:]
        x, y, c = _pos()
        cps = []
        for t in range(n):
            hr = gs[t].shape[1] // 2
            for j in range(NSH):
                cp = pltpu.make_async_remote_copy(
                    src_ref=g_refs[t].at[j, pl.ds((1 - c) * hr, hr)], dst_ref=o_refs[t].at[j],
                    send_sem=ssem.at[NSH * t + j], recv_sem=rsem.at[NSH * t + j],
                    device_id=(x, y, 1 - c), device_id_type=MESH)
                cp.start()
                cps.append(cp)
        for cp in cps:
            cp.wait()

    outs = _comm_call(body, name="sibling_half_exchange", args=list(gs),
                      out_shape=[_sds((NSH, g.shape[1] // 2, g.shape[2]), g.dtype) for g in gs], n_sem=NSH * n)
    return list(outs)


def chip_scatter(ss):
    n = len(ss)

    def body(*refs):
        s_refs, o_refs = refs[:n], refs[n:2 * n]
        ssem, rsem, _ = refs[2 * n:]
        x, y, c = _pos()
        cps = []
        for t in range(n):
            for k, (cx, cy) in enumerate(_other_chips(x, y)):
                cp = pltpu.make_async_remote_copy(
                    src_ref=s_refs[t].at[2 * cx + cy], dst_ref=o_refs[t].at[k],
                    send_sem=ssem.at[3 * t + k], recv_sem=rsem.at[3 * t + k],
                    device_id=(cx, cy, c), device_id_type=MESH)
                cp.start()
                cps.append(cp)
        for cp in cps:
            cp.wait()

    outs = _comm_call(body, name="chip_scatter", args=list(ss),
                      out_shape=[_sds((3,) + s.shape[1:], s.dtype) for s in ss], n_sem=3 * n)
    return list(outs)


def sibling_exchange(rs):
    n = len(rs)

    def body(*refs):
        r_refs, o_refs = refs[:n], refs[n:2 * n]
        ssem, rsem, _ = refs[2 * n:]
        x, y, c = _pos()
        cps = []
        for t in range(n):
            cp = pltpu.make_async_remote_copy(
                src_ref=r_refs[t], dst_ref=o_refs[t], send_sem=ssem.at[t], recv_sem=rsem.at[t],
                device_id=(x, y, 1 - c), device_id_type=MESH)
            cp.start()
            cps.append(cp)
        for cp in cps:
            cp.wait()

    outs = _comm_call(body, name="sibling_exchange", args=list(rs),
                      out_shape=[_sds(r.shape, r.dtype) for r in rs], n_sem=n)
    return list(outs)


def all_gather_devices(v):
    def body(v_ref, o_ref, ssem, rsem, lsem):
        x, y, c = _pos()
        me = 4 * x + 2 * y + c
        loc = pltpu.make_async_copy(v_ref, o_ref.at[me], lsem.at[0])
        loc.start()
        cps = []
        k = 0
        for fx in range(2):
            for fy in range(2):
                for fc in range(2):
                    if fx == fy == fc == 0:
                        continue
                    cp = pltpu.make_async_remote_copy(
                        src_ref=v_ref, dst_ref=o_ref.at[me], send_sem=ssem.at[k], recv_sem=rsem.at[k],
                        device_id=(x ^ fx, y ^ fy, c ^ fc), device_id_type=MESH)
                    cp.start()
                    src = 4 * (x ^ fx) + 2 * (y ^ fy) + (c ^ fc)
                    cps.append((cp, o_ref.at[src], k))
                    k += 1
        for cp, landed, k in cps:
            cp.wait_send()
            pltpu.make_async_remote_copy(
                src_ref=landed, dst_ref=landed, send_sem=ssem.at[k], recv_sem=rsem.at[k],
                device_id=(x, y, c), device_id_type=MESH).wait_recv()
        loc.wait()

    return _comm_call(body, name="all_gather_devices", args=[v],
                      out_shape=[_sds((8,) + v.shape, v.dtype)], n_sem=7)[0]


def _call_sp(body, *, name, grid, in_specs, out_specs, out_shape, pos, args):
    return pl.pallas_call(
        body, name=name,
        grid_spec=pltpu.PrefetchScalarGridSpec(num_scalar_prefetch=1, grid=grid, in_specs=in_specs,
                                               out_specs=out_specs),
        out_shape=out_shape,
        compiler_params=pltpu.CompilerParams(dimension_semantics=("arbitrary",) * len(grid),
                                             vmem_limit_bytes=VMEM_LIMIT))(pos, *args)


def _rows_tile(r):
    for t in (512, 256, 128, 64, 32, 16, 8):
        if r % t == 0:
            return t
    raise ValueError(r)


def pair_sum(g, r, pos):
    _, R, C = g.shape
    hr = R // 2
    tr = _rows_tile(hr)
    nbh = hr // tr

    def body(p_ref, g_ref, r_ref, o_ref):
        o_ref[...] = (g_ref[...] + r_ref[...]).astype(MXU)

    return _call_sp(
        body, name="pair_sum", grid=(NSH, nbh), pos=pos, args=[g, r],
        in_specs=[pl.BlockSpec((None, tr, C), lambda j, i, p: (j, p[0] * nbh + i, 0)),
                  pl.BlockSpec((None, tr, C), lambda j, i, p: (j, i, 0))],
        out_specs=pl.BlockSpec((None, tr, C), lambda j, i, p: (j, i, 0)),
        out_shape=_sds((NSH, hr, C), MXU))


def reduce_own(g, r_sib, r_ici, pos):
    _, R, C = g.shape
    hr = R // 2
    tr = _rows_tile(hr)
    nbh = hr // tr

    def body(p_ref, g_ref, rs_ref, ri_ref, o_ref):
        s = g_ref[...] + rs_ref[...]
        for k in range(3):
            s = s + ri_ref[k].astype(F32)
        o_ref[...] = s

    return _call_sp(
        body, name="reduce_own", grid=(nbh,), pos=pos, args=[g, r_sib, r_ici],
        in_specs=[pl.BlockSpec((None, tr, C), lambda i, p: (p[1], p[0] * nbh + i, 0)),
                  pl.BlockSpec((None, tr, C), lambda i, p: (p[1], i, 0)),
                  pl.BlockSpec((3, tr, C), lambda i, p: (0, i, 0))],
        out_specs=pl.BlockSpec((tr, C), lambda i, p: (i, 0)),
        out_shape=_sds((hr, C), F32))


def _adamw_math(w, g, m, v):
    m = ADAM_B1 * m + (1.0 - ADAM_B1) * g
    v = ADAM_B2 * v + (1.0 - ADAM_B2) * (g * g)
    m_hat = m / (1.0 - ADAM_B1 ** ADAM_STEP)
    v_hat = v / (1.0 - ADAM_B2 ** ADAM_STEP)
    delta = -ADAM_LR * (m_hat / (jnp.sqrt(v_hat) + ADAM_EPS) + ADAM_WD * w)
    return delta, m, v


def adamw_halves(w, m, v, ga, gb, pos):
    R, C = w.shape
    hr = R // 2
    tr = _rows_tile(hr)
    nbh = hr // tr

    def body(p_ref, w_ref, m_ref, v_ref, ga_ref, gb_ref, g_ref, d_ref, mo_ref, vo_ref):
        mine = (pl.program_id(0) // nbh) == p_ref[0]
        g = jnp.where(mine, ga_ref[...], gb_ref[...])
        g_ref[...] = g
        d_ref[...], mo_ref[...], vo_ref[...] = _adamw_math(w_ref[...], g, m_ref[...], v_ref[...])

    blk = pl.BlockSpec((tr, C), lambda i, p: (i, 0))
    return _call_sp(
        body, name="adamw_halves", grid=(R // tr,), pos=pos, args=[w, m, v, ga, gb],
        in_specs=[blk, blk, blk,
                  pl.BlockSpec((tr, C), lambda i, p: (jnp.where(i // nbh == p[0], i % nbh, 0), 0)),
                  pl.BlockSpec((tr, C), lambda i, p: (jnp.where(i // nbh == p[0], 0, i % nbh), 0))],
        out_specs=[blk, blk, blk, blk],
        out_shape=[_sds((R, C), F32)] * 4)


def adamw_sum(gall, w, m, v):
    n, R, C = gall.shape

    def body(ga_ref, w_ref, m_ref, v_ref, g_ref, d_ref, mo_ref, vo_ref):
        g = ga_ref[0]
        for k in range(1, n):
            g = g + ga_ref[k]
        g_ref[...] = g
        d_ref[...], mo_ref[...], vo_ref[...] = _adamw_math(w_ref[...], g, m_ref[...], v_ref[...])

    blk = pl.BlockSpec((R, C), lambda i: (0, 0))
    return _call(body, name="adamw_sum", grid=(1,),
                 in_specs=[pl.BlockSpec((n, R, C), lambda i: (0, 0, 0)), blk, blk, blk],
                 out_specs=[blk, blk, blk, blk], out_shape=[_sds((R, C), F32)] * 4)(gall, w, m, v)


_WEIGHTS = ['ffn_norm', 'ffn_w_gate', 'ffn_w_up', 'ffn_w_down', 'mix_norm', 'final_norm', 'conv_w_in', 'conv_b_in',
            'conv_dw', 'conv_dw_b', 'conv_ln_g', 'conv_ln_b', 'conv_w_out', 'fox_w_in', 'fox_b_f', 'fox_w_out',
            'hgrn_w_in', 'hgrn_lb_logits', 'hgrn_norm', 'hgrn_w_out', 'pool_w', 'pool_scale']
_BIG = ['ffn_w_gate', 'ffn_w_up', 'ffn_w_down', 'conv_w_in', 'conv_w_out', 'fox_w_in', 'fox_w_out',
        'hgrn_w_in', 'hgrn_w_out', 'pool_w']
_SHARDED_SMALL = ['ffn_norm', 'conv_dw', 'hgrn_norm', 'pool_scale']
_REPLICATED = ['mix_norm', 'final_norm', 'conv_b_in', 'conv_dw_b', 'conv_ln_g', 'conv_ln_b', 'fox_b_f', 'hgrn_lb_logits']
FOX_N = 3 * D + FOX_H
FOX_NP = 3200
QS = D // NSH


def _pad_rows(a, rows):
    return jnp.pad(a, ((0, rows - a.shape[0]), (0, 0)))


def _pack_sharded_small(get):
    return jnp.concatenate([get('ffn_norm').reshape(8, -1), _pad_rows(get('conv_dw')[0], 32),
                            get('hgrn_norm'), get('pool_scale'), jnp.zeros((6, get('pool_scale').shape[1]), F32)], axis=0)


def _pack_replicated(get):
    return jnp.concatenate([get('mix_norm'), get('final_norm').reshape(1, D), get('conv_b_in').reshape(2, D),
                            get('conv_dw_b'), get('conv_ln_g'), get('conv_ln_b'),
                            jnp.pad(get('fox_b_f'), ((0, 0), (0, D - FOX_H))), get('hgrn_lb_logits'),
                            jnp.zeros((9, D), F32)], axis=0)


def _unpack_replicated(p):
    return {'mix_norm': p[0:4], 'final_norm': p[4], 'conv_b_in': p[5:7].reshape(1, 2 * D), 'conv_dw_b': p[7:8],
            'conv_ln_g': p[8:9], 'conv_ln_b': p[9:10], 'fox_b_f': p[10:11, :FOX_H], 'hgrn_lb_logits': p[11:15]}


def _unpack_sharded_small(p):
    return {'ffn_norm': p[0:8].reshape(DEPTH, 2, -1), 'conv_dw': p[8:8 + CONV_W][None],
            'hgrn_norm': p[40:41], 'pool_scale': p[41:42]}


def kernel(x, ffn_norm, ffn_w_gate, ffn_w_up, ffn_w_down, mix_norm, final_norm, conv_w_in, conv_b_in, conv_dw, conv_dw_b, conv_ln_g, conv_ln_b, conv_w_out, fox_w_in, fox_b_f, fox_w_out, hgrn_w_in, hgrn_lb_logits, hgrn_norm, hgrn_w_out, pool_w, pool_scale, loss_target, m_ffn_norm, m_ffn_w_gate, m_ffn_w_up, m_ffn_w_down, m_mix_norm, m_final_norm, m_conv_w_in, m_conv_b_in, m_conv_dw, m_conv_dw_b, m_conv_ln_g, m_conv_ln_b, m_conv_w_out, m_fox_w_in, m_fox_b_f, m_fox_w_out, m_hgrn_w_in, m_hgrn_lb_logits, m_hgrn_norm, m_hgrn_w_out, m_pool_w, m_pool_scale, v_ffn_norm, v_ffn_w_gate, v_ffn_w_up, v_ffn_w_down, v_mix_norm, v_final_norm, v_conv_w_in, v_conv_b_in, v_conv_dw, v_conv_dw_b, v_conv_ln_g, v_conv_ln_b, v_conv_w_out, v_fox_w_in, v_fox_b_f, v_fox_w_out, v_hgrn_w_in, v_hgrn_lb_logits, v_hgrn_norm, v_hgrn_w_out, v_pool_w, v_pool_scale):
    W = dict(ffn_norm=ffn_norm, ffn_w_gate=ffn_w_gate, ffn_w_up=ffn_w_up, ffn_w_down=ffn_w_down, mix_norm=mix_norm, final_norm=final_norm, conv_w_in=conv_w_in, conv_b_in=conv_b_in, conv_dw=conv_dw, conv_dw_b=conv_dw_b, conv_ln_g=conv_ln_g, conv_ln_b=conv_ln_b, conv_w_out=conv_w_out, fox_w_in=fox_w_in, fox_b_f=fox_b_f, fox_w_out=fox_w_out, hgrn_w_in=hgrn_w_in, hgrn_lb_logits=hgrn_lb_logits, hgrn_norm=hgrn_norm, hgrn_w_out=hgrn_w_out, pool_w=pool_w, pool_scale=pool_scale)
    M = dict(ffn_norm=m_ffn_norm, ffn_w_gate=m_ffn_w_gate, ffn_w_up=m_ffn_w_up, ffn_w_down=m_ffn_w_down, mix_norm=m_mix_norm, final_norm=m_final_norm, conv_w_in=m_conv_w_in, conv_b_in=m_conv_b_in, conv_dw=m_conv_dw, conv_dw_b=m_conv_dw_b, conv_ln_g=m_conv_ln_g, conv_ln_b=m_conv_ln_b, conv_w_out=m_conv_w_out, fox_w_in=m_fox_w_in, fox_b_f=m_fox_b_f, fox_w_out=m_fox_w_out, hgrn_w_in=m_hgrn_w_in, hgrn_lb_logits=m_hgrn_lb_logits, hgrn_norm=m_hgrn_norm, hgrn_w_out=m_hgrn_w_out, pool_w=m_pool_w, pool_scale=m_pool_scale)
    V = dict(ffn_norm=v_ffn_norm, ffn_w_gate=v_ffn_w_gate, ffn_w_up=v_ffn_w_up, ffn_w_down=v_ffn_w_down, mix_norm=v_mix_norm, final_norm=v_final_norm, conv_w_in=v_conv_w_in, conv_b_in=v_conv_b_in, conv_dw=v_conv_dw, conv_dw_b=v_conv_dw_b, conv_ln_g=v_conv_ln_g, conv_ln_b=v_conv_ln_b, conv_w_out=v_conv_w_out, fox_w_in=v_fox_w_in, fox_b_f=v_fox_b_f, fox_w_out=v_fox_w_out, hgrn_w_in=v_hgrn_w_in, hgrn_lb_logits=v_hgrn_lb_logits, hgrn_norm=v_hgrn_norm, hgrn_w_out=v_hgrn_w_out, pool_w=v_pool_w, pool_scale=v_pool_scale)

    px, py, pc = _pos()
    jme = 2 * px + py
    pos = jnp.stack([pc, jme]).astype(jnp.int32)
    S = x.shape[1]
    T = x.shape[0] * S
    x2 = x.reshape(T, D)
    tgt = loss_target.reshape(T, D)

    flat = lambda a: a.reshape(-1, a.shape[-1])
    gathered = all_gather_chips([flat(W[n]).astype(MXU) for n in _BIG] + [_pack_sharded_small(W.get)])
    G = dict(zip(_BIG, gathered[:-1]))
    small = gathered[-1].transpose(1, 0, 2).reshape(48, D)
    ffn_norm_f, conv_dw_f = small[0:8], small[8:40]
    hgrn_norm_f, pool_scale_f = small[40:41], small[41:42]
    wg_all = G['ffn_w_gate'].reshape(NSH, 2 * DEPTH, D, FS)
    wu_all = G['ffn_w_up'].reshape(NSH, 2 * DEPTH, D, FS)
    wd_all = G['ffn_w_down'].reshape(NSH, 2 * DEPTH, FS, D)
    conv_wi = G['conv_w_in']
    conv_wo = G['conv_w_out'].reshape(D, D)
    fox_full = jnp.pad(G['fox_w_in'].transpose(1, 0, 2).reshape(D, FOX_N), ((0, 0), (0, FOX_NP - FOX_N)))
    fox_w5 = fox_full.reshape(D, 5, FOX_NP // 5).transpose(1, 0, 2)
    fox_wf = fox_full[:, 3 * D:][None]
    fox_bf = jnp.pad(fox_b_f, ((0, 0), (0, 128 - FOX_H)))
    fox_wo = G['fox_w_out'].reshape(D, D)
    hgrn_wi = G['hgrn_w_in']
    hgrn_wo = G['hgrn_w_out'].reshape(D, D)
    pool_wf = G['pool_w'].reshape(NSH, 4, 64, POOL_G).transpose(1, 0, 2, 3).reshape(4, POOL_G, POOL_G)
    conv_bi = conv_b_in.reshape(NSH, 1, 2 * D // NSH)

    def ffn_f(xs, e):
        xo, h, u, sa, z = ffn_fwd(xs, ffn_norm_f[e:e + 1], wg_all, wu_all, wd_all, e, tm=min(1024, xs.shape[0]))
        return xo, (xs, h, u, sa, z)

    saved = []
    xs = x2
    lb = lb_fwd(hgrn_lb_logits)
    for i in range(DEPTH):
        xs, r0 = ffn_f(xs, 2 * i)
        gm = mix_norm[i:i + 1]
        xin = xs
        if i == 0:
            p, h = norm_mm(xin, gm, conv_wi, conv_bi, name="conv_in", out_dtype=F32)
            u2, u4 = conv_fwd_core(p, conv_dw_f, conv_dw_b, conv_ln_g, conv_ln_b, S=S)
            xs = mm_res(u4, conv_wo, xin, name="conv_out")
            rm = (xin, p, h, u2, u4)
        elif i == 1:
            p, h = norm_mm(xin, gm, fox_w5, None, name="fox_in", out_dtype=MXU)
            fl, _ = norm_mm(xin, gm, fox_wf, None, name="fox_in_f", out_dtype=F32)
            cq = fox_cum(fl, fox_bf, S=S)
            qa, ka, va = fox_prep(p, cq)
            o, o32, lse = fox2_fwd(qa, ka, va, S=S, tq=min(FOX_T, S), tk=min(FOX_T, S))
            xs = mm_res(o, fox_wo, xin, name="fox_out")
            rm = (xin, h, fl, qa, ka, va, o, o32, lse)
        elif i == 2:
            p, h = norm_mm(xin, gm, hgrn_wi, None, name="hgrn_in", out_dtype=F32)
            yh, oh, st = hgrn_fwd(p, lb, hgrn_norm_f, S=S)
            xs = mm_res(yh, hgrn_wo, xin, name="hgrn_out")
            rm = (xin, p, h, yh, oh, st)
        else:
            xs, mp = pool_fwd(xin, gm, pool_wf, pool_scale_f, S=S)
            rm = (xin, mp)
        xs, r1 = ffn_f(xs, 2 * i + 1)
        saved.append((r0, rm, r1))

    loss8, dx, d_final = loss_head(xs, final_norm.reshape(1, D), tgt)

    conv_wit, fox_w5t, hgrn_wit = (w.transpose(0, 2, 1) for w in (conv_wi, fox_w5, hgrn_wi))
    conv_wot, fox_wot, hgrn_wot = conv_wo.T, fox_wo.T, hgrn_wo.T
    gb = {'g': None, 'u': None, 'd': None}
    d_ffn_norm = [None] * (2 * DEPTH)
    d_mix_norm = [None] * DEPTH
    gbig = {}
    gsm = {}

    def ffn_b(dy, res, e):
        xin, h, u, sa, z = res
        dxo, da, db, dyh, dg = ffn_bwd_dx(xin, ffn_norm_f[e:e + 1], dy, u, sa, wg_all, wu_all, wd_all, e)
        tk = min(4096, xin.shape[0])
        gb['g'] = mm_tn(h, da, name="ffn_dwg", G=NSH, M=D, N=FS, tk=tk, stack=(2 * DEPTH, e, gb['g']))
        gb['u'] = mm_tn(h, db, name="ffn_dwu", G=NSH, M=D, N=FS, tk=tk, stack=(2 * DEPTH, e, gb['u']))
        gb['d'] = mm_tn(z, dyh, name="ffn_dwd", G=NSH, M=FS, N=D, tk=tk, stack=(2 * DEPTH, e, gb['d']))
        d_ffn_norm[e] = dg
        return dxo

    for i in reversed(range(DEPTH)):
        r0, rm, r1 = saved[i]
        dx = ffn_b(dx, r1, 2 * i + 1)
        gm = mix_norm[i:i + 1]
        if i == 0:
            xin, p, h, u2, u4 = rm
            du2, dyb, gsm['conv_ln_g'], gsm['conv_ln_b'], gsm['conv_dw_b'] = conv_bwd_rows(dx, conv_wot, u2, conv_ln_g, conv_ln_b)
            dp, gsm['conv_b_in'], ddw = conv_bwd_core(du2, p, conv_dw_f, S=S)
            gsm['conv_dw'] = ddw
            gbig['conv_w_in'] = mm_tn(h, dp, name="conv_dwin", G=NSH, M=D, N=2 * D // NSH, b_step=1)
            gbig['conv_w_out'] = mm_tn(u4, dyb, name="conv_dwout", G=NSH, M=QS, N=D, a_step=1)
            dx, d_mix_norm[i] = inproj_bwd(dp, conv_wit, xin, gm, dx, name="conv_in_bwd")
        elif i == 1:
            xin, h, fl, qa, ka, va, o, o32, lse = rm
            do, dyb = mm_nt(dx, fox_wot, name="fox_out_bwd")
            qb, da = fox2_prep_bwd(do, o32, lse, qa)
            dq = fox2_dq(qb, ka, va, da, S=S, tq=min(FOX_T, S))
            dk, dv, dck = fox2_dkv(qb, ka, va, da, S=S, tq=min(FOX_T, S))
            dfl, dbf = fox_fin(dck, fl, fox_bf, S=S)
            gsm['fox_b_f'] = dbf
            dp = jnp.concatenate([dq, dk, dv, dfl], axis=1)
            dw5 = mm_tn(h, dp, name="fox_dwin", G=5, M=D, N=FOX_NP // 5, b_step=1)
            dwf = dw5.transpose(1, 0, 2).reshape(D, FOX_NP)[:, :FOX_N]
            gbig['fox_w_in'] = dwf.reshape(D, NSH, FOX_N // NSH).transpose(1, 0, 2)
            gbig['fox_w_out'] = mm_tn(o, dyb, name="fox_dwout", G=NSH, M=QS, N=D, a_step=1)
            dx, d_mix_norm[i] = inproj_bwd(dp, fox_w5t, xin, gm, dx, name="fox_in_bwd")
        elif i == 2:
            xin, p, h, yh, oh, st = rm
            dyo, dyb = mm_nt(dx, hgrn_wot, name="hgrn_out_bwd")
            dp, dlb, gsm['hgrn_norm'] = hgrn_bwd(p, oh, dyo, st, lb, hgrn_norm_f, S=S)
            gsm['hgrn_lb_logits'] = lb_bwd(hgrn_lb_logits, dlb)
            gbig['hgrn_w_in'] = mm_tn(h, dp, name="hgrn_dwin", G=NSH, M=D, N=D, b_step=1)
            gbig['hgrn_w_out'] = mm_tn(yh, dyb, name="hgrn_dwout", G=NSH, M=QS, N=D, a_step=1)
            dx, d_mix_norm[i] = inproj_bwd(dp, hgrn_wit, xin, gm, dx, name="hgrn_in_bwd")
        else:
            xin, mp = rm
            dmc, dyp, gsm['pool_scale'] = pool_bwd_rows(dx, mp, pool_wf, pool_scale_f, S=S)
            dwp = mm_tn(mp, dyp, name="pool_dw", G=4, M=POOL_G, N=POOL_G, a_step=1, b_step=1)
            gbig['pool_w'] = dwp.reshape(4, NSH, 64, POOL_G).transpose(1, 0, 2, 3).reshape(NSH, 4 * 64, POOL_G)
            dx, d_mix_norm[i] = pool_bwd_core(dmc, xin, gm, dx, S=S)
        dx = ffn_b(dx, r0, 2 * i)

    gbig['ffn_w_gate'] = gb['g'].reshape(NSH, 2 * DEPTH * D, FS)
    gbig['ffn_w_up'] = gb['u'].reshape(NSH, 2 * DEPTH * D, FS)
    gbig['ffn_w_down'] = gb['d'].reshape(NSH, 2 * DEPTH * FS, D)

    gl = [gbig[n] for n in _BIG]
    r_sib = sibling_half_exchange(gl)
    s16 = [pair_sum(g, r, pos) for g, r in zip(gl, r_sib)]
    r_ici = chip_scatter(s16)
    red = [reduce_own(g, rs, ri, pos) for g, rs, ri in zip(gl, r_sib, r_ici)]
    oth = sibling_exchange(red)
    out = {}
    for n, ga, gb_ in zip(_BIG, red, oth):
        res = adamw_halves(flat(W[n]), flat(M[n]), flat(V[n]), ga, gb_, pos)
        out[n] = [r.reshape(W[n].shape) for r in res]

    gfull = {'mix_norm': jnp.concatenate(d_mix_norm, axis=0), 'final_norm': d_final,
             'conv_b_in': gsm['conv_b_in'], 'conv_dw_b': gsm['conv_dw_b'], 'conv_ln_g': gsm['conv_ln_g'],
             'conv_ln_b': gsm['conv_ln_b'], 'fox_b_f': gsm['fox_b_f'][:, :FOX_H], 'hgrn_lb_logits': gsm['hgrn_lb_logits'],
             'ffn_norm': jnp.concatenate(d_ffn_norm, axis=0), 'conv_dw': gsm['conv_dw'][None, :CONV_W],
             'hgrn_norm': gsm['hgrn_norm'], 'pool_scale': gsm['pool_scale']}
    gpack = jnp.concatenate([_pack_replicated(gfull.get), _pack_sharded_small(gfull.get)], axis=0)
    gall = all_gather_devices(gpack)
    rep = adamw_sum(gall[:, :24], _pack_replicated(W.get), _pack_replicated(M.get), _pack_replicated(V.get))
    rep = [_unpack_replicated(r) for r in rep]
    for n in _REPLICATED:
        out[n] = [r[n].reshape(W[n].shape) for r in rep]
    gsh = lax.dynamic_slice_in_dim(gall[:, 24:], jme * QS, QS, axis=2)
    shd = adamw_sum(gsh, _pack_sharded_small(W.get), _pack_sharded_small(M.get), _pack_sharded_small(V.get))
    shd = [_unpack_sharded_small(r) for r in shd]
    for n in _SHARDED_SMALL:
        out[n] = [r[n].reshape(W[n].shape) for r in shd]

    loss = lax.psum(loss8[0, 0], ("x", "y", "c"))
    res = [loss, dx.reshape(x.shape)]
    for k in range(4):
        res += [out[n][k] for n in _WEIGHTS]
    return tuple(res)
```

```python
import jax
import jax.numpy as jnp
from jax import lax
from jax.experimental import pallas as pl
from jax.experimental.pallas import tpu as pltpu

D = 1024
F = 2816
NSH = 4
FS = F // NSH
DEPTH = 4
RMS_EPS = 1e-6
LN_EPS = 1e-5
CONV_W = 31
HALO = 32
FOX_H = 16
FOX_DH = 64
HG_H = 8
HG_DK = 128
HG_C = 32
POOL_WIN = (2, 4, 8, 16)
POOL_G = 256
MXU = jnp.bfloat16
F32 = jnp.float32
VMEM_LIMIT = 52 * 1024 * 1024

ADAM_LR = 0.001
ADAM_B1 = 0.9
ADAM_B2 = 0.999
ADAM_EPS = 1e-08
ADAM_WD = 0.01
ADAM_STEP = 10


def _call(body, *, name, grid, in_specs, out_specs, out_shape, scratch=(), vmem=VMEM_LIMIT):
    return pl.pallas_call(
        body, name=name, grid=grid, in_specs=in_specs, out_specs=out_specs, out_shape=out_shape,
        scratch_shapes=list(scratch),
        compiler_params=pltpu.CompilerParams(dimension_semantics=("arbitrary",) * len(grid),
                                             vmem_limit_bytes=vmem))


def _dot(a, b):
    return jnp.dot(a, b, preferred_element_type=F32)


def _dot_nt(a, b):
    return lax.dot_general(a, b, (((1,), (1,)), ((), ())), preferred_element_type=F32)


def _dot_tn(a, b):
    return lax.dot_general(a, b, (((0,), (0,)), ((), ())), preferred_element_type=F32)


def _split(x):
    hi = x.astype(MXU)
    return hi, (x - hi.astype(F32)).astype(MXU)


def _sigmoid(x):
    return 1.0 / (1.0 + jnp.exp(-x))


def _rms(x, g):
    r = lax.rsqrt(jnp.mean(x * x, axis=-1, keepdims=True) + RMS_EPS)
    return x * r * g


def _rms_bwd(dh, x, g):
    r = lax.rsqrt(jnp.mean(x * x, axis=-1, keepdims=True) + RMS_EPS)
    xh = x * r
    dhg = dh * g
    dx = r * (dhg - xh * jnp.mean(dhg * xh, axis=-1, keepdims=True))
    return dx, jnp.sum(dh * xh, axis=0, keepdims=True)


def _sds(shape, dtype):
    return jax.ShapeDtypeStruct(shape, dtype)


def _wspec(w, e):
    if w.ndim == 3:
        return pl.BlockSpec((None,) + w.shape[1:], lambda i, j: (j, 0, 0))
    return pl.BlockSpec((None, None) + w.shape[2:], lambda i, j: (j, e, 0, 0))


def ffn_fwd(x, g, wg, wu, wd, e=0, *, tm=512):
    T = x.shape[0]

    def body(x_ref, g_ref, wg_ref, wu_ref, wd_ref, xo_ref, h_ref, u_ref, sa_ref, z_ref, acc_ref):
        j = pl.program_id(1)

        @pl.when(j == 0)
        def _():
            h_ref[...] = _rms(x_ref[...], g_ref[...]).astype(MXU)
            acc_ref[...] = jnp.zeros_like(acc_ref)

        h = h_ref[...]
        a = _dot(h, wg_ref[...])
        b = _dot(h, wu_ref[...])
        s = _sigmoid(a)
        sa = a * s
        u_ref[...] = (b * (s * (1.0 + a * (1.0 - s)))).astype(MXU)
        sa_ref[...] = sa.astype(MXU)
        z = (sa * b).astype(MXU)
        z_ref[...] = z
        acc_ref[...] += _dot(z, wd_ref[...])

        @pl.when(j == NSH - 1)
        def _():
            xo_ref[...] = x_ref[...] + 0.5 * acc_ref[...]

    return _call(
        body, name="ffn_fwd", grid=(T // tm, NSH),
        in_specs=[pl.BlockSpec((tm, D), lambda i, j: (i, 0)),
                  pl.BlockSpec((1, D), lambda i, j: (0, 0)),
                  _wspec(wg, e), _wspec(wu, e), _wspec(wd, e)],
        out_specs=[pl.BlockSpec((tm, D), lambda i, j: (i, 0)),
                   pl.BlockSpec((tm, D), lambda i, j: (i, 0)),
                   pl.BlockSpec((None, tm, FS), lambda i, j: (j, i, 0)),
                   pl.BlockSpec((None, tm, FS), lambda i, j: (j, i, 0)),
                   pl.BlockSpec((None, tm, FS), lambda i, j: (j, i, 0))],
        out_shape=[_sds((T, D), F32), _sds((T, D), MXU)] + [_sds((NSH, T, FS), MXU)] * 3,
        scratch=[pltpu.VMEM((tm, D), F32)],
    )(x, g, wg, wu, wd)


def ffn_bwd_dx(x, g, dy, u, sa, wg, wu, wd, e=0, *, tm=512):
    T = x.shape[0]

    def body(x_ref, g_ref, dy_ref, u_ref, sa_ref, wg_ref, wu_ref, wd_ref,
             dx_ref, da_ref, db_ref, dyh_ref, dg_ref):
        i = pl.program_id(0)
        j = pl.program_id(1)
        acc_ref = dx_ref

        @pl.when(j == 0)
        def _():
            dyh_ref[...] = (0.5 * dy_ref[...]).astype(MXU)
            acc_ref[...] = jnp.zeros_like(acc_ref)

        @pl.when((i == 0) & (j == 0))
        def _():
            dg_ref[...] = jnp.zeros_like(dg_ref)

        dz = _dot_nt(dyh_ref[...], wd_ref[...])
        da = (dz * u_ref[...].astype(F32)).astype(MXU)
        db = (dz * sa_ref[...].astype(F32)).astype(MXU)
        da_ref[...] = da
        db_ref[...] = db
        acc_ref[...] += _dot_nt(da, wg_ref[...]) + _dot_nt(db, wu_ref[...])

        @pl.when(j == NSH - 1)
        def _():
            dxn, dg = _rms_bwd(acc_ref[...], x_ref[...], g_ref[...])
            dx_ref[...] = dy_ref[...] + dxn
            dg_ref[...] += dg

    return _call(
        body, name="ffn_bwd_dx", grid=(T // tm, NSH),
        in_specs=[pl.BlockSpec((tm, D), lambda i, j: (i, 0)),
                  pl.BlockSpec((1, D), lambda i, j: (0, 0)),
                  pl.BlockSpec((tm, D), lambda i, j: (i, 0)),
                  pl.BlockSpec((None, tm, FS), lambda i, j: (j, i, 0)),
                  pl.BlockSpec((None, tm, FS), lambda i, j: (j, i, 0)),
                  _wspec(wg, e), _wspec(wu, e), _wspec(wd, e)],
        out_specs=[pl.BlockSpec((tm, D), lambda i, j: (i, 0)),
                   pl.BlockSpec((None, tm, FS), lambda i, j: (j, i, 0)),
                   pl.BlockSpec((None, tm, FS), lambda i, j: (j, i, 0)),
                   pl.BlockSpec((tm, D), lambda i, j: (i, 0)),
                   pl.BlockSpec((1, D), lambda i, j: (0, 0))],
        out_shape=[_sds((T, D), F32), _sds((NSH, T, FS), MXU), _sds((NSH, T, FS), MXU),
                   _sds((T, D), MXU), _sds((1, D), F32)],
    )(x, g, dy, u, sa, wg, wu, wd)


def mm_tn(a, b, *, name, G, M, N, a_step=0, b_step=0, tk=2048, stack=None):
    T = a.shape[-2]
    tk = min(tk, T)
    if stack is not None:
        return _mm_tn_stack(a, b, name=name, G=G, M=M, N=N, tk=tk, stack=stack)

    def spec(arr, width, step):
        if arr.ndim == 3:
            return pl.BlockSpec((None, tk, width), lambda g, k: (g, k, 0))
        return pl.BlockSpec((tk, width), lambda g, k: (k, g * step))

    def body(a_ref, b_ref, o_ref):
        @pl.when(pl.program_id(1) == 0)
        def _():
            o_ref[...] = jnp.zeros_like(o_ref)

        o_ref[...] += _dot_tn(a_ref[...], b_ref[...])

    return _call(
        body, name=name, grid=(G, T // tk),
        in_specs=[spec(a, M, a_step), spec(b, N, b_step)],
        out_specs=pl.BlockSpec((None, M, N), lambda g, k: (g, 0, 0)),
        out_shape=_sds((G, M, N), F32),
    )(a, b)


def norm_mm(x, g, wb, bias, *, name, out_dtype, tm=1024):
    T = x.shape[0]
    tm = min(tm, T)
    G, _, ns = wb.shape
    has_bias = bias is not None

    def body(*refs):
        if has_bias:
            x_ref, g_ref, w_ref, bias_ref, p_ref, h_ref = refs
        else:
            x_ref, g_ref, w_ref, p_ref, h_ref = refs

        @pl.when(pl.program_id(1) == 0)
        def _():
            h_ref[...] = _rms(x_ref[...], g_ref[...]).astype(MXU)

        p = _dot(h_ref[...], w_ref[...])
        if has_bias:
            p = p + bias_ref[...]
        p_ref[...] = p.astype(out_dtype)

    in_specs = [pl.BlockSpec((tm, D), lambda i, j: (i, 0)),
                pl.BlockSpec((1, D), lambda i, j: (0, 0)),
                pl.BlockSpec((None, D, ns), lambda i, j: (j, 0, 0))]
    args = [x, g, wb]
    if has_bias:
        in_specs.append(pl.BlockSpec((None, 1, ns), lambda i, j: (j, 0, 0)))
        args.append(bias)
    return _call(
        body, name=name, grid=(T // tm, G), in_specs=in_specs,
        out_specs=[pl.BlockSpec((tm, ns), lambda i, j: (i, j)),
                   pl.BlockSpec((tm, D), lambda i, j: (i, 0))],
        out_shape=[_sds((T, G * ns), out_dtype), _sds((T, D), MXU)],
    )(*args)


def mm_res(y, w, x, *, name, tm=1024):
    T, K = y.shape
    tm = min(tm, T)

    def body(y_ref, w_ref, x_ref, o_ref):
        o_ref[...] = x_ref[...] + _dot(y_ref[...], w_ref[...])

    return _call(
        body, name=name, grid=(T // tm,),
        in_specs=[pl.BlockSpec((tm, K), lambda i: (i, 0)),
                  pl.BlockSpec((K, D), lambda i: (0, 0)),
                  pl.BlockSpec((tm, D), lambda i: (i, 0))],
        out_specs=pl.BlockSpec((tm, D), lambda i: (i, 0)),
        out_shape=_sds((T, D), F32),
    )(y, w, x)


def mm_nt(a, wt, *, name, tm=1024):
    T, K = a.shape
    tm = min(tm, T)
    N = wt.shape[1]
    w = wt

    def body(a_ref, w_ref, o_ref, ab_ref):
        ab = a_ref[...].astype(MXU)
        ab_ref[...] = ab
        o_ref[...] = _dot(ab, w_ref[...])

    return _call(
        body, name=name, grid=(T // tm,),
        in_specs=[pl.BlockSpec((tm, K), lambda i: (i, 0)),
                  pl.BlockSpec((K, N), lambda i: (0, 0))],
        out_specs=[pl.BlockSpec((tm, N), lambda i: (i, 0)),
                   pl.BlockSpec((tm, K), lambda i: (i, 0))],
        out_shape=[_sds((T, N), F32), _sds((T, K), MXU)],
    )(a, w)


def inproj_bwd(dp, wb, x, g, dres, *, name, tm=1024):
    T = x.shape[0]
    tm = min(tm, T)
    G, ns, _ = wb.shape

    def body(dp_ref, w_ref, x_ref, g_ref, dres_ref, dx_ref, dg_ref, acc_ref):
        i = pl.program_id(0)
        j = pl.program_id(1)

        @pl.when(j == 0)
        def _():
            acc_ref[...] = jnp.zeros_like(acc_ref)

        @pl.when((i == 0) & (j == 0))
        def _():
            dg_ref[...] = jnp.zeros_like(dg_ref)

        acc_ref[...] += _dot(dp_ref[...], w_ref[...])

        @pl.when(j == G - 1)
        def _():
            dxn, dg = _rms_bwd(acc_ref[...], x_ref[...], g_ref[...])
            dx_ref[...] = dres_ref[...] + dxn
            dg_ref[...] += dg

    return _call(
        body, name=name, grid=(T // tm, G),
        in_specs=[pl.BlockSpec((tm, ns), lambda i, j: (i, j)),
                  pl.BlockSpec((None, ns, D), lambda i, j: (j, 0, 0)),
                  pl.BlockSpec((tm, D), lambda i, j: (i, 0)),
                  pl.BlockSpec((1, D), lambda i, j: (0, 0)),
                  pl.BlockSpec((tm, D), lambda i, j: (i, 0))],
        out_specs=[pl.BlockSpec((tm, D), lambda i, j: (i, 0)),
                   pl.BlockSpec((1, D), lambda i, j: (0, 0))],
        out_shape=[_sds((T, D), F32), _sds((1, D), F32)],
        scratch=[pltpu.VMEM((tm, D), F32)],
    )(dp, wb, x, g, dres)


def loss_head(x, gf, tgt, *, tm=512):
    T = x.shape[0]

    def body(x_ref, g_ref, t_ref, loss_ref, dx_ref, dg_ref):
        @pl.when(pl.program_id(0) == 0)
        def _():
            loss_ref[...] = jnp.zeros_like(loss_ref)
            dg_ref[...] = jnp.zeros_like(dg_ref)

        xv = x_ref[...]
        gv = g_ref[...]
        e = _rms(xv, gv) - t_ref[...]
        loss_ref[...] += (0.5 / D) * jnp.sum(e * e)
        dxn, dg = _rms_bwd(e * (1.0 / D), xv, gv)
        dx_ref[...] = dxn
        dg_ref[...] += dg

    return _call(
        body, name="loss_head", grid=(T // tm,),
        in_specs=[pl.BlockSpec((tm, D), lambda i: (i, 0)),
                  pl.BlockSpec((1, D), lambda i: (0, 0)),
                  pl.BlockSpec((tm, D), lambda i: (i, 0))],
        out_specs=[pl.BlockSpec((8, 128), lambda i: (0, 0)),
                   pl.BlockSpec((tm, D), lambda i: (i, 0)),
                   pl.BlockSpec((1, D), lambda i: (0, 0))],
        out_shape=[_sds((8, 128), F32), _sds((T, D), F32), _sds((1, D), F32)],
    )(x, gf, tgt)


def _glu(p):
    return p[:, :D] * _sigmoid(p[:, D:])


def _ln_stats(u):
    mu = jnp.mean(u, axis=-1, keepdims=True)
    xc = u - mu
    rstd = lax.rsqrt(jnp.mean(xc * xc, axis=-1, keepdims=True) + LN_EPS)
    return xc * rstd, rstd


def conv_fwd_core(p, dw, dwb, lng, lnb, *, S, tt=256):
    T = p.shape[0]
    nb = S // tt
    r = tt // HALO

    def body(pc_ref, pp_ref, dw_ref, dwb_ref, lng_ref, lnb_ref, u2_ref, u4_ref, ubuf):
        first = (pl.program_id(0) % nb) == 0
        ubuf[0:HALO, :] = jnp.where(first, 0.0, _glu(pp_ref[...]))
        ubuf[HALO:, :] = _glu(pc_ref[...])
        for c in range(D // 128):
            cs = slice(c * 128, (c + 1) * 128)
            acc = jnp.zeros((tt, 128), F32)
            for k in range(CONV_W):
                acc = acc + dw_ref[k:k + 1, cs] * ubuf[k + 2:k + 2 + tt, cs]
            u2_ref[:, cs] = acc + dwb_ref[:, cs]
        xh, _ = _ln_stats(u2_ref[...])
        u3 = xh * lng_ref[...] + lnb_ref[...]
        u4_ref[...] = (u3 * _sigmoid(u3)).astype(MXU)

    row = pl.BlockSpec((1, D), lambda i: (0, 0))
    return _call(
        body, name="conv_fwd_core", grid=(T // tt,),
        in_specs=[pl.BlockSpec((tt, 2 * D), lambda i: (i, 0)),
                  pl.BlockSpec((HALO, 2 * D), lambda i: (jnp.maximum(i * r - 1, 0), 0)),
                  pl.BlockSpec((HALO, D), lambda i: (0, 0)), row, row, row],
        out_specs=[pl.BlockSpec((tt, D), lambda i: (i, 0)), pl.BlockSpec((tt, D), lambda i: (i, 0))],
        out_shape=[_sds((T, D), F32), _sds((T, D), MXU)],
        scratch=[pltpu.VMEM((tt + HALO, D), F32)],
    )(p, p, dw, dwb, lng, lnb)


def conv_bwd_rows(dy, wout, u2, lng, lnb, *, tm=512):
    T = dy.shape[0]

    def body(dy_ref, w_ref, u2_ref, lng_ref, lnb_ref, du2_ref, dyb_ref, dlng_ref, dlnb_ref, ddwb_ref):
        @pl.when(pl.program_id(0) == 0)
        def _():
            dlng_ref[...] = jnp.zeros_like(dlng_ref)
            dlnb_ref[...] = jnp.zeros_like(dlnb_ref)
            ddwb_ref[...] = jnp.zeros_like(ddwb_ref)

        dyb = dy_ref[...].astype(MXU)
        dyb_ref[...] = dyb
        du4 = _dot(dyb, w_ref[...])
        xh, rstd = _ln_stats(u2_ref[...])
        lng_v = lng_ref[...]
        u3 = xh * lng_v + lnb_ref[...]
        s = _sigmoid(u3)
        du3 = du4 * (s * (1.0 + u3 * (1.0 - s)))
        dlng_ref[...] += jnp.sum(du3 * xh, axis=0, keepdims=True)
        dlnb_ref[...] += jnp.sum(du3, axis=0, keepdims=True)
        dxh = du3 * lng_v
        du2 = rstd * (dxh - jnp.mean(dxh, axis=-1, keepdims=True)
                      - xh * jnp.mean(dxh * xh, axis=-1, keepdims=True))
        du2_ref[...] = du2
        ddwb_ref[...] += jnp.sum(du2, axis=0, keepdims=True)

    row = pl.BlockSpec((1, D), lambda i: (0, 0))
    blk = pl.BlockSpec((tm, D), lambda i: (i, 0))
    return _call(
        body, name="conv_bwd_rows", grid=(T // tm,),
        in_specs=[blk, pl.BlockSpec((D, D), lambda i: (0, 0)), blk, row, row],
        out_specs=[blk, blk, row, row, row],
        out_shape=[_sds((T, D), F32), _sds((T, D), MXU), _sds((1, D), F32), _sds((1, D), F32), _sds((1, D), F32)],
    )(dy, wout, u2, lng, lnb)


def conv_bwd_core(du2, p, dw, *, S, tt=256):
    T = p.shape[0]
    nb = S // tt
    r = tt // HALO
    last_halo = T // HALO - 1

    def body(dc_ref, dn_ref, pc_ref, pp_ref, dw_ref, dp_ref, dbin_ref, ddw_ref, ubuf, dbuf):
        i = pl.program_id(0)

        @pl.when(i == 0)
        def _():
            dbin_ref[...] = jnp.zeros_like(dbin_ref)
            ddw_ref[...] = jnp.zeros_like(ddw_ref)

        first = (i % nb) == 0
        last = (i % nb) == nb - 1
        ubuf[0:HALO, :] = jnp.where(first, 0.0, _glu(pp_ref[...]))
        ubuf[HALO:, :] = _glu(pc_ref[...])
        dbuf[0:tt, :] = dc_ref[...]
        dbuf[tt:, :] = jnp.where(last, 0.0, dn_ref[...])
        pc = pc_ref[...]
        for c in range(D // 128):
            cs = slice(c * 128, (c + 1) * 128)
            dcur = dbuf[0:tt, cs]
            du = jnp.zeros((tt, 128), F32)
            for k in range(CONV_W):
                ddw_ref[k:k + 1, cs] += jnp.sum(dcur * ubuf[k + 2:k + 2 + tt, cs], axis=0, keepdims=True)
                du = du + dw_ref[k:k + 1, cs] * dbuf[CONV_W - 1 - k:CONV_W - 1 - k + tt, cs]
            a = pc[:, c * 128:(c + 1) * 128]
            sb = _sigmoid(pc[:, D + c * 128:D + (c + 1) * 128])
            da = du * sb
            db = du * a * sb * (1.0 - sb)
            dp_ref[:, cs] = da.astype(MXU)
            dp_ref[:, D + c * 128:D + (c + 1) * 128] = db.astype(MXU)
            dbin_ref[:, cs] += jnp.sum(da, axis=0, keepdims=True)
            dbin_ref[:, D + c * 128:D + (c + 1) * 128] += jnp.sum(db, axis=0, keepdims=True)

    return _call(
        body, name="conv_bwd_core", grid=(T // tt,),
        in_specs=[pl.BlockSpec((tt, D), lambda i: (i, 0)),
                  pl.BlockSpec((HALO, D), lambda i: (jnp.minimum((i + 1) * r, last_halo), 0)),
                  pl.BlockSpec((tt, 2 * D), lambda i: (i, 0)),
                  pl.BlockSpec((HALO, 2 * D), lambda i: (jnp.maximum(i * r - 1, 0), 0)),
                  pl.BlockSpec((HALO, D), lambda i: (0, 0))],
        out_specs=[pl.BlockSpec((tt, 2 * D), lambda i: (i, 0)),
                   pl.BlockSpec((1, 2 * D), lambda i: (0, 0)),
                   pl.BlockSpec((HALO, D), lambda i: (0, 0))],
        out_shape=[_sds((T, 2 * D), MXU), _sds((1, 2 * D), F32), _sds((HALO, D), F32)],
        scratch=[pltpu.VMEM((tt + HALO, D), F32), pltpu.VMEM((tt + HALO, D), F32)],
    )(du2, du2, p, p, dw)


PH = 16


def _pool_cnt(i, nb, tt, win):
    pos = (i % nb) * tt + lax.broadcasted_iota(jnp.int32, (tt, 1), 0)
    return jnp.minimum(pos + 1, win).astype(F32)


def pool_fwd(x, g, wp, scale, *, S, tt=256):
    T = x.shape[0]
    nb = S // tt
    r = tt // PH

    def body(xc_ref, xp_ref, g_ref, wp_ref, sc_ref, xo_ref, m_ref, hbuf):
        i = pl.program_id(0)
        first = (i % nb) == 0
        gv = g_ref[...]
        hbuf[0:PH, :] = jnp.where(first, 0.0, _rms(xp_ref[...], gv))
        xc = xc_ref[...]
        hbuf[PH:, :] = _rms(xc, gv)
        for gi, win in enumerate(POOL_WIN):
            gs = slice(gi * POOL_G, (gi + 1) * POOL_G)
            acc = hbuf[PH:PH + tt, gs]
            for j in range(1, win):
                acc = acc + hbuf[PH - j:PH - j + tt, gs]
            m = (acc / _pool_cnt(i, nb, tt, win) - hbuf[PH:PH + tt, gs]).astype(MXU)
            m_ref[:, gs] = m
            xo_ref[:, gs] = xc[:, gs] + _dot(m, wp_ref[gi]) * sc_ref[:, gs]

    row = pl.BlockSpec((1, D), lambda i: (0, 0))
    blk = pl.BlockSpec((tt, D), lambda i: (i, 0))
    return _call(
        body, name="pool_fwd", grid=(T // tt,),
        in_specs=[blk, pl.BlockSpec((PH, D), lambda i: (jnp.maximum(i * r - 1, 0), 0)), row,
                  pl.BlockSpec((len(POOL_WIN), POOL_G, POOL_G), lambda i: (0, 0, 0)), row],
        out_specs=[blk, blk],
        out_shape=[_sds((T, D), F32), _sds((T, D), MXU)],
        scratch=[pltpu.VMEM((tt + PH, D), F32)],
    )(x, x, g, wp, scale)


def pool_bwd_rows(dy, m, wp, scale, *, S, tt=256):
    T = dy.shape[0]
    nb = S // tt

    def body(dy_ref, m_ref, wp_ref, sc_ref, dmc_ref, dyp_ref, dsc_ref):
        i = pl.program_id(0)

        @pl.when(i == 0)
        def _():
            dsc_ref[...] = jnp.zeros_like(dsc_ref)

        for gi, win in enumerate(POOL_WIN):
            gs = slice(gi * POOL_G, (gi + 1) * POOL_G)
            dyg = dy_ref[:, gs]
            w = wp_ref[gi]
            dsc_ref[:, gs] += jnp.sum(dyg * _dot(m_ref[:, gs], w), axis=0, keepdims=True)
            dyp = (dyg * sc_ref[:, gs]).astype(MXU)
            dyp_ref[:, gs] = dyp
            dmc_ref[:, gs] = _dot_nt(dyp, w) / _pool_cnt(i, nb, tt, win)

    row = pl.BlockSpec((1, D), lambda i: (0, 0))
    blk = pl.BlockSpec((tt, D), lambda i: (i, 0))
    return _call(
        body, name="pool_bwd_rows", grid=(T // tt,),
        in_specs=[blk, blk, pl.BlockSpec((len(POOL_WIN), POOL_G, POOL_G), lambda i: (0, 0, 0)), row],
        out_specs=[blk, blk, row],
        out_shape=[_sds((T, D), F32), _sds((T, D), MXU), _sds((1, D), F32)],
    )(dy, m, wp, scale)


def pool_bwd_core(dmc, x, g, dres, *, S, tt=256):
    T = x.shape[0]
    nb = S // tt
    r = tt // PH
    last_halo = T // PH - 1

    def body(dc_ref, dn_ref, x_ref, g_ref, dres_ref, dx_ref, dg_ref, dbuf, dh_buf):
        i = pl.program_id(0)

        @pl.when(i == 0)
        def _():
            dg_ref[...] = jnp.zeros_like(dg_ref)

        last = (i % nb) == nb - 1
        dbuf[0:tt, :] = dc_ref[...]
        dbuf[tt:, :] = jnp.where(last, 0.0, dn_ref[...])
        for gi, win in enumerate(POOL_WIN):
            gs = slice(gi * POOL_G, (gi + 1) * POOL_G)
            cur = dbuf[0:tt, gs]
            acc = cur
            for j in range(1, win):
                acc = acc + dbuf[j:j + tt, gs]
            dh_buf[:, gs] = acc - cur * _pool_cnt(i, nb, tt, win)
        dxn, dg = _rms_bwd(dh_buf[...], x_ref[...], g_ref[...])
        dx_ref[...] = dres_ref[...] + dxn
        dg_ref[...] += dg

    row = pl.BlockSpec((1, D), lambda i: (0, 0))
    blk = pl.BlockSpec((tt, D), lambda i: (i, 0))
    return _call(
        body, name="pool_bwd_core", grid=(T // tt,),
        in_specs=[blk, pl.BlockSpec((PH, D), lambda i: (jnp.minimum((i + 1) * r, last_halo), 0)), blk, row, blk],
        out_specs=[blk, row],
        out_shape=[_sds((T, D), F32), _sds((1, D), F32)],
        scratch=[pltpu.VMEM((tt + PH, D), F32), pltpu.VMEM((tt, D), F32)],
    )(dmc, dmc, x, g, dres)


NEG = -1e30


def _tri(n, upper=False):
    r = lax.broadcasted_iota(jnp.int32, (n, n), 0)
    c = lax.broadcasted_iota(jnp.int32, (n, n), 1)
    return (r <= c if upper else r >= c).astype(F32)


def _dot_hi(a, b):
    return jnp.dot(a, b, preferred_element_type=F32, precision=lax.Precision.HIGHEST)


def _log_sigmoid(z):
    return jnp.minimum(z, 0.0) - jnp.log(1.0 + jnp.exp(-jnp.abs(z)))


def fox_cum(fl, bf, *, S, tt=256):
    T = fl.shape[0]
    nb = S // tt

    def body(fl_ref, bf_ref, c_ref, carry):
        i = pl.program_id(0)

        @pl.when((i % nb) == 0)
        def _():
            carry[...] = jnp.zeros_like(carry)

        lf = _log_sigmoid(fl_ref[...] + bf_ref[...])
        c = _dot_hi(_tri(tt), lf) + carry[...]
        c_ref[...] = c
        carry[...] = c[tt - 1:tt, :]

    return _call(
        body, name="fox_cum", grid=(T // tt,),
        in_specs=[pl.BlockSpec((tt, 128), lambda i: (i, 0)), pl.BlockSpec((1, 128), lambda i: (0, 0))],
        out_specs=pl.BlockSpec((tt, 128), lambda i: (i, 0)),
        out_shape=_sds((T, 128), F32),
        scratch=[pltpu.VMEM((1, 128), F32)],
    )(fl, bf)


HW = 128
FOX_T = 512
COL_ONE = FOX_DH + 3
COL_LSE = FOX_DH + 6


def _parts(x):
    hi = x.astype(MXU).astype(F32)
    mid = (x - hi).astype(MXU).astype(F32)
    lo = (x - hi - mid).astype(MXU).astype(F32)
    return [hi, mid, lo]


def _aug(n, cols):
    lane = lax.broadcasted_iota(jnp.int32, (n, HW - FOX_DH), 1)
    out = jnp.zeros((n, HW - FOX_DH), F32)
    for i, cval in enumerate(cols):
        out = jnp.where(lane == i, cval, out)
    return out


def fox_prep(p, c, *, tt=256):
    T = p.shape[0]

    def body(q_ref, k_ref, v_ref, c_ref, qa_ref, ka_ref, va_ref):
        ones = [1.0, 1.0, 1.0]
        for h in range(FOX_H):
            hs = slice(h * FOX_DH, (h + 1) * FOX_DH)
            lo, mid = h * HW, h * HW + FOX_DH
            cp = _parts(c_ref[:, h:h + 1])
            qa_ref[:, lo:mid] = (q_ref[:, hs].astype(F32) * (FOX_DH ** -0.5)).astype(MXU)
            qa_ref[:, mid:lo + HW] = _aug(tt, cp + ones).astype(MXU)
            ka_ref[:, lo:mid] = k_ref[:, hs]
            ka_ref[:, mid:lo + HW] = _aug(tt, ones + [-x for x in cp] + ones).astype(MXU)
            va_ref[:, lo:mid] = v_ref[:, hs]
            va_ref[:, mid:lo + HW] = _aug(tt, ones).astype(MXU)

    wide = pl.BlockSpec((tt, FOX_H * HW), lambda i: (i, 0))
    col = lambda k: pl.BlockSpec((tt, D), lambda i: (i, k))
    return _call(body, name="fox_prep", grid=(T // tt,),
                 in_specs=[col(0), col(1), col(2), pl.BlockSpec((tt, 128), lambda i: (i, 0))],
                 out_specs=[wide, wide, wide], out_shape=[_sds((T, FOX_H * HW), MXU)] * 3)(p, p, p, c)


def _causal(x, fill):
    r = lax.broadcasted_iota(jnp.int32, x.shape, 0)
    c = lax.broadcasted_iota(jnp.int32, x.shape, 1)
    return jnp.where(r >= c, x, fill)


def _wide_specs(tq, nq, q_of, k_of):
    qs = pl.BlockSpec((tq, FOX_H * HW), lambda b, i, j: (b * nq + q_of(i, j), 0))
    ks = pl.BlockSpec((tq, FOX_H * HW), lambda b, i, j: (b * nq + k_of(i, j), 0))
    return qs, ks


def fox2_fwd(qa, ka, va, *, S, tq=256, tk=512):
    T = qa.shape[0]
    tk = min(tk, S)
    nq = S // tq
    nk = S // tk
    r = tk // tq

    def body(q_ref, k_ref, v_ref, o_ref, o32_ref, lse_ref, m_sc, acc, acc_lo):
        qi = pl.program_id(1)
        ki = pl.program_id(2)
        last = qi // r

        @pl.when(ki == 0)
        def _():
            m_sc[...] = jnp.full_like(m_sc, NEG)
            acc[...] = jnp.zeros_like(acc)
            acc_lo[...] = jnp.zeros_like(acc_lo)

        def step(diag):
            for h in range(FOX_H):
                ws = slice(h * HW, (h + 1) * HW)
                s = _dot_nt(q_ref[:, ws], k_ref[:, ws])
                if diag:
                    row = qi * tq + lax.broadcasted_iota(jnp.int32, s.shape, 0)
                    col = ki * tk + lax.broadcasted_iota(jnp.int32, s.shape, 1)
                    s = jnp.where(row >= col, s, NEG)
                m_prev = m_sc[h]
                m_new = jnp.maximum(m_prev, jnp.max(s, axis=-1, keepdims=True))
                alpha = jnp.exp(m_prev - m_new)
                hi, lo = _split(jnp.exp(s - jnp.tile(m_new, (1, tk // 128))))
                acc[h] = alpha * acc[h] + _dot(hi, v_ref[:, ws])
                acc_lo[h] = alpha * acc_lo[h] + _dot(lo, v_ref[:, ws])
                m_sc[h] = m_new

        @pl.when(ki < last)
        def _():
            step(False)

        @pl.when(ki == last)
        def _():
            step(True)
            for h in range(FOX_H):
                hs = slice(h * FOX_DH, (h + 1) * FOX_DH)
                full = acc[h] + acc_lo[h]
                l = full[:, FOX_DH:FOX_DH + 1]
                o_ref[:, hs] = (acc[h][:, :FOX_DH] / l).astype(MXU)
                o32_ref[:, hs] = full[:, :FOX_DH] / l
                lse_ref[:, h:h + 1] = m_sc[h][:, 0:1] + jnp.log(l)

    qs = pl.BlockSpec((tq, FOX_H * HW), lambda b, i, j: (b * nq + i, 0))
    ks = pl.BlockSpec((tk, FOX_H * HW), lambda b, i, j: (b * nk + jnp.minimum(j, i // r), 0))
    orow = pl.BlockSpec((tq, D), lambda b, i, j: (b * nq + i, 0))
    return _call(
        body, name="fox_fwd", grid=(T // S, nq, nk),
        in_specs=[qs, ks, ks],
        out_specs=[orow, orow, pl.BlockSpec((tq, 128), lambda b, i, j: (b * nq + i, 0))],
        out_shape=[_sds((T, D), MXU), _sds((T, D), F32), _sds((T, 128), F32)],
        scratch=[pltpu.VMEM((FOX_H, tq, 128), F32), pltpu.VMEM((FOX_H, tq, HW), F32),
                 pltpu.VMEM((FOX_H, tq, HW), F32)],
    )(qa, ka, va)


def fox2_prep_bwd(do, o32, lse, qa, *, tt=256):
    T = do.shape[0]

    def body(do_ref, o_ref, lse_ref, qa_ref, qb_ref, da_ref):
        lane = lax.broadcasted_iota(jnp.int32, (tt, HW), 1)
        dob = do_ref[...].astype(MXU)
        for h in range(FOX_H):
            hs = slice(h * FOX_DH, (h + 1) * FOX_DH)
            ws = slice(h * HW, (h + 1) * HW)
            doh = dob[:, hs]
            delta = jnp.sum(doh.astype(F32) * o_ref[:, hs], axis=-1, keepdims=True)
            da_ref[:, h * HW:h * HW + FOX_DH] = doh
            da_ref[:, h * HW + FOX_DH:(h + 1) * HW] = _aug(tt, [-x for x in _parts(delta)]).astype(MXU)
            tile = qa_ref[:, ws]
            for i, part in enumerate(_parts(lse_ref[:, h:h + 1])):
                tile = jnp.where(lane == COL_LSE + i, (-part).astype(MXU), tile)
            qb_ref[:, ws] = tile

    wide = pl.BlockSpec((tt, FOX_H * HW), lambda i: (i, 0))
    blk = pl.BlockSpec((tt, D), lambda i: (i, 0))
    return _call(body, name="fox_prep_bwd", grid=(T // tt,),
                 in_specs=[blk, blk, pl.BlockSpec((tt, 128), lambda i: (i, 0)), wide],
                 out_specs=[wide, wide],
                 out_shape=[_sds((T, FOX_H * HW), MXU), _sds((T, FOX_H * HW), MXU)],
                 )(do, o32, lse, qa)


def fox2_dq(qb, ka, va, da, *, S, tq=256):
    T = qb.shape[0]
    nq = S // tq

    def body(q_ref, k_ref, v_ref, d_ref, dq_ref, acc):
        qi = pl.program_id(1)
        ki = pl.program_id(2)

        @pl.when(ki == 0)
        def _():
            acc[...] = jnp.zeros_like(acc)

        def step(diag):
            for h in range(FOX_H):
                ws = slice(h * HW, (h + 1) * HW)
                kh = k_ref[:, ws]
                p = jnp.exp(_dot_nt(q_ref[:, ws], kh))
                if diag:
                    p = _causal(p, 0.0)
                ds = (p * _dot_nt(d_ref[:, ws], v_ref[:, ws])).astype(MXU)
                acc[h] += _dot(ds, kh)

        @pl.when(ki < qi)
        def _():
            step(False)

        @pl.when(ki == qi)
        def _():
            step(True)
            for h in range(FOX_H):
                dq_ref[:, h * FOX_DH:(h + 1) * FOX_DH] = (acc[h][:, :FOX_DH] * (FOX_DH ** -0.5)).astype(MXU)

    qs, ks = _wide_specs(tq, nq, lambda i, j: i, lambda i, j: jnp.minimum(i, j))
    return _call(
        body, name="fox_bwd_dq", grid=(T // S, nq, nq),
        in_specs=[qs, ks, ks, qs],
        out_specs=pl.BlockSpec((tq, D), lambda b, i, j: (b * nq + i, 0)),
        out_shape=_sds((T, D), MXU),
        scratch=[pltpu.VMEM((FOX_H, tq, HW), F32)],
    )(qb, ka, va, da)


def fox2_dkv(qb, ka, va, da, *, S, tq=256):
    T = qb.shape[0]
    nq = S // tq

    def body(q_ref, k_ref, v_ref, d_ref, dk_ref, dv_ref, dck_ref, dk_acc, dv_acc):
        ki = pl.program_id(1)
        qi = pl.program_id(2)

        @pl.when(qi == 0)
        def _():
            dk_acc[...] = jnp.zeros_like(dk_acc)
            dv_acc[...] = jnp.zeros_like(dv_acc)

        def step(diag):
            for h in range(FOX_H):
                ws = slice(h * HW, (h + 1) * HW)
                qh = q_ref[:, ws]
                dh = d_ref[:, ws]
                pt = jnp.exp(_dot_nt(k_ref[:, ws], qh))
                if diag:
                    r = lax.broadcasted_iota(jnp.int32, pt.shape, 0)
                    c = lax.broadcasted_iota(jnp.int32, pt.shape, 1)
                    pt = jnp.where(r <= c, pt, 0.0)
                dv_acc[h] += _dot(pt.astype(MXU), dh)
                hi, lo = _split(pt * _dot_nt(v_ref[:, ws], dh))
                dk_acc[h] += _dot(hi, qh) + _dot(lo, qh)

        @pl.when(qi > ki)
        def _():
            step(False)

        @pl.when(qi == ki)
        def _():
            step(True)

        @pl.when(qi == nq - 1)
        def _():
            dck_ref[...] = jnp.zeros_like(dck_ref)
            for h in range(FOX_H):
                hs = slice(h * FOX_DH, (h + 1) * FOX_DH)
                dk_ref[:, hs] = dk_acc[h][:, :FOX_DH].astype(MXU)
                dv_ref[:, hs] = dv_acc[h][:, :FOX_DH].astype(MXU)
                dck_ref[:, h:h + 1] = dk_acc[h][:, COL_ONE:COL_ONE + 1]

    qs, ks = _wide_specs(tq, nq, lambda i, j: jnp.maximum(i, j), lambda i, j: i)
    ko = pl.BlockSpec((tq, D), lambda b, i, j: (b * nq + i, 0))
    return _call(
        body, name="fox_bwd_dkv", grid=(T // S, nq, nq),
        in_specs=[qs, ks, ks, qs],
        out_specs=[ko, ko, pl.BlockSpec((tq, 128), lambda b, i, j: (b * nq + i, 0))],
        out_shape=[_sds((T, D), MXU), _sds((T, D), MXU), _sds((T, 128), F32)],
        scratch=[pltpu.VMEM((FOX_H, tq, HW), F32), pltpu.VMEM((FOX_H, tq, HW), F32)],
    )(qb, ka, va, da)


def fox_fin(dck, fl, bf, *, S, tt=256):
    T = fl.shape[0]
    nb = S // tt
    nblk = T // tt

    def body(dck_ref, fl_ref, bf_ref, dfl_ref, dbf_ref, carry):
        i = pl.program_id(0)

        @pl.when(i == 0)
        def _():
            dbf_ref[...] = jnp.zeros_like(dbf_ref)

        @pl.when((i % nb) == 0)
        def _():
            carry[...] = jnp.zeros_like(carry)

        lane = lax.broadcasted_iota(jnp.int32, (tt, 128), 1)
        dc = jnp.where(lane < FOX_H, -dck_ref[...], 0.0)
        dlf = _dot_hi(_tri(tt, upper=True), dc) + carry[...]
        carry[...] = dlf[0:1, :]
        dfl = dlf * _sigmoid(-(fl_ref[...] + bf_ref[...]))
        dfl_ref[...] = dfl.astype(MXU)
        dbf_ref[...] += jnp.sum(dfl, axis=0, keepdims=True)

    rev = pl.BlockSpec((tt, 128), lambda i: (nblk - 1 - i, 0))
    row = pl.BlockSpec((1, 128), lambda i: (0, 0))
    return _call(
        body, name="fox_fin", grid=(nblk,),
        in_specs=[rev, rev, row],
        out_specs=[rev, row],
        out_shape=[_sds((T, 128), MXU), _sds((1, 128), F32)],
        scratch=[pltpu.VMEM((1, 128), F32)],
    )(dck, fl, bf)


def lb_fwd(logits):
    def body(l_ref, lb_ref):
        lv = l_ref[...]
        e = jnp.exp(lv - jnp.max(lv, axis=0, keepdims=True))
        p = e / jnp.sum(e, axis=0, keepdims=True)
        lb_ref[...] = p[1:2, :] + p[2:3, :]

    return _call(body, name="lb_fwd", grid=(1,),
                 in_specs=[pl.BlockSpec((DEPTH, D), lambda i: (0, 0))],
                 out_specs=pl.BlockSpec((1, D), lambda i: (0, 0)),
                 out_shape=_sds((1, D), F32))(logits)


def lb_bwd(logits, dlb):
    def body(l_ref, d_ref, o_ref):
        lv = l_ref[...]
        e = jnp.exp(lv - jnp.max(lv, axis=0, keepdims=True))
        p = e / jnp.sum(e, axis=0, keepdims=True)
        lb = p[1:2, :] + p[2:3, :]
        row = lax.broadcasted_iota(jnp.int32, (DEPTH, D), 0)
        sel = ((row == 1) | (row == 2)).astype(F32)
        o_ref[...] = p * (sel - lb) * d_ref[...]

    return _call(body, name="lb_bwd", grid=(1,),
                 in_specs=[pl.BlockSpec((DEPTH, D), lambda i: (0, 0)), pl.BlockSpec((1, D), lambda i: (0, 0))],
                 out_specs=pl.BlockSpec((DEPTH, D), lambda i: (0, 0)),
                 out_shape=_sds((DEPTH, D), F32))(logits, dlb)


def _hgrn_gates(qr, fr, lb):
    sg = _sigmoid(fr)
    sneg = _sigmoid(-fr)
    f = lb + (1.0 - lb) * sg
    kk = (1.0 - lb) * sneg
    G = _dot_hi(_tri(HG_C), jnp.log(f))
    eG = jnp.exp(G)
    einv = jnp.exp(-G)
    elast = jnp.exp(G[HG_C - 1:HG_C, :] - G)
    q = qr * _sigmoid(qr)
    return dict(q=q, kk=kk, f=f, sg=sg, sneg=sneg, eG=eG, einv=einv, elast=elast,
                qg=q * eG, kinv=kk * einv, khat=kk * elast, glast=jnp.exp(G[HG_C - 1:HG_C, :]))


def _tril_mask(x):
    r = lax.broadcasted_iota(jnp.int32, x.shape, 0)
    c = lax.broadcasted_iota(jnp.int32, x.shape, 1)
    return jnp.where(r >= c, x, 0.0)


def hgrn_fwd(p, lb, ng, *, S, R=256):
    T = p.shape[0]
    R = min(R, S)
    nr = S // R
    ncr = R // HG_C

    def body(q_ref, f_ref, v_ref, gt_ref, lb_ref, ng_ref, y_ref, o_ref, st_ref, st):
        @pl.when(pl.program_id(1) == 0)
        def _():
            st[...] = jnp.zeros_like(st)

        lbv = lb_ref[...]
        for ch in range(ncr):
            rows = slice(ch * HG_C, (ch + 1) * HG_C)
            gt = _hgrn_gates(q_ref[rows, :], f_ref[rows, :], lbv)
            for h in range(HG_H):
                hs = slice(h * HG_DK, (h + 1) * HG_DK)
                sp = st[h]
                st_ref[ch, h] = sp
                qg = gt["qg"][:, hs].astype(MXU)
                vh = v_ref[rows, hs].astype(MXU)
                A = _tril_mask(_dot_nt(qg, gt["kinv"][:, hs].astype(MXU)))
                o_ref[rows, hs] = _dot_nt(qg, sp.astype(MXU)) + _dot(A.astype(MXU), vh)
                st[h] = sp * gt["glast"][:, hs] + _dot_tn(vh, gt["khat"][:, hs].astype(MXU))
        gate = gt_ref[...]
        sgate = gate * _sigmoid(gate)
        for h in range(HG_H):
            hs = slice(h * HG_DK, (h + 1) * HG_DK)
            oh = o_ref[:, hs]
            r = lax.rsqrt(jnp.mean(oh * oh, axis=-1, keepdims=True) + RMS_EPS)
            y_ref[:, hs] = (oh * r * ng_ref[:, hs] * sgate[:, hs]).astype(MXU)

    col = lambda c: pl.BlockSpec((R, D), lambda b, i: (b * nr + i, c))
    row = pl.BlockSpec((1, D), lambda b, i: (0, 0))
    return _call(
        body, name="hgrn_fwd", grid=(T // S, nr),
        in_specs=[col(0), col(1), col(2), col(3), row, row],
        out_specs=[col(0), col(0),
                   pl.BlockSpec((ncr, HG_H, HG_DK, HG_DK), lambda b, i: (b * nr + i, 0, 0, 0))],
        out_shape=[_sds((T, D), MXU), _sds((T, D), F32), _sds((T // HG_C, HG_H, HG_DK, HG_DK), F32)],
        scratch=[pltpu.VMEM((HG_H, HG_DK, HG_DK), F32)],
    )(p, p, p, p, lb, ng)


def hgrn_bwd(p, o, dyo, states, lb, ng, *, S, R=256):
    T = p.shape[0]
    R = min(R, S)
    nr = S // R
    ncr = R // HG_C

    def body(q_ref, f_ref, v_ref, gt_ref, o_ref, dy_ref, st_ref, lb_ref, ng_ref,
             dp_ref, dlb_ref, dng_ref, dst, do_buf, dG_buf, dqb, dkb):
        b = pl.program_id(0)
        i = pl.program_id(1)

        @pl.when(i == 0)
        def _():
            dst[...] = jnp.zeros_like(dst)

        @pl.when((b == 0) & (i == 0))
        def _():
            dlb_ref[...] = jnp.zeros_like(dlb_ref)
            dng_ref[...] = jnp.zeros_like(dng_ref)

        lbv = lb_ref[...]
        gate = gt_ref[...]
        sg_gate = _sigmoid(gate)
        silu_gate = gate * sg_gate
        for h in range(HG_H):
            hs = slice(h * HG_DK, (h + 1) * HG_DK)
            oh = o_ref[:, hs]
            r = lax.rsqrt(jnp.mean(oh * oh, axis=-1, keepdims=True) + RMS_EPS)
            ohat = oh * r
            dyh = dy_ref[:, hs]
            ngh = ng_ref[:, hs]
            dng_ref[:, hs] += jnp.sum(dyh * silu_gate[:, hs] * ohat, axis=0, keepdims=True)
            dp_ref[:, 3 * D + h * HG_DK:3 * D + (h + 1) * HG_DK] = (
                dyh * ohat * ngh * (sg_gate[:, hs] * (1.0 + gate[:, hs] * (1.0 - sg_gate[:, hs])))).astype(MXU)
            dn = dyh * ngh * silu_gate[:, hs]
            do_buf[:, hs] = r * (dn - ohat * jnp.mean(dn * ohat, axis=-1, keepdims=True))

        lastrow = lax.broadcasted_iota(jnp.int32, (HG_C, HG_DK), 0) == HG_C - 1
        for ch in reversed(range(ncr)):
            rows = slice(ch * HG_C, (ch + 1) * HG_C)
            qr = q_ref[rows, :]
            gt = _hgrn_gates(qr, f_ref[rows, :], lbv)
            for h in range(HG_H):
                hs = slice(h * HG_DK, (h + 1) * HG_DK)
                sp = st_ref[ch, h]
                ds = dst[h]
                qg32, kinv32, khat32 = gt["qg"][:, hs], gt["kinv"][:, hs], gt["khat"][:, hs]
                qg, kinv, khat = qg32.astype(MXU), kinv32.astype(MXU), khat32.astype(MXU)
                vh = v_ref[rows, hs].astype(MXU)
                doh = do_buf[rows, hs].astype(MXU)
                dsb = ds.astype(MXU)
                A = _tril_mask(_dot_nt(qg, kinv)).astype(MXU)
                dA = _tril_mask(_dot_nt(doh, vh)).astype(MXU)
                dqg = _dot(doh, sp.astype(MXU)) + _dot(dA, kinv)
                dkinv = _dot_tn(dA, qg)
                dp_ref[rows, 2 * D + h * HG_DK:2 * D + (h + 1) * HG_DK] = (
                    _dot_tn(A, doh) + _dot_nt(khat, dsb)).astype(MXU)
                dkhat = _dot(vh, dsb)
                glast = gt["glast"][:, hs]
                qg32, kinv32, khat32 = qg.astype(F32), kinv.astype(F32), khat.astype(F32)
                extra = (glast * jnp.sum(dsb.astype(F32) * sp.astype(MXU).astype(F32), axis=0, keepdims=True)
                         + jnp.sum(dkhat * khat32, axis=0, keepdims=True))
                dst[h] = ds * glast + _dot_tn(doh, qg)
                dG = dqg * qg32 - dkinv * kinv32 - dkhat * khat32
                dG_buf[:, hs] = dG + jnp.where(lastrow, extra, 0.0)
                dqb[:, hs] = dqg * gt["eG"][:, hs]
                dkb[:, hs] = dkinv * gt["einv"][:, hs] + dkhat * gt["elast"][:, hs]
            dg = _dot_hi(_tri(HG_C, upper=True), dG_buf[...])
            dk = dkb[...]
            sneg, f = gt["sneg"], gt["f"]
            c1 = (1.0 - lbv) * gt["sg"] * sneg
            dp_ref[rows, D:2 * D] = (dg * c1 / f - dk * c1).astype(MXU)
            dlb_ref[...] += jnp.sum(dg * sneg / f - dk * sneg, axis=0, keepdims=True)
            sq = _sigmoid(qr)
            dp_ref[rows, 0:D] = (dqb[...] * (sq * (1.0 + qr * (1.0 - sq)))).astype(MXU)

    rev = lambda b, i: b * nr + nr - 1 - i
    col = lambda c: pl.BlockSpec((R, D), lambda b, i: (rev(b, i), c))
    row = pl.BlockSpec((1, D), lambda b, i: (0, 0))
    return _call(
        body, name="hgrn_bwd", grid=(T // S, nr),
        in_specs=[col(0), col(1), col(2), col(3), col(0), col(0),
                  pl.BlockSpec((ncr, HG_H, HG_DK, HG_DK), lambda b, i: (rev(b, i), 0, 0, 0)), row, row],
        out_specs=[pl.BlockSpec((R, 4 * D), lambda b, i: (rev(b, i), 0)), row, row],
        out_shape=[_sds((T, 4 * D), MXU), _sds((1, D), F32), _sds((1, D), F32)],
        scratch=[pltpu.VMEM((HG_H, HG_DK, HG_DK), F32), pltpu.VMEM((R, D), F32), pltpu.VMEM((HG_C, D), F32),
                 pltpu.VMEM((HG_C, D), F32), pltpu.VMEM((HG_C, D), F32)],
    )(p, p, p, p, o, dyo, states, lb, ng)


def _mm_tn_stack(a, b, *, name, G, M, N, tk, stack):
    E, e, buf = stack
    T = a.shape[-2]

    def spec(arr, width):
        if arr.ndim == 3:
            return pl.BlockSpec((None, tk, width), lambda g, k: (g, k, 0))
        return pl.BlockSpec((tk, width), lambda g, k: (k, 0))

    def body(*refs):
        a_ref, b_ref, o_ref = refs[0], refs[1], refs[-1]

        @pl.when(pl.program_id(1) == 0)
        def _():
            o_ref[...] = jnp.zeros_like(o_ref)

        o_ref[...] += _dot_tn(a_ref[...], b_ref[...])

    in_specs = [spec(a, M), spec(b, N)]
    args = [a, b]
    aliases = {}
    if buf is not None:
        in_specs.append(pl.BlockSpec(memory_space=pl.ANY))
        args.append(buf)
        aliases = {2: 0}
    return pl.pallas_call(
        body, name=name, grid=(G, T // tk), in_specs=in_specs,
        out_specs=pl.BlockSpec((None, None, M, N), lambda g, k: (g, e, 0, 0)),
        out_shape=_sds((G, E, M, N), F32), input_output_aliases=aliases,
        compiler_params=pltpu.CompilerParams(dimension_semantics=("arbitrary", "arbitrary"),
                                             vmem_limit_bytes=VMEM_LIMIT))(*args)


MESH = pl.DeviceIdType.MESH
ANY = pl.BlockSpec(memory_space=pl.ANY)


def _pos():
    return lax.axis_index("x"), lax.axis_index("y"), lax.axis_index("c")


def _other_chips(x, y):
    return [(1 - x, y), (x, 1 - y), (1 - x, 1 - y)]


def _comm_call(body, *, name, args, out_shape, n_sem):
    return pl.pallas_call(
        body, name=name, in_specs=[ANY] * len(args), out_specs=[ANY] * len(out_shape), out_shape=out_shape,
        scratch_shapes=[pltpu.SemaphoreType.DMA((n_sem,)), pltpu.SemaphoreType.DMA((n_sem,)),
                        pltpu.SemaphoreType.DMA((len(args),))],
    )(*args)


def all_gather_chips(xs):
    n = len(xs)

    def body(*refs):
        x_refs, o_refs = refs[:n], refs[n:2 * n]
        ssem, rsem, lsem = refs[2 * n:]
        x, y, c = _pos()
        me = 2 * x + y
        chips = _other_chips(x, y)
        sib = (x, y, 1 - c)

        def rc(src, dst, idx, dev):
            return pltpu.make_async_remote_copy(src_ref=src, dst_ref=dst, send_sem=ssem.at[idx], recv_sem=rsem.at[idx],
                                                device_id=dev, device_id_type=MESH)

        via = jnp.where(c == 0, 2 * (1 - x) + y, 2 * x + (1 - y))
        to = (jnp.where(c == 0, x, 1 - x), jnp.where(c == 0, 1 - y, y), c)
        local, started = [], []
        for t in range(n):
            hr = xs[t].shape[0] // 2
            mine = pl.ds(c * hr, hr)
            cp = pltpu.make_async_copy(x_refs[t], o_refs[t].at[me], lsem.at[t])
            cp.start()
            local.append(cp)
            for k, (cx, cy) in enumerate(chips[:2]):
                cp = rc(x_refs[t].at[mine], o_refs[t].at[me, mine], 6 * t + k, (cx, cy, c))
                cp.start()
                started.append(cp)
        for t in range(n):
            hr = xs[t].shape[0] // 2
            mine = pl.ds(c * hr, hr)
            for k, (cx, cy) in enumerate(chips[:2]):
                landed = o_refs[t].at[2 * cx + cy, mine]
                rc(landed, landed, 6 * t + k, (cx, cy, c)).wait_recv()
            passed = o_refs[t].at[via, mine]
            cp = rc(passed, passed, 6 * t + 2, to)
            cp.start()
            started.append(cp)
            for k, (cx, cy) in enumerate(chips[:2]):
                landed = o_refs[t].at[2 * cx + cy, mine]
                cp = rc(landed, landed, 6 * t + 3 + k, sib)
                cp.start()
                started.append(cp)
        for t in range(n):
            hr = xs[t].shape[0] // 2
            mine = pl.ds(c * hr, hr)
            cx, cy = chips[2]
            landed = o_refs[t].at[2 * cx + cy, mine]
            rc(landed, landed, 6 * t + 2, to).wait_recv()
            cp = rc(landed, landed, 6 * t + 5, sib)
            cp.start()
            started.append(cp)
        for t in range(n):
            hr = xs[t].shape[0] // 2
            theirs = pl.ds((1 - c) * hr, hr)
            for k, (cx, cy) in enumerate(chips):
                other = o_refs[t].at[2 * cx + cy, theirs]
                rc(other, other, 6 * t + 3 + k, sib).wait_recv()
        for cp in started:
            cp.wait_send()
        for cp in local:
            cp.wait()

    outs = _comm_call(body, name="all_gather_chips", args=list(xs),
                      out_shape=[_sds((NSH,) + a.shape, a.dtype) for a in xs], n_sem=6 * n)
    return list(outs)


def sibling_half_exchange(gs):
    n = len(gs)

    def body(*refs):
        g_refs, o_refs = refs[:n], refs[n:2 * n]
        ssem, rsem, _ = refs[2 * n:]
        x, y, c = _pos()
        cps = []
        for t in range(n):
            hr = gs[t].shape[1] // 2
            for j in range(NSH):
                cp = pltpu.make_async_remote_copy(
                    src_ref=g_refs[t].at[j, pl.ds((1 - c) * hr, hr)], dst_ref=o_refs[t].at[j],
                    send_sem=ssem.at[NSH * t + j], recv_sem=rsem.at[NSH * t + j],
                    device_id=(x, y, 1 - c), device_id_type=MESH)
                cp.start()
                cps.append(cp)
        for cp in cps:
            cp.wait()

    outs = _comm_call(body, name="sibling_half_exchange", args=list(gs),
                      out_shape=[_sds((NSH, g.shape[1] // 2, g.shape[2]), g.dtype) for g in gs], n_sem=NSH * n)
    return list(outs)


def chip_scatter(ss):
    n = len(ss)

    def body(*refs):
        s_refs, o_refs = refs[:n], refs[n:2 * n]
        ssem, rsem, _ = refs[2 * n:]
        x, y, c = _pos()
        cps = []
        for t in range(n):
            for k, (cx, cy) in enumerate(_other_chips(x, y)):
                cp = pltpu.make_async_remote_copy(
                    src_ref=s_refs[t].at[2 * cx + cy], dst_ref=o_refs[t].at[k],
                    send_sem=ssem.at[3 * t + k], recv_sem=rsem.at[3 * t + k],
                    device_id=(cx, cy, c), device_id_type=MESH)
                cp.start()
                cps.append(cp)
        for cp in cps:
            cp.wait()

    outs = _comm_call(body, name="chip_scatter", args=list(ss),
                      out_shape=[_sds((3,) + s.shape[1:], s.dtype) for s in ss], n_sem=3 * n)
    return list(outs)


def sibling_exchange(rs):
    n = len(rs)

    def body(*refs):
        r_refs, o_refs = refs[:n], refs[n:2 * n]
        ssem, rsem, _ = refs[2 * n:]
        x, y, c = _pos()
        cps = []
        for t in range(n):
            cp = pltpu.make_async_remote_copy(
                src_ref=r_refs[t], dst_ref=o_refs[t], send_sem=ssem.at[t], recv_sem=rsem.at[t],
                device_id=(x, y, 1 - c), device_id_type=MESH)
            cp.start()
            cps.append(cp)
        for cp in cps:
            cp.wait()

    outs = _comm_call(body, name="sibling_exchange", args=list(rs),
                      out_shape=[_sds(r.shape, r.dtype) for r in rs], n_sem=n)
    return list(outs)


def all_gather_devices(v):
    def body(v_ref, o_ref, ssem, rsem, lsem):
        x, y, c = _pos()
        me = 4 * x + 2 * y + c
        loc = pltpu.make_async_copy(v_ref, o_ref.at[me], lsem.at[0])
        loc.start()
        cps = []
        k = 0
        for fx in range(2):
            for fy in range(2):
                for fc in range(2):
                    if fx == fy == fc == 0:
                        continue
                    cp = pltpu.make_async_remote_copy(
                        src_ref=v_ref, dst_ref=o_ref.at[me], send_sem=ssem.at[k], recv_sem=rsem.at[k],
                        device_id=(x ^ fx, y ^ fy, c ^ fc), device_id_type=MESH)
                    cp.start()
                    src = 4 * (x ^ fx) + 2 * (y ^ fy) + (c ^ fc)
                    cps.append((cp, o_ref.at[src], k))
                    k += 1
        for cp, landed, k in cps:
            cp.wait_send()
            pltpu.make_async_remote_copy(
                src_ref=landed, dst_ref=landed, send_sem=ssem.at[k], recv_sem=rsem.at[k],
                device_id=(x, y, c), device_id_type=MESH).wait_recv()
        loc.wait()

    return _comm_call(body, name="all_gather_devices", args=[v],
                      out_shape=[_sds((8,) + v.shape, v.dtype)], n_sem=7)[0]


def _call_sp(body, *, name, grid, in_specs, out_specs, out_shape, pos, args):
    return pl.pallas_call(
        body, name=name,
        grid_spec=pltpu.PrefetchScalarGridSpec(num_scalar_prefetch=1, grid=grid, in_specs=in_specs,
                                               out_specs=out_specs),
        out_shape=out_shape,
        compiler_params=pltpu.CompilerParams(dimension_semantics=("arbitrary",) * len(grid),
                                             vmem_limit_bytes=VMEM_LIMIT))(pos, *args)


def _rows_tile(r):
    for t in (512, 256, 128, 64, 32, 16, 8):
        if r % t == 0:
            return t
    raise ValueError(r)


def pair_sum(g, r, pos):
    _, R, C = g.shape
    hr = R // 2
    tr = _rows_tile(hr)
    nbh = hr // tr

    def body(p_ref, g_ref, r_ref, o_ref):
        o_ref[...] = (g_ref[...] + r_ref[...]).astype(MXU)

    return _call_sp(
        body, name="pair_sum", grid=(NSH, nbh), pos=pos, args=[g, r],
        in_specs=[pl.BlockSpec((None, tr, C), lambda j, i, p: (j, p[0] * nbh + i, 0)),
                  pl.BlockSpec((None, tr, C), lambda j, i, p: (j, i, 0))],
        out_specs=pl.BlockSpec((None, tr, C), lambda j, i, p: (j, i, 0)),
        out_shape=_sds((NSH, hr, C), MXU))


def reduce_own(g, r_sib, r_ici, pos):
    _, R, C = g.shape
    hr = R // 2
    tr = _rows_tile(hr)
    nbh = hr // tr

    def body(p_ref, g_ref, rs_ref, ri_ref, o_ref):
        s = g_ref[...] + rs_ref[...]
        for k in range(3):
            s = s + ri_ref[k].astype(F32)
        o_ref[...] = s

    return _call_sp(
        body, name="reduce_own", grid=(nbh,), pos=pos, args=[g, r_sib, r_ici],
        in_specs=[pl.BlockSpec((None, tr, C), lambda i, p: (p[1], p[0] * nbh + i, 0)),
                  pl.BlockSpec((None, tr, C), lambda i, p: (p[1], i, 0)),
                  pl.BlockSpec((3, tr, C), lambda i, p: (0, i, 0))],
        out_specs=pl.BlockSpec((tr, C), lambda i, p: (i, 0)),
        out_shape=_sds((hr, C), F32))


def _adamw_math(w, g, m, v):
    m = ADAM_B1 * m + (1.0 - ADAM_B1) * g
    v = ADAM_B2 * v + (1.0 - ADAM_B2) * (g * g)
    m_hat = m / (1.0 - ADAM_B1 ** ADAM_STEP)
    v_hat = v / (1.0 - ADAM_B2 ** ADAM_STEP)
    delta = -ADAM_LR * (m_hat / (jnp.sqrt(v_hat) + ADAM_EPS) + ADAM_WD * w)
    return delta, m, v


def adamw_halves(w, m, v, ga, gb, pos):
    R, C = w.shape
    hr = R // 2
    tr = _rows_tile(hr)
    nbh = hr // tr

    def body(p_ref, w_ref, m_ref, v_ref, ga_ref, gb_ref, g_ref, d_ref, mo_ref, vo_ref):
        mine = (pl.program_id(0) // nbh) == p_ref[0]
        g = jnp.where(mine, ga_ref[...], gb_ref[...])
        g_ref[...] = g
        d_ref[...], mo_ref[...], vo_ref[...] = _adamw_math(w_ref[...], g, m_ref[...], v_ref[...])

    blk = pl.BlockSpec((tr, C), lambda i, p: (i, 0))
    return _call_sp(
        body, name="adamw_halves", grid=(R // tr,), pos=pos, args=[w, m, v, ga, gb],
        in_specs=[blk, blk, blk,
                  pl.BlockSpec((tr, C), lambda i, p: (jnp.where(i // nbh == p[0], i % nbh, 0), 0)),
                  pl.BlockSpec((tr, C), lambda i, p: (jnp.where(i // nbh == p[0], 0, i % nbh), 0))],
        out_specs=[blk, blk, blk, blk],
        out_shape=[_sds((R, C), F32)] * 4)


def adamw_sum(gall, w, m, v):
    n, R, C = gall.shape

    def body(ga_ref, w_ref, m_ref, v_ref, g_ref, d_ref, mo_ref, vo_ref):
        g = ga_ref[0]
        for k in range(1, n):
            g = g + ga_ref[k]
        g_ref[...] = g
        d_ref[...], mo_ref[...], vo_ref[...] = _adamw_math(w_ref[...], g, m_ref[...], v_ref[...])

    blk = pl.BlockSpec((R, C), lambda i: (0, 0))
    return _call(body, name="adamw_sum", grid=(1,),
                 in_specs=[pl.BlockSpec((n, R, C), lambda i: (0, 0, 0)), blk, blk, blk],
                 out_specs=[blk, blk, blk, blk], out_shape=[_sds((R, C), F32)] * 4)(gall, w, m, v)


_WEIGHTS = ['ffn_norm', 'ffn_w_gate', 'ffn_w_up', 'ffn_w_down', 'mix_norm', 'final_norm', 'conv_w_in', 'conv_b_in',
            'conv_dw', 'conv_dw_b', 'conv_ln_g', 'conv_ln_b', 'conv_w_out', 'fox_w_in', 'fox_b_f', 'fox_w_out',
            'hgrn_w_in', 'hgrn_lb_logits', 'hgrn_norm', 'hgrn_w_out', 'pool_w', 'pool_scale']
_BIG = ['ffn_w_gate', 'ffn_w_up', 'ffn_w_down', 'conv_w_in', 'conv_w_out', 'fox_w_in', 'fox_w_out',
        'hgrn_w_in', 'hgrn_w_out', 'pool_w']
_SHARDED_SMALL = ['ffn_norm', 'conv_dw', 'hgrn_norm', 'pool_scale']
_REPLICATED = ['mix_norm', 'final_norm', 'conv_b_in', 'conv_dw_b', 'conv_ln_g', 'conv_ln_b', 'fox_b_f', 'hgrn_lb_logits']
FOX_N = 3 * D + FOX_H
FOX_NP = 3200
QS = D // NSH


def _pad_rows(a, rows):
    return jnp.pad(a, ((0, rows - a.shape[0]), (0, 0)))


def _pack_sharded_small(get):
    return jnp.concatenate([get('ffn_norm').reshape(8, -1), _pad_rows(get('conv_dw')[0], 32),
                            get('hgrn_norm'), get('pool_scale'), jnp.zeros((6, get('pool_scale').shape[1]), F32)], axis=0)


def _pack_replicated(get):
    return jnp.concatenate([get('mix_norm'), get('final_norm').reshape(1, D), get('conv_b_in').reshape(2, D),
                            get('conv_dw_b'), get('conv_ln_g'), get('conv_ln_b'),
                            jnp.pad(get('fox_b_f'), ((0, 0), (0, D - FOX_H))), get('hgrn_lb_logits'),
                            jnp.zeros((9, D), F32)], axis=0)


def _unpack_replicated(p):
    return {'mix_norm': p[0:4], 'final_norm': p[4], 'conv_b_in': p[5:7].reshape(1, 2 * D), 'conv_dw_b': p[7:8],
            'conv_ln_g': p[8:9], 'conv_ln_b': p[9:10], 'fox_b_f': p[10:11, :FOX_H], 'hgrn_lb_logits': p[11:15]}


def _unpack_sharded_small(p):
    return {'ffn_norm': p[0:8].reshape(DEPTH, 2, -1), 'conv_dw': p[8:8 + CONV_W][None],
            'hgrn_norm': p[40:41], 'pool_scale': p[41:42]}


def kernel(x, ffn_norm, ffn_w_gate, ffn_w_up, ffn_w_down, mix_norm, final_norm, conv_w_in, conv_b_in, conv_dw, conv_dw_b, conv_ln_g, conv_ln_b, conv_w_out, fox_w_in, fox_b_f, fox_w_out, hgrn_w_in, hgrn_lb_logits, hgrn_norm, hgrn_w_out, pool_w, pool_scale, loss_target, m_ffn_norm, m_ffn_w_gate, m_ffn_w_up, m_ffn_w_down, m_mix_norm, m_final_norm, m_conv_w_in, m_conv_b_in, m_conv_dw, m_conv_dw_b, m_conv_ln_g, m_conv_ln_b, m_conv_w_out, m_fox_w_in, m_fox_b_f, m_fox_w_out, m_hgrn_w_in, m_hgrn_lb_logits, m_hgrn_norm, m_hgrn_w_out, m_pool_w, m_pool_scale, v_ffn_norm, v_ffn_w_gate, v_ffn_w_up, v_ffn_w_down, v_mix_norm, v_final_norm, v_conv_w_in, v_conv_b_in, v_conv_dw, v_conv_dw_b, v_conv_ln_g, v_conv_ln_b, v_conv_w_out, v_fox_w_in, v_fox_b_f, v_fox_w_out, v_hgrn_w_in, v_hgrn_lb_logits, v_hgrn_norm, v_hgrn_w_out, v_pool_w, v_pool_scale):
    W = dict(ffn_norm=ffn_norm, ffn_w_gate=ffn_w_gate, ffn_w_up=ffn_w_up, ffn_w_down=ffn_w_down, mix_norm=mix_norm, final_norm=final_norm, conv_w_in=conv_w_in, conv_b_in=conv_b_in, conv_dw=conv_dw, conv_dw_b=conv_dw_b, conv_ln_g=conv_ln_g, conv_ln_b=conv_ln_b, conv_w_out=conv_w_out, fox_w_in=fox_w_in, fox_b_f=fox_b_f, fox_w_out=fox_w_out, hgrn_w_in=hgrn_w_in, hgrn_lb_logits=hgrn_lb_logits, hgrn_norm=hgrn_norm, hgrn_w_out=hgrn_w_out, pool_w=pool_w, pool_scale=pool_scale)
    M = dict(ffn_norm=m_ffn_norm, ffn_w_gate=m_ffn_w_gate, ffn_w_up=m_ffn_w_up, ffn_w_down=m_ffn_w_down, mix_norm=m_mix_norm, final_norm=m_final_norm, conv_w_in=m_conv_w_in, conv_b_in=m_conv_b_in, conv_dw=m_conv_dw, conv_dw_b=m_conv_dw_b, conv_ln_g=m_conv_ln_g, conv_ln_b=m_conv_ln_b, conv_w_out=m_conv_w_out, fox_w_in=m_fox_w_in, fox_b_f=m_fox_b_f, fox_w_out=m_fox_w_out, hgrn_w_in=m_hgrn_w_in, hgrn_lb_logits=m_hgrn_lb_logits, hgrn_norm=m_hgrn_norm, hgrn_w_out=m_hgrn_w_out, pool_w=m_pool_w, pool_scale=m_pool_scale)
    V = dict(ffn_norm=v_ffn_norm, ffn_w_gate=v_ffn_w_gate, ffn_w_up=v_ffn_w_up, ffn_w_down=v_ffn_w_down, mix_norm=v_mix_norm, final_norm=v_final_norm, conv_w_in=v_conv_w_in, conv_b_in=v_conv_b_in, conv_dw=v_conv_dw, conv_dw_b=v_conv_dw_b, conv_ln_g=v_conv_ln_g, conv_ln_b=v_conv_ln_b, conv_w_out=v_conv_w_out, fox_w_in=v_fox_w_in, fox_b_f=v_fox_b_f, fox_w_out=v_fox_w_out, hgrn_w_in=v_hgrn_w_in, hgrn_lb_logits=v_hgrn_lb_logits, hgrn_norm=v_hgrn_norm, hgrn_w_out=v_hgrn_w_out, pool_w=v_pool_w, pool_scale=v_pool_scale)

    px, py, pc = _pos()
    jme = 2 * px + py
    pos = jnp.stack([pc, jme]).astype(jnp.int32)
    S = x.shape[1]
    T = x.shape[0] * S
    x2 = x.reshape(T, D)
    tgt = loss_target.reshape(T, D)

    flat = lambda a: a.reshape(-1, a.shape[-1])
    gathered = all_gather_chips([flat(W[n]).astype(MXU) for n in _BIG] + [_pack_sharded_small(W.get)])
    G = dict(zip(_BIG, gathered[:-1]))
    small = gathered[-1].transpose(1, 0, 2).reshape(48, D)
    ffn_norm_f, conv_dw_f = small[0:8], small[8:40]
    hgrn_norm_f, pool_scale_f = small[40:41], small[41:42]
    wg_all = G['ffn_w_gate'].reshape(NSH, 2 * DEPTH, D, FS)
    wu_all = G['ffn_w_up'].reshape(NSH, 2 * DEPTH, D, FS)
    wd_all = G['ffn_w_down'].reshape(NSH, 2 * DEPTH, FS, D)
    conv_wi = G['conv_w_in']
    conv_wo = G['conv_w_out'].reshape(D, D)
    fox_full = jnp.pad(G['fox_w_in'].transpose(1, 0, 2).reshape(D, FOX_N), ((0, 0), (0, FOX_NP - FOX_N)))
    fox_w5 = fox_full.reshape(D, 5, FOX_NP // 5).transpose(1, 0, 2)
    fox_wf = fox_full[:, 3 * D:][None]
    fox_bf = jnp.pad(fox_b_f, ((0, 0), (0, 128 - FOX_H)))
    fox_wo = G['fox_w_out'].reshape(D, D)
    hgrn_wi = G['hgrn_w_in']
    hgrn_wo = G['hgrn_w_out'].reshape(D, D)
    pool_wf = G['pool_w'].reshape(NSH, 4, 64, POOL_G).transpose(1, 0, 2, 3).reshape(4, POOL_G, POOL_G)
    conv_bi = conv_b_in.reshape(NSH, 1, 2 * D // NSH)

    def ffn_f(xs, e):
        xo, h, u, sa, z = ffn_fwd(xs, ffn_norm_f[e:e + 1], wg_all, wu_all, wd_all, e, tm=min(1024, xs.shape[0]))
        return xo, (xs, h, u, sa, z)

    saved = []
    xs = x2
    lb = lb_fwd(hgrn_lb_logits)
    for i in range(DEPTH):
        xs, r0 = ffn_f(xs, 2 * i)
        gm = mix_norm[i:i + 1]
        xin = xs
        if i == 0:
            p, h = norm_mm(xin, gm, conv_wi, conv_bi, name="conv_in", out_dtype=F32)
            u2, u4 = conv_fwd_core(p, conv_dw_f, conv_dw_b, conv_ln_g, conv_ln_b, S=S)
            xs = mm_res(u4, conv_wo, xin, name="conv_out")
            rm = (xin, p, h, u2, u4)
        elif i == 1:
            p, h = norm_mm(xin, gm, fox_w5, None, name="fox_in", out_dtype=MXU)
            fl, _ = norm_mm(xin, gm, fox_wf, None, name="fox_in_f", out_dtype=F32)
            cq = fox_cum(fl, fox_bf, S=S)
            qa, ka, va = fox_prep(p, cq)
            o, o32, lse = fox2_fwd(qa, ka, va, S=S, tq=min(FOX_T, S), tk=min(FOX_T, S))
            xs = mm_res(o, fox_wo, xin, name="fox_out")
            rm = (xin, h, fl, qa, ka, va, o, o32, lse)
        elif i == 2:
            p, h = norm_mm(xin, gm, hgrn_wi, None, name="hgrn_in", out_dtype=F32)
            yh, oh, st = hgrn_fwd(p, lb, hgrn_norm_f, S=S)
            xs = mm_res(yh, hgrn_wo, xin, name="hgrn_out")
            rm = (xin, p, h, yh, oh, st)
        else:
            xs, mp = pool_fwd(xin, gm, pool_wf, pool_scale_f, S=S)
            rm = (xin, mp)
        xs, r1 = ffn_f(xs, 2 * i + 1)
        saved.append((r0, rm, r1))

    loss8, dx, d_final = loss_head(xs, final_norm.reshape(1, D), tgt)

    conv_wit, fox_w5t, hgrn_wit = (w.transpose(0, 2, 1) for w in (conv_wi, fox_w5, hgrn_wi))
    conv_wot, fox_wot, hgrn_wot = conv_wo.T, fox_wo.T, hgrn_wo.T
    gb = {'g': None, 'u': None, 'd': None}
    d_ffn_norm = [None] * (2 * DEPTH)
    d_mix_norm = [None] * DEPTH
    gbig = {}
    gsm = {}

    def ffn_b(dy, res, e):
        xin, h, u, sa, z = res
        dxo, da, db, dyh, dg = ffn_bwd_dx(xin, ffn_norm_f[e:e + 1], dy, u, sa, wg_all, wu_all, wd_all, e)
        tk = min(4096, xin.shape[0])
        gb['g'] = mm_tn(h, da, name="ffn_dwg", G=NSH, M=D, N=FS, tk=tk, stack=(2 * DEPTH, e, gb['g']))
        gb['u'] = mm_tn(h, db, name="ffn_dwu", G=NSH, M=D, N=FS, tk=tk, stack=(2 * DEPTH, e, gb['u']))
        gb['d'] = mm_tn(z, dyh, name="ffn_dwd", G=NSH, M=FS, N=D, tk=tk, stack=(2 * DEPTH, e, gb['d']))
        d_ffn_norm[e] = dg
        return dxo

    for i in reversed(range(DEPTH)):
        r0, rm, r1 = saved[i]
        dx = ffn_b(dx, r1, 2 * i + 1)
        gm = mix_norm[i:i + 1]
        if i == 0:
            xin, p, h, u2, u4 = rm
            du2, dyb, gsm['conv_ln_g'], gsm['conv_ln_b'], gsm['conv_dw_b'] = conv_bwd_rows(dx, conv_wot, u2, conv_ln_g, conv_ln_b)
            dp, gsm['conv_b_in'], ddw = conv_bwd_core(du2, p, conv_dw_f, S=S)
            gsm['conv_dw'] = ddw
            gbig['conv_w_in'] = mm_tn(h, dp, name="conv_dwin", G=NSH, M=D, N=2 * D // NSH, b_step=1)
            gbig['conv_w_out'] = mm_tn(u4, dyb, name="conv_dwout", G=NSH, M=QS, N=D, a_step=1)
            dx, d_mix_norm[i] = inproj_bwd(dp, conv_wit, xin, gm, dx, name="conv_in_bwd")
        elif i == 1:
            xin, h, fl, qa, ka, va, o, o32, lse = rm
            do, dyb = mm_nt(dx, fox_wot, name="fox_out_bwd")
            qb, da = fox2_prep_bwd(do, o32, lse, qa)
            dq = fox2_dq(qb, ka, va, da, S=S, tq=min(FOX_T, S))
            dk, dv, dck = fox2_dkv(qb, ka, va, da, S=S, tq=min(FOX_T, S))
            dfl, dbf = fox_fin(dck, fl, fox_bf, S=S)
            gsm['fox_b_f'] = dbf
            dp = jnp.concatenate([dq, dk, dv, dfl], axis=1)
            dw5 = mm_tn(h, dp, name="fox_dwin", G=5, M=D, N=FOX_NP // 5, b_step=1)
            dwf = dw5.transpose(1, 0, 2).reshape(D, FOX_NP)[:, :FOX_N]
            gbig['fox_w_in'] = dwf.reshape(D, NSH, FOX_N // NSH).transpose(1, 0, 2)
            gbig['fox_w_out'] = mm_tn(o, dyb, name="fox_dwout", G=NSH, M=QS, N=D, a_step=1)
            dx, d_mix_norm[i] = inproj_bwd(dp, fox_w5t, xin, gm, dx, name="fox_in_bwd")
        elif i == 2:
            xin, p, h, yh, oh, st = rm
            dyo, dyb = mm_nt(dx, hgrn_wot, name="hgrn_out_bwd")
            dp, dlb, gsm['hgrn_norm'] = hgrn_bwd(p, oh, dyo, st, lb, hgrn_norm_f, S=S)
            gsm['hgrn_lb_logits'] = lb_bwd(hgrn_lb_logits, dlb)
            gbig['hgrn_w_in'] = mm_tn(h, dp, name="hgrn_dwin", G=NSH, M=D, N=D, b_step=1)
            gbig['hgrn_w_out'] = mm_tn(yh, dyb, name="hgrn_dwout", G=NSH, M=QS, N=D, a_step=1)
            dx, d_mix_norm[i] = inproj_bwd(dp, hgrn_wit, xin, gm, dx, name="hgrn_in_bwd")
        else:
            xin, mp = rm
            dmc, dyp, gsm['pool_scale'] = pool_bwd_rows(dx, mp, pool_wf, pool_scale_f, S=S)
            dwp = mm_tn(mp, dyp, name="pool_dw", G=4, M=POOL_G, N=POOL_G, a_step=1, b_step=1)
            gbig['pool_w'] = dwp.reshape(4, NSH, 64, POOL_G).transpose(1, 0, 2, 3).reshape(NSH, 4 * 64, POOL_G)
            dx, d_mix_norm[i] = pool_bwd_core(dmc, xin, gm, dx, S=S)
        dx = ffn_b(dx, r0, 2 * i)

    gbig['ffn_w_gate'] = gb['g'].reshape(NSH, 2 * DEPTH * D, FS)
    gbig['ffn_w_up'] = gb['u'].reshape(NSH, 2 * DEPTH * D, FS)
    gbig['ffn_w_down'] = gb['d'].reshape(NSH, 2 * DEPTH * FS, D)

    gl = [gbig[n] for n in _BIG]
    r_sib = sibling_half_exchange(gl)
    s16 = [pair_sum(g, r, pos) for g, r in zip(gl, r_sib)]
    r_ici = chip_scatter(s16)
    red = [reduce_own(g, rs, ri, pos) for g, rs, ri in zip(gl, r_sib, r_ici)]
    oth = sibling_exchange(red)
    out = {}
    for n, ga, gb_ in zip(_BIG, red, oth):
        res = adamw_halves(flat(W[n]), flat(M[n]), flat(V[n]), ga, gb_, pos)
        out[n] = [r.reshape(W[n].shape) for r in res]

    gfull = {'mix_norm': jnp.concatenate(d_mix_norm, axis=0), 'final_norm': d_final,
             'conv_b_in': gsm['conv_b_in'], 'conv_dw_b': gsm['conv_dw_b'], 'conv_ln_g': gsm['conv_ln_g'],
             'conv_ln_b': gsm['conv_ln_b'], 'fox_b_f': gsm['fox_b_f'][:, :FOX_H], 'hgrn_lb_logits': gsm['hgrn_lb_logits'],
             'ffn_norm': jnp.concatenate(d_ffn_norm, axis=0), 'conv_dw': gsm['conv_dw'][None, :CONV_W],
             'hgrn_norm': gsm['hgrn_norm'], 'pool_scale': gsm['pool_scale']}
    gpack = jnp.concatenate([_pack_replicated(gfull.get), _pack_sharded_small(gfull.get)], axis=0)
    gall = all_gather_devices(gpack)
    rep = adamw_sum(gall[:, :24], _pack_replicated(W.get), _pack_replicated(M.get), _pack_replicated(V.get))
    rep = [_unpack_replicated(r) for r in rep]
    for n in _REPLICATED:
        out[n] = [r[n].reshape(W[n].shape) for r in rep]
    gsh = lax.dynamic_slice_in_dim(gall[:, 24:], jme * QS, QS, axis=2)
    shd = adamw_sum(gsh, _pack_sharded_small(W.get), _pack_sharded_small(M.get), _pack_sharded_small(V.get))
    shd = [_unpack_sharded_small(r) for r in shd]
    for n in _SHARDED_SMALL:
        out[n] = [r[n].reshape(W[n].shape) for r in shd]

    loss = lax.psum(loss8[0, 0], ("x", "y", "c"))
    res = [loss, dx.reshape(x.shape)]
    for k in range(4):
        res += [out[n][k] for n in _WEIGHTS]
    return tuple(res)
```

```python
import jax
import jax.numpy as jnp
from jax import lax
from jax.experimental import pallas as pl
from jax.experimental.pallas import tpu as pltpu

D = 1024
F = 2816
NSH = 4
FS = F // NSH
DEPTH = 4
RMS_EPS = 1e-6
LN_EPS = 1e-5
CONV_W = 31
HALO = 32
FOX_H = 16
FOX_DH = 64
HG_H = 8
HG_DK = 128
HG_C = 32
POOL_WIN = (2, 4, 8, 16)
POOL_G = 256
MXU = jnp.bfloat16
F32 = jnp.float32
VMEM_LIMIT = 52 * 1024 * 1024

ADAM_LR = 0.001
ADAM_B1 = 0.9
ADAM_B2 = 0.999
ADAM_EPS = 1e-08
ADAM_WD = 0.01
ADAM_STEP = 10


def _call(body, *, name, grid, in_specs, out_specs, out_shape, scratch=(), vmem=VMEM_LIMIT):
    return pl.pallas_call(
        body, name=name, grid=grid, in_specs=in_specs, out_specs=out_specs, out_shape=out_shape,
        scratch_shapes=list(scratch),
        compiler_params=pltpu.CompilerParams(dimension_semantics=("arbitrary",) * len(grid),
                                             vmem_limit_bytes=vmem))


def _dot(a, b):
    return jnp.dot(a, b, preferred_element_type=F32)


def _dot_nt(a, b):
    return lax.dot_general(a, b, (((1,), (1,)), ((), ())), preferred_element_type=F32)


def _dot_tn(a, b):
    return lax.dot_general(a, b, (((0,), (0,)), ((), ())), preferred_element_type=F32)


def _split(x):
    hi = x.astype(MXU)
    return hi, (x - hi.astype(F32)).astype(MXU)


def _sigmoid(x):
    return 1.0 / (1.0 + jnp.exp(-x))


def _rms(x, g):
    r = lax.rsqrt(jnp.mean(x * x, axis=-1, keepdims=True) + RMS_EPS)
    return x * r * g


def _rms_bwd(dh, x, g):
    r = lax.rsqrt(jnp.mean(x * x, axis=-1, keepdims=True) + RMS_EPS)
    xh = x * r
    dhg = dh * g
    dx = r * (dhg - xh * jnp.mean(dhg * xh, axis=-1, keepdims=True))
    return dx, jnp.sum(dh * xh, axis=0, keepdims=True)


def _sds(shape, dtype):
    return jax.ShapeDtypeStruct(shape, dtype)


def _wspec(w, e):
    if w.ndim == 3:
        return pl.BlockSpec((None,) + w.shape[1:], lambda i, j: (j, 0, 0))
    return pl.BlockSpec((None, None) + w.shape[2:], lambda i, j: (j, e, 0, 0))


def ffn_fwd(x, g, wg, wu, wd, e=0, *, tm=512):
    T = x.shape[0]

    def body(x_ref, g_ref, wg_ref, wu_ref, wd_ref, xo_ref, h_ref, u_ref, sa_ref, z_ref, acc_ref):
        j = pl.program_id(1)

        @pl.when(j == 0)
        def _():
            h_ref[...] = _rms(x_ref[...], g_ref[...]).astype(MXU)
            acc_ref[...] = jnp.zeros_like(acc_ref)

        h = h_ref[...]
        a = _dot(h, wg_ref[...])
        b = _dot(h, wu_ref[...])
        s = _sigmoid(a)
        sa = a * s
        u_ref[...] = (b * (s * (1.0 + a * (1.0 - s)))).astype(MXU)
        sa_ref[...] = sa.astype(MXU)
        z = (sa * b).astype(MXU)
        z_ref[...] = z
        acc_ref[...] += _dot(z, wd_ref[...])

        @pl.when(j == NSH - 1)
        def _():
            xo_ref[...] = x_ref[...] + 0.5 * acc_ref[...]

    return _call(
        body, name="ffn_fwd", grid=(T // tm, NSH),
        in_specs=[pl.BlockSpec((tm, D), lambda i, j: (i, 0)),
                  pl.BlockSpec((1, D), lambda i, j: (0, 0)),
                  _wspec(wg, e), _wspec(wu, e), _wspec(wd, e)],
        out_specs=[pl.BlockSpec((tm, D), lambda i, j: (i, 0)),
                   pl.BlockSpec((tm, D), lambda i, j: (i, 0)),
                   pl.BlockSpec((None, tm, FS), lambda i, j: (j, i, 0)),
                   pl.BlockSpec((None, tm, FS), lambda i, j: (j, i, 0)),
                   pl.BlockSpec((None, tm, FS), lambda i, j: (j, i, 0))],
        out_shape=[_sds((T, D), F32), _sds((T, D), MXU)] + [_sds((NSH, T, FS), MXU)] * 3,
        scratch=[pltpu.VMEM((tm, D), F32)],
    )(x, g, wg, wu, wd)


def ffn_bwd_dx(x, g, dy, u, sa, wg, wu, wd, e=0, *, tm=512):
    T = x.shape[0]

    def body(x_ref, g_ref, dy_ref, u_ref, sa_ref, wg_ref, wu_ref, wd_ref,
             dx_ref, da_ref, db_ref, dyh_ref, dg_ref):
        i = pl.program_id(0)
        j = pl.program_id(1)
        acc_ref = dx_ref

        @pl.when(j == 0)
        def _():
            dyh_ref[...] = (0.5 * dy_ref[...]).astype(MXU)
            acc_ref[...] = jnp.zeros_like(acc_ref)

        @pl.when((i == 0) & (j == 0))
        def _():
            dg_ref[...] = jnp.zeros_like(dg_ref)

        dz = _dot_nt(dyh_ref[...], wd_ref[...])
        da = (dz * u_ref[...].astype(F32)).astype(MXU)
        db = (dz * sa_ref[...].astype(F32)).astype(MXU)
        da_ref[...] = da
        db_ref[...] = db
        acc_ref[...] += _dot_nt(da, wg_ref[...]) + _dot_nt(db, wu_ref[...])

        @pl.when(j == NSH - 1)
        def _():
            dxn, dg = _rms_bwd(acc_ref[...], x_ref[...], g_ref[...])
            dx_ref[...] = dy_ref[...] + dxn
            dg_ref[...] += dg

    return _call(
        body, name="ffn_bwd_dx", grid=(T // tm, NSH),
        in_specs=[pl.BlockSpec((tm, D), lambda i, j: (i, 0)),
                  pl.BlockSpec((1, D), lambda i, j: (0, 0)),
                  pl.BlockSpec((tm, D), lambda i, j: (i, 0)),
                  pl.BlockSpec((None, tm, FS), lambda i, j: (j, i, 0)),
                  pl.BlockSpec((None, tm, FS), lambda i, j: (j, i, 0)),
                  _wspec(wg, e), _wspec(wu, e), _wspec(wd, e)],
        out_specs=[pl.BlockSpec((tm, D), lambda i, j: (i, 0)),
                   pl.BlockSpec((None, tm, FS), lambda i, j: (j, i, 0)),
                   pl.BlockSpec((None, tm, FS), lambda i, j: (j, i, 0)),
                   pl.BlockSpec((tm, D), lambda i, j: (i, 0)),
                   pl.BlockSpec((1, D), lambda i, j: (0, 0))],
        out_shape=[_sds((T, D), F32), _sds((NSH, T, FS), MXU), _sds((NSH, T, FS), MXU),
                   _sds((T, D), MXU), _sds((1, D), F32)],
    )(x, g, dy, u, sa, wg, wu, wd)


def mm_tn(a, b, *, name, G, M, N, a_step=0, b_step=0, tk=2048, stack=None):
    T = a.shape[-2]
    tk = min(tk, T)
    if stack is not None:
        return _mm_tn_stack(a, b, name=name, G=G, M=M, N=N, tk=tk, stack=stack)

    def spec(arr, width, step):
        if arr.ndim == 3:
            return pl.BlockSpec((None, tk, width), lambda g, k: (g, k, 0))
        return pl.BlockSpec((tk, width), lambda g, k: (k, g * step))

    def body(a_ref, b_ref, o_ref):
        @pl.when(pl.program_id(1) == 0)
        def _():
            o_ref[...] = jnp.zeros_like(o_ref)

        o_ref[...] += _dot_tn(a_ref[...], b_ref[...])

    return _call(
        body, name=name, grid=(G, T // tk),
        in_specs=[spec(a, M, a_step), spec(b, N, b_step)],
        out_specs=pl.BlockSpec((None, M, N), lambda g, k: (g, 0, 0)),
        out_shape=_sds((G, M, N), F32),
    )(a, b)


def norm_mm(x, g, wb, bias, *, name, out_dtype, tm=1024):
    T = x.shape[0]
    tm = min(tm, T)
    G, _, ns = wb.shape
    has_bias = bias is not None

    def body(*refs):
        if has_bias:
            x_ref, g_ref, w_ref, bias_ref, p_ref, h_ref = refs
        else:
            x_ref, g_ref, w_ref, p_ref, h_ref = refs

        @pl.when(pl.program_id(1) == 0)
        def _():
            h_ref[...] = _rms(x_ref[...], g_ref[...]).astype(MXU)

        p = _dot(h_ref[...], w_ref[...])
        if has_bias:
            p = p + bias_ref[...]
        p_ref[...] = p.astype(out_dtype)

    in_specs = [pl.BlockSpec((tm, D), lambda i, j: (i, 0)),
                pl.BlockSpec((1, D), lambda i, j: (0, 0)),
                pl.BlockSpec((None, D, ns), lambda i, j: (j, 0, 0))]
    args = [x, g, wb]
    if has_bias:
        in_specs.append(pl.BlockSpec((None, 1, ns), lambda i, j: (j, 0, 0)))
        args.append(bias)
    return _call(
        body, name=name, grid=(T // tm, G), in_specs=in_specs,
        out_specs=[pl.BlockSpec((tm, ns), lambda i, j: (i, j)),
                   pl.BlockSpec((tm, D), lambda i, j: (i, 0))],
        out_shape=[_sds((T, G * ns), out_dtype), _sds((T, D), MXU)],
    )(*args)


def mm_res(y, w, x, *, name, tm=1024):
    T, K = y.shape
    tm = min(tm, T)

    def body(y_ref, w_ref, x_ref, o_ref):
        o_ref[...] = x_ref[...] + _dot(y_ref[...], w_ref[...])

    return _call(
        body, name=name, grid=(T // tm,),
        in_specs=[pl.BlockSpec((tm, K), lambda i: (i, 0)),
                  pl.BlockSpec((K, D), lambda i: (0, 0)),
                  pl.BlockSpec((tm, D), lambda i: (i, 0))],
        out_specs=pl.BlockSpec((tm, D), lambda i: (i, 0)),
        out_shape=_sds((T, D), F32),
    )(y, w, x)


def mm_nt(a, wt, *, name, tm=1024):
    T, K = a.shape
    tm = min(tm, T)
    N = wt.shape[1]
    w = wt

    def body(a_ref, w_ref, o_ref, ab_ref):
        ab = a_ref[...].astype(MXU)
        ab_ref[...] = ab
        o_ref[...] = _dot(ab, w_ref[...])

    return _call(
        body, name=name, grid=(T // tm,),
        in_specs=[pl.BlockSpec((tm, K), lambda i: (i, 0)),
                  pl.BlockSpec((K, N), lambda i: (0, 0))],
        out_specs=[pl.BlockSpec((tm, N), lambda i: (i, 0)),
                   pl.BlockSpec((tm, K), lambda i: (i, 0))],
        out_shape=[_sds((T, N), F32), _sds((T, K), MXU)],
    )(a, w)


def inproj_bwd(dp, wb, x, g, dres, *, name, tm=1024):
    T = x.shape[0]
    tm = min(tm, T)
    G, ns, _ = wb.shape

    def body(dp_ref, w_ref, x_ref, g_ref, dres_ref, dx_ref, dg_ref, acc_ref):
        i = pl.program_id(0)
        j = pl.program_id(1)

        @pl.when(j == 0)
        def _():
            acc_ref[...] = jnp.zeros_like(acc_ref)

        @pl.when((i == 0) & (j == 0))
        def _():
            dg_ref[...] = jnp.zeros_like(dg_ref)

        acc_ref[...] += _dot(dp_ref[...], w_ref[...])

        @pl.when(j == G - 1)
        def _():
            dxn, dg = _rms_bwd(acc_ref[...], x_ref[...], g_ref[...])
            dx_ref[...] = dres_ref[...] + dxn
            dg_ref[...] += dg

    return _call(
        body, name=name, grid=(T // tm, G),
        in_specs=[pl.BlockSpec((tm, ns), lambda i, j: (i, j)),
                  pl.BlockSpec((None, ns, D), lambda i, j: (j, 0, 0)),
                  pl.BlockSpec((tm, D), lambda i, j: (i, 0)),
                  pl.BlockSpec((1, D), lambda i, j: (0, 0)),
                  pl.BlockSpec((tm, D), lambda i, j: (i, 0))],
        out_specs=[pl.BlockSpec((tm, D), lambda i, j: (i, 0)),
                   pl.BlockSpec((1, D), lambda i, j: (0, 0))],
        out_shape=[_sds((T, D), F32), _sds((1, D), F32)],
        scratch=[pltpu.VMEM((tm, D), F32)],
    )(dp, wb, x, g, dres)


def loss_head(x, gf, tgt, *, tm=512):
    T = x.shape[0]

    def body(x_ref, g_ref, t_ref, loss_ref, dx_ref, dg_ref):
        @pl.when(pl.program_id(0) == 0)
        def _():
            loss_ref[...] = jnp.zeros_like(loss_ref)
            dg_ref[...] = jnp.zeros_like(dg_ref)

        xv = x_ref[...]
        gv = g_ref[...]
        e = _rms(xv, gv) - t_ref[...]
        loss_ref[...] += (0.5 / D) * jnp.sum(e * e)
        dxn, dg = _rms_bwd(e * (1.0 / D), xv, gv)
        dx_ref[...] = dxn
        dg_ref[...] += dg

    return _call(
        body, name="loss_head", grid=(T // tm,),
        in_specs=[pl.BlockSpec((tm, D), lambda i: (i, 0)),
                  pl.BlockSpec((1, D), lambda i: (0, 0)),
                  pl.BlockSpec((tm, D), lambda i: (i, 0))],
        out_specs=[pl.BlockSpec((8, 128), lambda i: (0, 0)),
                   pl.BlockSpec((tm, D), lambda i: (i, 0)),
                   pl.BlockSpec((1, D), lambda i: (0, 0))],
        out_shape=[_sds((8, 128), F32), _sds((T, D), F32), _sds((1, D), F32)],
    )(x, gf, tgt)


def _glu(p):
    return p[:, :D] * _sigmoid(p[:, D:])


def _ln_stats(u):
    mu = jnp.mean(u, axis=-1, keepdims=True)
    xc = u - mu
    rstd = lax.rsqrt(jnp.mean(xc * xc, axis=-1, keepdims=True) + LN_EPS)
    return xc * rstd, rstd


def conv_fwd_core(p, dw, dwb, lng, lnb, *, S, tt=256):
    T = p.shape[0]
    nb = S // tt
    r = tt // HALO

    def body(pc_ref, pp_ref, dw_ref, dwb_ref, lng_ref, lnb_ref, u2_ref, u4_ref, ubuf):
        first = (pl.program_id(0) % nb) == 0
        ubuf[0:HALO, :] = jnp.where(first, 0.0, _glu(pp_ref[...]))
        ubuf[HALO:, :] = _glu(pc_ref[...])
        for c in range(D // 128):
            cs = slice(c * 128, (c + 1) * 128)
            acc = jnp.zeros((tt, 128), F32)
            for k in range(CONV_W):
                acc = acc + dw_ref[k:k + 1, cs] * ubuf[k + 2:k + 2 + tt, cs]
            u2_ref[:, cs] = acc + dwb_ref[:, cs]
        xh, _ = _ln_stats(u2_ref[...])
        u3 = xh * lng_ref[...] + lnb_ref[...]
        u4_ref[...] = (u3 * _sigmoid(u3)).astype(MXU)

    row = pl.BlockSpec((1, D), lambda i: (0, 0))
    return _call(
        body, name="conv_fwd_core", grid=(T // tt,),
        in_specs=[pl.BlockSpec((tt, 2 * D), lambda i: (i, 0)),
                  pl.BlockSpec((HALO, 2 * D), lambda i: (jnp.maximum(i * r - 1, 0), 0)),
                  pl.BlockSpec((HALO, D), lambda i: (0, 0)), row, row, row],
        out_specs=[pl.BlockSpec((tt, D), lambda i: (i, 0)), pl.BlockSpec((tt, D), lambda i: (i, 0))],
        out_shape=[_sds((T, D), F32), _sds((T, D), MXU)],
        scratch=[pltpu.VMEM((tt + HALO, D), F32)],
    )(p, p, dw, dwb, lng, lnb)


def conv_bwd_rows(dy, wout, u2, lng, lnb, *, tm=512):
    T = dy.shape[0]

    def body(dy_ref, w_ref, u2_ref, lng_ref, lnb_ref, du2_ref, dyb_ref, dlng_ref, dlnb_ref, ddwb_ref):
        @pl.when(pl.program_id(0) == 0)
        def _():
            dlng_ref[...] = jnp.zeros_like(dlng_ref)
            dlnb_ref[...] = jnp.zeros_like(dlnb_ref)
            ddwb_ref[...] = jnp.zeros_like(ddwb_ref)

        dyb = dy_ref[...].astype(MXU)
        dyb_ref[...] = dyb
        du4 = _dot(dyb, w_ref[...])
        xh, rstd = _ln_stats(u2_ref[...])
        lng_v = lng_ref[...]
        u3 = xh * lng_v + lnb_ref[...]
        s = _sigmoid(u3)
        du3 = du4 * (s * (1.0 + u3 * (1.0 - s)))
        dlng_ref[...] += jnp.sum(du3 * xh, axis=0, keepdims=True)
        dlnb_ref[...] += jnp.sum(du3, axis=0, keepdims=True)
        dxh = du3 * lng_v
        du2 = rstd * (dxh - jnp.mean(dxh, axis=-1, keepdims=True)
                      - xh * jnp.mean(dxh * xh, axis=-1, keepdims=True))
        du2_ref[...] = du2
        ddwb_ref[...] += jnp.sum(du2, axis=0, keepdims=True)

    row = pl.BlockSpec((1, D), lambda i: (0, 0))
    blk = pl.BlockSpec((tm, D), lambda i: (i, 0))
    return _call(
        body, name="conv_bwd_rows", grid=(T // tm,),
        in_specs=[blk, pl.BlockSpec((D, D), lambda i: (0, 0)), blk, row, row],
        out_specs=[blk, blk, row, row, row],
        out_shape=[_sds((T, D), F32), _sds((T, D), MXU), _sds((1, D), F32), _sds((1, D), F32), _sds((1, D), F32)],
    )(dy, wout, u2, lng, lnb)


def conv_bwd_core(du2, p, dw, *, S, tt=256):
    T = p.shape[0]
    nb = S // tt
    r = tt // HALO
    last_halo = T // HALO - 1

    def body(dc_ref, dn_ref, pc_ref, pp_ref, dw_ref, dp_ref, dbin_ref, ddw_ref, ubuf, dbuf):
        i = pl.program_id(0)

        @pl.when(i == 0)
        def _():
            dbin_ref[...] = jnp.zeros_like(dbin_ref)
            ddw_ref[...] = jnp.zeros_like(ddw_ref)

        first = (i % nb) == 0
        last = (i % nb) == nb - 1
        ubuf[0:HALO, :] = jnp.where(first, 0.0, _glu(pp_ref[...]))
        ubuf[HALO:, :] = _glu(pc_ref[...])
        dbuf[0:tt, :] = dc_ref[...]
        dbuf[tt:, :] = jnp.where(last, 0.0, dn_ref[...])
        pc = pc_ref[...]
        for c in range(D // 128):
            cs = slice(c * 128, (c + 1) * 128)
            dcur = dbuf[0:tt, cs]
            du = jnp.zeros((tt, 128), F32)
            for k in range(CONV_W):
                ddw_ref[k:k + 1, cs] += jnp.sum(dcur * ubuf[k + 2:k + 2 + tt, cs], axis=0, keepdims=True)
                du = du + dw_ref[k:k + 1, cs] * dbuf[CONV_W - 1 - k:CONV_W - 1 - k + tt, cs]
            a = pc[:, c * 128:(c + 1) * 128]
            sb = _sigmoid(pc[:, D + c * 128:D + (c + 1) * 128])
            da = du * sb
            db = du * a * sb * (1.0 - sb)
            dp_ref[:, cs] = da.astype(MXU)
            dp_ref[:, D + c * 128:D + (c + 1) * 128] = db.astype(MXU)
            dbin_ref[:, cs] += jnp.sum(da, axis=0, keepdims=True)
            dbin_ref[:, D + c * 128:D + (c + 1) * 128] += jnp.sum(db, axis=0, keepdims=True)

    return _call(
        body, name="conv_bwd_core", grid=(T // tt,),
        in_specs=[pl.BlockSpec((tt, D), lambda i: (i, 0)),
                  pl.BlockSpec((HALO, D), lambda i: (jnp.minimum((i + 1) * r, last_halo), 0)),
                  pl.BlockSpec((tt, 2 * D), lambda i: (i, 0)),
                  pl.BlockSpec((HALO, 2 * D), lambda i: (jnp.maximum(i * r - 1, 0), 0)),
                  pl.BlockSpec((HALO, D), lambda i: (0, 0))],
        out_specs=[pl.BlockSpec((tt, 2 * D), lambda i: (i, 0)),
                   pl.BlockSpec((1, 2 * D), lambda i: (0, 0)),
                   pl.BlockSpec((HALO, D), lambda i: (0, 0))],
        out_shape=[_sds((T, 2 * D), MXU), _sds((1, 2 * D), F32), _sds((HALO, D), F32)],
        scratch=[pltpu.VMEM((tt + HALO, D), F32), pltpu.VMEM((tt + HALO, D), F32)],
    )(du2, du2, p, p, dw)


PH = 16


def _pool_cnt(i, nb, tt, win):
    pos = (i % nb) * tt + lax.broadcasted_iota(jnp.int32, (tt, 1), 0)
    return jnp.minimum(pos + 1, win).astype(F32)


def pool_fwd(x, g, wp, scale, *, S, tt=256):
    T = x.shape[0]
    nb = S // tt
    r = tt // PH

    def body(xc_ref, xp_ref, g_ref, wp_ref, sc_ref, xo_ref, m_ref, hbuf):
        i = pl.program_id(0)
        first = (i % nb) == 0
        gv = g_ref[...]
        hbuf[0:PH, :] = jnp.where(first, 0.0, _rms(xp_ref[...], gv))
        xc = xc_ref[...]
        hbuf[PH:, :] = _rms(xc, gv)
        for gi, win in enumerate(POOL_WIN):
            gs = slice(gi * POOL_G, (gi + 1) * POOL_G)
            acc = hbuf[PH:PH + tt, gs]
            for j in range(1, win):
                acc = acc + hbuf[PH - j:PH - j + tt, gs]
            m = (acc / _pool_cnt(i, nb, tt, win) - hbuf[PH:PH + tt, gs]).astype(MXU)
            m_ref[:, gs] = m
            xo_ref[:, gs] = xc[:, gs] + _dot(m, wp_ref[gi]) * sc_ref[:, gs]

    row = pl.BlockSpec((1, D), lambda i: (0, 0))
    blk = pl.BlockSpec((tt, D), lambda i: (i, 0))
    return _call(
        body, name="pool_fwd", grid=(T // tt,),
        in_specs=[blk, pl.BlockSpec((PH, D), lambda i: (jnp.maximum(i * r - 1, 0), 0)), row,
                  pl.BlockSpec((len(POOL_WIN), POOL_G, POOL_G), lambda i: (0, 0, 0)), row],
        out_specs=[blk, blk],
        out_shape=[_sds((T, D), F32), _sds((T, D), MXU)],
        scratch=[pltpu.VMEM((tt + PH, D), F32)],
    )(x, x, g, wp, scale)


def pool_bwd_rows(dy, m, wp, scale, *, S, tt=256):
    T = dy.shape[0]
    nb = S // tt

    def body(dy_ref, m_ref, wp_ref, sc_ref, dmc_ref, dyp_ref, dsc_ref):
        i = pl.program_id(0)

        @pl.when(i == 0)
        def _():
            dsc_ref[...] = jnp.zeros_like(dsc_ref)

        for gi, win in enumerate(POOL_WIN):
            gs = slice(gi * POOL_G, (gi + 1) * POOL_G)
            dyg = dy_ref[:, gs]
            w = wp_ref[gi]
            dsc_ref[:, gs] += jnp.sum(dyg * _dot(m_ref[:, gs], w), axis=0, keepdims=True)
            dyp = (dyg * sc_ref[:, gs]).astype(MXU)
            dyp_ref[:, gs] = dyp
            dmc_ref[:, gs] = _dot_nt(dyp, w) / _pool_cnt(i, nb, tt, win)

    row = pl.BlockSpec((1, D), lambda i: (0, 0))
    blk = pl.BlockSpec((tt, D), lambda i: (i, 0))
    return _call(
        body, name="pool_bwd_rows", grid=(T // tt,),
        in_specs=[blk, blk, pl.BlockSpec((len(POOL_WIN), POOL_G, POOL_G), lambda i: (0, 0, 0)), row],
        out_specs=[blk, blk, row],
        out_shape=[_sds((T, D), F32), _sds((T, D), MXU), _sds((1, D), F32)],
    )(dy, m, wp, scale)


def pool_bwd_core(dmc, x, g, dres, *, S, tt=256):
    T = x.shape[0]
    nb = S // tt
    r = tt // PH
    last_halo = T // PH - 1

    def body(dc_ref, dn_ref, x_ref, g_ref, dres_ref, dx_ref, dg_ref, dbuf, dh_buf):
        i = pl.program_id(0)

        @pl.when(i == 0)
        def _():
            dg_ref[...] = jnp.zeros_like(dg_ref)

        last = (i % nb) == nb - 1
        dbuf[0:tt, :] = dc_ref[...]
        dbuf[tt:, :] = jnp.where(last, 0.0, dn_ref[...])
        for gi, win in enumerate(POOL_WIN):
            gs = slice(gi * POOL_G, (gi + 1) * POOL_G)
            cur = dbuf[0:tt, gs]
            acc = cur
            for j in range(1, win):
                acc = acc + dbuf[j:j + tt, gs]
            dh_buf[:, gs] = acc - cur * _pool_cnt(i, nb, tt, win)
        dxn, dg = _rms_bwd(dh_buf[...], x_ref[...], g_ref[...])
        dx_ref[...] = dres_ref[...] + dxn
        dg_ref[...] += dg

    row = pl.BlockSpec((1, D), lambda i: (0, 0))
    blk = pl.BlockSpec((tt, D), lambda i: (i, 0))
    return _call(
        body, name="pool_bwd_core", grid=(T // tt,),
        in_specs=[blk, pl.BlockSpec((PH, D), lambda i: (jnp.minimum((i + 1) * r, last_halo), 0)), blk, row, blk],
        out_specs=[blk, row],
        out_shape=[_sds((T, D), F32), _sds((1, D), F32)],
        scratch=[pltpu.VMEM((tt + PH, D), F32), pltpu.VMEM((tt, D), F32)],
    )(dmc, dmc, x, g, dres)


NEG = -1e30


def _tri(n, upper=False):
    r = lax.broadcasted_iota(jnp.int32, (n, n), 0)
    c = lax.broadcasted_iota(jnp.int32, (n, n), 1)
    return (r <= c if upper else r >= c).astype(F32)


def _dot_hi(a, b):
    return jnp.dot(a, b, preferred_element_type=F32, precision=lax.Precision.HIGHEST)


def _log_sigmoid(z):
    return jnp.minimum(z, 0.0) - jnp.log(1.0 + jnp.exp(-jnp.abs(z)))


def fox_cum(fl, bf, *, S, tt=256):
    T = fl.shape[0]
    nb = S // tt

    def body(fl_ref, bf_ref, c_ref, carry):
        i = pl.program_id(0)

        @pl.when((i % nb) == 0)
        def _():
            carry[...] = jnp.zeros_like(carry)

        lf = _log_sigmoid(fl_ref[...] + bf_ref[...])
        c = _dot_hi(_tri(tt), lf) + carry[...]
        c_ref[...] = c
        carry[...] = c[tt - 1:tt, :]

    return _call(
        body, name="fox_cum", grid=(T // tt,),
        in_specs=[pl.BlockSpec((tt, 128), lambda i: (i, 0)), pl.BlockSpec((1, 128), lambda i: (0, 0))],
        out_specs=pl.BlockSpec((tt, 128), lambda i: (i, 0)),
        out_shape=_sds((T, 128), F32),
        scratch=[pltpu.VMEM((1, 128), F32)],
    )(fl, bf)


HW = 128
FOX_T = 512
COL_ONE = FOX_DH + 3
COL_LSE = FOX_DH + 6


def _parts(x):
    hi = x.astype(MXU).astype(F32)
    mid = (x - hi).astype(MXU).astype(F32)
    lo = (x - hi - mid).astype(MXU).astype(F32)
    return [hi, mid, lo]


def _aug(n, cols):
    lane = lax.broadcasted_iota(jnp.int32, (n, HW - FOX_DH), 1)
    out = jnp.zeros((n, HW - FOX_DH), F32)
    for i, cval in enumerate(cols):
        out = jnp.where(lane == i, cval, out)
    return out


def fox_prep(p, c, *, tt=256):
    T = p.shape[0]

    def body(q_ref, k_ref, v_ref, c_ref, qa_ref, ka_ref, va_ref):
        ones = [1.0, 1.0, 1.0]
        for h in range(FOX_H):
            hs = slice(h * FOX_DH, (h + 1) * FOX_DH)
            lo, mid = h * HW, h * HW + FOX_DH
            cp = _parts(c_ref[:, h:h + 1])
            qa_ref[:, lo:mid] = (q_ref[:, hs].astype(F32) * (FOX_DH ** -0.5)).astype(MXU)
            qa_ref[:, mid:lo + HW] = _aug(tt, cp + ones).astype(MXU)
            ka_ref[:, lo:mid] = k_ref[:, hs]
            ka_ref[:, mid:lo + HW] = _aug(tt, ones + [-x for x in cp] + ones).astype(MXU)
            va_ref[:, lo:mid] = v_ref[:, hs]
            va_ref[:, mid:lo + HW] = _aug(tt, ones).astype(MXU)

    wide = pl.BlockSpec((tt, FOX_H * HW), lambda i: (i, 0))
    col = lambda k: pl.BlockSpec((tt, D), lambda i: (i, k))
    return _call(body, name="fox_prep", grid=(T // tt,),
                 in_specs=[col(0), col(1), col(2), pl.BlockSpec((tt, 128), lambda i: (i, 0))],
                 out_specs=[wide, wide, wide], out_shape=[_sds((T, FOX_H * HW), MXU)] * 3)(p, p, p, c)


def _causal(x, fill):
    r = lax.broadcasted_iota(jnp.int32, x.shape, 0)
    c = lax.broadcasted_iota(jnp.int32, x.shape, 1)
    return jnp.where(r >= c, x, fill)


def _wide_specs(tq, nq, q_of, k_of):
    qs = pl.BlockSpec((tq, FOX_H * HW), lambda b, i, j: (b * nq + q_of(i, j), 0))
    ks = pl.BlockSpec((tq, FOX_H * HW), lambda b, i, j: (b * nq + k_of(i, j), 0))
    return qs, ks


def fox2_fwd(qa, ka, va, *, S, tq=256, tk=512):
    T = qa.shape[0]
    tk = min(tk, S)
    nq = S // tq
    nk = S // tk
    r = tk // tq

    def body(q_ref, k_ref, v_ref, o_ref, o32_ref, lse_ref, m_sc, acc, acc_lo):
        qi = pl.program_id(1)
        ki = pl.program_id(2)
        last = qi // r

        @pl.when(ki == 0)
        def _():
            m_sc[...] = jnp.full_like(m_sc, NEG)
            acc[...] = jnp.zeros_like(acc)
            acc_lo[...] = jnp.zeros_like(acc_lo)

        def step(diag):
            for h in range(FOX_H):
                ws = slice(h * HW, (h + 1) * HW)
                s = _dot_nt(q_ref[:, ws], k_ref[:, ws])
                if diag:
                    row = qi * tq + lax.broadcasted_iota(jnp.int32, s.shape, 0)
                    col = ki * tk + lax.broadcasted_iota(jnp.int32, s.shape, 1)
                    s = jnp.where(row >= col, s, NEG)
                m_prev = m_sc[h]
                m_new = jnp.maximum(m_prev, jnp.max(s, axis=-1, keepdims=True))
                alpha = jnp.exp(m_prev - m_new)
                hi, lo = _split(jnp.exp(s - jnp.tile(m_new, (1, tk // 128))))
                acc[h] = alpha * acc[h] + _dot(hi, v_ref[:, ws])
                acc_lo[h] = alpha * acc_lo[h] + _dot(lo, v_ref[:, ws])
                m_sc[h] = m_new

        @pl.when(ki < last)
        def _():
            step(False)

        @pl.when(ki == last)
        def _():
            step(True)
            for h in range(FOX_H):
                hs = slice(h * FOX_DH, (h + 1) * FOX_DH)
                full = acc[h] + acc_lo[h]
                l = full[:, FOX_DH:FOX_DH + 1]
                o_ref[:, hs] = (acc[h][:, :FOX_DH] / l).astype(MXU)
                o32_ref[:, hs] = full[:, :FOX_DH] / l
                lse_ref[:, h:h + 1] = m_sc[h][:, 0:1] + jnp.log(l)

    qs = pl.BlockSpec((tq, FOX_H * HW), lambda b, i, j: (b * nq + i, 0))
    ks = pl.BlockSpec((tk, FOX_H * HW), lambda b, i, j: (b * nk + jnp.minimum(j, i // r), 0))
    orow = pl.BlockSpec((tq, D), lambda b, i, j: (b * nq + i, 0))
    return _call(
        body, name="fox_fwd", grid=(T // S, nq, nk),
        in_specs=[qs, ks, ks],
        out_specs=[orow, orow, pl.BlockSpec((tq, 128), lambda b, i, j: (b * nq + i, 0))],
        out_shape=[_sds((T, D), MXU), _sds((T, D), F32), _sds((T, 128), F32)],
        scratch=[pltpu.VMEM((FOX_H, tq, 128), F32), pltpu.VMEM((FOX_H, tq, HW), F32),
                 pltpu.VMEM((FOX_H, tq, HW), F32)],
    )(qa, ka, va)


def fox2_prep_bwd(do, o32, lse, qa, *, tt=256):
    T = do.shape[0]

    def body(do_ref, o_ref, lse_ref, qa_ref, qb_ref, da_ref):
        lane = lax.broadcasted_iota(jnp.int32, (tt, HW), 1)
        dob = do_ref[...].astype(MXU)
        for h in range(FOX_H):
            hs = slice(h * FOX_DH, (h + 1) * FOX_DH)
            ws = slice(h * HW, (h + 1) * HW)
            doh = dob[:, hs]
            delta = jnp.sum(doh.astype(F32) * o_ref[:, hs], axis=-1, keepdims=True)
            da_ref[:, h * HW:h * HW + FOX_DH] = doh
            da_ref[:, h * HW + FOX_DH:(h + 1) * HW] = _aug(tt, [-x for x in _parts(delta)]).astype(MXU)
            tile = qa_ref[:, ws]
            for i, part in enumerate(_parts(lse_ref[:, h:h + 1])):
                tile = jnp.where(lane == COL_LSE + i, (-part).astype(MXU), tile)
            qb_ref[:, ws] = tile

    wide = pl.BlockSpec((tt, FOX_H * HW), lambda i: (i, 0))
    blk = pl.BlockSpec((tt, D), lambda i: (i, 0))
    return _call(body, name="fox_prep_bwd", grid=(T // tt,),
                 in_specs=[blk, blk, pl.BlockSpec((tt, 128), lambda i: (i, 0)), wide],
                 out_specs=[wide, wide],
                 out_shape=[_sds((T, FOX_H * HW), MXU), _sds((T, FOX_H * HW), MXU)],
                 )(do, o32, lse, qa)


def fox2_dq(qb, ka, va, da, *, S, tq=256):
    T = qb.shape[0]
    nq = S // tq

    def body(q_ref, k_ref, v_ref, d_ref, dq_ref, acc):
        qi = pl.program_id(1)
        ki = pl.program_id(2)

        @pl.when(ki == 0)
        def _():
            acc[...] = jnp.zeros_like(acc)

        def step(diag):
            for h in range(FOX_H):
                ws = slice(h * HW, (h + 1) * HW)
                kh = k_ref[:, ws]
                p = jnp.exp(_dot_nt(q_ref[:, ws], kh))
                if diag:
                    p = _causal(p, 0.0)
                ds = (p * _dot_nt(d_ref[:, ws], v_ref[:, ws])).astype(MXU)
                acc[h] += _dot(ds, kh)

        @pl.when(ki < qi)
        def _():
            step(False)

        @pl.when(ki == qi)
        def _():
            step(True)
            for h in range(FOX_H):
                dq_ref[:, h * FOX_DH:(h + 1) * FOX_DH] = (acc[h][:, :FOX_DH] * (FOX_DH ** -0.5)).astype(MXU)

    qs, ks = _wide_specs(tq, nq, lambda i, j: i, lambda i, j: jnp.minimum(i, j))
    return _call(
        body, name="fox_bwd_dq", grid=(T // S, nq, nq),
        in_specs=[qs, ks, ks, qs],
        out_specs=pl.BlockSpec((tq, D), lambda b, i, j: (b * nq + i, 0)),
        out_shape=_sds((T, D), MXU),
        scratch=[pltpu.VMEM((FOX_H, tq, HW), F32)],
    )(qb, ka, va, da)


def fox2_dkv(qb, ka, va, da, *, S, tq=256):
    T = qb.shape[0]
    nq = S // tq

    def body(q_ref, k_ref, v_ref, d_ref, dk_ref, dv_ref, dck_ref, dk_acc, dv_acc):
        ki = pl.program_id(1)
        qi = pl.program_id(2)

        @pl.when(qi == 0)
        def _():
            dk_acc[...] = jnp.zeros_like(dk_acc)
            dv_acc[...] = jnp.zeros_like(dv_acc)

        def step(diag):
            for h in range(FOX_H):
                ws = slice(h * HW, (h + 1) * HW)
                qh = q_ref[:, ws]
                dh = d_ref[:, ws]
                pt = jnp.exp(_dot_nt(k_ref[:, ws], qh))
                if diag:
                    r = lax.broadcasted_iota(jnp.int32, pt.shape, 0)
                    c = lax.broadcasted_iota(jnp.int32, pt.shape, 1)
                    pt = jnp.where(r <= c, pt, 0.0)
                dv_acc[h] += _dot(pt.astype(MXU), dh)
                hi, lo = _split(pt * _dot_nt(v_ref[:, ws], dh))
                dk_acc[h] += _dot(hi, qh) + _dot(lo, qh)

        @pl.when(qi > ki)
        def _():
            step(False)

        @pl.when(qi == ki)
        def _():
            step(True)

        @pl.when(qi == nq - 1)
        def _():
            dck_ref[...] = jnp.zeros_like(dck_ref)
            for h in range(FOX_H):
                hs = slice(h * FOX_DH, (h + 1) * FOX_DH)
                dk_ref[:, hs] = dk_acc[h][:, :FOX_DH].astype(MXU)
                dv_ref[:, hs] = dv_acc[h][:, :FOX_DH].astype(MXU)
                dck_ref[:, h:h + 1] = dk_acc[h][:, COL_ONE:COL_ONE + 1]

    qs, ks = _wide_specs(tq, nq, lambda i, j: jnp.maximum(i, j), lambda i, j: i)
    ko = pl.BlockSpec((tq, D), lambda b, i, j: (b * nq + i, 0))
    return _call(
        body, name="fox_bwd_dkv", grid=(T // S, nq, nq),
        in_specs=[qs, ks, ks, qs],
        out_specs=[ko, ko, pl.BlockSpec((tq, 128), lambda b, i, j: (b * nq + i, 0))],
        out_shape=[_sds((T, D), MXU), _sds((T, D), MXU), _sds((T, 128), F32)],
        scratch=[pltpu.VMEM((FOX_H, tq, HW), F32), pltpu.VMEM((FOX_H, tq, HW), F32)],
    )(qb, ka, va, da)


def fox_fin(dck, fl, bf, *, S, tt=256):
    T = fl.shape[0]
    nb = S // tt
    nblk = T // tt

    def body(dck_ref, fl_ref, bf_ref, dfl_ref, dbf_ref, carry):
        i = pl.program_id(0)

        @pl.when(i == 0)
        def _():
            dbf_ref[...] = jnp.zeros_like(dbf_ref)

        @pl.when((i % nb) == 0)
        def _():
            carry[...] = jnp.zeros_like(carry)

        lane = lax.broadcasted_iota(jnp.int32, (tt, 128), 1)
        dc = jnp.where(lane < FOX_H, -dck_ref[...], 0.0)
        dlf = _dot_hi(_tri(tt, upper=True), dc) + carry[...]
        carry[...] = dlf[0:1, :]
        dfl = dlf * _sigmoid(-(fl_ref[...] + bf_ref[...]))
        dfl_ref[...] = dfl.astype(MXU)
        dbf_ref[...] += jnp.sum(dfl, axis=0, keepdims=True)

    rev = pl.BlockSpec((tt, 128), lambda i: (nblk - 1 - i, 0))
    row = pl.BlockSpec((1, 128), lambda i: (0, 0))
    return _call(
        body, name="fox_fin", grid=(nblk,),
        in_specs=[rev, rev, row],
        out_specs=[rev, row],
        out_shape=[_sds((T, 128), MXU), _sds((1, 128), F32)],
        scratch=[pltpu.VMEM((1, 128), F32)],
    )(dck, fl, bf)


def lb_fwd(logits):
    def body(l_ref, lb_ref):
        lv = l_ref[...]
        e = jnp.exp(lv - jnp.max(lv, axis=0, keepdims=True))
        p = e / jnp.sum(e, axis=0, keepdims=True)
        lb_ref[...] = p[1:2, :] + p[2:3, :]

    return _call(body, name="lb_fwd", grid=(1,),
                 in_specs=[pl.BlockSpec((DEPTH, D), lambda i: (0, 0))],
                 out_specs=pl.BlockSpec((1, D), lambda i: (0, 0)),
                 out_shape=_sds((1, D), F32))(logits)


def lb_bwd(logits, dlb):
    def body(l_ref, d_ref, o_ref):
        lv = l_ref[...]
        e = jnp.exp(lv - jnp.max(lv, axis=0, keepdims=True))
        p = e / jnp.sum(e, axis=0, keepdims=True)
        lb = p[1:2, :] + p[2:3, :]
        row = lax.broadcasted_iota(jnp.int32, (DEPTH, D), 0)
        sel = ((row == 1) | (row == 2)).astype(F32)
        o_ref[...] = p * (sel - lb) * d_ref[...]

    return _call(body, name="lb_bwd", grid=(1,),
                 in_specs=[pl.BlockSpec((DEPTH, D), lambda i: (0, 0)), pl.BlockSpec((1, D), lambda i: (0, 0))],
                 out_specs=pl.BlockSpec((DEPTH, D), lambda i: (0, 0)),
                 out_shape=_sds((DEPTH, D), F32))(logits, dlb)


def _hgrn_gates(qr, fr, lb):
    sg = _sigmoid(fr)
    sneg = _sigmoid(-fr)
    f = lb + (1.0 - lb) * sg
    kk = (1.0 - lb) * sneg
    G = _dot_hi(_tri(HG_C), jnp.log(f))
    eG = jnp.exp(G)
    einv = jnp.exp(-G)
    elast = jnp.exp(G[HG_C - 1:HG_C, :] - G)
    q = qr * _sigmoid(qr)
    return dict(q=q, kk=kk, f=f, sg=sg, sneg=sneg, eG=eG, einv=einv, elast=elast,
                qg=q * eG, kinv=kk * einv, khat=kk * elast, glast=jnp.exp(G[HG_C - 1:HG_C, :]))


def _tril_mask(x):
    r = lax.broadcasted_iota(jnp.int32, x.shape, 0)
    c = lax.broadcasted_iota(jnp.int32, x.shape, 1)
    return jnp.where(r >= c, x, 0.0)


def hgrn_fwd(p, lb, ng, *, S, R=256):
    T = p.shape[0]
    R = min(R, S)
    nr = S // R
    ncr = R // HG_C

    def body(q_ref, f_ref, v_ref, gt_ref, lb_ref, ng_ref, y_ref, o_ref, st_ref, st):
        @pl.when(pl.program_id(1) == 0)
        def _():
            st[...] = jnp.zeros_like(st)

        lbv = lb_ref[...]
        for ch in range(ncr):
            rows = slice(ch * HG_C, (ch + 1) * HG_C)
            gt = _hgrn_gates(q_ref[rows, :], f_ref[rows, :], lbv)
            for h in range(HG_H):
                hs = slice(h * HG_DK, (h + 1) * HG_DK)
                sp = st[h]
                st_ref[ch, h] = sp
                qg = gt["qg"][:, hs].astype(MXU)
                vh = v_ref[rows, hs].astype(MXU)
                A = _tril_mask(_dot_nt(qg, gt["kinv"][:, hs].astype(MXU)))
                o_ref[rows, hs] = _dot_nt(qg, sp.astype(MXU)) + _dot(A.astype(MXU), vh)
                st[h] = sp * gt["glast"][:, hs] + _dot_tn(vh, gt["khat"][:, hs].astype(MXU))
        gate = gt_ref[...]
        sgate = gate * _sigmoid(gate)
        for h in range(HG_H):
            hs = slice(h * HG_DK, (h + 1) * HG_DK)
            oh = o_ref[:, hs]
            r = lax.rsqrt(jnp.mean(oh * oh, axis=-1, keepdims=True) + RMS_EPS)
            y_ref[:, hs] = (oh * r * ng_ref[:, hs] * sgate[:, hs]).astype(MXU)

    col = lambda c: pl.BlockSpec((R, D), lambda b, i: (b * nr + i, c))
    row = pl.BlockSpec((1, D), lambda b, i: (0, 0))
    return _call(
        body, name="hgrn_fwd", grid=(T // S, nr),
        in_specs=[col(0), col(1), col(2), col(3), row, row],
        out_specs=[col(0), col(0),
                   pl.BlockSpec((ncr, HG_H, HG_DK, HG_DK), lambda b, i: (b * nr + i, 0, 0, 0))],
        out_shape=[_sds((T, D), MXU), _sds((T, D), F32), _sds((T // HG_C, HG_H, HG_DK, HG_DK), F32)],
        scratch=[pltpu.VMEM((HG_H, HG_DK, HG_DK), F32)],
    )(p, p, p, p, lb, ng)


def hgrn_bwd(p, o, dyo, states, lb, ng, *, S, R=256):
    T = p.shape[0]
    R = min(R, S)
    nr = S // R
    ncr = R // HG_C

    def body(q_ref, f_ref, v_ref, gt_ref, o_ref, dy_ref, st_ref, lb_ref, ng_ref,
             dp_ref, dlb_ref, dng_ref, dst, do_buf, dG_buf, dqb, dkb):
        b = pl.program_id(0)
        i = pl.program_id(1)

        @pl.when(i == 0)
        def _():
            dst[...] = jnp.zeros_like(dst)

        @pl.when((b == 0) & (i == 0))
        def _():
            dlb_ref[...] = jnp.zeros_like(dlb_ref)
            dng_ref[...] = jnp.zeros_like(dng_ref)

        lbv = lb_ref[...]
        gate = gt_ref[...]
        sg_gate = _sigmoid(gate)
        silu_gate = gate * sg_gate
        for h in range(HG_H):
            hs = slice(h * HG_DK, (h + 1) * HG_DK)
            oh = o_ref[:, hs]
            r = lax.rsqrt(jnp.mean(oh * oh, axis=-1, keepdims=True) + RMS_EPS)
            ohat = oh * r
            dyh = dy_ref[:, hs]
            ngh = ng_ref[:, hs]
            dng_ref[:, hs] += jnp.sum(dyh * silu_gate[:, hs] * ohat, axis=0, keepdims=True)
            dp_ref[:, 3 * D + h * HG_DK:3 * D + (h + 1) * HG_DK] = (
                dyh * ohat * ngh * (sg_gate[:, hs] * (1.0 + gate[:, hs] * (1.0 - sg_gate[:, hs])))).astype(MXU)
            dn = dyh * ngh * silu_gate[:, hs]
            do_buf[:, hs] = r * (dn - ohat * jnp.mean(dn * ohat, axis=-1, keepdims=True))

        lastrow = lax.broadcasted_iota(jnp.int32, (HG_C, HG_DK), 0) == HG_C - 1
        for ch in reversed(range(ncr)):
            rows = slice(ch * HG_C, (ch + 1) * HG_C)
            qr = q_ref[rows, :]
            gt = _hgrn_gates(qr, f_ref[rows, :], lbv)
            for h in range(HG_H):
                hs = slice(h * HG_DK, (h + 1) * HG_DK)
                sp = st_ref[ch, h]
                ds = dst[h]
                qg32, kinv32, khat32 = gt["qg"][:, hs], gt["kinv"][:, hs], gt["khat"][:, hs]
                qg, kinv, khat = qg32.astype(MXU), kinv32.astype(MXU), khat32.astype(MXU)
                vh = v_ref[rows, hs].astype(MXU)
                doh = do_buf[rows, hs].astype(MXU)
                dsb = ds.astype(MXU)
                A = _tril_mask(_dot_nt(qg, kinv)).astype(MXU)
                dA = _tril_mask(_dot_nt(doh, vh)).astype(MXU)
                dqg = _dot(doh, sp.astype(MXU)) + _dot(dA, kinv)
                dkinv = _dot_tn(dA, qg)
                dp_ref[rows, 2 * D + h * HG_DK:2 * D + (h + 1) * HG_DK] = (
                    _dot_tn(A, doh) + _dot_nt(khat, dsb)).astype(MXU)
                dkhat = _dot(vh, dsb)
                glast = gt["glast"][:, hs]
                qg32, kinv32, khat32 = qg.astype(F32), kinv.astype(F32), khat.astype(F32)
                extra = (glast * jnp.sum(dsb.astype(F32) * sp.astype(MXU).astype(F32), axis=0, keepdims=True)
                         + jnp.sum(dkhat * khat32, axis=0, keepdims=True))
                dst[h] = ds * glast + _dot_tn(doh, qg)
                dG = dqg * qg32 - dkinv * kinv32 - dkhat * khat32
                dG_buf[:, hs] = dG + jnp.where(lastrow, extra, 0.0)
                dqb[:, hs] = dqg * gt["eG"][:, hs]
                dkb[:, hs] = dkinv * gt["einv"][:, hs] + dkhat * gt["elast"][:, hs]
            dg = _dot_hi(_tri(HG_C, upper=True), dG_buf[...])
            dk = dkb[...]
            sneg, f = gt["sneg"], gt["f"]
            c1 = (1.0 - lbv) * gt["sg"] * sneg
            dp_ref[rows, D:2 * D] = (dg * c1 / f - dk * c1).astype(MXU)
            dlb_ref[...] += jnp.sum(dg * sneg / f - dk * sneg, axis=0, keepdims=True)
            sq = _sigmoid(qr)
            dp_ref[rows, 0:D] = (dqb[...] * (sq * (1.0 + qr * (1.0 - sq)))).astype(MXU)

    rev = lambda b, i: b * nr + nr - 1 - i
    col = lambda c: pl.BlockSpec((R, D), lambda b, i: (rev(b, i), c))
    row = pl.BlockSpec((1, D), lambda b, i: (0, 0))
    return _call(
        body, name="hgrn_bwd", grid=(T // S, nr),
        in_specs=[col(0), col(1), col(2), col(3), col(0), col(0),
                  pl.BlockSpec((ncr, HG_H, HG_DK, HG_DK), lambda b, i: (rev(b, i), 0, 0, 0)), row, row],
        out_specs=[pl.BlockSpec((R, 4 * D), lambda b, i: (rev(b, i), 0)), row, row],
        out_shape=[_sds((T, 4 * D), MXU), _sds((1, D), F32), _sds((1, D), F32)],
        scratch=[pltpu.VMEM((HG_H, HG_DK, HG_DK), F32), pltpu.VMEM((R, D), F32), pltpu.VMEM((HG_C, D), F32),
                 pltpu.VMEM((HG_C, D), F32), pltpu.VMEM((HG_C, D), F32)],
    )(p, p, p, p, o, dyo, states, lb, ng)


def mm_tn_pair(a, b1, b2, *, name, G, M, N, tk, E, e, buf1, buf2):
    T = a.shape[0]
    tk = min(tk, T)
    fresh = buf1 is None

    def body(*refs):
        a_ref, b1_ref, b2_ref = refs[:3]
        o1_ref, o2_ref = refs[-2:]

        @pl.when(pl.program_id(1) == 0)
        def _():
            o1_ref[...] = jnp.zeros_like(o1_ref)
            o2_ref[...] = jnp.zeros_like(o2_ref)

        at = a_ref[...].T
        o1_ref[...] += _dot(at, b1_ref[...])
        o2_ref[...] += _dot(at, b2_ref[...])

    bspec = pl.BlockSpec((None, tk, N), lambda g, k: (g, k, 0))
    in_specs = [pl.BlockSpec((tk, M), lambda g, k: (k, 0)), bspec, bspec]
    args = [a, b1, b2]
    aliases = {}
    if not fresh:
        in_specs += [pl.BlockSpec(memory_space=pl.ANY)] * 2
        args += [buf1, buf2]
        aliases = {3: 0, 4: 1}
    ospec = pl.BlockSpec((None, None, M, N), lambda g, k: (g, e, 0, 0))
    return pl.pallas_call(
        body, name=name, grid=(G, T // tk), in_specs=in_specs, out_specs=[ospec, ospec],
        out_shape=[_sds((G, E, M, N), F32)] * 2, input_output_aliases=aliases,
        compiler_params=pltpu.CompilerParams(dimension_semantics=("arbitrary", "arbitrary"),
                                             vmem_limit_bytes=VMEM_LIMIT))(*args)


def _mm_tn_stack(a, b, *, name, G, M, N, tk, stack):
    E, e, buf = stack
    T = a.shape[-2]

    def spec(arr, width):
        if arr.ndim == 3:
            return pl.BlockSpec((None, tk, width), lambda g, k: (g, k, 0))
        return pl.BlockSpec((tk, width), lambda g, k: (k, 0))

    def body(*refs):
        a_ref, b_ref, o_ref = refs[0], refs[1], refs[-1]

        @pl.when(pl.program_id(1) == 0)
        def _():
            o_ref[...] = jnp.zeros_like(o_ref)

        o_ref[...] += _dot_tn(a_ref[...], b_ref[...])

    in_specs = [spec(a, M), spec(b, N)]
    args = [a, b]
    aliases = {}
    if buf is not None:
        in_specs.append(pl.BlockSpec(memory_space=pl.ANY))
        args.append(buf)
        aliases = {2: 0}
    return pl.pallas_call(
        body, name=name, grid=(G, T // tk), in_specs=in_specs,
        out_specs=pl.BlockSpec((None, None, M, N), lambda g, k: (g, e, 0, 0)),
        out_shape=_sds((G, E, M, N), F32), input_output_aliases=aliases,
        compiler_params=pltpu.CompilerParams(dimension_semantics=("arbitrary", "arbitrary"),
                                             vmem_limit_bytes=VMEM_LIMIT))(*args)


MESH = pl.DeviceIdType.MESH
ANY = pl.BlockSpec(memory_space=pl.ANY)


def _pos():
    return lax.axis_index("x"), lax.axis_index("y"), lax.axis_index("c")


def _other_chips(x, y):
    return [(1 - x, y), (x, 1 - y), (1 - x, 1 - y)]


def _comm_call(body, *, name, args, out_shape, n_sem):
    return pl.pallas_call(
        body, name=name, in_specs=[ANY] * len(args), out_specs=[ANY] * len(out_shape), out_shape=out_shape,
        scratch_shapes=[pltpu.SemaphoreType.DMA((n_sem,)), pltpu.SemaphoreType.DMA((n_sem,)),
                        pltpu.SemaphoreType.DMA((len(args),))],
    )(*args)


def all_gather_chips(xs):
    n = len(xs)

    def body(*refs):
        x_refs, o_refs = refs[:n], refs[n:2 * n]
        ssem, rsem, lsem = refs[2 * n:]
        x, y, c = _pos()
        me = 2 * x + y
        chips = _other_chips(x, y)
        sib = (x, y, 1 - c)

        def rc(src, dst, idx, dev):
            return pltpu.make_async_remote_copy(src_ref=src, dst_ref=dst, send_sem=ssem.at[idx], recv_sem=rsem.at[idx],
                                                device_id=dev, device_id_type=MESH)

        via = jnp.where(c == 0, 2 * (1 - x) + y, 2 * x + (1 - y))
        to = (jnp.where(c == 0, x, 1 - x), jnp.where(c == 0, 1 - y, y), c)
        local, started = [], []
        for t in range(n):
            hr = xs[t].shape[0] // 2
            mine = pl.ds(c * hr, hr)
            cp = pltpu.make_async_copy(x_refs[t], o_refs[t].at[me], lsem.at[t])
            cp.start()
            local.append(cp)
            for k, (cx, cy) in enumerate(chips[:2]):
                cp = rc(x_refs[t].at[mine], o_refs[t].at[me, mine], 6 * t + k, (cx, cy, c))
                cp.start()
                started.append(cp)
        for t in range(n):
            hr = xs[t].shape[0] // 2
            mine = pl.ds(c * hr, hr)
            for k, (cx, cy) in enumerate(chips[:2]):
                landed = o_refs[t].at[2 * cx + cy, mine]
                rc(landed, landed, 6 * t + k, (cx, cy, c)).wait_recv()
            passed = o_refs[t].at[via, mine]
            cp = rc(passed, passed, 6 * t + 2, to)
            cp.start()
            started.append(cp)
            for k, (cx, cy) in enumerate(chips[:2]):
                landed = o_refs[t].at[2 * cx + cy, mine]
                cp = rc(landed, landed, 6 * t + 3 + k, sib)
                cp.start()
                started.append(cp)
        for t in range(n):
            hr = xs[t].shape[0] // 2
            mine = pl.ds(c * hr, hr)
            cx, cy = chips[2]
            landed = o_refs[t].at[2 * cx + cy, mine]
            rc(landed, landed, 6 * t + 2, to).wait_recv()
            cp = rc(landed, landed, 6 * t + 5, sib)
            cp.start()
            started.append(cp)
        for t in range(n):
            hr = xs[t].shape[0] // 2
            theirs = pl.ds((1 - c) * hr, hr)
            for k, (cx, cy) in enumerate(chips):
                other = o_refs[t].at[2 * cx + cy, theirs]
                rc(other, other, 6 * t + 3 + k, sib).wait_recv()
        for cp in started:
            cp.wait_send()
        for cp in local:
            cp.wait()

    outs = _comm_call(body, name="all_gather_chips", args=list(xs),
                      out_shape=[_sds((NSH,) + a.shape, a.dtype) for a in xs], n_sem=6 * n)
    return list(outs)


def sibling_half_exchange(gs):
    n = len(gs)

    def body(*refs):
        g_refs, o_refs = refs[:n], refs[n:2 * n]
        ssem, rsem, _ = refs[2 * n:]
        x, y, c = _pos()
        cps = []
        for t in range(n):
            hr = gs[t].shape[1] // 2
            for j in range(NSH):
                cp = pltpu.make_async_remote_copy(
                    src_ref=g_refs[t].at[j, pl.ds((1 - c) * hr, hr)], dst_ref=o_refs[t].at[j],
                    send_sem=ssem.at[NSH * t + j], recv_sem=rsem.at[NSH * t + j],
                    device_id=(x, y, 1 - c), device_id_type=MESH)
                cp.start()
                cps.append(cp)
        for cp in cps:
            cp.wait()

    outs = _comm_call(body, name="sibling_half_exchange", args=list(gs),
                      out_shape=[_sds((NSH, g.shape[1] // 2, g.shape[2]), g.dtype) for g in gs], n_sem=NSH * n)
    return list(outs)


def chip_scatter(ss):
    n = len(ss)

    def body(*refs):
        s_refs, o_refs = refs[:n], refs[n:2 * n]
        ssem, rsem, _ = refs[2 * n:]
        x, y, c = _pos()
        cps = []
        for t in range(n):
            for k, (cx, cy) in enumerate(_other_chips(x, y)):
                cp = pltpu.make_async_remote_copy(
                    src_ref=s_refs[t].at[2 * cx + cy], dst_ref=o_refs[t].at[k],
                    send_sem=ssem.at[3 * t + k], recv_sem=rsem.at[3 * t + k],
                    device_id=(cx, cy, c), device_id_type=MESH)
                cp.start()
                cps.append(cp)
        for cp in cps:
            cp.wait()

    outs = _comm_call(body, name="chip_scatter", args=list(ss),
                      out_shape=[_sds((3,) + s.shape[1:], s.dtype) for s in ss], n_sem=3 * n)
    return list(outs)


def sibling_exchange(rs):
    n = len(rs)

    def body(*refs):
        r_refs, o_refs = refs[:n], refs[n:2 * n]
        ssem, rsem, _ = refs[2 * n:]
        x, y, c = _pos()
        cps = []
        for t in range(n):
            cp = pltpu.make_async_remote_copy(
                src_ref=r_refs[t], dst_ref=o_refs[t], send_sem=ssem.at[t], recv_sem=rsem.at[t],
                device_id=(x, y, 1 - c), device_id_type=MESH)
            cp.start()
            cps.append(cp)
        for cp in cps:
            cp.wait()

    outs = _comm_call(body, name="sibling_exchange", args=list(rs),
                      out_shape=[_sds(r.shape, r.dtype) for r in rs], n_sem=n)
    return list(outs)


def all_gather_devices(v):
    def body(v_ref, o_ref, ssem, rsem, lsem):
        x, y, c = _pos()
        me = 4 * x + 2 * y + c
        loc = pltpu.make_async_copy(v_ref, o_ref.at[me], lsem.at[0])
        loc.start()
        cps = []
        k = 0
        for fx in range(2):
            for fy in range(2):
                for fc in range(2):
                    if fx == fy == fc == 0:
                        continue
                    cp = pltpu.make_async_remote_copy(
                        src_ref=v_ref, dst_ref=o_ref.at[me], send_sem=ssem.at[k], recv_sem=rsem.at[k],
                        device_id=(x ^ fx, y ^ fy, c ^ fc), device_id_type=MESH)
                    cp.start()
                    src = 4 * (x ^ fx) + 2 * (y ^ fy) + (c ^ fc)
                    cps.append((cp, o_ref.at[src], k))
                    k += 1
        for cp, landed, k in cps:
            cp.wait_send()
            pltpu.make_async_remote_copy(
                src_ref=landed, dst_ref=landed, send_sem=ssem.at[k], recv_sem=rsem.at[k],
                device_id=(x, y, c), device_id_type=MESH).wait_recv()
        loc.wait()

    return _comm_call(body, name="all_gather_devices", args=[v],
                      out_shape=[_sds((8,) + v.shape, v.dtype)], n_sem=7)[0]


def _call_sp(body, *, name, grid, in_specs, out_specs, out_shape, pos, args):
    return pl.pallas_call(
        body, name=name,
        grid_spec=pltpu.PrefetchScalarGridSpec(num_scalar_prefetch=1, grid=grid, in_specs=in_specs,
                                               out_specs=out_specs),
        out_shape=out_shape,
        compiler_params=pltpu.CompilerParams(dimension_semantics=("arbitrary",) * len(grid),
                                             vmem_limit_bytes=VMEM_LIMIT))(pos, *args)


def _rows_tile(r):
    for t in (512, 256, 128, 64, 32, 16, 8):
        if r % t == 0:
            return t
    raise ValueError(r)


def pair_sum(g, r, pos):
    _, R, C = g.shape
    hr = R // 2
    tr = _rows_tile(hr)
    nbh = hr // tr

    def body(p_ref, g_ref, r_ref, o_ref):
        o_ref[...] = (g_ref[...] + r_ref[...]).astype(MXU)

    return _call_sp(
        body, name="pair_sum", grid=(NSH, nbh), pos=pos, args=[g, r],
        in_specs=[pl.BlockSpec((None, tr, C), lambda j, i, p: (j, p[0] * nbh + i, 0)),
                  pl.BlockSpec((None, tr, C), lambda j, i, p: (j, i, 0))],
        out_specs=pl.BlockSpec((None, tr, C), lambda j, i, p: (j, i, 0)),
        out_shape=_sds((NSH, hr, C), MXU))


def reduce_own(g, r_sib, r_ici, pos):
    _, R, C = g.shape
    hr = R // 2
    tr = _rows_tile(hr)
    nbh = hr // tr

    def body(p_ref, g_ref, rs_ref, ri_ref, o_ref):
        s = g_ref[...] + rs_ref[...]
        for k in range(3):
            s = s + ri_ref[k].astype(F32)
        o_ref[...] = s

    return _call_sp(
        body, name="reduce_own", grid=(nbh,), pos=pos, args=[g, r_sib, r_ici],
        in_specs=[pl.BlockSpec((None, tr, C), lambda i, p: (p[1], p[0] * nbh + i, 0)),
                  pl.BlockSpec((None, tr, C), lambda i, p: (p[1], i, 0)),
                  pl.BlockSpec((3, tr, C), lambda i, p: (0, i, 0))],
        out_specs=pl.BlockSpec((tr, C), lambda i, p: (i, 0)),
        out_shape=_sds((hr, C), F32))


def _adamw_math(w, g, m, v):
    m = ADAM_B1 * m + (1.0 - ADAM_B1) * g
    v = ADAM_B2 * v + (1.0 - ADAM_B2) * (g * g)
    m_hat = m / (1.0 - ADAM_B1 ** ADAM_STEP)
    v_hat = v / (1.0 - ADAM_B2 ** ADAM_STEP)
    delta = -ADAM_LR * (m_hat / (jnp.sqrt(v_hat) + ADAM_EPS) + ADAM_WD * w)
    return delta, m, v


def adamw_halves(w, m, v, ga, gb, pos):
    R, C = w.shape
    hr = R // 2
    tr = _rows_tile(hr)
    nbh = hr // tr

    def body(p_ref, w_ref, m_ref, v_ref, ga_ref, gb_ref, g_ref, d_ref, mo_ref, vo_ref):
        mine = (pl.program_id(0) // nbh) == p_ref[0]
        g = jnp.where(mine, ga_ref[...], gb_ref[...])
        g_ref[...] = g
        d_ref[...], mo_ref[...], vo_ref[...] = _adamw_math(w_ref[...], g, m_ref[...], v_ref[...])

    blk = pl.BlockSpec((tr, C), lambda i, p: (i, 0))
    return _call_sp(
        body, name="adamw_halves", grid=(R // tr,), pos=pos, args=[w, m, v, ga, gb],
        in_specs=[blk, blk, blk,
                  pl.BlockSpec((tr, C), lambda i, p: (jnp.where(i // nbh == p[0], i % nbh, 0), 0)),
                  pl.BlockSpec((tr, C), lambda i, p: (jnp.where(i // nbh == p[0], 0, i % nbh), 0))],
        out_specs=[blk, blk, blk, blk],
        out_shape=[_sds((R, C), F32)] * 4)


def adamw_sum(gall, w, m, v):
    n, R, C = gall.shape

    def body(ga_ref, w_ref, m_ref, v_ref, g_ref, d_ref, mo_ref, vo_ref):
        g = ga_ref[0]
        for k in range(1, n):
            g = g + ga_ref[k]
        g_ref[...] = g
        d_ref[...], mo_ref[...], vo_ref[...] = _adamw_math(w_ref[...], g, m_ref[...], v_ref[...])

    blk = pl.BlockSpec((R, C), lambda i: (0, 0))
    return _call(body, name="adamw_sum", grid=(1,),
                 in_specs=[pl.BlockSpec((n, R, C), lambda i: (0, 0, 0)), blk, blk, blk],
                 out_specs=[blk, blk, blk, blk], out_shape=[_sds((R, C), F32)] * 4)(gall, w, m, v)


_WEIGHTS = ['ffn_norm', 'ffn_w_gate', 'ffn_w_up', 'ffn_w_down', 'mix_norm', 'final_norm', 'conv_w_in', 'conv_b_in',
            'conv_dw', 'conv_dw_b', 'conv_ln_g', 'conv_ln_b', 'conv_w_out', 'fox_w_in', 'fox_b_f', 'fox_w_out',
            'hgrn_w_in', 'hgrn_lb_logits', 'hgrn_norm', 'hgrn_w_out', 'pool_w', 'pool_scale']
_BIG = ['ffn_w_gate', 'ffn_w_up', 'ffn_w_down', 'conv_w_in', 'conv_w_out', 'fox_w_in', 'fox_w_out',
        'hgrn_w_in', 'hgrn_w_out', 'pool_w']
_SHARDED_SMALL = ['ffn_norm', 'conv_dw', 'hgrn_norm', 'pool_scale']
_REPLICATED = ['mix_norm', 'final_norm', 'conv_b_in', 'conv_dw_b', 'conv_ln_g', 'conv_ln_b', 'fox_b_f', 'hgrn_lb_logits']
FOX_N = 3 * D + FOX_H
FOX_NP = 3200
QS = D // NSH


def _pad_rows(a, rows):
    return jnp.pad(a, ((0, rows - a.shape[0]), (0, 0)))


def _pack_sharded_small(get):
    return jnp.concatenate([get('ffn_norm').reshape(8, -1), _pad_rows(get('conv_dw')[0], 32),
                            get('hgrn_norm'), get('pool_scale'), jnp.zeros((6, get('pool_scale').shape[1]), F32)], axis=0)


def _pack_replicated(get):
    return jnp.concatenate([get('mix_norm'), get('final_norm').reshape(1, D), get('conv_b_in').reshape(2, D),
                            get('conv_dw_b'), get('conv_ln_g'), get('conv_ln_b'),
                            jnp.pad(get('fox_b_f'), ((0, 0), (0, D - FOX_H))), get('hgrn_lb_logits'),
                            jnp.zeros((9, D), F32)], axis=0)


def _unpack_replicated(p):
    return {'mix_norm': p[0:4], 'final_norm': p[4], 'conv_b_in': p[5:7].reshape(1, 2 * D), 'conv_dw_b': p[7:8],
            'conv_ln_g': p[8:9], 'conv_ln_b': p[9:10], 'fox_b_f': p[10:11, :FOX_H], 'hgrn_lb_logits': p[11:15]}


def _unpack_sharded_small(p):
    return {'ffn_norm': p[0:8].reshape(DEPTH, 2, -1), 'conv_dw': p[8:8 + CONV_W][None],
            'hgrn_norm': p[40:41], 'pool_scale': p[41:42]}


def kernel(x, ffn_norm, ffn_w_gate, ffn_w_up, ffn_w_down, mix_norm, final_norm, conv_w_in, conv_b_in, conv_dw, conv_dw_b, conv_ln_g, conv_ln_b, conv_w_out, fox_w_in, fox_b_f, fox_w_out, hgrn_w_in, hgrn_lb_logits, hgrn_norm, hgrn_w_out, pool_w, pool_scale, loss_target, m_ffn_norm, m_ffn_w_gate, m_ffn_w_up, m_ffn_w_down, m_mix_norm, m_final_norm, m_conv_w_in, m_conv_b_in, m_conv_dw, m_conv_dw_b, m_conv_ln_g, m_conv_ln_b, m_conv_w_out, m_fox_w_in, m_fox_b_f, m_fox_w_out, m_hgrn_w_in, m_hgrn_lb_logits, m_hgrn_norm, m_hgrn_w_out, m_pool_w, m_pool_scale, v_ffn_norm, v_ffn_w_gate, v_ffn_w_up, v_ffn_w_down, v_mix_norm, v_final_norm, v_conv_w_in, v_conv_b_in, v_conv_dw, v_conv_dw_b, v_conv_ln_g, v_conv_ln_b, v_conv_w_out, v_fox_w_in, v_fox_b_f, v_fox_w_out, v_hgrn_w_in, v_hgrn_lb_logits, v_hgrn_norm, v_hgrn_w_out, v_pool_w, v_pool_scale):
    W = dict(ffn_norm=ffn_norm, ffn_w_gate=ffn_w_gate, ffn_w_up=ffn_w_up, ffn_w_down=ffn_w_down, mix_norm=mix_norm, final_norm=final_norm, conv_w_in=conv_w_in, conv_b_in=conv_b_in, conv_dw=conv_dw, conv_dw_b=conv_dw_b, conv_ln_g=conv_ln_g, conv_ln_b=conv_ln_b, conv_w_out=conv_w_out, fox_w_in=fox_w_in, fox_b_f=fox_b_f, fox_w_out=fox_w_out, hgrn_w_in=hgrn_w_in, hgrn_lb_logits=hgrn_lb_logits, hgrn_norm=hgrn_norm, hgrn_w_out=hgrn_w_out, pool_w=pool_w, pool_scale=pool_scale)
    M = dict(ffn_norm=m_ffn_norm, ffn_w_gate=m_ffn_w_gate, ffn_w_up=m_ffn_w_up, ffn_w_down=m_ffn_w_down, mix_norm=m_mix_norm, final_norm=m_final_norm, conv_w_in=m_conv_w_in, conv_b_in=m_conv_b_in, conv_dw=m_conv_dw, conv_dw_b=m_conv_dw_b, conv_ln_g=m_conv_ln_g, conv_ln_b=m_conv_ln_b, conv_w_out=m_conv_w_out, fox_w_in=m_fox_w_in, fox_b_f=m_fox_b_f, fox_w_out=m_fox_w_out, hgrn_w_in=m_hgrn_w_in, hgrn_lb_logits=m_hgrn_lb_logits, hgrn_norm=m_hgrn_norm, hgrn_w_out=m_hgrn_w_out, pool_w=m_pool_w, pool_scale=m_pool_scale)
    V = dict(ffn_norm=v_ffn_norm, ffn_w_gate=v_ffn_w_gate, ffn_w_up=v_ffn_w_up, ffn_w_down=v_ffn_w_down, mix_norm=v_mix_norm, final_norm=v_final_norm, conv_w_in=v_conv_w_in, conv_b_in=v_conv_b_in, conv_dw=v_conv_dw, conv_dw_b=v_conv_dw_b, conv_ln_g=v_conv_ln_g, conv_ln_b=v_conv_ln_b, conv_w_out=v_conv_w_out, fox_w_in=v_fox_w_in, fox_b_f=v_fox_b_f, fox_w_out=v_fox_w_out, hgrn_w_in=v_hgrn_w_in, hgrn_lb_logits=v_hgrn_lb_logits, hgrn_norm=v_hgrn_norm, hgrn_w_out=v_hgrn_w_out, pool_w=v_pool_w, pool_scale=v_pool_scale)

    px, py, pc = _pos()
    jme = 2 * px + py
    pos = jnp.stack([pc, jme]).astype(jnp.int32)
    S = x.shape[1]
    T = x.shape[0] * S
    x2 = x.reshape(T, D)
    tgt = loss_target.reshape(T, D)

    flat = lambda a: a.reshape(-1, a.shape[-1])
    gathered = all_gather_chips([flat(W[n]).astype(MXU) for n in _BIG] + [_pack_sharded_small(W.get)])
    G = dict(zip(_BIG, gathered[:-1]))
    small = gathered[-1].transpose(1, 0, 2).reshape(48, D)
    ffn_norm_f, conv_dw_f = small[0:8], small[8:40]
    hgrn_norm_f, pool_scale_f = small[40:41], small[41:42]
    wg_all = G['ffn_w_gate'].reshape(NSH, 2 * DEPTH, D, FS)
    wu_all = G['ffn_w_up'].reshape(NSH, 2 * DEPTH, D, FS)
    wd_all = G['ffn_w_down'].reshape(NSH, 2 * DEPTH, FS, D)
    conv_wi = G['conv_w_in']
    conv_wo = G['conv_w_out'].reshape(D, D)
    fox_full = jnp.pad(G['fox_w_in'].transpose(1, 0, 2).reshape(D, FOX_N), ((0, 0), (0, FOX_NP - FOX_N)))
    fox_w5 = fox_full.reshape(D, 5, FOX_NP // 5).transpose(1, 0, 2)
    fox_wf = fox_full[:, 3 * D:][None]
    fox_bf = jnp.pad(fox_b_f, ((0, 0), (0, 128 - FOX_H)))
    fox_wo = G['fox_w_out'].reshape(D, D)
    hgrn_wi = G['hgrn_w_in']
    hgrn_wo = G['hgrn_w_out'].reshape(D, D)
    pool_wf = G['pool_w'].reshape(NSH, 4, 64, POOL_G).transpose(1, 0, 2, 3).reshape(4, POOL_G, POOL_G)
    conv_bi = conv_b_in.reshape(NSH, 1, 2 * D // NSH)

    def ffn_f(xs, e):
        xo, h, u, sa, z = ffn_fwd(xs, ffn_norm_f[e:e + 1], wg_all, wu_all, wd_all, e, tm=min(1024, xs.shape[0]))
        return xo, (xs, h, u, sa, z)

    saved = []
    xs = x2
    lb = lb_fwd(hgrn_lb_logits)
    for i in range(DEPTH):
        xs, r0 = ffn_f(xs, 2 * i)
        gm = mix_norm[i:i + 1]
        xin = xs
        if i == 0:
            p, h = norm_mm(xin, gm, conv_wi, conv_bi, name="conv_in", out_dtype=F32)
            u2, u4 = conv_fwd_core(p, conv_dw_f, conv_dw_b, conv_ln_g, conv_ln_b, S=S)
            xs = mm_res(u4, conv_wo, xin, name="conv_out")
            rm = (xin, p, h, u2, u4)
        elif i == 1:
            p, h = norm_mm(xin, gm, fox_w5, None, name="fox_in", out_dtype=MXU)
            fl, _ = norm_mm(xin, gm, fox_wf, None, name="fox_in_f", out_dtype=F32)
            cq = fox_cum(fl, fox_bf, S=S)
            qa, ka, va = fox_prep(p, cq)
            o, o32, lse = fox2_fwd(qa, ka, va, S=S, tq=min(FOX_T, S), tk=min(FOX_T, S))
            xs = mm_res(o, fox_wo, xin, name="fox_out")
            rm = (xin, h, fl, qa, ka, va, o, o32, lse)
        elif i == 2:
            p, h = norm_mm(xin, gm, hgrn_wi, None, name="hgrn_in", out_dtype=F32)
            yh, oh, st = hgrn_fwd(p, lb, hgrn_norm_f, S=S)
            xs = mm_res(yh, hgrn_wo, xin, name="hgrn_out")
            rm = (xin, p, h, yh, oh, st)
        else:
            xs, mp = pool_fwd(xin, gm, pool_wf, pool_scale_f, S=S)
            rm = (xin, mp)
        xs, r1 = ffn_f(xs, 2 * i + 1)
        saved.append((r0, rm, r1))

    loss8, dx, d_final = loss_head(xs, final_norm.reshape(1, D), tgt)

    conv_wit, fox_w5t, hgrn_wit = (w.transpose(0, 2, 1) for w in (conv_wi, fox_w5, hgrn_wi))
    conv_wot, fox_wot, hgrn_wot = conv_wo.T, fox_wo.T, hgrn_wo.T
    gb = {'g': None, 'u': None, 'd': None}
    d_ffn_norm = [None] * (2 * DEPTH)
    d_mix_norm = [None] * DEPTH
    gbig = {}
    gsm = {}

    def ffn_b(dy, res, e):
        xin, h, u, sa, z = res
        dxo, da, db, dyh, dg = ffn_bwd_dx(xin, ffn_norm_f[e:e + 1], dy, u, sa, wg_all, wu_all, wd_all, e)
        tk = min(4096, xin.shape[0])
        gb['g'], gb['u'] = mm_tn_pair(h, da, db, name="ffn_dwgu", G=NSH, M=D, N=FS, tk=2048, E=2 * DEPTH, e=e,
                                      buf1=gb['g'], buf2=gb['u'])
        gb['d'] = mm_tn(z, dyh, name="ffn_dwd", G=NSH, M=FS, N=D, tk=tk, stack=(2 * DEPTH, e, gb['d']))
        d_ffn_norm[e] = dg
        return dxo

    for i in reversed(range(DEPTH)):
        r0, rm, r1 = saved[i]
        dx = ffn_b(dx, r1, 2 * i + 1)
        gm = mix_norm[i:i + 1]
        if i == 0:
            xin, p, h, u2, u4 = rm
            du2, dyb, gsm['conv_ln_g'], gsm['conv_ln_b'], gsm['conv_dw_b'] = conv_bwd_rows(dx, conv_wot, u2, conv_ln_g, conv_ln_b)
            dp, gsm['conv_b_in'], ddw = conv_bwd_core(du2, p, conv_dw_f, S=S)
            gsm['conv_dw'] = ddw
            gbig['conv_w_in'] = mm_tn(h, dp, name="conv_dwin", G=NSH, M=D, N=2 * D // NSH, b_step=1)
            gbig['conv_w_out'] = mm_tn(u4, dyb, name="conv_dwout", G=NSH, M=QS, N=D, a_step=1)
            dx, d_mix_norm[i] = inproj_bwd(dp, conv_wit, xin, gm, dx, name="conv_in_bwd")
        elif i == 1:
            xin, h, fl, qa, ka, va, o, o32, lse = rm
            do, dyb = mm_nt(dx, fox_wot, name="fox_out_bwd")
            qb, da = fox2_prep_bwd(do, o32, lse, qa)
            dq = fox2_dq(qb, ka, va, da, S=S, tq=min(FOX_T, S))
            dk, dv, dck = fox2_dkv(qb, ka, va, da, S=S, tq=min(FOX_T, S))
            dfl, dbf = fox_fin(dck, fl, fox_bf, S=S)
            gsm['fox_b_f'] = dbf
            dp = jnp.concatenate([dq, dk, dv, dfl], axis=1)
            dw5 = mm_tn(h, dp, name="fox_dwin", G=5, M=D, N=FOX_NP // 5, b_step=1)
            dwf = dw5.transpose(1, 0, 2).reshape(D, FOX_NP)[:, :FOX_N]
            gbig['fox_w_in'] = dwf.reshape(D, NSH, FOX_N // NSH).transpose(1, 0, 2)
            gbig['fox_w_out'] = mm_tn(o, dyb, name="fox_dwout", G=NSH, M=QS, N=D, a_step=1)
            dx, d_mix_norm[i] = inproj_bwd(dp, fox_w5t, xin, gm, dx, name="fox_in_bwd")
        elif i == 2:
            xin, p, h, yh, oh, st = rm
            dyo, dyb = mm_nt(dx, hgrn_wot, name="hgrn_out_bwd")
            dp, dlb, gsm['hgrn_norm'] = hgrn_bwd(p, oh, dyo, st, lb, hgrn_norm_f, S=S)
            gsm['hgrn_lb_logits'] = lb_bwd(hgrn_lb_logits, dlb)
            gbig['hgrn_w_in'] = mm_tn(h, dp, name="hgrn_dwin", G=NSH, M=D, N=D, b_step=1)
            gbig['hgrn_w_out'] = mm_tn(yh, dyb, name="hgrn_dwout", G=NSH, M=QS, N=D, a_step=1)
            dx, d_mix_norm[i] = inproj_bwd(dp, hgrn_wit, xin, gm, dx, name="hgrn_in_bwd")
        else:
            xin, mp = rm
            dmc, dyp, gsm['pool_scale'] = pool_bwd_rows(dx, mp, pool_wf, pool_scale_f, S=S)
            dwp = mm_tn(mp, dyp, name="pool_dw", G=4, M=POOL_G, N=POOL_G, a_step=1, b_step=1)
            gbig['pool_w'] = dwp.reshape(4, NSH, 64, POOL_G).transpose(1, 0, 2, 3).reshape(NSH, 4 * 64, POOL_G)
            dx, d_mix_norm[i] = pool_bwd_core(dmc, xin, gm, dx, S=S)
        dx = ffn_b(dx, r0, 2 * i)

    gbig['ffn_w_gate'] = gb['g'].reshape(NSH, 2 * DEPTH * D, FS)
    gbig['ffn_w_up'] = gb['u'].reshape(NSH, 2 * DEPTH * D, FS)
    gbig['ffn_w_down'] = gb['d'].reshape(NSH, 2 * DEPTH * FS, D)

    gl = [gbig[n] for n in _BIG]
    r_sib = sibling_half_exchange(gl)
    s16 = [pair_sum(g, r, pos) for g, r in zip(gl, r_sib)]
    r_ici = chip_scatter(s16)
    red = [reduce_own(g, rs, ri, pos) for g, rs, ri in zip(gl, r_sib, r_ici)]
    oth = sibling_exchange(red)
    out = {}
    for n, ga, gb_ in zip(_BIG, red, oth):
        res = adamw_halves(flat(W[n]), flat(M[n]), flat(V[n]), ga, gb_, pos)
        out[n] = [r.reshape(W[n].shape) for r in res]

    gfull = {'mix_norm': jnp.concatenate(d_mix_norm, axis=0), 'final_norm': d_final,
             'conv_b_in': gsm['conv_b_in'], 'conv_dw_b': gsm['conv_dw_b'], 'conv_ln_g': gsm['conv_ln_g'],
             'conv_ln_b': gsm['conv_ln_b'], 'fox_b_f': gsm['fox_b_f'][:, :FOX_H], 'hgrn_lb_logits': gsm['hgrn_lb_logits'],
             'ffn_norm': jnp.concatenate(d_ffn_norm, axis=0), 'conv_dw': gsm['conv_dw'][None, :CONV_W],
             'hgrn_norm': gsm['hgrn_norm'], 'pool_scale': gsm['pool_scale']}
    gpack = jnp.concatenate([_pack_replicated(gfull.get), _pack_sharded_small(gfull.get)], axis=0)
    gall = all_gather_devices(gpack)
    rep = adamw_sum(gall[:, :24], _pack_replicated(W.get), _pack_replicated(M.get), _pack_replicated(V.get))
    rep = [_unpack_replicated(r) for r in rep]
    for n in _REPLICATED:
        out[n] = [r[n].reshape(W[n].shape) for r in rep]
    gsh = lax.dynamic_slice_in_dim(gall[:, 24:], jme * QS, QS, axis=2)
    shd = adamw_sum(gsh, _pack_sharded_small(W.get), _pack_sharded_small(M.get), _pack_sharded_small(V.get))
    shd = [_unpack_sharded_small(r) for r in shd]
    for n in _SHARDED_SMALL:
        out[n] = [r[n].reshape(W[n].shape) for r in shd]

    loss = lax.psum(loss8[0, 0], ("x", "y", "c"))
    res = [loss, dx.reshape(x.shape)]
    for k in range(4):
        res += [out[n][k] for n in _WEIGHTS]
    return tuple(res)
```
